```python
import jax, jax.numpy as jnp
from jax import lax
import numpy as np

D_MODEL = 1024
BATCH = 8
SEQ = 4096
DEPTH = 4

N_MIXERS = 2
N_LAYERS_A = (DEPTH + 1) // 2
N_LAYERS_B = DEPTH // 2
CHUNK = 128
GMLP_WIDTH = 2 * D_MODEL
GMLP_GROUPS = 8
GMLP_GROUP_DIM = GMLP_WIDTH // GMLP_GROUPS
HEAD_DIM = 64
N_Q_HEADS = D_MODEL // HEAD_DIM
N_KV_HEADS = 4
GQA_GROUP = N_Q_HEADS // N_KV_HEADS
WINDOW = 128
ATTN_BLOCK = 128
ROPE_DIM = HEAD_DIM // 4
ROPE_THETA = 500000.0
Q_WIDTH = N_Q_HEADS * HEAD_DIM
KV_WIDTH = N_KV_HEADS * HEAD_DIM
QKV_WIDTH = Q_WIDTH + 2 * KV_WIDTH
FFN_HIDDEN = -(-(8 * D_MODEL) // (3 * 256)) * 256
RMS_EPS = 1e-6
LN_EPS = 1e-5
NEG_INF = -1e30

kernel_name = "hybrid_gmlp_swa_sink_sandwich"


def rmsnorm(x, g):
    xf = x.astype(jnp.float32)
    y = xf * lax.rsqrt(jnp.mean(xf * xf, axis=-1, keepdims=True) + RMS_EPS)
    return y.astype(x.dtype) * g


def layernorm(x, g, b):
    xf = x.astype(jnp.float32)
    mu = jnp.mean(xf, axis=-1, keepdims=True)
    var = jnp.mean(jnp.square(xf - mu), axis=-1, keepdims=True)
    return ((xf - mu) * lax.rsqrt(var + LN_EPS)).astype(x.dtype) * g + b


def rope_tables(positions):
    inv_freq = ROPE_THETA ** (-jnp.arange(0, ROPE_DIM, 2, dtype=jnp.float32) / ROPE_DIM)
    ang = positions.astype(jnp.float32)[..., None] * inv_freq
    return jnp.cos(ang)[:, :, None, :], jnp.sin(ang)[:, :, None, :]


def apply_partial_rope(x, cos, sin):
    half = ROPE_DIM // 2
    cos = cos.astype(x.dtype)
    sin = sin.astype(x.dtype)
    x1, x2, rest = x[..., :half], x[..., half:ROPE_DIM], x[..., ROPE_DIM:]
    return jnp.concatenate([x1 * cos - x2 * sin, x2 * cos + x1 * sin, rest], axis=-1)


def gmlp_mixer(h, w_in, b_in, ln_g, ln_b, w_s, b_s, w_out):
    B, S, _ = h.shape
    nc = S // CHUNK
    z = jax.nn.gelu(h @ w_in + b_in, approximate=False)
    u, v = jnp.split(z, 2, axis=-1)
    v = layernorm(v, ln_g, ln_b)
    v = v.reshape(B, nc, CHUNK, GMLP_GROUPS, GMLP_GROUP_DIM)
    causal = jnp.tril(jnp.ones((CHUNK, CHUNK), dtype=bool))
    w = jnp.where(causal[None], w_s, 0.0)
    sv = jnp.einsum("gts,bnsgc->bntgc", w, v) + b_s.T[None, None, :, :, None]
    gated = u * sv.reshape(B, S, GMLP_WIDTH)
    return gated @ w_out


def swa_sink_mixer(h, cos, sin, w_qkv, b_qkv, sinks, w_o):
    B, S, _ = h.shape
    nb = S // ATTN_BLOCK
    qkv = h @ w_qkv + b_qkv
    q = qkv[..., :Q_WIDTH].reshape(B, S, N_Q_HEADS, HEAD_DIM)
    k = qkv[..., Q_WIDTH:Q_WIDTH + KV_WIDTH].reshape(B, S, N_KV_HEADS, HEAD_DIM)
    v = qkv[..., Q_WIDTH + KV_WIDTH:].reshape(B, S, N_KV_HEADS, HEAD_DIM)
    q = apply_partial_rope(q, cos, sin) * (HEAD_DIM ** -0.5)
    k = apply_partial_rope(k, cos, sin)
    qb = q.reshape(B, nb, ATTN_BLOCK, N_KV_HEADS, GQA_GROUP, HEAD_DIM)
    kb = k.reshape(B, nb, ATTN_BLOCK, N_KV_HEADS, HEAD_DIM)
    vb = v.reshape(B, nb, ATTN_BLOCK, N_KV_HEADS, HEAD_DIM)
    kk = jnp.concatenate([jnp.concatenate([jnp.zeros_like(kb[:, :1]), kb[:, :-1]], axis=1), kb], axis=2)
    vv = jnp.concatenate([jnp.concatenate([jnp.zeros_like(vb[:, :1]), vb[:, :-1]], axis=1), vb], axis=2)
    s = jnp.einsum("bnqkgd,bnskd->bnkgqs", qb, kk).astype(jnp.float32)
    qi = jnp.arange(ATTN_BLOCK)[:, None]
    sj = jnp.arange(2 * ATTN_BLOCK)[None, :]
    diff = ATTN_BLOCK + qi - sj
    band = (diff >= 0) & (diff < WINDOW)
    exists = (jnp.arange(nb)[:, None, None] > 0) | (sj >= ATTN_BLOCK)[None]
    valid = band[None] & exists
    s = jnp.where(valid[None, :, None, None], s, NEG_INF)
    sink = sinks.astype(jnp.float32).reshape(N_KV_HEADS, GQA_GROUP)[None, None, :, :, None, None]
    m = jnp.maximum(jnp.max(s, axis=-1, keepdims=True), sink)
    p = jnp.exp(s - m)
    denom = jnp.sum(p, axis=-1, keepdims=True) + jnp.exp(sink - m)
    p = (p / denom).astype(vv.dtype)
    o = jnp.einsum("bnkgqs,bnskd->bnqkgd", p, vv).reshape(B, S, Q_WIDTH)
    return o @ w_o


def swiglu_ffn(h, w_gu, w_down):
    g, up = jnp.split(h @ w_gu, 2, axis=-1)
    return (jax.nn.silu(g) * up) @ w_down


def _fwd_setup_inputs(seed: int = 0) -> dict:
    key = jax.random.key(seed)
    ks = jax.random.split(key, 20)
    f32 = jnp.float32
    nrm = lambda k, shape, scale: jax.random.normal(k, shape, f32) * scale
    x = jax.random.normal(ks[0], (BATCH, SEQ, D_MODEL), f32)
    positions = jnp.broadcast_to(jnp.arange(SEQ, dtype=jnp.int32)[None, :], (BATCH, SEQ))
    gain = lambda k: 1.0 + nrm(k, (DEPTH, D_MODEL), 0.1)
    return {
        "x": x,
        "positions": positions,
        "pre_mix_g": gain(ks[1]),
        "post_mix_g": gain(ks[2]),
        "pre_ffn_g": gain(ks[3]),
        "post_ffn_g": gain(ks[4]),
        "a_w_in": nrm(ks[5], (N_LAYERS_A, D_MODEL, 2 * GMLP_WIDTH), D_MODEL ** -0.5),
        "a_b_in": nrm(ks[6], (N_LAYERS_A, 2 * GMLP_WIDTH), 0.01),
        "a_ln_g": 1.0 + nrm(ks[7], (N_LAYERS_A, GMLP_WIDTH), 0.1),
        "a_ln_b": nrm(ks[8], (N_LAYERS_A, GMLP_WIDTH), 0.01),
        "a_w_s": nrm(ks[9], (N_LAYERS_A, GMLP_GROUPS, CHUNK, CHUNK), 0.5 * CHUNK ** -0.5),
        "a_b_s": 1.0 + nrm(ks[10], (N_LAYERS_A, GMLP_GROUPS, CHUNK), 0.1),
        "a_w_out": nrm(ks[11], (N_LAYERS_A, GMLP_WIDTH, D_MODEL), GMLP_WIDTH ** -0.5),
        "b_w_qkv": nrm(ks[12], (N_LAYERS_B, D_MODEL, QKV_WIDTH), D_MODEL ** -0.5),
        "b_b_qkv": nrm(ks[13], (N_LAYERS_B, QKV_WIDTH), 0.01),
        "b_sinks": nrm(ks[14], (N_LAYERS_B, N_Q_HEADS), 1.0),
        "b_w_o": nrm(ks[15], (N_LAYERS_B, Q_WIDTH, D_MODEL), Q_WIDTH ** -0.5),
        "ffn_w_gu": nrm(ks[16], (DEPTH, D_MODEL, 2 * FFN_HIDDEN), D_MODEL ** -0.5),
        "ffn_w_down": nrm(ks[17], (DEPTH, FFN_HIDDEN, D_MODEL), FFN_HIDDEN ** -0.5),
    }


def _fwd_reference(x, positions, pre_mix_g, post_mix_g, pre_ffn_g, post_ffn_g,
              a_w_in, a_b_in, a_ln_g, a_ln_b, a_w_s, a_b_s, a_w_out,
              b_w_qkv, b_b_qkv, b_sinks, b_w_o, ffn_w_gu, ffn_w_down):
    cos, sin = rope_tables(positions)
    h = x
    for i in range(DEPTH):
        j = i // N_MIXERS
        hn = rmsnorm(h, pre_mix_g[i])
        if i % N_MIXERS == 0:
            mix = gmlp_mixer(hn, a_w_in[j], a_b_in[j], a_ln_g[j], a_ln_b[j],
                             a_w_s[j], a_b_s[j], a_w_out[j])
        else:
            mix = swa_sink_mixer(hn, cos, sin, b_w_qkv[j], b_b_qkv[j], b_sinks[j], b_w_o[j])
        h = h + rmsnorm(mix, post_mix_g[i])
        f = swiglu_ffn(rmsnorm(h, pre_ffn_g[i]), ffn_w_gu[i], ffn_w_down[i])
        h = h + rmsnorm(f, post_ffn_g[i])
    return h


import jax as _jax
import jax.numpy as _jnp

TWIN_FORMAT = 'train_step'
FWD_PARAMS = ['x', 'positions', 'pre_mix_g', 'post_mix_g', 'pre_ffn_g', 'post_ffn_g', 'a_w_in', 'a_b_in', 'a_ln_g', 'a_ln_b', 'a_w_s', 'a_b_s', 'a_w_out', 'b_w_qkv', 'b_b_qkv', 'b_sinks', 'b_w_o', 'ffn_w_gu', 'ffn_w_down']
TWIN_WEIGHTS = ['pre_mix_g', 'post_mix_g', 'pre_ffn_g', 'post_ffn_g', 'a_w_in', 'a_b_in', 'a_ln_g', 'a_ln_b', 'a_w_s', 'a_b_s', 'a_w_out', 'b_w_qkv', 'b_b_qkv', 'b_sinks', 'b_w_o', 'ffn_w_gu', 'ffn_w_down']
TWIN_DIFF_INPUT = 'x'
TWIN_INPUTS = ['x', 'positions', 'pre_mix_g', 'post_mix_g', 'pre_ffn_g', 'post_ffn_g', 'a_w_in', 'a_b_in', 'a_ln_g', 'a_ln_b', 'a_w_s', 'a_b_s', 'a_w_out', 'b_w_qkv', 'b_b_qkv', 'b_sinks', 'b_w_o', 'ffn_w_gu', 'ffn_w_down', 'loss_target', 'm_pre_mix_g', 'm_post_mix_g', 'm_pre_ffn_g', 'm_post_ffn_g', 'm_a_w_in', 'm_a_b_in', 'm_a_ln_g', 'm_a_ln_b', 'm_a_w_s', 'm_a_b_s', 'm_a_w_out', 'm_b_w_qkv', 'm_b_b_qkv', 'm_b_sinks', 'm_b_w_o', 'm_ffn_w_gu', 'm_ffn_w_down', 'v_pre_mix_g', 'v_post_mix_g', 'v_pre_ffn_g', 'v_post_ffn_g', 'v_a_w_in', 'v_a_b_in', 'v_a_ln_g', 'v_a_ln_b', 'v_a_w_s', 'v_a_b_s', 'v_a_w_out', 'v_b_w_qkv', 'v_b_b_qkv', 'v_b_sinks', 'v_b_w_o', 'v_ffn_w_gu', 'v_ffn_w_down']
TWIN_OUTPUTS = ['loss', 'grad_x', 'grad_pre_mix_g', 'grad_post_mix_g', 'grad_pre_ffn_g', 'grad_post_ffn_g', 'grad_a_w_in', 'grad_a_b_in', 'grad_a_ln_g', 'grad_a_ln_b', 'grad_a_w_s', 'grad_a_b_s', 'grad_a_w_out', 'grad_b_w_qkv', 'grad_b_b_qkv', 'grad_b_sinks', 'grad_b_w_o', 'grad_ffn_w_gu', 'grad_ffn_w_down', 'delta_pre_mix_g', 'delta_post_mix_g', 'delta_pre_ffn_g', 'delta_post_ffn_g', 'delta_a_w_in', 'delta_a_b_in', 'delta_a_ln_g', 'delta_a_ln_b', 'delta_a_w_s', 'delta_a_b_s', 'delta_a_w_out', 'delta_b_w_qkv', 'delta_b_b_qkv', 'delta_b_sinks', 'delta_b_w_o', 'delta_ffn_w_gu', 'delta_ffn_w_down', 'new_m_pre_mix_g', 'new_m_post_mix_g', 'new_m_pre_ffn_g', 'new_m_post_ffn_g', 'new_m_a_w_in', 'new_m_a_b_in', 'new_m_a_ln_g', 'new_m_a_ln_b', 'new_m_a_w_s', 'new_m_a_b_s', 'new_m_a_w_out', 'new_m_b_w_qkv', 'new_m_b_b_qkv', 'new_m_b_sinks', 'new_m_b_w_o', 'new_m_ffn_w_gu', 'new_m_ffn_w_down', 'new_v_pre_mix_g', 'new_v_post_mix_g', 'new_v_pre_ffn_g', 'new_v_post_ffn_g', 'new_v_a_w_in', 'new_v_a_b_in', 'new_v_a_ln_g', 'new_v_a_ln_b', 'new_v_a_w_s', 'new_v_a_b_s', 'new_v_a_w_out', 'new_v_b_w_qkv', 'new_v_b_b_qkv', 'new_v_b_sinks', 'new_v_b_w_o', 'new_v_ffn_w_gu', 'new_v_ffn_w_down']
TWIN_LEAF_KINDS = {'loss': 'loss', 'grad_x': 'grad_x', 'grad_pre_mix_g': 'grad_w', 'grad_post_mix_g': 'grad_w', 'grad_pre_ffn_g': 'grad_w', 'grad_post_ffn_g': 'grad_w', 'grad_a_w_in': 'grad_w', 'grad_a_b_in': 'grad_w', 'grad_a_ln_g': 'grad_w', 'grad_a_ln_b': 'grad_w', 'grad_a_w_s': 'grad_w', 'grad_a_b_s': 'grad_w', 'grad_a_w_out': 'grad_w', 'grad_b_w_qkv': 'grad_w', 'grad_b_b_qkv': 'grad_w', 'grad_b_sinks': 'grad_w', 'grad_b_w_o': 'grad_w', 'grad_ffn_w_gu': 'grad_w', 'grad_ffn_w_down': 'grad_w', 'delta_pre_mix_g': 'delta_w', 'delta_post_mix_g': 'delta_w', 'delta_pre_ffn_g': 'delta_w', 'delta_post_ffn_g': 'delta_w', 'delta_a_w_in': 'delta_w', 'delta_a_b_in': 'delta_w', 'delta_a_ln_g': 'delta_w', 'delta_a_ln_b': 'delta_w', 'delta_a_w_s': 'delta_w', 'delta_a_b_s': 'delta_w', 'delta_a_w_out': 'delta_w', 'delta_b_w_qkv': 'delta_w', 'delta_b_b_qkv': 'delta_w', 'delta_b_sinks': 'delta_w', 'delta_b_w_o': 'delta_w', 'delta_ffn_w_gu': 'delta_w', 'delta_ffn_w_down': 'delta_w', 'new_m_pre_mix_g': 'new_m', 'new_m_post_mix_g': 'new_m', 'new_m_pre_ffn_g': 'new_m', 'new_m_post_ffn_g': 'new_m', 'new_m_a_w_in': 'new_m', 'new_m_a_b_in': 'new_m', 'new_m_a_ln_g': 'new_m', 'new_m_a_ln_b': 'new_m', 'new_m_a_w_s': 'new_m', 'new_m_a_b_s': 'new_m', 'new_m_a_w_out': 'new_m', 'new_m_b_w_qkv': 'new_m', 'new_m_b_b_qkv': 'new_m', 'new_m_b_sinks': 'new_m', 'new_m_b_w_o': 'new_m', 'new_m_ffn_w_gu': 'new_m', 'new_m_ffn_w_down': 'new_m', 'new_v_pre_mix_g': 'new_v', 'new_v_post_mix_g': 'new_v', 'new_v_pre_ffn_g': 'new_v', 'new_v_post_ffn_g': 'new_v', 'new_v_a_w_in': 'new_v', 'new_v_a_b_in': 'new_v', 'new_v_a_ln_g': 'new_v', 'new_v_a_ln_b': 'new_v', 'new_v_a_w_s': 'new_v', 'new_v_a_b_s': 'new_v', 'new_v_a_w_out': 'new_v', 'new_v_b_w_qkv': 'new_v', 'new_v_b_b_qkv': 'new_v', 'new_v_b_sinks': 'new_v', 'new_v_b_w_o': 'new_v', 'new_v_ffn_w_gu': 'new_v', 'new_v_ffn_w_down': 'new_v'}


def _forward(args):
    return _fwd_reference(*[args[k] for k in FWD_PARAMS])


def _output_shape():
    out = _jax.eval_shape(lambda: _forward(_fwd_setup_inputs(0)))
    return out.shape, out.dtype

N_MICROBATCH = 1
ADAM_LR = 0.001
ADAM_B1 = 0.9
ADAM_B2 = 0.999
ADAM_EPS = 1e-08
ADAM_WD = 0.01
ADAM_STEP = 10
PER_EXAMPLE_BATCH_AXIS = {'x': 0, 'positions': 0, 'loss_target': 0}
SHARED_INPUTS = []
_WEIGHT_DTYPES = {'pre_mix_g': _jnp.float32, 'post_mix_g': _jnp.float32, 'pre_ffn_g': _jnp.float32, 'post_ffn_g': _jnp.float32, 'a_w_in': _jnp.float32, 'a_b_in': _jnp.float32, 'a_ln_g': _jnp.float32, 'a_ln_b': _jnp.float32, 'a_w_s': _jnp.float32, 'a_b_s': _jnp.float32, 'a_w_out': _jnp.float32, 'b_w_qkv': _jnp.float32, 'b_b_qkv': _jnp.float32, 'b_sinks': _jnp.float32, 'b_w_o': _jnp.float32, 'ffn_w_gu': _jnp.float32, 'ffn_w_down': _jnp.float32}
MOMENT_SCALE = {'pre_mix_g': 3.239153e+01, 'post_mix_g': 6.147424e+01, 'pre_ffn_g': 2.094263e+01, 'post_ffn_g': 3.629049e+01, 'a_w_in': 7.532515e+00, 'a_b_in': 3.961732e+01, 'a_ln_g': 3.983726e-01, 'a_ln_b': 1.180465e+00, 'a_w_s': 1.138735e+00, 'a_b_s': 2.631888e+00, 'a_w_out': 4.640568e+01, 'b_w_qkv': 3.583123e+01, 'b_b_qkv': 1.187841e+02, 'b_sinks': 1.613573e+00, 'b_w_o': 4.419769e+01, 'ffn_w_gu': 9.163334e+00, 'ffn_w_down': 1.666140e+01}


def _to_microbatches(a, axis):
    t = _jnp.moveaxis(a, axis, 0)
    t = t.reshape((N_MICROBATCH, t.shape[0] // N_MICROBATCH) + t.shape[1:])
    return _jnp.moveaxis(t, 1, axis + 1)


def setup_inputs(seed: int = 0) -> dict:
    inp = _fwd_setup_inputs(seed)
    key = _jax.random.fold_in(_jax.random.key(seed), 7919)
    shape, _ = _output_shape()
    out = dict(inp)
    out["loss_target"] = _jax.random.normal(_jax.random.fold_in(key, 0), shape, _jnp.float32)
    for i, name in enumerate(TWIN_WEIGHTS):
        w = inp[name].astype(_jnp.float32)
        if MOMENT_SCALE is None:
            s = _jnp.sqrt(_jnp.mean(_jnp.square(w)) + 1e-30)
        else:
            s = MOMENT_SCALE[name]
        km, kv = _jax.random.split(_jax.random.fold_in(key, i + 1))
        out[name] = w
        out["m_" + name] = s * _jax.random.normal(km, w.shape, _jnp.float32)
        out["v_" + name] = (s * s) * _jax.random.uniform(kv, w.shape, _jnp.float32, 0.5, 1.5)
    if N_MICROBATCH > 1:
        for name, axis in PER_EXAMPLE_BATCH_AXIS.items():
            out[name] = _to_microbatches(out[name], axis)
    return {'x': out['x'], 'positions': out['positions'], 'pre_mix_g': out['pre_mix_g'], 'post_mix_g': out['post_mix_g'], 'pre_ffn_g': out['pre_ffn_g'], 'post_ffn_g': out['post_ffn_g'], 'a_w_in': out['a_w_in'], 'a_b_in': out['a_b_in'], 'a_ln_g': out['a_ln_g'], 'a_ln_b': out['a_ln_b'], 'a_w_s': out['a_w_s'], 'a_b_s': out['a_b_s'], 'a_w_out': out['a_w_out'], 'b_w_qkv': out['b_w_qkv'], 'b_b_qkv': out['b_b_qkv'], 'b_sinks': out['b_sinks'], 'b_w_o': out['b_w_o'], 'ffn_w_gu': out['ffn_w_gu'], 'ffn_w_down': out['ffn_w_down'], 'loss_target': out['loss_target'], 'm_pre_mix_g': out['m_pre_mix_g'], 'm_post_mix_g': out['m_post_mix_g'], 'm_pre_ffn_g': out['m_pre_ffn_g'], 'm_post_ffn_g': out['m_post_ffn_g'], 'm_a_w_in': out['m_a_w_in'], 'm_a_b_in': out['m_a_b_in'], 'm_a_ln_g': out['m_a_ln_g'], 'm_a_ln_b': out['m_a_ln_b'], 'm_a_w_s': out['m_a_w_s'], 'm_a_b_s': out['m_a_b_s'], 'm_a_w_out': out['m_a_w_out'], 'm_b_w_qkv': out['m_b_w_qkv'], 'm_b_b_qkv': out['m_b_b_qkv'], 'm_b_sinks': out['m_b_sinks'], 'm_b_w_o': out['m_b_w_o'], 'm_ffn_w_gu': out['m_ffn_w_gu'], 'm_ffn_w_down': out['m_ffn_w_down'], 'v_pre_mix_g': out['v_pre_mix_g'], 'v_post_mix_g': out['v_post_mix_g'], 'v_pre_ffn_g': out['v_pre_ffn_g'], 'v_post_ffn_g': out['v_post_ffn_g'], 'v_a_w_in': out['v_a_w_in'], 'v_a_b_in': out['v_a_b_in'], 'v_a_ln_g': out['v_a_ln_g'], 'v_a_ln_b': out['v_a_ln_b'], 'v_a_w_s': out['v_a_w_s'], 'v_a_b_s': out['v_a_b_s'], 'v_a_w_out': out['v_a_w_out'], 'v_b_w_qkv': out['v_b_w_qkv'], 'v_b_b_qkv': out['v_b_b_qkv'], 'v_b_sinks': out['v_b_sinks'], 'v_b_w_o': out['v_b_w_o'], 'v_ffn_w_gu': out['v_ffn_w_gu'], 'v_ffn_w_down': out['v_ffn_w_down']}


def _loss(weights, diff, rest, loss_target):
    with _jax.named_scope("forward"):
        args = {**rest, TWIN_DIFF_INPUT: diff, **{k: w.astype(_WEIGHT_DTYPES[k]) for k, w in weights.items()}}
        y = _forward(args)
    with _jax.named_scope("loss_head"):
        err = _jnp.square(y.astype(_jnp.float32) - loss_target)
        return 0.5 * _jnp.sum(_jnp.mean(err, axis=-1)) if err.ndim else 0.5 * err


def _adamw(w, g, m, v):
    m = ADAM_B1 * m + (1.0 - ADAM_B1) * g
    v = ADAM_B2 * v + (1.0 - ADAM_B2) * _jnp.square(g)
    m_hat = m / (1.0 - ADAM_B1 ** ADAM_STEP)
    v_hat = v / (1.0 - ADAM_B2 ** ADAM_STEP)
    delta = -ADAM_LR * (m_hat / (_jnp.sqrt(v_hat) + ADAM_EPS) + ADAM_WD * w)
    return delta, m, v


def reference(x, positions, pre_mix_g, post_mix_g, pre_ffn_g, post_ffn_g, a_w_in, a_b_in, a_ln_g, a_ln_b, a_w_s, a_b_s, a_w_out, b_w_qkv, b_b_qkv, b_sinks, b_w_o, ffn_w_gu, ffn_w_down, loss_target, m_pre_mix_g, m_post_mix_g, m_pre_ffn_g, m_post_ffn_g, m_a_w_in, m_a_b_in, m_a_ln_g, m_a_ln_b, m_a_w_s, m_a_b_s, m_a_w_out, m_b_w_qkv, m_b_b_qkv, m_b_sinks, m_b_w_o, m_ffn_w_gu, m_ffn_w_down, v_pre_mix_g, v_post_mix_g, v_pre_ffn_g, v_post_ffn_g, v_a_w_in, v_a_b_in, v_a_ln_g, v_a_ln_b, v_a_w_s, v_a_b_s, v_a_w_out, v_b_w_qkv, v_b_b_qkv, v_b_sinks, v_b_w_o, v_ffn_w_gu, v_ffn_w_down):
    given = dict(x=x, positions=positions, pre_mix_g=pre_mix_g, post_mix_g=post_mix_g, pre_ffn_g=pre_ffn_g, post_ffn_g=post_ffn_g, a_w_in=a_w_in, a_b_in=a_b_in, a_ln_g=a_ln_g, a_ln_b=a_ln_b, a_w_s=a_w_s, a_b_s=a_b_s, a_w_out=a_w_out, b_w_qkv=b_w_qkv, b_b_qkv=b_b_qkv, b_sinks=b_sinks, b_w_o=b_w_o, ffn_w_gu=ffn_w_gu, ffn_w_down=ffn_w_down, loss_target=loss_target, m_pre_mix_g=m_pre_mix_g, m_post_mix_g=m_post_mix_g, m_pre_ffn_g=m_pre_ffn_g, m_post_ffn_g=m_post_ffn_g, m_a_w_in=m_a_w_in, m_a_b_in=m_a_b_in, m_a_ln_g=m_a_ln_g, m_a_ln_b=m_a_ln_b, m_a_w_s=m_a_w_s, m_a_b_s=m_a_b_s, m_a_w_out=m_a_w_out, m_b_w_qkv=m_b_w_qkv, m_b_b_qkv=m_b_b_qkv, m_b_sinks=m_b_sinks, m_b_w_o=m_b_w_o, m_ffn_w_gu=m_ffn_w_gu, m_ffn_w_down=m_ffn_w_down, v_pre_mix_g=v_pre_mix_g, v_post_mix_g=v_post_mix_g, v_pre_ffn_g=v_pre_ffn_g, v_post_ffn_g=v_post_ffn_g, v_a_w_in=v_a_w_in, v_a_b_in=v_a_b_in, v_a_ln_g=v_a_ln_g, v_a_ln_b=v_a_ln_b, v_a_w_s=v_a_w_s, v_a_b_s=v_a_b_s, v_a_w_out=v_a_w_out, v_b_w_qkv=v_b_w_qkv, v_b_b_qkv=v_b_b_qkv, v_b_sinks=v_b_sinks, v_b_w_o=v_b_w_o, v_ffn_w_gu=v_ffn_w_gu, v_ffn_w_down=v_ffn_w_down)
    weights = {n: given[n] for n in TWIN_WEIGHTS}
    shared = {n: given[n] for n in SHARED_INPUTS}
    per_example = {n: given[n] for n in ['x', 'positions']}
    grad_fn = _jax.value_and_grad(_loss, argnums=(0, 1))

    def one_microbatch(ex, loss_target):
        ex = dict(ex)
        diff = ex.pop(TWIN_DIFF_INPUT)
        return grad_fn(weights, diff, {**shared, **ex}, loss_target)

    if N_MICROBATCH == 1:
        loss, (grad_w, grad_x) = one_microbatch(per_example, given["loss_target"])
    else:
        def body(carry, xs):
            loss_sum, grad_sum = carry
            l_k, (gw_k, gx_k) = one_microbatch(xs[0], xs[1])
            with _jax.named_scope("update"):
                return (loss_sum + l_k, _jax.tree.map(_jnp.add, grad_sum, gw_k)), gx_k

        init = (_jnp.zeros((), _jnp.float32), _jax.tree.map(_jnp.zeros_like, weights))
        (loss, grad_w), grad_x = _jax.lax.scan(body, init, (per_example, given["loss_target"]))
    with _jax.named_scope("update"):
        delta_w, new_m, new_v = {}, {}, {}
        for n in TWIN_WEIGHTS:
            delta_w[n], new_m[n], new_v[n] = _adamw(weights[n], grad_w[n], given["m_" + n], given["v_" + n])
    return (loss, grad_x, *[grad_w[n] for n in TWIN_WEIGHTS], *[delta_w[n] for n in TWIN_WEIGHTS],
            *[new_m[n] for n in TWIN_WEIGHTS], *[new_v[n] for n in TWIN_WEIGHTS])
```

```python
import functools
import math

import jax
import jax.numpy as jnp
from jax import lax
from jax.experimental import pallas as pl
from jax.experimental.pallas import tpu as pltpu

F32, BF16 = jnp.float32, jnp.bfloat16
MESH = pl.DeviceIdType.MESH
N_DEV = 8

CHUNK = 128
HEAD_DIM = 64
N_KV_HEADS = 4
ROPE_DIM = HEAD_DIM // 4
ROPE_THETA = 500000.0
RMS_EPS = 1e-6
LN_EPS = 1e-5
NEG_INF = -1e30

ADAM_LR = 0.001
ADAM_B1 = 0.9
ADAM_B2 = 0.999
ADAM_EPS = 1e-08
ADAM_WD = 0.01
ADAM_STEP = 10

V7X_VMEM_BYTES = 64 * 2 ** 20
VMEM_LIMIT = V7X_VMEM_BYTES - 8 * 2 ** 20
LANES = 128


def _params(*sem):
    return pltpu.CompilerParams(dimension_semantics=sem or None, vmem_limit_bytes=VMEM_LIMIT)


def _tile(n, pref):
    if n <= pref:
        return n
    t = pref - pref % LANES
    while t >= LANES:
        if n % t == 0:
            return t
        t -= LANES
    return n


def _full(shape):
    return pl.BlockSpec(shape, lambda *_: (0,) * len(shape))


def _rows(tm, width):
    return pl.BlockSpec((tm, width), lambda i: (i, 0))


def _rms(x, g):
    r = lax.rsqrt(jnp.mean(x * x, axis=-1, keepdims=True) + RMS_EPS)
    return x * r * g


def _rms_bwd(x, g, dy):
    r = lax.rsqrt(jnp.mean(x * x, axis=-1, keepdims=True) + RMS_EPS)
    xhat = x * r
    dg = jnp.sum(dy * xhat, axis=0, keepdims=True)
    dxhat = dy * g
    dx = r * (dxhat - xhat * jnp.mean(dxhat * xhat, axis=-1, keepdims=True))
    return dx, dg


_INV_SQRT2 = 1.0 / math.sqrt(2.0)
_INV_SQRT2PI = 1.0 / math.sqrt(2.0 * math.pi)


def _gelu(x):
    return 0.5 * x * (1.0 + lax.erf(x * _INV_SQRT2))


def _gelu_grad(x):
    return 0.5 * (1.0 + lax.erf(x * _INV_SQRT2)) + x * jnp.exp(-0.5 * x * x) * _INV_SQRT2PI


def _sigmoid(x):
    return 1.0 / (1.0 + jnp.exp(-x))


def _position():
    return lax.axis_index("x"), lax.axis_index("y"), lax.axis_index("c")


def _index(p):
    return 4 * p[0] + 2 * p[1] + p[2]


def allgather(xs, name):
    n = len(xs)
    any_spec = pl.BlockSpec(memory_space=pl.ANY)

    def body(*refs):
        x_refs, o_refs = refs[:n], refs[n:2 * n]
        send_sems, recv_sems, local_sems = refs[2 * n:]
        x, y, c = _position()
        me, sibling = (x, y, c), (x, y, 1 - c)
        chips = [(1 - x, y), (x, 1 - y), (1 - x, 1 - y)]

        def slot(a, p):
            return o_refs[a].at[:, pl.ds(_index(p), 1)]

        def copy(a, k, block, to, src=None):
            return pltpu.make_async_remote_copy(
                src_ref=slot(a, block) if src is None else src, dst_ref=slot(a, block),
                send_sem=send_sems.at[7 * a + k], recv_sem=recv_sems.at[7 * a + k],
                device_id=to, device_id_type=MESH)

        mine = [pltpu.make_async_copy(x_refs[a], slot(a, me), local_sems.at[a]) for a in range(n)]
        for cp in mine:
            cp.start()
        first = []
        for a in range(n):
            first.append(copy(a, 0, me, sibling, src=x_refs[a]))
            first += [copy(a, 1 + j, me, (*chip, c), src=x_refs[a]) for j, chip in enumerate(chips)]
        for cp in first:
            cp.start()
        passed = []
        for j, chip in enumerate(chips):
            for a in range(n):
                copy(a, 1 + j, (*chip, c), me).wait_recv()
                cp = copy(a, 4 + j, (*chip, c), sibling)
                cp.start()
                passed.append(cp)
        for a in range(n):
            copy(a, 0, sibling, me).wait_recv()
            for j, chip in enumerate(chips):
                copy(a, 4 + j, (*chip, 1 - c), me).wait_recv()
        for cp in first + passed:
            cp.wait_send()
        for cp in mine:
            cp.wait()

    return pl.pallas_call(
        body, name=name,
        out_shape=[jax.ShapeDtypeStruct((v.shape[0], N_DEV) + v.shape[2:], v.dtype) for v in xs],
        in_specs=[any_spec] * n, out_specs=[any_spec] * n,
        scratch_shapes=[pltpu.SemaphoreType.DMA((7 * n,)), pltpu.SemaphoreType.DMA((7 * n,)),
                        pltpu.SemaphoreType.DMA((n,))],
    )(*xs)


def exchange(groups, name):
    flat = [g for grp in groups for g in grp]
    starts = []
    for grp in groups:
        starts.append(sum(len(g) for g in groups[:len(starts)]))
    n_in, n_out = len(flat), len(groups)
    any_spec = pl.BlockSpec(memory_space=pl.ANY)

    def body(*refs):
        g_refs, o_refs = refs[:n_in], refs[n_in:n_in + n_out]
        send_sems, recv_sems, local_sems = refs[n_in + n_out:]
        x, y, c = _position()
        me = (x, y, c)

        def peer(k):
            return (x ^ ((k >> 2) & 1), y ^ ((k >> 1) & 1), c ^ (k & 1))

        def block(q, p):
            return g_refs[q].at[pl.ds(_index(p), 1)]

        def slot(a, l, p):
            return o_refs[a].at[l, pl.ds(_index(p), 1)]

        arrays = [(a, l, starts[a] + l) for a, grp in enumerate(groups) for l in range(len(grp))]
        local = [pltpu.make_async_copy(block(q, me), slot(a, l, me), local_sems.at[q]) for a, l, q in arrays]
        for cp in local:
            cp.start()
        remote = [pltpu.make_async_remote_copy(
            src_ref=block(q, peer(k)), dst_ref=slot(a, l, me),
            send_sem=send_sems.at[7 * q + k - 1], recv_sem=recv_sems.at[7 * q + k - 1],
            device_id=peer(k), device_id_type=MESH) for a, l, q in arrays for k in range(1, N_DEV)]
        for cp in remote:
            cp.start()
        for cp in remote:
            cp.wait_send()
        for a, l, q in arrays:
            for k in range(1, N_DEV):
                landed = slot(a, l, peer(k))
                pltpu.make_async_remote_copy(
                    src_ref=landed, dst_ref=landed,
                    send_sem=send_sems.at[7 * q + k - 1], recv_sem=recv_sems.at[7 * q + k - 1],
                    device_id=peer(k), device_id_type=MESH).wait_recv()
        for cp in local:
            cp.wait()

    return pl.pallas_call(
        body, name=name,
        out_shape=[jax.ShapeDtypeStruct((len(grp),) + grp[0].shape, grp[0].dtype) for grp in groups],
        in_specs=[any_spec] * n_in, out_specs=[any_spec] * n_out,
        scratch_shapes=[pltpu.SemaphoreType.DMA((7 * n_in,)), pltpu.SemaphoreType.DMA((7 * n_in,)),
                        pltpu.SemaphoreType.DMA((n_in,))],
    )(*flat)


def prep_weight(w, transpose, name):
    L, K, n = w.shape
    out_block = (1, 1, n, K) if transpose else (1, 1, K, n)

    def body(w_ref, o_ref):
        v = w_ref[0]
        o_ref[0, 0] = (v.T if transpose else v).astype(BF16)

    return pl.pallas_call(
        body, name=name, grid=(L,),
        in_specs=[pl.BlockSpec((1, K, n), lambda l: (l, 0, 0))],
        out_specs=pl.BlockSpec(out_block, lambda l: (l, 0, 0, 0)),
        out_shape=jax.ShapeDtypeStruct((L,) + out_block[1:], BF16),
        compiler_params=_params("parallel"),
    )(w)


def sum_parts(parts, transpose, name):
    L, P, r, c = parts.shape
    tr = _tile(r, 256) if (not transpose and r % 8 == 0) else r
    if r % tr:
        tr = r
    out_block = (1, c, tr) if transpose else (1, tr, c)
    out_map = (lambda l, i: (l, 0, i)) if transpose else (lambda l, i: (l, i, 0))

    def body(p_ref, o_ref):
        acc = p_ref[0, 0].astype(F32)
        for s in range(1, P):
            acc = acc + p_ref[0, s].astype(F32)
        o_ref[0] = acc.T if transpose else acc

    return pl.pallas_call(
        body, name=name, grid=(L, r // tr),
        in_specs=[pl.BlockSpec((1, P, tr, c), lambda l, i: (l, 0, i, 0))],
        out_specs=pl.BlockSpec(out_block, out_map),
        out_shape=jax.ShapeDtypeStruct((L, c, r) if transpose else (L, r, c), F32),
        compiler_params=_params("parallel", "parallel"),
    )(parts)


def adamw(w, g, m, v, name):
    R, C = w.shape
    tr = R
    for cand in (512, 256, 128, 64, 32, 16, 8):
        if R % cand == 0:
            tr = cand
            break
    c1 = 1.0 / (1.0 - ADAM_B1 ** ADAM_STEP)
    c2 = 1.0 / (1.0 - ADAM_B2 ** ADAM_STEP)

    def body(w_ref, g_ref, m_ref, v_ref, d_ref, mo_ref, vo_ref):
        gv = g_ref[...]
        mn = ADAM_B1 * m_ref[...] + (1.0 - ADAM_B1) * gv
        vn = ADAM_B2 * v_ref[...] + (1.0 - ADAM_B2) * (gv * gv)
        d_ref[...] = -ADAM_LR * ((mn * c1) / (jnp.sqrt(vn * c2) + ADAM_EPS) + ADAM_WD * w_ref[...])
        mo_ref[...] = mn
        vo_ref[...] = vn

    spec = pl.BlockSpec((tr, C), lambda i: (i, 0))
    return pl.pallas_call(
        body, name=name, grid=(R // tr,),
        in_specs=[spec] * 4, out_specs=[spec] * 3,
        out_shape=[jax.ShapeDtypeStruct((R, C), F32)] * 3,
        compiler_params=_params("parallel"),
    )(w, g, m, v)


def add_norm(h, y, g_post, g_pre, name):
    T, D = h.shape
    tm = _tile(T, 512)
    has_y = y is not None

    def body(*refs):
        if has_y:
            h_ref, y_ref, gp_ref, g_ref, ho_ref, hn_ref = refs
            hv = h_ref[...] + _rms(y_ref[...], gp_ref[...])
            ho_ref[...] = hv
        else:
            h_ref, g_ref, hn_ref = refs
            hv = h_ref[...]
        hn_ref[...] = _rms(hv, g_ref[...]).astype(BF16)

    row, vec = _rows(tm, D), _full((1, D))
    if has_y:
        return pl.pallas_call(
            body, name=name, grid=(T // tm,), in_specs=[row, row, vec, vec], out_specs=[row, row],
            out_shape=[jax.ShapeDtypeStruct((T, D), F32), jax.ShapeDtypeStruct((T, D), BF16)],
            compiler_params=_params("parallel"))(h, y, g_post, g_pre)
    hn = pl.pallas_call(
        body, name=name, grid=(T // tm,), in_specs=[row, vec], out_specs=row,
        out_shape=jax.ShapeDtypeStruct((T, D), BF16), compiler_params=_params("parallel"))(h, g_pre)
    return h, hn


def add_norm_loss(h, y, g_post, target, name):
    T, D = h.shape
    tm = _tile(T, 512)

    def body(h_ref, y_ref, gp_ref, t_ref, dh_ref, loss_ref):
        @pl.when(pl.program_id(0) == 0)
        def _():
            loss_ref[...] = jnp.zeros_like(loss_ref)
        err = h_ref[...] + _rms(y_ref[...], gp_ref[...]) - t_ref[...]
        dh_ref[...] = err * (1.0 / D)
        loss_ref[...] += jnp.sum(err * err)

    row = _rows(tm, D)
    return pl.pallas_call(
        body, name=name, grid=(T // tm,), in_specs=[row, row, _full((1, D)), row],
        out_specs=[row, _full((8, LANES))],
        out_shape=[jax.ShapeDtypeStruct((T, D), F32), jax.ShapeDtypeStruct((8, LANES), F32)],
        compiler_params=_params("arbitrary"))(h, y, g_post, target)


def norm_bwd(dh_in, pre, post, name):
    T, D = dh_in.shape
    tm = _tile(T, 512)
    row, vec = _rows(tm, D), _full((1, D))
    ins, in_specs, outs, out_specs, names = [dh_in], [row], [], [], []
    if pre is not None:
        ins += list(pre)
        in_specs += [row, row, vec]
        outs += [jax.ShapeDtypeStruct((T, D), F32), jax.ShapeDtypeStruct((1, D), F32)]
        out_specs += [row, vec]
        names += ["dh", "dg_pre"]
    if post is not None:
        ins += list(post)
        in_specs += [row, vec]
        outs += [jax.ShapeDtypeStruct((T, D), BF16), jax.ShapeDtypeStruct((1, D), F32)]
        out_specs += [row, vec]
        names += ["dy", "dg_post"]

    def body(*refs):
        refs = list(refs)
        first = pl.program_id(0) == 0
        dh = refs.pop(0)[...]
        if pre is not None:
            dpre_ref, x_ref, g_ref = refs[:3]
            refs = refs[3:]
        if post is not None:
            y_ref, gp_ref = refs[:2]
            refs = refs[2:]
        if pre is not None:
            dh_ref, dg_ref = refs[:2]
            refs = refs[2:]
            dx, dg = _rms_bwd(x_ref[...], g_ref[...], dpre_ref[...].astype(F32))
            dh = dh + dx
            dh_ref[...] = dh

            @pl.when(first)
            def _():
                dg_ref[...] = jnp.zeros_like(dg_ref)
            dg_ref[...] += dg
        if post is not None:
            dy_ref, dgp_ref = refs[:2]
            dy, dgp = _rms_bwd(y_ref[...], gp_ref[...], dh)
            dy_ref[...] = dy.astype(BF16)

            @pl.when(first)
            def _():
                dgp_ref[...] = jnp.zeros_like(dgp_ref)
            dgp_ref[...] += dgp

    res = pl.pallas_call(
        body, name=name, grid=(T // tm,), in_specs=in_specs, out_specs=out_specs, out_shape=outs,
        compiler_params=_params("arbitrary"))(*ins)
    return dict(zip(names, res))


_NT = (((1,), (1,)), ((), ()))
_TN = (((0,), (0,)), ((), ()))


def rowmm(a, w, transposed_w, bias, out_dtype, name):
    M, K = a.shape
    N = w.shape[0] if transposed_w else w.shape[1]
    tm = _tile(M, 256)
    tn = _tile(N, 512)

    def body(*refs):
        a_ref, w_ref = refs[:2]
        o_ref = refs[-1]
        av = a_ref[...]
        for j in range(N // tn):
            cols = pl.ds(j * tn, tn)
            if transposed_w:
                acc = lax.dot_general(av, w_ref[cols, :], _NT, preferred_element_type=F32)
            else:
                acc = jnp.dot(av, w_ref[:, cols], preferred_element_type=F32)
            if bias is not None:
                acc = acc + refs[2][:, cols]
            o_ref[:, cols] = acc.astype(out_dtype)

    ins, in_specs = [a, w], [_rows(tm, K), _full(w.shape)]
    if bias is not None:
        ins.append(bias)
        in_specs.append(_full((1, N)))
    return pl.pallas_call(
        body, name=name, grid=(M // tm,), in_specs=in_specs, out_specs=_rows(tm, N),
        out_shape=jax.ShapeDtypeStruct((M, N), out_dtype), compiler_params=_params("parallel"))(*ins)


def grad_mm(a, b, name):
    T, N = a.shape
    K = b.shape[1]
    tn = _tile(N, 512)
    tt = _tile(T, 512)

    def body(a_ref, b_ref, o_ref, acc_ref):
        t = pl.program_id(1)

        @pl.when(t == 0)
        def _():
            acc_ref[...] = jnp.zeros_like(acc_ref)
        acc_ref[...] += lax.dot_general(a_ref[...], b_ref[...], _TN, preferred_element_type=F32)

        @pl.when(t == pl.num_programs(1) - 1)
        def _():
            o_ref[...] = acc_ref[...].astype(BF16)

    return pl.pallas_call(
        body, name=name, grid=(N // tn, T // tt),
        in_specs=[pl.BlockSpec((tt, tn), lambda j, t: (t, j)), pl.BlockSpec((tt, K), lambda j, t: (t, 0))],
        out_specs=pl.BlockSpec((tn, K), lambda j, t: (j, 0)),
        out_shape=jax.ShapeDtypeStruct((N, K), BF16),
        scratch_shapes=[pltpu.VMEM((tn, K), F32)],
        compiler_params=_params("parallel", "arbitrary"))(a, b)


def ffn_up(fn, w_gu_t, name):
    T, D = fn.shape
    F = w_gu_t.shape[0] // 2
    tm = _tile(T, 256)
    tn = _tile(F, 512)

    def body(a_ref, w_ref, gu_ref, act_ref):
        av = a_ref[...]
        for j in range(F // tn):
            g = lax.dot_general(av, w_ref[pl.ds(j * tn, tn), :], _NT, preferred_element_type=F32)
            up = lax.dot_general(av, w_ref[pl.ds(F + j * tn, tn), :], _NT, preferred_element_type=F32)
            gu_ref[:, pl.ds(j * tn, tn)] = g.astype(BF16)
            gu_ref[:, pl.ds(F + j * tn, tn)] = up.astype(BF16)
            act_ref[:, pl.ds(j * tn, tn)] = (g * _sigmoid(g) * up).astype(BF16)

    return pl.pallas_call(
        body, name=name, grid=(T // tm,), in_specs=[_rows(tm, D), _full(w_gu_t.shape)],
        out_specs=[_rows(tm, 2 * F), _rows(tm, F)],
        out_shape=[jax.ShapeDtypeStruct((T, 2 * F), BF16), jax.ShapeDtypeStruct((T, F), BF16)],
        compiler_params=_params("parallel"))(fn, w_gu_t)


def ffn_dact(df, w_down, gu, name):
    T, D = df.shape
    F = w_down.shape[0]
    tm = _tile(T, 256)
    tn = _tile(F, 512)

    def body(d_ref, w_ref, gu_ref, o_ref):
        dv = d_ref[...]
        for j in range(F // tn):
            dact = lax.dot_general(dv, w_ref[pl.ds(j * tn, tn), :], _NT, preferred_element_type=F32)
            g = gu_ref[:, pl.ds(j * tn, tn)].astype(F32)
            up = gu_ref[:, pl.ds(F + j * tn, tn)].astype(F32)
            sg = _sigmoid(g)
            o_ref[:, pl.ds(j * tn, tn)] = (dact * up * (sg * (1.0 + g * (1.0 - sg)))).astype(BF16)
            o_ref[:, pl.ds(F + j * tn, tn)] = (dact * g * sg).astype(BF16)

    return pl.pallas_call(
        body, name=name, grid=(T // tm,), in_specs=[_rows(tm, D), _full(w_down.shape), _rows(tm, 2 * F)],
        out_specs=_rows(tm, 2 * F), out_shape=jax.ShapeDtypeStruct((T, 2 * F), BF16),
        compiler_params=_params("parallel"))(df, w_down, gu)


def _layernorm_stats(v):
    mu = jnp.mean(v, axis=-1, keepdims=True)
    cen = v - mu
    rstd = lax.rsqrt(jnp.mean(cen * cen, axis=-1, keepdims=True) + LN_EPS)
    return cen * rstd, rstd


def gmlp_fwd(a, ln_g, ln_b, w_mask, b_s_t, name):
    T, W2 = a.shape
    W = W2 // 2
    G = w_mask.shape[0]
    C = W // G

    def body(a_ref, g_ref, b_ref, w_ref, bs_ref, o_ref):
        xhat, _ = _layernorm_stats(_gelu(a_ref[:, W:].astype(F32)))
        vln = (xhat * g_ref[...] + b_ref[...]).astype(BF16)
        for g in range(G):
            cols = pl.ds(g * C, C)
            sv = jnp.dot(w_ref[g], vln[:, g * C:(g + 1) * C], preferred_element_type=F32) + bs_ref[:, g:g + 1]
            o_ref[:, cols] = (_gelu(a_ref[:, cols].astype(F32)) * sv).astype(BF16)

    return pl.pallas_call(
        body, name=name, grid=(T // CHUNK,),
        in_specs=[_rows(CHUNK, W2), _full((1, W)), _full((1, W)), _full(w_mask.shape), _full(b_s_t.shape)],
        out_specs=_rows(CHUNK, W), out_shape=jax.ShapeDtypeStruct((T, W), BF16),
        compiler_params=_params("parallel"))(a, ln_g, ln_b, w_mask, b_s_t)


def gmlp_bwd(a, dgated, ln_g, ln_b, w_mask, w_mask_t, b_s_t, group_onehot, name):
    T, W2 = a.shape
    W = W2 // 2
    G = w_mask.shape[0]
    C = W // G

    def body(a_ref, dg_ref, g_ref, b_ref, w_ref, wt_ref, bs_ref, e_ref,
             da_ref, dws_ref, dbs_ref, dlg_ref, dlb_ref, dbin_ref, dvln_ref):
        @pl.when(pl.program_id(0) == 0)
        def _():
            for r in (dws_ref, dbs_ref, dlg_ref, dlb_ref, dbin_ref):
                r[...] = jnp.zeros_like(r)
        row = lax.broadcasted_iota(jnp.int32, (CHUNK, CHUNK), 0)
        col = lax.broadcasted_iota(jnp.int32, (CHUNK, CHUNK), 1)
        causal = col <= row
        av = a_ref[:, W:].astype(F32)
        xhat, rstd = _layernorm_stats(_gelu(av))
        vln = (xhat * g_ref[...] + b_ref[...]).astype(BF16)
        dsv_all = (dg_ref[...].astype(F32) * _gelu(a_ref[:, :W].astype(F32))).astype(BF16)
        dbs_ref[...] += jnp.dot(dsv_all, e_ref[...], preferred_element_type=F32)
        for g in range(G):
            cols = pl.ds(g * C, C)
            vg = vln[:, g * C:(g + 1) * C]
            dsv = dsv_all[:, g * C:(g + 1) * C]
            sv = jnp.dot(w_ref[g], vg, preferred_element_type=F32) + bs_ref[:, g:g + 1]
            au = a_ref[:, cols].astype(F32)
            dau = dg_ref[:, cols].astype(F32) * sv * _gelu_grad(au)
            da_ref[:, cols] = dau.astype(BF16)
            dbin_ref[:, cols] += jnp.sum(dau, axis=0, keepdims=True)
            dws_ref[g] += jnp.where(causal, lax.dot_general(dsv, vg, _NT, preferred_element_type=F32), 0.0)
            dvln_ref[:, cols] = jnp.dot(wt_ref[g], dsv, preferred_element_type=F32)
        dvln = dvln_ref[...]
        dlg_ref[...] += jnp.sum(dvln * xhat, axis=0, keepdims=True)
        dlb_ref[...] += jnp.sum(dvln, axis=0, keepdims=True)
        dxhat = dvln * g_ref[...]
        dv = rstd * (dxhat - jnp.mean(dxhat, axis=-1, keepdims=True)
                     - xhat * jnp.mean(dxhat * xhat, axis=-1, keepdims=True))
        dav = dv * _gelu_grad(av)
        da_ref[:, W:] = dav.astype(BF16)
        dbin_ref[:, W:] += jnp.sum(dav, axis=0, keepdims=True)

    return pl.pallas_call(
        body, name=name, grid=(T // CHUNK,),
        in_specs=[_rows(CHUNK, W2), _rows(CHUNK, W), _full((1, W)), _full((1, W)), _full(w_mask.shape),
                  _full(w_mask_t.shape), _full(b_s_t.shape), _full(group_onehot.shape)],
        out_specs=[_rows(CHUNK, W2), _full((G, CHUNK, CHUNK)), _full((CHUNK, LANES)), _full((1, W)),
                   _full((1, W)), _full((1, W2))],
        out_shape=[jax.ShapeDtypeStruct((T, W2), BF16), jax.ShapeDtypeStruct((G, CHUNK, CHUNK), F32),
                   jax.ShapeDtypeStruct((CHUNK, LANES), F32), jax.ShapeDtypeStruct((1, W), F32),
                   jax.ShapeDtypeStruct((1, W), F32), jax.ShapeDtypeStruct((1, W2), F32)],
        scratch_shapes=[pltpu.VMEM((CHUNK, W), F32)],
        compiler_params=_params("arbitrary"))(a, dgated, ln_g, ln_b, w_mask, w_mask_t, b_s_t, group_onehot)


def rope_tables(pos, inv_freq_row, name):
    T = pos.shape[0]
    tm = _tile(T, 512)

    def body(p_ref, f_ref, c_ref, s1_ref, s2_ref):
        ang = p_ref[...].astype(F32) * f_ref[...]
        lane = lax.broadcasted_iota(jnp.int32, (tm, LANES), 1) % HEAD_DIM
        sin = jnp.sin(ang)
        c_ref[...] = jnp.cos(ang)
        s1_ref[...] = jnp.where(lane < _HALF, -sin, 0.0)
        s2_ref[...] = jnp.where((lane >= _HALF) & (lane < ROPE_DIM), sin, 0.0)

    tab = _rows(tm, LANES)
    return pl.pallas_call(
        body, name=name, grid=(T // tm,), in_specs=[_rows(tm, 1), _full((1, LANES))], out_specs=[tab] * 3,
        out_shape=[jax.ShapeDtypeStruct((T, LANES), F32)] * 3, compiler_params=_params("parallel"))(pos, inv_freq_row)


_HALF = ROPE_DIM // 2


def _slabs(x):
    return [x[:, b * LANES:(b + 1) * LANES] for b in range(x.shape[1] // LANES)]


def _rotate(x, c, s1, s2):
    return [xs * c + pltpu.roll(xs, LANES - _HALF, 1) * s1 + pltpu.roll(xs, _HALF, 1) * s2 for xs in _slabs(x)]


def _rotate_transposed(dy, c, s1, s2):
    return [ds * c + pltpu.roll(ds * s1, _HALF, 1) + pltpu.roll(ds * s2, LANES - _HALF, 1) for ds in _slabs(dy)]


def rope_fwd(qkv, tabs, q_width, kv_width, name):
    T, QKV = qkv.shape
    tm = _tile(T, 512)
    QK = q_width + kv_width
    scale = HEAD_DIM ** -0.5

    def body(x_ref, c_ref, s1_ref, s2_ref, o_ref):
        x = x_ref[:, :QK].astype(F32)
        slabs = _rotate(x, c_ref[...], s1_ref[...], s2_ref[...])
        for b, y in enumerate(slabs):
            if b * LANES < q_width:
                y = y * scale
            o_ref[:, b * LANES:(b + 1) * LANES] = y.astype(BF16)

    tab = _rows(tm, LANES)
    return pl.pallas_call(
        body, name=name, grid=(T // tm,), in_specs=[_rows(tm, QKV), tab, tab, tab], out_specs=_rows(tm, QK),
        out_shape=jax.ShapeDtypeStruct((T, QK), BF16), compiler_params=_params("parallel"))(qkv, *tabs)


def rope_bwd(dq, dk, dv, tabs, name):
    T, Q = dq.shape
    KV = dk.shape[1]
    tm = _tile(T, 512)
    scale = HEAD_DIM ** -0.5

    def body(dq_ref, dk_ref, dv_ref, c_ref, s1_ref, s2_ref, o_ref, b_ref):
        @pl.when(pl.program_id(0) == 0)
        def _():
            b_ref[...] = jnp.zeros_like(b_ref)
        tabs_v = (c_ref[...], s1_ref[...], s2_ref[...])
        pieces = [s * scale for s in _rotate_transposed(dq_ref[...], *tabs_v)]
        pieces += _rotate_transposed(dk_ref[...], *tabs_v)
        pieces += _slabs(dv_ref[...])
        for b, y in enumerate(pieces):
            cols = pl.ds(b * LANES, LANES)
            o_ref[:, cols] = y.astype(BF16)
            b_ref[:, cols] += jnp.sum(y, axis=0, keepdims=True)

    tab = _rows(tm, LANES)
    return pl.pallas_call(
        body, name=name, grid=(T // tm,), in_specs=[_rows(tm, Q), _rows(tm, KV), _rows(tm, KV), tab, tab, tab],
        out_specs=[_rows(tm, Q + 2 * KV), _full((1, Q + 2 * KV))],
        out_shape=[jax.ShapeDtypeStruct((T, Q + 2 * KV), BF16), jax.ShapeDtypeStruct((1, Q + 2 * KV), F32)],
        compiler_params=_params("arbitrary"))(dq, dk, dv, *tabs)


def _attn_block(n, q_ref, kc_ref, kp_ref, vc_ref, vp_ref, sink_ref, kh, group):
    GR = group * CHUNK
    lanes = slice(kh * HEAD_DIM, (kh + 1) * HEAD_DIM)
    kk = jnp.concatenate([kp_ref[:, lanes], kc_ref[:, lanes]], axis=0)
    vv = jnp.concatenate([vp_ref[:, lanes], vc_ref[:, lanes]], axis=0)
    qg = jnp.concatenate(
        [q_ref[:, (kh * group + g) * HEAD_DIM:(kh * group + g + 1) * HEAD_DIM] for g in range(group)], axis=0)
    s = lax.dot_general(qg, kk, _NT, preferred_element_type=F32)
    qi = lax.broadcasted_iota(jnp.int32, (GR, 2 * CHUNK), 0) % CHUNK
    sj = lax.broadcasted_iota(jnp.int32, (GR, 2 * CHUNK), 1)
    valid = (sj > qi) & (sj <= qi + CHUNK) & ((n > 0) | (sj >= CHUNK))
    s = jnp.where(valid, s, NEG_INF)
    head = lax.broadcasted_iota(jnp.int32, (GR, 1), 0) // CHUNK
    sink = jnp.zeros((GR, 1), F32)
    for g in range(group):
        sink = jnp.where(head == g, sink_ref[kh * group + g], sink)
    m = jnp.maximum(jnp.max(s, axis=-1, keepdims=True), sink)
    p = jnp.exp(s - m)
    e_sink = jnp.exp(sink - m)
    inv = 1.0 / (jnp.sum(p, axis=-1, keepdims=True) + e_sink)
    return qg, kk, vv, p * inv, e_sink * inv


def _attn_specs(q_width, kv_width, order):
    qb = q_width // kv_width
    prev = lambda i: jnp.maximum(order(i) - 1, 0)
    return [pl.BlockSpec((CHUNK, q_width), lambda i: (order(i), 0)),
            pl.BlockSpec((CHUNK, kv_width), lambda i: (order(i), qb)),
            pl.BlockSpec((CHUNK, kv_width), lambda i: (prev(i), qb)),
            pl.BlockSpec((CHUNK, kv_width), lambda i: (order(i), qb + 1)),
            pl.BlockSpec((CHUNK, kv_width), lambda i: (prev(i), qb + 1))]


def attn_fwd(qk, qkv, sinks, q_width, kv_width, name):
    T = qk.shape[0]
    group = q_width // kv_width

    def body(q_ref, kc_ref, kp_ref, vc_ref, vp_ref, sink_ref, o_ref):
        n = pl.program_id(0)
        for kh in range(kv_width // HEAD_DIM):
            _, _, vv, p, _ = _attn_block(n, q_ref, kc_ref, kp_ref, vc_ref, vp_ref, sink_ref, kh, group)
            og = jnp.dot(p.astype(BF16), vv, preferred_element_type=F32)
            for g in range(group):
                h = kh * group + g
                o_ref[:, h * HEAD_DIM:(h + 1) * HEAD_DIM] = og[g * CHUNK:(g + 1) * CHUNK].astype(BF16)

    specs = _attn_specs(q_width, kv_width, lambda i: i)
    return pl.pallas_call(
        body, name=name, grid=(T // CHUNK,),
        in_specs=specs + [pl.BlockSpec(memory_space=pltpu.SMEM)],
        out_specs=_rows(CHUNK, q_width), out_shape=jax.ShapeDtypeStruct((T, q_width), BF16),
        compiler_params=_params("parallel"))(qk, qk, qk, qkv, qkv, sinks)


def attn_bwd(qk, qkv, do, sinks, q_width, kv_width, name):
    T = qk.shape[0]
    NB = T // CHUNK
    group = q_width // kv_width

    def body(q_ref, kc_ref, kp_ref, vc_ref, vp_ref, do_ref, sink_ref, dq_ref, dk_ref, dv_ref, ds_ref, ck_ref, cv_ref):
        i = pl.program_id(0)
        n = NB - 1 - i

        @pl.when(i == 0)
        def _():
            ck_ref[...] = jnp.zeros_like(ck_ref)
            cv_ref[...] = jnp.zeros_like(cv_ref)
            ds_ref[...] = jnp.zeros_like(ds_ref)
        lane = lax.broadcasted_iota(jnp.int32, (1, LANES), 1)
        dsink_row = jnp.zeros((1, LANES), F32)
        for kh in range(kv_width // HEAD_DIM):
            qg, kk, vv, p, p_sink = _attn_block(n, q_ref, kc_ref, kp_ref, vc_ref, vp_ref, sink_ref, kh, group)
            dog = jnp.concatenate(
                [do_ref[:, (kh * group + g) * HEAD_DIM:(kh * group + g + 1) * HEAD_DIM] for g in range(group)], axis=0)
            dp = lax.dot_general(dog, vv, _NT, preferred_element_type=F32)
            delta = jnp.sum(p * dp, axis=-1, keepdims=True)
            ds = (p * (dp - delta)).astype(BF16)
            dsink = -p_sink * delta
            for g in range(group):
                h = kh * group + g
                dsink_row = dsink_row + jnp.where(lane == h, jnp.sum(dsink[g * CHUNK:(g + 1) * CHUNK]), 0.0)
            dqg = jnp.dot(ds, kk, preferred_element_type=F32)
            for g in range(group):
                h = kh * group + g
                dq_ref[:, h * HEAD_DIM:(h + 1) * HEAD_DIM] = dqg[g * CHUNK:(g + 1) * CHUNK]
            dkk = lax.dot_general(ds, qg, _TN, preferred_element_type=F32)
            dvv = lax.dot_general(p.astype(BF16), dog, _TN, preferred_element_type=F32)
            lanes = slice(kh * HEAD_DIM, (kh + 1) * HEAD_DIM)
            dk_ref[:, lanes] = dkk[CHUNK:] + ck_ref[:, lanes]
            dv_ref[:, lanes] = dvv[CHUNK:] + cv_ref[:, lanes]
            ck_ref[:, lanes] = dkk[:CHUNK]
            cv_ref[:, lanes] = dvv[:CHUNK]
        ds_ref[0:1, :] += dsink_row

    order = lambda i: NB - 1 - i
    specs = _attn_specs(q_width, kv_width, order)
    kv_out = pl.BlockSpec((CHUNK, kv_width), lambda i: (order(i), 0))
    q_rows = pl.BlockSpec((CHUNK, q_width), lambda i: (order(i), 0))
    return pl.pallas_call(
        body, name=name, grid=(NB,),
        in_specs=specs + [q_rows, pl.BlockSpec(memory_space=pltpu.SMEM)],
        out_specs=[q_rows, kv_out, kv_out, _full((8, LANES))],
        out_shape=[jax.ShapeDtypeStruct((T, q_width), F32), jax.ShapeDtypeStruct((T, kv_width), F32),
                   jax.ShapeDtypeStruct((T, kv_width), F32), jax.ShapeDtypeStruct((8, LANES), F32)],
        scratch_shapes=[pltpu.VMEM((CHUNK, kv_width), F32), pltpu.VMEM((CHUNK, kv_width), F32)],
        compiler_params=_params("arbitrary"))(qk, qk, qk, qkv, qkv, do, sinks)


def _blocked(w):
    return w.reshape(N_DEV, w.shape[0] // N_DEV, w.shape[1])


def kernel(x, positions, pre_mix_g, post_mix_g, pre_ffn_g, post_ffn_g, a_w_in, a_b_in, a_ln_g, a_ln_b, a_w_s, a_b_s, a_w_out, b_w_qkv, b_b_qkv, b_sinks, b_w_o, ffn_w_gu, ffn_w_down, loss_target, m_pre_mix_g, m_post_mix_g, m_pre_ffn_g, m_post_ffn_g, m_a_w_in, m_a_b_in, m_a_ln_g, m_a_ln_b, m_a_w_s, m_a_b_s, m_a_w_out, m_b_w_qkv, m_b_b_qkv, m_b_sinks, m_b_w_o, m_ffn_w_gu, m_ffn_w_down, v_pre_mix_g, v_post_mix_g, v_pre_ffn_g, v_post_ffn_g, v_a_w_in, v_a_b_in, v_a_ln_g, v_a_ln_b, v_a_w_s, v_a_b_s, v_a_w_out, v_b_w_qkv, v_b_b_qkv, v_b_sinks, v_b_w_o, v_ffn_w_gu, v_ffn_w_down):
    weights = dict(pre_mix_g=pre_mix_g, post_mix_g=post_mix_g, pre_ffn_g=pre_ffn_g, post_ffn_g=post_ffn_g,
                   a_w_in=a_w_in, a_b_in=a_b_in, a_ln_g=a_ln_g, a_ln_b=a_ln_b, a_w_s=a_w_s, a_b_s=a_b_s,
                   a_w_out=a_w_out, b_w_qkv=b_w_qkv, b_b_qkv=b_b_qkv, b_sinks=b_sinks, b_w_o=b_w_o,
                   ffn_w_gu=ffn_w_gu, ffn_w_down=ffn_w_down)
    mom_m = dict(pre_mix_g=m_pre_mix_g, post_mix_g=m_post_mix_g, pre_ffn_g=m_pre_ffn_g, post_ffn_g=m_post_ffn_g,
                 a_w_in=m_a_w_in, a_b_in=m_a_b_in, a_ln_g=m_a_ln_g, a_ln_b=m_a_ln_b, a_w_s=m_a_w_s, a_b_s=m_a_b_s,
                 a_w_out=m_a_w_out, b_w_qkv=m_b_w_qkv, b_b_qkv=m_b_b_qkv, b_sinks=m_b_sinks, b_w_o=m_b_w_o,
                 ffn_w_gu=m_ffn_w_gu, ffn_w_down=m_ffn_w_down)
    mom_v = dict(pre_mix_g=v_pre_mix_g, post_mix_g=v_post_mix_g, pre_ffn_g=v_pre_ffn_g, post_ffn_g=v_post_ffn_g,
                 a_w_in=v_a_w_in, a_b_in=v_a_b_in, a_ln_g=v_a_ln_g, a_ln_b=v_a_ln_b, a_w_s=v_a_w_s, a_b_s=v_a_b_s,
                 a_w_out=v_a_w_out, b_w_qkv=v_b_w_qkv, b_b_qkv=v_b_b_qkv, b_sinks=v_b_sinks, b_w_o=v_b_w_o,
                 ffn_w_gu=v_ffn_w_gu, ffn_w_down=v_ffn_w_down)
    names = list(weights)
    column_sharded = ("a_w_in", "b_w_qkv", "ffn_w_gu")
    row_sharded = ("a_w_out", "b_w_o", "ffn_w_down")
    big = column_sharded + row_sharded

    T, D = x.shape[1], x.shape[2]
    depth = pre_mix_g.shape[0]
    n_heads = b_sinks.shape[1]
    q_width = n_heads * HEAD_DIM
    kv_width = N_KV_HEADS * HEAD_DIM
    G = a_w_s.shape[1]
    W = a_ln_g.shape[1]
    device = _index(_position())

    shards = [prep_weight(weights[k], k in column_sharded, "prep_" + k) for k in big]
    bias = b_b_qkv.reshape(b_b_qkv.shape[0], 1, 1, b_b_qkv.shape[1])
    gathered = allgather(shards + [bias], "gather_weights")
    full = {k: g.reshape(g.shape[0], N_DEV * g.shape[2], g.shape[3]) for k, g in zip(big, gathered)}
    b_qkv_full = gathered[-1].reshape(bias.shape[0], 1, -1)

    causal = jnp.tril(jnp.ones((CHUNK, CHUNK), dtype=bool))
    w_mask = jnp.where(causal[None, None], a_w_s, 0.0).astype(BF16)
    w_mask_t = jnp.swapaxes(w_mask, 2, 3)
    b_s_t = jnp.swapaxes(a_b_s, 1, 2)
    group_onehot = (jnp.arange(W)[:, None] // (W // G) == jnp.arange(LANES)[None, :]).astype(BF16)
    lane = jnp.arange(LANES) % HEAD_DIM
    inv_freq = ROPE_THETA ** (-jnp.arange(0, ROPE_DIM, 2, dtype=F32) / ROPE_DIM)
    inv_freq_row = jnp.where(lane < ROPE_DIM, inv_freq[lane % (ROPE_DIM // 2)], 0.0)[None, :].astype(F32)
    tabs = rope_tables(positions.reshape(T, 1), inv_freq_row, "rope_tables")

    saved = []
    h, hn = add_norm(x[0], None, None, pre_mix_g[0][None], "norm_first")
    for i in range(depth):
        j = i // 2
        s = dict(h=h, hn=hn)
        if i % 2 == 0:
            s["a"] = rowmm(hn, full["a_w_in"][j], True, a_b_in[j][None], BF16, "gmlp_in")
            s["gated"] = gmlp_fwd(s["a"], a_ln_g[j][None], a_ln_b[j][None], w_mask[j], b_s_t[j], "gmlp_gate")
            mix = rowmm(s["gated"], full["a_w_out"][j], False, None, F32, "gmlp_out")
        else:
            s["qkv"] = rowmm(hn, full["b_w_qkv"][j], True, b_qkv_full[j], BF16, "attn_qkv")
            s["qk"] = rope_fwd(s["qkv"], tabs, q_width, kv_width, "attn_rope")
            s["o"] = attn_fwd(s["qk"], s["qkv"], b_sinks[j], q_width, kv_width, "attn_core")
            mix = rowmm(s["o"], full["b_w_o"][j], False, None, F32, "attn_out")
        s["mix"] = mix
        s["h2"], s["fn"] = add_norm(h, mix, post_mix_g[i][None], pre_ffn_g[i][None], "norm_mid")
        s["gu"], s["act"] = ffn_up(s["fn"], full["ffn_w_gu"][i], "ffn_up")
        s["f"] = rowmm(s["act"], full["ffn_w_down"][i], False, None, F32, "ffn_down")
        saved.append(s)
        if i + 1 < depth:
            h, hn = add_norm(s["h2"], s["f"], post_ffn_g[i][None], pre_mix_g[i + 1][None], "norm_next")
    last = saved[-1]
    dh, loss_sum = add_norm_loss(last["h2"], last["f"], post_ffn_g[depth - 1][None], loss_target[0], "loss")
    loss = lax.psum(loss_sum[0, 0] * (0.5 / D), ("x", "y", "c"))

    small = {k: [None] * weights[k].shape[0] for k in names if k not in big}
    grads = {k: [None] * weights[k].shape[0] for k in big}
    r = norm_bwd(dh, None, (last["f"], post_ffn_g[depth - 1][None]), "norm_bwd_last")
    df, small["post_ffn_g"][depth - 1] = r["dy"], r["dg_post"]
    for i in reversed(range(depth)):
        j = i // 2
        s = saved[i]
        dgu = ffn_dact(df, full["ffn_w_down"][i], s["gu"], "ffn_dact")
        grads["ffn_w_down"][i] = grad_mm(s["act"], df, "ffn_down_grad")
        grads["ffn_w_gu"][i] = grad_mm(dgu, s["fn"], "ffn_up_grad")
        dfn = rowmm(dgu, full["ffn_w_gu"][i], False, None, F32, "ffn_dx")
        r = norm_bwd(dh, (dfn, s["h2"], pre_ffn_g[i][None]), (s["mix"], post_mix_g[i][None]), "norm_bwd_mid")
        dh, dmix = r["dh"], r["dy"]
        small["pre_ffn_g"][i], small["post_mix_g"][i] = r["dg_pre"], r["dg_post"]
        if i % 2 == 0:
            grads["a_w_out"][j] = grad_mm(s["gated"], dmix, "gmlp_out_grad")
            dgated = rowmm(dmix, full["a_w_out"][j], True, None, BF16, "gmlp_dgated")
            da, dws, dbs, dlg, dlb, dbin = gmlp_bwd(
                s["a"], dgated, a_ln_g[j][None], a_ln_b[j][None], w_mask[j], w_mask_t[j], b_s_t[j], group_onehot,
                "gmlp_gate_bwd")
            small["a_w_s"][j], small["a_b_s"][j] = dws, dbs[:, :G].T
            small["a_ln_g"][j], small["a_ln_b"][j], small["a_b_in"][j] = dlg[0], dlb[0], dbin[0]
            grads["a_w_in"][j] = grad_mm(da, s["hn"], "gmlp_in_grad")
            dhn = rowmm(da, full["a_w_in"][j], False, None, F32, "gmlp_dx")
        else:
            grads["b_w_o"][j] = grad_mm(s["o"], dmix, "attn_out_grad")
            do = rowmm(dmix, full["b_w_o"][j], True, None, BF16, "attn_do")
            dq, dk, dv, dsink = attn_bwd(s["qk"], s["qkv"], do, b_sinks[j], q_width, kv_width, "attn_core_bwd")
            dqkv, dbias = rope_bwd(dq, dk, dv, tabs, "attn_rope_bwd")
            small["b_sinks"][j], small["b_b_qkv"][j] = dsink[0, :n_heads], dbias[0]
            grads["b_w_qkv"][j] = grad_mm(dqkv, s["hn"], "attn_qkv_grad")
            dhn = rowmm(dqkv, full["b_w_qkv"][j], False, None, F32, "attn_dx")
        if i > 0:
            prev = saved[i - 1]
            r = norm_bwd(dh, (dhn, s["h"], pre_mix_g[i][None]), (prev["f"], post_ffn_g[i - 1][None]), "norm_bwd_next")
            dh, df = r["dh"], r["dy"]
            small["pre_mix_g"][i], small["post_ffn_g"][i - 1] = r["dg_pre"], r["dg_post"]
        else:
            r = norm_bwd(dh, (dhn, s["h"], pre_mix_g[0][None]), None, "norm_bwd_first")
            dh, small["pre_mix_g"][0] = r["dh"], r["dg_pre"]
    grad_x = dh[None]

    received = exchange([[_blocked(g) for g in grads[k]] for k in big], "exchange_grads")
    out_g, out_d, out_m, out_v = {}, {}, {}, {}
    for k, parts in zip(big, received):
        g = sum_parts(parts, k in column_sharded, "sum_" + k)
        shape = weights[k].shape
        flat = lambda t: t.reshape(shape[0] * shape[1], shape[2])
        d, mn, vn = adamw(flat(weights[k]), flat(g), flat(mom_m[k]), flat(mom_v[k]), "adamw_" + k)
        out_g[k], out_d[k], out_m[k], out_v[k] = g, d.reshape(shape), mn.reshape(shape), vn.reshape(shape)

    small_names = [k for k in names if k not in big]
    partial = {k: jnp.stack([p.reshape(-1) for p in small[k]]) for k in small_names}
    sizes = [partial[k].size for k in small_names]
    total = sum(sizes)
    padded = -(-total // (8 * LANES)) * (8 * LANES)
    pack = lambda vals: jnp.concatenate([v.reshape(-1) for v in vals] + [jnp.zeros((padded - total,), F32)])
    gathered_small = allgather([pack([partial[k] for k in small_names]).reshape(1, 1, padded // LANES, LANES)],
                               "gather_small_grads")[0]
    summed = sum_parts(gathered_small, False, "sum_small")[0].reshape(-1)
    offsets = [sum(sizes[:q]) for q in range(len(sizes))]
    g_small = {}
    for k, off, size in zip(small_names, offsets, sizes):
        g = summed[off:off + size]
        if k == "b_b_qkv":
            n_local = b_b_qkv.shape[1]
            g = lax.dynamic_slice_in_dim(g.reshape(b_b_qkv.shape[0], -1), device * n_local, n_local, axis=1)
        g_small[k] = g.reshape(weights[k].shape)
    local_total = sum(weights[k].size for k in small_names)
    local_padded = -(-local_total // (8 * LANES)) * (8 * LANES)
    pack_local = lambda src: jnp.concatenate(
        [src[k].reshape(-1) for k in small_names] + [jnp.ones((local_padded - local_total,), F32)]
    ).reshape(local_padded // LANES, LANES)
    d, mn, vn = adamw(pack_local(weights), pack_local(g_small), pack_local(mom_m), pack_local(mom_v), "adamw_small")
    off = 0
    for k in small_names:
        size, shape = weights[k].size, weights[k].shape
        cut = lambda t: t.reshape(-1)[off:off + size].reshape(shape)
        out_g[k], out_d[k], out_m[k], out_v[k] = g_small[k], cut(d), cut(mn), cut(vn)
        off += size

    return (loss, grad_x, *[out_g[k] for k in names], *[out_d[k] for k in names],
            *[out_m[k] for k in names], *[out_v[k] for k in names])
```

```python
import functools
import math

import jax
import jax.numpy as jnp
from jax import lax
from jax.experimental import pallas as pl
from jax.experimental.pallas import tpu as pltpu

F32, BF16 = jnp.float32, jnp.bfloat16
MESH = pl.DeviceIdType.MESH
N_DEV = 8

CHUNK = 128
HEAD_DIM = 64
N_KV_HEADS = 4
ROPE_DIM = HEAD_DIM // 4
ROPE_THETA = 500000.0
RMS_EPS = 1e-6
LN_EPS = 1e-5
NEG_INF = -1e30

ADAM_LR = 0.001
ADAM_B1 = 0.9
ADAM_B2 = 0.999
ADAM_EPS = 1e-08
ADAM_WD = 0.01
ADAM_STEP = 10

V7X_VMEM_BYTES = 64 * 2 ** 20
VMEM_LIMIT = V7X_VMEM_BYTES - 8 * 2 ** 20
LANES = 128


def _params(*sem):
    return pltpu.CompilerParams(dimension_semantics=sem or None, vmem_limit_bytes=VMEM_LIMIT)


def _tile(n, pref):
    if n <= pref:
        return n
    t = pref - pref % LANES
    while t >= LANES:
        if n % t == 0:
            return t
        t -= LANES
    return n


def _full(shape):
    return pl.BlockSpec(shape, lambda *_: (0,) * len(shape))


def _rows(tm, width):
    return pl.BlockSpec((tm, width), lambda i: (i, 0))


def _rms(x, g):
    r = lax.rsqrt(jnp.mean(x * x, axis=-1, keepdims=True) + RMS_EPS)
    return x * r * g


def _rms_bwd(x, g, dy):
    r = lax.rsqrt(jnp.mean(x * x, axis=-1, keepdims=True) + RMS_EPS)
    xhat = x * r
    dg = jnp.sum(dy * xhat, axis=0, keepdims=True)
    dxhat = dy * g
    dx = r * (dxhat - xhat * jnp.mean(dxhat * xhat, axis=-1, keepdims=True))
    return dx, dg


_INV_SQRT2 = 1.0 / math.sqrt(2.0)
_INV_SQRT2PI = 1.0 / math.sqrt(2.0 * math.pi)


def _gelu(x):
    return 0.5 * x * (1.0 + lax.erf(x * _INV_SQRT2))


def _gelu_grad(x):
    return 0.5 * (1.0 + lax.erf(x * _INV_SQRT2)) + x * jnp.exp(-0.5 * x * x) * _INV_SQRT2PI


def _sigmoid(x):
    return 1.0 / (1.0 + jnp.exp(-x))


def _position():
    return lax.axis_index("x"), lax.axis_index("y"), lax.axis_index("c")


def _index(p):
    return 4 * p[0] + 2 * p[1] + p[2]


def allgather(xs, name):
    n = len(xs)
    any_spec = pl.BlockSpec(memory_space=pl.ANY)

    def body(*refs):
        x_refs, o_refs = refs[:n], refs[n:2 * n]
        send_sems, recv_sems, local_sems = refs[2 * n:]
        x, y, c = _position()
        me, sibling = (x, y, c), (x, y, 1 - c)
        chips = [(1 - x, y), (x, 1 - y), (1 - x, 1 - y)]

        def slot(a, p):
            return o_refs[a].at[:, pl.ds(_index(p), 1)]

        def copy(a, k, block, to, src=None):
            return pltpu.make_async_remote_copy(
                src_ref=slot(a, block) if src is None else src, dst_ref=slot(a, block),
                send_sem=send_sems.at[7 * a + k], recv_sem=recv_sems.at[7 * a + k],
                device_id=to, device_id_type=MESH)

        mine = [pltpu.make_async_copy(x_refs[a], slot(a, me), local_sems.at[a]) for a in range(n)]
        for cp in mine:
            cp.start()
        first = []
        for a in range(n):
            first.append(copy(a, 0, me, sibling, src=x_refs[a]))
            first += [copy(a, 1 + j, me, (*chip, c), src=x_refs[a]) for j, chip in enumerate(chips)]
        for cp in first:
            cp.start()
        passed = []
        for j, chip in enumerate(chips):
            for a in range(n):
                copy(a, 1 + j, (*chip, c), me).wait_recv()
                cp = copy(a, 4 + j, (*chip, c), sibling)
                cp.start()
                passed.append(cp)
        for a in range(n):
            copy(a, 0, sibling, me).wait_recv()
            for j, chip in enumerate(chips):
                copy(a, 4 + j, (*chip, 1 - c), me).wait_recv()
        for cp in first + passed:
            cp.wait_send()
        for cp in mine:
            cp.wait()

    return pl.pallas_call(
        body, name=name,
        out_shape=[jax.ShapeDtypeStruct((v.shape[0], N_DEV) + v.shape[2:], v.dtype) for v in xs],
        in_specs=[any_spec] * n, out_specs=[any_spec] * n,
        scratch_shapes=[pltpu.SemaphoreType.DMA((7 * n,)), pltpu.SemaphoreType.DMA((7 * n,)),
                        pltpu.SemaphoreType.DMA((n,))],
    )(*xs)


_HBM = pl.BlockSpec(memory_space=pltpu.HBM)
_SEM = pl.BlockSpec(memory_space=pltpu.SEMAPHORE)
_ORDERED_BY_DATA = pltpu.SideEffectType.DATAFLOW_SIDE_EFFECTING


def _in_hbm(v):
    return pltpu.with_memory_space_constraint(v, pltpu.HBM)


def _peer(k):
    x, y, c = _position()
    return (x ^ ((k >> 2) & 1), y ^ ((k >> 1) & 1), c ^ (k & 1))


def _split_copies(srcs, dsts, send, recv):
    return [pltpu.make_async_remote_copy(
        src_ref=src(k), dst_ref=dst(k), send_sem=send.at[7 * a + k - 1], recv_sem=recv.at[7 * a + k - 1],
        device_id=_peer(k), device_id_type=MESH)
        for a, (src, dst) in enumerate(zip(srcs, dsts)) for k in range(1, N_DEV)]


def _start_call(groups, sent, copies_of, name):
    flat = [v for grp in groups for v in grp]
    n, ng = len(flat), len(groups)

    def body(*refs):
        bufs, sems = refs[:n], refs[n:n + 2 * ng]
        q = 0
        for g, grp in enumerate(groups):
            for cp in copies_of(bufs[q:q + len(grp)], sems[2 * g], sems[2 * g + 1]):
                cp.start()
            q += len(grp)

    sem_shapes = []
    for count in sent:
        sem_shapes += [pltpu.SemaphoreType.DMA((7 * count,)) for _ in range(2)]
    res = pl.pallas_call(
        body, name=name,
        out_shape=sem_shapes + [pltpu.HBM(v.shape, v.dtype) for v in flat],
        in_specs=[_HBM] * n, out_specs=[_SEM] * (2 * ng) + [_HBM] * n,
        input_output_aliases={i: 2 * ng + i for i in range(n)},
        compiler_params=pltpu.CompilerParams(has_side_effects=_ORDERED_BY_DATA),
    )(*[_in_hbm(v) for v in flat])
    sems = [(res[2 * g], res[2 * g + 1]) for g in range(ng)]
    return sems, list(res[2 * ng:])


def _wait_call(bufs, sems, copies_of, after, name):
    n = len(bufs)

    def body(*refs):
        for cp in copies_of(refs[:n], refs[n], refs[n + 1]):
            cp.wait_send()
            cp.wait_recv()

    return list(pl.pallas_call(
        body, name=name,
        out_shape=[pltpu.HBM(v.shape, v.dtype) for v in bufs],
        in_specs=[_HBM] * n + [_SEM, _SEM, pl.BlockSpec(memory_space=pl.ANY)], out_specs=[_HBM] * n,
        input_output_aliases={i: i for i in range(n)},
        compiler_params=pltpu.CompilerParams(has_side_effects=_ORDERED_BY_DATA),
    )(*bufs, sems[0], sems[1], after))


def _gather_copies(lands, send, recv):
    me = _index(_position())
    mine = [lambda k, ref=ref: ref.at[pl.ds(me, 1)] for ref in lands]
    return _split_copies(mine, mine, send, recv)


def _gather_arrivals(lands, send, recv):
    me = _index(_position())
    mine = [lambda k, ref=ref: ref.at[pl.ds(me, 1)] for ref in lands]
    theirs = [lambda k, ref=ref: ref.at[pl.ds(_index(_peer(k)), 1)] for ref in lands]
    return _split_copies(mine, theirs, send, recv)


def _exchange_copies(bufs, send, recv):
    half = len(bufs) // 2
    srcs = [lambda k, ref=ref: ref.at[pl.ds(_index(_peer(k)), 1)] for ref in bufs[:half]]
    dsts = [lambda k, ref=ref: ref.at[pl.ds(k - 1, 1)] for ref in bufs[half:]]
    return _split_copies(srcs, dsts, send, recv)


def gather_start(groups, name):
    return _start_call(groups, [len(grp) for grp in groups], _gather_copies, name)


def gather_wait(lands, sems, after, name):
    return _wait_call(lands, sems, _gather_arrivals, after, name)


def exchange_start(partials, name):
    lands = [lax.empty((N_DEV - 1,) + p.shape[1:], p.dtype) for p in partials]
    sems, bufs = _start_call([list(partials) + lands], [len(partials)], _exchange_copies, name)
    return sems[0], bufs


def exchange_wait(bufs, sems, after, name):
    bufs = _wait_call(bufs, sems, _exchange_copies, after, name)
    return bufs[:len(bufs) // 2], bufs[len(bufs) // 2:]


def prep_weight(w, layer, transpose, dev, name):
    L, K, n = w.shape
    block = (1, n, K) if transpose else (1, K, n)

    def body(d_ref, w_ref, o_ref):
        v = w_ref[0]
        o_ref[0] = (v.T if transpose else v).astype(BF16)

    return pl.pallas_call(
        body, name=name,
        grid_spec=pltpu.PrefetchScalarGridSpec(
            num_scalar_prefetch=1, grid=(1,),
            in_specs=[pl.BlockSpec((1, K, n), lambda i, d: (layer, 0, 0))],
            out_specs=pl.BlockSpec(block, lambda i, d: (d[0], 0, 0))),
        out_shape=jax.ShapeDtypeStruct((N_DEV,) + block[1:], BF16),
        compiler_params=_params("arbitrary"),
    )(dev, w)


def _adamw_math(w, g, m, v):
    mn = ADAM_B1 * m + (1.0 - ADAM_B1) * g
    vn = ADAM_B2 * v + (1.0 - ADAM_B2) * (g * g)
    m_hat = mn * (1.0 / (1.0 - ADAM_B1 ** ADAM_STEP))
    v_hat = vn * (1.0 / (1.0 - ADAM_B2 ** ADAM_STEP))
    return -ADAM_LR * (m_hat / (jnp.sqrt(v_hat) + ADAM_EPS) + ADAM_WD * w), mn, vn


def sum_adamw(partial, landed, w, m, v, prev, layer, transpose, dev, name):
    _, r, c = partial.shape
    if transpose:
        tc = _tile(c, 256)
        grid = (c // tc,)
        part_spec = pl.BlockSpec((1, r, tc), lambda i, d: (d[0], 0, i))
        land_spec = pl.BlockSpec((N_DEV - 1, r, tc), lambda i, d: (0, 0, i))
        w_spec = pl.BlockSpec((1, tc, r), lambda i, d: (layer, i, 0))
    else:
        tr = r // 2 if r % 32 == 0 else r
        grid = (r // tr,)
        part_spec = pl.BlockSpec((1, tr, c), lambda i, d: (d[0], i, 0))
        land_spec = pl.BlockSpec((N_DEV - 1, tr, c), lambda i, d: (0, i, 0))
        w_spec = pl.BlockSpec((1, tr, c), lambda i, d: (layer, i, 0))

    def body(d_ref, p_ref, l_ref, w_ref, m_ref, v_ref, *rest):
        g_ref, dl_ref, mo_ref, vo_ref = rest[4:]
        acc = p_ref[0].astype(F32)
        for k in range(N_DEV - 1):
            acc = acc + l_ref[k].astype(F32)
        g = acc.T if transpose else acc
        delta, mn, vn = _adamw_math(w_ref[0], g, m_ref[0], v_ref[0])
        g_ref[0], dl_ref[0], mo_ref[0], vo_ref[0] = g, delta, mn, vn

    return pl.pallas_call(
        body, name=name,
        grid_spec=pltpu.PrefetchScalarGridSpec(
            num_scalar_prefetch=1, grid=grid,
            in_specs=[part_spec, land_spec, w_spec, w_spec, w_spec] + [pl.BlockSpec(memory_space=pl.ANY)] * 4,
            out_specs=[w_spec] * 4),
        out_shape=[jax.ShapeDtypeStruct(w.shape, F32)] * 4,
        input_output_aliases={6 + q: q for q in range(4)},
        compiler_params=_params("parallel"),
    )(dev, partial, landed, w, m, v, *prev)


def sum_parts(parts, transpose, name):
    L, P, r, c = parts.shape
    tr = _tile(r, 256) if (not transpose and r % 8 == 0) else r
    if r % tr:
        tr = r
    out_block = (1, c, tr) if transpose else (1, tr, c)
    out_map = (lambda l, i: (l, 0, i)) if transpose else (lambda l, i: (l, i, 0))

    def body(p_ref, o_ref):
        acc = p_ref[0, 0].astype(F32)
        for s in range(1, P):
            acc = acc + p_ref[0, s].astype(F32)
        o_ref[0] = acc.T if transpose else acc

    return pl.pallas_call(
        body, name=name, grid=(L, r // tr),
        in_specs=[pl.BlockSpec((1, P, tr, c), lambda l, i: (l, 0, i, 0))],
        out_specs=pl.BlockSpec(out_block, out_map),
        out_shape=jax.ShapeDtypeStruct((L, c, r) if transpose else (L, r, c), F32),
        compiler_params=_params("parallel", "parallel"),
    )(parts)


def adamw(w, g, m, v, name):
    R, C = w.shape
    tr = _tile(R, 512)

    def body(w_ref, g_ref, m_ref, v_ref, d_ref, mo_ref, vo_ref):
        d_ref[...], mo_ref[...], vo_ref[...] = _adamw_math(w_ref[...], g_ref[...], m_ref[...], v_ref[...])

    spec = pl.BlockSpec((tr, C), lambda i: (i, 0))
    return pl.pallas_call(
        body, name=name, grid=(R // tr,),
        in_specs=[spec] * 4, out_specs=[spec] * 3,
        out_shape=[jax.ShapeDtypeStruct((R, C), F32)] * 3,
        compiler_params=_params("parallel"),
    )(w, g, m, v)


def add_norm(h, y, g_post, g_pre, name):
    T, D = h.shape
    tm = _tile(T, 512)
    has_y = y is not None

    def body(*refs):
        if has_y:
            h_ref, y_ref, gp_ref, g_ref, ho_ref, hn_ref = refs
            hv = h_ref[...] + _rms(y_ref[...], gp_ref[...])
            ho_ref[...] = hv
        else:
            h_ref, g_ref, hn_ref = refs
            hv = h_ref[...]
        hn_ref[...] = _rms(hv, g_ref[...]).astype(BF16)

    row, vec = _rows(tm, D), _full((1, D))
    if has_y:
        return pl.pallas_call(
            body, name=name, grid=(T // tm,), in_specs=[row, row, vec, vec], out_specs=[row, row],
            out_shape=[jax.ShapeDtypeStruct((T, D), F32), jax.ShapeDtypeStruct((T, D), BF16)],
            compiler_params=_params("parallel"))(h, y, g_post, g_pre)
    hn = pl.pallas_call(
        body, name=name, grid=(T // tm,), in_specs=[row, vec], out_specs=row,
        out_shape=jax.ShapeDtypeStruct((T, D), BF16), compiler_params=_params("parallel"))(h, g_pre)
    return h, hn


def add_norm_loss(h, y, g_post, target, name):
    T, D = h.shape
    tm = _tile(T, 512)

    def body(h_ref, y_ref, gp_ref, t_ref, dh_ref, loss_ref):
        @pl.when(pl.program_id(0) == 0)
        def _():
            loss_ref[...] = jnp.zeros_like(loss_ref)
        err = h_ref[...] + _rms(y_ref[...], gp_ref[...]) - t_ref[...]
        dh_ref[...] = err * (1.0 / D)
        loss_ref[...] += jnp.sum(err * err)

    row = _rows(tm, D)
    return pl.pallas_call(
        body, name=name, grid=(T // tm,), in_specs=[row, row, _full((1, D)), row],
        out_specs=[row, _full((8, LANES))],
        out_shape=[jax.ShapeDtypeStruct((T, D), F32), jax.ShapeDtypeStruct((8, LANES), F32)],
        compiler_params=_params("arbitrary"))(h, y, g_post, target)


def norm_bwd(dh_in, pre, post, name):
    T, D = dh_in.shape
    tm = _tile(T, 512)
    row, vec = _rows(tm, D), _full((1, D))
    ins, in_specs, outs, out_specs, names = [dh_in], [row], [], [], []
    if pre is not None:
        ins += list(pre)
        in_specs += [row, row, vec]
        outs += [jax.ShapeDtypeStruct((T, D), F32), jax.ShapeDtypeStruct((1, D), F32)]
        out_specs += [row, vec]
        names += ["dh", "dg_pre"]
    if post is not None:
        ins += list(post)
        in_specs += [row, vec]
        outs += [jax.ShapeDtypeStruct((T, D), BF16), jax.ShapeDtypeStruct((1, D), F32)]
        out_specs += [row, vec]
        names += ["dy", "dg_post"]

    def body(*refs):
        refs = list(refs)
        first = pl.program_id(0) == 0
        dh = refs.pop(0)[...]
        if pre is not None:
            dpre_ref, x_ref, g_ref = refs[:3]
            refs = refs[3:]
        if post is not None:
            y_ref, gp_ref = refs[:2]
            refs = refs[2:]
        if pre is not None:
            dh_ref, dg_ref = refs[:2]
            refs = refs[2:]
            dx, dg = _rms_bwd(x_ref[...], g_ref[...], dpre_ref[...].astype(F32))
            dh = dh + dx
            dh_ref[...] = dh

            @pl.when(first)
            def _():
                dg_ref[...] = jnp.zeros_like(dg_ref)
            dg_ref[...] += dg
        if post is not None:
            dy_ref, dgp_ref = refs[:2]
            dy, dgp = _rms_bwd(y_ref[...], gp_ref[...], dh)
            dy_ref[...] = dy.astype(BF16)

            @pl.when(first)
            def _():
                dgp_ref[...] = jnp.zeros_like(dgp_ref)
            dgp_ref[...] += dgp

    res = pl.pallas_call(
        body, name=name, grid=(T // tm,), in_specs=in_specs, out_specs=out_specs, out_shape=outs,
        compiler_params=_params("arbitrary"))(*ins)
    return dict(zip(names, res))


_NT = (((1,), (1,)), ((), ()))
_TN = (((0,), (0,)), ((), ()))


def rowmm(a, w, transposed_w, bias, out_dtype, name):
    M, K = a.shape
    N = w.shape[0] if transposed_w else w.shape[1]
    tm = _tile(M, 256)
    tn = _tile(N, 512)

    def body(*refs):
        a_ref, w_ref = refs[:2]
        o_ref = refs[-1]
        av = a_ref[...]
        for j in range(N // tn):
            cols = pl.ds(j * tn, tn)
            if transposed_w:
                acc = lax.dot_general(av, w_ref[cols, :], _NT, preferred_element_type=F32)
            else:
                acc = jnp.dot(av, w_ref[:, cols], preferred_element_type=F32)
            if bias is not None:
                acc = acc + refs[2][:, cols]
            o_ref[:, cols] = acc.astype(out_dtype)

    ins, in_specs = [a, w], [_rows(tm, K), _full(w.shape)]
    if bias is not None:
        ins.append(bias)
        in_specs.append(_full((1, N)))
    return pl.pallas_call(
        body, name=name, grid=(M // tm,), in_specs=in_specs, out_specs=_rows(tm, N),
        out_shape=jax.ShapeDtypeStruct((M, N), out_dtype), compiler_params=_params("parallel"))(*ins)


def grad_mm(a, b, name):
    T, N = a.shape
    K = b.shape[1]
    tn = _tile(N, 512)
    tt = _tile(T, 512)

    def body(a_ref, b_ref, o_ref, acc_ref):
        t = pl.program_id(1)

        @pl.when(t == 0)
        def _():
            acc_ref[...] = jnp.zeros_like(acc_ref)
        acc_ref[...] += lax.dot_general(a_ref[...], b_ref[...], _TN, preferred_element_type=F32)

        @pl.when(t == pl.num_programs(1) - 1)
        def _():
            o_ref[...] = acc_ref[...].astype(BF16)

    return pl.pallas_call(
        body, name=name, grid=(N // tn, T // tt),
        in_specs=[pl.BlockSpec((tt, tn), lambda j, t: (t, j)), pl.BlockSpec((tt, K), lambda j, t: (t, 0))],
        out_specs=pl.BlockSpec((tn, K), lambda j, t: (j, 0)),
        out_shape=jax.ShapeDtypeStruct((N, K), BF16),
        scratch_shapes=[pltpu.VMEM((tn, K), F32)],
        compiler_params=_params("parallel", "arbitrary"))(a, b)


def ffn_up(fn, w_gu_t, name):
    T, D = fn.shape
    F = w_gu_t.shape[0] // 2
    tm = _tile(T, 256)
    tn = _tile(F, 512)

    def body(a_ref, w_ref, gu_ref, act_ref):
        av = a_ref[...]
        for j in range(F // tn):
            g = lax.dot_general(av, w_ref[pl.ds(j * tn, tn), :], _NT, preferred_element_type=F32)
            up = lax.dot_general(av, w_ref[pl.ds(F + j * tn, tn), :], _NT, preferred_element_type=F32)
            gu_ref[:, pl.ds(j * tn, tn)] = g.astype(BF16)
            gu_ref[:, pl.ds(F + j * tn, tn)] = up.astype(BF16)
            act_ref[:, pl.ds(j * tn, tn)] = (g * _sigmoid(g) * up).astype(BF16)

    return pl.pallas_call(
        body, name=name, grid=(T // tm,), in_specs=[_rows(tm, D), _full(w_gu_t.shape)],
        out_specs=[_rows(tm, 2 * F), _rows(tm, F)],
        out_shape=[jax.ShapeDtypeStruct((T, 2 * F), BF16), jax.ShapeDtypeStruct((T, F), BF16)],
        compiler_params=_params("parallel"))(fn, w_gu_t)


def ffn_dact(df, w_down, gu, name):
    T, D = df.shape
    F = w_down.shape[0]
    tm = _tile(T, 256)
    tn = _tile(F, 512)

    def body(d_ref, w_ref, gu_ref, o_ref):
        dv = d_ref[...]
        for j in range(F // tn):
            dact = lax.dot_general(dv, w_ref[pl.ds(j * tn, tn), :], _NT, preferred_element_type=F32)
            g = gu_ref[:, pl.ds(j * tn, tn)].astype(F32)
            up = gu_ref[:, pl.ds(F + j * tn, tn)].astype(F32)
            sg = _sigmoid(g)
            o_ref[:, pl.ds(j * tn, tn)] = (dact * up * (sg * (1.0 + g * (1.0 - sg)))).astype(BF16)
            o_ref[:, pl.ds(F + j * tn, tn)] = (dact * g * sg).astype(BF16)

    return pl.pallas_call(
        body, name=name, grid=(T // tm,), in_specs=[_rows(tm, D), _full(w_down.shape), _rows(tm, 2 * F)],
        out_specs=_rows(tm, 2 * F), out_shape=jax.ShapeDtypeStruct((T, 2 * F), BF16),
        compiler_params=_params("parallel"))(df, w_down, gu)


def _layernorm_stats(v):
    mu = jnp.mean(v, axis=-1, keepdims=True)
    cen = v - mu
    rstd = lax.rsqrt(jnp.mean(cen * cen, axis=-1, keepdims=True) + LN_EPS)
    return cen * rstd, rstd


def gmlp_fwd(a, ln_g, ln_b, w_mask, b_s_t, name):
    T, W2 = a.shape
    W = W2 // 2
    G = w_mask.shape[0]
    C = W // G

    def body(a_ref, g_ref, b_ref, w_ref, bs_ref, o_ref):
        xhat, _ = _layernorm_stats(_gelu(a_ref[:, W:].astype(F32)))
        vln = (xhat * g_ref[...] + b_ref[...]).astype(BF16)
        for g in range(G):
            cols = pl.ds(g * C, C)
            sv = jnp.dot(w_ref[g], vln[:, g * C:(g + 1) * C], preferred_element_type=F32) + bs_ref[:, g:g + 1]
            o_ref[:, cols] = (_gelu(a_ref[:, cols].astype(F32)) * sv).astype(BF16)

    return pl.pallas_call(
        body, name=name, grid=(T // CHUNK,),
        in_specs=[_rows(CHUNK, W2), _full((1, W)), _full((1, W)), _full(w_mask.shape), _full(b_s_t.shape)],
        out_specs=_rows(CHUNK, W), out_shape=jax.ShapeDtypeStruct((T, W), BF16),
        compiler_params=_params("parallel"))(a, ln_g, ln_b, w_mask, b_s_t)


def gmlp_bwd(a, dgated, ln_g, ln_b, w_mask, w_mask_t, b_s_t, group_onehot, name):
    T, W2 = a.shape
    W = W2 // 2
    G = w_mask.shape[0]
    C = W // G

    def body(a_ref, dg_ref, g_ref, b_ref, w_ref, wt_ref, bs_ref, e_ref,
             da_ref, dws_ref, dbs_ref, dlg_ref, dlb_ref, dbin_ref, dvln_ref):
        @pl.when(pl.program_id(0) == 0)
        def _():
            for r in (dws_ref, dbs_ref, dlg_ref, dlb_ref, dbin_ref):
                r[...] = jnp.zeros_like(r)
        row = lax.broadcasted_iota(jnp.int32, (CHUNK, CHUNK), 0)
        col = lax.broadcasted_iota(jnp.int32, (CHUNK, CHUNK), 1)
        causal = col <= row
        av = a_ref[:, W:].astype(F32)
        xhat, rstd = _layernorm_stats(_gelu(av))
        vln = (xhat * g_ref[...] + b_ref[...]).astype(BF16)
        dsv_all = (dg_ref[...].astype(F32) * _gelu(a_ref[:, :W].astype(F32))).astype(BF16)
        dbs_ref[...] += jnp.dot(dsv_all, e_ref[...], preferred_element_type=F32)
        for g in range(G):
            cols = pl.ds(g * C, C)
            vg = vln[:, g * C:(g + 1) * C]
            dsv = dsv_all[:, g * C:(g + 1) * C]
            sv = jnp.dot(w_ref[g], vg, preferred_element_type=F32) + bs_ref[:, g:g + 1]
            au = a_ref[:, cols].astype(F32)
            dau = dg_ref[:, cols].astype(F32) * sv * _gelu_grad(au)
            da_ref[:, cols] = dau.astype(BF16)
            dbin_ref[:, cols] += jnp.sum(dau, axis=0, keepdims=True)
            dws_ref[g] += jnp.where(causal, lax.dot_general(dsv, vg, _NT, preferred_element_type=F32), 0.0)
            dvln_ref[:, cols] = jnp.dot(wt_ref[g], dsv, preferred_element_type=F32)
        dvln = dvln_ref[...]
        dlg_ref[...] += jnp.sum(dvln * xhat, axis=0, keepdims=True)
        dlb_ref[...] += jnp.sum(dvln, axis=0, keepdims=True)
        dxhat = dvln * g_ref[...]
        dv = rstd * (dxhat - jnp.mean(dxhat, axis=-1, keepdims=True)
                     - xhat * jnp.mean(dxhat * xhat, axis=-1, keepdims=True))
        dav = dv * _gelu_grad(av)
        da_ref[:, W:] = dav.astype(BF16)
        dbin_ref[:, W:] += jnp.sum(dav, axis=0, keepdims=True)

    return pl.pallas_call(
        body, name=name, grid=(T // CHUNK,),
        in_specs=[_rows(CHUNK, W2), _rows(CHUNK, W), _full((1, W)), _full((1, W)), _full(w_mask.shape),
                  _full(w_mask_t.shape), _full(b_s_t.shape), _full(group_onehot.shape)],
        out_specs=[_rows(CHUNK, W2), _full((G, CHUNK, CHUNK)), _full((CHUNK, LANES)), _full((1, W)),
                   _full((1, W)), _full((1, W2))],
        out_shape=[jax.ShapeDtypeStruct((T, W2), BF16), jax.ShapeDtypeStruct((G, CHUNK, CHUNK), F32),
                   jax.ShapeDtypeStruct((CHUNK, LANES), F32), jax.ShapeDtypeStruct((1, W), F32),
                   jax.ShapeDtypeStruct((1, W), F32), jax.ShapeDtypeStruct((1, W2), F32)],
        scratch_shapes=[pltpu.VMEM((CHUNK, W), F32)],
        compiler_params=_params("arbitrary"))(a, dgated, ln_g, ln_b, w_mask, w_mask_t, b_s_t, group_onehot)


def rope_tables(pos, inv_freq_row, name):
    T = pos.shape[0]
    tm = _tile(T, 512)

    def body(p_ref, f_ref, c_ref, s1_ref, s2_ref):
        ang = p_ref[...].astype(F32) * f_ref[...]
        lane = lax.broadcasted_iota(jnp.int32, (tm, LANES), 1) % HEAD_DIM
        sin = jnp.sin(ang)
        c_ref[...] = jnp.cos(ang)
        s1_ref[...] = jnp.where(lane < _HALF, -sin, 0.0)
        s2_ref[...] = jnp.where((lane >= _HALF) & (lane < ROPE_DIM), sin, 0.0)

    tab = _rows(tm, LANES)
    return pl.pallas_call(
        body, name=name, grid=(T // tm,), in_specs=[_rows(tm, 1), _full((1, LANES))], out_specs=[tab] * 3,
        out_shape=[jax.ShapeDtypeStruct((T, LANES), F32)] * 3, compiler_params=_params("parallel"))(pos, inv_freq_row)


_HALF = ROPE_DIM // 2


def _slabs(x):
    return [x[:, b * LANES:(b + 1) * LANES] for b in range(x.shape[1] // LANES)]


def _rotate(x, c, s1, s2):
    return [xs * c + pltpu.roll(xs, LANES - _HALF, 1) * s1 + pltpu.roll(xs, _HALF, 1) * s2 for xs in _slabs(x)]


def _rotate_transposed(dy, c, s1, s2):
    return [ds * c + pltpu.roll(ds * s1, _HALF, 1) + pltpu.roll(ds * s2, LANES - _HALF, 1) for ds in _slabs(dy)]


def rope_fwd(qkv, tabs, q_width, kv_width, name):
    T, QKV = qkv.shape
    tm = _tile(T, 512)
    QK = q_width + kv_width
    scale = HEAD_DIM ** -0.5

    def body(x_ref, c_ref, s1_ref, s2_ref, o_ref):
        x = x_ref[:, :QK].astype(F32)
        slabs = _rotate(x, c_ref[...], s1_ref[...], s2_ref[...])
        for b, y in enumerate(slabs):
            if b * LANES < q_width:
                y = y * scale
            o_ref[:, b * LANES:(b + 1) * LANES] = y.astype(BF16)

    tab = _rows(tm, LANES)
    return pl.pallas_call(
        body, name=name, grid=(T // tm,), in_specs=[_rows(tm, QKV), tab, tab, tab], out_specs=_rows(tm, QK),
        out_shape=jax.ShapeDtypeStruct((T, QK), BF16), compiler_params=_params("parallel"))(qkv, *tabs)


def rope_bwd(dq, dk, dv, tabs, name):
    T, Q = dq.shape
    KV = dk.shape[1]
    tm = _tile(T, 512)
    scale = HEAD_DIM ** -0.5

    def body(dq_ref, dk_ref, dv_ref, c_ref, s1_ref, s2_ref, o_ref, b_ref):
        @pl.when(pl.program_id(0) == 0)
        def _():
            b_ref[...] = jnp.zeros_like(b_ref)
        tabs_v = (c_ref[...], s1_ref[...], s2_ref[...])
        pieces = [s * scale for s in _rotate_transposed(dq_ref[...], *tabs_v)]
        pieces += _rotate_transposed(dk_ref[...], *tabs_v)
        pieces += _slabs(dv_ref[...])
        for b, y in enumerate(pieces):
            cols = pl.ds(b * LANES, LANES)
            o_ref[:, cols] = y.astype(BF16)
            b_ref[:, cols] += jnp.sum(y, axis=0, keepdims=True)

    tab = _rows(tm, LANES)
    return pl.pallas_call(
        body, name=name, grid=(T // tm,), in_specs=[_rows(tm, Q), _rows(tm, KV), _rows(tm, KV), tab, tab, tab],
        out_specs=[_rows(tm, Q + 2 * KV), _full((1, Q + 2 * KV))],
        out_shape=[jax.ShapeDtypeStruct((T, Q + 2 * KV), BF16), jax.ShapeDtypeStruct((1, Q + 2 * KV), F32)],
        compiler_params=_params("arbitrary"))(dq, dk, dv, *tabs)


def _attn_block(n, q_ref, kc_ref, kp_ref, vc_ref, vp_ref, sink_ref, kh, group):
    GR = group * CHUNK
    lanes = slice(kh * HEAD_DIM, (kh + 1) * HEAD_DIM)
    kk = jnp.concatenate([kp_ref[:, lanes], kc_ref[:, lanes]], axis=0)
    vv = jnp.concatenate([vp_ref[:, lanes], vc_ref[:, lanes]], axis=0)
    qg = jnp.concatenate(
        [q_ref[:, (kh * group + g) * HEAD_DIM:(kh * group + g + 1) * HEAD_DIM] for g in range(group)], axis=0)
    s = lax.dot_general(qg, kk, _NT, preferred_element_type=F32)
    qi = lax.broadcasted_iota(jnp.int32, (GR, 2 * CHUNK), 0) % CHUNK
    sj = lax.broadcasted_iota(jnp.int32, (GR, 2 * CHUNK), 1)
    valid = (sj > qi) & (sj <= qi + CHUNK) & ((n > 0) | (sj >= CHUNK))
    s = jnp.where(valid, s, NEG_INF)
    head = lax.broadcasted_iota(jnp.int32, (GR, 1), 0) // CHUNK
    sink = jnp.zeros((GR, 1), F32)
    for g in range(group):
        sink = jnp.where(head == g, sink_ref[kh * group + g], sink)
    m = jnp.maximum(jnp.max(s, axis=-1, keepdims=True), sink)
    p = jnp.exp(s - m)
    e_sink = jnp.exp(sink - m)
    inv = 1.0 / (jnp.sum(p, axis=-1, keepdims=True) + e_sink)
    return qg, kk, vv, p * inv, e_sink * inv


def _attn_specs(q_width, kv_width, order):
    qb = q_width // kv_width
    prev = lambda i: jnp.maximum(order(i) - 1, 0)
    return [pl.BlockSpec((CHUNK, q_width), lambda i: (order(i), 0)),
            pl.BlockSpec((CHUNK, kv_width), lambda i: (order(i), qb)),
            pl.BlockSpec((CHUNK, kv_width), lambda i: (prev(i), qb)),
            pl.BlockSpec((CHUNK, kv_width), lambda i: (order(i), qb + 1)),
            pl.BlockSpec((CHUNK, kv_width), lambda i: (prev(i), qb + 1))]


def attn_fwd(qk, qkv, sinks, q_width, kv_width, name):
    T = qk.shape[0]
    group = q_width // kv_width

    def body(q_ref, kc_ref, kp_ref, vc_ref, vp_ref, sink_ref, o_ref):
        n = pl.program_id(0)
        for kh in range(kv_width // HEAD_DIM):
            _, _, vv, p, _ = _attn_block(n, q_ref, kc_ref, kp_ref, vc_ref, vp_ref, sink_ref, kh, group)
            og = jnp.dot(p.astype(BF16), vv, preferred_element_type=F32)
            for g in range(group):
                h = kh * group + g
                o_ref[:, h * HEAD_DIM:(h + 1) * HEAD_DIM] = og[g * CHUNK:(g + 1) * CHUNK].astype(BF16)

    specs = _attn_specs(q_width, kv_width, lambda i: i)
    return pl.pallas_call(
        body, name=name, grid=(T // CHUNK,),
        in_specs=specs + [pl.BlockSpec(memory_space=pltpu.SMEM)],
        out_specs=_rows(CHUNK, q_width), out_shape=jax.ShapeDtypeStruct((T, q_width), BF16),
        compiler_params=_params("parallel"))(qk, qk, qk, qkv, qkv, sinks)


def attn_bwd(qk, qkv, do, sinks, q_width, kv_width, name):
    T = qk.shape[0]
    NB = T // CHUNK
    group = q_width // kv_width

    def body(q_ref, kc_ref, kp_ref, vc_ref, vp_ref, do_ref, sink_ref, dq_ref, dk_ref, dv_ref, ds_ref, ck_ref, cv_ref):
        i = pl.program_id(0)
        n = NB - 1 - i

        @pl.when(i == 0)
        def _():
            ck_ref[...] = jnp.zeros_like(ck_ref)
            cv_ref[...] = jnp.zeros_like(cv_ref)
            ds_ref[...] = jnp.zeros_like(ds_ref)
        lane = lax.broadcasted_iota(jnp.int32, (1, LANES), 1)
        dsink_row = jnp.zeros((1, LANES), F32)
        for kh in range(kv_width // HEAD_DIM):
            qg, kk, vv, p, p_sink = _attn_block(n, q_ref, kc_ref, kp_ref, vc_ref, vp_ref, sink_ref, kh, group)
            dog = jnp.concatenate(
                [do_ref[:, (kh * group + g) * HEAD_DIM:(kh * group + g + 1) * HEAD_DIM] for g in range(group)], axis=0)
            dp = lax.dot_general(dog, vv, _NT, preferred_element_type=F32)
            delta = jnp.sum(p * dp, axis=-1, keepdims=True)
            ds = (p * (dp - delta)).astype(BF16)
            dsink = -p_sink * delta
            for g in range(group):
                h = kh * group + g
                dsink_row = dsink_row + jnp.where(lane == h, jnp.sum(dsink[g * CHUNK:(g + 1) * CHUNK]), 0.0)
            dqg = jnp.dot(ds, kk, preferred_element_type=F32)
            for g in range(group):
                h = kh * group + g
                dq_ref[:, h * HEAD_DIM:(h + 1) * HEAD_DIM] = dqg[g * CHUNK:(g + 1) * CHUNK]
            dkk = lax.dot_general(ds, qg, _TN, preferred_element_type=F32)
            dvv = lax.dot_general(p.astype(BF16), dog, _TN, preferred_element_type=F32)
            lanes = slice(kh * HEAD_DIM, (kh + 1) * HEAD_DIM)
            dk_ref[:, lanes] = dkk[CHUNK:] + ck_ref[:, lanes]
            dv_ref[:, lanes] = dvv[CHUNK:] + cv_ref[:, lanes]
            ck_ref[:, lanes] = dkk[:CHUNK]
            cv_ref[:, lanes] = dvv[:CHUNK]
        ds_ref[0:1, :] += dsink_row

    order = lambda i: NB - 1 - i
    specs = _attn_specs(q_width, kv_width, order)
    kv_out = pl.BlockSpec((CHUNK, kv_width), lambda i: (order(i), 0))
    q_rows = pl.BlockSpec((CHUNK, q_width), lambda i: (order(i), 0))
    return pl.pallas_call(
        body, name=name, grid=(NB,),
        in_specs=specs + [q_rows, pl.BlockSpec(memory_space=pltpu.SMEM)],
        out_specs=[q_rows, kv_out, kv_out, _full((8, LANES))],
        out_shape=[jax.ShapeDtypeStruct((T, q_width), F32), jax.ShapeDtypeStruct((T, kv_width), F32),
                   jax.ShapeDtypeStruct((T, kv_width), F32), jax.ShapeDtypeStruct((8, LANES), F32)],
        scratch_shapes=[pltpu.VMEM((CHUNK, kv_width), F32), pltpu.VMEM((CHUNK, kv_width), F32)],
        compiler_params=_params("arbitrary"))(qk, qk, qk, qkv, qkv, do, sinks)


def _blocked(w):
    return w.reshape(N_DEV, w.shape[0] // N_DEV, w.shape[1])


def kernel(x, positions, pre_mix_g, post_mix_g, pre_ffn_g, post_ffn_g, a_w_in, a_b_in, a_ln_g, a_ln_b, a_w_s, a_b_s, a_w_out, b_w_qkv, b_b_qkv, b_sinks, b_w_o, ffn_w_gu, ffn_w_down, loss_target, m_pre_mix_g, m_post_mix_g, m_pre_ffn_g, m_post_ffn_g, m_a_w_in, m_a_b_in, m_a_ln_g, m_a_ln_b, m_a_w_s, m_a_b_s, m_a_w_out, m_b_w_qkv, m_b_b_qkv, m_b_sinks, m_b_w_o, m_ffn_w_gu, m_ffn_w_down, v_pre_mix_g, v_post_mix_g, v_pre_ffn_g, v_post_ffn_g, v_a_w_in, v_a_b_in, v_a_ln_g, v_a_ln_b, v_a_w_s, v_a_b_s, v_a_w_out, v_b_w_qkv, v_b_b_qkv, v_b_sinks, v_b_w_o, v_ffn_w_gu, v_ffn_w_down):
    weights = dict(pre_mix_g=pre_mix_g, post_mix_g=post_mix_g, pre_ffn_g=pre_ffn_g, post_ffn_g=post_ffn_g,
                   a_w_in=a_w_in, a_b_in=a_b_in, a_ln_g=a_ln_g, a_ln_b=a_ln_b, a_w_s=a_w_s, a_b_s=a_b_s,
                   a_w_out=a_w_out, b_w_qkv=b_w_qkv, b_b_qkv=b_b_qkv, b_sinks=b_sinks, b_w_o=b_w_o,
                   ffn_w_gu=ffn_w_gu, ffn_w_down=ffn_w_down)
    mom_m = dict(pre_mix_g=m_pre_mix_g, post_mix_g=m_post_mix_g, pre_ffn_g=m_pre_ffn_g, post_ffn_g=m_post_ffn_g,
                 a_w_in=m_a_w_in, a_b_in=m_a_b_in, a_ln_g=m_a_ln_g, a_ln_b=m_a_ln_b, a_w_s=m_a_w_s, a_b_s=m_a_b_s,
                 a_w_out=m_a_w_out, b_w_qkv=m_b_w_qkv, b_b_qkv=m_b_b_qkv, b_sinks=m_b_sinks, b_w_o=m_b_w_o,
                 ffn_w_gu=m_ffn_w_gu, ffn_w_down=m_ffn_w_down)
    mom_v = dict(pre_mix_g=v_pre_mix_g, post_mix_g=v_post_mix_g, pre_ffn_g=v_pre_ffn_g, post_ffn_g=v_post_ffn_g,
                 a_w_in=v_a_w_in, a_b_in=v_a_b_in, a_ln_g=v_a_ln_g, a_ln_b=v_a_ln_b, a_w_s=v_a_w_s, a_b_s=v_a_b_s,
                 a_w_out=v_a_w_out, b_w_qkv=v_b_w_qkv, b_b_qkv=v_b_b_qkv, b_sinks=v_b_sinks, b_w_o=v_b_w_o,
                 ffn_w_gu=v_ffn_w_gu, ffn_w_down=v_ffn_w_down)
    names = list(weights)
    column_sharded = ("a_w_in", "b_w_qkv", "ffn_w_gu")
    row_sharded = ("a_w_out", "b_w_o", "ffn_w_down")
    big = column_sharded + row_sharded

    T, D = x.shape[1], x.shape[2]
    depth = pre_mix_g.shape[0]
    n_heads = b_sinks.shape[1]
    q_width = n_heads * HEAD_DIM
    kv_width = N_KV_HEADS * HEAD_DIM
    G = a_w_s.shape[1]
    W = a_ln_g.shape[1]
    device = _index(_position())

    dev = device.reshape(1).astype(jnp.int32)

    def layer_keys(i):
        mixer = ("a_w_in", "a_w_out") if i % 2 == 0 else ("b_w_qkv", "b_w_o")
        return [(k, i // 2) for k in mixer] + [("ffn_w_gu", i), ("ffn_w_down", i)]

    lands = [[prep_weight(weights[k], l, k in column_sharded, dev, "prep_" + k) for k, l in layer_keys(i)]
             for i in range(depth)]
    bias_land = lax.dynamic_update_slice(
        jnp.zeros((N_DEV,) + b_b_qkv.shape, F32), b_b_qkv[None], (device, 0, 0))
    lands[1].append(bias_land)
    gather_sems, flat_lands = gather_start(lands, "gather_start")
    lands = [[flat_lands.pop(0) for _ in grp] for grp in lands]

    def weights_of(i, after):
        got = gather_wait(lands[i], gather_sems[i], after, f"gather_wait_{i}")
        mats = {k: g.reshape(N_DEV * g.shape[1], g.shape[2]) for (k, _), g in zip(layer_keys(i), got)}
        if i == 1:
            mats["b_b_qkv"] = jnp.swapaxes(got[-1], 0, 1).reshape(b_b_qkv.shape[0], 1, -1)
        return mats

    causal = jnp.tril(jnp.ones((CHUNK, CHUNK), dtype=bool))
    w_mask = jnp.where(causal[None, None], a_w_s, 0.0).astype(BF16)
    w_mask_t = jnp.swapaxes(w_mask, 2, 3)
    b_s_t = jnp.swapaxes(a_b_s, 1, 2)
    group_onehot = (jnp.arange(W)[:, None] // (W // G) == jnp.arange(LANES)[None, :]).astype(BF16)
    lane = jnp.arange(LANES) % HEAD_DIM
    inv_freq = ROPE_THETA ** (-jnp.arange(0, ROPE_DIM, 2, dtype=F32) / ROPE_DIM)
    inv_freq_row = jnp.where(lane < ROPE_DIM, inv_freq[lane % (ROPE_DIM // 2)], 0.0)[None, :].astype(F32)
    tabs = rope_tables(positions.reshape(T, 1), inv_freq_row, "rope_tables")

    saved, full, b_qkv_full = [], [], None
    h, hn = add_norm(x[0], None, None, pre_mix_g[0][None], "norm_first")
    for i in range(depth):
        j = i // 2
        s = dict(h=h, hn=hn)
        wl = weights_of(i, hn if i == 0 else saved[-1]["h2"])
        full.append(wl)
        if i % 2 == 0:
            s["a"] = rowmm(hn, wl["a_w_in"], True, a_b_in[j][None], BF16, "gmlp_in")
            s["gated"] = gmlp_fwd(s["a"], a_ln_g[j][None], a_ln_b[j][None], w_mask[j], b_s_t[j], "gmlp_gate")
            mix = rowmm(s["gated"], wl["a_w_out"], False, None, F32, "gmlp_out")
        else:
            b_qkv_full = wl.get("b_b_qkv", b_qkv_full)
            s["qkv"] = rowmm(hn, wl["b_w_qkv"], True, b_qkv_full[j], BF16, "attn_qkv")
            s["qk"] = rope_fwd(s["qkv"], tabs, q_width, kv_width, "attn_rope")
            s["o"] = attn_fwd(s["qk"], s["qkv"], b_sinks[j], q_width, kv_width, "attn_core")
            mix = rowmm(s["o"], wl["b_w_o"], False, None, F32, "attn_out")
        s["mix"] = mix
        s["h2"], s["fn"] = add_norm(h, mix, post_mix_g[i][None], pre_ffn_g[i][None], "norm_mid")
        s["gu"], s["act"] = ffn_up(s["fn"], wl["ffn_w_gu"], "ffn_up")
        s["f"] = rowmm(s["act"], wl["ffn_w_down"], False, None, F32, "ffn_down")
        saved.append(s)
        if i + 1 < depth:
            h, hn = add_norm(s["h2"], s["f"], post_ffn_g[i][None], pre_mix_g[i + 1][None], "norm_next")
    last = saved[-1]
    dh, loss_sum = add_norm_loss(last["h2"], last["f"], post_ffn_g[depth - 1][None], loss_target[0], "loss")
    loss = lax.psum(loss_sum[0, 0] * (0.5 / D), ("x", "y", "c"))

    small = {k: [None] * weights[k].shape[0] for k in names if k not in big}
    grads = {k: [None] * weights[k].shape[0] for k in big}
    in_flight = []

    def send_grads(keys, tag):
        sems, bufs = exchange_start([_blocked(grads[k][l]) for k, l in keys], "exchange_start_" + tag)
        in_flight.append((keys, sems, bufs, tag))

    r = norm_bwd(dh, None, (last["f"], post_ffn_g[depth - 1][None]), "norm_bwd_last")
    df, small["post_ffn_g"][depth - 1] = r["dy"], r["dg_post"]
    for i in reversed(range(depth)):
        j = i // 2
        s = saved[i]
        dgu = ffn_dact(df, full[i]["ffn_w_down"], s["gu"], "ffn_dact")
        grads["ffn_w_down"][i] = grad_mm(s["act"], df, "ffn_down_grad")
        grads["ffn_w_gu"][i] = grad_mm(dgu, s["fn"], "ffn_up_grad")
        send_grads(layer_keys(i)[2:], f"ffn_{i}")
        dfn = rowmm(dgu, full[i]["ffn_w_gu"], False, None, F32, "ffn_dx")
        r = norm_bwd(dh, (dfn, s["h2"], pre_ffn_g[i][None]), (s["mix"], post_mix_g[i][None]), "norm_bwd_mid")
        dh, dmix = r["dh"], r["dy"]
        small["pre_ffn_g"][i], small["post_mix_g"][i] = r["dg_pre"], r["dg_post"]
        if i % 2 == 0:
            grads["a_w_out"][j] = grad_mm(s["gated"], dmix, "gmlp_out_grad")
            dgated = rowmm(dmix, full[i]["a_w_out"], True, None, BF16, "gmlp_dgated")
            da, dws, dbs, dlg, dlb, dbin = gmlp_bwd(
                s["a"], dgated, a_ln_g[j][None], a_ln_b[j][None], w_mask[j], w_mask_t[j], b_s_t[j], group_onehot,
                "gmlp_gate_bwd")
            small["a_w_s"][j], small["a_b_s"][j] = dws, dbs[:, :G].T
            small["a_ln_g"][j], small["a_ln_b"][j], small["a_b_in"][j] = dlg[0], dlb[0], dbin[0]
            grads["a_w_in"][j] = grad_mm(da, s["hn"], "gmlp_in_grad")
            dhn = rowmm(da, full[i]["a_w_in"], False, None, F32, "gmlp_dx")
        else:
            grads["b_w_o"][j] = grad_mm(s["o"], dmix, "attn_out_grad")
            do = rowmm(dmix, full[i]["b_w_o"], True, None, BF16, "attn_do")
            dq, dk, dv, dsink = attn_bwd(s["qk"], s["qkv"], do, b_sinks[j], q_width, kv_width, "attn_core_bwd")
            dqkv, dbias = rope_bwd(dq, dk, dv, tabs, "attn_rope_bwd")
            small["b_sinks"][j], small["b_b_qkv"][j] = dsink[0, :n_heads], dbias[0]
            grads["b_w_qkv"][j] = grad_mm(dqkv, s["hn"], "attn_qkv_grad")
            dhn = rowmm(dqkv, full[i]["b_w_qkv"], False, None, F32, "attn_dx")
        send_grads(layer_keys(i)[:2], f"mixer_{i}")
        if i > 0:
            prev = saved[i - 1]
            r = norm_bwd(dh, (dhn, s["h"], pre_mix_g[i][None]), (prev["f"], post_ffn_g[i - 1][None]), "norm_bwd_next")
            dh, df = r["dh"], r["dy"]
            small["pre_mix_g"][i], small["post_ffn_g"][i - 1] = r["dg_pre"], r["dg_post"]
        else:
            r = norm_bwd(dh, (dhn, s["h"], pre_mix_g[0][None]), None, "norm_bwd_first")
            dh, small["pre_mix_g"][0] = r["dh"], r["dg_pre"]
    grad_x = dh[None]

    stacked = {k: [lax.empty(weights[k].shape, F32) for _ in range(4)] for k in big}
    for keys, sems, bufs, tag in in_flight:
        partials, landed = exchange_wait(bufs, sems, dh, "exchange_wait_" + tag)
        for (k, l), part, land in zip(keys, partials, landed):
            stacked[k] = sum_adamw(part, land, weights[k], mom_m[k], mom_v[k], stacked[k], l,
                                   k in column_sharded, dev, "adamw_" + k)
    out_g, out_d, out_m, out_v = {}, {}, {}, {}
    for k in big:
        out_g[k], out_d[k], out_m[k], out_v[k] = stacked[k]

    small_names = [k for k in names if k not in big]
    partial = {k: jnp.stack([p.reshape(-1) for p in small[k]]) for k in small_names}
    sizes = [partial[k].size for k in small_names]
    total = sum(sizes)
    padded = -(-total // (LANES * LANES)) * (LANES * LANES)
    pack = lambda vals: jnp.concatenate([v.reshape(-1) for v in vals] + [jnp.zeros((padded - total,), F32)])
    gathered_small = allgather([pack([partial[k] for k in small_names]).reshape(1, 1, padded // LANES, LANES)],
                               "gather_small_grads")[0]
    summed = sum_parts(gathered_small, False, "sum_small")[0].reshape(-1)
    offsets = [sum(sizes[:q]) for q in range(len(sizes))]
    g_small = {}
    for k, off, size in zip(small_names, offsets, sizes):
        g = summed[off:off + size]
        if k == "b_b_qkv":
            n_local = b_b_qkv.shape[1]
            g = lax.dynamic_slice_in_dim(g.reshape(b_b_qkv.shape[0], -1), device * n_local, n_local, axis=1)
        g_small[k] = g.reshape(weights[k].shape)
    local_total = sum(weights[k].size for k in small_names)
    local_padded = -(-local_total // (LANES * LANES)) * (LANES * LANES)
    pack_local = lambda src: jnp.concatenate(
        [src[k].reshape(-1) for k in small_names] + [jnp.ones((local_padded - local_total,), F32)]
    ).reshape(local_padded // LANES, LANES)
    d, mn, vn = adamw(pack_local(weights), pack_local(g_small), pack_local(mom_m), pack_local(mom_v), "adamw_small")
    off = 0
    for k in small_names:
        size, shape = weights[k].size, weights[k].shape
        cut = lambda t: t.reshape(-1)[off:off + size].reshape(shape)
        out_g[k], out_d[k], out_m[k], out_v[k] = g_small[k], cut(d), cut(mn), cut(vn)
        off += size

    return (loss, grad_x, *[out_g[k] for k in names], *[out_d[k] for k in names],
            *[out_m[k] for k in names], *[out_v[k] for k in names])
```

```python
import functools
import math

import jax
import jax.numpy as jnp
from jax import lax
from jax.experimental import pallas as pl
from jax.experimental.pallas import tpu as pltpu

F32, BF16 = jnp.float32, jnp.bfloat16
MESH = pl.DeviceIdType.MESH
N_DEV = 8

CHUNK = 128
HEAD_DIM = 64
N_KV_HEADS = 4
ROPE_DIM = HEAD_DIM // 4
ROPE_THETA = 500000.0
RMS_EPS = 1e-6
LN_EPS = 1e-5
NEG_INF = -1e30

ADAM_LR = 0.001
ADAM_B1 = 0.9
ADAM_B2 = 0.999
ADAM_EPS = 1e-08
ADAM_WD = 0.01
ADAM_STEP = 10

V7X_VMEM_BYTES = 64 * 2 ** 20
VMEM_LIMIT = V7X_VMEM_BYTES - 8 * 2 ** 20
LANES = 128


def _params(*sem):
    return pltpu.CompilerParams(dimension_semantics=sem or None, vmem_limit_bytes=VMEM_LIMIT)


def _tile(n, pref):
    if n <= pref:
        return n
    t = pref - pref % LANES
    while t >= LANES:
        if n % t == 0:
            return t
        t -= LANES
    return n


def _full(shape):
    return pl.BlockSpec(shape, lambda *_: (0,) * len(shape))


def _rows(tm, width):
    return pl.BlockSpec((tm, width), lambda i: (i, 0))


def _rms(x, g):
    r = lax.rsqrt(jnp.mean(x * x, axis=-1, keepdims=True) + RMS_EPS)
    return x * r * g


def _rms_bwd(x, g, dy):
    r = lax.rsqrt(jnp.mean(x * x, axis=-1, keepdims=True) + RMS_EPS)
    xhat = x * r
    dg = jnp.sum(dy * xhat, axis=0, keepdims=True)
    dxhat = dy * g
    dx = r * (dxhat - xhat * jnp.mean(dxhat * xhat, axis=-1, keepdims=True))
    return dx, dg


_INV_SQRT2 = 1.0 / math.sqrt(2.0)
_INV_SQRT2PI = 1.0 / math.sqrt(2.0 * math.pi)


def _gelu(x):
    return 0.5 * x * (1.0 + lax.erf(x * _INV_SQRT2))


def _gelu_grad(x):
    return 0.5 * (1.0 + lax.erf(x * _INV_SQRT2)) + x * jnp.exp(-0.5 * x * x) * _INV_SQRT2PI


def _sigmoid(x):
    return 1.0 / (1.0 + jnp.exp(-x))


def _position():
    return lax.axis_index("x"), lax.axis_index("y"), lax.axis_index("c")


def _index(p):
    return 4 * p[0] + 2 * p[1] + p[2]


def allgather(xs, name):
    n = len(xs)
    any_spec = pl.BlockSpec(memory_space=pl.ANY)

    def body(*refs):
        x_refs, o_refs = refs[:n], refs[n:2 * n]
        send_sems, recv_sems, local_sems = refs[2 * n:]
        x, y, c = _position()
        me, sibling = (x, y, c), (x, y, 1 - c)
        chips = [(1 - x, y), (x, 1 - y), (1 - x, 1 - y)]

        def slot(a, p):
            return o_refs[a].at[:, pl.ds(_index(p), 1)]

        def copy(a, k, block, to, src=None):
            return pltpu.make_async_remote_copy(
                src_ref=slot(a, block) if src is None else src, dst_ref=slot(a, block),
                send_sem=send_sems.at[7 * a + k], recv_sem=recv_sems.at[7 * a + k],
                device_id=to, device_id_type=MESH)

        mine = [pltpu.make_async_copy(x_refs[a], slot(a, me), local_sems.at[a]) for a in range(n)]
        for cp in mine:
            cp.start()
        first = []
        for a in range(n):
            first.append(copy(a, 0, me, sibling, src=x_refs[a]))
            first += [copy(a, 1 + j, me, (*chip, c), src=x_refs[a]) for j, chip in enumerate(chips)]
        for cp in first:
            cp.start()
        passed = []
        for j, chip in enumerate(chips):
            for a in range(n):
                copy(a, 1 + j, (*chip, c), me).wait_recv()
                cp = copy(a, 4 + j, (*chip, c), sibling)
                cp.start()
                passed.append(cp)
        for a in range(n):
            copy(a, 0, sibling, me).wait_recv()
            for j, chip in enumerate(chips):
                copy(a, 4 + j, (*chip, 1 - c), me).wait_recv()
        for cp in first + passed:
            cp.wait_send()
        for cp in mine:
            cp.wait()

    return pl.pallas_call(
        body, name=name,
        out_shape=[jax.ShapeDtypeStruct((v.shape[0], N_DEV) + v.shape[2:], v.dtype) for v in xs],
        in_specs=[any_spec] * n, out_specs=[any_spec] * n,
        scratch_shapes=[pltpu.SemaphoreType.DMA((7 * n,)), pltpu.SemaphoreType.DMA((7 * n,)),
                        pltpu.SemaphoreType.DMA((n,))],
    )(*xs)


_HBM = pl.BlockSpec(memory_space=pltpu.HBM)
_SEM = pl.BlockSpec(memory_space=pltpu.SEMAPHORE)
_ORDERED_BY_DATA = pltpu.SideEffectType.DATAFLOW_SIDE_EFFECTING


def _in_hbm(v):
    return pltpu.with_memory_space_constraint(v, pltpu.HBM)


def _peer(k):
    x, y, c = _position()
    return (x ^ ((k >> 2) & 1), y ^ ((k >> 1) & 1), c ^ (k & 1))


def _split_copies(srcs, dsts, send, recv):
    return [pltpu.make_async_remote_copy(
        src_ref=src(k), dst_ref=dst(k), send_sem=send.at[7 * a + k - 1], recv_sem=recv.at[7 * a + k - 1],
        device_id=_peer(k), device_id_type=MESH)
        for a, (src, dst) in enumerate(zip(srcs, dsts)) for k in range(1, N_DEV)]


def _start_call(groups, sent, copies_of, name):
    flat = [v for grp in groups for v in grp]
    n, ng = len(flat), len(groups)

    def body(*refs):
        bufs, sems = refs[:n], refs[n:n + 2 * ng]
        q = 0
        for g, grp in enumerate(groups):
            for cp in copies_of(bufs[q:q + len(grp)], sems[2 * g], sems[2 * g + 1]):
                cp.start()
            q += len(grp)
        refs[-1][...] = jnp.zeros_like(refs[-1])

    sem_shapes = []
    for count in sent:
        sem_shapes += [pltpu.SemaphoreType.DMA((7 * count,)) for _ in range(2)]
    res = pl.pallas_call(
        body, name=name,
        out_shape=sem_shapes + [pltpu.HBM(v.shape, v.dtype) for v in flat] + [jax.ShapeDtypeStruct((8, LANES), F32)],
        in_specs=[_HBM] * n,
        out_specs=[_SEM] * (2 * ng) + [_HBM] * n + [pl.BlockSpec(memory_space=pltpu.VMEM)],
        input_output_aliases={i: 2 * ng + i for i in range(n)},
        compiler_params=pltpu.CompilerParams(has_side_effects=_ORDERED_BY_DATA),
    )(*[_in_hbm(v) for v in flat])
    sems = [(res[2 * g], res[2 * g + 1]) for g in range(ng)]
    return sems, list(res[2 * ng:-1]), res[-1]


def _wait_call(bufs, sems, copies_of, after, name):
    n = len(bufs)

    def body(*refs):
        for cp in copies_of(refs[:n], refs[n], refs[n + 1]):
            cp.wait_send()
            cp.wait_recv()

    return list(pl.pallas_call(
        body, name=name,
        out_shape=[pltpu.HBM(v.shape, v.dtype) for v in bufs],
        in_specs=[_HBM] * n + [_SEM, _SEM, pl.BlockSpec(memory_space=pl.ANY)], out_specs=[_HBM] * n,
        input_output_aliases={i: i for i in range(n)},
        compiler_params=pltpu.CompilerParams(has_side_effects=_ORDERED_BY_DATA),
    )(*bufs, sems[0], sems[1], after))


def _gather_copies(lands, send, recv):
    me = _index(_position())
    mine = [lambda k, ref=ref: ref.at[pl.ds(me, 1)] for ref in lands]
    return _split_copies(mine, mine, send, recv)


def _gather_arrivals(lands, send, recv):
    me = _index(_position())
    mine = [lambda k, ref=ref: ref.at[pl.ds(me, 1)] for ref in lands]
    theirs = [lambda k, ref=ref: ref.at[pl.ds(_index(_peer(k)), 1)] for ref in lands]
    return _split_copies(mine, theirs, send, recv)


def _exchange_copies(bufs, send, recv):
    half = len(bufs) // 2
    srcs = [lambda k, ref=ref: ref.at[pl.ds(_index(_peer(k)), 1)] for ref in bufs[:half]]
    dsts = [lambda k, ref=ref: ref.at[pl.ds(k - 1, 1)] for ref in bufs[half:]]
    return _split_copies(srcs, dsts, send, recv)


def gather_start(groups, name):
    sems, lands, _ = _start_call(groups, [len(grp) for grp in groups], _gather_copies, name)
    return sems, lands


def gather_wait(lands, sems, after, name):
    return _wait_call(lands, sems, _gather_arrivals, after, name)


def exchange_start(partials, name):
    lands = [lax.empty((N_DEV - 1,) + p.shape[1:], p.dtype) for p in partials]
    sems, bufs, token = _start_call([list(partials) + lands], [len(partials)], _exchange_copies, name)
    return sems[0], bufs, token


def exchange_wait(bufs, sems, after, name):
    bufs = _wait_call(bufs, sems, _exchange_copies, after, name)
    return bufs[:len(bufs) // 2], bufs[len(bufs) // 2:]


def prep_weight(w, layer, transpose, dev, name):
    L, K, n = w.shape
    block = (1, n, K) if transpose else (1, K, n)

    def body(d_ref, w_ref, o_ref):
        v = w_ref[0]
        o_ref[0] = (v.T if transpose else v).astype(BF16)

    return pl.pallas_call(
        body, name=name,
        grid_spec=pltpu.PrefetchScalarGridSpec(
            num_scalar_prefetch=1, grid=(1,),
            in_specs=[pl.BlockSpec((1, K, n), lambda i, d: (layer, 0, 0))],
            out_specs=pl.BlockSpec(block, lambda i, d: (d[0], 0, 0))),
        out_shape=jax.ShapeDtypeStruct((N_DEV,) + block[1:], BF16),
        compiler_params=_params("arbitrary"),
    )(dev, w)


def _adamw_math(w, g, m, v):
    mn = ADAM_B1 * m + (1.0 - ADAM_B1) * g
    vn = ADAM_B2 * v + (1.0 - ADAM_B2) * (g * g)
    m_hat = mn * (1.0 / (1.0 - ADAM_B1 ** ADAM_STEP))
    v_hat = vn * (1.0 / (1.0 - ADAM_B2 ** ADAM_STEP))
    return -ADAM_LR * (m_hat / (jnp.sqrt(v_hat) + ADAM_EPS) + ADAM_WD * w), mn, vn


def sum_adamw(partial, landed, w, m, v, prev, layer, transpose, dev, name):
    _, r, c = partial.shape
    if transpose:
        tc = _tile(c, 256)
        grid = (c // tc,)
        part_spec = pl.BlockSpec((1, r, tc), lambda i, d: (d[0], 0, i))
        land_spec = pl.BlockSpec((N_DEV - 1, r, tc), lambda i, d: (0, 0, i))
        w_spec = pl.BlockSpec((1, tc, r), lambda i, d: (layer, i, 0))
    else:
        tr = r // 2 if r % 32 == 0 else r
        grid = (r // tr,)
        part_spec = pl.BlockSpec((1, tr, c), lambda i, d: (d[0], i, 0))
        land_spec = pl.BlockSpec((N_DEV - 1, tr, c), lambda i, d: (0, i, 0))
        w_spec = pl.BlockSpec((1, tr, c), lambda i, d: (layer, i, 0))

    def body(d_ref, p_ref, l_ref, w_ref, m_ref, v_ref, *rest):
        g_ref, dl_ref, mo_ref, vo_ref = rest[4:]
        acc = p_ref[0].astype(F32)
        for k in range(N_DEV - 1):
            acc = acc + l_ref[k].astype(F32)
        g = acc.T if transpose else acc
        delta, mn, vn = _adamw_math(w_ref[0], g, m_ref[0], v_ref[0])
        g_ref[0], dl_ref[0], mo_ref[0], vo_ref[0] = g, delta, mn, vn

    return pl.pallas_call(
        body, name=name,
        grid_spec=pltpu.PrefetchScalarGridSpec(
            num_scalar_prefetch=1, grid=grid,
            in_specs=[part_spec, land_spec, w_spec, w_spec, w_spec] + [pl.BlockSpec(memory_space=pl.ANY)] * 4,
            out_specs=[w_spec] * 4),
        out_shape=[jax.ShapeDtypeStruct(w.shape, F32)] * 4,
        input_output_aliases={6 + q: q for q in range(4)},
        compiler_params=_params("parallel"),
    )(dev, partial, landed, w, m, v, *prev)


def sum_parts(parts, transpose, name):
    L, P, r, c = parts.shape
    tr = _tile(r, 256) if (not transpose and r % 8 == 0) else r
    if r % tr:
        tr = r
    out_block = (1, c, tr) if transpose else (1, tr, c)
    out_map = (lambda l, i: (l, 0, i)) if transpose else (lambda l, i: (l, i, 0))

    def body(p_ref, o_ref):
        acc = p_ref[0, 0].astype(F32)
        for s in range(1, P):
            acc = acc + p_ref[0, s].astype(F32)
        o_ref[0] = acc.T if transpose else acc

    return pl.pallas_call(
        body, name=name, grid=(L, r // tr),
        in_specs=[pl.BlockSpec((1, P, tr, c), lambda l, i: (l, 0, i, 0))],
        out_specs=pl.BlockSpec(out_block, out_map),
        out_shape=jax.ShapeDtypeStruct((L, c, r) if transpose else (L, r, c), F32),
        compiler_params=_params("parallel", "parallel"),
    )(parts)


def adamw(w, g, m, v, name):
    R, C = w.shape
    tr = _tile(R, 512)

    def body(w_ref, g_ref, m_ref, v_ref, d_ref, mo_ref, vo_ref):
        d_ref[...], mo_ref[...], vo_ref[...] = _adamw_math(w_ref[...], g_ref[...], m_ref[...], v_ref[...])

    spec = pl.BlockSpec((tr, C), lambda i: (i, 0))
    return pl.pallas_call(
        body, name=name, grid=(R // tr,),
        in_specs=[spec] * 4, out_specs=[spec] * 3,
        out_shape=[jax.ShapeDtypeStruct((R, C), F32)] * 3,
        compiler_params=_params("parallel"),
    )(w, g, m, v)


def add_norm(h, y, g_post, g_pre, name):
    T, D = h.shape
    tm = _tile(T, 512)
    has_y = y is not None

    def body(*refs):
        if has_y:
            h_ref, y_ref, gp_ref, g_ref, ho_ref, hn_ref = refs
            hv = h_ref[...] + _rms(y_ref[...], gp_ref[...])
            ho_ref[...] = hv
        else:
            h_ref, g_ref, hn_ref = refs
            hv = h_ref[...]
        hn_ref[...] = _rms(hv, g_ref[...]).astype(BF16)

    row, vec = _rows(tm, D), _full((1, D))
    if has_y:
        return pl.pallas_call(
            body, name=name, grid=(T // tm,), in_specs=[row, row, vec, vec], out_specs=[row, row],
            out_shape=[jax.ShapeDtypeStruct((T, D), F32), jax.ShapeDtypeStruct((T, D), BF16)],
            compiler_params=_params("parallel"))(h, y, g_post, g_pre)
    hn = pl.pallas_call(
        body, name=name, grid=(T // tm,), in_specs=[row, vec], out_specs=row,
        out_shape=jax.ShapeDtypeStruct((T, D), BF16), compiler_params=_params("parallel"))(h, g_pre)
    return h, hn


def add_norm_loss(h, y, g_post, target, name):
    T, D = h.shape
    tm = _tile(T, 512)

    def body(h_ref, y_ref, gp_ref, t_ref, dh_ref, loss_ref):
        @pl.when(pl.program_id(0) == 0)
        def _():
            loss_ref[...] = jnp.zeros_like(loss_ref)
        err = h_ref[...] + _rms(y_ref[...], gp_ref[...]) - t_ref[...]
        dh_ref[...] = err * (1.0 / D)
        loss_ref[...] += jnp.sum(err * err)

    row = _rows(tm, D)
    return pl.pallas_call(
        body, name=name, grid=(T // tm,), in_specs=[row, row, _full((1, D)), row],
        out_specs=[row, _full((8, LANES))],
        out_shape=[jax.ShapeDtypeStruct((T, D), F32), jax.ShapeDtypeStruct((8, LANES), F32)],
        compiler_params=_params("arbitrary"))(h, y, g_post, target)


def norm_bwd(dh_in, pre, post, name):
    T, D = dh_in.shape
    tm = _tile(T, 512)
    row, vec = _rows(tm, D), _full((1, D))
    ins, in_specs, outs, out_specs, names = [dh_in], [row], [], [], []
    if pre is not None:
        ins += list(pre)
        in_specs += [row, row, vec]
        outs += [jax.ShapeDtypeStruct((T, D), F32), jax.ShapeDtypeStruct((1, D), F32)]
        out_specs += [row, vec]
        names += ["dh", "dg_pre"]
    if post is not None:
        ins += list(post)
        in_specs += [row, vec]
        outs += [jax.ShapeDtypeStruct((T, D), BF16), jax.ShapeDtypeStruct((1, D), F32)]
        out_specs += [row, vec]
        names += ["dy", "dg_post"]

    def body(*refs):
        refs = list(refs)
        first = pl.program_id(0) == 0
        dh = refs.pop(0)[...]
        if pre is not None:
            dpre_ref, x_ref, g_ref = refs[:3]
            refs = refs[3:]
        if post is not None:
            y_ref, gp_ref = refs[:2]
            refs = refs[2:]
        if pre is not None:
            dh_ref, dg_ref = refs[:2]
            refs = refs[2:]
            dx, dg = _rms_bwd(x_ref[...], g_ref[...], dpre_ref[...].astype(F32))
            dh = dh + dx
            dh_ref[...] = dh

            @pl.when(first)
            def _():
                dg_ref[...] = jnp.zeros_like(dg_ref)
            dg_ref[...] += dg
        if post is not None:
            dy_ref, dgp_ref = refs[:2]
            dy, dgp = _rms_bwd(y_ref[...], gp_ref[...], dh)
            dy_ref[...] = dy.astype(BF16)

            @pl.when(first)
            def _():
                dgp_ref[...] = jnp.zeros_like(dgp_ref)
            dgp_ref[...] += dgp

    res = pl.pallas_call(
        body, name=name, grid=(T // tm,), in_specs=in_specs, out_specs=out_specs, out_shape=outs,
        compiler_params=_params("arbitrary"))(*ins)
    return dict(zip(names, res))


_NT = (((1,), (1,)), ((), ()))
_TN = (((0,), (0,)), ((), ()))


def rowmm(a, w, transposed_w, bias, out_dtype, name, after=None):
    M, K = a.shape
    N = w.shape[0] if transposed_w else w.shape[1]
    tm = _tile(M, 256)
    tn = _tile(N, 512)

    def body(*refs):
        a_ref, w_ref = refs[:2]
        o_ref = refs[-1]
        av = a_ref[...]
        for j in range(N // tn):
            cols = pl.ds(j * tn, tn)
            if transposed_w:
                acc = lax.dot_general(av, w_ref[cols, :], _NT, preferred_element_type=F32)
            else:
                acc = jnp.dot(av, w_ref[:, cols], preferred_element_type=F32)
            if bias is not None:
                acc = acc + refs[2][:, cols]
            o_ref[:, cols] = acc.astype(out_dtype)

    ins, in_specs = [a, w], [_rows(tm, K), _full(w.shape)]
    if bias is not None:
        ins.append(bias)
        in_specs.append(_full((1, N)))
    if after is not None:
        ins.append(after)
        in_specs.append(pl.BlockSpec(memory_space=pl.ANY))
    return pl.pallas_call(
        body, name=name, grid=(M // tm,), in_specs=in_specs, out_specs=_rows(tm, N),
        out_shape=jax.ShapeDtypeStruct((M, N), out_dtype), compiler_params=_params("parallel"))(*ins)


def grad_mm(a, b, name):
    T, N = a.shape
    K = b.shape[1]
    tn = _tile(N, 512)
    tt = _tile(T, 512)

    def body(a_ref, b_ref, o_ref, acc_ref):
        t = pl.program_id(1)

        @pl.when(t == 0)
        def _():
            acc_ref[...] = jnp.zeros_like(acc_ref)
        acc_ref[...] += lax.dot_general(a_ref[...], b_ref[...], _TN, preferred_element_type=F32)

        @pl.when(t == pl.num_programs(1) - 1)
        def _():
            o_ref[...] = acc_ref[...].astype(BF16)

    return pl.pallas_call(
        body, name=name, grid=(N // tn, T // tt),
        in_specs=[pl.BlockSpec((tt, tn), lambda j, t: (t, j)), pl.BlockSpec((tt, K), lambda j, t: (t, 0))],
        out_specs=pl.BlockSpec((tn, K), lambda j, t: (j, 0)),
        out_shape=jax.ShapeDtypeStruct((N, K), BF16),
        scratch_shapes=[pltpu.VMEM((tn, K), F32)],
        compiler_params=_params("parallel", "arbitrary"))(a, b)


def ffn_up(fn, w_gu_t, name):
    T, D = fn.shape
    F = w_gu_t.shape[0] // 2
    tm = _tile(T, 256)
    tn = _tile(F, 512)

    def body(a_ref, w_ref, gu_ref, act_ref):
        av = a_ref[...]
        for j in range(F // tn):
            g = lax.dot_general(av, w_ref[pl.ds(j * tn, tn), :], _NT, preferred_element_type=F32)
            up = lax.dot_general(av, w_ref[pl.ds(F + j * tn, tn), :], _NT, preferred_element_type=F32)
            gu_ref[:, pl.ds(j * tn, tn)] = g.astype(BF16)
            gu_ref[:, pl.ds(F + j * tn, tn)] = up.astype(BF16)
            act_ref[:, pl.ds(j * tn, tn)] = (g * _sigmoid(g) * up).astype(BF16)

    return pl.pallas_call(
        body, name=name, grid=(T // tm,), in_specs=[_rows(tm, D), _full(w_gu_t.shape)],
        out_specs=[_rows(tm, 2 * F), _rows(tm, F)],
        out_shape=[jax.ShapeDtypeStruct((T, 2 * F), BF16), jax.ShapeDtypeStruct((T, F), BF16)],
        compiler_params=_params("parallel"))(fn, w_gu_t)


def ffn_dact(df, w_down, gu, name):
    T, D = df.shape
    F = w_down.shape[0]
    tm = _tile(T, 256)
    tn = _tile(F, 512)

    def body(d_ref, w_ref, gu_ref, o_ref):
        dv = d_ref[...]
        for j in range(F // tn):
            dact = lax.dot_general(dv, w_ref[pl.ds(j * tn, tn), :], _NT, preferred_element_type=F32)
            g = gu_ref[:, pl.ds(j * tn, tn)].astype(F32)
            up = gu_ref[:, pl.ds(F + j * tn, tn)].astype(F32)
            sg = _sigmoid(g)
            o_ref[:, pl.ds(j * tn, tn)] = (dact * up * (sg * (1.0 + g * (1.0 - sg)))).astype(BF16)
            o_ref[:, pl.ds(F + j * tn, tn)] = (dact * g * sg).astype(BF16)

    return pl.pallas_call(
        body, name=name, grid=(T // tm,), in_specs=[_rows(tm, D), _full(w_down.shape), _rows(tm, 2 * F)],
        out_specs=_rows(tm, 2 * F), out_shape=jax.ShapeDtypeStruct((T, 2 * F), BF16),
        compiler_params=_params("parallel"))(df, w_down, gu)


def _layernorm_stats(v):
    mu = jnp.mean(v, axis=-1, keepdims=True)
    cen = v - mu
    rstd = lax.rsqrt(jnp.mean(cen * cen, axis=-1, keepdims=True) + LN_EPS)
    return cen * rstd, rstd


def gmlp_fwd(a, ln_g, ln_b, w_mask, b_s_t, name):
    T, W2 = a.shape
    W = W2 // 2
    G = w_mask.shape[0]
    C = W // G

    def body(a_ref, g_ref, b_ref, w_ref, bs_ref, o_ref):
        xhat, _ = _layernorm_stats(_gelu(a_ref[:, W:].astype(F32)))
        vln = (xhat * g_ref[...] + b_ref[...]).astype(BF16)
        for g in range(G):
            cols = pl.ds(g * C, C)
            sv = jnp.dot(w_ref[g], vln[:, g * C:(g + 1) * C], preferred_element_type=F32) + bs_ref[:, g:g + 1]
            o_ref[:, cols] = (_gelu(a_ref[:, cols].astype(F32)) * sv).astype(BF16)

    return pl.pallas_call(
        body, name=name, grid=(T // CHUNK,),
        in_specs=[_rows(CHUNK, W2), _full((1, W)), _full((1, W)), _full(w_mask.shape), _full(b_s_t.shape)],
        out_specs=_rows(CHUNK, W), out_shape=jax.ShapeDtypeStruct((T, W), BF16),
        compiler_params=_params("parallel"))(a, ln_g, ln_b, w_mask, b_s_t)


def gmlp_bwd(a, dgated, ln_g, ln_b, w_mask, w_mask_t, b_s_t, group_onehot, name):
    T, W2 = a.shape
    W = W2 // 2
    G = w_mask.shape[0]
    C = W // G

    def body(a_ref, dg_ref, g_ref, b_ref, w_ref, wt_ref, bs_ref, e_ref,
             da_ref, dws_ref, dbs_ref, dlg_ref, dlb_ref, dbin_ref, dvln_ref):
        @pl.when(pl.program_id(0) == 0)
        def _():
            for r in (dws_ref, dbs_ref, dlg_ref, dlb_ref, dbin_ref):
                r[...] = jnp.zeros_like(r)
        row = lax.broadcasted_iota(jnp.int32, (CHUNK, CHUNK), 0)
        col = lax.broadcasted_iota(jnp.int32, (CHUNK, CHUNK), 1)
        causal = col <= row
        av = a_ref[:, W:].astype(F32)
        xhat, rstd = _layernorm_stats(_gelu(av))
        vln = (xhat * g_ref[...] + b_ref[...]).astype(BF16)
        dsv_all = (dg_ref[...].astype(F32) * _gelu(a_ref[:, :W].astype(F32))).astype(BF16)
        dbs_ref[...] += jnp.dot(dsv_all, e_ref[...], preferred_element_type=F32)
        for g in range(G):
            cols = pl.ds(g * C, C)
            vg = vln[:, g * C:(g + 1) * C]
            dsv = dsv_all[:, g * C:(g + 1) * C]
            sv = jnp.dot(w_ref[g], vg, preferred_element_type=F32) + bs_ref[:, g:g + 1]
            au = a_ref[:, cols].astype(F32)
            dau = dg_ref[:, cols].astype(F32) * sv * _gelu_grad(au)
            da_ref[:, cols] = dau.astype(BF16)
            dbin_ref[:, cols] += jnp.sum(dau, axis=0, keepdims=True)
            dws_ref[g] += jnp.where(causal, lax.dot_general(dsv, vg, _NT, preferred_element_type=F32), 0.0)
            dvln_ref[:, cols] = jnp.dot(wt_ref[g], dsv, preferred_element_type=F32)
        dvln = dvln_ref[...]
        dlg_ref[...] += jnp.sum(dvln * xhat, axis=0, keepdims=True)
        dlb_ref[...] += jnp.sum(dvln, axis=0, keepdims=True)
        dxhat = dvln * g_ref[...]
        dv = rstd * (dxhat - jnp.mean(dxhat, axis=-1, keepdims=True)
                     - xhat * jnp.mean(dxhat * xhat, axis=-1, keepdims=True))
        dav = dv * _gelu_grad(av)
        da_ref[:, W:] = dav.astype(BF16)
        dbin_ref[:, W:] += jnp.sum(dav, axis=0, keepdims=True)

    return pl.pallas_call(
        body, name=name, grid=(T // CHUNK,),
        in_specs=[_rows(CHUNK, W2), _rows(CHUNK, W), _full((1, W)), _full((1, W)), _full(w_mask.shape),
                  _full(w_mask_t.shape), _full(b_s_t.shape), _full(group_onehot.shape)],
        out_specs=[_rows(CHUNK, W2), _full((G, CHUNK, CHUNK)), _full((CHUNK, LANES)), _full((1, W)),
                   _full((1, W)), _full((1, W2))],
        out_shape=[jax.ShapeDtypeStruct((T, W2), BF16), jax.ShapeDtypeStruct((G, CHUNK, CHUNK), F32),
                   jax.ShapeDtypeStruct((CHUNK, LANES), F32), jax.ShapeDtypeStruct((1, W), F32),
                   jax.ShapeDtypeStruct((1, W), F32), jax.ShapeDtypeStruct((1, W2), F32)],
        scratch_shapes=[pltpu.VMEM((CHUNK, W), F32)],
        compiler_params=_params("arbitrary"))(a, dgated, ln_g, ln_b, w_mask, w_mask_t, b_s_t, group_onehot)


def rope_tables(pos, inv_freq_row, name):
    T = pos.shape[0]
    tm = _tile(T, 512)

    def body(p_ref, f_ref, c_ref, s1_ref, s2_ref):
        ang = p_ref[...].astype(F32) * f_ref[...]
        lane = lax.broadcasted_iota(jnp.int32, (tm, LANES), 1) % HEAD_DIM
        sin = jnp.sin(ang)
        c_ref[...] = jnp.cos(ang)
        s1_ref[...] = jnp.where(lane < _HALF, -sin, 0.0)
        s2_ref[...] = jnp.where((lane >= _HALF) & (lane < ROPE_DIM), sin, 0.0)

    tab = _rows(tm, LANES)
    return pl.pallas_call(
        body, name=name, grid=(T // tm,), in_specs=[_rows(tm, 1), _full((1, LANES))], out_specs=[tab] * 3,
        out_shape=[jax.ShapeDtypeStruct((T, LANES), F32)] * 3, compiler_params=_params("parallel"))(pos, inv_freq_row)


_HALF = ROPE_DIM // 2


def _slabs(x):
    return [x[:, b * LANES:(b + 1) * LANES] for b in range(x.shape[1] // LANES)]


def _rotate(x, c, s1, s2):
    return [xs * c + pltpu.roll(xs, LANES - _HALF, 1) * s1 + pltpu.roll(xs, _HALF, 1) * s2 for xs in _slabs(x)]


def _rotate_transposed(dy, c, s1, s2):
    return [ds * c + pltpu.roll(ds * s1, _HALF, 1) + pltpu.roll(ds * s2, LANES - _HALF, 1) for ds in _slabs(dy)]


def rope_fwd(qkv, tabs, q_width, kv_width, name):
    T, QKV = qkv.shape
    tm = _tile(T, 512)
    QK = q_width + kv_width
    scale = HEAD_DIM ** -0.5

    def body(x_ref, c_ref, s1_ref, s2_ref, o_ref):
        x = x_ref[:, :QK].astype(F32)
        slabs = _rotate(x, c_ref[...], s1_ref[...], s2_ref[...])
        for b, y in enumerate(slabs):
            if b * LANES < q_width:
                y = y * scale
            o_ref[:, b * LANES:(b + 1) * LANES] = y.astype(BF16)

    tab = _rows(tm, LANES)
    return pl.pallas_call(
        body, name=name, grid=(T // tm,), in_specs=[_rows(tm, QKV), tab, tab, tab], out_specs=_rows(tm, QK),
        out_shape=jax.ShapeDtypeStruct((T, QK), BF16), compiler_params=_params("parallel"))(qkv, *tabs)


def rope_bwd(dq, dk, dv, tabs, name):
    T, Q = dq.shape
    KV = dk.shape[1]
    tm = _tile(T, 512)
    scale = HEAD_DIM ** -0.5

    def body(dq_ref, dk_ref, dv_ref, c_ref, s1_ref, s2_ref, o_ref, b_ref):
        @pl.when(pl.program_id(0) == 0)
        def _():
            b_ref[...] = jnp.zeros_like(b_ref)
        tabs_v = (c_ref[...], s1_ref[...], s2_ref[...])
        pieces = [s * scale for s in _rotate_transposed(dq_ref[...], *tabs_v)]
        pieces += _rotate_transposed(dk_ref[...], *tabs_v)
        pieces += _slabs(dv_ref[...])
        for b, y in enumerate(pieces):
            cols = pl.ds(b * LANES, LANES)
            o_ref[:, cols] = y.astype(BF16)
            b_ref[:, cols] += jnp.sum(y, axis=0, keepdims=True)

    tab = _rows(tm, LANES)
    return pl.pallas_call(
        body, name=name, grid=(T // tm,), in_specs=[_rows(tm, Q), _rows(tm, KV), _rows(tm, KV), tab, tab, tab],
        out_specs=[_rows(tm, Q + 2 * KV), _full((1, Q + 2 * KV))],
        out_shape=[jax.ShapeDtypeStruct((T, Q + 2 * KV), BF16), jax.ShapeDtypeStruct((1, Q + 2 * KV), F32)],
        compiler_params=_params("arbitrary"))(dq, dk, dv, *tabs)


def _attn_block(n, q_ref, kc_ref, kp_ref, vc_ref, vp_ref, sink_ref, kh, group):
    GR = group * CHUNK
    lanes = slice(kh * HEAD_DIM, (kh + 1) * HEAD_DIM)
    kk = jnp.concatenate([kp_ref[:, lanes], kc_ref[:, lanes]], axis=0)
    vv = jnp.concatenate([vp_ref[:, lanes], vc_ref[:, lanes]], axis=0)
    qg = jnp.concatenate(
        [q_ref[:, (kh * group + g) * HEAD_DIM:(kh * group + g + 1) * HEAD_DIM] for g in range(group)], axis=0)
    s = lax.dot_general(qg, kk, _NT, preferred_element_type=F32)
    qi = lax.broadcasted_iota(jnp.int32, (GR, 2 * CHUNK), 0) % CHUNK
    sj = lax.broadcasted_iota(jnp.int32, (GR, 2 * CHUNK), 1)
    valid = (sj > qi) & (sj <= qi + CHUNK) & ((n > 0) | (sj >= CHUNK))
    s = jnp.where(valid, s, NEG_INF)
    head = lax.broadcasted_iota(jnp.int32, (GR, 1), 0) // CHUNK
    sink = jnp.zeros((GR, 1), F32)
    for g in range(group):
        sink = jnp.where(head == g, sink_ref[kh * group + g], sink)
    m = jnp.maximum(jnp.max(s, axis=-1, keepdims=True), sink)
    p = jnp.exp(s - m)
    e_sink = jnp.exp(sink - m)
    inv = 1.0 / (jnp.sum(p, axis=-1, keepdims=True) + e_sink)
    return qg, kk, vv, p * inv, e_sink * inv


def _attn_specs(q_width, kv_width, order):
    qb = q_width // kv_width
    prev = lambda i: jnp.maximum(order(i) - 1, 0)
    return [pl.BlockSpec((CHUNK, q_width), lambda i: (order(i), 0)),
            pl.BlockSpec((CHUNK, kv_width), lambda i: (order(i), qb)),
            pl.BlockSpec((CHUNK, kv_width), lambda i: (prev(i), qb)),
            pl.BlockSpec((CHUNK, kv_width), lambda i: (order(i), qb + 1)),
            pl.BlockSpec((CHUNK, kv_width), lambda i: (prev(i), qb + 1))]


def attn_fwd(qk, qkv, sinks, q_width, kv_width, name):
    T = qk.shape[0]
    group = q_width // kv_width

    def body(q_ref, kc_ref, kp_ref, vc_ref, vp_ref, sink_ref, o_ref):
        n = pl.program_id(0)
        for kh in range(kv_width // HEAD_DIM):
            _, _, vv, p, _ = _attn_block(n, q_ref, kc_ref, kp_ref, vc_ref, vp_ref, sink_ref, kh, group)
            og = jnp.dot(p.astype(BF16), vv, preferred_element_type=F32)
            for g in range(group):
                h = kh * group + g
                o_ref[:, h * HEAD_DIM:(h + 1) * HEAD_DIM] = og[g * CHUNK:(g + 1) * CHUNK].astype(BF16)

    specs = _attn_specs(q_width, kv_width, lambda i: i)
    return pl.pallas_call(
        body, name=name, grid=(T // CHUNK,),
        in_specs=specs + [pl.BlockSpec(memory_space=pltpu.SMEM)],
        out_specs=_rows(CHUNK, q_width), out_shape=jax.ShapeDtypeStruct((T, q_width), BF16),
        compiler_params=_params("parallel"))(qk, qk, qk, qkv, qkv, sinks)


def attn_bwd(qk, qkv, do, sinks, q_width, kv_width, name):
    T = qk.shape[0]
    NB = T // CHUNK
    group = q_width // kv_width

    def body(q_ref, kc_ref, kp_ref, vc_ref, vp_ref, do_ref, sink_ref, dq_ref, dk_ref, dv_ref, ds_ref, ck_ref, cv_ref):
        i = pl.program_id(0)
        n = NB - 1 - i

        @pl.when(i == 0)
        def _():
            ck_ref[...] = jnp.zeros_like(ck_ref)
            cv_ref[...] = jnp.zeros_like(cv_ref)
            ds_ref[...] = jnp.zeros_like(ds_ref)
        lane = lax.broadcasted_iota(jnp.int32, (1, LANES), 1)
        dsink_row = jnp.zeros((1, LANES), F32)
        for kh in range(kv_width // HEAD_DIM):
            qg, kk, vv, p, p_sink = _attn_block(n, q_ref, kc_ref, kp_ref, vc_ref, vp_ref, sink_ref, kh, group)
            dog = jnp.concatenate(
                [do_ref[:, (kh * group + g) * HEAD_DIM:(kh * group + g + 1) * HEAD_DIM] for g in range(group)], axis=0)
            dp = lax.dot_general(dog, vv, _NT, preferred_element_type=F32)
            delta = jnp.sum(p * dp, axis=-1, keepdims=True)
            ds = (p * (dp - delta)).astype(BF16)
            dsink = -p_sink * delta
            for g in range(group):
                h = kh * group + g
                dsink_row = dsink_row + jnp.where(lane == h, jnp.sum(dsink[g * CHUNK:(g + 1) * CHUNK]), 0.0)
            dqg = jnp.dot(ds, kk, preferred_element_type=F32)
            for g in range(group):
                h = kh * group + g
                dq_ref[:, h * HEAD_DIM:(h + 1) * HEAD_DIM] = dqg[g * CHUNK:(g + 1) * CHUNK]
            dkk = lax.dot_general(ds, qg, _TN, preferred_element_type=F32)
            dvv = lax.dot_general(p.astype(BF16), dog, _TN, preferred_element_type=F32)
            lanes = slice(kh * HEAD_DIM, (kh + 1) * HEAD_DIM)
            dk_ref[:, lanes] = dkk[CHUNK:] + ck_ref[:, lanes]
            dv_ref[:, lanes] = dvv[CHUNK:] + cv_ref[:, lanes]
            ck_ref[:, lanes] = dkk[:CHUNK]
            cv_ref[:, lanes] = dvv[:CHUNK]
        ds_ref[0:1, :] += dsink_row

    order = lambda i: NB - 1 - i
    specs = _attn_specs(q_width, kv_width, order)
    kv_out = pl.BlockSpec((CHUNK, kv_width), lambda i: (order(i), 0))
    q_rows = pl.BlockSpec((CHUNK, q_width), lambda i: (order(i), 0))
    return pl.pallas_call(
        body, name=name, grid=(NB,),
        in_specs=specs + [q_rows, pl.BlockSpec(memory_space=pltpu.SMEM)],
        out_specs=[q_rows, kv_out, kv_out, _full((8, LANES))],
        out_shape=[jax.ShapeDtypeStruct((T, q_width), F32), jax.ShapeDtypeStruct((T, kv_width), F32),
                   jax.ShapeDtypeStruct((T, kv_width), F32), jax.ShapeDtypeStruct((8, LANES), F32)],
        scratch_shapes=[pltpu.VMEM((CHUNK, kv_width), F32), pltpu.VMEM((CHUNK, kv_width), F32)],
        compiler_params=_params("arbitrary"))(qk, qk, qk, qkv, qkv, do, sinks)


def _blocked(w):
    return w.reshape(N_DEV, w.shape[0] // N_DEV, w.shape[1])


def kernel(x, positions, pre_mix_g, post_mix_g, pre_ffn_g, post_ffn_g, a_w_in, a_b_in, a_ln_g, a_ln_b, a_w_s, a_b_s, a_w_out, b_w_qkv, b_b_qkv, b_sinks, b_w_o, ffn_w_gu, ffn_w_down, loss_target, m_pre_mix_g, m_post_mix_g, m_pre_ffn_g, m_post_ffn_g, m_a_w_in, m_a_b_in, m_a_ln_g, m_a_ln_b, m_a_w_s, m_a_b_s, m_a_w_out, m_b_w_qkv, m_b_b_qkv, m_b_sinks, m_b_w_o, m_ffn_w_gu, m_ffn_w_down, v_pre_mix_g, v_post_mix_g, v_pre_ffn_g, v_post_ffn_g, v_a_w_in, v_a_b_in, v_a_ln_g, v_a_ln_b, v_a_w_s, v_a_b_s, v_a_w_out, v_b_w_qkv, v_b_b_qkv, v_b_sinks, v_b_w_o, v_ffn_w_gu, v_ffn_w_down):
    weights = dict(pre_mix_g=pre_mix_g, post_mix_g=post_mix_g, pre_ffn_g=pre_ffn_g, post_ffn_g=post_ffn_g,
                   a_w_in=a_w_in, a_b_in=a_b_in, a_ln_g=a_ln_g, a_ln_b=a_ln_b, a_w_s=a_w_s, a_b_s=a_b_s,
                   a_w_out=a_w_out, b_w_qkv=b_w_qkv, b_b_qkv=b_b_qkv, b_sinks=b_sinks, b_w_o=b_w_o,
                   ffn_w_gu=ffn_w_gu, ffn_w_down=ffn_w_down)
    mom_m = dict(pre_mix_g=m_pre_mix_g, post_mix_g=m_post_mix_g, pre_ffn_g=m_pre_ffn_g, post_ffn_g=m_post_ffn_g,
                 a_w_in=m_a_w_in, a_b_in=m_a_b_in, a_ln_g=m_a_ln_g, a_ln_b=m_a_ln_b, a_w_s=m_a_w_s, a_b_s=m_a_b_s,
                 a_w_out=m_a_w_out, b_w_qkv=m_b_w_qkv, b_b_qkv=m_b_b_qkv, b_sinks=m_b_sinks, b_w_o=m_b_w_o,
                 ffn_w_gu=m_ffn_w_gu, ffn_w_down=m_ffn_w_down)
    mom_v = dict(pre_mix_g=v_pre_mix_g, post_mix_g=v_post_mix_g, pre_ffn_g=v_pre_ffn_g, post_ffn_g=v_post_ffn_g,
                 a_w_in=v_a_w_in, a_b_in=v_a_b_in, a_ln_g=v_a_ln_g, a_ln_b=v_a_ln_b, a_w_s=v_a_w_s, a_b_s=v_a_b_s,
                 a_w_out=v_a_w_out, b_w_qkv=v_b_w_qkv, b_b_qkv=v_b_b_qkv, b_sinks=v_b_sinks, b_w_o=v_b_w_o,
                 ffn_w_gu=v_ffn_w_gu, ffn_w_down=v_ffn_w_down)
    names = list(weights)
    column_sharded = ("a_w_in", "b_w_qkv", "ffn_w_gu")
    row_sharded = ("a_w_out", "b_w_o", "ffn_w_down")
    big = column_sharded + row_sharded

    T, D = x.shape[1], x.shape[2]
    depth = pre_mix_g.shape[0]
    n_heads = b_sinks.shape[1]
    q_width = n_heads * HEAD_DIM
    kv_width = N_KV_HEADS * HEAD_DIM
    G = a_w_s.shape[1]
    W = a_ln_g.shape[1]
    device = _index(_position())

    dev = device.reshape(1).astype(jnp.int32)

    def layer_keys(i):
        mixer = ("a_w_in", "a_w_out") if i % 2 == 0 else ("b_w_qkv", "b_w_o")
        return [(k, i // 2) for k in mixer] + [("ffn_w_gu", i), ("ffn_w_down", i)]

    order = [(i, k, l) for i in range(depth) for k, l in layer_keys(i)]
    groups = [[prep_weight(weights[k], l, k in column_sharded, dev, "prep_" + k)] for _, k, l in order]
    bias_land = lax.dynamic_update_slice(
        jnp.zeros((N_DEV,) + b_b_qkv.shape, F32), b_b_qkv[None], (device, 0, 0))
    groups[order.index((1, "b_w_qkv", 0))].append(bias_land)
    gather_sems, flat_lands = gather_start(groups, "gather_start")
    in_gather = {(i, k): ([flat_lands.pop(0) for _ in grp], sems)
                 for (i, k, _), grp, sems in zip(order, groups, gather_sems)}
    gathered = {}

    def weight(i, k, after=None):
        if (i, k) not in gathered:
            lands, sems = in_gather[(i, k)]
            gathered[(i, k)] = gather_wait(lands, sems, after, f"gather_wait_{k}_{i}")
        g = gathered[(i, k)][0]
        return g.reshape(N_DEV * g.shape[1], g.shape[2])

    causal = jnp.tril(jnp.ones((CHUNK, CHUNK), dtype=bool))
    w_mask = jnp.where(causal[None, None], a_w_s, 0.0).astype(BF16)
    w_mask_t = jnp.swapaxes(w_mask, 2, 3)
    b_s_t = jnp.swapaxes(a_b_s, 1, 2)
    group_onehot = (jnp.arange(W)[:, None] // (W // G) == jnp.arange(LANES)[None, :]).astype(BF16)
    lane = jnp.arange(LANES) % HEAD_DIM
    inv_freq = ROPE_THETA ** (-jnp.arange(0, ROPE_DIM, 2, dtype=F32) / ROPE_DIM)
    inv_freq_row = jnp.where(lane < ROPE_DIM, inv_freq[lane % (ROPE_DIM // 2)], 0.0)[None, :].astype(F32)
    tabs = rope_tables(positions.reshape(T, 1), inv_freq_row, "rope_tables")

    saved = []
    h, hn = add_norm(x[0], None, None, pre_mix_g[0][None], "norm_first")
    for i in range(depth):
        j = i // 2
        s = dict(h=h, hn=hn)
        if i % 2 == 0:
            s["a"] = rowmm(hn, weight(i, "a_w_in", hn), True, a_b_in[j][None], BF16, "gmlp_in")
            s["gated"] = gmlp_fwd(s["a"], a_ln_g[j][None], a_ln_b[j][None], w_mask[j], b_s_t[j], "gmlp_gate")
            mix = rowmm(s["gated"], weight(i, "a_w_out", s["gated"]), False, None, F32, "gmlp_out")
        else:
            w_qkv = weight(i, "b_w_qkv", hn)
            b_qkv_full = jnp.swapaxes(gathered[(1, "b_w_qkv")][1], 0, 1).reshape(b_b_qkv.shape[0], 1, -1)
            s["qkv"] = rowmm(hn, w_qkv, True, b_qkv_full[j], BF16, "attn_qkv")
            s["qk"] = rope_fwd(s["qkv"], tabs, q_width, kv_width, "attn_rope")
            s["o"] = attn_fwd(s["qk"], s["qkv"], b_sinks[j], q_width, kv_width, "attn_core")
            mix = rowmm(s["o"], weight(i, "b_w_o", s["o"]), False, None, F32, "attn_out")
        s["mix"] = mix
        s["h2"], s["fn"] = add_norm(h, mix, post_mix_g[i][None], pre_ffn_g[i][None], "norm_mid")
        s["gu"], s["act"] = ffn_up(s["fn"], weight(i, "ffn_w_gu", s["fn"]), "ffn_up")
        s["f"] = rowmm(s["act"], weight(i, "ffn_w_down", s["act"]), False, None, F32, "ffn_down")
        saved.append(s)
        if i + 1 < depth:
            h, hn = add_norm(s["h2"], s["f"], post_ffn_g[i][None], pre_mix_g[i + 1][None], "norm_next")
    last = saved[-1]
    dh, loss_sum = add_norm_loss(last["h2"], last["f"], post_ffn_g[depth - 1][None], loss_target[0], "loss")
    loss = lax.psum(loss_sum[0, 0] * (0.5 / D), ("x", "y", "c"))

    small = {k: [None] * weights[k].shape[0] for k in names if k not in big}
    grads = {k: [None] * weights[k].shape[0] for k in big}
    in_flight = []

    def send_grads(keys, tag):
        sems, bufs, token = exchange_start([_blocked(grads[k][l]) for k, l in keys], "exchange_start_" + tag)
        in_flight.append((keys, sems, bufs, tag))
        return token

    r = norm_bwd(dh, None, (last["f"], post_ffn_g[depth - 1][None]), "norm_bwd_last")
    df, small["post_ffn_g"][depth - 1] = r["dy"], r["dg_post"]
    for i in reversed(range(depth)):
        j = i // 2
        s = saved[i]
        dgu = ffn_dact(df, weight(i, "ffn_w_down"), s["gu"], "ffn_dact")
        grads["ffn_w_down"][i] = grad_mm(s["act"], df, "ffn_down_grad")
        grads["ffn_w_gu"][i] = grad_mm(dgu, s["fn"], "ffn_up_grad")
        sent = send_grads(layer_keys(i)[2:], f"ffn_{i}")
        dfn = rowmm(dgu, weight(i, "ffn_w_gu"), False, None, F32, "ffn_dx", after=sent)
        r = norm_bwd(dh, (dfn, s["h2"], pre_ffn_g[i][None]), (s["mix"], post_mix_g[i][None]), "norm_bwd_mid")
        dh, dmix = r["dh"], r["dy"]
        small["pre_ffn_g"][i], small["post_mix_g"][i] = r["dg_pre"], r["dg_post"]
        if i % 2 == 0:
            grads["a_w_out"][j] = grad_mm(s["gated"], dmix, "gmlp_out_grad")
            dgated = rowmm(dmix, weight(i, "a_w_out"), True, None, BF16, "gmlp_dgated")
            da, dws, dbs, dlg, dlb, dbin = gmlp_bwd(
                s["a"], dgated, a_ln_g[j][None], a_ln_b[j][None], w_mask[j], w_mask_t[j], b_s_t[j], group_onehot,
                "gmlp_gate_bwd")
            small["a_w_s"][j], small["a_b_s"][j] = dws, dbs[:, :G].T
            small["a_ln_g"][j], small["a_ln_b"][j], small["a_b_in"][j] = dlg[0], dlb[0], dbin[0]
            grads["a_w_in"][j] = grad_mm(da, s["hn"], "gmlp_in_grad")
            sent = send_grads(layer_keys(i)[:2], f"mixer_{i}")
            dhn = rowmm(da, weight(i, "a_w_in"), False, None, F32, "gmlp_dx", after=sent)
        else:
            grads["b_w_o"][j] = grad_mm(s["o"], dmix, "attn_out_grad")
            do = rowmm(dmix, weight(i, "b_w_o"), True, None, BF16, "attn_do")
            dq, dk, dv, dsink = attn_bwd(s["qk"], s["qkv"], do, b_sinks[j], q_width, kv_width, "attn_core_bwd")
            dqkv, dbias = rope_bwd(dq, dk, dv, tabs, "attn_rope_bwd")
            small["b_sinks"][j], small["b_b_qkv"][j] = dsink[0, :n_heads], dbias[0]
            grads["b_w_qkv"][j] = grad_mm(dqkv, s["hn"], "attn_qkv_grad")
            sent = send_grads(layer_keys(i)[:2], f"mixer_{i}")
            dhn = rowmm(dqkv, weight(i, "b_w_qkv"), False, None, F32, "attn_dx", after=sent)
        if i > 0:
            prev = saved[i - 1]
            r = norm_bwd(dh, (dhn, s["h"], pre_mix_g[i][None]), (prev["f"], post_ffn_g[i - 1][None]), "norm_bwd_next")
            dh, df = r["dh"], r["dy"]
            small["pre_mix_g"][i], small["post_ffn_g"][i - 1] = r["dg_pre"], r["dg_post"]
        else:
            r = norm_bwd(dh, (dhn, s["h"], pre_mix_g[0][None]), None, "norm_bwd_first")
            dh, small["pre_mix_g"][0] = r["dh"], r["dg_pre"]
    grad_x = dh[None]

    stacked = {k: [lax.empty(weights[k].shape, F32) for _ in range(4)] for k in big}
    for keys, sems, bufs, tag in in_flight:
        partials, landed = exchange_wait(bufs, sems, dh, "exchange_wait_" + tag)
        for (k, l), part, land in zip(keys, partials, landed):
            stacked[k] = sum_adamw(part, land, weights[k], mom_m[k], mom_v[k], stacked[k], l,
                                   k in column_sharded, dev, "adamw_" + k)
    out_g, out_d, out_m, out_v = {}, {}, {}, {}
    for k in big:
        out_g[k], out_d[k], out_m[k], out_v[k] = stacked[k]

    small_names = [k for k in names if k not in big]
    partial = {k: jnp.stack([p.reshape(-1) for p in small[k]]) for k in small_names}
    sizes = [partial[k].size for k in small_names]
    total = sum(sizes)
    padded = -(-total // (LANES * LANES)) * (LANES * LANES)
    pack = lambda vals: jnp.concatenate([v.reshape(-1) for v in vals] + [jnp.zeros((padded - total,), F32)])
    gathered_small = allgather([pack([partial[k] for k in small_names]).reshape(1, 1, padded // LANES, LANES)],
                               "gather_small_grads")[0]
    summed = sum_parts(gathered_small, False, "sum_small")[0].reshape(-1)
    offsets = [sum(sizes[:q]) for q in range(len(sizes))]
    g_small = {}
    for k, off, size in zip(small_names, offsets, sizes):
        g = summed[off:off + size]
        if k == "b_b_qkv":
            n_local = b_b_qkv.shape[1]
            g = lax.dynamic_slice_in_dim(g.reshape(b_b_qkv.shape[0], -1), device * n_local, n_local, axis=1)
        g_small[k] = g.reshape(weights[k].shape)
    local_total = sum(weights[k].size for k in small_names)
    local_padded = -(-local_total // (LANES * LANES)) * (LANES * LANES)
    pack_local = lambda src: jnp.concatenate(
        [src[k].reshape(-1) for k in small_names] + [jnp.ones((local_padded - local_total,), F32)]
    ).reshape(local_padded // LANES, LANES)
    d, mn, vn = adamw(pack_local(weights), pack_local(g_small), pack_local(mom_m), pack_local(mom_v), "adamw_small")
    off = 0
    for k in small_names:
        size, shape = weights[k].size, weights[k].shape
        cut = lambda t: t.reshape(-1)[off:off + size].reshape(shape)
        out_g[k], out_d[k], out_m[k], out_v[k] = g_small[k], cut(d), cut(mn), cut(vn)
        off += size

    return (loss, grad_x, *[out_g[k] for k in names], *[out_d[k] for k in names],
            *[out_m[k] for k in names], *[out_v[k] for k in names])
```

```python
import functools
import math

import jax
import jax.numpy as jnp
from jax import lax
from jax.experimental import pallas as pl
from jax.experimental.pallas import tpu as pltpu

F32, BF16 = jnp.float32, jnp.bfloat16
MESH = pl.DeviceIdType.MESH
N_DEV = 8

CHUNK = 128
HEAD_DIM = 64
N_KV_HEADS = 4
ROPE_DIM = HEAD_DIM // 4
ROPE_THETA = 500000.0
RMS_EPS = 1e-6
LN_EPS = 1e-5
NEG_INF = -1e30

ADAM_LR = 0.001
ADAM_B1 = 0.9
ADAM_B2 = 0.999
ADAM_EPS = 1e-08
ADAM_WD = 0.01
ADAM_STEP = 10

V7X_VMEM_BYTES = 64 * 2 ** 20
VMEM_LIMIT = V7X_VMEM_BYTES - 8 * 2 ** 20
LANES = 128


def _params(*sem):
    return pltpu.CompilerParams(dimension_semantics=sem or None, vmem_limit_bytes=VMEM_LIMIT)


def _tile(n, pref):
    if n <= pref:
        return n
    t = pref - pref % LANES
    while t >= LANES:
        if n % t == 0:
            return t
        t -= LANES
    return n


def _full(shape):
    return pl.BlockSpec(shape, lambda *_: (0,) * len(shape))


def _rows(tm, width):
    return pl.BlockSpec((tm, width), lambda i: (i, 0))


def _rms(x, g):
    r = lax.rsqrt(jnp.mean(x * x, axis=-1, keepdims=True) + RMS_EPS)
    return x * r * g


def _rms_bwd(x, g, dy):
    r = lax.rsqrt(jnp.mean(x * x, axis=-1, keepdims=True) + RMS_EPS)
    xhat = x * r
    dg = jnp.sum(dy * xhat, axis=0, keepdims=True)
    dxhat = dy * g
    dx = r * (dxhat - xhat * jnp.mean(dxhat * xhat, axis=-1, keepdims=True))
    return dx, dg


_INV_SQRT2 = 1.0 / math.sqrt(2.0)
_INV_SQRT2PI = 1.0 / math.sqrt(2.0 * math.pi)


def _gelu(x):
    return 0.5 * x * (1.0 + lax.erf(x * _INV_SQRT2))


def _gelu_grad(x):
    return 0.5 * (1.0 + lax.erf(x * _INV_SQRT2)) + x * jnp.exp(-0.5 * x * x) * _INV_SQRT2PI


def _sigmoid(x):
    return 1.0 / (1.0 + jnp.exp(-x))


def _position():
    return lax.axis_index("x"), lax.axis_index("y"), lax.axis_index("c")


def _index(p):
    return 4 * p[0] + 2 * p[1] + p[2]


def allgather(xs, name):
    n = len(xs)
    any_spec = pl.BlockSpec(memory_space=pl.ANY)

    def body(*refs):
        x_refs, o_refs = refs[:n], refs[n:2 * n]
        send_sems, recv_sems, local_sems = refs[2 * n:]
        x, y, c = _position()
        me, sibling = (x, y, c), (x, y, 1 - c)
        chips = [(1 - x, y), (x, 1 - y), (1 - x, 1 - y)]

        def slot(a, p):
            return o_refs[a].at[:, pl.ds(_index(p), 1)]

        def copy(a, k, block, to, src=None):
            return pltpu.make_async_remote_copy(
                src_ref=slot(a, block) if src is None else src, dst_ref=slot(a, block),
                send_sem=send_sems.at[7 * a + k], recv_sem=recv_sems.at[7 * a + k],
                device_id=to, device_id_type=MESH)

        mine = [pltpu.make_async_copy(x_refs[a], slot(a, me), local_sems.at[a]) for a in range(n)]
        for cp in mine:
            cp.start()
        first = []
        for a in range(n):
            first.append(copy(a, 0, me, sibling, src=x_refs[a]))
            first += [copy(a, 1 + j, me, (*chip, c), src=x_refs[a]) for j, chip in enumerate(chips)]
        for cp in first:
            cp.start()
        passed = []
        for j, chip in enumerate(chips):
            for a in range(n):
                copy(a, 1 + j, (*chip, c), me).wait_recv()
                cp = copy(a, 4 + j, (*chip, c), sibling)
                cp.start()
                passed.append(cp)
        for a in range(n):
            copy(a, 0, sibling, me).wait_recv()
            for j, chip in enumerate(chips):
                copy(a, 4 + j, (*chip, 1 - c), me).wait_recv()
        for cp in first + passed:
            cp.wait_send()
        for cp in mine:
            cp.wait()

    return pl.pallas_call(
        body, name=name,
        out_shape=[jax.ShapeDtypeStruct((v.shape[0], N_DEV) + v.shape[2:], v.dtype) for v in xs],
        in_specs=[any_spec] * n, out_specs=[any_spec] * n,
        scratch_shapes=[pltpu.SemaphoreType.DMA((7 * n,)), pltpu.SemaphoreType.DMA((7 * n,)),
                        pltpu.SemaphoreType.DMA((n,))],
    )(*xs)


_HBM = pl.BlockSpec(memory_space=pltpu.HBM)
_SEM = pl.BlockSpec(memory_space=pltpu.SEMAPHORE)
_ORDERED_BY_DATA = pltpu.SideEffectType.DATAFLOW_SIDE_EFFECTING


def _in_hbm(v):
    return pltpu.with_memory_space_constraint(v, pltpu.HBM)


def _peer(k):
    x, y, c = _position()
    return (x ^ ((k >> 2) & 1), y ^ ((k >> 1) & 1), c ^ (k & 1))


def _split_copies(srcs, dsts, send, recv):
    return [pltpu.make_async_remote_copy(
        src_ref=src(k), dst_ref=dst(k), send_sem=send.at[7 * a + k - 1], recv_sem=recv.at[7 * a + k - 1],
        device_id=_peer(k), device_id_type=MESH)
        for a, (src, dst) in enumerate(zip(srcs, dsts)) for k in range(1, N_DEV)]


def _start_call(groups, sent, copies_of, name):
    flat = [v for grp in groups for v in grp]
    n, ng = len(flat), len(groups)

    def body(*refs):
        bufs, sems = refs[:n], refs[n:n + 2 * ng]
        q = 0
        for g, grp in enumerate(groups):
            for cp in copies_of(bufs[q:q + len(grp)], sems[2 * g], sems[2 * g + 1]):
                cp.start()
            q += len(grp)
        refs[-1][...] = jnp.zeros_like(refs[-1])

    sem_shapes = []
    for count in sent:
        sem_shapes += [pltpu.SemaphoreType.DMA((7 * count,)) for _ in range(2)]
    res = pl.pallas_call(
        body, name=name,
        out_shape=sem_shapes + [pltpu.HBM(v.shape, v.dtype) for v in flat] + [jax.ShapeDtypeStruct((8, LANES), F32)],
        in_specs=[_HBM] * n,
        out_specs=[_SEM] * (2 * ng) + [_HBM] * n + [pl.BlockSpec(memory_space=pltpu.VMEM)],
        input_output_aliases={i: 2 * ng + i for i in range(n)},
        compiler_params=pltpu.CompilerParams(has_side_effects=_ORDERED_BY_DATA),
    )(*[_in_hbm(v) for v in flat])
    sems = [(res[2 * g], res[2 * g + 1]) for g in range(ng)]
    return sems, list(res[2 * ng:-1]), res[-1]


def _wait_call(bufs, sems, copies_of, after, name):
    n = len(bufs)

    def body(*refs):
        for cp in copies_of(refs[:n], refs[n], refs[n + 1]):
            cp.wait_send()
            cp.wait_recv()

    return list(pl.pallas_call(
        body, name=name,
        out_shape=[pltpu.HBM(v.shape, v.dtype) for v in bufs],
        in_specs=[_HBM] * n + [_SEM, _SEM, pl.BlockSpec(memory_space=pl.ANY)], out_specs=[_HBM] * n,
        input_output_aliases={i: i for i in range(n)},
        compiler_params=pltpu.CompilerParams(has_side_effects=_ORDERED_BY_DATA),
    )(*bufs, sems[0], sems[1], after))


def _gather_copies(lands, send, recv):
    me = _index(_position())
    mine = [lambda k, ref=ref: ref.at[pl.ds(me, 1)] for ref in lands]
    return _split_copies(mine, mine, send, recv)


def _gather_arrivals(lands, send, recv):
    me = _index(_position())
    mine = [lambda k, ref=ref: ref.at[pl.ds(me, 1)] for ref in lands]
    theirs = [lambda k, ref=ref: ref.at[pl.ds(_index(_peer(k)), 1)] for ref in lands]
    return _split_copies(mine, theirs, send, recv)


def _exchange_copies(bufs, send, recv):
    half = len(bufs) // 2
    srcs = [lambda k, ref=ref: ref.at[pl.ds(_index(_peer(k)), 1)] for ref in bufs[:half]]
    dsts = [lambda k, ref=ref: ref.at[pl.ds(k - 1, 1)] for ref in bufs[half:]]
    return _split_copies(srcs, dsts, send, recv)


def gather_start(groups, name):
    return _start_call(groups, [len(grp) for grp in groups], _gather_copies, name)


def gather_wait(lands, sems, after, name):
    return _wait_call(lands, sems, _gather_arrivals, after, name)


def exchange_start(partials, name):
    lands = [lax.empty((N_DEV - 1,) + p.shape[1:], p.dtype) for p in partials]
    sems, bufs, token = _start_call([list(partials) + lands], [len(partials)], _exchange_copies, name)
    return sems[0], bufs, token


def exchange_wait(bufs, sems, after, name):
    bufs = _wait_call(bufs, sems, _exchange_copies, after, name)
    return bufs[:len(bufs) // 2], bufs[len(bufs) // 2:]


def prep_weight(w, layer, transpose, dev, name, dtype=BF16, after=()):
    L, K, n = w.shape
    block = (1, n, K) if transpose else (1, K, n)

    def body(d_ref, w_ref, *rest):
        o_ref = rest[-1]
        v = w_ref[0]
        o_ref[0] = (v.T if transpose else v).astype(dtype)

    return pl.pallas_call(
        body, name=name,
        grid_spec=pltpu.PrefetchScalarGridSpec(
            num_scalar_prefetch=1, grid=(1,),
            in_specs=[pl.BlockSpec((1, K, n), lambda i, d: (layer, 0, 0))]
            + [pl.BlockSpec(memory_space=pl.ANY)] * len(after),
            out_specs=pl.BlockSpec(block, lambda i, d: (d[0], 0, 0))),
        out_shape=jax.ShapeDtypeStruct((N_DEV,) + block[1:], dtype),
        compiler_params=_params("arbitrary"),
    )(dev, w, *after)


def _adamw_math(w, g, m, v):
    mn = ADAM_B1 * m + (1.0 - ADAM_B1) * g
    vn = ADAM_B2 * v + (1.0 - ADAM_B2) * (g * g)
    m_hat = mn * (1.0 / (1.0 - ADAM_B1 ** ADAM_STEP))
    v_hat = vn * (1.0 / (1.0 - ADAM_B2 ** ADAM_STEP))
    return -ADAM_LR * (m_hat / (jnp.sqrt(v_hat) + ADAM_EPS) + ADAM_WD * w), mn, vn


def sum_adamw(partial, landed, w, m, v, prev, layer, transpose, dev, name, after=()):
    _, r, c = partial.shape
    if transpose:
        tc = _tile(c, 256)
        grid = (c // tc,)
        part_spec = pl.BlockSpec((1, r, tc), lambda i, d: (d[0], 0, i))
        land_spec = pl.BlockSpec((N_DEV - 1, r, tc), lambda i, d: (0, 0, i))
        w_spec = pl.BlockSpec((1, tc, r), lambda i, d: (layer, i, 0))
    else:
        tr = r // 2 if r % 32 == 0 else r
        grid = (r // tr,)
        part_spec = pl.BlockSpec((1, tr, c), lambda i, d: (d[0], i, 0))
        land_spec = pl.BlockSpec((N_DEV - 1, tr, c), lambda i, d: (0, i, 0))
        w_spec = pl.BlockSpec((1, tr, c), lambda i, d: (layer, i, 0))

    def body(d_ref, p_ref, l_ref, w_ref, m_ref, v_ref, *rest):
        g_ref, dl_ref, mo_ref, vo_ref = rest[-4:]
        acc = p_ref[0].astype(F32)
        for k in range(N_DEV - 1):
            acc = acc + l_ref[k].astype(F32)
        g = acc.T if transpose else acc
        delta, mn, vn = _adamw_math(w_ref[0], g, m_ref[0], v_ref[0])
        g_ref[0], dl_ref[0], mo_ref[0], vo_ref[0] = g, delta, mn, vn

    return pl.pallas_call(
        body, name=name,
        grid_spec=pltpu.PrefetchScalarGridSpec(
            num_scalar_prefetch=1, grid=grid,
            in_specs=[part_spec, land_spec, w_spec, w_spec, w_spec]
            + [pl.BlockSpec(memory_space=pl.ANY)] * (4 + len(after)),
            out_specs=[w_spec] * 4),
        out_shape=[jax.ShapeDtypeStruct(w.shape, F32)] * 4,
        input_output_aliases={6 + q: q for q in range(4)},
        compiler_params=_params("parallel"),
    )(dev, partial, landed, w, m, v, *prev, *after)


def sum_parts(parts, transpose, name):
    L, P, r, c = parts.shape
    tr = _tile(r, 256) if (not transpose and r % 8 == 0) else r
    if r % tr:
        tr = r
    out_block = (1, c, tr) if transpose else (1, tr, c)
    out_map = (lambda l, i: (l, 0, i)) if transpose else (lambda l, i: (l, i, 0))

    def body(p_ref, o_ref):
        acc = p_ref[0, 0].astype(F32)
        for s in range(1, P):
            acc = acc + p_ref[0, s].astype(F32)
        o_ref[0] = acc.T if transpose else acc

    return pl.pallas_call(
        body, name=name, grid=(L, r // tr),
        in_specs=[pl.BlockSpec((1, P, tr, c), lambda l, i: (l, 0, i, 0))],
        out_specs=pl.BlockSpec(out_block, out_map),
        out_shape=jax.ShapeDtypeStruct((L, c, r) if transpose else (L, r, c), F32),
        compiler_params=_params("parallel", "parallel"),
    )(parts)


def adamw(w, g, m, v, name):
    R, C = w.shape
    tr = _tile(R, 512)

    def body(w_ref, g_ref, m_ref, v_ref, d_ref, mo_ref, vo_ref):
        d_ref[...], mo_ref[...], vo_ref[...] = _adamw_math(w_ref[...], g_ref[...], m_ref[...], v_ref[...])

    spec = pl.BlockSpec((tr, C), lambda i: (i, 0))
    return pl.pallas_call(
        body, name=name, grid=(R // tr,),
        in_specs=[spec] * 4, out_specs=[spec] * 3,
        out_shape=[jax.ShapeDtypeStruct((R, C), F32)] * 3,
        compiler_params=_params("parallel"),
    )(w, g, m, v)


def add_norm(h, y, g_post, g_pre, name):
    T, D = h.shape
    tm = _tile(T, 512)
    has_y = y is not None

    def body(*refs):
        if has_y:
            h_ref, y_ref, gp_ref, g_ref, ho_ref, hn_ref = refs
            hv = h_ref[...] + _rms(y_ref[...], gp_ref[...])
            ho_ref[...] = hv
        else:
            h_ref, g_ref, hn_ref = refs
            hv = h_ref[...]
        hn_ref[...] = _rms(hv, g_ref[...]).astype(BF16)

    row, vec = _rows(tm, D), _full((1, D))
    if has_y:
        return pl.pallas_call(
            body, name=name, grid=(T // tm,), in_specs=[row, row, vec, vec], out_specs=[row, row],
            out_shape=[jax.ShapeDtypeStruct((T, D), F32), jax.ShapeDtypeStruct((T, D), BF16)],
            compiler_params=_params("parallel"))(h, y, g_post, g_pre)
    hn = pl.pallas_call(
        body, name=name, grid=(T // tm,), in_specs=[row, vec], out_specs=row,
        out_shape=jax.ShapeDtypeStruct((T, D), BF16), compiler_params=_params("parallel"))(h, g_pre)
    return h, hn


def add_norm_loss(h, y, g_post, target, name):
    T, D = h.shape
    tm = _tile(T, 512)

    def body(h_ref, y_ref, gp_ref, t_ref, dh_ref, loss_ref):
        @pl.when(pl.program_id(0) == 0)
        def _():
            loss_ref[...] = jnp.zeros_like(loss_ref)
        err = h_ref[...] + _rms(y_ref[...], gp_ref[...]) - t_ref[...]
        dh_ref[...] = err * (1.0 / D)
        loss_ref[...] += jnp.sum(err * err)

    row = _rows(tm, D)
    return pl.pallas_call(
        body, name=name, grid=(T // tm,), in_specs=[row, row, _full((1, D)), row],
        out_specs=[row, _full((8, LANES))],
        out_shape=[jax.ShapeDtypeStruct((T, D), F32), jax.ShapeDtypeStruct((8, LANES), F32)],
        compiler_params=_params("arbitrary"))(h, y, g_post, target)


def norm_bwd(dh_in, pre, post, name):
    T, D = dh_in.shape
    tm = _tile(T, 512)
    row, vec = _rows(tm, D), _full((1, D))
    ins, in_specs, outs, out_specs, names = [dh_in], [row], [], [], []
    if pre is not None:
        ins += list(pre)
        in_specs += [row, row, vec]
        outs += [jax.ShapeDtypeStruct((T, D), F32), jax.ShapeDtypeStruct((1, D), F32)]
        out_specs += [row, vec]
        names += ["dh", "dg_pre"]
    if post is not None:
        ins += list(post)
        in_specs += [row, vec]
        outs += [jax.ShapeDtypeStruct((T, D), BF16), jax.ShapeDtypeStruct((1, D), F32)]
        out_specs += [row, vec]
        names += ["dy", "dg_post"]

    def body(*refs):
        refs = list(refs)
        first = pl.program_id(0) == 0
        dh = refs.pop(0)[...]
        if pre is not None:
            dpre_ref, x_ref, g_ref = refs[:3]
            refs = refs[3:]
        if post is not None:
            y_ref, gp_ref = refs[:2]
            refs = refs[2:]
        if pre is not None:
            dh_ref, dg_ref = refs[:2]
            refs = refs[2:]
            dx, dg = _rms_bwd(x_ref[...], g_ref[...], dpre_ref[...].astype(F32))
            dh = dh + dx
            dh_ref[...] = dh

            @pl.when(first)
            def _():
                dg_ref[...] = jnp.zeros_like(dg_ref)
            dg_ref[...] += dg
        if post is not None:
            dy_ref, dgp_ref = refs[:2]
            dy, dgp = _rms_bwd(y_ref[...], gp_ref[...], dh)
            dy_ref[...] = dy.astype(BF16)

            @pl.when(first)
            def _():
                dgp_ref[...] = jnp.zeros_like(dgp_ref)
            dgp_ref[...] += dgp

    res = pl.pallas_call(
        body, name=name, grid=(T // tm,), in_specs=in_specs, out_specs=out_specs, out_shape=outs,
        compiler_params=_params("arbitrary"))(*ins)
    return dict(zip(names, res))


_NT = (((1,), (1,)), ((), ()))
_TN = (((0,), (0,)), ((), ()))


def rowmm(a, w, transposed_w, bias, out_dtype, name, after=None):
    M, K = a.shape
    N = w.shape[0] if transposed_w else w.shape[1]
    tm = _tile(M, 256)
    tn = _tile(N, 512)

    def body(*refs):
        a_ref, w_ref = refs[:2]
        o_ref = refs[-1]
        av = a_ref[...]
        for j in range(N // tn):
            cols = pl.ds(j * tn, tn)
            if transposed_w:
                acc = lax.dot_general(av, w_ref[cols, :], _NT, preferred_element_type=F32)
            else:
                acc = jnp.dot(av, w_ref[:, cols], preferred_element_type=F32)
            if bias is not None:
                acc = acc + refs[2][:, cols]
            o_ref[:, cols] = acc.astype(out_dtype)

    ins, in_specs = [a, w], [_rows(tm, K), _full(w.shape)]
    if bias is not None:
        ins.append(bias)
        in_specs.append(_full((1, N)))
    if after is not None:
        ins.append(after)
        in_specs.append(pl.BlockSpec(memory_space=pl.ANY))
    return pl.pallas_call(
        body, name=name, grid=(M // tm,), in_specs=in_specs, out_specs=_rows(tm, N),
        out_shape=jax.ShapeDtypeStruct((M, N), out_dtype), compiler_params=_params("parallel"))(*ins)


def grad_mm(a, b, name):
    T, N = a.shape
    K = b.shape[1]
    tn = _tile(N, 1408)
    tt = _tile(T, 1024)

    def body(a_ref, b_ref, o_ref, acc_ref):
        t = pl.program_id(1)

        @pl.when(t == 0)
        def _():
            acc_ref[...] = jnp.zeros_like(acc_ref)
        acc_ref[...] += lax.dot_general(a_ref[...], b_ref[...], _TN, preferred_element_type=F32)

        @pl.when(t == pl.num_programs(1) - 1)
        def _():
            o_ref[...] = acc_ref[...].astype(BF16)

    return pl.pallas_call(
        body, name=name, grid=(N // tn, T // tt),
        in_specs=[pl.BlockSpec((tt, tn), lambda j, t: (t, j)), pl.BlockSpec((tt, K), lambda j, t: (t, 0))],
        out_specs=pl.BlockSpec((tn, K), lambda j, t: (j, 0)),
        out_shape=jax.ShapeDtypeStruct((N, K), BF16),
        scratch_shapes=[pltpu.VMEM((tn, K), F32)],
        compiler_params=_params("parallel", "arbitrary"))(a, b)


def ffn_up(fn, w_gu_t, name):
    T, D = fn.shape
    F = w_gu_t.shape[0] // 2
    tm = _tile(T, 256)
    tn = _tile(F, 512)

    def body(a_ref, w_ref, gu_ref, act_ref):
        av = a_ref[...]
        for j in range(F // tn):
            g = lax.dot_general(av, w_ref[pl.ds(j * tn, tn), :], _NT, preferred_element_type=F32)
            up = lax.dot_general(av, w_ref[pl.ds(F + j * tn, tn), :], _NT, preferred_element_type=F32)
            gu_ref[:, pl.ds(j * tn, tn)] = g.astype(BF16)
            gu_ref[:, pl.ds(F + j * tn, tn)] = up.astype(BF16)
            act_ref[:, pl.ds(j * tn, tn)] = (g * _sigmoid(g) * up).astype(BF16)

    return pl.pallas_call(
        body, name=name, grid=(T // tm,), in_specs=[_rows(tm, D), _full(w_gu_t.shape)],
        out_specs=[_rows(tm, 2 * F), _rows(tm, F)],
        out_shape=[jax.ShapeDtypeStruct((T, 2 * F), BF16), jax.ShapeDtypeStruct((T, F), BF16)],
        compiler_params=_params("parallel"))(fn, w_gu_t)


def ffn_dact(df, w_down, gu, name):
    T, D = df.shape
    F = w_down.shape[0]
    tm = _tile(T, 256)
    tn = _tile(F, 512)

    def body(d_ref, w_ref, gu_ref, o_ref):
        dv = d_ref[...]
        for j in range(F // tn):
            dact = lax.dot_general(dv, w_ref[pl.ds(j * tn, tn), :], _NT, preferred_element_type=F32)
            g = gu_ref[:, pl.ds(j * tn, tn)].astype(F32)
            up = gu_ref[:, pl.ds(F + j * tn, tn)].astype(F32)
            sg = _sigmoid(g)
            o_ref[:, pl.ds(j * tn, tn)] = (dact * up * (sg * (1.0 + g * (1.0 - sg)))).astype(BF16)
            o_ref[:, pl.ds(F + j * tn, tn)] = (dact * g * sg).astype(BF16)

    return pl.pallas_call(
        body, name=name, grid=(T // tm,), in_specs=[_rows(tm, D), _full(w_down.shape), _rows(tm, 2 * F)],
        out_specs=_rows(tm, 2 * F), out_shape=jax.ShapeDtypeStruct((T, 2 * F), BF16),
        compiler_params=_params("parallel"))(df, w_down, gu)


def _layernorm_stats(v):
    mu = jnp.mean(v, axis=-1, keepdims=True)
    cen = v - mu
    rstd = lax.rsqrt(jnp.mean(cen * cen, axis=-1, keepdims=True) + LN_EPS)
    return cen * rstd, rstd


def gmlp_fwd(a, ln_g, ln_b, w_mask, b_s_t, name):
    T, W2 = a.shape
    W = W2 // 2
    G = w_mask.shape[0]
    C = W // G

    def body(a_ref, g_ref, b_ref, w_ref, bs_ref, o_ref):
        xhat, _ = _layernorm_stats(_gelu(a_ref[:, W:].astype(F32)))
        vln = (xhat * g_ref[...] + b_ref[...]).astype(BF16)
        for g in range(G):
            cols = pl.ds(g * C, C)
            sv = jnp.dot(w_ref[g], vln[:, g * C:(g + 1) * C], preferred_element_type=F32) + bs_ref[:, g:g + 1]
            o_ref[:, cols] = (_gelu(a_ref[:, cols].astype(F32)) * sv).astype(BF16)

    return pl.pallas_call(
        body, name=name, grid=(T // CHUNK,),
        in_specs=[_rows(CHUNK, W2), _full((1, W)), _full((1, W)), _full(w_mask.shape), _full(b_s_t.shape)],
        out_specs=_rows(CHUNK, W), out_shape=jax.ShapeDtypeStruct((T, W), BF16),
        compiler_params=_params("parallel"))(a, ln_g, ln_b, w_mask, b_s_t)


def gmlp_bwd(a, dgated, ln_g, ln_b, w_mask, w_mask_t, b_s_t, group_onehot, name):
    T, W2 = a.shape
    W = W2 // 2
    G = w_mask.shape[0]
    C = W // G

    def body(a_ref, dg_ref, g_ref, b_ref, w_ref, wt_ref, bs_ref, e_ref,
             da_ref, dws_ref, dbs_ref, dlg_ref, dlb_ref, dbin_ref, dvln_ref):
        @pl.when(pl.program_id(0) == 0)
        def _():
            for r in (dws_ref, dbs_ref, dlg_ref, dlb_ref, dbin_ref):
                r[...] = jnp.zeros_like(r)
        row = lax.broadcasted_iota(jnp.int32, (CHUNK, CHUNK), 0)
        col = lax.broadcasted_iota(jnp.int32, (CHUNK, CHUNK), 1)
        causal = col <= row
        av = a_ref[:, W:].astype(F32)
        xhat, rstd = _layernorm_stats(_gelu(av))
        vln = (xhat * g_ref[...] + b_ref[...]).astype(BF16)
        dsv_all = (dg_ref[...].astype(F32) * _gelu(a_ref[:, :W].astype(F32))).astype(BF16)
        dbs_ref[...] += jnp.dot(dsv_all, e_ref[...], preferred_element_type=F32)
        for g in range(G):
            cols = pl.ds(g * C, C)
            vg = vln[:, g * C:(g + 1) * C]
            dsv = dsv_all[:, g * C:(g + 1) * C]
            sv = jnp.dot(w_ref[g], vg, preferred_element_type=F32) + bs_ref[:, g:g + 1]
            au = a_ref[:, cols].astype(F32)
            dau = dg_ref[:, cols].astype(F32) * sv * _gelu_grad(au)
            da_ref[:, cols] = dau.astype(BF16)
            dbin_ref[:, cols] += jnp.sum(dau, axis=0, keepdims=True)
            dws_ref[g] += jnp.where(causal, lax.dot_general(dsv, vg, _NT, preferred_element_type=F32), 0.0)
            dvln_ref[:, cols] = jnp.dot(wt_ref[g], dsv, preferred_element_type=F32)
        dvln = dvln_ref[...]
        dlg_ref[...] += jnp.sum(dvln * xhat, axis=0, keepdims=True)
        dlb_ref[...] += jnp.sum(dvln, axis=0, keepdims=True)
        dxhat = dvln * g_ref[...]
        dv = rstd * (dxhat - jnp.mean(dxhat, axis=-1, keepdims=True)
                     - xhat * jnp.mean(dxhat * xhat, axis=-1, keepdims=True))
        dav = dv * _gelu_grad(av)
        da_ref[:, W:] = dav.astype(BF16)
        dbin_ref[:, W:] += jnp.sum(dav, axis=0, keepdims=True)

    return pl.pallas_call(
        body, name=name, grid=(T // CHUNK,),
        in_specs=[_rows(CHUNK, W2), _rows(CHUNK, W), _full((1, W)), _full((1, W)), _full(w_mask.shape),
                  _full(w_mask_t.shape), _full(b_s_t.shape), _full(group_onehot.shape)],
        out_specs=[_rows(CHUNK, W2), _full((G, CHUNK, CHUNK)), _full((CHUNK, LANES)), _full((1, W)),
                   _full((1, W)), _full((1, W2))],
        out_shape=[jax.ShapeDtypeStruct((T, W2), BF16), jax.ShapeDtypeStruct((G, CHUNK, CHUNK), F32),
                   jax.ShapeDtypeStruct((CHUNK, LANES), F32), jax.ShapeDtypeStruct((1, W), F32),
                   jax.ShapeDtypeStruct((1, W), F32), jax.ShapeDtypeStruct((1, W2), F32)],
        scratch_shapes=[pltpu.VMEM((CHUNK, W), F32)],
        compiler_params=_params("arbitrary"))(a, dgated, ln_g, ln_b, w_mask, w_mask_t, b_s_t, group_onehot)


def rope_tables(pos, inv_freq_row, name):
    T = pos.shape[0]
    tm = _tile(T, 512)

    def body(p_ref, f_ref, c_ref, s1_ref, s2_ref):
        ang = p_ref[...].astype(F32) * f_ref[...]
        lane = lax.broadcasted_iota(jnp.int32, (tm, LANES), 1) % HEAD_DIM
        sin = jnp.sin(ang)
        c_ref[...] = jnp.cos(ang)
        s1_ref[...] = jnp.where(lane < _HALF, -sin, 0.0)
        s2_ref[...] = jnp.where((lane >= _HALF) & (lane < ROPE_DIM), sin, 0.0)

    tab = _rows(tm, LANES)
    return pl.pallas_call(
        body, name=name, grid=(T // tm,), in_specs=[_rows(tm, 1), _full((1, LANES))], out_specs=[tab] * 3,
        out_shape=[jax.ShapeDtypeStruct((T, LANES), F32)] * 3, compiler_params=_params("parallel"))(pos, inv_freq_row)


_HALF = ROPE_DIM // 2


def _slabs(x):
    return [x[:, b * LANES:(b + 1) * LANES] for b in range(x.shape[1] // LANES)]


def _rotate(x, c, s1, s2):
    return [xs * c + pltpu.roll(xs, LANES - _HALF, 1) * s1 + pltpu.roll(xs, _HALF, 1) * s2 for xs in _slabs(x)]


def _rotate_transposed(dy, c, s1, s2):
    return [ds * c + pltpu.roll(ds * s1, _HALF, 1) + pltpu.roll(ds * s2, LANES - _HALF, 1) for ds in _slabs(dy)]


def rope_fwd(qkv, tabs, q_width, kv_width, name):
    T, QKV = qkv.shape
    tm = _tile(T, 512)
    QK = q_width + kv_width
    scale = HEAD_DIM ** -0.5

    def body(x_ref, c_ref, s1_ref, s2_ref, o_ref):
        x = x_ref[:, :QK].astype(F32)
        slabs = _rotate(x, c_ref[...], s1_ref[...], s2_ref[...])
        for b, y in enumerate(slabs):
            if b * LANES < q_width:
                y = y * scale
            o_ref[:, b * LANES:(b + 1) * LANES] = y.astype(BF16)

    tab = _rows(tm, LANES)
    return pl.pallas_call(
        body, name=name, grid=(T // tm,), in_specs=[_rows(tm, QKV), tab, tab, tab], out_specs=_rows(tm, QK),
        out_shape=jax.ShapeDtypeStruct((T, QK), BF16), compiler_params=_params("parallel"))(qkv, *tabs)


def rope_bwd(dq, dk, dv, tabs, name):
    T, Q = dq.shape
    KV = dk.shape[1]
    tm = _tile(T, 512)
    scale = HEAD_DIM ** -0.5

    def body(dq_ref, dk_ref, dv_ref, c_ref, s1_ref, s2_ref, o_ref, b_ref):
        @pl.when(pl.program_id(0) == 0)
        def _():
            b_ref[...] = jnp.zeros_like(b_ref)
        tabs_v = (c_ref[...], s1_ref[...], s2_ref[...])
        pieces = [s * scale for s in _rotate_transposed(dq_ref[...], *tabs_v)]
        pieces += _rotate_transposed(dk_ref[...], *tabs_v)
        pieces += _slabs(dv_ref[...])
        for b, y in enumerate(pieces):
            cols = pl.ds(b * LANES, LANES)
            o_ref[:, cols] = y.astype(BF16)
            b_ref[:, cols] += jnp.sum(y, axis=0, keepdims=True)

    tab = _rows(tm, LANES)
    return pl.pallas_call(
        body, name=name, grid=(T // tm,), in_specs=[_rows(tm, Q), _rows(tm, KV), _rows(tm, KV), tab, tab, tab],
        out_specs=[_rows(tm, Q + 2 * KV), _full((1, Q + 2 * KV))],
        out_shape=[jax.ShapeDtypeStruct((T, Q + 2 * KV), BF16), jax.ShapeDtypeStruct((1, Q + 2 * KV), F32)],
        compiler_params=_params("arbitrary"))(dq, dk, dv, *tabs)


def _attn_block(n, q_ref, kc_ref, kp_ref, vc_ref, vp_ref, sink_ref, kh, group):
    GR = group * CHUNK
    lanes = slice(kh * HEAD_DIM, (kh + 1) * HEAD_DIM)
    kk = jnp.concatenate([kp_ref[:, lanes], kc_ref[:, lanes]], axis=0)
    vv = jnp.concatenate([vp_ref[:, lanes], vc_ref[:, lanes]], axis=0)
    qg = jnp.concatenate(
        [q_ref[:, (kh * group + g) * HEAD_DIM:(kh * group + g + 1) * HEAD_DIM] for g in range(group)], axis=0)
    s = lax.dot_general(qg, kk, _NT, preferred_element_type=F32)
    qi = lax.broadcasted_iota(jnp.int32, (GR, 2 * CHUNK), 0) % CHUNK
    sj = lax.broadcasted_iota(jnp.int32, (GR, 2 * CHUNK), 1)
    valid = (sj > qi) & (sj <= qi + CHUNK) & ((n > 0) | (sj >= CHUNK))
    s = jnp.where(valid, s, NEG_INF)
    head = lax.broadcasted_iota(jnp.int32, (GR, 1), 0) // CHUNK
    sink = jnp.zeros((GR, 1), F32)
    for g in range(group):
        sink = jnp.where(head == g, sink_ref[kh * group + g], sink)
    m = jnp.maximum(jnp.max(s, axis=-1, keepdims=True), sink)
    p = jnp.exp(s - m)
    e_sink = jnp.exp(sink - m)
    inv = 1.0 / (jnp.sum(p, axis=-1, keepdims=True) + e_sink)
    return qg, kk, vv, p * inv, e_sink * inv


def _attn_specs(q_width, kv_width, order):
    qb = q_width // kv_width
    prev = lambda i: jnp.maximum(order(i) - 1, 0)
    return [pl.BlockSpec((CHUNK, q_width), lambda i: (order(i), 0)),
            pl.BlockSpec((CHUNK, kv_width), lambda i: (order(i), qb)),
            pl.BlockSpec((CHUNK, kv_width), lambda i: (prev(i), qb)),
            pl.BlockSpec((CHUNK, kv_width), lambda i: (order(i), qb + 1)),
            pl.BlockSpec((CHUNK, kv_width), lambda i: (prev(i), qb + 1))]


def attn_fwd(qk, qkv, sinks, q_width, kv_width, name):
    T = qk.shape[0]
    group = q_width // kv_width

    def body(q_ref, kc_ref, kp_ref, vc_ref, vp_ref, sink_ref, o_ref):
        n = pl.program_id(0)
        for kh in range(kv_width // HEAD_DIM):
            _, _, vv, p, _ = _attn_block(n, q_ref, kc_ref, kp_ref, vc_ref, vp_ref, sink_ref, kh, group)
            og = jnp.dot(p.astype(BF16), vv, preferred_element_type=F32)
            for g in range(group):
                h = kh * group + g
                o_ref[:, h * HEAD_DIM:(h + 1) * HEAD_DIM] = og[g * CHUNK:(g + 1) * CHUNK].astype(BF16)

    specs = _attn_specs(q_width, kv_width, lambda i: i)
    return pl.pallas_call(
        body, name=name, grid=(T // CHUNK,),
        in_specs=specs + [pl.BlockSpec(memory_space=pltpu.SMEM)],
        out_specs=_rows(CHUNK, q_width), out_shape=jax.ShapeDtypeStruct((T, q_width), BF16),
        compiler_params=_params("parallel"))(qk, qk, qk, qkv, qkv, sinks)


def attn_bwd(qk, qkv, do, sinks, q_width, kv_width, name):
    T = qk.shape[0]
    NB = T // CHUNK
    group = q_width // kv_width

    def body(q_ref, kc_ref, kp_ref, vc_ref, vp_ref, do_ref, sink_ref, dq_ref, dk_ref, dv_ref, ds_ref, ck_ref, cv_ref):
        i = pl.program_id(0)
        n = NB - 1 - i

        @pl.when(i == 0)
        def _():
            ck_ref[...] = jnp.zeros_like(ck_ref)
            cv_ref[...] = jnp.zeros_like(cv_ref)
            ds_ref[...] = jnp.zeros_like(ds_ref)
        lane = lax.broadcasted_iota(jnp.int32, (1, LANES), 1)
        dsink_row = jnp.zeros((1, LANES), F32)
        for kh in range(kv_width // HEAD_DIM):
            qg, kk, vv, p, p_sink = _attn_block(n, q_ref, kc_ref, kp_ref, vc_ref, vp_ref, sink_ref, kh, group)
            dog = jnp.concatenate(
                [do_ref[:, (kh * group + g) * HEAD_DIM:(kh * group + g + 1) * HEAD_DIM] for g in range(group)], axis=0)
            dp = lax.dot_general(dog, vv, _NT, preferred_element_type=F32)
            delta = jnp.sum(p * dp, axis=-1, keepdims=True)
            ds = (p * (dp - delta)).astype(BF16)
            dsink = -p_sink * delta
            for g in range(group):
                h = kh * group + g
                dsink_row = dsink_row + jnp.where(lane == h, jnp.sum(dsink[g * CHUNK:(g + 1) * CHUNK]), 0.0)
            dqg = jnp.dot(ds, kk, preferred_element_type=F32)
            for g in range(group):
                h = kh * group + g
                dq_ref[:, h * HEAD_DIM:(h + 1) * HEAD_DIM] = dqg[g * CHUNK:(g + 1) * CHUNK]
            dkk = lax.dot_general(ds, qg, _TN, preferred_element_type=F32)
            dvv = lax.dot_general(p.astype(BF16), dog, _TN, preferred_element_type=F32)
            lanes = slice(kh * HEAD_DIM, (kh + 1) * HEAD_DIM)
            dk_ref[:, lanes] = dkk[CHUNK:] + ck_ref[:, lanes]
            dv_ref[:, lanes] = dvv[CHUNK:] + cv_ref[:, lanes]
            ck_ref[:, lanes] = dkk[:CHUNK]
            cv_ref[:, lanes] = dvv[:CHUNK]
        ds_ref[0:1, :] += dsink_row

    order = lambda i: NB - 1 - i
    specs = _attn_specs(q_width, kv_width, order)
    kv_out = pl.BlockSpec((CHUNK, kv_width), lambda i: (order(i), 0))
    q_rows = pl.BlockSpec((CHUNK, q_width), lambda i: (order(i), 0))
    return pl.pallas_call(
        body, name=name, grid=(NB,),
        in_specs=specs + [q_rows, pl.BlockSpec(memory_space=pltpu.SMEM)],
        out_specs=[q_rows, kv_out, kv_out, _full((8, LANES))],
        out_shape=[jax.ShapeDtypeStruct((T, q_width), F32), jax.ShapeDtypeStruct((T, kv_width), F32),
                   jax.ShapeDtypeStruct((T, kv_width), F32), jax.ShapeDtypeStruct((8, LANES), F32)],
        scratch_shapes=[pltpu.VMEM((CHUNK, kv_width), F32), pltpu.VMEM((CHUNK, kv_width), F32)],
        compiler_params=_params("arbitrary"))(qk, qk, qk, qkv, qkv, do, sinks)


def _blocked(w):
    return w.reshape(N_DEV, w.shape[0] // N_DEV, w.shape[1])


def kernel(x, positions, pre_mix_g, post_mix_g, pre_ffn_g, post_ffn_g, a_w_in, a_b_in, a_ln_g, a_ln_b, a_w_s, a_b_s, a_w_out, b_w_qkv, b_b_qkv, b_sinks, b_w_o, ffn_w_gu, ffn_w_down, loss_target, m_pre_mix_g, m_post_mix_g, m_pre_ffn_g, m_post_ffn_g, m_a_w_in, m_a_b_in, m_a_ln_g, m_a_ln_b, m_a_w_s, m_a_b_s, m_a_w_out, m_b_w_qkv, m_b_b_qkv, m_b_sinks, m_b_w_o, m_ffn_w_gu, m_ffn_w_down, v_pre_mix_g, v_post_mix_g, v_pre_ffn_g, v_post_ffn_g, v_a_w_in, v_a_b_in, v_a_ln_g, v_a_ln_b, v_a_w_s, v_a_b_s, v_a_w_out, v_b_w_qkv, v_b_b_qkv, v_b_sinks, v_b_w_o, v_ffn_w_gu, v_ffn_w_down):
    weights = dict(pre_mix_g=pre_mix_g, post_mix_g=post_mix_g, pre_ffn_g=pre_ffn_g, post_ffn_g=post_ffn_g,
                   a_w_in=a_w_in, a_b_in=a_b_in, a_ln_g=a_ln_g, a_ln_b=a_ln_b, a_w_s=a_w_s, a_b_s=a_b_s,
                   a_w_out=a_w_out, b_w_qkv=b_w_qkv, b_b_qkv=b_b_qkv, b_sinks=b_sinks, b_w_o=b_w_o,
                   ffn_w_gu=ffn_w_gu, ffn_w_down=ffn_w_down)
    mom_m = dict(pre_mix_g=m_pre_mix_g, post_mix_g=m_post_mix_g, pre_ffn_g=m_pre_ffn_g, post_ffn_g=m_post_ffn_g,
                 a_w_in=m_a_w_in, a_b_in=m_a_b_in, a_ln_g=m_a_ln_g, a_ln_b=m_a_ln_b, a_w_s=m_a_w_s, a_b_s=m_a_b_s,
                 a_w_out=m_a_w_out, b_w_qkv=m_b_w_qkv, b_b_qkv=m_b_b_qkv, b_sinks=m_b_sinks, b_w_o=m_b_w_o,
                 ffn_w_gu=m_ffn_w_gu, ffn_w_down=m_ffn_w_down)
    mom_v = dict(pre_mix_g=v_pre_mix_g, post_mix_g=v_post_mix_g, pre_ffn_g=v_pre_ffn_g, post_ffn_g=v_post_ffn_g,
                 a_w_in=v_a_w_in, a_b_in=v_a_b_in, a_ln_g=v_a_ln_g, a_ln_b=v_a_ln_b, a_w_s=v_a_w_s, a_b_s=v_a_b_s,
                 a_w_out=v_a_w_out, b_w_qkv=v_b_w_qkv, b_b_qkv=v_b_b_qkv, b_sinks=v_b_sinks, b_w_o=v_b_w_o,
                 ffn_w_gu=v_ffn_w_gu, ffn_w_down=v_ffn_w_down)
    names = list(weights)
    column_sharded = ("a_w_in", "b_w_qkv", "ffn_w_gu")
    row_sharded = ("a_w_out", "b_w_o", "ffn_w_down")
    big = column_sharded + row_sharded

    T, D = x.shape[1], x.shape[2]
    depth = pre_mix_g.shape[0]
    n_heads = b_sinks.shape[1]
    q_width = n_heads * HEAD_DIM
    kv_width = N_KV_HEADS * HEAD_DIM
    G = a_w_s.shape[1]
    W = a_ln_g.shape[1]
    device = _index(_position())

    dev = device.reshape(1).astype(jnp.int32)

    def layer_keys(i):
        mixer = ("a_w_in", "a_w_out") if i % 2 == 0 else ("b_w_qkv", "b_w_o")
        return [(k, i // 2) for k in mixer] + [("ffn_w_gu", i), ("ffn_w_down", i)]

    as_view = ("b_w_qkv", "ffn_w_gu")
    in_kernel = ("a_w_in",)
    view = lambda k, t: jnp.swapaxes(t, 1, 2) if k in as_view else t
    w_view = {k: view(k, weights[k]) for k in big}
    m_view = {k: view(k, mom_m[k]) for k in big}
    v_view = {k: view(k, mom_v[k]) for k in big}

    order = [(i, k, l) for i in range(depth) for k, l in layer_keys(i)]
    bias_land = lax.dynamic_update_slice(
        jnp.zeros((N_DEV,) + b_b_qkv.shape, F32), b_b_qkv[None], (device, 0, 0))
    first = len(layer_keys(0))
    in_gather, token = {}, ()
    for name, lo, hi in (("gather_start_first", 0, first), ("gather_start_rest", first, len(order))):
        groups = [[prep_weight(w_view[k], l, k in in_kernel, dev, "prep_" + k, after=token)] for _, k, l in order[lo:hi]]
        if lo:
            groups[0].append(bias_land)
        gather_sems, flat_lands, token = gather_start(groups, name)
        token = (token,)
        for (i, k, _), grp, sems in zip(order[lo:hi], groups, gather_sems):
            in_gather[(i, k)] = ([flat_lands.pop(0) for _ in grp], sems)
    gathered = {}

    def weight(i, k, after=None):
        if (i, k) not in gathered:
            lands, sems = in_gather[(i, k)]
            gathered[(i, k)] = gather_wait(lands, sems, after, f"gather_wait_{k}_{i}")
        g = gathered[(i, k)][0]
        return g.reshape(N_DEV * g.shape[1], g.shape[2])

    causal = jnp.tril(jnp.ones((CHUNK, CHUNK), dtype=bool))
    w_mask = jnp.where(causal[None, None], a_w_s, 0.0).astype(BF16)
    w_mask_t = jnp.swapaxes(w_mask, 2, 3)
    b_s_t = jnp.swapaxes(a_b_s, 1, 2)
    group_onehot = (jnp.arange(W)[:, None] // (W // G) == jnp.arange(LANES)[None, :]).astype(BF16)
    lane = jnp.arange(LANES) % HEAD_DIM
    inv_freq = ROPE_THETA ** (-jnp.arange(0, ROPE_DIM, 2, dtype=F32) / ROPE_DIM)
    inv_freq_row = jnp.where(lane < ROPE_DIM, inv_freq[lane % (ROPE_DIM // 2)], 0.0)[None, :].astype(F32)
    tabs = rope_tables(positions.reshape(T, 1), inv_freq_row, "rope_tables")

    saved = []
    h, hn = add_norm(x[0], None, None, pre_mix_g[0][None], "norm_first")
    for i in range(depth):
        j = i // 2
        s = dict(h=h, hn=hn)
        if i % 2 == 0:
            s["a"] = rowmm(hn, weight(i, "a_w_in", hn), True, a_b_in[j][None], BF16, "gmlp_in")
            s["gated"] = gmlp_fwd(s["a"], a_ln_g[j][None], a_ln_b[j][None], w_mask[j], b_s_t[j], "gmlp_gate")
            mix = rowmm(s["gated"], weight(i, "a_w_out", s["gated"]), False, None, F32, "gmlp_out")
        else:
            w_qkv = weight(i, "b_w_qkv", hn)
            b_qkv_full = jnp.swapaxes(gathered[(1, "b_w_qkv")][1], 0, 1).reshape(b_b_qkv.shape[0], 1, -1)
            s["qkv"] = rowmm(hn, w_qkv, True, b_qkv_full[j], BF16, "attn_qkv")
            s["qk"] = rope_fwd(s["qkv"], tabs, q_width, kv_width, "attn_rope")
            s["o"] = attn_fwd(s["qk"], s["qkv"], b_sinks[j], q_width, kv_width, "attn_core")
            mix = rowmm(s["o"], weight(i, "b_w_o", s["o"]), False, None, F32, "attn_out")
        s["mix"] = mix
        s["h2"], s["fn"] = add_norm(h, mix, post_mix_g[i][None], pre_ffn_g[i][None], "norm_mid")
        s["gu"], s["act"] = ffn_up(s["fn"], weight(i, "ffn_w_gu", s["fn"]), "ffn_up")
        s["f"] = rowmm(s["act"], weight(i, "ffn_w_down", s["act"]), False, None, F32, "ffn_down")
        saved.append(s)
        if i + 1 < depth:
            h, hn = add_norm(s["h2"], s["f"], post_ffn_g[i][None], pre_mix_g[i + 1][None], "norm_next")
    last = saved[-1]
    dh, loss_sum = add_norm_loss(last["h2"], last["f"], post_ffn_g[depth - 1][None], loss_target[0], "loss")
    loss = lax.psum(loss_sum[0, 0] * (0.5 / D), ("x", "y", "c"))

    small = {k: [None] * weights[k].shape[0] for k in names if k not in big}
    grads = {k: [None] * weights[k].shape[0] for k in big}
    in_flight = []

    def send_grads(keys, tag):
        sems, bufs, token = exchange_start([_blocked(grads[k][l]) for k, l in keys], "exchange_start_" + tag)
        in_flight.append((keys, sems, bufs, tag))
        return token

    r = norm_bwd(dh, None, (last["f"], post_ffn_g[depth - 1][None]), "norm_bwd_last")
    df, small["post_ffn_g"][depth - 1] = r["dy"], r["dg_post"]
    for i in reversed(range(depth)):
        j = i // 2
        s = saved[i]
        dgu = ffn_dact(df, weight(i, "ffn_w_down"), s["gu"], "ffn_dact")
        grads["ffn_w_down"][i] = grad_mm(s["act"], df, "ffn_down_grad")
        grads["ffn_w_gu"][i] = grad_mm(dgu, s["fn"], "ffn_up_grad")
        sent = send_grads(layer_keys(i)[2:], f"ffn_{i}")
        dfn = rowmm(dgu, weight(i, "ffn_w_gu"), False, None, F32, "ffn_dx", after=sent)
        r = norm_bwd(dh, (dfn, s["h2"], pre_ffn_g[i][None]), (s["mix"], post_mix_g[i][None]), "norm_bwd_mid")
        dh, dmix = r["dh"], r["dy"]
        small["pre_ffn_g"][i], small["post_mix_g"][i] = r["dg_pre"], r["dg_post"]
        if i % 2 == 0:
            grads["a_w_out"][j] = grad_mm(s["gated"], dmix, "gmlp_out_grad")
            dgated = rowmm(dmix, weight(i, "a_w_out"), True, None, BF16, "gmlp_dgated")
            da, dws, dbs, dlg, dlb, dbin = gmlp_bwd(
                s["a"], dgated, a_ln_g[j][None], a_ln_b[j][None], w_mask[j], w_mask_t[j], b_s_t[j], group_onehot,
                "gmlp_gate_bwd")
            small["a_w_s"][j], small["a_b_s"][j] = dws, dbs[:, :G].T
            small["a_ln_g"][j], small["a_ln_b"][j], small["a_b_in"][j] = dlg[0], dlb[0], dbin[0]
            grads["a_w_in"][j] = grad_mm(da, s["hn"], "gmlp_in_grad")
            sent = send_grads(layer_keys(i)[:2], f"mixer_{i}")
            dhn = rowmm(da, weight(i, "a_w_in"), False, None, F32, "gmlp_dx", after=sent)
        else:
            grads["b_w_o"][j] = grad_mm(s["o"], dmix, "attn_out_grad")
            do = rowmm(dmix, weight(i, "b_w_o"), True, None, BF16, "attn_do")
            dq, dk, dv, dsink = attn_bwd(s["qk"], s["qkv"], do, b_sinks[j], q_width, kv_width, "attn_core_bwd")
            dqkv, dbias = rope_bwd(dq, dk, dv, tabs, "attn_rope_bwd")
            small["b_sinks"][j], small["b_b_qkv"][j] = dsink[0, :n_heads], dbias[0]
            grads["b_w_qkv"][j] = grad_mm(dqkv, s["hn"], "attn_qkv_grad")
            sent = send_grads(layer_keys(i)[:2], f"mixer_{i}")
            dhn = rowmm(dqkv, weight(i, "b_w_qkv"), False, None, F32, "attn_dx", after=sent)
        if i > 0:
            prev = saved[i - 1]
            r = norm_bwd(dh, (dhn, s["h"], pre_mix_g[i][None]), (prev["f"], post_ffn_g[i - 1][None]), "norm_bwd_next")
            dh, df = r["dh"], r["dy"]
            small["pre_mix_g"][i], small["post_ffn_g"][i - 1] = r["dg_pre"], r["dg_post"]
        else:
            r = norm_bwd(dh, (dhn, s["h"], pre_mix_g[0][None]), None, "norm_bwd_first")
            dh, small["pre_mix_g"][0] = r["dh"], r["dg_pre"]
    grad_x = dh[None]

    small_names = [k for k in names if k not in big]
    partial = {k: jnp.stack([p.reshape(-1) for p in small[k]]) for k in small_names}
    sizes = [partial[k].size for k in small_names]
    total = sum(sizes)
    padded = -(-total // (LANES * LANES)) * (LANES * LANES)
    pack = lambda vals: jnp.concatenate([v.reshape(-1) for v in vals] + [jnp.zeros((padded - total,), F32)])
    packed = pack([partial[k] for k in small_names]).reshape(1, padded // LANES, LANES)
    small_sems, small_lands, small_sent = gather_start(
        [[prep_weight(packed, 0, False, dev, "place_small_grads", F32)]], "gather_start_small")

    stacked = {k: [lax.empty(w_view[k].shape, F32) for _ in range(4)] for k in big}
    after = (small_sent,)
    for keys, sems, bufs, tag in in_flight:
        partials, landed = exchange_wait(bufs, sems, dh, "exchange_wait_" + tag)
        for (k, l), part, land in zip(keys, partials, landed):
            stacked[k] = sum_adamw(part, land, w_view[k], m_view[k], v_view[k], stacked[k], l,
                                   k in in_kernel, dev, "adamw_" + k, after=after)
            after = ()
            last_update = stacked[k][0]
    out_g, out_d, out_m, out_v = {}, {}, {}, {}
    for k in big:
        out_g[k], out_d[k], out_m[k], out_v[k] = [view(k, t) for t in stacked[k]]

    gathered_small = gather_wait(small_lands, small_sems[0], last_update, "gather_wait_small")[0]
    summed = sum_parts(gathered_small[None], False, "sum_small")[0].reshape(-1)
    offsets = [sum(sizes[:q]) for q in range(len(sizes))]
    g_small = {}
    for k, off, size in zip(small_names, offsets, sizes):
        g = summed[off:off + size]
        if k == "b_b_qkv":
            n_local = b_b_qkv.shape[1]
            g = lax.dynamic_slice_in_dim(g.reshape(b_b_qkv.shape[0], -1), device * n_local, n_local, axis=1)
        g_small[k] = g.reshape(weights[k].shape)
    local_total = sum(weights[k].size for k in small_names)
    local_padded = -(-local_total // (LANES * LANES)) * (LANES * LANES)
    pack_local = lambda src: jnp.concatenate(
        [src[k].reshape(-1) for k in small_names] + [jnp.ones((local_padded - local_total,), F32)]
    ).reshape(local_padded // LANES, LANES)
    d, mn, vn = adamw(pack_local(weights), pack_local(g_small), pack_local(mom_m), pack_local(mom_v), "adamw_small")
    off = 0
    for k in small_names:
        size, shape = weights[k].size, weights[k].shape
        cut = lambda t: t.reshape(-1)[off:off + size].reshape(shape)
        out_g[k], out_d[k], out_m[k], out_v[k] = g_small[k], cut(d), cut(mn), cut(vn)
        off += size

    return (loss, grad_x, *[out_g[k] for k in names], *[out_d[k] for k in names],
            *[out_m[k] for k in names], *[out_v[k] for k in names])
```

```python
import functools
import math

import jax
import jax.numpy as jnp
from jax import lax
from jax.experimental import pallas as pl
from jax.experimental.pallas import tpu as pltpu

F32, BF16 = jnp.float32, jnp.bfloat16
MESH = pl.DeviceIdType.MESH
N_DEV = 8

CHUNK = 128
HEAD_DIM = 64
N_KV_HEADS = 4
ROPE_DIM = HEAD_DIM // 4
ROPE_THETA = 500000.0
RMS_EPS = 1e-6
LN_EPS = 1e-5
NEG_INF = -1e30

ADAM_LR = 0.001
ADAM_B1 = 0.9
ADAM_B2 = 0.999
ADAM_EPS = 1e-08
ADAM_WD = 0.01
ADAM_STEP = 10

V7X_VMEM_BYTES = 64 * 2 ** 20
VMEM_LIMIT = V7X_VMEM_BYTES - 8 * 2 ** 20
LANES = 128


def _params(*sem):
    return pltpu.CompilerParams(dimension_semantics=sem or None, vmem_limit_bytes=VMEM_LIMIT)


def _tile(n, pref):
    if n <= pref:
        return n
    t = pref - pref % LANES
    while t >= LANES:
        if n % t == 0:
            return t
        t -= LANES
    return n


def _full(shape):
    return pl.BlockSpec(shape, lambda *_: (0,) * len(shape))


def _rows(tm, width):
    return pl.BlockSpec((tm, width), lambda i: (i, 0))


def _rms(x, g):
    r = lax.rsqrt(jnp.mean(x * x, axis=-1, keepdims=True) + RMS_EPS)
    return x * r * g


def _rms_bwd(x, g, dy):
    r = lax.rsqrt(jnp.mean(x * x, axis=-1, keepdims=True) + RMS_EPS)
    xhat = x * r
    dg = jnp.sum(dy * xhat, axis=0, keepdims=True)
    dxhat = dy * g
    dx = r * (dxhat - xhat * jnp.mean(dxhat * xhat, axis=-1, keepdims=True))
    return dx, dg


_INV_SQRT2 = 1.0 / math.sqrt(2.0)
_INV_SQRT2PI = 1.0 / math.sqrt(2.0 * math.pi)


def _gelu(x):
    return 0.5 * x * (1.0 + lax.erf(x * _INV_SQRT2))


def _gelu_and_grad(x):
    cdf = 0.5 * (1.0 + lax.erf(x * _INV_SQRT2))
    return x * cdf, cdf + x * jnp.exp(-0.5 * x * x) * _INV_SQRT2PI


def _sigmoid(x):
    return 0.5 * jnp.tanh(0.5 * x) + 0.5


def _position():
    return lax.axis_index("x"), lax.axis_index("y"), lax.axis_index("c")


def _index(p):
    return 4 * p[0] + 2 * p[1] + p[2]


def allgather(xs, name):
    n = len(xs)
    any_spec = pl.BlockSpec(memory_space=pl.ANY)

    def body(*refs):
        x_refs, o_refs = refs[:n], refs[n:2 * n]
        send_sems, recv_sems, local_sems = refs[2 * n:]
        x, y, c = _position()
        me, sibling = (x, y, c), (x, y, 1 - c)
        chips = [(1 - x, y), (x, 1 - y), (1 - x, 1 - y)]

        def slot(a, p):
            return o_refs[a].at[:, pl.ds(_index(p), 1)]

        def copy(a, k, block, to, src=None):
            return pltpu.make_async_remote_copy(
                src_ref=slot(a, block) if src is None else src, dst_ref=slot(a, block),
                send_sem=send_sems.at[7 * a + k], recv_sem=recv_sems.at[7 * a + k],
                device_id=to, device_id_type=MESH)

        mine = [pltpu.make_async_copy(x_refs[a], slot(a, me), local_sems.at[a]) for a in range(n)]
        for cp in mine:
            cp.start()
        first = []
        for a in range(n):
            first.append(copy(a, 0, me, sibling, src=x_refs[a]))
            first += [copy(a, 1 + j, me, (*chip, c), src=x_refs[a]) for j, chip in enumerate(chips)]
        for cp in first:
            cp.start()
        passed = []
        for j, chip in enumerate(chips):
            for a in range(n):
                copy(a, 1 + j, (*chip, c), me).wait_recv()
                cp = copy(a, 4 + j, (*chip, c), sibling)
                cp.start()
                passed.append(cp)
        for a in range(n):
            copy(a, 0, sibling, me).wait_recv()
            for j, chip in enumerate(chips):
                copy(a, 4 + j, (*chip, 1 - c), me).wait_recv()
        for cp in first + passed:
            cp.wait_send()
        for cp in mine:
            cp.wait()

    return pl.pallas_call(
        body, name=name,
        out_shape=[jax.ShapeDtypeStruct((v.shape[0], N_DEV) + v.shape[2:], v.dtype) for v in xs],
        in_specs=[any_spec] * n, out_specs=[any_spec] * n,
        scratch_shapes=[pltpu.SemaphoreType.DMA((7 * n,)), pltpu.SemaphoreType.DMA((7 * n,)),
                        pltpu.SemaphoreType.DMA((n,))],
    )(*xs)


_HBM = pl.BlockSpec(memory_space=pltpu.HBM)
_SEM = pl.BlockSpec(memory_space=pltpu.SEMAPHORE)
_ORDERED_BY_DATA = pltpu.SideEffectType.DATAFLOW_SIDE_EFFECTING


def _in_hbm(v):
    return pltpu.with_memory_space_constraint(v, pltpu.HBM)


def _peer(k):
    x, y, c = _position()
    return (x ^ ((k >> 2) & 1), y ^ ((k >> 1) & 1), c ^ (k & 1))


def _split_copies(srcs, dsts, send, recv):
    return [pltpu.make_async_remote_copy(
        src_ref=src(k), dst_ref=dst(k), send_sem=send.at[7 * a + k - 1], recv_sem=recv.at[7 * a + k - 1],
        device_id=_peer(k), device_id_type=MESH)
        for a, (src, dst) in enumerate(zip(srcs, dsts)) for k in range(1, N_DEV)]


def _start_call(groups, sent, copies_of, name):
    flat = [v for grp in groups for v in grp]
    n, ng = len(flat), len(groups)

    def body(*refs):
        bufs, sems = refs[:n], refs[n:n + 2 * ng]
        q = 0
        for g, grp in enumerate(groups):
            for cp in copies_of(bufs[q:q + len(grp)], sems[2 * g], sems[2 * g + 1]):
                cp.start()
            q += len(grp)
        refs[-1][...] = jnp.zeros_like(refs[-1])

    sem_shapes = []
    for count in sent:
        sem_shapes += [pltpu.SemaphoreType.DMA((7 * count,)) for _ in range(2)]
    res = pl.pallas_call(
        body, name=name,
        out_shape=sem_shapes + [pltpu.HBM(v.shape, v.dtype) for v in flat] + [jax.ShapeDtypeStruct((8, LANES), F32)],
        in_specs=[_HBM] * n,
        out_specs=[_SEM] * (2 * ng) + [_HBM] * n + [pl.BlockSpec(memory_space=pltpu.VMEM)],
        input_output_aliases={i: 2 * ng + i for i in range(n)},
        compiler_params=pltpu.CompilerParams(has_side_effects=_ORDERED_BY_DATA),
    )(*[_in_hbm(v) for v in flat])
    sems = [(res[2 * g], res[2 * g + 1]) for g in range(ng)]
    return sems, list(res[2 * ng:-1]), res[-1]


def _wait_call(bufs, sems, copies_of, after, name):
    n = len(bufs)

    def body(*refs):
        for cp in copies_of(refs[:n], refs[n], refs[n + 1]):
            cp.wait_send()
            cp.wait_recv()

    return list(pl.pallas_call(
        body, name=name,
        out_shape=[pltpu.HBM(v.shape, v.dtype) for v in bufs],
        in_specs=[_HBM] * n + [_SEM, _SEM] + [pl.BlockSpec(memory_space=pl.ANY)] * len(after),
        out_specs=[_HBM] * n,
        input_output_aliases={i: i for i in range(n)},
        compiler_params=pltpu.CompilerParams(has_side_effects=_ORDERED_BY_DATA),
    )(*bufs, sems[0], sems[1], *after))


def _gather_copies(lands, send, recv):
    me = _index(_position())
    mine = [lambda k, ref=ref: ref.at[pl.ds(me, 1)] for ref in lands]
    return _split_copies(mine, mine, send, recv)


def _gather_arrivals(lands, send, recv):
    me = _index(_position())
    mine = [lambda k, ref=ref: ref.at[pl.ds(me, 1)] for ref in lands]
    theirs = [lambda k, ref=ref: ref.at[pl.ds(_index(_peer(k)), 1)] for ref in lands]
    return _split_copies(mine, theirs, send, recv)


def _exchange_copies(bufs, send, recv):
    half = len(bufs) // 2
    srcs = [lambda k, ref=ref: ref.at[pl.ds(_index(_peer(k)), 1)] for ref in bufs[:half]]
    dsts = [lambda k, ref=ref: ref.at[pl.ds(k - 1, 1)] for ref in bufs[half:]]
    return _split_copies(srcs, dsts, send, recv)


def gather_start(groups, name):
    return _start_call(groups, [len(grp) for grp in groups], _gather_copies, name)


def gather_wait(lands, sems, after, name):
    return _wait_call(lands, sems, _gather_arrivals, after, name)


def exchange_start(partials, name):
    lands = [lax.empty((N_DEV - 1,) + p.shape[1:], p.dtype) for p in partials]
    sems, bufs, token = _start_call([list(partials) + lands], [len(partials)], _exchange_copies, name)
    return sems[0], bufs, token


def exchange_wait(bufs, sems, after, name):
    bufs = _wait_call(bufs, sems, _exchange_copies, after, name)
    return bufs[:len(bufs) // 2], bufs[len(bufs) // 2:]


def prep_weight(w, layer, transpose, dev, name, dtype=BF16, after=()):
    L, K, n = w.shape
    block = (1, n, K) if transpose else (1, K, n)

    def body(d_ref, w_ref, *rest):
        o_ref = rest[-1]
        v = w_ref[0]
        o_ref[0] = (v.T if transpose else v).astype(dtype)

    return pl.pallas_call(
        body, name=name,
        grid_spec=pltpu.PrefetchScalarGridSpec(
            num_scalar_prefetch=1, grid=(1,),
            in_specs=[pl.BlockSpec((1, K, n), lambda i, d: (layer, 0, 0))]
            + [pl.BlockSpec(memory_space=pl.ANY)] * len(after),
            out_specs=pl.BlockSpec(block, lambda i, d: (d[0], 0, 0))),
        out_shape=jax.ShapeDtypeStruct((N_DEV,) + block[1:], dtype),
        compiler_params=_params("arbitrary"),
    )(dev, w, *after)


def _adamw_math(w, g, m, v):
    mn = ADAM_B1 * m + (1.0 - ADAM_B1) * g
    vn = ADAM_B2 * v + (1.0 - ADAM_B2) * (g * g)
    m_hat = mn * (1.0 / (1.0 - ADAM_B1 ** ADAM_STEP))
    v_hat = vn * (1.0 / (1.0 - ADAM_B2 ** ADAM_STEP))
    return -ADAM_LR * (m_hat / (jnp.sqrt(v_hat) + ADAM_EPS) + ADAM_WD * w), mn, vn


def sum_adamw(partial, landed, w, m, v, prev, layer, transpose, dev, name, after=()):
    _, r, c = partial.shape
    if transpose:
        tc = _tile(c, 256)
        grid = (c // tc,)
        part_spec = pl.BlockSpec((1, r, tc), lambda i, d: (d[0], 0, i))
        land_spec = pl.BlockSpec((N_DEV - 1, r, tc), lambda i, d: (0, 0, i))
        w_spec = pl.BlockSpec((1, tc, r), lambda i, d: (layer, i, 0))
    else:
        tr = r // 2 if r % 32 == 0 else r
        grid = (r // tr,)
        part_spec = pl.BlockSpec((1, tr, c), lambda i, d: (d[0], i, 0))
        land_spec = pl.BlockSpec((N_DEV - 1, tr, c), lambda i, d: (0, i, 0))
        w_spec = pl.BlockSpec((1, tr, c), lambda i, d: (layer, i, 0))

    def body(d_ref, p_ref, l_ref, w_ref, m_ref, v_ref, *rest):
        g_ref, dl_ref, mo_ref, vo_ref = rest[-4:]
        acc = p_ref[0].astype(F32)
        for k in range(N_DEV - 1):
            acc = acc + l_ref[k].astype(F32)
        g = acc.T if transpose else acc
        delta, mn, vn = _adamw_math(w_ref[0], g, m_ref[0], v_ref[0])
        g_ref[0], dl_ref[0], mo_ref[0], vo_ref[0] = g, delta, mn, vn

    return pl.pallas_call(
        body, name=name,
        grid_spec=pltpu.PrefetchScalarGridSpec(
            num_scalar_prefetch=1, grid=grid,
            in_specs=[part_spec, land_spec, w_spec, w_spec, w_spec]
            + [pl.BlockSpec(memory_space=pl.ANY)] * (4 + len(after)),
            out_specs=[w_spec] * 4),
        out_shape=[jax.ShapeDtypeStruct(w.shape, F32)] * 4,
        input_output_aliases={6 + q: q for q in range(4)},
        compiler_params=_params("parallel"),
    )(dev, partial, landed, w, m, v, *prev, *after)


def sum_parts(parts, transpose, name):
    L, P, r, c = parts.shape
    tr = _tile(r, 256) if (not transpose and r % 8 == 0) else r
    if r % tr:
        tr = r
    out_block = (1, c, tr) if transpose else (1, tr, c)
    out_map = (lambda l, i: (l, 0, i)) if transpose else (lambda l, i: (l, i, 0))

    def body(p_ref, o_ref):
        acc = p_ref[0, 0].astype(F32)
        for s in range(1, P):
            acc = acc + p_ref[0, s].astype(F32)
        o_ref[0] = acc.T if transpose else acc

    return pl.pallas_call(
        body, name=name, grid=(L, r // tr),
        in_specs=[pl.BlockSpec((1, P, tr, c), lambda l, i: (l, 0, i, 0))],
        out_specs=pl.BlockSpec(out_block, out_map),
        out_shape=jax.ShapeDtypeStruct((L, c, r) if transpose else (L, r, c), F32),
        compiler_params=_params("parallel", "parallel"),
    )(parts)


def adamw(w, g, m, v, name):
    R, C = w.shape
    tr = _tile(R, 512)

    def body(w_ref, g_ref, m_ref, v_ref, d_ref, mo_ref, vo_ref):
        d_ref[...], mo_ref[...], vo_ref[...] = _adamw_math(w_ref[...], g_ref[...], m_ref[...], v_ref[...])

    spec = pl.BlockSpec((tr, C), lambda i: (i, 0))
    return pl.pallas_call(
        body, name=name, grid=(R // tr,),
        in_specs=[spec] * 4, out_specs=[spec] * 3,
        out_shape=[jax.ShapeDtypeStruct((R, C), F32)] * 3,
        compiler_params=_params("parallel"),
    )(w, g, m, v)


def add_norm(h, y, g_post, g_pre, name):
    T, D = h.shape
    tm = _tile(T, 512)
    has_y = y is not None

    def body(*refs):
        if has_y:
            h_ref, y_ref, gp_ref, g_ref, ho_ref, hn_ref = refs
            hv = h_ref[...] + _rms(y_ref[...], gp_ref[...])
            ho_ref[...] = hv
        else:
            h_ref, g_ref, hn_ref = refs
            hv = h_ref[...]
        hn_ref[...] = _rms(hv, g_ref[...]).astype(BF16)

    row, vec = _rows(tm, D), _full((1, D))
    if has_y:
        return pl.pallas_call(
            body, name=name, grid=(T // tm,), in_specs=[row, row, vec, vec], out_specs=[row, row],
            out_shape=[jax.ShapeDtypeStruct((T, D), F32), jax.ShapeDtypeStruct((T, D), BF16)],
            compiler_params=_params("parallel"))(h, y, g_post, g_pre)
    hn = pl.pallas_call(
        body, name=name, grid=(T // tm,), in_specs=[row, vec], out_specs=row,
        out_shape=jax.ShapeDtypeStruct((T, D), BF16), compiler_params=_params("parallel"))(h, g_pre)
    return h, hn


def add_norm_loss(h, y, g_post, target, name):
    T, D = h.shape
    tm = _tile(T, 512)

    def body(h_ref, y_ref, gp_ref, t_ref, dh_ref, loss_ref):
        @pl.when(pl.program_id(0) == 0)
        def _():
            loss_ref[...] = jnp.zeros_like(loss_ref)
        err = h_ref[...] + _rms(y_ref[...], gp_ref[...]) - t_ref[...]
        dh_ref[...] = err * (1.0 / D)
        loss_ref[...] += jnp.sum(err * err)

    row = _rows(tm, D)
    return pl.pallas_call(
        body, name=name, grid=(T // tm,), in_specs=[row, row, _full((1, D)), row],
        out_specs=[row, _full((8, LANES))],
        out_shape=[jax.ShapeDtypeStruct((T, D), F32), jax.ShapeDtypeStruct((8, LANES), F32)],
        compiler_params=_params("arbitrary"))(h, y, g_post, target)


def norm_bwd(dh_in, pre, post, name):
    T, D = dh_in.shape
    tm = _tile(T, 512)
    row, vec = _rows(tm, D), _full((1, D))
    ins, in_specs, outs, out_specs, names = [dh_in], [row], [], [], []
    if pre is not None:
        ins += list(pre)
        in_specs += [row, row, vec]
        outs += [jax.ShapeDtypeStruct((T, D), F32), jax.ShapeDtypeStruct((1, D), F32)]
        out_specs += [row, vec]
        names += ["dh", "dg_pre"]
    if post is not None:
        ins += list(post)
        in_specs += [row, vec]
        outs += [jax.ShapeDtypeStruct((T, D), BF16), jax.ShapeDtypeStruct((1, D), F32)]
        out_specs += [row, vec]
        names += ["dy", "dg_post"]

    def body(*refs):
        refs = list(refs)
        first = pl.program_id(0) == 0
        dh = refs.pop(0)[...]
        if pre is not None:
            dpre_ref, x_ref, g_ref = refs[:3]
            refs = refs[3:]
        if post is not None:
            y_ref, gp_ref = refs[:2]
            refs = refs[2:]
        if pre is not None:
            dh_ref, dg_ref = refs[:2]
            refs = refs[2:]
            dx, dg = _rms_bwd(x_ref[...], g_ref[...], dpre_ref[...].astype(F32))
            dh = dh + dx
            dh_ref[...] = dh

            @pl.when(first)
            def _():
                dg_ref[...] = jnp.zeros_like(dg_ref)
            dg_ref[...] += dg
        if post is not None:
            dy_ref, dgp_ref = refs[:2]
            dy, dgp = _rms_bwd(y_ref[...], gp_ref[...], dh)
            dy_ref[...] = dy.astype(BF16)

            @pl.when(first)
            def _():
                dgp_ref[...] = jnp.zeros_like(dgp_ref)
            dgp_ref[...] += dgp

    res = pl.pallas_call(
        body, name=name, grid=(T // tm,), in_specs=in_specs, out_specs=out_specs, out_shape=outs,
        compiler_params=_params("arbitrary"))(*ins)
    return dict(zip(names, res))


_NT = (((1,), (1,)), ((), ()))
_TN = (((0,), (0,)), ((), ()))


def rowmm(a, w, transposed_w, bias, out_dtype, name, after=None):
    M, K = a.shape
    N = w.shape[0] if transposed_w else w.shape[1]
    tm = _tile(M, 256)
    tn = _tile(N, 512)

    def body(*refs):
        a_ref, w_ref = refs[:2]
        o_ref = refs[-1]
        av = a_ref[...]
        for j in range(N // tn):
            cols = pl.ds(j * tn, tn)
            if transposed_w:
                acc = lax.dot_general(av, w_ref[cols, :], _NT, preferred_element_type=F32)
            else:
                acc = jnp.dot(av, w_ref[:, cols], preferred_element_type=F32)
            if bias is not None:
                acc = acc + refs[2][:, cols]
            o_ref[:, cols] = acc.astype(out_dtype)

    ins, in_specs = [a, w], [_rows(tm, K), _full(w.shape)]
    if bias is not None:
        ins.append(bias)
        in_specs.append(_full((1, N)))
    if after is not None:
        ins.append(after)
        in_specs.append(pl.BlockSpec(memory_space=pl.ANY))
    return pl.pallas_call(
        body, name=name, grid=(M // tm,), in_specs=in_specs, out_specs=_rows(tm, N),
        out_shape=jax.ShapeDtypeStruct((M, N), out_dtype), compiler_params=_params("parallel"))(*ins)


def grad_mm(a, b, name):
    T, N = a.shape
    K = b.shape[1]
    tn = _tile(N, 1408)
    tt = _tile(T, 1024)

    def body(a_ref, b_ref, o_ref, acc_ref):
        t = pl.program_id(1)

        @pl.when(t == 0)
        def _():
            acc_ref[...] = jnp.zeros_like(acc_ref)
        acc_ref[...] += lax.dot_general(a_ref[...], b_ref[...], _TN, preferred_element_type=F32)

        @pl.when(t == pl.num_programs(1) - 1)
        def _():
            o_ref[...] = acc_ref[...].astype(BF16)

    return pl.pallas_call(
        body, name=name, grid=(N // tn, T // tt),
        in_specs=[pl.BlockSpec((tt, tn), lambda j, t: (t, j)), pl.BlockSpec((tt, K), lambda j, t: (t, 0))],
        out_specs=pl.BlockSpec((tn, K), lambda j, t: (j, 0)),
        out_shape=jax.ShapeDtypeStruct((N, K), BF16),
        scratch_shapes=[pltpu.VMEM((tn, K), F32)],
        compiler_params=_params("parallel", "arbitrary"))(a, b)


def ffn_up(fn, w_gu_t, name):
    T, D = fn.shape
    F = w_gu_t.shape[0] // 2
    tm = _tile(T, 256)
    tn = _tile(F, 512)

    def body(a_ref, w_ref, gu_ref, act_ref):
        av = a_ref[...]
        for j in range(F // tn):
            g = lax.dot_general(av, w_ref[pl.ds(j * tn, tn), :], _NT, preferred_element_type=F32)
            up = lax.dot_general(av, w_ref[pl.ds(F + j * tn, tn), :], _NT, preferred_element_type=F32)
            gu_ref[:, pl.ds(j * tn, tn)] = g.astype(BF16)
            gu_ref[:, pl.ds(F + j * tn, tn)] = up.astype(BF16)
            act_ref[:, pl.ds(j * tn, tn)] = (g * _sigmoid(g) * up).astype(BF16)

    return pl.pallas_call(
        body, name=name, grid=(T // tm,), in_specs=[_rows(tm, D), _full(w_gu_t.shape)],
        out_specs=[_rows(tm, 2 * F), _rows(tm, F)],
        out_shape=[jax.ShapeDtypeStruct((T, 2 * F), BF16), jax.ShapeDtypeStruct((T, F), BF16)],
        compiler_params=_params("parallel"))(fn, w_gu_t)


def ffn_dact(df, w_down, gu, name):
    T, D = df.shape
    F = w_down.shape[0]
    tm = _tile(T, 256)
    tn = _tile(F, 512)

    def body(d_ref, w_ref, gu_ref, o_ref):
        dv = d_ref[...]
        for j in range(F // tn):
            dact = lax.dot_general(dv, w_ref[pl.ds(j * tn, tn), :], _NT, preferred_element_type=F32)
            g = gu_ref[:, pl.ds(j * tn, tn)].astype(F32)
            up = gu_ref[:, pl.ds(F + j * tn, tn)].astype(F32)
            sg = _sigmoid(g)
            o_ref[:, pl.ds(j * tn, tn)] = (dact * up * (sg * (1.0 + g * (1.0 - sg)))).astype(BF16)
            o_ref[:, pl.ds(F + j * tn, tn)] = (dact * g * sg).astype(BF16)

    return pl.pallas_call(
        body, name=name, grid=(T // tm,), in_specs=[_rows(tm, D), _full(w_down.shape), _rows(tm, 2 * F)],
        out_specs=_rows(tm, 2 * F), out_shape=jax.ShapeDtypeStruct((T, 2 * F), BF16),
        compiler_params=_params("parallel"))(df, w_down, gu)


def _layernorm_stats(v):
    mu = jnp.mean(v, axis=-1, keepdims=True)
    cen = v - mu
    rstd = lax.rsqrt(jnp.mean(cen * cen, axis=-1, keepdims=True) + LN_EPS)
    return cen * rstd, rstd


def gmlp_fwd(a, ln_g, ln_b, w_mask, b_s_t, name):
    T, W2 = a.shape
    W = W2 // 2
    G = w_mask.shape[0]
    C = W // G

    def body(a_ref, g_ref, b_ref, w_ref, bs_ref, o_ref):
        xhat, _ = _layernorm_stats(_gelu(a_ref[:, W:].astype(F32)))
        vln = (xhat * g_ref[...] + b_ref[...]).astype(BF16)
        for g in range(G):
            cols = pl.ds(g * C, C)
            sv = jnp.dot(w_ref[g], vln[:, g * C:(g + 1) * C], preferred_element_type=F32) + bs_ref[:, g:g + 1]
            o_ref[:, cols] = (_gelu(a_ref[:, cols].astype(F32)) * sv).astype(BF16)

    return pl.pallas_call(
        body, name=name, grid=(T // CHUNK,),
        in_specs=[_rows(CHUNK, W2), _full((1, W)), _full((1, W)), _full(w_mask.shape), _full(b_s_t.shape)],
        out_specs=_rows(CHUNK, W), out_shape=jax.ShapeDtypeStruct((T, W), BF16),
        compiler_params=_params("parallel"))(a, ln_g, ln_b, w_mask, b_s_t)


def gmlp_bwd(a, dgated, ln_g, ln_b, w_mask, w_mask_t, b_s_t, group_onehot, name):
    T, W2 = a.shape
    W = W2 // 2
    G = w_mask.shape[0]
    C = W // G

    def body(a_ref, dg_ref, g_ref, b_ref, w_ref, wt_ref, bs_ref, e_ref,
             da_ref, dws_ref, dbs_ref, dlg_ref, dlb_ref, dbin_ref, dvln_ref):
        @pl.when(pl.program_id(0) == 0)
        def _():
            for r in (dws_ref, dbs_ref, dlg_ref, dlb_ref, dbin_ref):
                r[...] = jnp.zeros_like(r)
        row = lax.broadcasted_iota(jnp.int32, (CHUNK, CHUNK), 0)
        col = lax.broadcasted_iota(jnp.int32, (CHUNK, CHUNK), 1)
        causal = col <= row
        gelu_v, gelu_grad_v = _gelu_and_grad(a_ref[:, W:].astype(F32))
        xhat, rstd = _layernorm_stats(gelu_v)
        vln = (xhat * g_ref[...] + b_ref[...]).astype(BF16)
        for g in range(G):
            cols = pl.ds(g * C, C)
            vg = vln[:, g * C:(g + 1) * C]
            gelu_u, gelu_grad_u = _gelu_and_grad(a_ref[:, cols].astype(F32))
            dgated = dg_ref[:, cols].astype(F32)
            dsv = (dgated * gelu_u).astype(BF16)
            dbs_ref[...] += jnp.dot(dsv, e_ref[cols, :], preferred_element_type=F32)
            sv = jnp.dot(w_ref[g], vg, preferred_element_type=F32) + bs_ref[:, g:g + 1]
            dau = dgated * sv * gelu_grad_u
            da_ref[:, cols] = dau.astype(BF16)
            dbin_ref[:, cols] += jnp.sum(dau, axis=0, keepdims=True)
            dws_ref[g] += jnp.where(causal, lax.dot_general(dsv, vg, _NT, preferred_element_type=F32), 0.0)
            dvln_ref[:, cols] = jnp.dot(wt_ref[g], dsv, preferred_element_type=F32)
        dvln = dvln_ref[...]
        dlg_ref[...] += jnp.sum(dvln * xhat, axis=0, keepdims=True)
        dlb_ref[...] += jnp.sum(dvln, axis=0, keepdims=True)
        dxhat = dvln * g_ref[...]
        dv = rstd * (dxhat - jnp.mean(dxhat, axis=-1, keepdims=True)
                     - xhat * jnp.mean(dxhat * xhat, axis=-1, keepdims=True))
        dav = dv * gelu_grad_v
        da_ref[:, W:] = dav.astype(BF16)
        dbin_ref[:, W:] += jnp.sum(dav, axis=0, keepdims=True)

    return pl.pallas_call(
        body, name=name, grid=(T // CHUNK,),
        in_specs=[_rows(CHUNK, W2), _rows(CHUNK, W), _full((1, W)), _full((1, W)), _full(w_mask.shape),
                  _full(w_mask_t.shape), _full(b_s_t.shape), _full(group_onehot.shape)],
        out_specs=[_rows(CHUNK, W2), _full((G, CHUNK, CHUNK)), _full((CHUNK, LANES)), _full((1, W)),
                   _full((1, W)), _full((1, W2))],
        out_shape=[jax.ShapeDtypeStruct((T, W2), BF16), jax.ShapeDtypeStruct((G, CHUNK, CHUNK), F32),
                   jax.ShapeDtypeStruct((CHUNK, LANES), F32), jax.ShapeDtypeStruct((1, W), F32),
                   jax.ShapeDtypeStruct((1, W), F32), jax.ShapeDtypeStruct((1, W2), F32)],
        scratch_shapes=[pltpu.VMEM((CHUNK, W), F32)],
        compiler_params=_params("arbitrary"))(a, dgated, ln_g, ln_b, w_mask, w_mask_t, b_s_t, group_onehot)


def rope_tables(pos, inv_freq_row, name):
    T = pos.shape[0]
    tm = _tile(T, 512)

    def body(p_ref, f_ref, c_ref, s1_ref, s2_ref):
        ang = p_ref[...].astype(F32) * f_ref[...]
        lane = lax.broadcasted_iota(jnp.int32, (tm, LANES), 1) % HEAD_DIM
        sin = jnp.sin(ang)
        c_ref[...] = jnp.cos(ang)
        s1_ref[...] = jnp.where(lane < _HALF, -sin, 0.0)
        s2_ref[...] = jnp.where((lane >= _HALF) & (lane < ROPE_DIM), sin, 0.0)

    tab = _rows(tm, LANES)
    return pl.pallas_call(
        body, name=name, grid=(T // tm,), in_specs=[_rows(tm, 1), _full((1, LANES))], out_specs=[tab] * 3,
        out_shape=[jax.ShapeDtypeStruct((T, LANES), F32)] * 3, compiler_params=_params("parallel"))(pos, inv_freq_row)


_HALF = ROPE_DIM // 2


def _slabs(x):
    return [x[:, b * LANES:(b + 1) * LANES] for b in range(x.shape[1] // LANES)]


def _rotate(x, c, s1, s2):
    return [xs * c + pltpu.roll(xs, LANES - _HALF, 1) * s1 + pltpu.roll(xs, _HALF, 1) * s2 for xs in _slabs(x)]


def _rotate_transposed(dy, c, s1, s2):
    return [ds * c + pltpu.roll(ds * s1, _HALF, 1) + pltpu.roll(ds * s2, LANES - _HALF, 1) for ds in _slabs(dy)]


def rope_fwd(qkv, tabs, q_width, kv_width, name):
    T, QKV = qkv.shape
    tm = _tile(T, 512)
    QK = q_width + kv_width
    scale = HEAD_DIM ** -0.5

    def body(x_ref, c_ref, s1_ref, s2_ref, o_ref):
        x = x_ref[:, :QK].astype(F32)
        slabs = _rotate(x, c_ref[...], s1_ref[...], s2_ref[...])
        for b, y in enumerate(slabs):
            if b * LANES < q_width:
                y = y * scale
            o_ref[:, b * LANES:(b + 1) * LANES] = y.astype(BF16)

    tab = _rows(tm, LANES)
    return pl.pallas_call(
        body, name=name, grid=(T // tm,), in_specs=[_rows(tm, QKV), tab, tab, tab], out_specs=_rows(tm, QK),
        out_shape=jax.ShapeDtypeStruct((T, QK), BF16), compiler_params=_params("parallel"))(qkv, *tabs)


def rope_bwd(dq, dk, dv, tabs, name):
    T, Q = dq.shape
    KV = dk.shape[1]
    tm = _tile(T, 512)
    scale = HEAD_DIM ** -0.5

    def body(dq_ref, dk_ref, dv_ref, c_ref, s1_ref, s2_ref, o_ref, b_ref):
        @pl.when(pl.program_id(0) == 0)
        def _():
            b_ref[...] = jnp.zeros_like(b_ref)
        tabs_v = (c_ref[...], s1_ref[...], s2_ref[...])
        pieces = [s * scale for s in _rotate_transposed(dq_ref[...], *tabs_v)]
        pieces += _rotate_transposed(dk_ref[...], *tabs_v)
        pieces += _slabs(dv_ref[...])
        for b, y in enumerate(pieces):
            cols = pl.ds(b * LANES, LANES)
            o_ref[:, cols] = y.astype(BF16)
            b_ref[:, cols] += jnp.sum(y, axis=0, keepdims=True)

    tab = _rows(tm, LANES)
    return pl.pallas_call(
        body, name=name, grid=(T // tm,), in_specs=[_rows(tm, Q), _rows(tm, KV), _rows(tm, KV), tab, tab, tab],
        out_specs=[_rows(tm, Q + 2 * KV), _full((1, Q + 2 * KV))],
        out_shape=[jax.ShapeDtypeStruct((T, Q + 2 * KV), BF16), jax.ShapeDtypeStruct((1, Q + 2 * KV), F32)],
        compiler_params=_params("arbitrary"))(dq, dk, dv, *tabs)


def _band_mask(n, heads=1):
    qi = lax.broadcasted_iota(jnp.int32, (heads * CHUNK, 2 * CHUNK), 0) % CHUNK
    sj = lax.broadcasted_iota(jnp.int32, (heads * CHUNK, 2 * CHUNK), 1)
    return (sj > qi) & (sj <= qi + CHUNK) & ((n > 0) | (sj >= CHUNK))


def _kv_head(kc_ref, kp_ref, vc_ref, vp_ref, kh):
    lanes = slice(kh * HEAD_DIM, (kh + 1) * HEAD_DIM)
    return (jnp.concatenate([kp_ref[:, lanes], kc_ref[:, lanes]], axis=0),
            jnp.concatenate([vp_ref[:, lanes], vc_ref[:, lanes]], axis=0))


def _head_probs(q, kk, valid, sink):
    s = jnp.where(valid, lax.dot_general(q, kk, _NT, preferred_element_type=F32), NEG_INF)
    m = jnp.maximum(jnp.max(s, axis=-1, keepdims=True), sink)
    p = jnp.exp(s - m)
    e_sink = jnp.exp(sink - m)
    return p, 1.0 / (jnp.sum(p, axis=-1, keepdims=True) + e_sink), e_sink


def _attn_specs(q_width, kv_width, order):
    qb = q_width // kv_width
    prev = lambda i: jnp.maximum(order(i) - 1, 0)
    return [pl.BlockSpec((CHUNK, q_width), lambda i: (order(i), 0)),
            pl.BlockSpec((CHUNK, kv_width), lambda i: (order(i), qb)),
            pl.BlockSpec((CHUNK, kv_width), lambda i: (prev(i), qb)),
            pl.BlockSpec((CHUNK, kv_width), lambda i: (order(i), qb + 1)),
            pl.BlockSpec((CHUNK, kv_width), lambda i: (prev(i), qb + 1))]


def attn_fwd(qk, qkv, sinks, q_width, kv_width, name):
    T = qk.shape[0]
    group = q_width // kv_width

    def body(q_ref, kc_ref, kp_ref, vc_ref, vp_ref, sink_ref, o_ref):
        valid = _band_mask(pl.program_id(0))
        for kh in range(kv_width // HEAD_DIM):
            kk, vv = _kv_head(kc_ref, kp_ref, vc_ref, vp_ref, kh)
            for g in range(group):
                h = kh * group + g
                lanes = slice(h * HEAD_DIM, (h + 1) * HEAD_DIM)
                p, inv, _ = _head_probs(q_ref[:, lanes], kk, valid, sink_ref[h])
                o_ref[:, lanes] = (jnp.dot(p.astype(BF16), vv, preferred_element_type=F32) * inv).astype(BF16)

    specs = _attn_specs(q_width, kv_width, lambda i: i)
    return pl.pallas_call(
        body, name=name, grid=(T // CHUNK,),
        in_specs=specs + [pl.BlockSpec(memory_space=pltpu.SMEM)],
        out_specs=_rows(CHUNK, q_width), out_shape=jax.ShapeDtypeStruct((T, q_width), BF16),
        compiler_params=_params("parallel"))(qk, qk, qk, qkv, qkv, sinks)


def attn_bwd(qk, qkv, do, sinks, q_width, kv_width, name):
    T = qk.shape[0]
    NB = T // CHUNK
    group = q_width // kv_width

    def body(q_ref, kc_ref, kp_ref, vc_ref, vp_ref, do_ref, sink_ref, dq_ref, dk_ref, dv_ref, ds_ref, ck_ref, cv_ref):
        i = pl.program_id(0)
        n = NB - 1 - i

        @pl.when(i == 0)
        def _():
            ck_ref[...] = jnp.zeros_like(ck_ref)
            cv_ref[...] = jnp.zeros_like(cv_ref)
            ds_ref[...] = jnp.zeros_like(ds_ref)
        lane = lax.broadcasted_iota(jnp.int32, (1, LANES), 1)
        dsink_row = jnp.zeros((1, LANES), F32)
        valid = _band_mask(n, group)
        head = lax.broadcasted_iota(jnp.int32, (group * CHUNK, 1), 0) // CHUNK
        for kh in range(kv_width // HEAD_DIM):
            kk, vv = _kv_head(kc_ref, kp_ref, vc_ref, vp_ref, kh)
            heads = [slice((kh * group + g) * HEAD_DIM, (kh * group + g + 1) * HEAD_DIM) for g in range(group)]
            q = jnp.concatenate([q_ref[:, hs] for hs in heads], axis=0)
            do = jnp.concatenate([do_ref[:, hs] for hs in heads], axis=0)
            sink = jnp.zeros((group * CHUNK, 1), F32)
            for g in range(group):
                sink = jnp.where(head == g, sink_ref[kh * group + g], sink)
            p, inv, e_sink = _head_probs(q, kk, valid, sink)
            p = p * inv
            dp = lax.dot_general(do, vv, _NT, preferred_element_type=F32)
            delta = jnp.sum(p * dp, axis=-1, keepdims=True)
            ds = (p * (dp - delta)).astype(BF16)
            dsink = -e_sink * inv * delta
            dq = jnp.dot(ds, kk, preferred_element_type=F32)
            for g, hs in enumerate(heads):
                rows = slice(g * CHUNK, (g + 1) * CHUNK)
                dsink_row = dsink_row + jnp.where(lane == kh * group + g, jnp.sum(dsink[rows]), 0.0)
                dq_ref[:, hs] = dq[rows]
            dkk = lax.dot_general(ds, q, _TN, preferred_element_type=F32)
            dvv = lax.dot_general(p.astype(BF16), do, _TN, preferred_element_type=F32)
            lanes = slice(kh * HEAD_DIM, (kh + 1) * HEAD_DIM)
            dk_ref[:, lanes] = dkk[CHUNK:] + ck_ref[:, lanes]
            dv_ref[:, lanes] = dvv[CHUNK:] + cv_ref[:, lanes]
            ck_ref[:, lanes] = dkk[:CHUNK]
            cv_ref[:, lanes] = dvv[:CHUNK]
        ds_ref[0:1, :] += dsink_row

    order = lambda i: NB - 1 - i
    specs = _attn_specs(q_width, kv_width, order)
    kv_out = pl.BlockSpec((CHUNK, kv_width), lambda i: (order(i), 0))
    q_rows = pl.BlockSpec((CHUNK, q_width), lambda i: (order(i), 0))
    return pl.pallas_call(
        body, name=name, grid=(NB,),
        in_specs=specs + [q_rows, pl.BlockSpec(memory_space=pltpu.SMEM)],
        out_specs=[q_rows, kv_out, kv_out, _full((8, LANES))],
        out_shape=[jax.ShapeDtypeStruct((T, q_width), F32), jax.ShapeDtypeStruct((T, kv_width), F32),
                   jax.ShapeDtypeStruct((T, kv_width), F32), jax.ShapeDtypeStruct((8, LANES), F32)],
        scratch_shapes=[pltpu.VMEM((CHUNK, kv_width), F32), pltpu.VMEM((CHUNK, kv_width), F32)],
        compiler_params=_params("arbitrary"))(qk, qk, qk, qkv, qkv, do, sinks)


def _blocked(w):
    return w.reshape(N_DEV, w.shape[0] // N_DEV, w.shape[1])


def kernel(x, positions, pre_mix_g, post_mix_g, pre_ffn_g, post_ffn_g, a_w_in, a_b_in, a_ln_g, a_ln_b, a_w_s, a_b_s, a_w_out, b_w_qkv, b_b_qkv, b_sinks, b_w_o, ffn_w_gu, ffn_w_down, loss_target, m_pre_mix_g, m_post_mix_g, m_pre_ffn_g, m_post_ffn_g, m_a_w_in, m_a_b_in, m_a_ln_g, m_a_ln_b, m_a_w_s, m_a_b_s, m_a_w_out, m_b_w_qkv, m_b_b_qkv, m_b_sinks, m_b_w_o, m_ffn_w_gu, m_ffn_w_down, v_pre_mix_g, v_post_mix_g, v_pre_ffn_g, v_post_ffn_g, v_a_w_in, v_a_b_in, v_a_ln_g, v_a_ln_b, v_a_w_s, v_a_b_s, v_a_w_out, v_b_w_qkv, v_b_b_qkv, v_b_sinks, v_b_w_o, v_ffn_w_gu, v_ffn_w_down):
    weights = dict(pre_mix_g=pre_mix_g, post_mix_g=post_mix_g, pre_ffn_g=pre_ffn_g, post_ffn_g=post_ffn_g,
                   a_w_in=a_w_in, a_b_in=a_b_in, a_ln_g=a_ln_g, a_ln_b=a_ln_b, a_w_s=a_w_s, a_b_s=a_b_s,
                   a_w_out=a_w_out, b_w_qkv=b_w_qkv, b_b_qkv=b_b_qkv, b_sinks=b_sinks, b_w_o=b_w_o,
                   ffn_w_gu=ffn_w_gu, ffn_w_down=ffn_w_down)
    mom_m = dict(pre_mix_g=m_pre_mix_g, post_mix_g=m_post_mix_g, pre_ffn_g=m_pre_ffn_g, post_ffn_g=m_post_ffn_g,
                 a_w_in=m_a_w_in, a_b_in=m_a_b_in, a_ln_g=m_a_ln_g, a_ln_b=m_a_ln_b, a_w_s=m_a_w_s, a_b_s=m_a_b_s,
                 a_w_out=m_a_w_out, b_w_qkv=m_b_w_qkv, b_b_qkv=m_b_b_qkv, b_sinks=m_b_sinks, b_w_o=m_b_w_o,
                 ffn_w_gu=m_ffn_w_gu, ffn_w_down=m_ffn_w_down)
    mom_v = dict(pre_mix_g=v_pre_mix_g, post_mix_g=v_post_mix_g, pre_ffn_g=v_pre_ffn_g, post_ffn_g=v_post_ffn_g,
                 a_w_in=v_a_w_in, a_b_in=v_a_b_in, a_ln_g=v_a_ln_g, a_ln_b=v_a_ln_b, a_w_s=v_a_w_s, a_b_s=v_a_b_s,
                 a_w_out=v_a_w_out, b_w_qkv=v_b_w_qkv, b_b_qkv=v_b_b_qkv, b_sinks=v_b_sinks, b_w_o=v_b_w_o,
                 ffn_w_gu=v_ffn_w_gu, ffn_w_down=v_ffn_w_down)
    names = list(weights)
    column_sharded = ("a_w_in", "b_w_qkv", "ffn_w_gu")
    row_sharded = ("a_w_out", "b_w_o", "ffn_w_down")
    big = column_sharded + row_sharded

    T, D = x.shape[1], x.shape[2]
    depth = pre_mix_g.shape[0]
    n_heads = b_sinks.shape[1]
    q_width = n_heads * HEAD_DIM
    kv_width = N_KV_HEADS * HEAD_DIM
    G = a_w_s.shape[1]
    W = a_ln_g.shape[1]
    device = _index(_position())

    dev = device.reshape(1).astype(jnp.int32)

    def layer_keys(i):
        mixer = ("a_w_in", "a_w_out") if i % 2 == 0 else ("b_w_qkv", "b_w_o")
        return [(k, i // 2) for k in mixer] + [("ffn_w_gu", i), ("ffn_w_down", i)]

    as_view = ("b_w_qkv", "ffn_w_gu")
    in_kernel = ("a_w_in",)
    view = lambda k, t: jnp.swapaxes(t, 1, 2) if k in as_view else t
    w_view = {k: view(k, weights[k]) for k in big}
    m_view = {k: view(k, mom_m[k]) for k in big}
    v_view = {k: view(k, mom_v[k]) for k in big}

    order = [(i, k, l) for i in range(depth) for k, l in layer_keys(i)]
    bias_land = lax.dynamic_update_slice(
        jnp.zeros((N_DEV,) + b_b_qkv.shape, F32), b_b_qkv[None], (device, 0, 0))
    h, hn = add_norm(x[0], None, None, pre_mix_g[0][None], "norm_first")
    first = len(layer_keys(0))
    in_gather, token = {}, ()
    for name, lo, hi in (("gather_start_first", 0, first), ("gather_start_rest", first, len(order))):
        groups = [[prep_weight(w_view[k], l, k in in_kernel, dev, "prep_" + k, after=token)] for _, k, l in order[lo:hi]]
        if lo:
            groups[0].append(bias_land)
        gather_sems, flat_lands, token = gather_start(groups, name)
        token = (token, hn)
        for (i, k, _), grp, sems in zip(order[lo:hi], groups, gather_sems):
            in_gather[(i, k)] = ([flat_lands.pop(0) for _ in grp], sems)
    all_started = token[0]
    gathered = {}

    def weight(i, k, *after):
        if (i, k) not in gathered:
            lands, sems = in_gather[(i, k)]
            gathered[(i, k)] = gather_wait(lands, sems, after, f"gather_wait_{k}_{i}")
        g = gathered[(i, k)][0]
        return g.reshape(N_DEV * g.shape[1], g.shape[2])

    causal = jnp.tril(jnp.ones((CHUNK, CHUNK), dtype=bool))
    w_mask = jnp.where(causal[None, None], a_w_s, 0.0).astype(BF16)
    w_mask_t = jnp.swapaxes(w_mask, 2, 3)
    b_s_t = jnp.swapaxes(a_b_s, 1, 2)
    group_onehot = (jnp.arange(W)[:, None] // (W // G) == jnp.arange(LANES)[None, :]).astype(BF16)
    lane = jnp.arange(LANES) % HEAD_DIM
    inv_freq = ROPE_THETA ** (-jnp.arange(0, ROPE_DIM, 2, dtype=F32) / ROPE_DIM)
    inv_freq_row = jnp.where(lane < ROPE_DIM, inv_freq[lane % (ROPE_DIM // 2)], 0.0)[None, :].astype(F32)
    tabs = rope_tables(positions.reshape(T, 1), inv_freq_row, "rope_tables")

    saved = []
    for i in range(depth):
        j = i // 2
        s = dict(h=h, hn=hn)
        if i % 2 == 0:
            s["a"] = rowmm(hn, weight(i, "a_w_in", hn, all_started), True, a_b_in[j][None], BF16, "gmlp_in")
            s["gated"] = gmlp_fwd(s["a"], a_ln_g[j][None], a_ln_b[j][None], w_mask[j], b_s_t[j], "gmlp_gate")
            mix = rowmm(s["gated"], weight(i, "a_w_out", s["gated"]), False, None, F32, "gmlp_out")
        else:
            w_qkv = weight(i, "b_w_qkv", hn)
            b_qkv_full = jnp.swapaxes(gathered[(1, "b_w_qkv")][1], 0, 1).reshape(b_b_qkv.shape[0], 1, -1)
            s["qkv"] = rowmm(hn, w_qkv, True, b_qkv_full[j], BF16, "attn_qkv")
            s["qk"] = rope_fwd(s["qkv"], tabs, q_width, kv_width, "attn_rope")
            s["o"] = attn_fwd(s["qk"], s["qkv"], b_sinks[j], q_width, kv_width, "attn_core")
            mix = rowmm(s["o"], weight(i, "b_w_o", s["o"]), False, None, F32, "attn_out")
        s["mix"] = mix
        s["h2"], s["fn"] = add_norm(h, mix, post_mix_g[i][None], pre_ffn_g[i][None], "norm_mid")
        s["gu"], s["act"] = ffn_up(s["fn"], weight(i, "ffn_w_gu", s["fn"]), "ffn_up")
        s["f"] = rowmm(s["act"], weight(i, "ffn_w_down", s["act"]), False, None, F32, "ffn_down")
        saved.append(s)
        if i + 1 < depth:
            h, hn = add_norm(s["h2"], s["f"], post_ffn_g[i][None], pre_mix_g[i + 1][None], "norm_next")
    last = saved[-1]
    dh, loss_sum = add_norm_loss(last["h2"], last["f"], post_ffn_g[depth - 1][None], loss_target[0], "loss")
    loss = lax.psum(loss_sum[0, 0] * (0.5 / D), ("x", "y", "c"))

    small = {k: [None] * weights[k].shape[0] for k in names if k not in big}
    grads = {k: [None] * weights[k].shape[0] for k in big}
    in_flight = []

    def send_grads(keys, tag):
        sems, bufs, token = exchange_start([_blocked(grads[k][l]) for k, l in keys], "exchange_start_" + tag)
        in_flight.append((keys, sems, bufs, tag))
        return token

    r = norm_bwd(dh, None, (last["f"], post_ffn_g[depth - 1][None]), "norm_bwd_last")
    df, small["post_ffn_g"][depth - 1] = r["dy"], r["dg_post"]
    for i in reversed(range(depth)):
        j = i // 2
        s = saved[i]
        dgu = ffn_dact(df, weight(i, "ffn_w_down"), s["gu"], "ffn_dact")
        grads["ffn_w_down"][i] = grad_mm(s["act"], df, "ffn_down_grad")
        grads["ffn_w_gu"][i] = grad_mm(dgu, s["fn"], "ffn_up_grad")
        sent = send_grads(layer_keys(i)[2:], f"ffn_{i}")
        dfn = rowmm(dgu, weight(i, "ffn_w_gu"), False, None, F32, "ffn_dx", after=sent)
        r = norm_bwd(dh, (dfn, s["h2"], pre_ffn_g[i][None]), (s["mix"], post_mix_g[i][None]), "norm_bwd_mid")
        dh, dmix = r["dh"], r["dy"]
        small["pre_ffn_g"][i], small["post_mix_g"][i] = r["dg_pre"], r["dg_post"]
        if i % 2 == 0:
            grads["a_w_out"][j] = grad_mm(s["gated"], dmix, "gmlp_out_grad")
            dgated = rowmm(dmix, weight(i, "a_w_out"), True, None, BF16, "gmlp_dgated")
            da, dws, dbs, dlg, dlb, dbin = gmlp_bwd(
                s["a"], dgated, a_ln_g[j][None], a_ln_b[j][None], w_mask[j], w_mask_t[j], b_s_t[j], group_onehot,
                "gmlp_gate_bwd")
            small["a_w_s"][j], small["a_b_s"][j] = dws, dbs[:, :G].T
            small["a_ln_g"][j], small["a_ln_b"][j], small["a_b_in"][j] = dlg[0], dlb[0], dbin[0]
            grads["a_w_in"][j] = grad_mm(da, s["hn"], "gmlp_in_grad")
            sent = send_grads(layer_keys(i)[:2], f"mixer_{i}")
            dhn = rowmm(da, weight(i, "a_w_in"), False, None, F32, "gmlp_dx", after=sent)
        else:
            grads["b_w_o"][j] = grad_mm(s["o"], dmix, "attn_out_grad")
            do = rowmm(dmix, weight(i, "b_w_o"), True, None, BF16, "attn_do")
            dq, dk, dv, dsink = attn_bwd(s["qk"], s["qkv"], do, b_sinks[j], q_width, kv_width, "attn_core_bwd")
            dqkv, dbias = rope_bwd(dq, dk, dv, tabs, "attn_rope_bwd")
            small["b_sinks"][j], small["b_b_qkv"][j] = dsink[0, :n_heads], dbias[0]
            grads["b_w_qkv"][j] = grad_mm(dqkv, s["hn"], "attn_qkv_grad")
            sent = send_grads(layer_keys(i)[:2], f"mixer_{i}")
            dhn = rowmm(dqkv, weight(i, "b_w_qkv"), False, None, F32, "attn_dx", after=sent)
        if i > 0:
            prev = saved[i - 1]
            r = norm_bwd(dh, (dhn, s["h"], pre_mix_g[i][None]), (prev["f"], post_ffn_g[i - 1][None]), "norm_bwd_next")
            dh, df = r["dh"], r["dy"]
            small["pre_mix_g"][i], small["post_ffn_g"][i - 1] = r["dg_pre"], r["dg_post"]
        else:
            r = norm_bwd(dh, (dhn, s["h"], pre_mix_g[0][None]), None, "norm_bwd_first")
            dh, small["pre_mix_g"][0] = r["dh"], r["dg_pre"]
    grad_x = dh[None]

    small_names = [k for k in names if k not in big]
    partial = {k: jnp.stack([p.reshape(-1) for p in small[k]]) for k in small_names}
    sizes = [partial[k].size for k in small_names]
    total = sum(sizes)
    padded = -(-total // (LANES * LANES)) * (LANES * LANES)
    pack = lambda vals: jnp.concatenate([v.reshape(-1) for v in vals] + [jnp.zeros((padded - total,), F32)])
    packed = pack([partial[k] for k in small_names]).reshape(1, padded // LANES, LANES)
    small_sems, small_lands, small_sent = gather_start(
        [[prep_weight(packed, 0, False, dev, "place_small_grads", F32)]], "gather_start_small")

    stacked = {k: [lax.empty(w_view[k].shape, F32) for _ in range(4)] for k in big}
    after = (small_sent, dh)
    for keys, sems, bufs, tag in in_flight:
        partials, landed = exchange_wait(bufs, sems, after, "exchange_wait_" + tag)
        for (k, l), part, land in zip(keys, partials, landed):
            stacked[k] = sum_adamw(part, land, w_view[k], m_view[k], v_view[k], stacked[k], l,
                                   k in in_kernel, dev, "adamw_" + k)
        after = tuple(stacked[k][0] for k, _ in keys)
    out_g, out_d, out_m, out_v = {}, {}, {}, {}
    for k in big:
        out_g[k], out_d[k], out_m[k], out_v[k] = [view(k, t) for t in stacked[k]]

    gathered_small = gather_wait(small_lands, small_sems[0], after, "gather_wait_small")[0]
    summed = sum_parts(gathered_small[None], False, "sum_small")[0].reshape(-1)
    offsets = [sum(sizes[:q]) for q in range(len(sizes))]
    g_small = {}
    for k, off, size in zip(small_names, offsets, sizes):
        g = summed[off:off + size]
        if k == "b_b_qkv":
            n_local = b_b_qkv.shape[1]
            g = lax.dynamic_slice_in_dim(g.reshape(b_b_qkv.shape[0], -1), device * n_local, n_local, axis=1)
        g_small[k] = g.reshape(weights[k].shape)
    local_total = sum(weights[k].size for k in small_names)
    local_padded = -(-local_total // (LANES * LANES)) * (LANES * LANES)
    pack_local = lambda src: jnp.concatenate(
        [src[k].reshape(-1) for k in small_names] + [jnp.ones((local_padded - local_total,), F32)]
    ).reshape(local_padded // LANES, LANES)
    d, mn, vn = adamw(pack_local(weights), pack_local(g_small), pack_local(mom_m), pack_local(mom_v), "adamw_small")
    off = 0
    for k in small_names:
        size, shape = weights[k].size, weights[k].shape
        cut = lambda t: t.reshape(-1)[off:off + size].reshape(shape)
        out_g[k], out_d[k], out_m[k], out_v[k] = g_small[k], cut(d), cut(mn), cut(vn)
        off += size

    return (loss, grad_x, *[out_g[k] for k in names], *[out_d[k] for k in names],
            *[out_m[k] for k in names], *[out_v[k] for k in names])
```

```python
import functools
import math

import jax
import jax.numpy as jnp
from jax import lax
from jax.experimental import pallas as pl
from jax.experimental.pallas import tpu as pltpu

F32, BF16 = jnp.float32, jnp.bfloat16
MESH = pl.DeviceIdType.MESH
N_DEV = 8

CHUNK = 128
HEAD_DIM = 64
N_KV_HEADS = 4
ROPE_DIM = HEAD_DIM // 4
ROPE_THETA = 500000.0
RMS_EPS = 1e-6
LN_EPS = 1e-5
NEG_INF = -1e30

ADAM_LR = 0.001
ADAM_B1 = 0.9
ADAM_B2 = 0.999
ADAM_EPS = 1e-08
ADAM_WD = 0.01
ADAM_STEP = 10

V7X_VMEM_BYTES = 64 * 2 ** 20
VMEM_LIMIT = V7X_VMEM_BYTES - 8 * 2 ** 20
LANES = 128


def _params(*sem):
    return pltpu.CompilerParams(dimension_semantics=sem or None, vmem_limit_bytes=VMEM_LIMIT)


def _tile(n, pref):
    if n <= pref:
        return n
    t = pref - pref % LANES
    while t >= LANES:
        if n % t == 0:
            return t
        t -= LANES
    return n


def _full(shape):
    return pl.BlockSpec(shape, lambda *_: (0,) * len(shape))


def _rows(tm, width):
    return pl.BlockSpec((tm, width), lambda i: (i, 0))


def _rms(x, g):
    r = lax.rsqrt(jnp.mean(x * x, axis=-1, keepdims=True) + RMS_EPS)
    return x * r * g


def _rms_bwd(x, g, dy):
    r = lax.rsqrt(jnp.mean(x * x, axis=-1, keepdims=True) + RMS_EPS)
    xhat = x * r
    dg = jnp.sum(dy * xhat, axis=0, keepdims=True)
    dxhat = dy * g
    dx = r * (dxhat - xhat * jnp.mean(dxhat * xhat, axis=-1, keepdims=True))
    return dx, dg


_INV_SQRT2 = 1.0 / math.sqrt(2.0)
_INV_SQRT2PI = 1.0 / math.sqrt(2.0 * math.pi)


def _gelu(x):
    return 0.5 * x * (1.0 + lax.erf(x * _INV_SQRT2))


def _gelu_and_grad(x):
    cdf = 0.5 * (1.0 + lax.erf(x * _INV_SQRT2))
    return x * cdf, cdf + x * jnp.exp(-0.5 * x * x) * _INV_SQRT2PI


def _sigmoid(x):
    return 0.5 * jnp.tanh(0.5 * x) + 0.5


def _position():
    return lax.axis_index("x"), lax.axis_index("y"), lax.axis_index("c")


def _index(p):
    return 4 * p[0] + 2 * p[1] + p[2]


def allgather(xs, name):
    n = len(xs)
    any_spec = pl.BlockSpec(memory_space=pl.ANY)

    def body(*refs):
        x_refs, o_refs = refs[:n], refs[n:2 * n]
        send_sems, recv_sems, local_sems = refs[2 * n:]
        x, y, c = _position()
        me, sibling = (x, y, c), (x, y, 1 - c)
        chips = [(1 - x, y), (x, 1 - y), (1 - x, 1 - y)]

        def slot(a, p):
            return o_refs[a].at[:, pl.ds(_index(p), 1)]

        def copy(a, k, block, to, src=None):
            return pltpu.make_async_remote_copy(
                src_ref=slot(a, block) if src is None else src, dst_ref=slot(a, block),
                send_sem=send_sems.at[7 * a + k], recv_sem=recv_sems.at[7 * a + k],
                device_id=to, device_id_type=MESH)

        mine = [pltpu.make_async_copy(x_refs[a], slot(a, me), local_sems.at[a]) for a in range(n)]
        for cp in mine:
            cp.start()
        first = []
        for a in range(n):
            first.append(copy(a, 0, me, sibling, src=x_refs[a]))
            first += [copy(a, 1 + j, me, (*chip, c), src=x_refs[a]) for j, chip in enumerate(chips)]
        for cp in first:
            cp.start()
        passed = []
        for j, chip in enumerate(chips):
            for a in range(n):
                copy(a, 1 + j, (*chip, c), me).wait_recv()
                cp = copy(a, 4 + j, (*chip, c), sibling)
                cp.start()
                passed.append(cp)
        for a in range(n):
            copy(a, 0, sibling, me).wait_recv()
            for j, chip in enumerate(chips):
                copy(a, 4 + j, (*chip, 1 - c), me).wait_recv()
        for cp in first + passed:
            cp.wait_send()
        for cp in mine:
            cp.wait()

    return pl.pallas_call(
        body, name=name,
        out_shape=[jax.ShapeDtypeStruct((v.shape[0], N_DEV) + v.shape[2:], v.dtype) for v in xs],
        in_specs=[any_spec] * n, out_specs=[any_spec] * n,
        scratch_shapes=[pltpu.SemaphoreType.DMA((7 * n,)), pltpu.SemaphoreType.DMA((7 * n,)),
                        pltpu.SemaphoreType.DMA((n,))],
    )(*xs)


_HBM = pl.BlockSpec(memory_space=pltpu.HBM)
_SEM = pl.BlockSpec(memory_space=pltpu.SEMAPHORE)
_ORDERED_BY_DATA = pltpu.SideEffectType.DATAFLOW_SIDE_EFFECTING


def _in_hbm(v):
    return pltpu.with_memory_space_constraint(v, pltpu.HBM)


def _peer(k):
    x, y, c = _position()
    return (x ^ ((k >> 2) & 1), y ^ ((k >> 1) & 1), c ^ (k & 1))


def _split_copies(srcs, dsts, send, recv):
    return [pltpu.make_async_remote_copy(
        src_ref=src(k), dst_ref=dst(k), send_sem=send.at[7 * a + k - 1], recv_sem=recv.at[7 * a + k - 1],
        device_id=_peer(k), device_id_type=MESH)
        for a, (src, dst) in enumerate(zip(srcs, dsts)) for k in range(1, N_DEV)]


def _start_call(groups, sent, copies_of, name):
    flat = [v for grp in groups for v in grp]
    n, ng = len(flat), len(groups)

    def body(*refs):
        bufs, sems = refs[:n], refs[n:n + 2 * ng]
        q = 0
        for g, grp in enumerate(groups):
            for cp in copies_of(bufs[q:q + len(grp)], sems[2 * g], sems[2 * g + 1]):
                cp.start()
            q += len(grp)
        refs[-1][...] = jnp.zeros_like(refs[-1])

    sem_shapes = []
    for count in sent:
        sem_shapes += [pltpu.SemaphoreType.DMA((7 * count,)) for _ in range(2)]
    res = pl.pallas_call(
        body, name=name,
        out_shape=sem_shapes + [pltpu.HBM(v.shape, v.dtype) for v in flat] + [jax.ShapeDtypeStruct((8, LANES), F32)],
        in_specs=[_HBM] * n,
        out_specs=[_SEM] * (2 * ng) + [_HBM] * n + [pl.BlockSpec(memory_space=pltpu.VMEM)],
        input_output_aliases={i: 2 * ng + i for i in range(n)},
        compiler_params=pltpu.CompilerParams(has_side_effects=_ORDERED_BY_DATA),
    )(*[_in_hbm(v) for v in flat])
    sems = [(res[2 * g], res[2 * g + 1]) for g in range(ng)]
    return sems, list(res[2 * ng:-1]), res[-1]


def _wait_call(bufs, sems, copies_of, after, name):
    n = len(bufs)

    def body(*refs):
        for cp in copies_of(refs[:n], refs[n], refs[n + 1]):
            cp.wait_send()
            cp.wait_recv()

    return list(pl.pallas_call(
        body, name=name,
        out_shape=[pltpu.HBM(v.shape, v.dtype) for v in bufs],
        in_specs=[_HBM] * n + [_SEM, _SEM] + [pl.BlockSpec(memory_space=pl.ANY)] * len(after),
        out_specs=[_HBM] * n,
        input_output_aliases={i: i for i in range(n)},
        compiler_params=pltpu.CompilerParams(has_side_effects=_ORDERED_BY_DATA),
    )(*bufs, sems[0], sems[1], *after))


def _gather_copies(lands, send, recv):
    me = _index(_position())
    mine = [lambda k, ref=ref: ref.at[pl.ds(me, 1)] for ref in lands]
    return _split_copies(mine, mine, send, recv)


def _gather_arrivals(lands, send, recv):
    me = _index(_position())
    mine = [lambda k, ref=ref: ref.at[pl.ds(me, 1)] for ref in lands]
    theirs = [lambda k, ref=ref: ref.at[pl.ds(_index(_peer(k)), 1)] for ref in lands]
    return _split_copies(mine, theirs, send, recv)


def _exchange_copies(bufs, send, recv):
    half = len(bufs) // 2
    srcs = [lambda k, ref=ref: ref.at[pl.ds(_index(_peer(k)), 1)] for ref in bufs[:half]]
    dsts = [lambda k, ref=ref: ref.at[pl.ds(k - 1, 1)] for ref in bufs[half:]]
    return _split_copies(srcs, dsts, send, recv)


def gather_start(groups, name):
    return _start_call(groups, [len(grp) for grp in groups], _gather_copies, name)


def gather_wait(lands, sems, after, name):
    return _wait_call(lands, sems, _gather_arrivals, after, name)


def exchange_start(partials, name):
    lands = [lax.empty((N_DEV - 1,) + p.shape[1:], p.dtype) for p in partials]
    sems, bufs, token = _start_call([list(partials) + lands], [len(partials)], _exchange_copies, name)
    return sems[0], bufs, token


def exchange_wait(bufs, sems, after, name):
    bufs = _wait_call(bufs, sems, _exchange_copies, after, name)
    return bufs[:len(bufs) // 2], bufs[len(bufs) // 2:]


def prep_weight(w, layer, transpose, dev, name, dtype=BF16, after=()):
    L, K, n = w.shape
    block = (1, n, K) if transpose else (1, K, n)

    def body(d_ref, w_ref, *rest):
        o_ref = rest[-1]
        v = w_ref[0]
        o_ref[0] = (v.T if transpose else v).astype(dtype)

    return pl.pallas_call(
        body, name=name,
        grid_spec=pltpu.PrefetchScalarGridSpec(
            num_scalar_prefetch=1, grid=(1,),
            in_specs=[pl.BlockSpec((1, K, n), lambda i, d: (layer, 0, 0))]
            + [pl.BlockSpec(memory_space=pl.ANY)] * len(after),
            out_specs=pl.BlockSpec(block, lambda i, d: (d[0], 0, 0))),
        out_shape=jax.ShapeDtypeStruct((N_DEV,) + block[1:], dtype),
        compiler_params=_params("arbitrary"),
    )(dev, w, *after)


def _adamw_math(w, g, m, v):
    mn = ADAM_B1 * m + (1.0 - ADAM_B1) * g
    vn = ADAM_B2 * v + (1.0 - ADAM_B2) * (g * g)
    m_hat = mn * (1.0 / (1.0 - ADAM_B1 ** ADAM_STEP))
    v_hat = vn * (1.0 / (1.0 - ADAM_B2 ** ADAM_STEP))
    return -ADAM_LR * (m_hat / (jnp.sqrt(v_hat) + ADAM_EPS) + ADAM_WD * w), mn, vn


def sum_adamw(partial, landed, w, m, v, prev, layer, transpose, dev, name, after=()):
    _, r, c = partial.shape
    if transpose:
        tc = _tile(c, 256)
        grid = (c // tc,)
        part_spec = pl.BlockSpec((1, r, tc), lambda i, d: (d[0], 0, i))
        land_spec = pl.BlockSpec((N_DEV - 1, r, tc), lambda i, d: (0, 0, i))
        w_spec = pl.BlockSpec((1, tc, r), lambda i, d: (layer, i, 0))
    else:
        tr = r // 2 if r % 32 == 0 else r
        grid = (r // tr,)
        part_spec = pl.BlockSpec((1, tr, c), lambda i, d: (d[0], i, 0))
        land_spec = pl.BlockSpec((N_DEV - 1, tr, c), lambda i, d: (0, i, 0))
        w_spec = pl.BlockSpec((1, tr, c), lambda i, d: (layer, i, 0))

    def body(d_ref, p_ref, l_ref, w_ref, m_ref, v_ref, *rest):
        g_ref, dl_ref, mo_ref, vo_ref = rest[-4:]
        acc = p_ref[0].astype(F32)
        for k in range(N_DEV - 1):
            acc = acc + l_ref[k].astype(F32)
        g = acc.T if transpose else acc
        delta, mn, vn = _adamw_math(w_ref[0], g, m_ref[0], v_ref[0])
        g_ref[0], dl_ref[0], mo_ref[0], vo_ref[0] = g, delta, mn, vn

    return pl.pallas_call(
        body, name=name,
        grid_spec=pltpu.PrefetchScalarGridSpec(
            num_scalar_prefetch=1, grid=grid,
            in_specs=[part_spec, land_spec, w_spec, w_spec, w_spec]
            + [pl.BlockSpec(memory_space=pl.ANY)] * (4 + len(after)),
            out_specs=[w_spec] * 4),
        out_shape=[jax.ShapeDtypeStruct(w.shape, F32)] * 4,
        input_output_aliases={6 + q: q for q in range(4)},
        compiler_params=_params("parallel"),
    )(dev, partial, landed, w, m, v, *prev, *after)


def sum_parts(parts, transpose, name):
    L, P, r, c = parts.shape
    tr = _tile(r, 256) if (not transpose and r % 8 == 0) else r
    if r % tr:
        tr = r
    out_block = (1, c, tr) if transpose else (1, tr, c)
    out_map = (lambda l, i: (l, 0, i)) if transpose else (lambda l, i: (l, i, 0))

    def body(p_ref, o_ref):
        acc = p_ref[0, 0].astype(F32)
        for s in range(1, P):
            acc = acc + p_ref[0, s].astype(F32)
        o_ref[0] = acc.T if transpose else acc

    return pl.pallas_call(
        body, name=name, grid=(L, r // tr),
        in_specs=[pl.BlockSpec((1, P, tr, c), lambda l, i: (l, 0, i, 0))],
        out_specs=pl.BlockSpec(out_block, out_map),
        out_shape=jax.ShapeDtypeStruct((L, c, r) if transpose else (L, r, c), F32),
        compiler_params=_params("parallel", "parallel"),
    )(parts)


def adamw(w, g, m, v, name):
    R, C = w.shape
    tr = _tile(R, 512)

    def body(w_ref, g_ref, m_ref, v_ref, d_ref, mo_ref, vo_ref):
        d_ref[...], mo_ref[...], vo_ref[...] = _adamw_math(w_ref[...], g_ref[...], m_ref[...], v_ref[...])

    spec = pl.BlockSpec((tr, C), lambda i: (i, 0))
    return pl.pallas_call(
        body, name=name, grid=(R // tr,),
        in_specs=[spec] * 4, out_specs=[spec] * 3,
        out_shape=[jax.ShapeDtypeStruct((R, C), F32)] * 3,
        compiler_params=_params("parallel"),
    )(w, g, m, v)


def add_norm(h, y, g_post, g_pre, name):
    T, D = h.shape
    tm = _tile(T, 512)
    has_y = y is not None

    def body(*refs):
        if has_y:
            h_ref, y_ref, gp_ref, g_ref, ho_ref, hn_ref = refs
            hv = h_ref[...] + _rms(y_ref[...].astype(F32), gp_ref[...])
            ho_ref[...] = hv
        else:
            h_ref, g_ref, hn_ref = refs
            hv = h_ref[...]
        hn_ref[...] = _rms(hv, g_ref[...]).astype(BF16)

    row, vec = _rows(tm, D), _full((1, D))
    if has_y:
        return pl.pallas_call(
            body, name=name, grid=(T // tm,), in_specs=[row, row, vec, vec], out_specs=[row, row],
            out_shape=[jax.ShapeDtypeStruct((T, D), F32), jax.ShapeDtypeStruct((T, D), BF16)],
            compiler_params=_params("parallel"))(h, y, g_post, g_pre)
    hn = pl.pallas_call(
        body, name=name, grid=(T // tm,), in_specs=[row, vec], out_specs=row,
        out_shape=jax.ShapeDtypeStruct((T, D), BF16), compiler_params=_params("parallel"))(h, g_pre)
    return h, hn


def add_norm_loss(h, y, g_post, target, name):
    T, D = h.shape
    tm = _tile(T, 512)

    def body(h_ref, y_ref, gp_ref, t_ref, dh_ref, loss_ref):
        @pl.when(pl.program_id(0) == 0)
        def _():
            loss_ref[...] = jnp.zeros_like(loss_ref)
        err = h_ref[...] + _rms(y_ref[...].astype(F32), gp_ref[...]) - t_ref[...]
        dh_ref[...] = err * (1.0 / D)
        loss_ref[...] += jnp.sum(err * err)

    row = _rows(tm, D)
    return pl.pallas_call(
        body, name=name, grid=(T // tm,), in_specs=[row, row, _full((1, D)), row],
        out_specs=[row, _full((8, LANES))],
        out_shape=[jax.ShapeDtypeStruct((T, D), F32), jax.ShapeDtypeStruct((8, LANES), F32)],
        compiler_params=_params("arbitrary"))(h, y, g_post, target)


def norm_bwd(dh_in, pre, post, name):
    T, D = dh_in.shape
    tm = _tile(T, 512)
    row, vec = _rows(tm, D), _full((1, D))
    ins, in_specs, outs, out_specs, names = [dh_in], [row], [], [], []
    if pre is not None:
        ins += list(pre)
        in_specs += [row, row, vec]
        outs += [jax.ShapeDtypeStruct((T, D), F32), jax.ShapeDtypeStruct((1, D), F32)]
        out_specs += [row, vec]
        names += ["dh", "dg_pre"]
    if post is not None:
        ins += list(post)
        in_specs += [row, vec]
        outs += [jax.ShapeDtypeStruct((T, D), BF16), jax.ShapeDtypeStruct((1, D), F32)]
        out_specs += [row, vec]
        names += ["dy", "dg_post"]

    def body(*refs):
        refs = list(refs)
        first = pl.program_id(0) == 0
        dh = refs.pop(0)[...]
        if pre is not None:
            dpre_ref, x_ref, g_ref = refs[:3]
            refs = refs[3:]
        if post is not None:
            y_ref, gp_ref = refs[:2]
            refs = refs[2:]
        if pre is not None:
            dh_ref, dg_ref = refs[:2]
            refs = refs[2:]
            dx, dg = _rms_bwd(x_ref[...], g_ref[...], dpre_ref[...].astype(F32))
            dh = dh + dx
            dh_ref[...] = dh

            @pl.when(first)
            def _():
                dg_ref[...] = jnp.zeros_like(dg_ref)
            dg_ref[...] += dg
        if post is not None:
            dy_ref, dgp_ref = refs[:2]
            dy, dgp = _rms_bwd(y_ref[...].astype(F32), gp_ref[...], dh)
            dy_ref[...] = dy.astype(BF16)

            @pl.when(first)
            def _():
                dgp_ref[...] = jnp.zeros_like(dgp_ref)
            dgp_ref[...] += dgp

    res = pl.pallas_call(
        body, name=name, grid=(T // tm,), in_specs=in_specs, out_specs=out_specs, out_shape=outs,
        compiler_params=_params("arbitrary"))(*ins)
    return dict(zip(names, res))


_NT = (((1,), (1,)), ((), ()))
_TN = (((0,), (0,)), ((), ()))


def rowmm(a, w, transposed_w, bias, out_dtype, name, after=None):
    M, K = a.shape
    N = w.shape[0] if transposed_w else w.shape[1]
    tm = _tile(M, 256)
    tn = _tile(N, 512)

    def body(*refs):
        a_ref, w_ref = refs[:2]
        o_ref = refs[-1]
        av = a_ref[...]
        for j in range(N // tn):
            cols = pl.ds(j * tn, tn)
            if transposed_w:
                acc = lax.dot_general(av, w_ref[cols, :], _NT, preferred_element_type=F32)
            else:
                acc = jnp.dot(av, w_ref[:, cols], preferred_element_type=F32)
            if bias is not None:
                acc = acc + refs[2][:, cols]
            o_ref[:, cols] = acc.astype(out_dtype)

    ins, in_specs = [a, w], [_rows(tm, K), _full(w.shape)]
    if bias is not None:
        ins.append(bias)
        in_specs.append(_full((1, N)))
    if after is not None:
        ins.append(after)
        in_specs.append(pl.BlockSpec(memory_space=pl.ANY))
    return pl.pallas_call(
        body, name=name, grid=(M // tm,), in_specs=in_specs, out_specs=_rows(tm, N),
        out_shape=jax.ShapeDtypeStruct((M, N), out_dtype), compiler_params=_params("parallel"))(*ins)


def grad_mm(a, b, name):
    T, N = a.shape
    K = b.shape[1]
    tn = _tile(N, 1408)
    tt = _tile(T, 1024)

    def body(a_ref, b_ref, o_ref, acc_ref):
        t = pl.program_id(1)

        @pl.when(t == 0)
        def _():
            acc_ref[...] = jnp.zeros_like(acc_ref)
        acc_ref[...] += lax.dot_general(a_ref[...], b_ref[...], _TN, preferred_element_type=F32)

        @pl.when(t == pl.num_programs(1) - 1)
        def _():
            o_ref[...] = acc_ref[...].astype(BF16)

    return pl.pallas_call(
        body, name=name, grid=(N // tn, T // tt),
        in_specs=[pl.BlockSpec((tt, tn), lambda j, t: (t, j)), pl.BlockSpec((tt, K), lambda j, t: (t, 0))],
        out_specs=pl.BlockSpec((tn, K), lambda j, t: (j, 0)),
        out_shape=jax.ShapeDtypeStruct((N, K), BF16),
        scratch_shapes=[pltpu.VMEM((tn, K), F32)],
        compiler_params=_params("parallel", "arbitrary"))(a, b)


def ffn_up(fn, w_gu_t, name):
    T, D = fn.shape
    F = w_gu_t.shape[0] // 2
    tm = _tile(T, 256)
    tn = _tile(F, 512)

    def body(a_ref, w_ref, act_ref, part_ref):
        av = a_ref[...]
        for j in range(F // tn):
            g = lax.dot_general(av, w_ref[pl.ds(j * tn, tn), :], _NT, preferred_element_type=F32)
            up = lax.dot_general(av, w_ref[pl.ds(F + j * tn, tn), :], _NT, preferred_element_type=F32)
            sg = _sigmoid(g)
            silu = g * sg
            act_ref[:, pl.ds(j * tn, tn)] = (silu * up).astype(BF16)
            part_ref[:, pl.ds(j * tn, tn)] = (up * (sg + silu * (1.0 - sg))).astype(BF16)
            part_ref[:, pl.ds(F + j * tn, tn)] = silu.astype(BF16)

    return pl.pallas_call(
        body, name=name, grid=(T // tm,), in_specs=[_rows(tm, D), _full(w_gu_t.shape)],
        out_specs=[_rows(tm, F), _rows(tm, 2 * F)],
        out_shape=[jax.ShapeDtypeStruct((T, F), BF16), jax.ShapeDtypeStruct((T, 2 * F), BF16)],
        compiler_params=_params("parallel"))(fn, w_gu_t)


def ffn_dact(df, w_down, partials, name):
    T, D = df.shape
    F = w_down.shape[0]
    tm = _tile(T, 256)
    tn = _tile(F, 512)

    def body(d_ref, w_ref, part_ref, o_ref):
        dv = d_ref[...]
        for j in range(F // tn):
            dact = lax.dot_general(dv, w_ref[pl.ds(j * tn, tn), :], _NT, preferred_element_type=F32)
            for cols in (pl.ds(j * tn, tn), pl.ds(F + j * tn, tn)):
                o_ref[:, cols] = (dact * part_ref[:, cols].astype(F32)).astype(BF16)

    return pl.pallas_call(
        body, name=name, grid=(T // tm,), in_specs=[_rows(tm, D), _full(w_down.shape), _rows(tm, 2 * F)],
        out_specs=_rows(tm, 2 * F), out_shape=jax.ShapeDtypeStruct((T, 2 * F), BF16),
        compiler_params=_params("parallel"))(df, w_down, partials)


def _layernorm_stats(v):
    mu = jnp.mean(v, axis=-1, keepdims=True)
    cen = v - mu
    rstd = lax.rsqrt(jnp.mean(cen * cen, axis=-1, keepdims=True) + LN_EPS)
    return cen * rstd, rstd


def gmlp_fwd(a, ln_g, ln_b, w_mask, b_s_t, name):
    T, W2 = a.shape
    W = W2 // 2
    G = w_mask.shape[0]
    C = W // G

    def body(a_ref, g_ref, b_ref, w_ref, bs_ref, o_ref):
        xhat, _ = _layernorm_stats(_gelu(a_ref[:, W:].astype(F32)))
        vln = (xhat * g_ref[...] + b_ref[...]).astype(BF16)
        for g in range(G):
            cols = pl.ds(g * C, C)
            sv = jnp.dot(w_ref[g], vln[:, g * C:(g + 1) * C], preferred_element_type=F32) + bs_ref[:, g:g + 1]
            o_ref[:, cols] = (_gelu(a_ref[:, cols].astype(F32)) * sv).astype(BF16)

    return pl.pallas_call(
        body, name=name, grid=(T // CHUNK,),
        in_specs=[_rows(CHUNK, W2), _full((1, W)), _full((1, W)), _full(w_mask.shape), _full(b_s_t.shape)],
        out_specs=_rows(CHUNK, W), out_shape=jax.ShapeDtypeStruct((T, W), BF16),
        compiler_params=_params("parallel"))(a, ln_g, ln_b, w_mask, b_s_t)


def gmlp_bwd(a, dgated, ln_g, ln_b, w_mask, w_mask_t, b_s_t, group_onehot, name):
    T, W2 = a.shape
    W = W2 // 2
    G = w_mask.shape[0]
    C = W // G

    def body(a_ref, dg_ref, g_ref, b_ref, w_ref, wt_ref, bs_ref, e_ref,
             da_ref, dws_ref, dbs_ref, dlg_ref, dlb_ref, dbin_ref, dvln_ref):
        @pl.when(pl.program_id(0) == 0)
        def _():
            for r in (dws_ref, dbs_ref, dlg_ref, dlb_ref, dbin_ref):
                r[...] = jnp.zeros_like(r)
        row = lax.broadcasted_iota(jnp.int32, (CHUNK, CHUNK), 0)
        col = lax.broadcasted_iota(jnp.int32, (CHUNK, CHUNK), 1)
        causal = col <= row
        gelu_v, gelu_grad_v = _gelu_and_grad(a_ref[:, W:].astype(F32))
        xhat, rstd = _layernorm_stats(gelu_v)
        vln = (xhat * g_ref[...] + b_ref[...]).astype(BF16)
        for g in range(G):
            cols = pl.ds(g * C, C)
            vg = vln[:, g * C:(g + 1) * C]
            gelu_u, gelu_grad_u = _gelu_and_grad(a_ref[:, cols].astype(F32))
            dgated = dg_ref[:, cols].astype(F32)
            dsv = (dgated * gelu_u).astype(BF16)
            dbs_ref[...] += jnp.dot(dsv, e_ref[cols, :], preferred_element_type=F32)
            sv = jnp.dot(w_ref[g], vg, preferred_element_type=F32) + bs_ref[:, g:g + 1]
            dau = dgated * sv * gelu_grad_u
            da_ref[:, cols] = dau.astype(BF16)
            dbin_ref[:, cols] += jnp.sum(dau, axis=0, keepdims=True)
            dws_ref[g] += jnp.where(causal, lax.dot_general(dsv, vg, _NT, preferred_element_type=F32), 0.0)
            dvln_ref[:, cols] = jnp.dot(wt_ref[g], dsv, preferred_element_type=F32)
        dvln = dvln_ref[...]
        dlg_ref[...] += jnp.sum(dvln * xhat, axis=0, keepdims=True)
        dlb_ref[...] += jnp.sum(dvln, axis=0, keepdims=True)
        dxhat = dvln * g_ref[...]
        dv = rstd * (dxhat - jnp.mean(dxhat, axis=-1, keepdims=True)
                     - xhat * jnp.mean(dxhat * xhat, axis=-1, keepdims=True))
        dav = dv * gelu_grad_v
        da_ref[:, W:] = dav.astype(BF16)
        dbin_ref[:, W:] += jnp.sum(dav, axis=0, keepdims=True)

    return pl.pallas_call(
        body, name=name, grid=(T // CHUNK,),
        in_specs=[_rows(CHUNK, W2), _rows(CHUNK, W), _full((1, W)), _full((1, W)), _full(w_mask.shape),
                  _full(w_mask_t.shape), _full(b_s_t.shape), _full(group_onehot.shape)],
        out_specs=[_rows(CHUNK, W2), _full((G, CHUNK, CHUNK)), _full((CHUNK, LANES)), _full((1, W)),
                   _full((1, W)), _full((1, W2))],
        out_shape=[jax.ShapeDtypeStruct((T, W2), BF16), jax.ShapeDtypeStruct((G, CHUNK, CHUNK), F32),
                   jax.ShapeDtypeStruct((CHUNK, LANES), F32), jax.ShapeDtypeStruct((1, W), F32),
                   jax.ShapeDtypeStruct((1, W), F32), jax.ShapeDtypeStruct((1, W2), F32)],
        scratch_shapes=[pltpu.VMEM((CHUNK, W), F32)],
        compiler_params=_params("arbitrary"))(a, dgated, ln_g, ln_b, w_mask, w_mask_t, b_s_t, group_onehot)


def rope_tables(pos, inv_freq_row, name):
    T = pos.shape[0]
    tm = _tile(T, 512)

    def body(p_ref, f_ref, c_ref, s1_ref, s2_ref):
        ang = p_ref[...].astype(F32) * f_ref[...]
        lane = lax.broadcasted_iota(jnp.int32, (tm, LANES), 1) % HEAD_DIM
        sin = jnp.sin(ang)
        c_ref[...] = jnp.cos(ang)
        s1_ref[...] = jnp.where(lane < _HALF, -sin, 0.0)
        s2_ref[...] = jnp.where((lane >= _HALF) & (lane < ROPE_DIM), sin, 0.0)

    tab = _rows(tm, LANES)
    return pl.pallas_call(
        body, name=name, grid=(T // tm,), in_specs=[_rows(tm, 1), _full((1, LANES))], out_specs=[tab] * 3,
        out_shape=[jax.ShapeDtypeStruct((T, LANES), F32)] * 3, compiler_params=_params("parallel"))(pos, inv_freq_row)


_HALF = ROPE_DIM // 2


def _slabs(x):
    return [x[:, b * LANES:(b + 1) * LANES] for b in range(x.shape[1] // LANES)]


def _rotate(x, c, s1, s2):
    return [xs * c + pltpu.roll(xs, LANES - _HALF, 1) * s1 + pltpu.roll(xs, _HALF, 1) * s2 for xs in _slabs(x)]


def _rotate_transposed(dy, c, s1, s2):
    return [ds * c + pltpu.roll(ds * s1, _HALF, 1) + pltpu.roll(ds * s2, LANES - _HALF, 1) for ds in _slabs(dy)]


def rope_fwd(qkv, tabs, q_width, kv_width, name):
    T, QKV = qkv.shape
    tm = _tile(T, 512)
    QK = q_width + kv_width
    scale = HEAD_DIM ** -0.5

    def body(x_ref, c_ref, s1_ref, s2_ref, o_ref):
        x = x_ref[:, :QK].astype(F32)
        slabs = _rotate(x, c_ref[...], s1_ref[...], s2_ref[...])
        for b, y in enumerate(slabs):
            if b * LANES < q_width:
                y = y * scale
            o_ref[:, b * LANES:(b + 1) * LANES] = y.astype(BF16)

    tab = _rows(tm, LANES)
    return pl.pallas_call(
        body, name=name, grid=(T // tm,), in_specs=[_rows(tm, QKV), tab, tab, tab], out_specs=_rows(tm, QK),
        out_shape=jax.ShapeDtypeStruct((T, QK), BF16), compiler_params=_params("parallel"))(qkv, *tabs)


def rope_bwd(dq, dk, dv, tabs, name):
    T, Q = dq.shape
    KV = dk.shape[1]
    tm = _tile(T, 512)
    scale = HEAD_DIM ** -0.5

    def body(dq_ref, dk_ref, dv_ref, c_ref, s1_ref, s2_ref, o_ref, b_ref):
        @pl.when(pl.program_id(0) == 0)
        def _():
            b_ref[...] = jnp.zeros_like(b_ref)
        tabs_v = (c_ref[...], s1_ref[...], s2_ref[...])
        pieces = [s * scale for s in _rotate_transposed(dq_ref[...], *tabs_v)]
        pieces += _rotate_transposed(dk_ref[...], *tabs_v)
        pieces += _slabs(dv_ref[...])
        for b, y in enumerate(pieces):
            cols = pl.ds(b * LANES, LANES)
            o_ref[:, cols] = y.astype(BF16)
            b_ref[:, cols] += jnp.sum(y, axis=0, keepdims=True)

    tab = _rows(tm, LANES)
    return pl.pallas_call(
        body, name=name, grid=(T // tm,), in_specs=[_rows(tm, Q), _rows(tm, KV), _rows(tm, KV), tab, tab, tab],
        out_specs=[_rows(tm, Q + 2 * KV), _full((1, Q + 2 * KV))],
        out_shape=[jax.ShapeDtypeStruct((T, Q + 2 * KV), BF16), jax.ShapeDtypeStruct((1, Q + 2 * KV), F32)],
        compiler_params=_params("arbitrary"))(dq, dk, dv, *tabs)


def _band_mask(n, heads=1):
    qi = lax.broadcasted_iota(jnp.int32, (heads * CHUNK, 2 * CHUNK), 0) % CHUNK
    sj = lax.broadcasted_iota(jnp.int32, (heads * CHUNK, 2 * CHUNK), 1)
    return (sj > qi) & (sj <= qi + CHUNK) & ((n > 0) | (sj >= CHUNK))


def _kv_head(kc_ref, kp_ref, vc_ref, vp_ref, kh):
    lanes = slice(kh * HEAD_DIM, (kh + 1) * HEAD_DIM)
    return (jnp.concatenate([kp_ref[:, lanes], kc_ref[:, lanes]], axis=0),
            jnp.concatenate([vp_ref[:, lanes], vc_ref[:, lanes]], axis=0))


def _head_probs(q, kk, valid, sink):
    s = jnp.where(valid, lax.dot_general(q, kk, _NT, preferred_element_type=F32), NEG_INF)
    m = jnp.maximum(jnp.max(s, axis=-1, keepdims=True), sink)
    p = jnp.exp(s - m)
    e_sink = jnp.exp(sink - m)
    return p, 1.0 / (jnp.sum(p, axis=-1, keepdims=True) + e_sink), e_sink


def _attn_specs(q_width, kv_width, order):
    qb = q_width // kv_width
    prev = lambda i: jnp.maximum(order(i) - 1, 0)
    return [pl.BlockSpec((CHUNK, q_width), lambda i: (order(i), 0)),
            pl.BlockSpec((CHUNK, kv_width), lambda i: (order(i), qb)),
            pl.BlockSpec((CHUNK, kv_width), lambda i: (prev(i), qb)),
            pl.BlockSpec((CHUNK, kv_width), lambda i: (order(i), qb + 1)),
            pl.BlockSpec((CHUNK, kv_width), lambda i: (prev(i), qb + 1))]


def attn_fwd(qk, qkv, sinks, q_width, kv_width, name):
    T = qk.shape[0]
    group = q_width // kv_width

    def body(q_ref, kc_ref, kp_ref, vc_ref, vp_ref, sink_ref, o_ref):
        valid = _band_mask(pl.program_id(0))
        for kh in range(kv_width // HEAD_DIM):
            kk, vv = _kv_head(kc_ref, kp_ref, vc_ref, vp_ref, kh)
            for g in range(group):
                h = kh * group + g
                lanes = slice(h * HEAD_DIM, (h + 1) * HEAD_DIM)
                p, inv, _ = _head_probs(q_ref[:, lanes], kk, valid, sink_ref[h])
                o_ref[:, lanes] = (jnp.dot(p.astype(BF16), vv, preferred_element_type=F32) * inv).astype(BF16)

    specs = _attn_specs(q_width, kv_width, lambda i: i)
    return pl.pallas_call(
        body, name=name, grid=(T // CHUNK,),
        in_specs=specs + [pl.BlockSpec(memory_space=pltpu.SMEM)],
        out_specs=_rows(CHUNK, q_width), out_shape=jax.ShapeDtypeStruct((T, q_width), BF16),
        compiler_params=_params("parallel"))(qk, qk, qk, qkv, qkv, sinks)


def attn_bwd(qk, qkv, do, sinks, q_width, kv_width, name):
    T = qk.shape[0]
    NB = T // CHUNK
    group = q_width // kv_width

    def body(q_ref, kc_ref, kp_ref, vc_ref, vp_ref, do_ref, sink_ref, dq_ref, dk_ref, dv_ref, ds_ref, ck_ref, cv_ref):
        i = pl.program_id(0)
        n = NB - 1 - i

        @pl.when(i == 0)
        def _():
            ck_ref[...] = jnp.zeros_like(ck_ref)
            cv_ref[...] = jnp.zeros_like(cv_ref)
            ds_ref[...] = jnp.zeros_like(ds_ref)
        lane = lax.broadcasted_iota(jnp.int32, (1, LANES), 1)
        dsink_row = jnp.zeros((1, LANES), F32)
        valid = _band_mask(n, group)
        head = lax.broadcasted_iota(jnp.int32, (group * CHUNK, 1), 0) // CHUNK
        for kh in range(kv_width // HEAD_DIM):
            kk, vv = _kv_head(kc_ref, kp_ref, vc_ref, vp_ref, kh)
            heads = [slice((kh * group + g) * HEAD_DIM, (kh * group + g + 1) * HEAD_DIM) for g in range(group)]
            q = jnp.concatenate([q_ref[:, hs] for hs in heads], axis=0)
            do = jnp.concatenate([do_ref[:, hs] for hs in heads], axis=0)
            sink = jnp.zeros((group * CHUNK, 1), F32)
            for g in range(group):
                sink = jnp.where(head == g, sink_ref[kh * group + g], sink)
            p, inv, e_sink = _head_probs(q, kk, valid, sink)
            p = p * inv
            dp = lax.dot_general(do, vv, _NT, preferred_element_type=F32)
            delta = jnp.sum(p * dp, axis=-1, keepdims=True)
            ds = (p * (dp - delta)).astype(BF16)
            dsink = -e_sink * inv * delta
            dq = jnp.dot(ds, kk, preferred_element_type=F32)
            for g, hs in enumerate(heads):
                rows = slice(g * CHUNK, (g + 1) * CHUNK)
                dsink_row = dsink_row + jnp.where(lane == kh * group + g, jnp.sum(dsink[rows]), 0.0)
                dq_ref[:, hs] = dq[rows]
            dkk = lax.dot_general(ds, q, _TN, preferred_element_type=F32)
            dvv = lax.dot_general(p.astype(BF16), do, _TN, preferred_element_type=F32)
            lanes = slice(kh * HEAD_DIM, (kh + 1) * HEAD_DIM)
            dk_ref[:, lanes] = dkk[CHUNK:] + ck_ref[:, lanes]
            dv_ref[:, lanes] = dvv[CHUNK:] + cv_ref[:, lanes]
            ck_ref[:, lanes] = dkk[:CHUNK]
            cv_ref[:, lanes] = dvv[:CHUNK]
        ds_ref[0:1, :] += dsink_row

    order = lambda i: NB - 1 - i
    specs = _attn_specs(q_width, kv_width, order)
    kv_out = pl.BlockSpec((CHUNK, kv_width), lambda i: (order(i), 0))
    q_rows = pl.BlockSpec((CHUNK, q_width), lambda i: (order(i), 0))
    return pl.pallas_call(
        body, name=name, grid=(NB,),
        in_specs=specs + [q_rows, pl.BlockSpec(memory_space=pltpu.SMEM)],
        out_specs=[q_rows, kv_out, kv_out, _full((8, LANES))],
        out_shape=[jax.ShapeDtypeStruct((T, q_width), F32), jax.ShapeDtypeStruct((T, kv_width), F32),
                   jax.ShapeDtypeStruct((T, kv_width), F32), jax.ShapeDtypeStruct((8, LANES), F32)],
        scratch_shapes=[pltpu.VMEM((CHUNK, kv_width), F32), pltpu.VMEM((CHUNK, kv_width), F32)],
        compiler_params=_params("arbitrary"))(qk, qk, qk, qkv, qkv, do, sinks)


def _blocked(w):
    return w.reshape(N_DEV, w.shape[0] // N_DEV, w.shape[1])


def kernel(x, positions, pre_mix_g, post_mix_g, pre_ffn_g, post_ffn_g, a_w_in, a_b_in, a_ln_g, a_ln_b, a_w_s, a_b_s, a_w_out, b_w_qkv, b_b_qkv, b_sinks, b_w_o, ffn_w_gu, ffn_w_down, loss_target, m_pre_mix_g, m_post_mix_g, m_pre_ffn_g, m_post_ffn_g, m_a_w_in, m_a_b_in, m_a_ln_g, m_a_ln_b, m_a_w_s, m_a_b_s, m_a_w_out, m_b_w_qkv, m_b_b_qkv, m_b_sinks, m_b_w_o, m_ffn_w_gu, m_ffn_w_down, v_pre_mix_g, v_post_mix_g, v_pre_ffn_g, v_post_ffn_g, v_a_w_in, v_a_b_in, v_a_ln_g, v_a_ln_b, v_a_w_s, v_a_b_s, v_a_w_out, v_b_w_qkv, v_b_b_qkv, v_b_sinks, v_b_w_o, v_ffn_w_gu, v_ffn_w_down):
    weights = dict(pre_mix_g=pre_mix_g, post_mix_g=post_mix_g, pre_ffn_g=pre_ffn_g, post_ffn_g=post_ffn_g,
                   a_w_in=a_w_in, a_b_in=a_b_in, a_ln_g=a_ln_g, a_ln_b=a_ln_b, a_w_s=a_w_s, a_b_s=a_b_s,
                   a_w_out=a_w_out, b_w_qkv=b_w_qkv, b_b_qkv=b_b_qkv, b_sinks=b_sinks, b_w_o=b_w_o,
                   ffn_w_gu=ffn_w_gu, ffn_w_down=ffn_w_down)
    mom_m = dict(pre_mix_g=m_pre_mix_g, post_mix_g=m_post_mix_g, pre_ffn_g=m_pre_ffn_g, post_ffn_g=m_post_ffn_g,
                 a_w_in=m_a_w_in, a_b_in=m_a_b_in, a_ln_g=m_a_ln_g, a_ln_b=m_a_ln_b, a_w_s=m_a_w_s, a_b_s=m_a_b_s,
                 a_w_out=m_a_w_out, b_w_qkv=m_b_w_qkv, b_b_qkv=m_b_b_qkv, b_sinks=m_b_sinks, b_w_o=m_b_w_o,
                 ffn_w_gu=m_ffn_w_gu, ffn_w_down=m_ffn_w_down)
    mom_v = dict(pre_mix_g=v_pre_mix_g, post_mix_g=v_post_mix_g, pre_ffn_g=v_pre_ffn_g, post_ffn_g=v_post_ffn_g,
                 a_w_in=v_a_w_in, a_b_in=v_a_b_in, a_ln_g=v_a_ln_g, a_ln_b=v_a_ln_b, a_w_s=v_a_w_s, a_b_s=v_a_b_s,
                 a_w_out=v_a_w_out, b_w_qkv=v_b_w_qkv, b_b_qkv=v_b_b_qkv, b_sinks=v_b_sinks, b_w_o=v_b_w_o,
                 ffn_w_gu=v_ffn_w_gu, ffn_w_down=v_ffn_w_down)
    names = list(weights)
    column_sharded = ("a_w_in", "b_w_qkv", "ffn_w_gu")
    row_sharded = ("a_w_out", "b_w_o", "ffn_w_down")
    big = column_sharded + row_sharded

    T, D = x.shape[1], x.shape[2]
    depth = pre_mix_g.shape[0]
    n_heads = b_sinks.shape[1]
    q_width = n_heads * HEAD_DIM
    kv_width = N_KV_HEADS * HEAD_DIM
    G = a_w_s.shape[1]
    W = a_ln_g.shape[1]
    device = _index(_position())

    dev = device.reshape(1).astype(jnp.int32)

    def layer_keys(i):
        mixer = ("a_w_in", "a_w_out") if i % 2 == 0 else ("b_w_qkv", "b_w_o")
        return [(k, i // 2) for k in mixer] + [("ffn_w_gu", i), ("ffn_w_down", i)]

    as_view = ("b_w_qkv", "ffn_w_gu")
    in_kernel = ("a_w_in",)
    view = lambda k, t: jnp.swapaxes(t, 1, 2) if k in as_view else t
    w_view = {k: view(k, weights[k]) for k in big}
    m_view = {k: view(k, mom_m[k]) for k in big}
    v_view = {k: view(k, mom_v[k]) for k in big}

    order = [(i, k, l) for i in range(depth) for k, l in layer_keys(i)]
    bias_land = lax.dynamic_update_slice(
        jnp.zeros((N_DEV,) + b_b_qkv.shape, F32), b_b_qkv[None], (device, 0, 0))
    h, hn = add_norm(x[0], None, None, pre_mix_g[0][None], "norm_first")
    first = len(layer_keys(0))
    in_gather, token = {}, ()
    for name, lo, hi in (("gather_start_first", 0, first), ("gather_start_rest", first, len(order))):
        groups = [[prep_weight(w_view[k], l, k in in_kernel, dev, "prep_" + k, after=token)] for _, k, l in order[lo:hi]]
        if lo:
            groups[0].append(bias_land)
        gather_sems, flat_lands, token = gather_start(groups, name)
        token = (token, hn)
        for (i, k, _), grp, sems in zip(order[lo:hi], groups, gather_sems):
            in_gather[(i, k)] = ([flat_lands.pop(0) for _ in grp], sems)
    all_started = token[0]
    gathered = {}

    def weight(i, k, *after):
        if (i, k) not in gathered:
            lands, sems = in_gather[(i, k)]
            gathered[(i, k)] = gather_wait(lands, sems, after, f"gather_wait_{k}_{i}")
        g = gathered[(i, k)][0]
        return g.reshape(N_DEV * g.shape[1], g.shape[2])

    causal = jnp.tril(jnp.ones((CHUNK, CHUNK), dtype=bool))
    w_mask = jnp.where(causal[None, None], a_w_s, 0.0).astype(BF16)
    w_mask_t = jnp.swapaxes(w_mask, 2, 3)
    b_s_t = jnp.swapaxes(a_b_s, 1, 2)
    group_onehot = (jnp.arange(W)[:, None] // (W // G) == jnp.arange(LANES)[None, :]).astype(BF16)
    lane = jnp.arange(LANES) % HEAD_DIM
    inv_freq = ROPE_THETA ** (-jnp.arange(0, ROPE_DIM, 2, dtype=F32) / ROPE_DIM)
    inv_freq_row = jnp.where(lane < ROPE_DIM, inv_freq[lane % (ROPE_DIM // 2)], 0.0)[None, :].astype(F32)
    tabs = rope_tables(positions.reshape(T, 1), inv_freq_row, "rope_tables")

    saved = []
    for i in range(depth):
        j = i // 2
        s = dict(h=h, hn=hn)
        if i % 2 == 0:
            s["a"] = rowmm(hn, weight(i, "a_w_in", hn, all_started), True, a_b_in[j][None], BF16, "gmlp_in")
            s["gated"] = gmlp_fwd(s["a"], a_ln_g[j][None], a_ln_b[j][None], w_mask[j], b_s_t[j], "gmlp_gate")
            mix = rowmm(s["gated"], weight(i, "a_w_out", s["gated"]), False, None, BF16, "gmlp_out")
        else:
            w_qkv = weight(i, "b_w_qkv", hn)
            b_qkv_full = jnp.swapaxes(gathered[(1, "b_w_qkv")][1], 0, 1).reshape(b_b_qkv.shape[0], 1, -1)
            s["qkv"] = rowmm(hn, w_qkv, True, b_qkv_full[j], BF16, "attn_qkv")
            s["qk"] = rope_fwd(s["qkv"], tabs, q_width, kv_width, "attn_rope")
            s["o"] = attn_fwd(s["qk"], s["qkv"], b_sinks[j], q_width, kv_width, "attn_core")
            mix = rowmm(s["o"], weight(i, "b_w_o", s["o"]), False, None, BF16, "attn_out")
        s["mix"] = mix
        s["h2"], s["fn"] = add_norm(h, mix, post_mix_g[i][None], pre_ffn_g[i][None], "norm_mid")
        s["act"], s["act_partials"] = ffn_up(s["fn"], weight(i, "ffn_w_gu", s["fn"]), "ffn_up")
        s["f"] = rowmm(s["act"], weight(i, "ffn_w_down", s["act"]), False, None, BF16, "ffn_down")
        saved.append(s)
        if i + 1 < depth:
            h, hn = add_norm(s["h2"], s["f"], post_ffn_g[i][None], pre_mix_g[i + 1][None], "norm_next")
    last = saved[-1]
    dh, loss_sum = add_norm_loss(last["h2"], last["f"], post_ffn_g[depth - 1][None], loss_target[0], "loss")
    loss = lax.psum(loss_sum[0, 0] * (0.5 / D), ("x", "y", "c"))

    small = {k: [None] * weights[k].shape[0] for k in names if k not in big}
    grads = {k: [None] * weights[k].shape[0] for k in big}
    in_flight = []

    def send_grads(keys, tag):
        sems, bufs, token = exchange_start([_blocked(grads[k][l]) for k, l in keys], "exchange_start_" + tag)
        in_flight.append((keys, sems, bufs, tag))
        return token

    r = norm_bwd(dh, None, (last["f"], post_ffn_g[depth - 1][None]), "norm_bwd_last")
    df, small["post_ffn_g"][depth - 1] = r["dy"], r["dg_post"]
    for i in reversed(range(depth)):
        j = i // 2
        s = saved[i]
        dgu = ffn_dact(df, weight(i, "ffn_w_down"), s["act_partials"], "ffn_dact")
        grads["ffn_w_down"][i] = grad_mm(s["act"], df, "ffn_down_grad")
        grads["ffn_w_gu"][i] = grad_mm(dgu, s["fn"], "ffn_up_grad")
        sent = send_grads(layer_keys(i)[2:], f"ffn_{i}")
        dfn = rowmm(dgu, weight(i, "ffn_w_gu"), False, None, BF16, "ffn_dx", after=sent)
        r = norm_bwd(dh, (dfn, s["h2"], pre_ffn_g[i][None]), (s["mix"], post_mix_g[i][None]), "norm_bwd_mid")
        dh, dmix = r["dh"], r["dy"]
        small["pre_ffn_g"][i], small["post_mix_g"][i] = r["dg_pre"], r["dg_post"]
        if i % 2 == 0:
            grads["a_w_out"][j] = grad_mm(s["gated"], dmix, "gmlp_out_grad")
            dgated = rowmm(dmix, weight(i, "a_w_out"), True, None, BF16, "gmlp_dgated")
            da, dws, dbs, dlg, dlb, dbin = gmlp_bwd(
                s["a"], dgated, a_ln_g[j][None], a_ln_b[j][None], w_mask[j], w_mask_t[j], b_s_t[j], group_onehot,
                "gmlp_gate_bwd")
            small["a_w_s"][j], small["a_b_s"][j] = dws, dbs[:, :G].T
            small["a_ln_g"][j], small["a_ln_b"][j], small["a_b_in"][j] = dlg[0], dlb[0], dbin[0]
            grads["a_w_in"][j] = grad_mm(da, s["hn"], "gmlp_in_grad")
            sent = send_grads(layer_keys(i)[:2], f"mixer_{i}")
            dhn = rowmm(da, weight(i, "a_w_in"), False, None, BF16, "gmlp_dx", after=sent)
        else:
            grads["b_w_o"][j] = grad_mm(s["o"], dmix, "attn_out_grad")
            do = rowmm(dmix, weight(i, "b_w_o"), True, None, BF16, "attn_do")
            dq, dk, dv, dsink = attn_bwd(s["qk"], s["qkv"], do, b_sinks[j], q_width, kv_width, "attn_core_bwd")
            dqkv, dbias = rope_bwd(dq, dk, dv, tabs, "attn_rope_bwd")
            small["b_sinks"][j], small["b_b_qkv"][j] = dsink[0, :n_heads], dbias[0]
            grads["b_w_qkv"][j] = grad_mm(dqkv, s["hn"], "attn_qkv_grad")
            sent = send_grads(layer_keys(i)[:2], f"mixer_{i}")
            dhn = rowmm(dqkv, weight(i, "b_w_qkv"), False, None, BF16, "attn_dx", after=sent)
        if i > 0:
            prev = saved[i - 1]
            r = norm_bwd(dh, (dhn, s["h"], pre_mix_g[i][None]), (prev["f"], post_ffn_g[i - 1][None]), "norm_bwd_next")
            dh, df = r["dh"], r["dy"]
            small["pre_mix_g"][i], small["post_ffn_g"][i - 1] = r["dg_pre"], r["dg_post"]
        else:
            r = norm_bwd(dh, (dhn, s["h"], pre_mix_g[0][None]), None, "norm_bwd_first")
            dh, small["pre_mix_g"][0] = r["dh"], r["dg_pre"]
    grad_x = dh[None]

    small_names = [k for k in names if k not in big]
    partial = {k: jnp.stack([p.reshape(-1) for p in small[k]]) for k in small_names}
    sizes = [partial[k].size for k in small_names]
    total = sum(sizes)
    padded = -(-total // (LANES * LANES)) * (LANES * LANES)
    pack = lambda vals: jnp.concatenate([v.reshape(-1) for v in vals] + [jnp.zeros((padded - total,), F32)])
    packed = pack([partial[k] for k in small_names]).reshape(1, padded // LANES, LANES)
    small_sems, small_lands, small_sent = gather_start(
        [[prep_weight(packed, 0, False, dev, "place_small_grads", F32)]], "gather_start_small")

    stacked = {k: [lax.empty(w_view[k].shape, F32) for _ in range(4)] for k in big}
    after = (small_sent, dh)
    for keys, sems, bufs, tag in in_flight:
        partials, landed = exchange_wait(bufs, sems, after, "exchange_wait_" + tag)
        for (k, l), part, land in zip(keys, partials, landed):
            stacked[k] = sum_adamw(part, land, w_view[k], m_view[k], v_view[k], stacked[k], l,
                                   k in in_kernel, dev, "adamw_" + k)
        after = tuple(stacked[k][0] for k, _ in keys)
    out_g, out_d, out_m, out_v = {}, {}, {}, {}
    for k in big:
        out_g[k], out_d[k], out_m[k], out_v[k] = [view(k, t) for t in stacked[k]]

    gathered_small = gather_wait(small_lands, small_sems[0], after, "gather_wait_small")[0]
    summed = sum_parts(gathered_small[None], False, "sum_small")[0].reshape(-1)
    offsets = [sum(sizes[:q]) for q in range(len(sizes))]
    g_small = {}
    for k, off, size in zip(small_names, offsets, sizes):
        g = summed[off:off + size]
        if k == "b_b_qkv":
            n_local = b_b_qkv.shape[1]
            g = lax.dynamic_slice_in_dim(g.reshape(b_b_qkv.shape[0], -1), device * n_local, n_local, axis=1)
        g_small[k] = g.reshape(weights[k].shape)
    local_total = sum(weights[k].size for k in small_names)
    local_padded = -(-local_total // (LANES * LANES)) * (LANES * LANES)
    pack_local = lambda src: jnp.concatenate(
        [src[k].reshape(-1) for k in small_names] + [jnp.ones((local_padded - local_total,), F32)]
    ).reshape(local_padded // LANES, LANES)
    d, mn, vn = adamw(pack_local(weights), pack_local(g_small), pack_local(mom_m), pack_local(mom_v), "adamw_small")
    off = 0
    for k in small_names:
        size, shape = weights[k].size, weights[k].shape
        cut = lambda t: t.reshape(-1)[off:off + size].reshape(shape)
        out_g[k], out_d[k], out_m[k], out_v[k] = g_small[k], cut(d), cut(mn), cut(vn)
        off += size

    return (loss, grad_x, *[out_g[k] for k in names], *[out_d[k] for k in names],
            *[out_m[k] for k in names], *[out_v[k] for k in names])
```

```python
import functools
import math

import jax
import jax.numpy as jnp
from jax import lax
from jax.experimental import pallas as pl
from jax.experimental.pallas import tpu as pltpu

F32, BF16 = jnp.float32, jnp.bfloat16
MESH = pl.DeviceIdType.MESH
N_DEV = 8

CHUNK = 128
HEAD_DIM = 64
N_KV_HEADS = 4
ROPE_DIM = HEAD_DIM // 4
ROPE_THETA = 500000.0
RMS_EPS = 1e-6
LN_EPS = 1e-5
NEG_INF = -1e30

ADAM_LR = 0.001
ADAM_B1 = 0.9
ADAM_B2 = 0.999
ADAM_EPS = 1e-08
ADAM_WD = 0.01
ADAM_STEP = 10

V7X_VMEM_BYTES = 64 * 2 ** 20
VMEM_LIMIT = V7X_VMEM_BYTES - 8 * 2 ** 20
LANES = 128


def _params(*sem):
    return pltpu.CompilerParams(dimension_semantics=sem or None, vmem_limit_bytes=VMEM_LIMIT)


def _tile(n, pref):
    if n <= pref:
        return n
    t = pref - pref % LANES
    while t >= LANES:
        if n % t == 0:
            return t
        t -= LANES
    return n


def _full(shape):
    return pl.BlockSpec(shape, lambda *_: (0,) * len(shape))


def _rows(tm, width):
    return pl.BlockSpec((tm, width), lambda i: (i, 0))


def _rms(x, g):
    r = lax.rsqrt(jnp.mean(x * x, axis=-1, keepdims=True) + RMS_EPS)
    return x * r * g


def _rms_bwd(x, g, dy):
    r = lax.rsqrt(jnp.mean(x * x, axis=-1, keepdims=True) + RMS_EPS)
    xhat = x * r
    dg = jnp.sum(dy * xhat, axis=0, keepdims=True)
    dxhat = dy * g
    dx = r * (dxhat - xhat * jnp.mean(dxhat * xhat, axis=-1, keepdims=True))
    return dx, dg


_INV_SQRT2 = 1.0 / math.sqrt(2.0)
_INV_SQRT2PI = 1.0 / math.sqrt(2.0 * math.pi)


def _gelu(x):
    return 0.5 * x * (1.0 + lax.erf(x * _INV_SQRT2))


def _gelu_and_grad(x):
    cdf = 0.5 * (1.0 + lax.erf(x * _INV_SQRT2))
    return x * cdf, cdf + x * jnp.exp(-0.5 * x * x) * _INV_SQRT2PI


def _sigmoid(x):
    return 0.5 * jnp.tanh(0.5 * x) + 0.5


def _position():
    return lax.axis_index("x"), lax.axis_index("y"), lax.axis_index("c")


def _index(p):
    return 4 * p[0] + 2 * p[1] + p[2]


def allgather(xs, name):
    n = len(xs)
    any_spec = pl.BlockSpec(memory_space=pl.ANY)

    def body(*refs):
        x_refs, o_refs = refs[:n], refs[n:2 * n]
        send_sems, recv_sems, local_sems = refs[2 * n:]
        x, y, c = _position()
        me, sibling = (x, y, c), (x, y, 1 - c)
        chips = [(1 - x, y), (x, 1 - y), (1 - x, 1 - y)]

        def slot(a, p):
            return o_refs[a].at[:, pl.ds(_index(p), 1)]

        def copy(a, k, block, to, src=None):
            return pltpu.make_async_remote_copy(
                src_ref=slot(a, block) if src is None else src, dst_ref=slot(a, block),
                send_sem=send_sems.at[7 * a + k], recv_sem=recv_sems.at[7 * a + k],
                device_id=to, device_id_type=MESH)

        mine = [pltpu.make_async_copy(x_refs[a], slot(a, me), local_sems.at[a]) for a in range(n)]
        for cp in mine:
            cp.start()
        first = []
        for a in range(n):
            first.append(copy(a, 0, me, sibling, src=x_refs[a]))
            first += [copy(a, 1 + j, me, (*chip, c), src=x_refs[a]) for j, chip in enumerate(chips)]
        for cp in first:
            cp.start()
        passed = []
        for j, chip in enumerate(chips):
            for a in range(n):
                copy(a, 1 + j, (*chip, c), me).wait_recv()
                cp = copy(a, 4 + j, (*chip, c), sibling)
                cp.start()
                passed.append(cp)
        for a in range(n):
            copy(a, 0, sibling, me).wait_recv()
            for j, chip in enumerate(chips):
                copy(a, 4 + j, (*chip, 1 - c), me).wait_recv()
        for cp in first + passed:
            cp.wait_send()
        for cp in mine:
            cp.wait()

    return pl.pallas_call(
        body, name=name,
        out_shape=[jax.ShapeDtypeStruct((v.shape[0], N_DEV) + v.shape[2:], v.dtype) for v in xs],
        in_specs=[any_spec] * n, out_specs=[any_spec] * n,
        scratch_shapes=[pltpu.SemaphoreType.DMA((7 * n,)), pltpu.SemaphoreType.DMA((7 * n,)),
                        pltpu.SemaphoreType.DMA((n,))],
    )(*xs)


_HBM = pl.BlockSpec(memory_space=pltpu.HBM)
_SEM = pl.BlockSpec(memory_space=pltpu.SEMAPHORE)
_ORDERED_BY_DATA = pltpu.SideEffectType.DATAFLOW_SIDE_EFFECTING


def _in_hbm(v):
    return pltpu.with_memory_space_constraint(v, pltpu.HBM)


def _peer(k):
    x, y, c = _position()
    return (x ^ ((k >> 2) & 1), y ^ ((k >> 1) & 1), c ^ (k & 1))


def _split_copies(srcs, dsts, send, recv):
    return [pltpu.make_async_remote_copy(
        src_ref=src(k), dst_ref=dst(k), send_sem=send.at[7 * a + k - 1], recv_sem=recv.at[7 * a + k - 1],
        device_id=_peer(k), device_id_type=MESH)
        for a, (src, dst) in enumerate(zip(srcs, dsts)) for k in range(1, N_DEV)]


def _start_call(groups, sent, copies_of, name):
    flat = [v for grp in groups for v in grp]
    n, ng = len(flat), len(groups)

    def body(*refs):
        bufs, sems = refs[:n], refs[n:n + 2 * ng]
        q = 0
        for g, grp in enumerate(groups):
            for cp in copies_of(bufs[q:q + len(grp)], sems[2 * g], sems[2 * g + 1]):
                cp.start()
            q += len(grp)
        refs[-1][...] = jnp.zeros_like(refs[-1])

    sem_shapes = []
    for count in sent:
        sem_shapes += [pltpu.SemaphoreType.DMA((7 * count,)) for _ in range(2)]
    res = pl.pallas_call(
        body, name=name,
        out_shape=sem_shapes + [pltpu.HBM(v.shape, v.dtype) for v in flat] + [jax.ShapeDtypeStruct((8, LANES), F32)],
        in_specs=[_HBM] * n,
        out_specs=[_SEM] * (2 * ng) + [_HBM] * n + [pl.BlockSpec(memory_space=pltpu.VMEM)],
        input_output_aliases={i: 2 * ng + i for i in range(n)},
        compiler_params=pltpu.CompilerParams(has_side_effects=_ORDERED_BY_DATA),
    )(*[_in_hbm(v) for v in flat])
    sems = [(res[2 * g], res[2 * g + 1]) for g in range(ng)]
    return sems, list(res[2 * ng:-1]), res[-1]


def _wait_call(bufs, sems, copies_of, after, name):
    n = len(bufs)

    def body(*refs):
        for cp in copies_of(refs[:n], refs[n], refs[n + 1]):
            cp.wait_send()
            cp.wait_recv()

    return list(pl.pallas_call(
        body, name=name,
        out_shape=[pltpu.HBM(v.shape, v.dtype) for v in bufs],
        in_specs=[_HBM] * n + [_SEM, _SEM] + [pl.BlockSpec(memory_space=pl.ANY)] * len(after),
        out_specs=[_HBM] * n,
        input_output_aliases={i: i for i in range(n)},
        compiler_params=pltpu.CompilerParams(has_side_effects=_ORDERED_BY_DATA),
    )(*bufs, sems[0], sems[1], *after))


def _gather_copies(lands, send, recv):
    me = _index(_position())
    mine = [lambda k, ref=ref: ref.at[pl.ds(me, 1)] for ref in lands]
    return _split_copies(mine, mine, send, recv)


def _gather_arrivals(lands, send, recv):
    me = _index(_position())
    mine = [lambda k, ref=ref: ref.at[pl.ds(me, 1)] for ref in lands]
    theirs = [lambda k, ref=ref: ref.at[pl.ds(_index(_peer(k)), 1)] for ref in lands]
    return _split_copies(mine, theirs, send, recv)


def _exchange_copies(bufs, send, recv):
    half = len(bufs) // 2
    srcs = [lambda k, ref=ref: ref.at[pl.ds(_index(_peer(k)), 1)] for ref in bufs[:half]]
    dsts = [lambda k, ref=ref: ref.at[pl.ds(k - 1, 1)] for ref in bufs[half:]]
    return _split_copies(srcs, dsts, send, recv)


def gather_start(groups, name):
    return _start_call(groups, [len(grp) for grp in groups], _gather_copies, name)


def gather_wait(lands, sems, after, name):
    return _wait_call(lands, sems, _gather_arrivals, after, name)


def exchange_start(partials, name):
    lands = [lax.empty((N_DEV - 1,) + p.shape[1:], p.dtype) for p in partials]
    sems, bufs, token = _start_call([list(partials) + lands], [len(partials)], _exchange_copies, name)
    return sems[0], bufs, token


def exchange_wait(bufs, sems, after, name):
    bufs = _wait_call(bufs, sems, _exchange_copies, after, name)
    return bufs[:len(bufs) // 2], bufs[len(bufs) // 2:]


def prep_weight(w, layer, transpose, dev, name, dtype=BF16, after=()):
    L, K, n = w.shape
    block = (1, n, K) if transpose else (1, K, n)

    def body(d_ref, w_ref, *rest):
        o_ref = rest[-1]
        v = w_ref[0]
        o_ref[0] = (v.T if transpose else v).astype(dtype)

    return pl.pallas_call(
        body, name=name,
        grid_spec=pltpu.PrefetchScalarGridSpec(
            num_scalar_prefetch=1, grid=(1,),
            in_specs=[pl.BlockSpec((1, K, n), lambda i, d: (layer, 0, 0))]
            + [pl.BlockSpec(memory_space=pl.ANY)] * len(after),
            out_specs=pl.BlockSpec(block, lambda i, d: (d[0], 0, 0))),
        out_shape=jax.ShapeDtypeStruct((N_DEV,) + block[1:], dtype),
        compiler_params=_params("arbitrary"),
    )(dev, w, *after)


def _adamw_math(w, g, m, v):
    mn = ADAM_B1 * m + (1.0 - ADAM_B1) * g
    vn = ADAM_B2 * v + (1.0 - ADAM_B2) * (g * g)
    m_hat = mn * (1.0 / (1.0 - ADAM_B1 ** ADAM_STEP))
    v_hat = vn * (1.0 / (1.0 - ADAM_B2 ** ADAM_STEP))
    return -ADAM_LR * (m_hat / (jnp.sqrt(v_hat) + ADAM_EPS) + ADAM_WD * w), mn, vn


def sum_adamw(partial, landed, w, m, v, prev, layer, transpose, dev, name, after=()):
    _, r, c = partial.shape
    if transpose:
        tc = _tile(c, 256)
        grid = (c // tc,)
        part_spec = pl.BlockSpec((1, r, tc), lambda i, d: (d[0], 0, i))
        land_spec = pl.BlockSpec((N_DEV - 1, r, tc), lambda i, d: (0, 0, i))
        w_spec = pl.BlockSpec((1, tc, r), lambda i, d: (layer, i, 0))
    else:
        tr = r // 2 if r % 32 == 0 else r
        grid = (r // tr,)
        part_spec = pl.BlockSpec((1, tr, c), lambda i, d: (d[0], i, 0))
        land_spec = pl.BlockSpec((N_DEV - 1, tr, c), lambda i, d: (0, i, 0))
        w_spec = pl.BlockSpec((1, tr, c), lambda i, d: (layer, i, 0))

    def body(d_ref, p_ref, l_ref, w_ref, m_ref, v_ref, *rest):
        g_ref, dl_ref, mo_ref, vo_ref = rest[-4:]
        acc = p_ref[0].astype(F32)
        for k in range(N_DEV - 1):
            acc = acc + l_ref[k].astype(F32)
        g = acc.T if transpose else acc
        delta, mn, vn = _adamw_math(w_ref[0], g, m_ref[0], v_ref[0])
        g_ref[0], dl_ref[0], mo_ref[0], vo_ref[0] = g, delta, mn, vn

    return pl.pallas_call(
        body, name=name,
        grid_spec=pltpu.PrefetchScalarGridSpec(
            num_scalar_prefetch=1, grid=grid,
            in_specs=[part_spec, land_spec, w_spec, w_spec, w_spec]
            + [pl.BlockSpec(memory_space=pl.ANY)] * (4 + len(after)),
            out_specs=[w_spec] * 4),
        out_shape=[jax.ShapeDtypeStruct(w.shape, F32)] * 4,
        input_output_aliases={6 + q: q for q in range(4)},
        compiler_params=_params("parallel"),
    )(dev, partial, landed, w, m, v, *prev, *after)


def sum_parts(parts, transpose, name):
    L, P, r, c = parts.shape
    tr = _tile(r, 256) if (not transpose and r % 8 == 0) else r
    if r % tr:
        tr = r
    out_block = (1, c, tr) if transpose else (1, tr, c)
    out_map = (lambda l, i: (l, 0, i)) if transpose else (lambda l, i: (l, i, 0))

    def body(p_ref, o_ref):
        acc = p_ref[0, 0].astype(F32)
        for s in range(1, P):
            acc = acc + p_ref[0, s].astype(F32)
        o_ref[0] = acc.T if transpose else acc

    return pl.pallas_call(
        body, name=name, grid=(L, r // tr),
        in_specs=[pl.BlockSpec((1, P, tr, c), lambda l, i: (l, 0, i, 0))],
        out_specs=pl.BlockSpec(out_block, out_map),
        out_shape=jax.ShapeDtypeStruct((L, c, r) if transpose else (L, r, c), F32),
        compiler_params=_params("parallel", "parallel"),
    )(parts)


def adamw(w, g, m, v, name):
    R, C = w.shape
    tr = _tile(R, 512)

    def body(w_ref, g_ref, m_ref, v_ref, d_ref, mo_ref, vo_ref):
        d_ref[...], mo_ref[...], vo_ref[...] = _adamw_math(w_ref[...], g_ref[...], m_ref[...], v_ref[...])

    spec = pl.BlockSpec((tr, C), lambda i: (i, 0))
    return pl.pallas_call(
        body, name=name, grid=(R // tr,),
        in_specs=[spec] * 4, out_specs=[spec] * 3,
        out_shape=[jax.ShapeDtypeStruct((R, C), F32)] * 3,
        compiler_params=_params("parallel"),
    )(w, g, m, v)


def add_norm(h, y, g_post, g_pre, name):
    T, D = h.shape
    tm = _tile(T, 512)
    has_y = y is not None

    def body(*refs):
        if has_y:
            h_ref, y_ref, gp_ref, g_ref, ho_ref, hn_ref = refs
            hv = h_ref[...] + _rms(y_ref[...].astype(F32), gp_ref[...])
            ho_ref[...] = hv
        else:
            h_ref, g_ref, hn_ref = refs
            hv = h_ref[...]
        hn_ref[...] = _rms(hv, g_ref[...]).astype(BF16)

    row, vec = _rows(tm, D), _full((1, D))
    if has_y:
        return pl.pallas_call(
            body, name=name, grid=(T // tm,), in_specs=[row, row, vec, vec], out_specs=[row, row],
            out_shape=[jax.ShapeDtypeStruct((T, D), F32), jax.ShapeDtypeStruct((T, D), BF16)],
            compiler_params=_params("parallel"))(h, y, g_post, g_pre)
    hn = pl.pallas_call(
        body, name=name, grid=(T // tm,), in_specs=[row, vec], out_specs=row,
        out_shape=jax.ShapeDtypeStruct((T, D), BF16), compiler_params=_params("parallel"))(h, g_pre)
    return h, hn


def add_norm_loss(h, y, g_post, target, name):
    T, D = h.shape
    tm = _tile(T, 512)

    def body(h_ref, y_ref, gp_ref, t_ref, dh_ref, loss_ref):
        @pl.when(pl.program_id(0) == 0)
        def _():
            loss_ref[...] = jnp.zeros_like(loss_ref)
        err = h_ref[...] + _rms(y_ref[...].astype(F32), gp_ref[...]) - t_ref[...]
        dh_ref[...] = err * (1.0 / D)
        loss_ref[...] += jnp.sum(err * err)

    row = _rows(tm, D)
    return pl.pallas_call(
        body, name=name, grid=(T // tm,), in_specs=[row, row, _full((1, D)), row],
        out_specs=[row, _full((8, LANES))],
        out_shape=[jax.ShapeDtypeStruct((T, D), F32), jax.ShapeDtypeStruct((8, LANES), F32)],
        compiler_params=_params("arbitrary"))(h, y, g_post, target)


def norm_bwd(dh_in, pre, post, name):
    T, D = dh_in.shape
    tm = _tile(T, 512)
    row, vec = _rows(tm, D), _full((1, D))
    ins, in_specs, outs, out_specs, names = [dh_in], [row], [], [], []
    if pre is not None:
        ins += list(pre)
        in_specs += [row, row, vec]
        outs += [jax.ShapeDtypeStruct((T, D), F32), jax.ShapeDtypeStruct((1, D), F32)]
        out_specs += [row, vec]
        names += ["dh", "dg_pre"]
    if post is not None:
        ins += list(post)
        in_specs += [row, vec]
        outs += [jax.ShapeDtypeStruct((T, D), BF16), jax.ShapeDtypeStruct((1, D), F32)]
        out_specs += [row, vec]
        names += ["dy", "dg_post"]

    def body(*refs):
        refs = list(refs)
        first = pl.program_id(0) == 0
        dh = refs.pop(0)[...]
        if pre is not None:
            dpre_ref, x_ref, g_ref = refs[:3]
            refs = refs[3:]
        if post is not None:
            y_ref, gp_ref = refs[:2]
            refs = refs[2:]
        if pre is not None:
            dh_ref, dg_ref = refs[:2]
            refs = refs[2:]
            dx, dg = _rms_bwd(x_ref[...], g_ref[...], dpre_ref[...].astype(F32))
            dh = dh + dx
            dh_ref[...] = dh

            @pl.when(first)
            def _():
                dg_ref[...] = jnp.zeros_like(dg_ref)
            dg_ref[...] += dg
        if post is not None:
            dy_ref, dgp_ref = refs[:2]
            dy, dgp = _rms_bwd(y_ref[...].astype(F32), gp_ref[...], dh)
            dy_ref[...] = dy.astype(BF16)

            @pl.when(first)
            def _():
                dgp_ref[...] = jnp.zeros_like(dgp_ref)
            dgp_ref[...] += dgp

    res = pl.pallas_call(
        body, name=name, grid=(T // tm,), in_specs=in_specs, out_specs=out_specs, out_shape=outs,
        compiler_params=_params("arbitrary"))(*ins)
    return dict(zip(names, res))


_NT = (((1,), (1,)), ((), ()))
_TN = (((0,), (0,)), ((), ()))


def rowmm(a, w, transposed_w, bias, out_dtype, name, after=None):
    M, K = a.shape
    N = w.shape[0] if transposed_w else w.shape[1]
    tm = _tile(M, 256)
    tn = _tile(N, 512)

    def body(*refs):
        a_ref, w_ref = refs[:2]
        o_ref = refs[-1]
        av = a_ref[...]
        for j in range(N // tn):
            cols = pl.ds(j * tn, tn)
            if transposed_w:
                acc = lax.dot_general(av, w_ref[cols, :], _NT, preferred_element_type=F32)
            else:
                acc = jnp.dot(av, w_ref[:, cols], preferred_element_type=F32)
            if bias is not None:
                acc = acc + refs[2][:, cols]
            o_ref[:, cols] = acc.astype(out_dtype)

    ins, in_specs = [a, w], [_rows(tm, K), _full(w.shape)]
    if bias is not None:
        ins.append(bias)
        in_specs.append(_full((1, N)))
    if after is not None:
        ins.append(after)
        in_specs.append(pl.BlockSpec(memory_space=pl.ANY))
    return pl.pallas_call(
        body, name=name, grid=(M // tm,), in_specs=in_specs, out_specs=_rows(tm, N),
        out_shape=jax.ShapeDtypeStruct((M, N), out_dtype), compiler_params=_params("parallel"))(*ins)


def _rows_times_weight(a_ref, w_ref, acc_ref):
    N = w_ref.shape[1]
    tn = _tile(N, 512)
    av = a_ref[...]
    for j in range(N // tn):
        cols = pl.ds(j * tn, tn)
        acc_ref[:, cols] = jnp.dot(av, w_ref[:, cols], preferred_element_type=F32)


def mm_add_norm(a, w, h, g_post, g_pre, name):
    T, K = a.shape
    D = w.shape[1]
    tm = _tile(T, 256)

    def body(a_ref, w_ref, h_ref, gp_ref, g_ref, y_ref, ho_ref, hn_ref, acc_ref):
        _rows_times_weight(a_ref, w_ref, acc_ref)
        y = acc_ref[...]
        y_ref[...] = y.astype(BF16)
        hv = h_ref[...] + _rms(y, gp_ref[...])
        ho_ref[...] = hv
        hn_ref[...] = _rms(hv, g_ref[...]).astype(BF16)

    row, vec = _rows(tm, D), _full((1, D))
    return pl.pallas_call(
        body, name=name, grid=(T // tm,), in_specs=[_rows(tm, K), _full(w.shape), row, vec, vec],
        out_specs=[row, row, row],
        out_shape=[jax.ShapeDtypeStruct((T, D), BF16), jax.ShapeDtypeStruct((T, D), F32),
                   jax.ShapeDtypeStruct((T, D), BF16)],
        scratch_shapes=[pltpu.VMEM((tm, D), F32)], compiler_params=_params("parallel"))(a, w, h, g_post, g_pre)


def mm_add_norm_loss(a, w, h, g_post, target, name):
    T, K = a.shape
    D = w.shape[1]
    tm = _tile(T, 256)

    def body(a_ref, w_ref, h_ref, gp_ref, t_ref, dh_ref, dy_ref, loss_ref, dgp_ref, acc_ref):
        @pl.when(pl.program_id(0) == 0)
        def _():
            loss_ref[...] = jnp.zeros_like(loss_ref)
            dgp_ref[...] = jnp.zeros_like(dgp_ref)
        _rows_times_weight(a_ref, w_ref, acc_ref)
        y = acc_ref[...]
        err = h_ref[...] + _rms(y, gp_ref[...]) - t_ref[...]
        dh = err * (1.0 / D)
        dh_ref[...] = dh
        loss_ref[...] += jnp.sum(err * err)
        dy, dgp = _rms_bwd(y, gp_ref[...], dh)
        dy_ref[...] = dy.astype(BF16)
        dgp_ref[...] += dgp

    row, vec = _rows(tm, D), _full((1, D))
    return pl.pallas_call(
        body, name=name, grid=(T // tm,), in_specs=[_rows(tm, K), _full(w.shape), row, vec, row],
        out_specs=[row, row, _full((8, LANES)), vec],
        out_shape=[jax.ShapeDtypeStruct((T, D), F32), jax.ShapeDtypeStruct((T, D), BF16),
                   jax.ShapeDtypeStruct((8, LANES), F32), jax.ShapeDtypeStruct((1, D), F32)],
        scratch_shapes=[pltpu.VMEM((tm, D), F32)], compiler_params=_params("arbitrary"))(a, w, h, g_post, target)


def mm_norm_bwd(a, w, dh_in, x, g_pre, post, name, after):
    T, K = a.shape
    D = w.shape[1]
    tm = _tile(T, 256)
    row, vec = _rows(tm, D), _full((1, D))
    ins, in_specs = [a, w, dh_in, x, g_pre], [_rows(tm, K), _full(w.shape), row, row, vec]
    outs = [jax.ShapeDtypeStruct((T, D), F32), jax.ShapeDtypeStruct((1, D), F32)]
    out_specs, names = [row, vec], ["dh", "dg_pre"]
    if post is not None:
        ins += list(post)
        in_specs += [row, vec]
        outs += [jax.ShapeDtypeStruct((T, D), BF16), jax.ShapeDtypeStruct((1, D), F32)]
        out_specs += [row, vec]
        names += ["dy", "dg_post"]
    n_in = len(ins) + 1

    def body(*refs):
        a_ref, w_ref, dh_ref_in, x_ref, g_ref = refs[:5]
        out_refs, acc_ref = refs[n_in:-1], refs[-1]
        first = pl.program_id(0) == 0
        _rows_times_weight(a_ref, w_ref, acc_ref)
        dx, dg = _rms_bwd(x_ref[...], g_ref[...], acc_ref[...])
        dh = dh_ref_in[...] + dx
        out_refs[0][...] = dh

        @pl.when(first)
        def _():
            for r in out_refs[1::2]:
                r[...] = jnp.zeros_like(r)
        out_refs[1][...] += dg
        if post is not None:
            y_ref, gp_ref = refs[5:7]
            dy, dgp = _rms_bwd(y_ref[...].astype(F32), gp_ref[...], dh)
            out_refs[2][...] = dy.astype(BF16)
            out_refs[3][...] += dgp

    res = pl.pallas_call(
        body, name=name, grid=(T // tm,), in_specs=in_specs + [pl.BlockSpec(memory_space=pl.ANY)],
        out_specs=out_specs, out_shape=outs, scratch_shapes=[pltpu.VMEM((tm, D), F32)],
        compiler_params=_params("arbitrary"))(*ins, after)
    return dict(zip(names, res))


def grad_mm(a, b, name):
    T, N = a.shape
    K = b.shape[1]
    tn = _tile(N, 1408)
    tt = _tile(T, 1024)

    def body(a_ref, b_ref, o_ref, acc_ref):
        t = pl.program_id(1)

        @pl.when(t == 0)
        def _():
            acc_ref[...] = jnp.zeros_like(acc_ref)
        acc_ref[...] += lax.dot_general(a_ref[...], b_ref[...], _TN, preferred_element_type=F32)

        @pl.when(t == pl.num_programs(1) - 1)
        def _():
            o_ref[...] = acc_ref[...].astype(BF16)

    return pl.pallas_call(
        body, name=name, grid=(N // tn, T // tt),
        in_specs=[pl.BlockSpec((tt, tn), lambda j, t: (t, j)), pl.BlockSpec((tt, K), lambda j, t: (t, 0))],
        out_specs=pl.BlockSpec((tn, K), lambda j, t: (j, 0)),
        out_shape=jax.ShapeDtypeStruct((N, K), BF16),
        scratch_shapes=[pltpu.VMEM((tn, K), F32)],
        compiler_params=_params("parallel", "arbitrary"))(a, b)


def ffn_up(fn, w_gu_t, name):
    T, D = fn.shape
    F = w_gu_t.shape[0] // 2
    tm = _tile(T, 256)
    tn = _tile(F, 512)

    def body(a_ref, w_ref, act_ref, part_ref):
        av = a_ref[...]
        for j in range(F // tn):
            g = lax.dot_general(av, w_ref[pl.ds(j * tn, tn), :], _NT, preferred_element_type=F32)
            up = lax.dot_general(av, w_ref[pl.ds(F + j * tn, tn), :], _NT, preferred_element_type=F32)
            sg = _sigmoid(g)
            silu = g * sg
            act_ref[:, pl.ds(j * tn, tn)] = (silu * up).astype(BF16)
            part_ref[:, pl.ds(j * tn, tn)] = (up * (sg + silu * (1.0 - sg))).astype(BF16)
            part_ref[:, pl.ds(F + j * tn, tn)] = silu.astype(BF16)

    return pl.pallas_call(
        body, name=name, grid=(T // tm,), in_specs=[_rows(tm, D), _full(w_gu_t.shape)],
        out_specs=[_rows(tm, F), _rows(tm, 2 * F)],
        out_shape=[jax.ShapeDtypeStruct((T, F), BF16), jax.ShapeDtypeStruct((T, 2 * F), BF16)],
        compiler_params=_params("parallel"))(fn, w_gu_t)


def ffn_dact(df, w_down, partials, name):
    T, D = df.shape
    F = w_down.shape[0]
    tm = _tile(T, 256)
    tn = _tile(F, 512)

    def body(d_ref, w_ref, part_ref, o_ref):
        dv = d_ref[...]
        for j in range(F // tn):
            dact = lax.dot_general(dv, w_ref[pl.ds(j * tn, tn), :], _NT, preferred_element_type=F32)
            for cols in (pl.ds(j * tn, tn), pl.ds(F + j * tn, tn)):
                o_ref[:, cols] = (dact * part_ref[:, cols].astype(F32)).astype(BF16)

    return pl.pallas_call(
        body, name=name, grid=(T // tm,), in_specs=[_rows(tm, D), _full(w_down.shape), _rows(tm, 2 * F)],
        out_specs=_rows(tm, 2 * F), out_shape=jax.ShapeDtypeStruct((T, 2 * F), BF16),
        compiler_params=_params("parallel"))(df, w_down, partials)


def _layernorm_stats(v):
    mu = jnp.mean(v, axis=-1, keepdims=True)
    cen = v - mu
    rstd = lax.rsqrt(jnp.mean(cen * cen, axis=-1, keepdims=True) + LN_EPS)
    return cen * rstd, rstd


def gmlp_fwd(a, ln_g, ln_b, w_mask, b_s_t, name):
    T, W2 = a.shape
    W = W2 // 2
    G = w_mask.shape[0]
    C = W // G

    def body(a_ref, g_ref, b_ref, w_ref, bs_ref, o_ref):
        xhat, _ = _layernorm_stats(_gelu(a_ref[:, W:].astype(F32)))
        vln = (xhat * g_ref[...] + b_ref[...]).astype(BF16)
        for g in range(G):
            cols = pl.ds(g * C, C)
            sv = jnp.dot(w_ref[g], vln[:, g * C:(g + 1) * C], preferred_element_type=F32) + bs_ref[:, g:g + 1]
            o_ref[:, cols] = (_gelu(a_ref[:, cols].astype(F32)) * sv).astype(BF16)

    return pl.pallas_call(
        body, name=name, grid=(T // CHUNK,),
        in_specs=[_rows(CHUNK, W2), _full((1, W)), _full((1, W)), _full(w_mask.shape), _full(b_s_t.shape)],
        out_specs=_rows(CHUNK, W), out_shape=jax.ShapeDtypeStruct((T, W), BF16),
        compiler_params=_params("parallel"))(a, ln_g, ln_b, w_mask, b_s_t)


def gmlp_bwd(a, dgated, ln_g, ln_b, w_mask, w_mask_t, b_s_t, group_onehot, name):
    T, W2 = a.shape
    W = W2 // 2
    G = w_mask.shape[0]
    C = W // G

    def body(a_ref, dg_ref, g_ref, b_ref, w_ref, wt_ref, bs_ref, e_ref,
             da_ref, dws_ref, dbs_ref, dlg_ref, dlb_ref, dbin_ref, dvln_ref):
        @pl.when(pl.program_id(0) == 0)
        def _():
            for r in (dws_ref, dbs_ref, dlg_ref, dlb_ref, dbin_ref):
                r[...] = jnp.zeros_like(r)
        row = lax.broadcasted_iota(jnp.int32, (CHUNK, CHUNK), 0)
        col = lax.broadcasted_iota(jnp.int32, (CHUNK, CHUNK), 1)
        causal = col <= row
        gelu_v, gelu_grad_v = _gelu_and_grad(a_ref[:, W:].astype(F32))
        xhat, rstd = _layernorm_stats(gelu_v)
        vln = (xhat * g_ref[...] + b_ref[...]).astype(BF16)
        for g in range(G):
            cols = pl.ds(g * C, C)
            vg = vln[:, g * C:(g + 1) * C]
            gelu_u, gelu_grad_u = _gelu_and_grad(a_ref[:, cols].astype(F32))
            dgated = dg_ref[:, cols].astype(F32)
            dsv = (dgated * gelu_u).astype(BF16)
            dbs_ref[...] += jnp.dot(dsv, e_ref[cols, :], preferred_element_type=F32)
            sv = jnp.dot(w_ref[g], vg, preferred_element_type=F32) + bs_ref[:, g:g + 1]
            dau = dgated * sv * gelu_grad_u
            da_ref[:, cols] = dau.astype(BF16)
            dbin_ref[:, cols] += jnp.sum(dau, axis=0, keepdims=True)
            dws_ref[g] += jnp.where(causal, lax.dot_general(dsv, vg, _NT, preferred_element_type=F32), 0.0)
            dvln_ref[:, cols] = jnp.dot(wt_ref[g], dsv, preferred_element_type=F32)
        dvln = dvln_ref[...]
        dlg_ref[...] += jnp.sum(dvln * xhat, axis=0, keepdims=True)
        dlb_ref[...] += jnp.sum(dvln, axis=0, keepdims=True)
        dxhat = dvln * g_ref[...]
        dv = rstd * (dxhat - jnp.mean(dxhat, axis=-1, keepdims=True)
                     - xhat * jnp.mean(dxhat * xhat, axis=-1, keepdims=True))
        dav = dv * gelu_grad_v
        da_ref[:, W:] = dav.astype(BF16)
        dbin_ref[:, W:] += jnp.sum(dav, axis=0, keepdims=True)

    return pl.pallas_call(
        body, name=name, grid=(T // CHUNK,),
        in_specs=[_rows(CHUNK, W2), _rows(CHUNK, W), _full((1, W)), _full((1, W)), _full(w_mask.shape),
                  _full(w_mask_t.shape), _full(b_s_t.shape), _full(group_onehot.shape)],
        out_specs=[_rows(CHUNK, W2), _full((G, CHUNK, CHUNK)), _full((CHUNK, LANES)), _full((1, W)),
                   _full((1, W)), _full((1, W2))],
        out_shape=[jax.ShapeDtypeStruct((T, W2), BF16), jax.ShapeDtypeStruct((G, CHUNK, CHUNK), F32),
                   jax.ShapeDtypeStruct((CHUNK, LANES), F32), jax.ShapeDtypeStruct((1, W), F32),
                   jax.ShapeDtypeStruct((1, W), F32), jax.ShapeDtypeStruct((1, W2), F32)],
        scratch_shapes=[pltpu.VMEM((CHUNK, W), F32)],
        compiler_params=_params("arbitrary"))(a, dgated, ln_g, ln_b, w_mask, w_mask_t, b_s_t, group_onehot)


def rope_tables(pos, inv_freq_row, name):
    T = pos.shape[0]
    tm = _tile(T, 512)

    def body(p_ref, f_ref, c_ref, s1_ref, s2_ref):
        ang = p_ref[...].astype(F32) * f_ref[...]
        lane = lax.broadcasted_iota(jnp.int32, (tm, LANES), 1) % HEAD_DIM
        sin = jnp.sin(ang)
        c_ref[...] = jnp.cos(ang)
        s1_ref[...] = jnp.where(lane < _HALF, -sin, 0.0)
        s2_ref[...] = jnp.where((lane >= _HALF) & (lane < ROPE_DIM), sin, 0.0)

    tab = _rows(tm, LANES)
    return pl.pallas_call(
        body, name=name, grid=(T // tm,), in_specs=[_rows(tm, 1), _full((1, LANES))], out_specs=[tab] * 3,
        out_shape=[jax.ShapeDtypeStruct((T, LANES), F32)] * 3, compiler_params=_params("parallel"))(pos, inv_freq_row)


_HALF = ROPE_DIM // 2


def _slabs(x):
    return [x[:, b * LANES:(b + 1) * LANES] for b in range(x.shape[1] // LANES)]


def _rotate(x, c, s1, s2):
    return [xs * c + pltpu.roll(xs, LANES - _HALF, 1) * s1 + pltpu.roll(xs, _HALF, 1) * s2 for xs in _slabs(x)]


def _rotate_transposed(dy, c, s1, s2):
    return [ds * c + pltpu.roll(ds * s1, _HALF, 1) + pltpu.roll(ds * s2, LANES - _HALF, 1) for ds in _slabs(dy)]


def rope_fwd(qkv, tabs, q_width, kv_width, name):
    T, QKV = qkv.shape
    tm = _tile(T, 512)
    QK = q_width + kv_width
    scale = HEAD_DIM ** -0.5

    def body(x_ref, c_ref, s1_ref, s2_ref, o_ref):
        x = x_ref[:, :QK].astype(F32)
        slabs = _rotate(x, c_ref[...], s1_ref[...], s2_ref[...])
        for b, y in enumerate(slabs):
            if b * LANES < q_width:
                y = y * scale
            o_ref[:, b * LANES:(b + 1) * LANES] = y.astype(BF16)

    tab = _rows(tm, LANES)
    return pl.pallas_call(
        body, name=name, grid=(T // tm,), in_specs=[_rows(tm, QKV), tab, tab, tab], out_specs=_rows(tm, QK),
        out_shape=jax.ShapeDtypeStruct((T, QK), BF16), compiler_params=_params("parallel"))(qkv, *tabs)


def rope_bwd(dq, dk, dv, tabs, name):
    T, Q = dq.shape
    KV = dk.shape[1]
    tm = _tile(T, 512)
    scale = HEAD_DIM ** -0.5

    def body(dq_ref, dk_ref, dv_ref, c_ref, s1_ref, s2_ref, o_ref, b_ref):
        @pl.when(pl.program_id(0) == 0)
        def _():
            b_ref[...] = jnp.zeros_like(b_ref)
        tabs_v = (c_ref[...], s1_ref[...], s2_ref[...])
        pieces = [s * scale for s in _rotate_transposed(dq_ref[...], *tabs_v)]
        pieces += _rotate_transposed(dk_ref[...], *tabs_v)
        pieces += _slabs(dv_ref[...])
        for b, y in enumerate(pieces):
            cols = pl.ds(b * LANES, LANES)
            o_ref[:, cols] = y.astype(BF16)
            b_ref[:, cols] += jnp.sum(y, axis=0, keepdims=True)

    tab = _rows(tm, LANES)
    return pl.pallas_call(
        body, name=name, grid=(T // tm,), in_specs=[_rows(tm, Q), _rows(tm, KV), _rows(tm, KV), tab, tab, tab],
        out_specs=[_rows(tm, Q + 2 * KV), _full((1, Q + 2 * KV))],
        out_shape=[jax.ShapeDtypeStruct((T, Q + 2 * KV), BF16), jax.ShapeDtypeStruct((1, Q + 2 * KV), F32)],
        compiler_params=_params("arbitrary"))(dq, dk, dv, *tabs)


def _band_mask(n, heads=1):
    qi = lax.broadcasted_iota(jnp.int32, (heads * CHUNK, 2 * CHUNK), 0) % CHUNK
    sj = lax.broadcasted_iota(jnp.int32, (heads * CHUNK, 2 * CHUNK), 1)
    return (sj > qi) & (sj <= qi + CHUNK) & ((n > 0) | (sj >= CHUNK))


def _kv_head(kc_ref, kp_ref, vc_ref, vp_ref, kh):
    lanes = slice(kh * HEAD_DIM, (kh + 1) * HEAD_DIM)
    return (jnp.concatenate([kp_ref[:, lanes], kc_ref[:, lanes]], axis=0),
            jnp.concatenate([vp_ref[:, lanes], vc_ref[:, lanes]], axis=0))


def _head_probs(q, kk, valid, sink):
    s = jnp.where(valid, lax.dot_general(q, kk, _NT, preferred_element_type=F32), NEG_INF)
    m = jnp.maximum(jnp.max(s, axis=-1, keepdims=True), sink)
    p = jnp.exp(s - m)
    e_sink = jnp.exp(sink - m)
    return p, 1.0 / (jnp.sum(p, axis=-1, keepdims=True) + e_sink), e_sink


def _attn_specs(q_width, kv_width, order):
    qb = q_width // kv_width
    prev = lambda i: jnp.maximum(order(i) - 1, 0)
    return [pl.BlockSpec((CHUNK, q_width), lambda i: (order(i), 0)),
            pl.BlockSpec((CHUNK, kv_width), lambda i: (order(i), qb)),
            pl.BlockSpec((CHUNK, kv_width), lambda i: (prev(i), qb)),
            pl.BlockSpec((CHUNK, kv_width), lambda i: (order(i), qb + 1)),
            pl.BlockSpec((CHUNK, kv_width), lambda i: (prev(i), qb + 1))]


def attn_fwd(qk, qkv, sinks, q_width, kv_width, name):
    T = qk.shape[0]
    group = q_width // kv_width

    def body(q_ref, kc_ref, kp_ref, vc_ref, vp_ref, sink_ref, o_ref):
        valid = _band_mask(pl.program_id(0))
        for kh in range(kv_width // HEAD_DIM):
            kk, vv = _kv_head(kc_ref, kp_ref, vc_ref, vp_ref, kh)
            for g in range(group):
                h = kh * group + g
                lanes = slice(h * HEAD_DIM, (h + 1) * HEAD_DIM)
                p, inv, _ = _head_probs(q_ref[:, lanes], kk, valid, sink_ref[h])
                o_ref[:, lanes] = (jnp.dot(p.astype(BF16), vv, preferred_element_type=F32) * inv).astype(BF16)

    specs = _attn_specs(q_width, kv_width, lambda i: i)
    return pl.pallas_call(
        body, name=name, grid=(T // CHUNK,),
        in_specs=specs + [pl.BlockSpec(memory_space=pltpu.SMEM)],
        out_specs=_rows(CHUNK, q_width), out_shape=jax.ShapeDtypeStruct((T, q_width), BF16),
        compiler_params=_params("parallel"))(qk, qk, qk, qkv, qkv, sinks)


def attn_bwd(qk, qkv, do, sinks, q_width, kv_width, name):
    T = qk.shape[0]
    NB = T // CHUNK
    group = q_width // kv_width

    def body(q_ref, kc_ref, kp_ref, vc_ref, vp_ref, do_ref, sink_ref, dq_ref, dk_ref, dv_ref, ds_ref, ck_ref, cv_ref):
        i = pl.program_id(0)
        n = NB - 1 - i

        @pl.when(i == 0)
        def _():
            ck_ref[...] = jnp.zeros_like(ck_ref)
            cv_ref[...] = jnp.zeros_like(cv_ref)
            ds_ref[...] = jnp.zeros_like(ds_ref)
        lane = lax.broadcasted_iota(jnp.int32, (1, LANES), 1)
        dsink_row = jnp.zeros((1, LANES), F32)
        valid = _band_mask(n, group)
        head = lax.broadcasted_iota(jnp.int32, (group * CHUNK, 1), 0) // CHUNK
        for kh in range(kv_width // HEAD_DIM):
            kk, vv = _kv_head(kc_ref, kp_ref, vc_ref, vp_ref, kh)
            heads = [slice((kh * group + g) * HEAD_DIM, (kh * group + g + 1) * HEAD_DIM) for g in range(group)]
            q = jnp.concatenate([q_ref[:, hs] for hs in heads], axis=0)
            do = jnp.concatenate([do_ref[:, hs] for hs in heads], axis=0)
            sink = jnp.zeros((group * CHUNK, 1), F32)
            for g in range(group):
                sink = jnp.where(head == g, sink_ref[kh * group + g], sink)
            p, inv, e_sink = _head_probs(q, kk, valid, sink)
            p = p * inv
            dp = lax.dot_general(do, vv, _NT, preferred_element_type=F32)
            delta = jnp.sum(p * dp, axis=-1, keepdims=True)
            ds = (p * (dp - delta)).astype(BF16)
            dsink = -e_sink * inv * delta
            dq = jnp.dot(ds, kk, preferred_element_type=F32)
            for g, hs in enumerate(heads):
                rows = slice(g * CHUNK, (g + 1) * CHUNK)
                dsink_row = dsink_row + jnp.where(lane == kh * group + g, jnp.sum(dsink[rows]), 0.0)
                dq_ref[:, hs] = dq[rows]
            dkk = lax.dot_general(ds, q, _TN, preferred_element_type=F32)
            dvv = lax.dot_general(p.astype(BF16), do, _TN, preferred_element_type=F32)
            lanes = slice(kh * HEAD_DIM, (kh + 1) * HEAD_DIM)
            dk_ref[:, lanes] = dkk[CHUNK:] + ck_ref[:, lanes]
            dv_ref[:, lanes] = dvv[CHUNK:] + cv_ref[:, lanes]
            ck_ref[:, lanes] = dkk[:CHUNK]
            cv_ref[:, lanes] = dvv[:CHUNK]
        ds_ref[0:1, :] += dsink_row

    order = lambda i: NB - 1 - i
    specs = _attn_specs(q_width, kv_width, order)
    kv_out = pl.BlockSpec((CHUNK, kv_width), lambda i: (order(i), 0))
    q_rows = pl.BlockSpec((CHUNK, q_width), lambda i: (order(i), 0))
    return pl.pallas_call(
        body, name=name, grid=(NB,),
        in_specs=specs + [q_rows, pl.BlockSpec(memory_space=pltpu.SMEM)],
        out_specs=[q_rows, kv_out, kv_out, _full((8, LANES))],
        out_shape=[jax.ShapeDtypeStruct((T, q_width), F32), jax.ShapeDtypeStruct((T, kv_width), F32),
                   jax.ShapeDtypeStruct((T, kv_width), F32), jax.ShapeDtypeStruct((8, LANES), F32)],
        scratch_shapes=[pltpu.VMEM((CHUNK, kv_width), F32), pltpu.VMEM((CHUNK, kv_width), F32)],
        compiler_params=_params("arbitrary"))(qk, qk, qk, qkv, qkv, do, sinks)


def _blocked(w):
    return w.reshape(N_DEV, w.shape[0] // N_DEV, w.shape[1])


def kernel(x, positions, pre_mix_g, post_mix_g, pre_ffn_g, post_ffn_g, a_w_in, a_b_in, a_ln_g, a_ln_b, a_w_s, a_b_s, a_w_out, b_w_qkv, b_b_qkv, b_sinks, b_w_o, ffn_w_gu, ffn_w_down, loss_target, m_pre_mix_g, m_post_mix_g, m_pre_ffn_g, m_post_ffn_g, m_a_w_in, m_a_b_in, m_a_ln_g, m_a_ln_b, m_a_w_s, m_a_b_s, m_a_w_out, m_b_w_qkv, m_b_b_qkv, m_b_sinks, m_b_w_o, m_ffn_w_gu, m_ffn_w_down, v_pre_mix_g, v_post_mix_g, v_pre_ffn_g, v_post_ffn_g, v_a_w_in, v_a_b_in, v_a_ln_g, v_a_ln_b, v_a_w_s, v_a_b_s, v_a_w_out, v_b_w_qkv, v_b_b_qkv, v_b_sinks, v_b_w_o, v_ffn_w_gu, v_ffn_w_down):
    weights = dict(pre_mix_g=pre_mix_g, post_mix_g=post_mix_g, pre_ffn_g=pre_ffn_g, post_ffn_g=post_ffn_g,
                   a_w_in=a_w_in, a_b_in=a_b_in, a_ln_g=a_ln_g, a_ln_b=a_ln_b, a_w_s=a_w_s, a_b_s=a_b_s,
                   a_w_out=a_w_out, b_w_qkv=b_w_qkv, b_b_qkv=b_b_qkv, b_sinks=b_sinks, b_w_o=b_w_o,
                   ffn_w_gu=ffn_w_gu, ffn_w_down=ffn_w_down)
    mom_m = dict(pre_mix_g=m_pre_mix_g, post_mix_g=m_post_mix_g, pre_ffn_g=m_pre_ffn_g, post_ffn_g=m_post_ffn_g,
                 a_w_in=m_a_w_in, a_b_in=m_a_b_in, a_ln_g=m_a_ln_g, a_ln_b=m_a_ln_b, a_w_s=m_a_w_s, a_b_s=m_a_b_s,
                 a_w_out=m_a_w_out, b_w_qkv=m_b_w_qkv, b_b_qkv=m_b_b_qkv, b_sinks=m_b_sinks, b_w_o=m_b_w_o,
                 ffn_w_gu=m_ffn_w_gu, ffn_w_down=m_ffn_w_down)
    mom_v = dict(pre_mix_g=v_pre_mix_g, post_mix_g=v_post_mix_g, pre_ffn_g=v_pre_ffn_g, post_ffn_g=v_post_ffn_g,
                 a_w_in=v_a_w_in, a_b_in=v_a_b_in, a_ln_g=v_a_ln_g, a_ln_b=v_a_ln_b, a_w_s=v_a_w_s, a_b_s=v_a_b_s,
                 a_w_out=v_a_w_out, b_w_qkv=v_b_w_qkv, b_b_qkv=v_b_b_qkv, b_sinks=v_b_sinks, b_w_o=v_b_w_o,
                 ffn_w_gu=v_ffn_w_gu, ffn_w_down=v_ffn_w_down)
    names = list(weights)
    column_sharded = ("a_w_in", "b_w_qkv", "ffn_w_gu")
    row_sharded = ("a_w_out", "b_w_o", "ffn_w_down")
    big = column_sharded + row_sharded

    T, D = x.shape[1], x.shape[2]
    depth = pre_mix_g.shape[0]
    n_heads = b_sinks.shape[1]
    q_width = n_heads * HEAD_DIM
    kv_width = N_KV_HEADS * HEAD_DIM
    G = a_w_s.shape[1]
    W = a_ln_g.shape[1]
    device = _index(_position())

    dev = device.reshape(1).astype(jnp.int32)

    def layer_keys(i):
        mixer = ("a_w_in", "a_w_out") if i % 2 == 0 else ("b_w_qkv", "b_w_o")
        return [(k, i // 2) for k in mixer] + [("ffn_w_gu", i), ("ffn_w_down", i)]

    as_view = ("b_w_qkv", "ffn_w_gu")
    in_kernel = ("a_w_in",)
    view = lambda k, t: jnp.swapaxes(t, 1, 2) if k in as_view else t
    w_view = {k: view(k, weights[k]) for k in big}
    m_view = {k: view(k, mom_m[k]) for k in big}
    v_view = {k: view(k, mom_v[k]) for k in big}

    order = [(i, k, l) for i in range(depth) for k, l in layer_keys(i)]
    bias_land = lax.dynamic_update_slice(
        jnp.zeros((N_DEV,) + b_b_qkv.shape, F32), b_b_qkv[None], (device, 0, 0))
    h, hn = add_norm(x[0], None, None, pre_mix_g[0][None], "norm_first")
    first = len(layer_keys(0))
    in_gather, token = {}, ()
    for name, lo, hi in (("gather_start_first", 0, first), ("gather_start_rest", first, len(order))):
        groups = [[prep_weight(w_view[k], l, k in in_kernel, dev, "prep_" + k, after=token)] for _, k, l in order[lo:hi]]
        if lo:
            groups[0].append(bias_land)
        gather_sems, flat_lands, token = gather_start(groups, name)
        token = (token, hn)
        for (i, k, _), grp, sems in zip(order[lo:hi], groups, gather_sems):
            in_gather[(i, k)] = ([flat_lands.pop(0) for _ in grp], sems)
    all_started = token[0]
    gathered = {}

    def weight(i, k, *after):
        if (i, k) not in gathered:
            lands, sems = in_gather[(i, k)]
            gathered[(i, k)] = gather_wait(lands, sems, after, f"gather_wait_{k}_{i}")
        g = gathered[(i, k)][0]
        return g.reshape(N_DEV * g.shape[1], g.shape[2])

    causal = jnp.tril(jnp.ones((CHUNK, CHUNK), dtype=bool))
    w_mask = jnp.where(causal[None, None], a_w_s, 0.0).astype(BF16)
    w_mask_t = jnp.swapaxes(w_mask, 2, 3)
    b_s_t = jnp.swapaxes(a_b_s, 1, 2)
    group_onehot = (jnp.arange(W)[:, None] // (W // G) == jnp.arange(LANES)[None, :]).astype(BF16)
    lane = jnp.arange(LANES) % HEAD_DIM
    inv_freq = ROPE_THETA ** (-jnp.arange(0, ROPE_DIM, 2, dtype=F32) / ROPE_DIM)
    inv_freq_row = jnp.where(lane < ROPE_DIM, inv_freq[lane % (ROPE_DIM // 2)], 0.0)[None, :].astype(F32)
    tabs = rope_tables(positions.reshape(T, 1), inv_freq_row, "rope_tables")

    saved = []
    for i in range(depth):
        j = i // 2
        s = dict(h=h, hn=hn)
        if i % 2 == 0:
            s["a"] = rowmm(hn, weight(i, "a_w_in", hn, all_started), True, a_b_in[j][None], BF16, "gmlp_in")
            s["gated"] = gmlp_fwd(s["a"], a_ln_g[j][None], a_ln_b[j][None], w_mask[j], b_s_t[j], "gmlp_gate")
            mixed, w_mix = s["gated"], weight(i, "a_w_out", s["gated"])
        else:
            w_qkv = weight(i, "b_w_qkv", hn)
            b_qkv_full = jnp.swapaxes(gathered[(1, "b_w_qkv")][1], 0, 1).reshape(b_b_qkv.shape[0], 1, -1)
            s["qkv"] = rowmm(hn, w_qkv, True, b_qkv_full[j], BF16, "attn_qkv")
            s["qk"] = rope_fwd(s["qkv"], tabs, q_width, kv_width, "attn_rope")
            s["o"] = attn_fwd(s["qk"], s["qkv"], b_sinks[j], q_width, kv_width, "attn_core")
            mixed, w_mix = s["o"], weight(i, "b_w_o", s["o"])
        s["mix"], s["h2"], s["fn"] = mm_add_norm(mixed, w_mix, h, post_mix_g[i][None], pre_ffn_g[i][None], "mixer_out")
        s["act"], s["act_partials"] = ffn_up(s["fn"], weight(i, "ffn_w_gu", s["fn"]), "ffn_up")
        w_down = weight(i, "ffn_w_down", s["act"])
        saved.append(s)
        if i + 1 < depth:
            s["f"], h, hn = mm_add_norm(s["act"], w_down, s["h2"], post_ffn_g[i][None], pre_mix_g[i + 1][None], "ffn_down")
    small = {k: [None] * weights[k].shape[0] for k in names if k not in big}
    dh, df, loss_sum, small["post_ffn_g"][depth - 1] = mm_add_norm_loss(
        saved[-1]["act"], w_down, saved[-1]["h2"], post_ffn_g[depth - 1][None], loss_target[0], "ffn_down_loss")
    loss = lax.psum(loss_sum[0, 0] * (0.5 / D), ("x", "y", "c"))

    grads = {k: [None] * weights[k].shape[0] for k in big}
    in_flight = []

    def send_grads(keys, tag):
        sems, bufs, token = exchange_start([_blocked(grads[k][l]) for k, l in keys], "exchange_start_" + tag)
        in_flight.append((keys, sems, bufs, tag))
        return token

    for i in reversed(range(depth)):
        j = i // 2
        s = saved[i]
        dgu = ffn_dact(df, weight(i, "ffn_w_down"), s["act_partials"], "ffn_dact")
        grads["ffn_w_down"][i] = grad_mm(s["act"], df, "ffn_down_grad")
        grads["ffn_w_gu"][i] = grad_mm(dgu, s["fn"], "ffn_up_grad")
        sent = send_grads(layer_keys(i)[2:], f"ffn_{i}")
        r = mm_norm_bwd(dgu, weight(i, "ffn_w_gu"), dh, s["h2"], pre_ffn_g[i][None],
                        (s["mix"], post_mix_g[i][None]), "ffn_dx", sent)
        dh, dmix = r["dh"], r["dy"]
        small["pre_ffn_g"][i], small["post_mix_g"][i] = r["dg_pre"], r["dg_post"]
        if i % 2 == 0:
            grads["a_w_out"][j] = grad_mm(s["gated"], dmix, "gmlp_out_grad")
            dgated = rowmm(dmix, weight(i, "a_w_out"), True, None, BF16, "gmlp_dgated")
            dmixed, dws, dbs, dlg, dlb, dbin = gmlp_bwd(
                s["a"], dgated, a_ln_g[j][None], a_ln_b[j][None], w_mask[j], w_mask_t[j], b_s_t[j], group_onehot,
                "gmlp_gate_bwd")
            small["a_w_s"][j], small["a_b_s"][j] = dws, dbs[:, :G].T
            small["a_ln_g"][j], small["a_ln_b"][j], small["a_b_in"][j] = dlg[0], dlb[0], dbin[0]
            grads["a_w_in"][j] = grad_mm(dmixed, s["hn"], "gmlp_in_grad")
            w_in = weight(i, "a_w_in")
        else:
            grads["b_w_o"][j] = grad_mm(s["o"], dmix, "attn_out_grad")
            do = rowmm(dmix, weight(i, "b_w_o"), True, None, BF16, "attn_do")
            dq, dk, dv, dsink = attn_bwd(s["qk"], s["qkv"], do, b_sinks[j], q_width, kv_width, "attn_core_bwd")
            dmixed, dbias = rope_bwd(dq, dk, dv, tabs, "attn_rope_bwd")
            small["b_sinks"][j], small["b_b_qkv"][j] = dsink[0, :n_heads], dbias[0]
            grads["b_w_qkv"][j] = grad_mm(dmixed, s["hn"], "attn_qkv_grad")
            w_in = weight(i, "b_w_qkv")
        sent = send_grads(layer_keys(i)[:2], f"mixer_{i}")
        post = (saved[i - 1]["f"], post_ffn_g[i - 1][None]) if i > 0 else None
        r = mm_norm_bwd(dmixed, w_in, dh, s["h"], pre_mix_g[i][None], post, "mixer_dx", sent)
        dh, small["pre_mix_g"][i] = r["dh"], r["dg_pre"]
        if i > 0:
            df, small["post_ffn_g"][i - 1] = r["dy"], r["dg_post"]
    grad_x = dh[None]

    small_names = [k for k in names if k not in big]
    partial = {k: jnp.stack([p.reshape(-1) for p in small[k]]) for k in small_names}
    sizes = [partial[k].size for k in small_names]
    total = sum(sizes)
    padded = -(-total // (LANES * LANES)) * (LANES * LANES)
    pack = lambda vals: jnp.concatenate([v.reshape(-1) for v in vals] + [jnp.zeros((padded - total,), F32)])
    packed = pack([partial[k] for k in small_names]).reshape(1, padded // LANES, LANES)
    small_sems, small_lands, small_sent = gather_start(
        [[prep_weight(packed, 0, False, dev, "place_small_grads", F32)]], "gather_start_small")

    stacked = {k: [lax.empty(w_view[k].shape, F32) for _ in range(4)] for k in big}
    after = (small_sent, dh)
    for keys, sems, bufs, tag in in_flight:
        partials, landed = exchange_wait(bufs, sems, after, "exchange_wait_" + tag)
        for (k, l), part, land in zip(keys, partials, landed):
            stacked[k] = sum_adamw(part, land, w_view[k], m_view[k], v_view[k], stacked[k], l,
                                   k in in_kernel, dev, "adamw_" + k)
        after = tuple(stacked[k][0] for k, _ in keys)
    out_g, out_d, out_m, out_v = {}, {}, {}, {}
    for k in big:
        out_g[k], out_d[k], out_m[k], out_v[k] = [view(k, t) for t in stacked[k]]

    gathered_small = gather_wait(small_lands, small_sems[0], after, "gather_wait_small")[0]
    summed = sum_parts(gathered_small[None], False, "sum_small")[0].reshape(-1)
    offsets = [sum(sizes[:q]) for q in range(len(sizes))]
    g_small = {}
    for k, off, size in zip(small_names, offsets, sizes):
        g = summed[off:off + size]
        if k == "b_b_qkv":
            n_local = b_b_qkv.shape[1]
            g = lax.dynamic_slice_in_dim(g.reshape(b_b_qkv.shape[0], -1), device * n_local, n_local, axis=1)
        g_small[k] = g.reshape(weights[k].shape)
    local_total = sum(weights[k].size for k in small_names)
    local_padded = -(-local_total // (LANES * LANES)) * (LANES * LANES)
    pack_local = lambda src: jnp.concatenate(
        [src[k].reshape(-1) for k in small_names] + [jnp.ones((local_padded - local_total,), F32)]
    ).reshape(local_padded // LANES, LANES)
    d, mn, vn = adamw(pack_local(weights), pack_local(g_small), pack_local(mom_m), pack_local(mom_v), "adamw_small")
    off = 0
    for k in small_names:
        size, shape = weights[k].size, weights[k].shape
        cut = lambda t: t.reshape(-1)[off:off + size].reshape(shape)
        out_g[k], out_d[k], out_m[k], out_v[k] = g_small[k], cut(d), cut(mn), cut(vn)
        off += size

    return (loss, grad_x, *[out_g[k] for k in names], *[out_d[k] for k in names],
            *[out_m[k] for k in names], *[out_v[k] for k in names])
```

```python
import functools
import math

import jax
import jax.numpy as jnp
from jax import lax
from jax.experimental import pallas as pl
from jax.experimental.pallas import tpu as pltpu

F32, BF16 = jnp.float32, jnp.bfloat16
MESH = pl.DeviceIdType.MESH
N_DEV = 8

CHUNK = 128
HEAD_DIM = 64
N_KV_HEADS = 4
ROPE_DIM = HEAD_DIM // 4
ROPE_THETA = 500000.0
RMS_EPS = 1e-6
LN_EPS = 1e-5
NEG_INF = -1e30

ADAM_LR = 0.001
ADAM_B1 = 0.9
ADAM_B2 = 0.999
ADAM_EPS = 1e-08
ADAM_WD = 0.01
ADAM_STEP = 10

V7X_VMEM_BYTES = 64 * 2 ** 20
VMEM_LIMIT = V7X_VMEM_BYTES - 8 * 2 ** 20
LANES = 128


def _params(*sem):
    return pltpu.CompilerParams(dimension_semantics=sem or None, vmem_limit_bytes=VMEM_LIMIT)


def _tile(n, pref):
    if n <= pref:
        return n
    t = pref - pref % LANES
    while t >= LANES:
        if n % t == 0:
            return t
        t -= LANES
    return n


def _full(shape):
    return pl.BlockSpec(shape, lambda *_: (0,) * len(shape))


def _rows(tm, width):
    return pl.BlockSpec((tm, width), lambda i: (i, 0))


def _rms(x, g):
    r = lax.rsqrt(jnp.mean(x * x, axis=-1, keepdims=True) + RMS_EPS)
    return x * r * g


def _rms_bwd(x, g, dy):
    r = lax.rsqrt(jnp.mean(x * x, axis=-1, keepdims=True) + RMS_EPS)
    xhat = x * r
    dg = jnp.sum(dy * xhat, axis=0, keepdims=True)
    dxhat = dy * g
    dx = r * (dxhat - xhat * jnp.mean(dxhat * xhat, axis=-1, keepdims=True))
    return dx, dg


_INV_SQRT2 = 1.0 / math.sqrt(2.0)
_INV_SQRT2PI = 1.0 / math.sqrt(2.0 * math.pi)


def _gelu(x):
    return 0.5 * x * (1.0 + lax.erf(x * _INV_SQRT2))


def _gelu_and_grad(x):
    cdf = 0.5 * (1.0 + lax.erf(x * _INV_SQRT2))
    return x * cdf, cdf + x * jnp.exp(-0.5 * x * x) * _INV_SQRT2PI


def _sigmoid(x):
    return 0.5 * jnp.tanh(0.5 * x) + 0.5


def _position():
    return lax.axis_index("x"), lax.axis_index("y"), lax.axis_index("c")


def _index(p):
    return 4 * p[0] + 2 * p[1] + p[2]


_HBM = pl.BlockSpec(memory_space=pltpu.HBM)
_SEM = pl.BlockSpec(memory_space=pltpu.SEMAPHORE)
_ORDERED_BY_DATA = pltpu.SideEffectType.DATAFLOW_SIDE_EFFECTING


def _in_hbm(v):
    return pltpu.with_memory_space_constraint(v, pltpu.HBM)


def _peer(k):
    x, y, c = _position()
    return (x ^ ((k >> 2) & 1), y ^ ((k >> 1) & 1), c ^ (k & 1))


def _split_copies(srcs, dsts, send, recv):
    return [pltpu.make_async_remote_copy(
        src_ref=src(k), dst_ref=dst(k), send_sem=send.at[7 * a + k - 1], recv_sem=recv.at[7 * a + k - 1],
        device_id=_peer(k), device_id_type=MESH)
        for a, (src, dst) in enumerate(zip(srcs, dsts)) for k in range(1, N_DEV)]


def _start_call(groups, sent, copies_of, name, per_array=N_DEV - 1):
    flat = [v for grp in groups for v in grp]
    n, ng = len(flat), len(groups)

    def body(*refs):
        bufs, sems = refs[:n], refs[n:n + 2 * ng]
        q = 0
        for g, grp in enumerate(groups):
            for cp in copies_of(bufs[q:q + len(grp)], sems[2 * g], sems[2 * g + 1]):
                cp.start()
            q += len(grp)
        refs[-1][...] = jnp.zeros_like(refs[-1])

    sem_shapes = []
    for count in sent:
        sem_shapes += [pltpu.SemaphoreType.DMA((per_array * count,)) for _ in range(2)]
    res = pl.pallas_call(
        body, name=name,
        out_shape=sem_shapes + [pltpu.HBM(v.shape, v.dtype) for v in flat] + [jax.ShapeDtypeStruct((8, LANES), F32)],
        in_specs=[_HBM] * n,
        out_specs=[_SEM] * (2 * ng) + [_HBM] * n + [pl.BlockSpec(memory_space=pltpu.VMEM)],
        input_output_aliases={i: 2 * ng + i for i in range(n)},
        compiler_params=pltpu.CompilerParams(has_side_effects=_ORDERED_BY_DATA),
    )(*[_in_hbm(v) for v in flat])
    sems = [(res[2 * g], res[2 * g + 1]) for g in range(ng)]
    return sems, list(res[2 * ng:-1]), res[-1]


def _wait_call(bufs, sems, copies_of, after, name):
    n = len(bufs)

    def body(*refs):
        for cp in copies_of(refs[:n], refs[n], refs[n + 1]):
            cp.wait_send()
            cp.wait_recv()

    return list(pl.pallas_call(
        body, name=name,
        out_shape=[pltpu.HBM(v.shape, v.dtype) for v in bufs],
        in_specs=[_HBM] * n + [_SEM, _SEM] + [pl.BlockSpec(memory_space=pl.ANY)] * len(after),
        out_specs=[_HBM] * n,
        input_output_aliases={i: i for i in range(n)},
        compiler_params=pltpu.CompilerParams(has_side_effects=_ORDERED_BY_DATA),
    )(*bufs, sems[0], sems[1], *after))


def _gather_copies(lands, send, recv):
    me = _index(_position())
    mine = [lambda k, ref=ref: ref.at[pl.ds(me, 1)] for ref in lands]
    return _split_copies(mine, mine, send, recv)


def _gather_arrivals(lands, send, recv):
    me = _index(_position())
    mine = [lambda k, ref=ref: ref.at[pl.ds(me, 1)] for ref in lands]
    theirs = [lambda k, ref=ref: ref.at[pl.ds(_index(_peer(k)), 1)] for ref in lands]
    return _split_copies(mine, theirs, send, recv)


def _exchange_copies(bufs, send, recv):
    half = len(bufs) // 2
    srcs = [lambda k, ref=ref: ref.at[pl.ds(_index(_peer(k)), 1)] for ref in bufs[:half]]
    dsts = [lambda k, ref=ref: ref.at[pl.ds(k - 1, 1)] for ref in bufs[half:]]
    return _split_copies(srcs, dsts, send, recv)


def gather_start(groups, name):
    return _start_call(groups, [len(grp) for grp in groups], _gather_copies, name)


def gather_wait(lands, sems, after, name):
    return _wait_call(lands, sems, _gather_arrivals, after, name)


_DIRECT = (1, 2, 4, 6)
_PASSED = (3, 5, 7)


def _remote(src, dst, send, recv, q, k):
    return pltpu.make_async_remote_copy(src_ref=src, dst_ref=dst, send_sem=send.at[q], recv_sem=recv.at[q],
                                        device_id=_peer(k), device_id_type=MESH)


def _slot_of(land, k):
    return land.at[pl.ds(_index(_peer(k)), 1)]


def relay_start(groups, name):
    def copies(lands, send, recv):
        return [_remote(_slot_of(land, 0), _slot_of(land, 0), send, recv, len(_DIRECT) * a + j, k)
                for a, land in enumerate(lands) for j, k in enumerate(_DIRECT)]
    return _start_call(groups, [len(grp) for grp in groups], copies, name, per_array=len(_DIRECT))


def relay_pass(lands, sems, after, name):
    n = len(lands)

    def body(*refs):
        bufs, recv_first = refs[:n], refs[n]
        send, recv = refs[n + 1 + len(after):n + 3 + len(after)]
        for a, land in enumerate(bufs):
            for j, k in enumerate(_PASSED):
                came = _slot_of(land, k ^ 1)
                _remote(came, came, recv_first, recv_first, len(_DIRECT) * a + _DIRECT.index(k ^ 1), k ^ 1).wait_recv()
                _remote(came, came, send, recv, len(_PASSED) * a + j, 1).start()

    res = pl.pallas_call(
        body, name=name,
        out_shape=[pltpu.SemaphoreType.DMA((len(_PASSED) * n,))] * 2 + [pltpu.HBM(v.shape, v.dtype) for v in lands],
        in_specs=[_HBM] * n + [_SEM] + [pl.BlockSpec(memory_space=pl.ANY)] * len(after),
        out_specs=[_SEM, _SEM] + [_HBM] * n,
        input_output_aliases={i: 2 + i for i in range(n)},
        compiler_params=pltpu.CompilerParams(has_side_effects=_ORDERED_BY_DATA),
    )(*lands, sems[1], *after)
    return (res[0], res[1]), list(res[2:])


def relay_wait(lands, first, second, after, name):
    n = len(lands)

    def body(*refs):
        bufs = refs[:n]
        send_first, recv_first, send, recv = refs[n:n + 4]
        for a, land in enumerate(bufs):
            mine = _slot_of(land, 0)
            _remote(mine, _slot_of(land, 1), send_first, recv_first, len(_DIRECT) * a, 1).wait_recv()
            for j, k in enumerate(_DIRECT):
                _remote(mine, mine, send_first, recv_first, len(_DIRECT) * a + j, k).wait_send()
            for j, k in enumerate(_PASSED):
                cp = _remote(_slot_of(land, k ^ 1), _slot_of(land, k), send, recv, len(_PASSED) * a + j, 1)
                cp.wait_send()
                cp.wait_recv()

    return list(pl.pallas_call(
        body, name=name,
        out_shape=[pltpu.HBM(v.shape, v.dtype) for v in lands],
        in_specs=[_HBM] * n + [_SEM] * 4 + [pl.BlockSpec(memory_space=pl.ANY)] * len(after),
        out_specs=[_HBM] * n,
        input_output_aliases={i: i for i in range(n)},
        compiler_params=pltpu.CompilerParams(has_side_effects=_ORDERED_BY_DATA),
    )(*lands, first[0], first[1], second[0], second[1], *after))


def exchange_start(partials, name):
    lands = [lax.empty((N_DEV - 1,) + p.shape[1:], p.dtype) for p in partials]
    sems, bufs, token = _start_call([list(partials) + lands], [len(partials)], _exchange_copies, name)
    return sems[0], bufs, token


def exchange_wait(bufs, sems, after, name):
    bufs = _wait_call(bufs, sems, _exchange_copies, after, name)
    return bufs[:len(bufs) // 2], bufs[len(bufs) // 2:]


def prep_weight(w, layer, transpose, dev, name, dtype=BF16, after=()):
    L, K, n = w.shape
    block = (1, n, K) if transpose else (1, K, n)

    def body(d_ref, w_ref, *rest):
        o_ref = rest[-1]
        v = w_ref[0]
        o_ref[0] = (v.T if transpose else v).astype(dtype)

    return pl.pallas_call(
        body, name=name,
        grid_spec=pltpu.PrefetchScalarGridSpec(
            num_scalar_prefetch=1, grid=(1,),
            in_specs=[pl.BlockSpec((1, K, n), lambda i, d: (layer, 0, 0))]
            + [pl.BlockSpec(memory_space=pl.ANY)] * len(after),
            out_specs=pl.BlockSpec(block, lambda i, d: (d[0], 0, 0))),
        out_shape=jax.ShapeDtypeStruct((N_DEV,) + block[1:], dtype),
        compiler_params=_params("arbitrary"),
    )(dev, w, *after)


def _adamw_math(w, g, m, v):
    mn = ADAM_B1 * m + (1.0 - ADAM_B1) * g
    vn = ADAM_B2 * v + (1.0 - ADAM_B2) * (g * g)
    m_hat = mn * (1.0 / (1.0 - ADAM_B1 ** ADAM_STEP))
    v_hat = vn * (1.0 / (1.0 - ADAM_B2 ** ADAM_STEP))
    return -ADAM_LR * (m_hat / (jnp.sqrt(v_hat) + ADAM_EPS) + ADAM_WD * w), mn, vn


def sum_adamw(partial, landed, w, m, v, prev, layer, transpose, dev, name, after=()):
    _, r, c = partial.shape
    if transpose:
        tc = _tile(c, 256)
        grid = (c // tc,)
        part_spec = pl.BlockSpec((1, r, tc), lambda i, d: (d[0], 0, i))
        land_spec = pl.BlockSpec((N_DEV - 1, r, tc), lambda i, d: (0, 0, i))
        w_spec = pl.BlockSpec((1, tc, r), lambda i, d: (layer, i, 0))
    else:
        tr = r // 2 if r % 32 == 0 else r
        grid = (r // tr,)
        part_spec = pl.BlockSpec((1, tr, c), lambda i, d: (d[0], i, 0))
        land_spec = pl.BlockSpec((N_DEV - 1, tr, c), lambda i, d: (0, i, 0))
        w_spec = pl.BlockSpec((1, tr, c), lambda i, d: (layer, i, 0))

    def body(d_ref, p_ref, l_ref, w_ref, m_ref, v_ref, *rest):
        g_ref, dl_ref, mo_ref, vo_ref = rest[-4:]
        acc = p_ref[0].astype(F32)
        for k in range(N_DEV - 1):
            acc = acc + l_ref[k].astype(F32)
        g = acc.T if transpose else acc
        delta, mn, vn = _adamw_math(w_ref[0], g, m_ref[0], v_ref[0])
        g_ref[0], dl_ref[0], mo_ref[0], vo_ref[0] = g, delta, mn, vn

    return pl.pallas_call(
        body, name=name,
        grid_spec=pltpu.PrefetchScalarGridSpec(
            num_scalar_prefetch=1, grid=grid,
            in_specs=[part_spec, land_spec, w_spec, w_spec, w_spec]
            + [pl.BlockSpec(memory_space=pl.ANY)] * (4 + len(after)),
            out_specs=[w_spec] * 4),
        out_shape=[jax.ShapeDtypeStruct(w.shape, F32)] * 4,
        input_output_aliases={6 + q: q for q in range(4)},
        compiler_params=_params("parallel"),
    )(dev, partial, landed, w, m, v, *prev, *after)


def sum_parts(parts, transpose, name):
    L, P, r, c = parts.shape
    tr = _tile(r, 256) if (not transpose and r % 8 == 0) else r
    if r % tr:
        tr = r
    out_block = (1, c, tr) if transpose else (1, tr, c)
    out_map = (lambda l, i: (l, 0, i)) if transpose else (lambda l, i: (l, i, 0))

    def body(p_ref, o_ref):
        acc = p_ref[0, 0].astype(F32)
        for s in range(1, P):
            acc = acc + p_ref[0, s].astype(F32)
        o_ref[0] = acc.T if transpose else acc

    return pl.pallas_call(
        body, name=name, grid=(L, r // tr),
        in_specs=[pl.BlockSpec((1, P, tr, c), lambda l, i: (l, 0, i, 0))],
        out_specs=pl.BlockSpec(out_block, out_map),
        out_shape=jax.ShapeDtypeStruct((L, c, r) if transpose else (L, r, c), F32),
        compiler_params=_params("parallel", "parallel"),
    )(parts)


def adamw(w, g, m, v, name):
    R, C = w.shape
    tr = _tile(R, 512)

    def body(w_ref, g_ref, m_ref, v_ref, d_ref, mo_ref, vo_ref):
        d_ref[...], mo_ref[...], vo_ref[...] = _adamw_math(w_ref[...], g_ref[...], m_ref[...], v_ref[...])

    spec = pl.BlockSpec((tr, C), lambda i: (i, 0))
    return pl.pallas_call(
        body, name=name, grid=(R // tr,),
        in_specs=[spec] * 4, out_specs=[spec] * 3,
        out_shape=[jax.ShapeDtypeStruct((R, C), F32)] * 3,
        compiler_params=_params("parallel"),
    )(w, g, m, v)


def sum_adamw_small(gathered, w, m, v, name):
    rows = w.shape[0]
    tr = _tile(rows, 512)

    def body(p_ref, w_ref, m_ref, v_ref, g_ref, d_ref, mo_ref, vo_ref):
        g = p_ref[0]
        for s in range(1, N_DEV):
            g = g + p_ref[s]
        g_ref[...] = g
        d_ref[...], mo_ref[...], vo_ref[...] = _adamw_math(w_ref[...], g, m_ref[...], v_ref[...])

    spec = pl.BlockSpec((tr, LANES), lambda i: (i, 0))
    return pl.pallas_call(
        body, name=name, grid=(rows // tr,),
        in_specs=[pl.BlockSpec((N_DEV, tr, LANES), lambda i: (0, i, 0)), spec, spec, spec], out_specs=[spec] * 4,
        out_shape=[jax.ShapeDtypeStruct((rows, LANES), F32)] * 4, compiler_params=_params("parallel"))(gathered, w, m, v)


def first_norm(h, g, name):
    T, D = h.shape
    tm = _tile(T, 512)

    def body(h_ref, g_ref, hn_ref):
        hn_ref[...] = _rms(h_ref[...], g_ref[...]).astype(BF16)

    row = _rows(tm, D)
    return pl.pallas_call(
        body, name=name, grid=(T // tm,), in_specs=[row, _full((1, D))], out_specs=row,
        out_shape=jax.ShapeDtypeStruct((T, D), BF16), compiler_params=_params("parallel"))(h, g)


_NT = (((1,), (1,)), ((), ()))
_TN = (((0,), (0,)), ((), ()))


def rowmm(a, w, transposed_w, bias, out_dtype, name, after=None):
    M, K = a.shape
    N = w.shape[0] if transposed_w else w.shape[1]
    tm = _tile(M, 256)
    tn = _tile(N, 512)

    def body(*refs):
        a_ref, w_ref = refs[:2]
        o_ref = refs[-1]
        av = a_ref[...]
        for j in range(N // tn):
            cols = pl.ds(j * tn, tn)
            if transposed_w:
                acc = lax.dot_general(av, w_ref[cols, :], _NT, preferred_element_type=F32)
            else:
                acc = jnp.dot(av, w_ref[:, cols], preferred_element_type=F32)
            if bias is not None:
                acc = acc + refs[2][:, cols]
            o_ref[:, cols] = acc.astype(out_dtype)

    ins, in_specs = [a, w], [_rows(tm, K), _full(w.shape)]
    if bias is not None:
        ins.append(bias)
        in_specs.append(_full((1, N)))
    if after is not None:
        ins.append(after)
        in_specs.append(pl.BlockSpec(memory_space=pl.ANY))
    return pl.pallas_call(
        body, name=name, grid=(M // tm,), in_specs=in_specs, out_specs=_rows(tm, N),
        out_shape=jax.ShapeDtypeStruct((M, N), out_dtype), compiler_params=_params("parallel"))(*ins)


def _rows_times_weight(a_ref, w_ref, acc_ref):
    N = w_ref.shape[1]
    tn = _tile(N, 512)
    av = a_ref[...]
    for j in range(N // tn):
        cols = pl.ds(j * tn, tn)
        acc_ref[:, cols] = jnp.dot(av, w_ref[:, cols], preferred_element_type=F32)


def mm_add_norm(a, w, h, g_post, g_pre, name):
    T, K = a.shape
    D = w.shape[1]
    tm = _tile(T, 256)

    def body(a_ref, w_ref, h_ref, gp_ref, g_ref, y_ref, ho_ref, hn_ref, acc_ref):
        _rows_times_weight(a_ref, w_ref, acc_ref)
        y = acc_ref[...]
        y_ref[...] = y.astype(BF16)
        hv = h_ref[...] + _rms(y, gp_ref[...])
        ho_ref[...] = hv
        hn_ref[...] = _rms(hv, g_ref[...]).astype(BF16)

    row, vec = _rows(tm, D), _full((1, D))
    return pl.pallas_call(
        body, name=name, grid=(T // tm,), in_specs=[_rows(tm, K), _full(w.shape), row, vec, vec],
        out_specs=[row, row, row],
        out_shape=[jax.ShapeDtypeStruct((T, D), BF16), jax.ShapeDtypeStruct((T, D), F32),
                   jax.ShapeDtypeStruct((T, D), BF16)],
        scratch_shapes=[pltpu.VMEM((tm, D), F32)], compiler_params=_params("parallel"))(a, w, h, g_post, g_pre)


def mm_add_norm_loss(a, w, h, g_post, target, name):
    T, K = a.shape
    D = w.shape[1]
    tm = _tile(T, 256)

    def body(a_ref, w_ref, h_ref, gp_ref, t_ref, dh_ref, dy_ref, loss_ref, dgp_ref, acc_ref):
        @pl.when(pl.program_id(0) == 0)
        def _():
            loss_ref[...] = jnp.zeros_like(loss_ref)
            dgp_ref[...] = jnp.zeros_like(dgp_ref)
        _rows_times_weight(a_ref, w_ref, acc_ref)
        y = acc_ref[...]
        err = h_ref[...] + _rms(y, gp_ref[...]) - t_ref[...]
        dh = err * (1.0 / D)
        dh_ref[...] = dh
        loss_ref[...] += jnp.sum(err * err)
        dy, dgp = _rms_bwd(y, gp_ref[...], dh)
        dy_ref[...] = dy.astype(BF16)
        dgp_ref[...] += dgp

    row, vec = _rows(tm, D), _full((1, D))
    return pl.pallas_call(
        body, name=name, grid=(T // tm,), in_specs=[_rows(tm, K), _full(w.shape), row, vec, row],
        out_specs=[row, row, _full((8, LANES)), vec],
        out_shape=[jax.ShapeDtypeStruct((T, D), F32), jax.ShapeDtypeStruct((T, D), BF16),
                   jax.ShapeDtypeStruct((8, LANES), F32), jax.ShapeDtypeStruct((1, D), F32)],
        scratch_shapes=[pltpu.VMEM((tm, D), F32)], compiler_params=_params("arbitrary"))(a, w, h, g_post, target)


def mm_norm_bwd(a, w, dh_in, x, g_pre, post, name, after):
    T, K = a.shape
    D = w.shape[1]
    tm = _tile(T, 256)
    row, vec = _rows(tm, D), _full((1, D))
    ins, in_specs = [a, w, dh_in, x, g_pre], [_rows(tm, K), _full(w.shape), row, row, vec]
    outs = [jax.ShapeDtypeStruct((T, D), F32), jax.ShapeDtypeStruct((1, D), F32)]
    out_specs, names = [row, vec], ["dh", "dg_pre"]
    if post is not None:
        ins += list(post)
        in_specs += [row, vec]
        outs += [jax.ShapeDtypeStruct((T, D), BF16), jax.ShapeDtypeStruct((1, D), F32)]
        out_specs += [row, vec]
        names += ["dy", "dg_post"]
    n_in = len(ins) + 1

    def body(*refs):
        a_ref, w_ref, dh_ref_in, x_ref, g_ref = refs[:5]
        out_refs, acc_ref = refs[n_in:-1], refs[-1]
        first = pl.program_id(0) == 0
        _rows_times_weight(a_ref, w_ref, acc_ref)
        dx, dg = _rms_bwd(x_ref[...], g_ref[...], acc_ref[...])
        dh = dh_ref_in[...] + dx
        out_refs[0][...] = dh

        @pl.when(first)
        def _():
            for r in out_refs[1::2]:
                r[...] = jnp.zeros_like(r)
        out_refs[1][...] += dg
        if post is not None:
            y_ref, gp_ref = refs[5:7]
            dy, dgp = _rms_bwd(y_ref[...].astype(F32), gp_ref[...], dh)
            out_refs[2][...] = dy.astype(BF16)
            out_refs[3][...] += dgp

    res = pl.pallas_call(
        body, name=name, grid=(T // tm,), in_specs=in_specs + [pl.BlockSpec(memory_space=pl.ANY)],
        out_specs=out_specs, out_shape=outs, scratch_shapes=[pltpu.VMEM((tm, D), F32)],
        compiler_params=_params("arbitrary"))(*ins, after)
    return dict(zip(names, res))


def grad_mm(a, b, name):
    T, N = a.shape
    K = b.shape[1]
    tn = _tile(N, 1408)
    tt = _tile(T, 1024)

    def body(a_ref, b_ref, o_ref, acc_ref):
        t = pl.program_id(1)

        @pl.when(t == 0)
        def _():
            acc_ref[...] = jnp.zeros_like(acc_ref)
        acc_ref[...] += lax.dot_general(a_ref[...], b_ref[...], _TN, preferred_element_type=F32)

        @pl.when(t == pl.num_programs(1) - 1)
        def _():
            o_ref[...] = acc_ref[...].astype(BF16)

    return pl.pallas_call(
        body, name=name, grid=(N // tn, T // tt),
        in_specs=[pl.BlockSpec((tt, tn), lambda j, t: (t, j)), pl.BlockSpec((tt, K), lambda j, t: (t, 0))],
        out_specs=pl.BlockSpec((tn, K), lambda j, t: (j, 0)),
        out_shape=jax.ShapeDtypeStruct((N, K), BF16),
        scratch_shapes=[pltpu.VMEM((tn, K), F32)],
        compiler_params=_params("parallel", "arbitrary"))(a, b)


def ffn_up(fn, w_gu_t, name):
    T, D = fn.shape
    F = w_gu_t.shape[0] // 2
    tm = _tile(T, 256)
    tn = _tile(F, 512)

    def body(a_ref, w_ref, act_ref, part_ref):
        av = a_ref[...]
        for j in range(F // tn):
            g = lax.dot_general(av, w_ref[pl.ds(j * tn, tn), :], _NT, preferred_element_type=F32)
            up = lax.dot_general(av, w_ref[pl.ds(F + j * tn, tn), :], _NT, preferred_element_type=F32)
            sg = _sigmoid(g)
            silu = g * sg
            act_ref[:, pl.ds(j * tn, tn)] = (silu * up).astype(BF16)
            part_ref[:, pl.ds(j * tn, tn)] = (up * (sg + silu * (1.0 - sg))).astype(BF16)
            part_ref[:, pl.ds(F + j * tn, tn)] = silu.astype(BF16)

    return pl.pallas_call(
        body, name=name, grid=(T // tm,), in_specs=[_rows(tm, D), _full(w_gu_t.shape)],
        out_specs=[_rows(tm, F), _rows(tm, 2 * F)],
        out_shape=[jax.ShapeDtypeStruct((T, F), BF16), jax.ShapeDtypeStruct((T, 2 * F), BF16)],
        compiler_params=_params("parallel"))(fn, w_gu_t)


def ffn_dact(df, w_down, partials, name):
    T, D = df.shape
    F = w_down.shape[0]
    tm = _tile(T, 256)
    tn = _tile(F, 512)

    def body(d_ref, w_ref, part_ref, o_ref):
        dv = d_ref[...]
        for j in range(F // tn):
            dact = lax.dot_general(dv, w_ref[pl.ds(j * tn, tn), :], _NT, preferred_element_type=F32)
            for cols in (pl.ds(j * tn, tn), pl.ds(F + j * tn, tn)):
                o_ref[:, cols] = (dact * part_ref[:, cols].astype(F32)).astype(BF16)

    return pl.pallas_call(
        body, name=name, grid=(T // tm,), in_specs=[_rows(tm, D), _full(w_down.shape), _rows(tm, 2 * F)],
        out_specs=_rows(tm, 2 * F), out_shape=jax.ShapeDtypeStruct((T, 2 * F), BF16),
        compiler_params=_params("parallel"))(df, w_down, partials)


def _layernorm_stats(v):
    mu = jnp.mean(v, axis=-1, keepdims=True)
    cen = v - mu
    rstd = lax.rsqrt(jnp.mean(cen * cen, axis=-1, keepdims=True) + LN_EPS)
    return cen * rstd, rstd


def gmlp_fwd(a, ln_g, ln_b, w_mask, b_s_t, name):
    T, W2 = a.shape
    W = W2 // 2
    G = w_mask.shape[0]
    C = W // G

    def body(a_ref, g_ref, b_ref, w_ref, bs_ref, o_ref):
        xhat, _ = _layernorm_stats(_gelu(a_ref[:, W:].astype(F32)))
        vln = (xhat * g_ref[...] + b_ref[...]).astype(BF16)
        for g in range(G):
            cols = pl.ds(g * C, C)
            sv = jnp.dot(w_ref[g], vln[:, g * C:(g + 1) * C], preferred_element_type=F32) + bs_ref[:, g:g + 1]
            o_ref[:, cols] = (_gelu(a_ref[:, cols].astype(F32)) * sv).astype(BF16)

    return pl.pallas_call(
        body, name=name, grid=(T // CHUNK,),
        in_specs=[_rows(CHUNK, W2), _full((1, W)), _full((1, W)), _full(w_mask.shape), _full(b_s_t.shape)],
        out_specs=_rows(CHUNK, W), out_shape=jax.ShapeDtypeStruct((T, W), BF16),
        compiler_params=_params("parallel"))(a, ln_g, ln_b, w_mask, b_s_t)


def gmlp_bwd(a, dgated, ln_g, ln_b, w_mask, w_mask_t, b_s_t, group_onehot, name):
    T, W2 = a.shape
    W = W2 // 2
    G = w_mask.shape[0]
    C = W // G

    def body(a_ref, dg_ref, g_ref, b_ref, w_ref, wt_ref, bs_ref, e_ref,
             da_ref, dws_ref, dbs_ref, dlg_ref, dlb_ref, dbin_ref, dvln_ref):
        @pl.when(pl.program_id(0) == 0)
        def _():
            for r in (dws_ref, dbs_ref, dlg_ref, dlb_ref, dbin_ref):
                r[...] = jnp.zeros_like(r)
        row = lax.broadcasted_iota(jnp.int32, (CHUNK, CHUNK), 0)
        col = lax.broadcasted_iota(jnp.int32, (CHUNK, CHUNK), 1)
        causal = col <= row
        gelu_v, gelu_grad_v = _gelu_and_grad(a_ref[:, W:].astype(F32))
        xhat, rstd = _layernorm_stats(gelu_v)
        vln = (xhat * g_ref[...] + b_ref[...]).astype(BF16)
        for g in range(G):
            cols = pl.ds(g * C, C)
            vg = vln[:, g * C:(g + 1) * C]
            gelu_u, gelu_grad_u = _gelu_and_grad(a_ref[:, cols].astype(F32))
            dgated = dg_ref[:, cols].astype(F32)
            dsv = (dgated * gelu_u).astype(BF16)
            dbs_ref[...] += jnp.dot(dsv, e_ref[cols, :], preferred_element_type=F32)
            sv = jnp.dot(w_ref[g], vg, preferred_element_type=F32) + bs_ref[:, g:g + 1]
            dau = dgated * sv * gelu_grad_u
            da_ref[:, cols] = dau.astype(BF16)
            dbin_ref[:, cols] += jnp.sum(dau, axis=0, keepdims=True)
            dws_ref[g] += jnp.where(causal, lax.dot_general(dsv, vg, _NT, preferred_element_type=F32), 0.0)
            dvln_ref[:, cols] = jnp.dot(wt_ref[g], dsv, preferred_element_type=F32)
        dvln = dvln_ref[...]
        dlg_ref[...] += jnp.sum(dvln * xhat, axis=0, keepdims=True)
        dlb_ref[...] += jnp.sum(dvln, axis=0, keepdims=True)
        dxhat = dvln * g_ref[...]
        dv = rstd * (dxhat - jnp.mean(dxhat, axis=-1, keepdims=True)
                     - xhat * jnp.mean(dxhat * xhat, axis=-1, keepdims=True))
        dav = dv * gelu_grad_v
        da_ref[:, W:] = dav.astype(BF16)
        dbin_ref[:, W:] += jnp.sum(dav, axis=0, keepdims=True)

    return pl.pallas_call(
        body, name=name, grid=(T // CHUNK,),
        in_specs=[_rows(CHUNK, W2), _rows(CHUNK, W), _full((1, W)), _full((1, W)), _full(w_mask.shape),
                  _full(w_mask_t.shape), _full(b_s_t.shape), _full(group_onehot.shape)],
        out_specs=[_rows(CHUNK, W2), _full((G, CHUNK, CHUNK)), _full((CHUNK, LANES)), _full((1, W)),
                   _full((1, W)), _full((1, W2))],
        out_shape=[jax.ShapeDtypeStruct((T, W2), BF16), jax.ShapeDtypeStruct((G, CHUNK, CHUNK), F32),
                   jax.ShapeDtypeStruct((CHUNK, LANES), F32), jax.ShapeDtypeStruct((1, W), F32),
                   jax.ShapeDtypeStruct((1, W), F32), jax.ShapeDtypeStruct((1, W2), F32)],
        scratch_shapes=[pltpu.VMEM((CHUNK, W), F32)],
        compiler_params=_params("arbitrary"))(a, dgated, ln_g, ln_b, w_mask, w_mask_t, b_s_t, group_onehot)


def rope_tables(pos, inv_freq_row, name):
    T = pos.shape[0]
    tm = _tile(T, 512)

    def body(p_ref, f_ref, c_ref, s1_ref, s2_ref):
        ang = p_ref[...].astype(F32) * f_ref[...]
        lane = lax.broadcasted_iota(jnp.int32, (tm, LANES), 1) % HEAD_DIM
        sin = jnp.sin(ang)
        c_ref[...] = jnp.cos(ang)
        s1_ref[...] = jnp.where(lane < _HALF, -sin, 0.0)
        s2_ref[...] = jnp.where((lane >= _HALF) & (lane < ROPE_DIM), sin, 0.0)

    tab = _rows(tm, LANES)
    return pl.pallas_call(
        body, name=name, grid=(T // tm,), in_specs=[_rows(tm, 1), _full((1, LANES))], out_specs=[tab] * 3,
        out_shape=[jax.ShapeDtypeStruct((T, LANES), F32)] * 3, compiler_params=_params("parallel"))(pos, inv_freq_row)


_HALF = ROPE_DIM // 2


def _slabs(x):
    return [x[:, b * LANES:(b + 1) * LANES] for b in range(x.shape[1] // LANES)]


def _rotate(x, c, s1, s2):
    return [xs * c + pltpu.roll(xs, LANES - _HALF, 1) * s1 + pltpu.roll(xs, _HALF, 1) * s2 for xs in _slabs(x)]


def _rotate_transposed(dy, c, s1, s2):
    return [ds * c + pltpu.roll(ds * s1, _HALF, 1) + pltpu.roll(ds * s2, LANES - _HALF, 1) for ds in _slabs(dy)]


def rope_fwd(qkv, tabs, q_width, kv_width, name):
    T, QKV = qkv.shape
    tm = _tile(T, 512)
    QK = q_width + kv_width
    scale = HEAD_DIM ** -0.5

    def body(x_ref, c_ref, s1_ref, s2_ref, o_ref):
        x = x_ref[:, :QK].astype(F32)
        slabs = _rotate(x, c_ref[...], s1_ref[...], s2_ref[...])
        for b, y in enumerate(slabs):
            if b * LANES < q_width:
                y = y * scale
            o_ref[:, b * LANES:(b + 1) * LANES] = y.astype(BF16)

    tab = _rows(tm, LANES)
    return pl.pallas_call(
        body, name=name, grid=(T // tm,), in_specs=[_rows(tm, QKV), tab, tab, tab], out_specs=_rows(tm, QK),
        out_shape=jax.ShapeDtypeStruct((T, QK), BF16), compiler_params=_params("parallel"))(qkv, *tabs)


def rope_bwd(dq, dk, dv, tabs, name):
    T, Q = dq.shape
    KV = dk.shape[1]
    tm = _tile(T, 512)
    scale = HEAD_DIM ** -0.5

    def body(dq_ref, dk_ref, dv_ref, c_ref, s1_ref, s2_ref, o_ref, b_ref):
        @pl.when(pl.program_id(0) == 0)
        def _():
            b_ref[...] = jnp.zeros_like(b_ref)
        tabs_v = (c_ref[...], s1_ref[...], s2_ref[...])
        pieces = [s * scale for s in _rotate_transposed(dq_ref[...], *tabs_v)]
        pieces += _rotate_transposed(dk_ref[...], *tabs_v)
        pieces += _slabs(dv_ref[...])
        for b, y in enumerate(pieces):
            cols = pl.ds(b * LANES, LANES)
            o_ref[:, cols] = y.astype(BF16)
            b_ref[:, cols] += jnp.sum(y, axis=0, keepdims=True)

    tab = _rows(tm, LANES)
    return pl.pallas_call(
        body, name=name, grid=(T // tm,), in_specs=[_rows(tm, Q), _rows(tm, KV), _rows(tm, KV), tab, tab, tab],
        out_specs=[_rows(tm, Q + 2 * KV), _full((1, Q + 2 * KV))],
        out_shape=[jax.ShapeDtypeStruct((T, Q + 2 * KV), BF16), jax.ShapeDtypeStruct((1, Q + 2 * KV), F32)],
        compiler_params=_params("arbitrary"))(dq, dk, dv, *tabs)


def _band_mask(n, heads=1):
    qi = lax.broadcasted_iota(jnp.int32, (heads * CHUNK, 2 * CHUNK), 0) % CHUNK
    sj = lax.broadcasted_iota(jnp.int32, (heads * CHUNK, 2 * CHUNK), 1)
    return (sj > qi) & (sj <= qi + CHUNK) & ((n > 0) | (sj >= CHUNK))


def _kv_head(kc_ref, kp_ref, vc_ref, vp_ref, kh):
    lanes = slice(kh * HEAD_DIM, (kh + 1) * HEAD_DIM)
    return (jnp.concatenate([kp_ref[:, lanes], kc_ref[:, lanes]], axis=0),
            jnp.concatenate([vp_ref[:, lanes], vc_ref[:, lanes]], axis=0))


def _head_probs(q, kk, valid, sink):
    s = jnp.where(valid, lax.dot_general(q, kk, _NT, preferred_element_type=F32), NEG_INF)
    m = jnp.maximum(jnp.max(s, axis=-1, keepdims=True), sink)
    p = jnp.exp(s - m)
    e_sink = jnp.exp(sink - m)
    return p, 1.0 / (jnp.sum(p, axis=-1, keepdims=True) + e_sink), e_sink


def _attn_specs(q_width, kv_width, order):
    qb = q_width // kv_width
    prev = lambda i: jnp.maximum(order(i) - 1, 0)
    return [pl.BlockSpec((CHUNK, q_width), lambda i: (order(i), 0)),
            pl.BlockSpec((CHUNK, kv_width), lambda i: (order(i), qb)),
            pl.BlockSpec((CHUNK, kv_width), lambda i: (prev(i), qb)),
            pl.BlockSpec((CHUNK, kv_width), lambda i: (order(i), qb + 1)),
            pl.BlockSpec((CHUNK, kv_width), lambda i: (prev(i), qb + 1))]


def attn_fwd(qk, qkv, sinks, q_width, kv_width, name):
    T = qk.shape[0]
    group = q_width // kv_width

    def body(q_ref, kc_ref, kp_ref, vc_ref, vp_ref, sink_ref, o_ref):
        valid = _band_mask(pl.program_id(0))
        for kh in range(kv_width // HEAD_DIM):
            kk, vv = _kv_head(kc_ref, kp_ref, vc_ref, vp_ref, kh)
            for g in range(group):
                h = kh * group + g
                lanes = slice(h * HEAD_DIM, (h + 1) * HEAD_DIM)
                p, inv, _ = _head_probs(q_ref[:, lanes], kk, valid, sink_ref[h])
                o_ref[:, lanes] = (jnp.dot(p.astype(BF16), vv, preferred_element_type=F32) * inv).astype(BF16)

    specs = _attn_specs(q_width, kv_width, lambda i: i)
    return pl.pallas_call(
        body, name=name, grid=(T // CHUNK,),
        in_specs=specs + [pl.BlockSpec(memory_space=pltpu.SMEM)],
        out_specs=_rows(CHUNK, q_width), out_shape=jax.ShapeDtypeStruct((T, q_width), BF16),
        compiler_params=_params("parallel"))(qk, qk, qk, qkv, qkv, sinks)


def attn_bwd(qk, qkv, do, sinks, q_width, kv_width, name):
    T = qk.shape[0]
    NB = T // CHUNK
    group = q_width // kv_width

    def body(q_ref, kc_ref, kp_ref, vc_ref, vp_ref, do_ref, sink_ref, dq_ref, dk_ref, dv_ref, ds_ref, ck_ref, cv_ref):
        i = pl.program_id(0)
        n = NB - 1 - i

        @pl.when(i == 0)
        def _():
            ck_ref[...] = jnp.zeros_like(ck_ref)
            cv_ref[...] = jnp.zeros_like(cv_ref)
            ds_ref[...] = jnp.zeros_like(ds_ref)
        lane = lax.broadcasted_iota(jnp.int32, (1, LANES), 1)
        dsink_row = jnp.zeros((1, LANES), F32)
        valid = _band_mask(n, group)
        head = lax.broadcasted_iota(jnp.int32, (group * CHUNK, 1), 0) // CHUNK
        for kh in range(kv_width // HEAD_DIM):
            kk, vv = _kv_head(kc_ref, kp_ref, vc_ref, vp_ref, kh)
            heads = [slice((kh * group + g) * HEAD_DIM, (kh * group + g + 1) * HEAD_DIM) for g in range(group)]
            q = jnp.concatenate([q_ref[:, hs] for hs in heads], axis=0)
            do = jnp.concatenate([do_ref[:, hs] for hs in heads], axis=0)
            sink = jnp.zeros((group * CHUNK, 1), F32)
            for g in range(group):
                sink = jnp.where(head == g, sink_ref[kh * group + g], sink)
            p, inv, e_sink = _head_probs(q, kk, valid, sink)
            p = p * inv
            dp = lax.dot_general(do, vv, _NT, preferred_element_type=F32)
            delta = jnp.sum(p * dp, axis=-1, keepdims=True)
            ds = (p * (dp - delta)).astype(BF16)
            dsink = -e_sink * inv * delta
            dq = jnp.dot(ds, kk, preferred_element_type=F32)
            for g, hs in enumerate(heads):
                rows = slice(g * CHUNK, (g + 1) * CHUNK)
                dsink_row = dsink_row + jnp.where(lane == kh * group + g, jnp.sum(dsink[rows]), 0.0)
                dq_ref[:, hs] = dq[rows]
            dkk = lax.dot_general(ds, q, _TN, preferred_element_type=F32)
            dvv = lax.dot_general(p.astype(BF16), do, _TN, preferred_element_type=F32)
            lanes = slice(kh * HEAD_DIM, (kh + 1) * HEAD_DIM)
            dk_ref[:, lanes] = dkk[CHUNK:] + ck_ref[:, lanes]
            dv_ref[:, lanes] = dvv[CHUNK:] + cv_ref[:, lanes]
            ck_ref[:, lanes] = dkk[:CHUNK]
            cv_ref[:, lanes] = dvv[:CHUNK]
        ds_ref[0:1, :] += dsink_row

    order = lambda i: NB - 1 - i
    specs = _attn_specs(q_width, kv_width, order)
    kv_out = pl.BlockSpec((CHUNK, kv_width), lambda i: (order(i), 0))
    q_rows = pl.BlockSpec((CHUNK, q_width), lambda i: (order(i), 0))
    return pl.pallas_call(
        body, name=name, grid=(NB,),
        in_specs=specs + [q_rows, pl.BlockSpec(memory_space=pltpu.SMEM)],
        out_specs=[q_rows, kv_out, kv_out, _full((8, LANES))],
        out_shape=[jax.ShapeDtypeStruct((T, q_width), F32), jax.ShapeDtypeStruct((T, kv_width), F32),
                   jax.ShapeDtypeStruct((T, kv_width), F32), jax.ShapeDtypeStruct((8, LANES), F32)],
        scratch_shapes=[pltpu.VMEM((CHUNK, kv_width), F32), pltpu.VMEM((CHUNK, kv_width), F32)],
        compiler_params=_params("arbitrary"))(qk, qk, qk, qkv, qkv, do, sinks)


def _blocked(w):
    return w.reshape(N_DEV, w.shape[0] // N_DEV, w.shape[1])


def kernel(x, positions, pre_mix_g, post_mix_g, pre_ffn_g, post_ffn_g, a_w_in, a_b_in, a_ln_g, a_ln_b, a_w_s, a_b_s, a_w_out, b_w_qkv, b_b_qkv, b_sinks, b_w_o, ffn_w_gu, ffn_w_down, loss_target, m_pre_mix_g, m_post_mix_g, m_pre_ffn_g, m_post_ffn_g, m_a_w_in, m_a_b_in, m_a_ln_g, m_a_ln_b, m_a_w_s, m_a_b_s, m_a_w_out, m_b_w_qkv, m_b_b_qkv, m_b_sinks, m_b_w_o, m_ffn_w_gu, m_ffn_w_down, v_pre_mix_g, v_post_mix_g, v_pre_ffn_g, v_post_ffn_g, v_a_w_in, v_a_b_in, v_a_ln_g, v_a_ln_b, v_a_w_s, v_a_b_s, v_a_w_out, v_b_w_qkv, v_b_b_qkv, v_b_sinks, v_b_w_o, v_ffn_w_gu, v_ffn_w_down):
    weights = dict(pre_mix_g=pre_mix_g, post_mix_g=post_mix_g, pre_ffn_g=pre_ffn_g, post_ffn_g=post_ffn_g,
                   a_w_in=a_w_in, a_b_in=a_b_in, a_ln_g=a_ln_g, a_ln_b=a_ln_b, a_w_s=a_w_s, a_b_s=a_b_s,
                   a_w_out=a_w_out, b_w_qkv=b_w_qkv, b_b_qkv=b_b_qkv, b_sinks=b_sinks, b_w_o=b_w_o,
                   ffn_w_gu=ffn_w_gu, ffn_w_down=ffn_w_down)
    mom_m = dict(pre_mix_g=m_pre_mix_g, post_mix_g=m_post_mix_g, pre_ffn_g=m_pre_ffn_g, post_ffn_g=m_post_ffn_g,
                 a_w_in=m_a_w_in, a_b_in=m_a_b_in, a_ln_g=m_a_ln_g, a_ln_b=m_a_ln_b, a_w_s=m_a_w_s, a_b_s=m_a_b_s,
                 a_w_out=m_a_w_out, b_w_qkv=m_b_w_qkv, b_b_qkv=m_b_b_qkv, b_sinks=m_b_sinks, b_w_o=m_b_w_o,
                 ffn_w_gu=m_ffn_w_gu, ffn_w_down=m_ffn_w_down)
    mom_v = dict(pre_mix_g=v_pre_mix_g, post_mix_g=v_post_mix_g, pre_ffn_g=v_pre_ffn_g, post_ffn_g=v_post_ffn_g,
                 a_w_in=v_a_w_in, a_b_in=v_a_b_in, a_ln_g=v_a_ln_g, a_ln_b=v_a_ln_b, a_w_s=v_a_w_s, a_b_s=v_a_b_s,
                 a_w_out=v_a_w_out, b_w_qkv=v_b_w_qkv, b_b_qkv=v_b_b_qkv, b_sinks=v_b_sinks, b_w_o=v_b_w_o,
                 ffn_w_gu=v_ffn_w_gu, ffn_w_down=v_ffn_w_down)
    names = list(weights)
    column_sharded = ("a_w_in", "b_w_qkv", "ffn_w_gu")
    row_sharded = ("a_w_out", "b_w_o", "ffn_w_down")
    big = column_sharded + row_sharded

    T, D = x.shape[1], x.shape[2]
    depth = pre_mix_g.shape[0]
    n_heads = b_sinks.shape[1]
    q_width = n_heads * HEAD_DIM
    kv_width = N_KV_HEADS * HEAD_DIM
    G = a_w_s.shape[1]
    W = a_ln_g.shape[1]
    device = _index(_position())

    dev = device.reshape(1).astype(jnp.int32)

    def layer_keys(i):
        mixer = ("a_w_in", "a_w_out") if i % 2 == 0 else ("b_w_qkv", "b_w_o")
        return [(k, i // 2) for k in mixer] + [("ffn_w_gu", i), ("ffn_w_down", i)]

    as_view = ("b_w_qkv", "ffn_w_gu")
    in_kernel = ("a_w_in",)
    view = lambda k, t: jnp.swapaxes(t, 1, 2) if k in as_view else t
    w_view = {k: view(k, weights[k]) for k in big}
    m_view = {k: view(k, mom_m[k]) for k in big}
    v_view = {k: view(k, mom_v[k]) for k in big}

    order = [(i, k, l) for i in range(depth) for k, l in layer_keys(i)]
    bias_land = lax.dynamic_update_slice(
        jnp.zeros((N_DEV,) + b_b_qkv.shape, F32), b_b_qkv[None], (device, 0, 0))
    h, hn = x[0], first_norm(x[0], pre_mix_g[0][None], "norm_first")
    first = len(layer_keys(0))
    in_gather, token = {}, ()
    for name, lo, hi in (("gather_start_first", 0, first), ("gather_start_rest", first, len(order))):
        groups = [[prep_weight(w_view[k], l, k in in_kernel, dev, "prep_" + k, after=token)] for _, k, l in order[lo:hi]]
        if lo:
            groups[0].append(bias_land)
        gather_sems, flat_lands, token = relay_start(groups, name)
        token = (token, hn)
        for (i, k, _), grp, sems in zip(order[lo:hi], groups, gather_sems):
            in_gather[(i, k)] = ([flat_lands.pop(0) for _ in grp], sems)
    all_started = token[0]
    use_order = [(i, k) for i, k, _ in order]
    passed, gathered = {}, {}

    def pass_on(key, after, tag):
        if key not in passed:
            lands, sems = in_gather[key]
            passed[key] = relay_pass(lands, sems, after, f"gather_pass_{tag}_{key[1]}_{key[0]}")
            in_gather[key] = (passed[key][1], sems)

    pass_on(use_order[0], (all_started,), "first")

    def weight(i, k, *after):
        if (i, k) not in gathered:
            at = use_order.index((i, k))
            if at + 1 < len(use_order):
                pass_on(use_order[at + 1], after, "ahead")
            lands, sems = in_gather[(i, k)]
            gathered[(i, k)] = relay_wait(lands, sems, passed[(i, k)][0], after, f"gather_wait_{k}_{i}")
        g = gathered[(i, k)][0]
        return g.reshape(N_DEV * g.shape[1], g.shape[2])

    causal = jnp.tril(jnp.ones((CHUNK, CHUNK), dtype=bool))
    w_mask = jnp.where(causal[None, None], a_w_s, 0.0).astype(BF16)
    w_mask_t = jnp.swapaxes(w_mask, 2, 3)
    b_s_t = jnp.swapaxes(a_b_s, 1, 2)
    group_onehot = (jnp.arange(W)[:, None] // (W // G) == jnp.arange(LANES)[None, :]).astype(BF16)
    lane = jnp.arange(LANES) % HEAD_DIM
    inv_freq = ROPE_THETA ** (-jnp.arange(0, ROPE_DIM, 2, dtype=F32) / ROPE_DIM)
    inv_freq_row = jnp.where(lane < ROPE_DIM, inv_freq[lane % (ROPE_DIM // 2)], 0.0)[None, :].astype(F32)
    tabs = rope_tables(positions.reshape(T, 1), inv_freq_row, "rope_tables")

    saved = []
    for i in range(depth):
        j = i // 2
        s = dict(h=h, hn=hn)
        if i % 2 == 0:
            s["a"] = rowmm(hn, weight(i, "a_w_in", hn, all_started), True, a_b_in[j][None], BF16, "gmlp_in")
            s["gated"] = gmlp_fwd(s["a"], a_ln_g[j][None], a_ln_b[j][None], w_mask[j], b_s_t[j], "gmlp_gate")
            mixed, w_mix = s["gated"], weight(i, "a_w_out", s["gated"])
        else:
            w_qkv = weight(i, "b_w_qkv", hn)
            b_qkv_full = jnp.swapaxes(gathered[(1, "b_w_qkv")][1], 0, 1).reshape(b_b_qkv.shape[0], 1, -1)
            s["qkv"] = rowmm(hn, w_qkv, True, b_qkv_full[j], BF16, "attn_qkv")
            s["qk"] = rope_fwd(s["qkv"], tabs, q_width, kv_width, "attn_rope")
            s["o"] = attn_fwd(s["qk"], s["qkv"], b_sinks[j], q_width, kv_width, "attn_core")
            mixed, w_mix = s["o"], weight(i, "b_w_o", s["o"])
        s["mix"], s["h2"], s["fn"] = mm_add_norm(mixed, w_mix, h, post_mix_g[i][None], pre_ffn_g[i][None], "mixer_out")
        s["act"], s["act_partials"] = ffn_up(s["fn"], weight(i, "ffn_w_gu", s["fn"]), "ffn_up")
        w_down = weight(i, "ffn_w_down", s["act"])
        saved.append(s)
        if i + 1 < depth:
            s["f"], h, hn = mm_add_norm(s["act"], w_down, s["h2"], post_ffn_g[i][None], pre_mix_g[i + 1][None], "ffn_down")
    small = {k: [None] * weights[k].shape[0] for k in names if k not in big}
    dh, df, loss_sum, small["post_ffn_g"][depth - 1] = mm_add_norm_loss(
        saved[-1]["act"], w_down, saved[-1]["h2"], post_ffn_g[depth - 1][None], loss_target[0], "ffn_down_loss")
    loss = lax.psum(loss_sum[0, 0] * (0.5 / D), ("x", "y", "c"))

    grads = {k: [None] * weights[k].shape[0] for k in big}
    in_flight = []

    def send_grads(keys, tag):
        sems, bufs, token = exchange_start([_blocked(grads[k][l]) for k, l in keys], "exchange_start_" + tag)
        in_flight.append((keys, sems, bufs, tag))
        return token

    for i in reversed(range(depth)):
        j = i // 2
        s = saved[i]
        dgu = ffn_dact(df, weight(i, "ffn_w_down"), s["act_partials"], "ffn_dact")
        grads["ffn_w_down"][i] = grad_mm(s["act"], df, "ffn_down_grad")
        grads["ffn_w_gu"][i] = grad_mm(dgu, s["fn"], "ffn_up_grad")
        sent = send_grads(layer_keys(i)[2:], f"ffn_{i}")
        r = mm_norm_bwd(dgu, weight(i, "ffn_w_gu"), dh, s["h2"], pre_ffn_g[i][None],
                        (s["mix"], post_mix_g[i][None]), "ffn_dx", sent)
        dh, dmix = r["dh"], r["dy"]
        small["pre_ffn_g"][i], small["post_mix_g"][i] = r["dg_pre"], r["dg_post"]
        if i % 2 == 0:
            grads["a_w_out"][j] = grad_mm(s["gated"], dmix, "gmlp_out_grad")
            dgated = rowmm(dmix, weight(i, "a_w_out"), True, None, BF16, "gmlp_dgated")
            dmixed, dws, dbs, dlg, dlb, dbin = gmlp_bwd(
                s["a"], dgated, a_ln_g[j][None], a_ln_b[j][None], w_mask[j], w_mask_t[j], b_s_t[j], group_onehot,
                "gmlp_gate_bwd")
            small["a_w_s"][j], small["a_b_s"][j] = dws, dbs[:, :G].T
            small["a_ln_g"][j], small["a_ln_b"][j], small["a_b_in"][j] = dlg[0], dlb[0], dbin[0]
            grads["a_w_in"][j] = grad_mm(dmixed, s["hn"], "gmlp_in_grad")
            w_in = weight(i, "a_w_in")
        else:
            grads["b_w_o"][j] = grad_mm(s["o"], dmix, "attn_out_grad")
            do = rowmm(dmix, weight(i, "b_w_o"), True, None, BF16, "attn_do")
            dq, dk, dv, dsink = attn_bwd(s["qk"], s["qkv"], do, b_sinks[j], q_width, kv_width, "attn_core_bwd")
            dmixed, dbias = rope_bwd(dq, dk, dv, tabs, "attn_rope_bwd")
            small["b_sinks"][j], small["b_b_qkv"][j] = dsink[0, :n_heads], dbias[0]
            grads["b_w_qkv"][j] = grad_mm(dmixed, s["hn"], "attn_qkv_grad")
            w_in = weight(i, "b_w_qkv")
        sent = send_grads(layer_keys(i)[:2], f"mixer_{i}")
        post = (saved[i - 1]["f"], post_ffn_g[i - 1][None]) if i > 0 else None
        r = mm_norm_bwd(dmixed, w_in, dh, s["h"], pre_mix_g[i][None], post, "mixer_dx", sent)
        dh, small["pre_mix_g"][i] = r["dh"], r["dg_pre"]
        if i > 0:
            df, small["post_ffn_g"][i - 1] = r["dy"], r["dg_post"]
    grad_x = dh[None]

    replicated = [k for k in names if k not in big and k != "b_b_qkv"]
    tile = LANES * LANES
    main = sum(weights[k].size for k in replicated)
    main_rows = -(-main // tile) * LANES
    bias_size = b_b_qkv.shape[0] * N_DEV * b_b_qkv.shape[1]
    bias_rows = -(-bias_size // (8 * LANES)) * 8
    flat = lambda vals: [v.reshape(-1) for v in vals]
    packed = jnp.concatenate(
        flat(p for k in replicated for p in small[k]) + [jnp.zeros((main_rows * LANES - main,), F32)]
        + flat(small["b_b_qkv"]) + [jnp.zeros((bias_rows * LANES - bias_size,), F32)]
    ).reshape(1, main_rows + bias_rows, LANES)
    small_sems, small_lands, small_sent = gather_start(
        [[prep_weight(packed, 0, False, dev, "place_small_grads", F32)]], "gather_start_small")
    pack_main = lambda src: jnp.concatenate(
        flat(src[k] for k in replicated) + [jnp.ones((main_rows * LANES - main,), F32)]).reshape(main_rows, LANES)
    packed_w, packed_m, packed_v = pack_main(weights), pack_main(mom_m), pack_main(mom_v)

    stacked = {k: [lax.empty(w_view[k].shape, F32) for _ in range(4)] for k in big}
    after = (small_sent, dh)
    for keys, sems, bufs, tag in in_flight:
        partials, landed = exchange_wait(bufs, sems, after, "exchange_wait_" + tag)
        for (k, l), part, land in zip(keys, partials, landed):
            stacked[k] = sum_adamw(part, land, w_view[k], m_view[k], v_view[k], stacked[k], l,
                                   k in in_kernel, dev, "adamw_" + k)
        after = tuple(stacked[k][0] for k, _ in keys)
    out_g, out_d, out_m, out_v = {}, {}, {}, {}
    for k in big:
        out_g[k], out_d[k], out_m[k], out_v[k] = [view(k, t) for t in stacked[k]]

    gathered_small = gather_wait(small_lands, small_sems[0], after + (packed_w, packed_m, packed_v),
                                 "gather_wait_small")[0]
    results = sum_adamw_small(gathered_small, packed_w, packed_m, packed_v, "adamw_small")
    off = 0
    for k in replicated:
        size, shape = weights[k].size, weights[k].shape
        out_g[k], out_d[k], out_m[k], out_v[k] = [t.reshape(-1)[off:off + size].reshape(shape) for t in results]
        off += size
    n_local = b_b_qkv.shape[1]
    bias_sum = sum_parts(gathered_small[None, :, main_rows:], False, "sum_bias_grads")[0].reshape(-1)[:bias_size]
    out_g["b_b_qkv"] = lax.dynamic_slice_in_dim(
        bias_sum.reshape(b_b_qkv.shape[0], -1), device * n_local, n_local, axis=1)
    pad = lambda t: jnp.concatenate([t.reshape(-1), jnp.ones((8 * LANES - t.size,), F32)]).reshape(8, LANES)
    bias_update = adamw(pad(b_b_qkv), pad(out_g["b_b_qkv"]), pad(m_b_b_qkv), pad(v_b_b_qkv), "adamw_bias")
    out_d["b_b_qkv"], out_m["b_b_qkv"], out_v["b_b_qkv"] = [
        t.reshape(-1)[:b_b_qkv.size].reshape(b_b_qkv.shape) for t in bias_update]

    return (loss, grad_x, *[out_g[k] for k in names], *[out_d[k] for k in names],
            *[out_m[k] for k in names], *[out_v[k] for k in names])
```

```python
import math

import jax
import jax.numpy as jnp
from jax import lax
from jax.experimental import pallas as pl
from jax.experimental.pallas import tpu as pltpu

F32, BF16 = jnp.float32, jnp.bfloat16
MESH = pl.DeviceIdType.MESH
N_DEV = 8

CHUNK = 128
HEAD_DIM = 64
N_KV_HEADS = 4
ROPE_DIM = HEAD_DIM // 4
ROPE_THETA = 500000.0
RMS_EPS = 1e-6
LN_EPS = 1e-5
NEG_INF = -1e30

ADAM_LR = 0.001
ADAM_B1 = 0.9
ADAM_B2 = 0.999
ADAM_EPS = 1e-08
ADAM_WD = 0.01
ADAM_STEP = 10

V7X_VMEM_BYTES = 64 * 2 ** 20
VMEM_LIMIT = V7X_VMEM_BYTES - 8 * 2 ** 20
LANES = 128


def _params(*sem):
    return pltpu.CompilerParams(dimension_semantics=sem or None, vmem_limit_bytes=VMEM_LIMIT)


def _tile(n, pref):
    if n <= pref:
        return n
    t = pref - pref % LANES
    while t >= LANES:
        if n % t == 0:
            return t
        t -= LANES
    return n


def _full(shape):
    return pl.BlockSpec(shape, lambda *_: (0,) * len(shape))


def _rows(tm, width):
    return pl.BlockSpec((tm, width), lambda i: (i, 0))


def _rms(x, g):
    r = lax.rsqrt(jnp.mean(x * x, axis=-1, keepdims=True) + RMS_EPS)
    return x * r * g


def _rms_bwd(x, g, dy):
    r = lax.rsqrt(jnp.mean(x * x, axis=-1, keepdims=True) + RMS_EPS)
    xhat = x * r
    dg = jnp.sum(dy * xhat, axis=0, keepdims=True)
    dxhat = dy * g
    dx = r * (dxhat - xhat * jnp.mean(dxhat * xhat, axis=-1, keepdims=True))
    return dx, dg


_INV_SQRT2 = 1.0 / math.sqrt(2.0)
_INV_SQRT2PI = 1.0 / math.sqrt(2.0 * math.pi)


def _gelu(x):
    return 0.5 * x * (1.0 + lax.erf(x * _INV_SQRT2))


def _gelu_and_grad(x):
    cdf = 0.5 * (1.0 + lax.erf(x * _INV_SQRT2))
    return x * cdf, cdf + x * jnp.exp(-0.5 * x * x) * _INV_SQRT2PI


def _sigmoid(x):
    return 0.5 * jnp.tanh(0.5 * x) + 0.5


def _position():
    return lax.axis_index("x"), lax.axis_index("y"), lax.axis_index("c")


def _index(p):
    return 4 * p[0] + 2 * p[1] + p[2]


_HBM = pl.BlockSpec(memory_space=pltpu.HBM)
_SEM = pl.BlockSpec(memory_space=pltpu.SEMAPHORE)
_ORDERED_BY_DATA = pltpu.SideEffectType.DATAFLOW_SIDE_EFFECTING


def _in_hbm(v):
    return pltpu.with_memory_space_constraint(v, pltpu.HBM)


def _peer(k):
    x, y, c = _position()
    return (x ^ ((k >> 2) & 1), y ^ ((k >> 1) & 1), c ^ (k & 1))


def _split_copies(srcs, dsts, send, recv):
    return [pltpu.make_async_remote_copy(
        src_ref=src(k), dst_ref=dst(k), send_sem=send.at[7 * a + k - 1], recv_sem=recv.at[7 * a + k - 1],
        device_id=_peer(k), device_id_type=MESH)
        for a, (src, dst) in enumerate(zip(srcs, dsts)) for k in range(1, N_DEV)]


def _start_call(groups, sent, copies_of, name, per_array=N_DEV - 1):
    flat = [v for grp in groups for v in grp]
    n, ng = len(flat), len(groups)

    def body(*refs):
        bufs, sems = refs[:n], refs[n:n + 2 * ng]
        q = 0
        for g, grp in enumerate(groups):
            for cp in copies_of(bufs[q:q + len(grp)], sems[2 * g], sems[2 * g + 1]):
                cp.start()
            q += len(grp)
        refs[-1][...] = jnp.zeros_like(refs[-1])

    sem_shapes = []
    for count in sent:
        sem_shapes += [pltpu.SemaphoreType.DMA((per_array * count,)) for _ in range(2)]
    res = pl.pallas_call(
        body, name=name,
        out_shape=sem_shapes + [pltpu.HBM(v.shape, v.dtype) for v in flat] + [jax.ShapeDtypeStruct((8, LANES), F32)],
        in_specs=[_HBM] * n,
        out_specs=[_SEM] * (2 * ng) + [_HBM] * n + [pl.BlockSpec(memory_space=pltpu.VMEM)],
        input_output_aliases={i: 2 * ng + i for i in range(n)},
        compiler_params=pltpu.CompilerParams(has_side_effects=_ORDERED_BY_DATA),
    )(*[_in_hbm(v) for v in flat])
    sems = [(res[2 * g], res[2 * g + 1]) for g in range(ng)]
    return sems, list(res[2 * ng:-1]), res[-1]


def _wait_call(bufs, sems, copies_of, after, name):
    n = len(bufs)

    def body(*refs):
        for cp in copies_of(refs[:n], refs[n], refs[n + 1]):
            cp.wait_send()
            cp.wait_recv()

    return list(pl.pallas_call(
        body, name=name,
        out_shape=[pltpu.HBM(v.shape, v.dtype) for v in bufs],
        in_specs=[_HBM] * n + [_SEM, _SEM] + [pl.BlockSpec(memory_space=pl.ANY)] * len(after),
        out_specs=[_HBM] * n,
        input_output_aliases={i: i for i in range(n)},
        compiler_params=pltpu.CompilerParams(has_side_effects=_ORDERED_BY_DATA),
    )(*bufs, sems[0], sems[1], *after))


def _gather_copies(lands, send, recv):
    me = _index(_position())
    mine = [lambda k, ref=ref: ref.at[pl.ds(me, 1)] for ref in lands]
    return _split_copies(mine, mine, send, recv)


def _gather_arrivals(lands, send, recv):
    me = _index(_position())
    mine = [lambda k, ref=ref: ref.at[pl.ds(me, 1)] for ref in lands]
    theirs = [lambda k, ref=ref: ref.at[pl.ds(_index(_peer(k)), 1)] for ref in lands]
    return _split_copies(mine, theirs, send, recv)


def _exchange_copies(bufs, send, recv):
    half = len(bufs) // 2
    srcs = [lambda k, ref=ref: ref.at[pl.ds(_index(_peer(k)), 1)] for ref in bufs[:half]]
    dsts = [lambda k, ref=ref: ref.at[pl.ds(k - 1, 1)] for ref in bufs[half:]]
    return _split_copies(srcs, dsts, send, recv)


def gather_start(groups, name):
    return _start_call(groups, [len(grp) for grp in groups], _gather_copies, name)


def gather_wait(lands, sems, after, name):
    return _wait_call(lands, sems, _gather_arrivals, after, name)


_DIRECT = (1, 2, 4, 6)
_PASSED = (3, 5, 7)


def _remote(src, dst, send, recv, q, k):
    return pltpu.make_async_remote_copy(src_ref=src, dst_ref=dst, send_sem=send.at[q], recv_sem=recv.at[q],
                                        device_id=_peer(k), device_id_type=MESH)


def _slot_of(land, k):
    return land.at[pl.ds(_index(_peer(k)), 1)]


def relay_start(groups, name):
    def copies(lands, send, recv):
        return [_remote(_slot_of(land, 0), _slot_of(land, 0), send, recv, len(_DIRECT) * a + j, k)
                for a, land in enumerate(lands) for j, k in enumerate(_DIRECT)]
    return _start_call(groups, [len(grp) for grp in groups], copies, name, per_array=len(_DIRECT))


def relay_pass(lands, sems, after, name):
    n = len(lands)

    def body(*refs):
        bufs, recv_first = refs[:n], refs[n]
        send, recv = refs[n + 1 + len(after):n + 3 + len(after)]
        for a, land in enumerate(bufs):
            for j, k in enumerate(_PASSED):
                came = _slot_of(land, k ^ 1)
                _remote(came, came, recv_first, recv_first, len(_DIRECT) * a + _DIRECT.index(k ^ 1), k ^ 1).wait_recv()
                _remote(came, came, send, recv, len(_PASSED) * a + j, 1).start()

    res = pl.pallas_call(
        body, name=name,
        out_shape=[pltpu.SemaphoreType.DMA((len(_PASSED) * n,))] * 2 + [pltpu.HBM(v.shape, v.dtype) for v in lands],
        in_specs=[_HBM] * n + [_SEM] + [pl.BlockSpec(memory_space=pl.ANY)] * len(after),
        out_specs=[_SEM, _SEM] + [_HBM] * n,
        input_output_aliases={i: 2 + i for i in range(n)},
        compiler_params=pltpu.CompilerParams(has_side_effects=_ORDERED_BY_DATA),
    )(*lands, sems[1], *after)
    return (res[0], res[1]), list(res[2:])


def relay_wait(lands, first, second, after, name):
    n = len(lands)

    def body(*refs):
        bufs = refs[:n]
        send_first, recv_first, send, recv = refs[n:n + 4]
        for a, land in enumerate(bufs):
            mine = _slot_of(land, 0)
            _remote(mine, _slot_of(land, 1), send_first, recv_first, len(_DIRECT) * a, 1).wait_recv()
            for j, k in enumerate(_DIRECT):
                _remote(mine, mine, send_first, recv_first, len(_DIRECT) * a + j, k).wait_send()
            for j, k in enumerate(_PASSED):
                cp = _remote(_slot_of(land, k ^ 1), _slot_of(land, k), send, recv, len(_PASSED) * a + j, 1)
                cp.wait_send()
                cp.wait_recv()

    return list(pl.pallas_call(
        body, name=name,
        out_shape=[pltpu.HBM(v.shape, v.dtype) for v in lands],
        in_specs=[_HBM] * n + [_SEM] * 4 + [pl.BlockSpec(memory_space=pl.ANY)] * len(after),
        out_specs=[_HBM] * n,
        input_output_aliases={i: i for i in range(n)},
        compiler_params=pltpu.CompilerParams(has_side_effects=_ORDERED_BY_DATA),
    )(*lands, first[0], first[1], second[0], second[1], *after))


def exchange_start(partials, name):
    lands = [lax.empty((N_DEV - 1,) + p.shape[1:], p.dtype) for p in partials]
    sems, bufs, token = _start_call([list(partials) + lands], [len(partials)], _exchange_copies, name)
    return sems[0], bufs, token


def exchange_wait(bufs, sems, after, name):
    bufs = _wait_call(bufs, sems, _exchange_copies, after, name)
    return bufs[:len(bufs) // 2], bufs[len(bufs) // 2:]


def prep_weights(items, dev, name, dtype=BF16, after=()):
    n = len(items)
    blocks = [(1, w.shape[2], w.shape[1]) if t else (1,) + w.shape[1:] for w, _, t in items]

    def body(d_ref, *refs):
        for (_, _, transpose), w_ref, o_ref in zip(items, refs[:n], refs[-n:]):
            v = w_ref[0]
            o_ref[0] = (v.T if transpose else v).astype(dtype)

    return pl.pallas_call(
        body, name=name,
        grid_spec=pltpu.PrefetchScalarGridSpec(
            num_scalar_prefetch=1, grid=(1,),
            in_specs=[pl.BlockSpec((1,) + w.shape[1:], lambda i, d, layer=layer: (layer, 0, 0)) for w, layer, _ in items]
            + [pl.BlockSpec(memory_space=pl.ANY)] * len(after),
            out_specs=[pl.BlockSpec(blk, lambda i, d: (d[0], 0, 0)) for blk in blocks]),
        out_shape=[jax.ShapeDtypeStruct((N_DEV,) + blk[1:], dtype) for blk in blocks],
        compiler_params=_params("arbitrary"),
    )(dev, *[w for w, _, _ in items], *after)


def _adamw_math(w, g, m, v):
    mn = ADAM_B1 * m + (1.0 - ADAM_B1) * g
    vn = ADAM_B2 * v + (1.0 - ADAM_B2) * (g * g)
    m_hat = mn * (1.0 / (1.0 - ADAM_B1 ** ADAM_STEP))
    v_hat = vn * (1.0 / (1.0 - ADAM_B2 ** ADAM_STEP))
    return -ADAM_LR * (m_hat / (jnp.sqrt(v_hat) + ADAM_EPS) + ADAM_WD * w), mn, vn


def sum_adamw(partial, landed, w, m, v, prev, layer, transpose, dev, name):
    _, r, c = partial.shape
    if transpose:
        tc = _tile(c, 256)
        grid = (c // tc,)
        part_spec = pl.BlockSpec((1, r, tc), lambda i, d: (d[0], 0, i))
        land_spec = pl.BlockSpec((N_DEV - 1, r, tc), lambda i, d: (0, 0, i))
        w_spec = pl.BlockSpec((1, tc, r), lambda i, d: (layer, i, 0))
    else:
        tr = r // 2 if r % 32 == 0 else r
        grid = (r // tr,)
        part_spec = pl.BlockSpec((1, tr, c), lambda i, d: (d[0], i, 0))
        land_spec = pl.BlockSpec((N_DEV - 1, tr, c), lambda i, d: (0, i, 0))
        w_spec = pl.BlockSpec((1, tr, c), lambda i, d: (layer, i, 0))

    def body(d_ref, p_ref, l_ref, w_ref, m_ref, v_ref, *rest):
        g_ref, dl_ref, mo_ref, vo_ref = rest[-4:]
        acc = p_ref[0].astype(F32)
        for k in range(N_DEV - 1):
            acc = acc + l_ref[k].astype(F32)
        g = acc.T if transpose else acc
        delta, mn, vn = _adamw_math(w_ref[0], g, m_ref[0], v_ref[0])
        g_ref[0], dl_ref[0], mo_ref[0], vo_ref[0] = g, delta, mn, vn

    return pl.pallas_call(
        body, name=name,
        grid_spec=pltpu.PrefetchScalarGridSpec(
            num_scalar_prefetch=1, grid=grid,
            in_specs=[part_spec, land_spec, w_spec, w_spec, w_spec] + [pl.BlockSpec(memory_space=pl.ANY)] * 4,
            out_specs=[w_spec] * 4),
        out_shape=[jax.ShapeDtypeStruct(w.shape, F32)] * 4,
        input_output_aliases={6 + q: q for q in range(4)},
        compiler_params=_params("parallel"),
    )(dev, partial, landed, w, m, v, *prev)


def sum_parts(parts, name):
    P, r, c = parts.shape

    def body(p_ref, o_ref):
        acc = p_ref[0].astype(F32)
        for s in range(1, P):
            acc = acc + p_ref[s].astype(F32)
        o_ref[...] = acc

    return pl.pallas_call(
        body, name=name, in_specs=[_full((P, r, c))], out_specs=_full((r, c)), grid=(1,),
        out_shape=jax.ShapeDtypeStruct((r, c), F32), compiler_params=_params("arbitrary"))(parts)


def adamw(w, g, m, v, name):
    R, C = w.shape
    tr = _tile(R, 512)

    def body(w_ref, g_ref, m_ref, v_ref, d_ref, mo_ref, vo_ref):
        d_ref[...], mo_ref[...], vo_ref[...] = _adamw_math(w_ref[...], g_ref[...], m_ref[...], v_ref[...])

    spec = pl.BlockSpec((tr, C), lambda i: (i, 0))
    return pl.pallas_call(
        body, name=name, grid=(R // tr,),
        in_specs=[spec] * 4, out_specs=[spec] * 3,
        out_shape=[jax.ShapeDtypeStruct((R, C), F32)] * 3,
        compiler_params=_params("parallel"),
    )(w, g, m, v)


def sum_adamw_small(gathered, w, m, v, name):
    rows = w.shape[0]
    tr = _tile(rows, 512)

    def body(p_ref, w_ref, m_ref, v_ref, g_ref, d_ref, mo_ref, vo_ref):
        g = p_ref[0]
        for s in range(1, N_DEV):
            g = g + p_ref[s]
        g_ref[...] = g
        d_ref[...], mo_ref[...], vo_ref[...] = _adamw_math(w_ref[...], g, m_ref[...], v_ref[...])

    spec = pl.BlockSpec((tr, LANES), lambda i: (i, 0))
    return pl.pallas_call(
        body, name=name, grid=(rows // tr,),
        in_specs=[pl.BlockSpec((N_DEV, tr, LANES), lambda i: (0, i, 0)), spec, spec, spec], out_specs=[spec] * 4,
        out_shape=[jax.ShapeDtypeStruct((rows, LANES), F32)] * 4, compiler_params=_params("parallel"))(gathered, w, m, v)


def first_norm(h, g, name):
    T, D = h.shape
    tm = _tile(T, 512)

    def body(h_ref, g_ref, hn_ref):
        hn_ref[...] = _rms(h_ref[...], g_ref[...]).astype(BF16)

    row = _rows(tm, D)
    return pl.pallas_call(
        body, name=name, grid=(T // tm,), in_specs=[row, _full((1, D))], out_specs=row,
        out_shape=jax.ShapeDtypeStruct((T, D), BF16), compiler_params=_params("parallel"))(h, g)


_NT = (((1,), (1,)), ((), ()))
_TN = (((0,), (0,)), ((), ()))


def rowmm(a, w, transposed_w, bias, out_dtype, name, after=None):
    M, K = a.shape
    N = w.shape[0] if transposed_w else w.shape[1]
    tm = _tile(M, 256)
    tn = _tile(N, 512)

    def body(*refs):
        a_ref, w_ref = refs[:2]
        o_ref = refs[-1]
        av = a_ref[...]
        for j in range(N // tn):
            cols = pl.ds(j * tn, tn)
            if transposed_w:
                acc = lax.dot_general(av, w_ref[cols, :], _NT, preferred_element_type=F32)
            else:
                acc = jnp.dot(av, w_ref[:, cols], preferred_element_type=F32)
            if bias is not None:
                acc = acc + refs[2][:, cols]
            o_ref[:, cols] = acc.astype(out_dtype)

    ins, in_specs = [a, w], [_rows(tm, K), _full(w.shape)]
    if bias is not None:
        ins.append(bias)
        in_specs.append(_full((1, N)))
    if after is not None:
        ins.append(after)
        in_specs.append(pl.BlockSpec(memory_space=pl.ANY))
    return pl.pallas_call(
        body, name=name, grid=(M // tm,), in_specs=in_specs, out_specs=_rows(tm, N),
        out_shape=jax.ShapeDtypeStruct((M, N), out_dtype), compiler_params=_params("parallel"))(*ins)


def _rows_times_weight(a_ref, w_ref, acc_ref):
    N = w_ref.shape[1]
    tn = _tile(N, 512)
    av = a_ref[...]
    for j in range(N // tn):
        cols = pl.ds(j * tn, tn)
        acc_ref[:, cols] = jnp.dot(av, w_ref[:, cols], preferred_element_type=F32)


def mm_add_norm(a, w, h, g_post, g_pre, name):
    T, K = a.shape
    D = w.shape[1]
    tm = _tile(T, 256)

    def body(a_ref, w_ref, h_ref, gp_ref, g_ref, y_ref, ho_ref, hn_ref, acc_ref):
        _rows_times_weight(a_ref, w_ref, acc_ref)
        y = acc_ref[...]
        y_ref[...] = y.astype(BF16)
        hv = h_ref[...] + _rms(y, gp_ref[...])
        ho_ref[...] = hv
        hn_ref[...] = _rms(hv, g_ref[...]).astype(BF16)

    row, vec = _rows(tm, D), _full((1, D))
    return pl.pallas_call(
        body, name=name, grid=(T // tm,), in_specs=[_rows(tm, K), _full(w.shape), row, vec, vec],
        out_specs=[row, row, row],
        out_shape=[jax.ShapeDtypeStruct((T, D), BF16), jax.ShapeDtypeStruct((T, D), F32),
                   jax.ShapeDtypeStruct((T, D), BF16)],
        scratch_shapes=[pltpu.VMEM((tm, D), F32)], compiler_params=_params("parallel"))(a, w, h, g_post, g_pre)


def mm_add_norm_loss(a, w, h, g_post, target, name):
    T, K = a.shape
    D = w.shape[1]
    tm = _tile(T, 256)

    def body(a_ref, w_ref, h_ref, gp_ref, t_ref, dh_ref, dy_ref, loss_ref, dgp_ref, acc_ref):
        @pl.when(pl.program_id(0) == 0)
        def _():
            loss_ref[...] = jnp.zeros_like(loss_ref)
            dgp_ref[...] = jnp.zeros_like(dgp_ref)
        _rows_times_weight(a_ref, w_ref, acc_ref)
        y = acc_ref[...]
        err = h_ref[...] + _rms(y, gp_ref[...]) - t_ref[...]
        dh = err * (1.0 / D)
        dh_ref[...] = dh
        loss_ref[...] += jnp.sum(err * err)
        dy, dgp = _rms_bwd(y, gp_ref[...], dh)
        dy_ref[...] = dy.astype(BF16)
        dgp_ref[...] += dgp

    row, vec = _rows(tm, D), _full((1, D))
    return pl.pallas_call(
        body, name=name, grid=(T // tm,), in_specs=[_rows(tm, K), _full(w.shape), row, vec, row],
        out_specs=[row, row, _full((8, LANES)), vec],
        out_shape=[jax.ShapeDtypeStruct((T, D), F32), jax.ShapeDtypeStruct((T, D), BF16),
                   jax.ShapeDtypeStruct((8, LANES), F32), jax.ShapeDtypeStruct((1, D), F32)],
        scratch_shapes=[pltpu.VMEM((tm, D), F32)], compiler_params=_params("arbitrary"))(a, w, h, g_post, target)


def mm_norm_bwd(a, w, dh_in, x, g_pre, post, name, after):
    T, K = a.shape
    D = w.shape[1]
    tm = _tile(T, 256)
    row, vec = _rows(tm, D), _full((1, D))
    ins, in_specs = [a, w, dh_in, x, g_pre], [_rows(tm, K), _full(w.shape), row, row, vec]
    outs = [jax.ShapeDtypeStruct((T, D), F32), jax.ShapeDtypeStruct((1, D), F32)]
    out_specs, names = [row, vec], ["dh", "dg_pre"]
    if post is not None:
        ins += list(post)
        in_specs += [row, vec]
        outs += [jax.ShapeDtypeStruct((T, D), BF16), jax.ShapeDtypeStruct((1, D), F32)]
        out_specs += [row, vec]
        names += ["dy", "dg_post"]
    n_in = len(ins) + 1

    def body(*refs):
        a_ref, w_ref, dh_ref_in, x_ref, g_ref = refs[:5]
        out_refs, acc_ref = refs[n_in:-1], refs[-1]
        first = pl.program_id(0) == 0
        _rows_times_weight(a_ref, w_ref, acc_ref)
        dx, dg = _rms_bwd(x_ref[...], g_ref[...], acc_ref[...])
        dh = dh_ref_in[...] + dx
        out_refs[0][...] = dh

        @pl.when(first)
        def _():
            for r in out_refs[1::2]:
                r[...] = jnp.zeros_like(r)
        out_refs[1][...] += dg
        if post is not None:
            y_ref, gp_ref = refs[5:7]
            dy, dgp = _rms_bwd(y_ref[...].astype(F32), gp_ref[...], dh)
            out_refs[2][...] = dy.astype(BF16)
            out_refs[3][...] += dgp

    res = pl.pallas_call(
        body, name=name, grid=(T // tm,), in_specs=in_specs + [pl.BlockSpec(memory_space=pl.ANY)],
        out_specs=out_specs, out_shape=outs, scratch_shapes=[pltpu.VMEM((tm, D), F32)],
        compiler_params=_params("arbitrary"))(*ins, after)
    return dict(zip(names, res))


def grad_mm(a, b, name):
    T, N = a.shape
    K = b.shape[1]
    tn = _tile(N, 1408)
    tt = _tile(T, 1024)

    def body(a_ref, b_ref, o_ref, acc_ref):
        t = pl.program_id(1)

        @pl.when(t == 0)
        def _():
            acc_ref[...] = jnp.zeros_like(acc_ref)
        acc_ref[...] += lax.dot_general(a_ref[...], b_ref[...], _TN, preferred_element_type=F32)

        @pl.when(t == pl.num_programs(1) - 1)
        def _():
            o_ref[...] = acc_ref[...].astype(BF16)

    return pl.pallas_call(
        body, name=name, grid=(N // tn, T // tt),
        in_specs=[pl.BlockSpec((tt, tn), lambda j, t: (t, j)), pl.BlockSpec((tt, K), lambda j, t: (t, 0))],
        out_specs=pl.BlockSpec((tn, K), lambda j, t: (j, 0)),
        out_shape=jax.ShapeDtypeStruct((N, K), BF16),
        scratch_shapes=[pltpu.VMEM((tn, K), F32)],
        compiler_params=_params("parallel", "arbitrary"))(a, b)


def ffn_up(fn, w_gu_t, name):
    T, D = fn.shape
    F = w_gu_t.shape[0] // 2
    tm = _tile(T, 256)
    tn = _tile(F, 512)

    def body(a_ref, w_ref, act_ref, part_ref):
        av = a_ref[...]
        for j in range(F // tn):
            g = lax.dot_general(av, w_ref[pl.ds(j * tn, tn), :], _NT, preferred_element_type=F32)
            up = lax.dot_general(av, w_ref[pl.ds(F + j * tn, tn), :], _NT, preferred_element_type=F32)
            sg = _sigmoid(g)
            silu = g * sg
            act_ref[:, pl.ds(j * tn, tn)] = (silu * up).astype(BF16)
            part_ref[:, pl.ds(j * tn, tn)] = (up * (sg + silu * (1.0 - sg))).astype(BF16)
            part_ref[:, pl.ds(F + j * tn, tn)] = silu.astype(BF16)

    return pl.pallas_call(
        body, name=name, grid=(T // tm,), in_specs=[_rows(tm, D), _full(w_gu_t.shape)],
        out_specs=[_rows(tm, F), _rows(tm, 2 * F)],
        out_shape=[jax.ShapeDtypeStruct((T, F), BF16), jax.ShapeDtypeStruct((T, 2 * F), BF16)],
        compiler_params=_params("parallel"))(fn, w_gu_t)


def ffn_dact(df, w_down, partials, name):
    T, D = df.shape
    F = w_down.shape[0]
    tm = _tile(T, 256)
    tn = _tile(F, 512)

    def body(d_ref, w_ref, part_ref, o_ref):
        dv = d_ref[...]
        for j in range(F // tn):
            dact = lax.dot_general(dv, w_ref[pl.ds(j * tn, tn), :], _NT, preferred_element_type=F32)
            for cols in (pl.ds(j * tn, tn), pl.ds(F + j * tn, tn)):
                o_ref[:, cols] = (dact * part_ref[:, cols].astype(F32)).astype(BF16)

    return pl.pallas_call(
        body, name=name, grid=(T // tm,), in_specs=[_rows(tm, D), _full(w_down.shape), _rows(tm, 2 * F)],
        out_specs=_rows(tm, 2 * F), out_shape=jax.ShapeDtypeStruct((T, 2 * F), BF16),
        compiler_params=_params("parallel"))(df, w_down, partials)


def _layernorm_stats(v):
    mu = jnp.mean(v, axis=-1, keepdims=True)
    cen = v - mu
    rstd = lax.rsqrt(jnp.mean(cen * cen, axis=-1, keepdims=True) + LN_EPS)
    return cen * rstd, rstd


def gmlp_in(hn, w_t, bias, name):
    T, D = hn.shape
    N = w_t.shape[0]
    tm = _tile(T, 256)
    tn = _tile(N, 512)

    def body(a_ref, w_ref, b_ref, act_ref, dact_ref):
        av = a_ref[...]
        for j in range(N // tn):
            cols = pl.ds(j * tn, tn)
            pre = lax.dot_general(av, w_ref[cols, :], _NT, preferred_element_type=F32) + b_ref[:, cols]
            act, dact = _gelu_and_grad(pre)
            act_ref[:, cols] = act.astype(BF16)
            dact_ref[:, cols] = dact.astype(BF16)

    return pl.pallas_call(
        body, name=name, grid=(T // tm,), in_specs=[_rows(tm, D), _full(w_t.shape), _full((1, N))],
        out_specs=[_rows(tm, N), _rows(tm, N)], out_shape=[jax.ShapeDtypeStruct((T, N), BF16)] * 2,
        compiler_params=_params("parallel"))(hn, w_t, bias)


def gmlp_fwd(act, ln_g, ln_b, w_mask, b_s_t, name):
    T, W2 = act.shape
    W = W2 // 2
    G = w_mask.shape[0]
    C = W // G

    def body(a_ref, g_ref, b_ref, w_ref, bs_ref, o_ref):
        xhat, _ = _layernorm_stats(a_ref[:, W:].astype(F32))
        vln = (xhat * g_ref[...] + b_ref[...]).astype(BF16)
        for g in range(G):
            cols = pl.ds(g * C, C)
            sv = jnp.dot(w_ref[g], vln[:, g * C:(g + 1) * C], preferred_element_type=F32) + bs_ref[:, g:g + 1]
            o_ref[:, cols] = (a_ref[:, cols].astype(F32) * sv).astype(BF16)

    return pl.pallas_call(
        body, name=name, grid=(T // CHUNK,),
        in_specs=[_rows(CHUNK, W2), _full((1, W)), _full((1, W)), _full(w_mask.shape), _full(b_s_t.shape)],
        out_specs=_rows(CHUNK, W), out_shape=jax.ShapeDtypeStruct((T, W), BF16),
        compiler_params=_params("parallel"))(act, ln_g, ln_b, w_mask, b_s_t)


def gmlp_bwd(act, dact, dgated, ln_g, ln_b, w_mask, w_mask_t, b_s_t, group_onehot, name):
    T, W2 = act.shape
    W = W2 // 2
    G = w_mask.shape[0]
    C = W // G

    def body(a_ref, da_in_ref, dg_ref, g_ref, b_ref, w_ref, wt_ref, bs_ref, e_ref,
             da_ref, dws_ref, dbs_ref, dlg_ref, dlb_ref, dbin_ref, dvln_ref):
        @pl.when(pl.program_id(0) == 0)
        def _():
            for r in (dws_ref, dbs_ref, dlg_ref, dlb_ref, dbin_ref):
                r[...] = jnp.zeros_like(r)
        row = lax.broadcasted_iota(jnp.int32, (CHUNK, CHUNK), 0)
        col = lax.broadcasted_iota(jnp.int32, (CHUNK, CHUNK), 1)
        causal = col <= row
        gelu_v, gelu_grad_v = a_ref[:, W:].astype(F32), da_in_ref[:, W:].astype(F32)
        xhat, rstd = _layernorm_stats(gelu_v)
        vln = (xhat * g_ref[...] + b_ref[...]).astype(BF16)
        for g in range(G):
            cols = pl.ds(g * C, C)
            vg = vln[:, g * C:(g + 1) * C]
            gelu_u, gelu_grad_u = a_ref[:, cols].astype(F32), da_in_ref[:, cols].astype(F32)
            dgated = dg_ref[:, cols].astype(F32)
            dsv = (dgated * gelu_u).astype(BF16)
            dbs_ref[...] += jnp.dot(dsv, e_ref[cols, :], preferred_element_type=F32)
            sv = jnp.dot(w_ref[g], vg, preferred_element_type=F32) + bs_ref[:, g:g + 1]
            dau = dgated * sv * gelu_grad_u
            da_ref[:, cols] = dau.astype(BF16)
            dbin_ref[:, cols] += jnp.sum(dau, axis=0, keepdims=True)
            dws_ref[g] += jnp.where(causal, lax.dot_general(dsv, vg, _NT, preferred_element_type=F32), 0.0)
            dvln_ref[:, cols] = jnp.dot(wt_ref[g], dsv, preferred_element_type=F32)
        dvln = dvln_ref[...]
        dlg_ref[...] += jnp.sum(dvln * xhat, axis=0, keepdims=True)
        dlb_ref[...] += jnp.sum(dvln, axis=0, keepdims=True)
        dxhat = dvln * g_ref[...]
        dv = rstd * (dxhat - jnp.mean(dxhat, axis=-1, keepdims=True)
                     - xhat * jnp.mean(dxhat * xhat, axis=-1, keepdims=True))
        dav = dv * gelu_grad_v
        da_ref[:, W:] = dav.astype(BF16)
        dbin_ref[:, W:] += jnp.sum(dav, axis=0, keepdims=True)

    return pl.pallas_call(
        body, name=name, grid=(T // CHUNK,),
        in_specs=[_rows(CHUNK, W2), _rows(CHUNK, W2), _rows(CHUNK, W), _full((1, W)), _full((1, W)), _full(w_mask.shape),
                  _full(w_mask_t.shape), _full(b_s_t.shape), _full(group_onehot.shape)],
        out_specs=[_rows(CHUNK, W2), _full((G, CHUNK, CHUNK)), _full((CHUNK, LANES)), _full((1, W)),
                   _full((1, W)), _full((1, W2))],
        out_shape=[jax.ShapeDtypeStruct((T, W2), BF16), jax.ShapeDtypeStruct((G, CHUNK, CHUNK), F32),
                   jax.ShapeDtypeStruct((CHUNK, LANES), F32), jax.ShapeDtypeStruct((1, W), F32),
                   jax.ShapeDtypeStruct((1, W), F32), jax.ShapeDtypeStruct((1, W2), F32)],
        scratch_shapes=[pltpu.VMEM((CHUNK, W), F32)],
        compiler_params=_params("arbitrary"))(act, dact, dgated, ln_g, ln_b, w_mask, w_mask_t, b_s_t, group_onehot)


def rope_tables(pos, inv_freq_row, name):
    T = pos.shape[0]
    tm = _tile(T, 512)

    def body(p_ref, f_ref, c_ref, s1_ref, s2_ref):
        ang = p_ref[...].astype(F32) * f_ref[...]
        lane = lax.broadcasted_iota(jnp.int32, (tm, LANES), 1) % HEAD_DIM
        sin = jnp.sin(ang)
        c_ref[...] = jnp.cos(ang)
        s1_ref[...] = jnp.where(lane < _HALF, -sin, 0.0)
        s2_ref[...] = jnp.where((lane >= _HALF) & (lane < ROPE_DIM), sin, 0.0)

    tab = _rows(tm, LANES)
    return pl.pallas_call(
        body, name=name, grid=(T // tm,), in_specs=[_rows(tm, 1), _full((1, LANES))], out_specs=[tab] * 3,
        out_shape=[jax.ShapeDtypeStruct((T, LANES), F32)] * 3, compiler_params=_params("parallel"))(pos, inv_freq_row)


_HALF = ROPE_DIM // 2


def _slabs(x):
    return [x[:, b * LANES:(b + 1) * LANES] for b in range(x.shape[1] // LANES)]


def _rotate(x, c, s1, s2):
    return [xs * c + pltpu.roll(xs, LANES - _HALF, 1) * s1 + pltpu.roll(xs, _HALF, 1) * s2 for xs in _slabs(x)]


def _rotate_transposed(dy, c, s1, s2):
    return [ds * c + pltpu.roll(ds * s1, _HALF, 1) + pltpu.roll(ds * s2, LANES - _HALF, 1) for ds in _slabs(dy)]


def rope_fwd(qkv, tabs, q_width, kv_width, name):
    T, QKV = qkv.shape
    tm = _tile(T, 512)
    QK = q_width + kv_width
    scale = HEAD_DIM ** -0.5

    def body(x_ref, c_ref, s1_ref, s2_ref, o_ref):
        x = x_ref[:, :QK].astype(F32)
        slabs = _rotate(x, c_ref[...], s1_ref[...], s2_ref[...])
        for b, y in enumerate(slabs):
            if b * LANES < q_width:
                y = y * scale
            o_ref[:, b * LANES:(b + 1) * LANES] = y.astype(BF16)

    tab = _rows(tm, LANES)
    return pl.pallas_call(
        body, name=name, grid=(T // tm,), in_specs=[_rows(tm, QKV), tab, tab, tab], out_specs=_rows(tm, QK),
        out_shape=jax.ShapeDtypeStruct((T, QK), BF16), compiler_params=_params("parallel"))(qkv, *tabs)


def rope_bwd(dq, dk, dv, tabs, name):
    T, Q = dq.shape
    KV = dk.shape[1]
    tm = _tile(T, 512)
    scale = HEAD_DIM ** -0.5

    def body(dq_ref, dk_ref, dv_ref, c_ref, s1_ref, s2_ref, o_ref, b_ref):
        @pl.when(pl.program_id(0) == 0)
        def _():
            b_ref[...] = jnp.zeros_like(b_ref)
        tabs_v = (c_ref[...], s1_ref[...], s2_ref[...])
        pieces = [s * scale for s in _rotate_transposed(dq_ref[...], *tabs_v)]
        pieces += _rotate_transposed(dk_ref[...], *tabs_v)
        pieces += _slabs(dv_ref[...])
        for b, y in enumerate(pieces):
            cols = pl.ds(b * LANES, LANES)
            o_ref[:, cols] = y.astype(BF16)
            b_ref[:, cols] += jnp.sum(y, axis=0, keepdims=True)

    tab = _rows(tm, LANES)
    return pl.pallas_call(
        body, name=name, grid=(T // tm,), in_specs=[_rows(tm, Q), _rows(tm, KV), _rows(tm, KV), tab, tab, tab],
        out_specs=[_rows(tm, Q + 2 * KV), _full((1, Q + 2 * KV))],
        out_shape=[jax.ShapeDtypeStruct((T, Q + 2 * KV), BF16), jax.ShapeDtypeStruct((1, Q + 2 * KV), F32)],
        compiler_params=_params("arbitrary"))(dq, dk, dv, *tabs)


def _band_mask(n, heads=1):
    qi = lax.broadcasted_iota(jnp.int32, (heads * CHUNK, 2 * CHUNK), 0) % CHUNK
    sj = lax.broadcasted_iota(jnp.int32, (heads * CHUNK, 2 * CHUNK), 1)
    return (sj > qi) & (sj <= qi + CHUNK) & ((n > 0) | (sj >= CHUNK))


def _kv_head(kc_ref, kp_ref, vc_ref, vp_ref, kh):
    lanes = slice(kh * HEAD_DIM, (kh + 1) * HEAD_DIM)
    return (jnp.concatenate([kp_ref[:, lanes], kc_ref[:, lanes]], axis=0),
            jnp.concatenate([vp_ref[:, lanes], vc_ref[:, lanes]], axis=0))


def _head_probs(q, kk, valid, sink):
    s = jnp.where(valid, lax.dot_general(q, kk, _NT, preferred_element_type=F32), NEG_INF)
    m = jnp.maximum(jnp.max(s, axis=-1, keepdims=True), sink)
    p = jnp.exp(s - m)
    e_sink = jnp.exp(sink - m)
    return p, 1.0 / (jnp.sum(p, axis=-1, keepdims=True) + e_sink), e_sink


def _attn_specs(q_width, kv_width, order):
    qb = q_width // kv_width
    prev = lambda i: jnp.maximum(order(i) - 1, 0)
    return [pl.BlockSpec((CHUNK, q_width), lambda i: (order(i), 0)),
            pl.BlockSpec((CHUNK, kv_width), lambda i: (order(i), qb)),
            pl.BlockSpec((CHUNK, kv_width), lambda i: (prev(i), qb)),
            pl.BlockSpec((CHUNK, kv_width), lambda i: (order(i), qb + 1)),
            pl.BlockSpec((CHUNK, kv_width), lambda i: (prev(i), qb + 1))]


def attn_fwd(qk, qkv, sinks, q_width, kv_width, name):
    T = qk.shape[0]
    group = q_width // kv_width

    def body(q_ref, kc_ref, kp_ref, vc_ref, vp_ref, sink_ref, o_ref):
        valid = _band_mask(pl.program_id(0))
        for kh in range(kv_width // HEAD_DIM):
            kk, vv = _kv_head(kc_ref, kp_ref, vc_ref, vp_ref, kh)
            for g in range(group):
                h = kh * group + g
                lanes = slice(h * HEAD_DIM, (h + 1) * HEAD_DIM)
                p, inv, _ = _head_probs(q_ref[:, lanes], kk, valid, sink_ref[h])
                o_ref[:, lanes] = (jnp.dot(p.astype(BF16), vv, preferred_element_type=F32) * inv).astype(BF16)

    specs = _attn_specs(q_width, kv_width, lambda i: i)
    return pl.pallas_call(
        body, name=name, grid=(T // CHUNK,),
        in_specs=specs + [pl.BlockSpec(memory_space=pltpu.SMEM)],
        out_specs=_rows(CHUNK, q_width), out_shape=jax.ShapeDtypeStruct((T, q_width), BF16),
        compiler_params=_params("parallel"))(qk, qk, qk, qkv, qkv, sinks)


def attn_bwd(qk, qkv, do, sinks, q_width, kv_width, name):
    T = qk.shape[0]
    NB = T // CHUNK
    group = q_width // kv_width

    def body(q_ref, kc_ref, kp_ref, vc_ref, vp_ref, do_ref, sink_ref, dq_ref, dk_ref, dv_ref, ds_ref, ck_ref, cv_ref):
        i = pl.program_id(0)
        n = NB - 1 - i

        @pl.when(i == 0)
        def _():
            ck_ref[...] = jnp.zeros_like(ck_ref)
            cv_ref[...] = jnp.zeros_like(cv_ref)
            ds_ref[...] = jnp.zeros_like(ds_ref)
        lane = lax.broadcasted_iota(jnp.int32, (1, LANES), 1)
        dsink_row = jnp.zeros((1, LANES), F32)
        valid = _band_mask(n, group)
        head = lax.broadcasted_iota(jnp.int32, (group * CHUNK, 1), 0) // CHUNK
        for kh in range(kv_width // HEAD_DIM):
            kk, vv = _kv_head(kc_ref, kp_ref, vc_ref, vp_ref, kh)
            heads = [slice((kh * group + g) * HEAD_DIM, (kh * group + g + 1) * HEAD_DIM) for g in range(group)]
            q = jnp.concatenate([q_ref[:, hs] for hs in heads], axis=0)
            do = jnp.concatenate([do_ref[:, hs] for hs in heads], axis=0)
            sink = jnp.zeros((group * CHUNK, 1), F32)
            for g in range(group):
                sink = jnp.where(head == g, sink_ref[kh * group + g], sink)
            p, inv, e_sink = _head_probs(q, kk, valid, sink)
            p = p * inv
            dp = lax.dot_general(do, vv, _NT, preferred_element_type=F32)
            delta = jnp.sum(p * dp, axis=-1, keepdims=True)
            ds = (p * (dp - delta)).astype(BF16)
            dsink = -e_sink * inv * delta
            dq = jnp.dot(ds, kk, preferred_element_type=F32)
            for g, hs in enumerate(heads):
                rows = slice(g * CHUNK, (g + 1) * CHUNK)
                dsink_row = dsink_row + jnp.where(lane == kh * group + g, jnp.sum(dsink[rows]), 0.0)
                dq_ref[:, hs] = dq[rows]
            dkk = lax.dot_general(ds, q, _TN, preferred_element_type=F32)
            dvv = lax.dot_general(p.astype(BF16), do, _TN, preferred_element_type=F32)
            lanes = slice(kh * HEAD_DIM, (kh + 1) * HEAD_DIM)
            dk_ref[:, lanes] = dkk[CHUNK:] + ck_ref[:, lanes]
            dv_ref[:, lanes] = dvv[CHUNK:] + cv_ref[:, lanes]
            ck_ref[:, lanes] = dkk[:CHUNK]
            cv_ref[:, lanes] = dvv[:CHUNK]
        ds_ref[0:1, :] += dsink_row

    order = lambda i: NB - 1 - i
    specs = _attn_specs(q_width, kv_width, order)
    kv_out = pl.BlockSpec((CHUNK, kv_width), lambda i: (order(i), 0))
    q_rows = pl.BlockSpec((CHUNK, q_width), lambda i: (order(i), 0))
    return pl.pallas_call(
        body, name=name, grid=(NB,),
        in_specs=specs + [q_rows, pl.BlockSpec(memory_space=pltpu.SMEM)],
        out_specs=[q_rows, kv_out, kv_out, _full((8, LANES))],
        out_shape=[jax.ShapeDtypeStruct((T, q_width), F32), jax.ShapeDtypeStruct((T, kv_width), F32),
                   jax.ShapeDtypeStruct((T, kv_width), F32), jax.ShapeDtypeStruct((8, LANES), F32)],
        scratch_shapes=[pltpu.VMEM((CHUNK, kv_width), F32), pltpu.VMEM((CHUNK, kv_width), F32)],
        compiler_params=_params("arbitrary"))(qk, qk, qk, qkv, qkv, do, sinks)


def _blocked(w):
    return w.reshape(N_DEV, w.shape[0] // N_DEV, w.shape[1])


def kernel(x, positions, pre_mix_g, post_mix_g, pre_ffn_g, post_ffn_g, a_w_in, a_b_in, a_ln_g, a_ln_b, a_w_s, a_b_s, a_w_out, b_w_qkv, b_b_qkv, b_sinks, b_w_o, ffn_w_gu, ffn_w_down, loss_target, m_pre_mix_g, m_post_mix_g, m_pre_ffn_g, m_post_ffn_g, m_a_w_in, m_a_b_in, m_a_ln_g, m_a_ln_b, m_a_w_s, m_a_b_s, m_a_w_out, m_b_w_qkv, m_b_b_qkv, m_b_sinks, m_b_w_o, m_ffn_w_gu, m_ffn_w_down, v_pre_mix_g, v_post_mix_g, v_pre_ffn_g, v_post_ffn_g, v_a_w_in, v_a_b_in, v_a_ln_g, v_a_ln_b, v_a_w_s, v_a_b_s, v_a_w_out, v_b_w_qkv, v_b_b_qkv, v_b_sinks, v_b_w_o, v_ffn_w_gu, v_ffn_w_down):
    weights = dict(pre_mix_g=pre_mix_g, post_mix_g=post_mix_g, pre_ffn_g=pre_ffn_g, post_ffn_g=post_ffn_g,
                   a_w_in=a_w_in, a_b_in=a_b_in, a_ln_g=a_ln_g, a_ln_b=a_ln_b, a_w_s=a_w_s, a_b_s=a_b_s,
                   a_w_out=a_w_out, b_w_qkv=b_w_qkv, b_b_qkv=b_b_qkv, b_sinks=b_sinks, b_w_o=b_w_o,
                   ffn_w_gu=ffn_w_gu, ffn_w_down=ffn_w_down)
    mom_m = dict(pre_mix_g=m_pre_mix_g, post_mix_g=m_post_mix_g, pre_ffn_g=m_pre_ffn_g, post_ffn_g=m_post_ffn_g,
                 a_w_in=m_a_w_in, a_b_in=m_a_b_in, a_ln_g=m_a_ln_g, a_ln_b=m_a_ln_b, a_w_s=m_a_w_s, a_b_s=m_a_b_s,
                 a_w_out=m_a_w_out, b_w_qkv=m_b_w_qkv, b_b_qkv=m_b_b_qkv, b_sinks=m_b_sinks, b_w_o=m_b_w_o,
                 ffn_w_gu=m_ffn_w_gu, ffn_w_down=m_ffn_w_down)
    mom_v = dict(pre_mix_g=v_pre_mix_g, post_mix_g=v_post_mix_g, pre_ffn_g=v_pre_ffn_g, post_ffn_g=v_post_ffn_g,
                 a_w_in=v_a_w_in, a_b_in=v_a_b_in, a_ln_g=v_a_ln_g, a_ln_b=v_a_ln_b, a_w_s=v_a_w_s, a_b_s=v_a_b_s,
                 a_w_out=v_a_w_out, b_w_qkv=v_b_w_qkv, b_b_qkv=v_b_b_qkv, b_sinks=v_b_sinks, b_w_o=v_b_w_o,
                 ffn_w_gu=v_ffn_w_gu, ffn_w_down=v_ffn_w_down)
    names = list(weights)
    big = ("a_w_in", "b_w_qkv", "ffn_w_gu", "a_w_out", "b_w_o", "ffn_w_down")

    T, D = x.shape[1], x.shape[2]
    depth = pre_mix_g.shape[0]
    n_heads = b_sinks.shape[1]
    q_width = n_heads * HEAD_DIM
    kv_width = N_KV_HEADS * HEAD_DIM
    G = a_w_s.shape[1]
    W = a_ln_g.shape[1]
    device = _index(_position())

    dev = device.reshape(1).astype(jnp.int32)

    def layer_keys(i):
        mixer = ("a_w_in", "a_w_out") if i % 2 == 0 else ("b_w_qkv", "b_w_o")
        return [(k, i // 2) for k in mixer] + [("ffn_w_gu", i), ("ffn_w_down", i)]

    as_view = ("b_w_qkv", "ffn_w_gu")
    in_kernel = ("a_w_in",)
    view = lambda k, t: jnp.swapaxes(t, 1, 2) if k in as_view else t
    w_view = {k: view(k, weights[k]) for k in big}
    m_view = {k: view(k, mom_m[k]) for k in big}
    v_view = {k: view(k, mom_v[k]) for k in big}

    groups_keys = [[key] for key in layer_keys(0)] + [layer_keys(i) for i in range(1, depth)]
    group_of = {}
    for g, keys in enumerate(groups_keys):
        layer = 0 if g < len(layer_keys(0)) else g - len(layer_keys(0)) + 1
        for pos, (k, _) in enumerate(keys):
            group_of[(layer, k)] = (g, pos)
    bias_land = lax.dynamic_update_slice(
        jnp.zeros((N_DEV,) + b_b_qkv.shape, F32), b_b_qkv[None], (device, 0, 0))
    h, hn = x[0], first_norm(x[0], pre_mix_g[0][None], "norm_first")
    first = len(layer_keys(0))
    in_gather, token = [], ()
    for name, lo, hi in (("gather_start_first", 0, first), ("gather_start_rest", first, len(groups_keys))):
        groups = [list(prep_weights([(w_view[k], l, k in in_kernel) for k, l in keys], dev, f"prep_{name}_{g}",
                                    after=token)) for g, keys in enumerate(groups_keys[lo:hi])]
        if lo:
            groups[0].append(bias_land)
        gather_sems, flat_lands, token = relay_start(groups, name)
        token = (token, hn)
        for grp, sems in zip(groups, gather_sems):
            in_gather.append(([flat_lands.pop(0) for _ in grp], sems))
    all_started = token[0]
    passed, gathered = {}, {}

    def pass_on(g, after):
        if g not in passed and g < len(in_gather):
            lands, sems = in_gather[g]
            passed[g], lands = relay_pass(lands, sems, after, f"gather_pass_{g}")
            in_gather[g] = (lands, sems)

    pass_on(0, (all_started,))

    def weight(i, k, *after):
        g, pos = group_of[(i, k)]
        if g not in gathered:
            pass_on(g + 1, after)
            lands, sems = in_gather[g]
            gathered[g] = relay_wait(lands, sems, passed[g], after, f"gather_wait_{g}")
        w = gathered[g][pos]
        return w.reshape(N_DEV * w.shape[1], w.shape[2])

    causal = jnp.tril(jnp.ones((CHUNK, CHUNK), dtype=bool))
    w_mask = jnp.where(causal[None, None], a_w_s, 0.0).astype(BF16)
    w_mask_t = jnp.swapaxes(w_mask, 2, 3)
    b_s_t = jnp.swapaxes(a_b_s, 1, 2)
    group_onehot = (jnp.arange(W)[:, None] // (W // G) == jnp.arange(LANES)[None, :]).astype(BF16)
    lane = jnp.arange(LANES) % HEAD_DIM
    inv_freq = ROPE_THETA ** (-jnp.arange(0, ROPE_DIM, 2, dtype=F32) / ROPE_DIM)
    inv_freq_row = jnp.where(lane < ROPE_DIM, inv_freq[lane % (ROPE_DIM // 2)], 0.0)[None, :].astype(F32)
    tabs = rope_tables(positions.reshape(T, 1), inv_freq_row, "rope_tables")

    saved = []
    for i in range(depth):
        j = i // 2
        s = dict(h=h, hn=hn)
        if i % 2 == 0:
            s["gelu"], s["dgelu"] = gmlp_in(hn, weight(i, "a_w_in", hn, all_started), a_b_in[j][None], "gmlp_in")
            s["gated"] = gmlp_fwd(s["gelu"], a_ln_g[j][None], a_ln_b[j][None], w_mask[j], b_s_t[j], "gmlp_gate")
            mixed, w_mix = s["gated"], weight(i, "a_w_out", s["gated"])
        else:
            w_qkv = weight(i, "b_w_qkv", hn)
            b_qkv_full = jnp.swapaxes(gathered[group_of[(1, "b_w_qkv")][0]][-1], 0, 1).reshape(b_b_qkv.shape[0], 1, -1)
            s["qkv"] = rowmm(hn, w_qkv, True, b_qkv_full[j], BF16, "attn_qkv")
            s["qk"] = rope_fwd(s["qkv"], tabs, q_width, kv_width, "attn_rope")
            s["o"] = attn_fwd(s["qk"], s["qkv"], b_sinks[j], q_width, kv_width, "attn_core")
            mixed, w_mix = s["o"], weight(i, "b_w_o", s["o"])
        s["mix"], s["h2"], s["fn"] = mm_add_norm(mixed, w_mix, h, post_mix_g[i][None], pre_ffn_g[i][None], "mixer_out")
        s["act"], s["act_partials"] = ffn_up(s["fn"], weight(i, "ffn_w_gu", s["fn"]), "ffn_up")
        w_down = weight(i, "ffn_w_down", s["act"])
        saved.append(s)
        if i + 1 < depth:
            s["f"], h, hn = mm_add_norm(s["act"], w_down, s["h2"], post_ffn_g[i][None], pre_mix_g[i + 1][None], "ffn_down")
    small = {k: [None] * weights[k].shape[0] for k in names if k not in big}
    dh, df, loss_sum, small["post_ffn_g"][depth - 1] = mm_add_norm_loss(
        saved[-1]["act"], w_down, saved[-1]["h2"], post_ffn_g[depth - 1][None], loss_target[0], "ffn_down_loss")
    loss = lax.psum(loss_sum[0, 0] * (0.5 / D), ("x", "y", "c"))

    grads = {k: [None] * weights[k].shape[0] for k in big}
    in_flight = []

    def send_grads(keys, tag):
        sems, bufs, token = exchange_start([_blocked(grads[k][l]) for k, l in keys], "exchange_start_" + tag)
        in_flight.append((keys, sems, bufs, tag))
        return token

    for i in reversed(range(depth)):
        j = i // 2
        s = saved[i]
        dgu = ffn_dact(df, weight(i, "ffn_w_down"), s["act_partials"], "ffn_dact")
        grads["ffn_w_down"][i] = grad_mm(s["act"], df, "ffn_down_grad")
        grads["ffn_w_gu"][i] = grad_mm(dgu, s["fn"], "ffn_up_grad")
        sent = send_grads(layer_keys(i)[2:], f"ffn_{i}")
        r = mm_norm_bwd(dgu, weight(i, "ffn_w_gu"), dh, s["h2"], pre_ffn_g[i][None],
                        (s["mix"], post_mix_g[i][None]), "ffn_dx", sent)
        dh, dmix = r["dh"], r["dy"]
        small["pre_ffn_g"][i], small["post_mix_g"][i] = r["dg_pre"], r["dg_post"]
        if i % 2 == 0:
            grads["a_w_out"][j] = grad_mm(s["gated"], dmix, "gmlp_out_grad")
            dgated = rowmm(dmix, weight(i, "a_w_out"), True, None, BF16, "gmlp_dgated")
            dmixed, dws, dbs, dlg, dlb, dbin = gmlp_bwd(
                s["gelu"], s["dgelu"], dgated, a_ln_g[j][None], a_ln_b[j][None], w_mask[j], w_mask_t[j], b_s_t[j], group_onehot,
                "gmlp_gate_bwd")
            small["a_w_s"][j], small["a_b_s"][j] = dws, dbs[:, :G].T
            small["a_ln_g"][j], small["a_ln_b"][j], small["a_b_in"][j] = dlg[0], dlb[0], dbin[0]
            grads["a_w_in"][j] = grad_mm(dmixed, s["hn"], "gmlp_in_grad")
            w_in = weight(i, "a_w_in")
        else:
            grads["b_w_o"][j] = grad_mm(s["o"], dmix, "attn_out_grad")
            do = rowmm(dmix, weight(i, "b_w_o"), True, None, BF16, "attn_do")
            dq, dk, dv, dsink = attn_bwd(s["qk"], s["qkv"], do, b_sinks[j], q_width, kv_width, "attn_core_bwd")
            dmixed, dbias = rope_bwd(dq, dk, dv, tabs, "attn_rope_bwd")
            small["b_sinks"][j], small["b_b_qkv"][j] = dsink[0, :n_heads], dbias[0]
            grads["b_w_qkv"][j] = grad_mm(dmixed, s["hn"], "attn_qkv_grad")
            w_in = weight(i, "b_w_qkv")
        sent = send_grads(layer_keys(i)[:2], f"mixer_{i}")
        post = (saved[i - 1]["f"], post_ffn_g[i - 1][None]) if i > 0 else None
        r = mm_norm_bwd(dmixed, w_in, dh, s["h"], pre_mix_g[i][None], post, "mixer_dx", sent)
        dh, small["pre_mix_g"][i] = r["dh"], r["dg_pre"]
        if i > 0:
            df, small["post_ffn_g"][i - 1] = r["dy"], r["dg_post"]
    grad_x = dh[None]

    replicated = [k for k in names if k not in big and k != "b_b_qkv"]
    tile = LANES * LANES
    main = sum(weights[k].size for k in replicated)
    main_rows = -(-main // tile) * LANES
    bias_size = b_b_qkv.shape[0] * N_DEV * b_b_qkv.shape[1]
    bias_rows = -(-bias_size // (8 * LANES)) * 8
    flat = lambda vals: [v.reshape(-1) for v in vals]
    packed = jnp.concatenate(
        flat(p for k in replicated for p in small[k]) + [jnp.zeros((main_rows * LANES - main,), F32)]
        + flat(small["b_b_qkv"]) + [jnp.zeros((bias_rows * LANES - bias_size,), F32)]
    ).reshape(1, main_rows + bias_rows, LANES)
    small_sems, small_lands, small_sent = gather_start(
        [prep_weights([(packed, 0, False)], dev, "place_small_grads", F32)], "gather_start_small")
    pack_main = lambda src: jnp.concatenate(
        flat(src[k] for k in replicated) + [jnp.ones((main_rows * LANES - main,), F32)]).reshape(main_rows, LANES)
    packed_w, packed_m, packed_v = pack_main(weights), pack_main(mom_m), pack_main(mom_v)

    stacked = {k: [lax.empty(w_view[k].shape, F32) for _ in range(4)] for k in big}
    after = (small_sent, dh)
    for keys, sems, bufs, tag in in_flight:
        partials, landed = exchange_wait(bufs, sems, after, "exchange_wait_" + tag)
        for (k, l), part, land in zip(keys, partials, landed):
            stacked[k] = sum_adamw(part, land, w_view[k], m_view[k], v_view[k], stacked[k], l,
                                   k in in_kernel, dev, "adamw_" + k)
        after = tuple(stacked[k][0] for k, _ in keys)
    out_g, out_d, out_m, out_v = {}, {}, {}, {}
    for k in big:
        out_g[k], out_d[k], out_m[k], out_v[k] = [view(k, t) for t in stacked[k]]

    gathered_small = gather_wait(small_lands, small_sems[0], after + (packed_w, packed_m, packed_v),
                                 "gather_wait_small")[0]
    results = sum_adamw_small(gathered_small, packed_w, packed_m, packed_v, "adamw_small")
    off = 0
    for k in replicated:
        size, shape = weights[k].size, weights[k].shape
        out_g[k], out_d[k], out_m[k], out_v[k] = [t.reshape(-1)[off:off + size].reshape(shape) for t in results]
        off += size
    n_local = b_b_qkv.shape[1]
    bias_sum = sum_parts(gathered_small[:, main_rows:], "sum_bias_grads").reshape(-1)[:bias_size]
    out_g["b_b_qkv"] = lax.dynamic_slice_in_dim(
        bias_sum.reshape(b_b_qkv.shape[0], -1), device * n_local, n_local, axis=1)
    pad = lambda t: jnp.concatenate([t.reshape(-1), jnp.ones((8 * LANES - t.size,), F32)]).reshape(8, LANES)
    bias_update = adamw(pad(b_b_qkv), pad(out_g["b_b_qkv"]), pad(m_b_b_qkv), pad(v_b_b_qkv), "adamw_bias")
    out_d["b_b_qkv"], out_m["b_b_qkv"], out_v["b_b_qkv"] = [
        t.reshape(-1)[:b_b_qkv.size].reshape(b_b_qkv.shape) for t in bias_update]

    return (loss, grad_x, *[out_g[k] for k in names], *[out_d[k] for k in names],
            *[out_m[k] for k in names], *[out_v[k] for k in names])
```

```python
import math

import jax
import jax.numpy as jnp
from jax import lax
from jax.experimental import pallas as pl
from jax.experimental.pallas import tpu as pltpu

F32, BF16 = jnp.float32, jnp.bfloat16
MESH = pl.DeviceIdType.MESH
N_DEV = 8

CHUNK = 128
HEAD_DIM = 64
N_KV_HEADS = 4
ROPE_DIM = HEAD_DIM // 4
ROPE_THETA = 500000.0
RMS_EPS = 1e-6
LN_EPS = 1e-5
NEG_INF = -1e30

ADAM_LR = 0.001
ADAM_B1 = 0.9
ADAM_B2 = 0.999
ADAM_EPS = 1e-08
ADAM_WD = 0.01
ADAM_STEP = 10

V7X_VMEM_BYTES = 64 * 2 ** 20
VMEM_LIMIT = V7X_VMEM_BYTES - 8 * 2 ** 20
LANES = 128


def _params(*sem):
    return pltpu.CompilerParams(dimension_semantics=sem or None, vmem_limit_bytes=VMEM_LIMIT)


def _tile(n, pref):
    if n <= pref:
        return n
    t = pref - pref % LANES
    while t >= LANES:
        if n % t == 0:
            return t
        t -= LANES
    return n


def _full(shape):
    return pl.BlockSpec(shape, lambda *_: (0,) * len(shape))


def _rows(tm, width):
    return pl.BlockSpec((tm, width), lambda i: (i, 0))


def _rms(x, g):
    r = lax.rsqrt(jnp.mean(x * x, axis=-1, keepdims=True) + RMS_EPS)
    return x * r * g


def _rms_bwd(x, g, dy):
    r = lax.rsqrt(jnp.mean(x * x, axis=-1, keepdims=True) + RMS_EPS)
    xhat = x * r
    dg = jnp.sum(dy * xhat, axis=0, keepdims=True)
    dxhat = dy * g
    dx = r * (dxhat - xhat * jnp.mean(dxhat * xhat, axis=-1, keepdims=True))
    return dx, dg


_INV_SQRT2 = 1.0 / math.sqrt(2.0)
_INV_SQRT2PI = 1.0 / math.sqrt(2.0 * math.pi)


def _gelu(x):
    return 0.5 * x * (1.0 + lax.erf(x * _INV_SQRT2))


def _gelu_and_grad(x):
    cdf = 0.5 * (1.0 + lax.erf(x * _INV_SQRT2))
    return x * cdf, cdf + x * jnp.exp(-0.5 * x * x) * _INV_SQRT2PI


def _sigmoid(x):
    return 0.5 * jnp.tanh(0.5 * x) + 0.5


def _position():
    return lax.axis_index("x"), lax.axis_index("y"), lax.axis_index("c")


def _index(p):
    return 4 * p[0] + 2 * p[1] + p[2]


_HBM = pl.BlockSpec(memory_space=pltpu.HBM)
_SEM = pl.BlockSpec(memory_space=pltpu.SEMAPHORE)
_ORDERED_BY_DATA = pltpu.SideEffectType.DATAFLOW_SIDE_EFFECTING


def _in_hbm(v):
    return pltpu.with_memory_space_constraint(v, pltpu.HBM)


def _peer(k):
    x, y, c = _position()
    return (x ^ ((k >> 2) & 1), y ^ ((k >> 1) & 1), c ^ (k & 1))


def _split_copies(srcs, dsts, send, recv):
    return [pltpu.make_async_remote_copy(
        src_ref=src(k), dst_ref=dst(k), send_sem=send.at[7 * a + k - 1], recv_sem=recv.at[7 * a + k - 1],
        device_id=_peer(k), device_id_type=MESH)
        for a, (src, dst) in enumerate(zip(srcs, dsts)) for k in range(1, N_DEV)]


def _start_call(groups, sent, copies_of, name, per_array=N_DEV - 1):
    flat = [v for grp in groups for v in grp]
    n, ng = len(flat), len(groups)

    def body(*refs):
        bufs, sems = refs[:n], refs[n:n + 2 * ng]
        q = 0
        for g, grp in enumerate(groups):
            for cp in copies_of(bufs[q:q + len(grp)], sems[2 * g], sems[2 * g + 1]):
                cp.start()
            q += len(grp)
        refs[-1][...] = jnp.zeros_like(refs[-1])

    sem_shapes = []
    for count in sent:
        sem_shapes += [pltpu.SemaphoreType.DMA((per_array * count,)) for _ in range(2)]
    res = pl.pallas_call(
        body, name=name,
        out_shape=sem_shapes + [pltpu.HBM(v.shape, v.dtype) for v in flat] + [jax.ShapeDtypeStruct((8, LANES), F32)],
        in_specs=[_HBM] * n,
        out_specs=[_SEM] * (2 * ng) + [_HBM] * n + [pl.BlockSpec(memory_space=pltpu.VMEM)],
        input_output_aliases={i: 2 * ng + i for i in range(n)},
        compiler_params=pltpu.CompilerParams(has_side_effects=_ORDERED_BY_DATA),
    )(*[_in_hbm(v) for v in flat])
    sems = [(res[2 * g], res[2 * g + 1]) for g in range(ng)]
    return sems, list(res[2 * ng:-1]), res[-1]


def _wait_call(bufs, sems, copies_of, after, name):
    n = len(bufs)

    def body(*refs):
        for cp in copies_of(refs[:n], refs[n], refs[n + 1]):
            cp.wait_send()
            cp.wait_recv()

    return list(pl.pallas_call(
        body, name=name,
        out_shape=[pltpu.HBM(v.shape, v.dtype) for v in bufs],
        in_specs=[_HBM] * n + [_SEM, _SEM] + [pl.BlockSpec(memory_space=pl.ANY)] * len(after),
        out_specs=[_HBM] * n,
        input_output_aliases={i: i for i in range(n)},
        compiler_params=pltpu.CompilerParams(has_side_effects=_ORDERED_BY_DATA),
    )(*bufs, sems[0], sems[1], *after))


def _gather_copies(lands, send, recv):
    me = _index(_position())
    mine = [lambda k, ref=ref: ref.at[pl.ds(me, 1)] for ref in lands]
    return _split_copies(mine, mine, send, recv)


def _gather_arrivals(lands, send, recv):
    me = _index(_position())
    mine = [lambda k, ref=ref: ref.at[pl.ds(me, 1)] for ref in lands]
    theirs = [lambda k, ref=ref: ref.at[pl.ds(_index(_peer(k)), 1)] for ref in lands]
    return _split_copies(mine, theirs, send, recv)


def _exchange_copies(bufs, send, recv):
    half = len(bufs) // 2
    srcs = [lambda k, ref=ref: ref.at[pl.ds(_index(_peer(k)), 1)] for ref in bufs[:half]]
    dsts = [lambda k, ref=ref: ref.at[pl.ds(k - 1, 1)] for ref in bufs[half:]]
    return _split_copies(srcs, dsts, send, recv)


def gather_start(groups, name):
    return _start_call(groups, [len(grp) for grp in groups], _gather_copies, name)


def gather_wait(lands, sems, after, name):
    return _wait_call(lands, sems, _gather_arrivals, after, name)


_DIRECT = (1, 2, 4, 6)
_PASSED = (3, 5, 7)


def _remote(src, dst, send, recv, q, k):
    return pltpu.make_async_remote_copy(src_ref=src, dst_ref=dst, send_sem=send.at[q], recv_sem=recv.at[q],
                                        device_id=_peer(k), device_id_type=MESH)


def _slot_of(land, k):
    return land.at[pl.ds(_index(_peer(k)), 1)]


def relay_start(groups, name):
    def copies(lands, send, recv):
        return [_remote(_slot_of(land, 0), _slot_of(land, 0), send, recv, len(_DIRECT) * a + j, k)
                for a, land in enumerate(lands) for j, k in enumerate(_DIRECT)]
    return _start_call(groups, [len(grp) for grp in groups], copies, name, per_array=len(_DIRECT))


def relay_pass(lands, sems, after, name):
    n = len(lands)

    def body(*refs):
        bufs, recv_first = refs[:n], refs[n]
        send, recv = refs[n + 1 + len(after):n + 3 + len(after)]
        for a, land in enumerate(bufs):
            for j, k in enumerate(_PASSED):
                came = _slot_of(land, k ^ 1)
                _remote(came, came, recv_first, recv_first, len(_DIRECT) * a + _DIRECT.index(k ^ 1), k ^ 1).wait_recv()
                _remote(came, came, send, recv, len(_PASSED) * a + j, 1).start()

    res = pl.pallas_call(
        body, name=name,
        out_shape=[pltpu.SemaphoreType.DMA((len(_PASSED) * n,))] * 2 + [pltpu.HBM(v.shape, v.dtype) for v in lands],
        in_specs=[_HBM] * n + [_SEM] + [pl.BlockSpec(memory_space=pl.ANY)] * len(after),
        out_specs=[_SEM, _SEM] + [_HBM] * n,
        input_output_aliases={i: 2 + i for i in range(n)},
        compiler_params=pltpu.CompilerParams(has_side_effects=_ORDERED_BY_DATA),
    )(*lands, sems[1], *after)
    return (res[0], res[1]), list(res[2:])


def relay_wait(lands, first, second, after, name):
    n = len(lands)

    def body(*refs):
        bufs = refs[:n]
        send_first, recv_first, send, recv = refs[n:n + 4]
        for a, land in enumerate(bufs):
            mine = _slot_of(land, 0)
            _remote(mine, _slot_of(land, 1), send_first, recv_first, len(_DIRECT) * a, 1).wait_recv()
            for j, k in enumerate(_DIRECT):
                _remote(mine, mine, send_first, recv_first, len(_DIRECT) * a + j, k).wait_send()
            for j, k in enumerate(_PASSED):
                cp = _remote(_slot_of(land, k ^ 1), _slot_of(land, k), send, recv, len(_PASSED) * a + j, 1)
                cp.wait_send()
                cp.wait_recv()

    return list(pl.pallas_call(
        body, name=name,
        out_shape=[pltpu.HBM(v.shape, v.dtype) for v in lands],
        in_specs=[_HBM] * n + [_SEM] * 4 + [pl.BlockSpec(memory_space=pl.ANY)] * len(after),
        out_specs=[_HBM] * n,
        input_output_aliases={i: i for i in range(n)},
        compiler_params=pltpu.CompilerParams(has_side_effects=_ORDERED_BY_DATA),
    )(*lands, first[0], first[1], second[0], second[1], *after))


def exchange_start(partials, name):
    lands = [lax.empty((N_DEV - 1,) + p.shape[1:], p.dtype) for p in partials]
    sems, bufs, token = _start_call([list(partials) + lands], [len(partials)], _exchange_copies, name)
    return sems[0], bufs, token


def exchange_wait(bufs, sems, after, name):
    bufs = _wait_call(bufs, sems, _exchange_copies, after, name)
    return bufs[:len(bufs) // 2], bufs[len(bufs) // 2:]


def prep_weights(items, dev, name, dtype=BF16, after=()):
    n = len(items)
    blocks = [(1, w.shape[2], w.shape[1]) if t else (1,) + w.shape[1:] for w, _, t in items]

    def body(d_ref, *refs):
        for (_, _, transpose), w_ref, o_ref in zip(items, refs[:n], refs[-n:]):
            v = w_ref[0]
            o_ref[0] = (v.T if transpose else v).astype(dtype)

    return pl.pallas_call(
        body, name=name,
        grid_spec=pltpu.PrefetchScalarGridSpec(
            num_scalar_prefetch=1, grid=(1,),
            in_specs=[pl.BlockSpec((1,) + w.shape[1:], lambda i, d, layer=layer: (layer, 0, 0)) for w, layer, _ in items]
            + [pl.BlockSpec(memory_space=pl.ANY)] * len(after),
            out_specs=[pl.BlockSpec(blk, lambda i, d: (d[0], 0, 0)) for blk in blocks]),
        out_shape=[jax.ShapeDtypeStruct((N_DEV,) + blk[1:], dtype) for blk in blocks],
        compiler_params=_params("arbitrary"),
    )(dev, *[w for w, _, _ in items], *after)


def _adamw_math(w, g, m, v):
    mn = ADAM_B1 * m + (1.0 - ADAM_B1) * g
    vn = ADAM_B2 * v + (1.0 - ADAM_B2) * (g * g)
    m_hat = mn * (1.0 / (1.0 - ADAM_B1 ** ADAM_STEP))
    v_hat = vn * (1.0 / (1.0 - ADAM_B2 ** ADAM_STEP))
    return -ADAM_LR * (m_hat / (jnp.sqrt(v_hat) + ADAM_EPS) + ADAM_WD * w), mn, vn


def sum_adamw(partial, landed, w, m, v, prev, layer, transpose, dev, name):
    _, r, c = partial.shape
    if transpose:
        tc = _tile(c, 256)
        grid = (c // tc,)
        part_spec = pl.BlockSpec((1, r, tc), lambda i, d: (d[0], 0, i))
        land_spec = pl.BlockSpec((N_DEV - 1, r, tc), lambda i, d: (0, 0, i))
        w_spec = pl.BlockSpec((1, tc, r), lambda i, d: (layer, i, 0))
    else:
        tr = r // 2 if r % 32 == 0 else r
        grid = (r // tr,)
        part_spec = pl.BlockSpec((1, tr, c), lambda i, d: (d[0], i, 0))
        land_spec = pl.BlockSpec((N_DEV - 1, tr, c), lambda i, d: (0, i, 0))
        w_spec = pl.BlockSpec((1, tr, c), lambda i, d: (layer, i, 0))

    def body(d_ref, p_ref, l_ref, w_ref, m_ref, v_ref, *rest):
        g_ref, dl_ref, mo_ref, vo_ref = rest[-4:]
        acc = p_ref[0].astype(F32)
        for k in range(N_DEV - 1):
            acc = acc + l_ref[k].astype(F32)
        g = acc.T if transpose else acc
        delta, mn, vn = _adamw_math(w_ref[0], g, m_ref[0], v_ref[0])
        g_ref[0], dl_ref[0], mo_ref[0], vo_ref[0] = g, delta, mn, vn

    return pl.pallas_call(
        body, name=name,
        grid_spec=pltpu.PrefetchScalarGridSpec(
            num_scalar_prefetch=1, grid=grid,
            in_specs=[part_spec, land_spec, w_spec, w_spec, w_spec] + [pl.BlockSpec(memory_space=pl.ANY)] * 4,
            out_specs=[w_spec] * 4),
        out_shape=[jax.ShapeDtypeStruct(w.shape, F32)] * 4,
        input_output_aliases={6 + q: q for q in range(4)},
        compiler_params=_params("parallel"),
    )(dev, partial, landed, w, m, v, *prev)


def sum_parts(parts, name):
    P, r, c = parts.shape

    def body(p_ref, o_ref):
        acc = p_ref[0].astype(F32)
        for s in range(1, P):
            acc = acc + p_ref[s].astype(F32)
        o_ref[...] = acc

    return pl.pallas_call(
        body, name=name, in_specs=[_full((P, r, c))], out_specs=_full((r, c)), grid=(1,),
        out_shape=jax.ShapeDtypeStruct((r, c), F32), compiler_params=_params("arbitrary"))(parts)


def adamw(w, g, m, v, name):
    R, C = w.shape
    tr = _tile(R, 512)

    def body(w_ref, g_ref, m_ref, v_ref, d_ref, mo_ref, vo_ref):
        d_ref[...], mo_ref[...], vo_ref[...] = _adamw_math(w_ref[...], g_ref[...], m_ref[...], v_ref[...])

    spec = pl.BlockSpec((tr, C), lambda i: (i, 0))
    return pl.pallas_call(
        body, name=name, grid=(R // tr,),
        in_specs=[spec] * 4, out_specs=[spec] * 3,
        out_shape=[jax.ShapeDtypeStruct((R, C), F32)] * 3,
        compiler_params=_params("parallel"),
    )(w, g, m, v)


def sum_adamw_small(gathered, w, m, v, name):
    rows = w.shape[0]
    tr = _tile(rows, 512)

    def body(p_ref, w_ref, m_ref, v_ref, g_ref, d_ref, mo_ref, vo_ref):
        g = p_ref[0]
        for s in range(1, N_DEV):
            g = g + p_ref[s]
        g_ref[...] = g
        d_ref[...], mo_ref[...], vo_ref[...] = _adamw_math(w_ref[...], g, m_ref[...], v_ref[...])

    spec = pl.BlockSpec((tr, LANES), lambda i: (i, 0))
    return pl.pallas_call(
        body, name=name, grid=(rows // tr,),
        in_specs=[pl.BlockSpec((N_DEV, tr, LANES), lambda i: (0, i, 0)), spec, spec, spec], out_specs=[spec] * 4,
        out_shape=[jax.ShapeDtypeStruct((rows, LANES), F32)] * 4, compiler_params=_params("parallel"))(gathered, w, m, v)


def first_norm(h, g, name):
    T, D = h.shape
    tm = _tile(T, 512)

    def body(h_ref, g_ref, hn_ref):
        hn_ref[...] = _rms(h_ref[...], g_ref[...]).astype(BF16)

    row = _rows(tm, D)
    return pl.pallas_call(
        body, name=name, grid=(T // tm,), in_specs=[row, _full((1, D))], out_specs=row,
        out_shape=jax.ShapeDtypeStruct((T, D), BF16), compiler_params=_params("parallel"))(h, g)


_NT = (((1,), (1,)), ((), ()))
_TN = (((0,), (0,)), ((), ()))


def rowmm(a, w, transposed_w, bias, out_dtype, name, after=None):
    M, K = a.shape
    N = w.shape[0] if transposed_w else w.shape[1]
    tm = _tile(M, 256)
    tn = _tile(N, 512)

    def body(*refs):
        a_ref, w_ref = refs[:2]
        o_ref = refs[-1]
        av = a_ref[...]
        for j in range(N // tn):
            cols = pl.ds(j * tn, tn)
            if transposed_w:
                acc = lax.dot_general(av, w_ref[cols, :], _NT, preferred_element_type=F32)
            else:
                acc = jnp.dot(av, w_ref[:, cols], preferred_element_type=F32)
            if bias is not None:
                acc = acc + refs[2][:, cols]
            o_ref[:, cols] = acc.astype(out_dtype)

    ins, in_specs = [a, w], [_rows(tm, K), _full(w.shape)]
    if bias is not None:
        ins.append(bias)
        in_specs.append(_full((1, N)))
    if after is not None:
        ins.append(after)
        in_specs.append(pl.BlockSpec(memory_space=pl.ANY))
    return pl.pallas_call(
        body, name=name, grid=(M // tm,), in_specs=in_specs, out_specs=_rows(tm, N),
        out_shape=jax.ShapeDtypeStruct((M, N), out_dtype), compiler_params=_params("parallel"))(*ins)


def _rows_times_weight(a_ref, w_ref, acc_ref):
    N = w_ref.shape[1]
    tn = _tile(N, 512)
    av = a_ref[...]
    for j in range(N // tn):
        cols = pl.ds(j * tn, tn)
        acc_ref[:, cols] = jnp.dot(av, w_ref[:, cols], preferred_element_type=F32)


def mm_add_norm(a, w, h, g_post, g_pre, name):
    T, K = a.shape
    D = w.shape[1]
    tm = _tile(T, 256)

    def body(a_ref, w_ref, h_ref, gp_ref, g_ref, y_ref, ho_ref, hn_ref, acc_ref):
        _rows_times_weight(a_ref, w_ref, acc_ref)
        y = acc_ref[...]
        y_ref[...] = y.astype(BF16)
        hv = h_ref[...] + _rms(y, gp_ref[...])
        ho_ref[...] = hv
        hn_ref[...] = _rms(hv, g_ref[...]).astype(BF16)

    row, vec = _rows(tm, D), _full((1, D))
    return pl.pallas_call(
        body, name=name, grid=(T // tm,), in_specs=[_rows(tm, K), _full(w.shape), row, vec, vec],
        out_specs=[row, row, row],
        out_shape=[jax.ShapeDtypeStruct((T, D), BF16), jax.ShapeDtypeStruct((T, D), F32),
                   jax.ShapeDtypeStruct((T, D), BF16)],
        scratch_shapes=[pltpu.VMEM((tm, D), F32)], compiler_params=_params("parallel"))(a, w, h, g_post, g_pre)


def mm_add_norm_loss(a, w, h, g_post, target, name):
    T, K = a.shape
    D = w.shape[1]
    tm = _tile(T, 256)

    def body(a_ref, w_ref, h_ref, gp_ref, t_ref, dh_ref, dy_ref, loss_ref, dgp_ref, acc_ref):
        @pl.when(pl.program_id(0) == 0)
        def _():
            loss_ref[...] = jnp.zeros_like(loss_ref)
            dgp_ref[...] = jnp.zeros_like(dgp_ref)
        _rows_times_weight(a_ref, w_ref, acc_ref)
        y = acc_ref[...]
        err = h_ref[...] + _rms(y, gp_ref[...]) - t_ref[...]
        dh = err * (1.0 / D)
        dh_ref[...] = dh
        loss_ref[...] += jnp.sum(err * err)
        dy, dgp = _rms_bwd(y, gp_ref[...], dh)
        dy_ref[...] = dy.astype(BF16)
        dgp_ref[...] += dgp

    row, vec = _rows(tm, D), _full((1, D))
    return pl.pallas_call(
        body, name=name, grid=(T // tm,), in_specs=[_rows(tm, K), _full(w.shape), row, vec, row],
        out_specs=[row, row, _full((8, LANES)), vec],
        out_shape=[jax.ShapeDtypeStruct((T, D), F32), jax.ShapeDtypeStruct((T, D), BF16),
                   jax.ShapeDtypeStruct((8, LANES), F32), jax.ShapeDtypeStruct((1, D), F32)],
        scratch_shapes=[pltpu.VMEM((tm, D), F32)], compiler_params=_params("arbitrary"))(a, w, h, g_post, target)


def mm_norm_bwd(a, w, dh_in, x, g_pre, post, name, after):
    T, K = a.shape
    D = w.shape[1]
    tm = _tile(T, 256)
    row, vec = _rows(tm, D), _full((1, D))
    ins, in_specs = [a, w, dh_in, x, g_pre], [_rows(tm, K), _full(w.shape), row, row, vec]
    outs = [jax.ShapeDtypeStruct((T, D), F32), jax.ShapeDtypeStruct((1, D), F32)]
    out_specs, names = [row, vec], ["dh", "dg_pre"]
    if post is not None:
        ins += list(post)
        in_specs += [row, vec]
        outs += [jax.ShapeDtypeStruct((T, D), BF16), jax.ShapeDtypeStruct((1, D), F32)]
        out_specs += [row, vec]
        names += ["dy", "dg_post"]
    n_in = len(ins) + 1

    def body(*refs):
        a_ref, w_ref, dh_ref_in, x_ref, g_ref = refs[:5]
        out_refs, acc_ref = refs[n_in:-1], refs[-1]
        first = pl.program_id(0) == 0
        _rows_times_weight(a_ref, w_ref, acc_ref)
        dx, dg = _rms_bwd(x_ref[...], g_ref[...], acc_ref[...])
        dh = dh_ref_in[...] + dx
        out_refs[0][...] = dh

        @pl.when(first)
        def _():
            for r in out_refs[1::2]:
                r[...] = jnp.zeros_like(r)
        out_refs[1][...] += dg
        if post is not None:
            y_ref, gp_ref = refs[5:7]
            dy, dgp = _rms_bwd(y_ref[...].astype(F32), gp_ref[...], dh)
            out_refs[2][...] = dy.astype(BF16)
            out_refs[3][...] += dgp

    res = pl.pallas_call(
        body, name=name, grid=(T // tm,), in_specs=in_specs + [pl.BlockSpec(memory_space=pl.ANY)],
        out_specs=out_specs, out_shape=outs, scratch_shapes=[pltpu.VMEM((tm, D), F32)],
        compiler_params=_params("arbitrary"))(*ins, after)
    return dict(zip(names, res))


def grad_mm(a, b, name):
    T, N = a.shape
    K = b.shape[1]
    tn = _tile(N, 1408)
    tt = _tile(T, 1024)

    def body(a_ref, b_ref, o_ref, acc_ref):
        t = pl.program_id(1)

        @pl.when(t == 0)
        def _():
            acc_ref[...] = jnp.zeros_like(acc_ref)
        acc_ref[...] += lax.dot_general(a_ref[...], b_ref[...], _TN, preferred_element_type=F32)

        @pl.when(t == pl.num_programs(1) - 1)
        def _():
            o_ref[...] = acc_ref[...].astype(BF16)

    return pl.pallas_call(
        body, name=name, grid=(N // tn, T // tt),
        in_specs=[pl.BlockSpec((tt, tn), lambda j, t: (t, j)), pl.BlockSpec((tt, K), lambda j, t: (t, 0))],
        out_specs=pl.BlockSpec((tn, K), lambda j, t: (j, 0)),
        out_shape=jax.ShapeDtypeStruct((N, K), BF16),
        scratch_shapes=[pltpu.VMEM((tn, K), F32)],
        compiler_params=_params("parallel", "arbitrary"))(a, b)


def ffn_up(fn, w_gu_t, name):
    T, D = fn.shape
    F = w_gu_t.shape[0] // 2
    tm = _tile(T, 256)
    tn = _tile(F, 512)

    def body(a_ref, w_ref, act_ref, part_ref):
        av = a_ref[...]
        for j in range(F // tn):
            g = lax.dot_general(av, w_ref[pl.ds(j * tn, tn), :], _NT, preferred_element_type=F32)
            up = lax.dot_general(av, w_ref[pl.ds(F + j * tn, tn), :], _NT, preferred_element_type=F32)
            sg = _sigmoid(g)
            silu = g * sg
            act_ref[:, pl.ds(j * tn, tn)] = (silu * up).astype(BF16)
            part_ref[:, pl.ds(j * tn, tn)] = (up * (sg + silu * (1.0 - sg))).astype(BF16)
            part_ref[:, pl.ds(F + j * tn, tn)] = silu.astype(BF16)

    return pl.pallas_call(
        body, name=name, grid=(T // tm,), in_specs=[_rows(tm, D), _full(w_gu_t.shape)],
        out_specs=[_rows(tm, F), _rows(tm, 2 * F)],
        out_shape=[jax.ShapeDtypeStruct((T, F), BF16), jax.ShapeDtypeStruct((T, 2 * F), BF16)],
        compiler_params=_params("parallel"))(fn, w_gu_t)


def ffn_dact(df, w_down, partials, name, after=()):
    T, D = df.shape
    F = w_down.shape[0]
    tm = _tile(T, 256)
    tn = _tile(F, 512)

    def body(d_ref, w_ref, part_ref, *rest):
        o_ref = rest[-1]
        dv = d_ref[...]
        for j in range(F // tn):
            dact = lax.dot_general(dv, w_ref[pl.ds(j * tn, tn), :], _NT, preferred_element_type=F32)
            for cols in (pl.ds(j * tn, tn), pl.ds(F + j * tn, tn)):
                o_ref[:, cols] = (dact * part_ref[:, cols].astype(F32)).astype(BF16)

    return pl.pallas_call(
        body, name=name, grid=(T // tm,),
        in_specs=[_rows(tm, D), _full(w_down.shape), _rows(tm, 2 * F)] + [pl.BlockSpec(memory_space=pl.ANY)] * len(after),
        out_specs=_rows(tm, 2 * F), out_shape=jax.ShapeDtypeStruct((T, 2 * F), BF16),
        compiler_params=_params("parallel"))(df, w_down, partials, *after)


def _layernorm_stats(v):
    mu = jnp.mean(v, axis=-1, keepdims=True)
    cen = v - mu
    rstd = lax.rsqrt(jnp.mean(cen * cen, axis=-1, keepdims=True) + LN_EPS)
    return cen * rstd, rstd


GMLP_ROWS = 2 * CHUNK


def gmlp_gate_out(a, ln_g, ln_b, w_mask, b_s_t, w_out, h, g_post, g_pre, name):
    T, W2 = a.shape
    W = W2 // 2
    G = w_mask.shape[0]
    C = W // G
    D = w_out.shape[1]
    tm = min(T, GMLP_ROWS)

    def body(a_ref, lg_ref, lb_ref, w_ref, bs_ref, wo_ref, h_ref, gp_ref, g_ref,
             gated_ref, y_ref, ho_ref, hn_ref, acc_ref):
        for c in range(tm // CHUNK):
            rows = pl.ds(c * CHUNK, CHUNK)
            xhat, _ = _layernorm_stats(_gelu(a_ref[rows, W:].astype(F32)))
            vln = (xhat * lg_ref[...] + lb_ref[...]).astype(BF16)
            for g in range(G):
                cols = pl.ds(g * C, C)
                sv = jnp.dot(w_ref[g], vln[:, g * C:(g + 1) * C], preferred_element_type=F32) + bs_ref[:, g:g + 1]
                gated_ref[rows, cols] = (_gelu(a_ref[rows, cols].astype(F32)) * sv).astype(BF16)
        _rows_times_weight(gated_ref, wo_ref, acc_ref)
        y = acc_ref[...]
        y_ref[...] = y.astype(BF16)
        hv = h_ref[...] + _rms(y, gp_ref[...])
        ho_ref[...] = hv
        hn_ref[...] = _rms(hv, g_ref[...]).astype(BF16)

    row, vec = _rows(tm, D), _full((1, D))
    return pl.pallas_call(
        body, name=name, grid=(T // tm,),
        in_specs=[_rows(tm, W2), _full((1, W)), _full((1, W)), _full(w_mask.shape), _full(b_s_t.shape),
                  _full(w_out.shape), row, vec, vec],
        out_specs=[_rows(tm, W), row, row, row],
        out_shape=[jax.ShapeDtypeStruct((T, W), BF16), jax.ShapeDtypeStruct((T, D), BF16),
                   jax.ShapeDtypeStruct((T, D), F32), jax.ShapeDtypeStruct((T, D), BF16)],
        scratch_shapes=[pltpu.VMEM((tm, D), F32)],
        compiler_params=_params("parallel"))(a, ln_g, ln_b, w_mask, b_s_t, w_out, h, g_post, g_pre)


def gmlp_gate_bwd_dx(a, dmix, w_out, ln_g, ln_b, w_mask, w_mask_t, b_s_t, group_onehot, w_in_t,
                     dh_in, x, g_pre, post, name, after):
    T, W2 = a.shape
    W = W2 // 2
    G = w_mask.shape[0]
    C = W // G
    D = w_out.shape[1]
    tm = min(T, GMLP_ROWS)
    tn = _tile(W, 512)
    row, vec = _rows(tm, D), _full((1, D))
    ins = [a, dmix, w_out, ln_g, ln_b, w_mask, w_mask_t, b_s_t, group_onehot, w_in_t, dh_in, x, g_pre]
    in_specs = [_rows(tm, W2), row, _full(w_out.shape), _full((1, W)), _full((1, W)), _full(w_mask.shape),
                _full(w_mask_t.shape), _full(b_s_t.shape), _full(group_onehot.shape), _full(w_in_t.shape), row, row, vec]
    names = ["da", "dw_s", "db_s_t", "dln_g", "dln_b", "db_in", "dh", "dg_pre"]
    outs = [jax.ShapeDtypeStruct((T, W2), BF16), jax.ShapeDtypeStruct((G, CHUNK, CHUNK), F32),
            jax.ShapeDtypeStruct((CHUNK, LANES), F32), jax.ShapeDtypeStruct((1, W), F32),
            jax.ShapeDtypeStruct((1, W), F32), jax.ShapeDtypeStruct((1, W2), F32),
            jax.ShapeDtypeStruct((T, D), F32), jax.ShapeDtypeStruct((1, D), F32)]
    out_specs = [_rows(tm, W2), _full((G, CHUNK, CHUNK)), _full((CHUNK, LANES)), _full((1, W)), _full((1, W)),
                 _full((1, W2)), row, vec]
    if post is not None:
        ins += list(post)
        in_specs += [row, vec]
        names += ["dy", "dg_post"]
        outs += [jax.ShapeDtypeStruct((T, D), BF16), jax.ShapeDtypeStruct((1, D), F32)]
        out_specs += [row, vec]
    n_in = len(ins) + 1

    def body(*refs):
        (a_ref, dm_ref, wo_ref, lg_ref, lb_ref, w_ref, wt_ref, bs_ref, e_ref, wi_ref, dh_in_ref, x_ref, gpre_ref) = refs[:13]
        out_refs = refs[n_in:-3]
        da_ref, dws_ref, dbs_ref, dlg_ref, dlb_ref, dbin_ref, dh_ref, dgpre_ref = out_refs[:8]
        dgated_ref, dvln_ref, acc_ref = refs[-3:]

        @pl.when(pl.program_id(0) == 0)
        def _():
            for r in out_refs[1:6] + out_refs[7::2]:
                r[...] = jnp.zeros_like(r)
        dm = dm_ref[...]
        for j in range(W // tn):
            cols = pl.ds(j * tn, tn)
            dgated_ref[:, cols] = lax.dot_general(dm, wo_ref[cols, :], _NT, preferred_element_type=F32)
        causal = (lax.broadcasted_iota(jnp.int32, (CHUNK, CHUNK), 1)
                  <= lax.broadcasted_iota(jnp.int32, (CHUNK, CHUNK), 0))
        for c in range(tm // CHUNK):
            rows = pl.ds(c * CHUNK, CHUNK)
            gelu_v, gelu_grad_v = _gelu_and_grad(a_ref[rows, W:].astype(F32))
            xhat, rstd = _layernorm_stats(gelu_v)
            vln = (xhat * lg_ref[...] + lb_ref[...]).astype(BF16)
            for g in range(G):
                cols = pl.ds(g * C, C)
                vg = vln[:, g * C:(g + 1) * C]
                gelu_u, gelu_grad_u = _gelu_and_grad(a_ref[rows, cols].astype(F32))
                dgated = dgated_ref[rows, cols]
                dsv = (dgated * gelu_u).astype(BF16)
                dbs_ref[...] += jnp.dot(dsv, e_ref[cols, :], preferred_element_type=F32)
                sv = jnp.dot(w_ref[g], vg, preferred_element_type=F32) + bs_ref[:, g:g + 1]
                dau = dgated * sv * gelu_grad_u
                da_ref[rows, cols] = dau.astype(BF16)
                dbin_ref[:, cols] += jnp.sum(dau, axis=0, keepdims=True)
                dws_ref[g] += jnp.where(causal, lax.dot_general(dsv, vg, _NT, preferred_element_type=F32), 0.0)
                dvln_ref[:, cols] = jnp.dot(wt_ref[g], dsv, preferred_element_type=F32)
            dvln = dvln_ref[...]
            dlg_ref[...] += jnp.sum(dvln * xhat, axis=0, keepdims=True)
            dlb_ref[...] += jnp.sum(dvln, axis=0, keepdims=True)
            dxhat = dvln * lg_ref[...]
            dv = rstd * (dxhat - jnp.mean(dxhat, axis=-1, keepdims=True)
                         - xhat * jnp.mean(dxhat * xhat, axis=-1, keepdims=True))
            dav = dv * gelu_grad_v
            da_ref[rows, W:] = dav.astype(BF16)
            dbin_ref[:, W:] += jnp.sum(dav, axis=0, keepdims=True)
        _rows_times_weight(da_ref, wi_ref, acc_ref)
        dx, dg = _rms_bwd(x_ref[...], gpre_ref[...], acc_ref[...])
        dh = dh_in_ref[...] + dx
        dh_ref[...] = dh
        dgpre_ref[...] += dg
        if post is not None:
            y_ref, gp_ref = refs[13:15]
            dy, dgp = _rms_bwd(y_ref[...].astype(F32), gp_ref[...], dh)
            out_refs[8][...] = dy.astype(BF16)
            out_refs[9][...] += dgp

    res = pl.pallas_call(
        body, name=name, grid=(T // tm,), in_specs=in_specs + [pl.BlockSpec(memory_space=pl.ANY)],
        out_specs=out_specs, out_shape=outs,
        scratch_shapes=[pltpu.VMEM((tm, W), F32), pltpu.VMEM((CHUNK, W), F32), pltpu.VMEM((tm, D), F32)],
        compiler_params=_params("arbitrary"))(*ins, after)
    return dict(zip(names, res))


def rope_tables(pos, inv_freq_row, name):
    T = pos.shape[0]
    tm = _tile(T, 512)

    def body(p_ref, f_ref, c_ref, s1_ref, s2_ref):
        ang = p_ref[...].astype(F32) * f_ref[...]
        lane = lax.broadcasted_iota(jnp.int32, (tm, LANES), 1) % HEAD_DIM
        sin = jnp.sin(ang)
        c_ref[...] = jnp.cos(ang)
        s1_ref[...] = jnp.where(lane < _HALF, -sin, 0.0)
        s2_ref[...] = jnp.where((lane >= _HALF) & (lane < ROPE_DIM), sin, 0.0)

    tab = _rows(tm, LANES)
    return pl.pallas_call(
        body, name=name, grid=(T // tm,), in_specs=[_rows(tm, 1), _full((1, LANES))], out_specs=[tab] * 3,
        out_shape=[jax.ShapeDtypeStruct((T, LANES), F32)] * 3, compiler_params=_params("parallel"))(pos, inv_freq_row)


_HALF = ROPE_DIM // 2


def _slabs(x):
    return [x[:, b * LANES:(b + 1) * LANES] for b in range(x.shape[1] // LANES)]


def _rotate(x, c, s1, s2):
    return [xs * c + pltpu.roll(xs, LANES - _HALF, 1) * s1 + pltpu.roll(xs, _HALF, 1) * s2 for xs in _slabs(x)]


def _rotate_transposed(dy, c, s1, s2):
    return [ds * c + pltpu.roll(ds * s1, _HALF, 1) + pltpu.roll(ds * s2, LANES - _HALF, 1) for ds in _slabs(dy)]


def rope_fwd(qkv, tabs, q_width, kv_width, name):
    T, QKV = qkv.shape
    tm = _tile(T, 512)
    QK = q_width + kv_width
    scale = HEAD_DIM ** -0.5

    def body(x_ref, c_ref, s1_ref, s2_ref, o_ref):
        x = x_ref[:, :QK].astype(F32)
        slabs = _rotate(x, c_ref[...], s1_ref[...], s2_ref[...])
        for b, y in enumerate(slabs):
            if b * LANES < q_width:
                y = y * scale
            o_ref[:, b * LANES:(b + 1) * LANES] = y.astype(BF16)

    tab = _rows(tm, LANES)
    return pl.pallas_call(
        body, name=name, grid=(T // tm,), in_specs=[_rows(tm, QKV), tab, tab, tab], out_specs=_rows(tm, QK),
        out_shape=jax.ShapeDtypeStruct((T, QK), BF16), compiler_params=_params("parallel"))(qkv, *tabs)


def rope_bwd(dq, dk, dv, tabs, name):
    T, Q = dq.shape
    KV = dk.shape[1]
    tm = _tile(T, 512)
    scale = HEAD_DIM ** -0.5

    def body(dq_ref, dk_ref, dv_ref, c_ref, s1_ref, s2_ref, o_ref, b_ref):
        @pl.when(pl.program_id(0) == 0)
        def _():
            b_ref[...] = jnp.zeros_like(b_ref)
        tabs_v = (c_ref[...], s1_ref[...], s2_ref[...])
        pieces = [s * scale for s in _rotate_transposed(dq_ref[...], *tabs_v)]
        pieces += _rotate_transposed(dk_ref[...], *tabs_v)
        pieces += _slabs(dv_ref[...])
        for b, y in enumerate(pieces):
            cols = pl.ds(b * LANES, LANES)
            o_ref[:, cols] = y.astype(BF16)
            b_ref[:, cols] += jnp.sum(y, axis=0, keepdims=True)

    tab = _rows(tm, LANES)
    return pl.pallas_call(
        body, name=name, grid=(T // tm,), in_specs=[_rows(tm, Q), _rows(tm, KV), _rows(tm, KV), tab, tab, tab],
        out_specs=[_rows(tm, Q + 2 * KV), _full((1, Q + 2 * KV))],
        out_shape=[jax.ShapeDtypeStruct((T, Q + 2 * KV), BF16), jax.ShapeDtypeStruct((1, Q + 2 * KV), F32)],
        compiler_params=_params("arbitrary"))(dq, dk, dv, *tabs)


def _band_mask(n, heads=1):
    qi = lax.broadcasted_iota(jnp.int32, (heads * CHUNK, 2 * CHUNK), 0) % CHUNK
    sj = lax.broadcasted_iota(jnp.int32, (heads * CHUNK, 2 * CHUNK), 1)
    return (sj > qi) & (sj <= qi + CHUNK) & ((n > 0) | (sj >= CHUNK))


def _kv_head(kc_ref, kp_ref, vc_ref, vp_ref, kh):
    lanes = slice(kh * HEAD_DIM, (kh + 1) * HEAD_DIM)
    return (jnp.concatenate([kp_ref[:, lanes], kc_ref[:, lanes]], axis=0),
            jnp.concatenate([vp_ref[:, lanes], vc_ref[:, lanes]], axis=0))


def _head_probs(q, kk, valid, sink):
    s = jnp.where(valid, lax.dot_general(q, kk, _NT, preferred_element_type=F32), NEG_INF)
    m = jnp.maximum(jnp.max(s, axis=-1, keepdims=True), sink)
    p = jnp.exp(s - m)
    e_sink = jnp.exp(sink - m)
    return p, 1.0 / (jnp.sum(p, axis=-1, keepdims=True) + e_sink), e_sink


def _attn_specs(q_width, kv_width, order):
    qb = q_width // kv_width
    prev = lambda i: jnp.maximum(order(i) - 1, 0)
    return [pl.BlockSpec((CHUNK, q_width), lambda i: (order(i), 0)),
            pl.BlockSpec((CHUNK, kv_width), lambda i: (order(i), qb)),
            pl.BlockSpec((CHUNK, kv_width), lambda i: (prev(i), qb)),
            pl.BlockSpec((CHUNK, kv_width), lambda i: (order(i), qb + 1)),
            pl.BlockSpec((CHUNK, kv_width), lambda i: (prev(i), qb + 1))]


def attn_fwd(qk, qkv, sinks, q_width, kv_width, name):
    T = qk.shape[0]
    group = q_width // kv_width

    def body(q_ref, kc_ref, kp_ref, vc_ref, vp_ref, sink_ref, o_ref):
        valid = _band_mask(pl.program_id(0))
        for kh in range(kv_width // HEAD_DIM):
            kk, vv = _kv_head(kc_ref, kp_ref, vc_ref, vp_ref, kh)
            for g in range(group):
                h = kh * group + g
                lanes = slice(h * HEAD_DIM, (h + 1) * HEAD_DIM)
                p, inv, _ = _head_probs(q_ref[:, lanes], kk, valid, sink_ref[h])
                o_ref[:, lanes] = (jnp.dot(p.astype(BF16), vv, preferred_element_type=F32) * inv).astype(BF16)

    specs = _attn_specs(q_width, kv_width, lambda i: i)
    return pl.pallas_call(
        body, name=name, grid=(T // CHUNK,),
        in_specs=specs + [pl.BlockSpec(memory_space=pltpu.SMEM)],
        out_specs=_rows(CHUNK, q_width), out_shape=jax.ShapeDtypeStruct((T, q_width), BF16),
        compiler_params=_params("parallel"))(qk, qk, qk, qkv, qkv, sinks)


def attn_bwd(qk, qkv, do, sinks, q_width, kv_width, name):
    T = qk.shape[0]
    NB = T // CHUNK
    group = q_width // kv_width

    def body(q_ref, kc_ref, kp_ref, vc_ref, vp_ref, do_ref, sink_ref, dq_ref, dk_ref, dv_ref, ds_ref, ck_ref, cv_ref):
        i = pl.program_id(0)
        n = NB - 1 - i

        @pl.when(i == 0)
        def _():
            ck_ref[...] = jnp.zeros_like(ck_ref)
            cv_ref[...] = jnp.zeros_like(cv_ref)
            ds_ref[...] = jnp.zeros_like(ds_ref)
        lane = lax.broadcasted_iota(jnp.int32, (1, LANES), 1)
        dsink_row = jnp.zeros((1, LANES), F32)
        valid = _band_mask(n, group)
        head = lax.broadcasted_iota(jnp.int32, (group * CHUNK, 1), 0) // CHUNK
        for kh in range(kv_width // HEAD_DIM):
            kk, vv = _kv_head(kc_ref, kp_ref, vc_ref, vp_ref, kh)
            heads = [slice((kh * group + g) * HEAD_DIM, (kh * group + g + 1) * HEAD_DIM) for g in range(group)]
            q = jnp.concatenate([q_ref[:, hs] for hs in heads], axis=0)
            do = jnp.concatenate([do_ref[:, hs] for hs in heads], axis=0)
            sink = jnp.zeros((group * CHUNK, 1), F32)
            for g in range(group):
                sink = jnp.where(head == g, sink_ref[kh * group + g], sink)
            p, inv, e_sink = _head_probs(q, kk, valid, sink)
            p = p * inv
            dp = lax.dot_general(do, vv, _NT, preferred_element_type=F32)
            delta = jnp.sum(p * dp, axis=-1, keepdims=True)
            ds = (p * (dp - delta)).astype(BF16)
            dsink = -e_sink * inv * delta
            dq = jnp.dot(ds, kk, preferred_element_type=F32)
            for g, hs in enumerate(heads):
                rows = slice(g * CHUNK, (g + 1) * CHUNK)
                dsink_row = dsink_row + jnp.where(lane == kh * group + g, jnp.sum(dsink[rows]), 0.0)
                dq_ref[:, hs] = dq[rows]
            dkk = lax.dot_general(ds, q, _TN, preferred_element_type=F32)
            dvv = lax.dot_general(p.astype(BF16), do, _TN, preferred_element_type=F32)
            lanes = slice(kh * HEAD_DIM, (kh + 1) * HEAD_DIM)
            dk_ref[:, lanes] = dkk[CHUNK:] + ck_ref[:, lanes]
            dv_ref[:, lanes] = dvv[CHUNK:] + cv_ref[:, lanes]
            ck_ref[:, lanes] = dkk[:CHUNK]
            cv_ref[:, lanes] = dvv[:CHUNK]
        ds_ref[0:1, :] += dsink_row

    order = lambda i: NB - 1 - i
    specs = _attn_specs(q_width, kv_width, order)
    kv_out = pl.BlockSpec((CHUNK, kv_width), lambda i: (order(i), 0))
    q_rows = pl.BlockSpec((CHUNK, q_width), lambda i: (order(i), 0))
    return pl.pallas_call(
        body, name=name, grid=(NB,),
        in_specs=specs + [q_rows, pl.BlockSpec(memory_space=pltpu.SMEM)],
        out_specs=[q_rows, kv_out, kv_out, _full((8, LANES))],
        out_shape=[jax.ShapeDtypeStruct((T, q_width), F32), jax.ShapeDtypeStruct((T, kv_width), F32),
                   jax.ShapeDtypeStruct((T, kv_width), F32), jax.ShapeDtypeStruct((8, LANES), F32)],
        scratch_shapes=[pltpu.VMEM((CHUNK, kv_width), F32), pltpu.VMEM((CHUNK, kv_width), F32)],
        compiler_params=_params("arbitrary"))(qk, qk, qk, qkv, qkv, do, sinks)


def _blocked(w):
    return w.reshape(N_DEV, w.shape[0] // N_DEV, w.shape[1])


def kernel(x, positions, pre_mix_g, post_mix_g, pre_ffn_g, post_ffn_g, a_w_in, a_b_in, a_ln_g, a_ln_b, a_w_s, a_b_s, a_w_out, b_w_qkv, b_b_qkv, b_sinks, b_w_o, ffn_w_gu, ffn_w_down, loss_target, m_pre_mix_g, m_post_mix_g, m_pre_ffn_g, m_post_ffn_g, m_a_w_in, m_a_b_in, m_a_ln_g, m_a_ln_b, m_a_w_s, m_a_b_s, m_a_w_out, m_b_w_qkv, m_b_b_qkv, m_b_sinks, m_b_w_o, m_ffn_w_gu, m_ffn_w_down, v_pre_mix_g, v_post_mix_g, v_pre_ffn_g, v_post_ffn_g, v_a_w_in, v_a_b_in, v_a_ln_g, v_a_ln_b, v_a_w_s, v_a_b_s, v_a_w_out, v_b_w_qkv, v_b_b_qkv, v_b_sinks, v_b_w_o, v_ffn_w_gu, v_ffn_w_down):
    weights = dict(pre_mix_g=pre_mix_g, post_mix_g=post_mix_g, pre_ffn_g=pre_ffn_g, post_ffn_g=post_ffn_g,
                   a_w_in=a_w_in, a_b_in=a_b_in, a_ln_g=a_ln_g, a_ln_b=a_ln_b, a_w_s=a_w_s, a_b_s=a_b_s,
                   a_w_out=a_w_out, b_w_qkv=b_w_qkv, b_b_qkv=b_b_qkv, b_sinks=b_sinks, b_w_o=b_w_o,
                   ffn_w_gu=ffn_w_gu, ffn_w_down=ffn_w_down)
    mom_m = dict(pre_mix_g=m_pre_mix_g, post_mix_g=m_post_mix_g, pre_ffn_g=m_pre_ffn_g, post_ffn_g=m_post_ffn_g,
                 a_w_in=m_a_w_in, a_b_in=m_a_b_in, a_ln_g=m_a_ln_g, a_ln_b=m_a_ln_b, a_w_s=m_a_w_s, a_b_s=m_a_b_s,
                 a_w_out=m_a_w_out, b_w_qkv=m_b_w_qkv, b_b_qkv=m_b_b_qkv, b_sinks=m_b_sinks, b_w_o=m_b_w_o,
                 ffn_w_gu=m_ffn_w_gu, ffn_w_down=m_ffn_w_down)
    mom_v = dict(pre_mix_g=v_pre_mix_g, post_mix_g=v_post_mix_g, pre_ffn_g=v_pre_ffn_g, post_ffn_g=v_post_ffn_g,
                 a_w_in=v_a_w_in, a_b_in=v_a_b_in, a_ln_g=v_a_ln_g, a_ln_b=v_a_ln_b, a_w_s=v_a_w_s, a_b_s=v_a_b_s,
                 a_w_out=v_a_w_out, b_w_qkv=v_b_w_qkv, b_b_qkv=v_b_b_qkv, b_sinks=v_b_sinks, b_w_o=v_b_w_o,
                 ffn_w_gu=v_ffn_w_gu, ffn_w_down=v_ffn_w_down)
    names = list(weights)
    big = ("a_w_in", "b_w_qkv", "ffn_w_gu", "a_w_out", "b_w_o", "ffn_w_down")

    T, D = x.shape[1], x.shape[2]
    depth = pre_mix_g.shape[0]
    n_heads = b_sinks.shape[1]
    q_width = n_heads * HEAD_DIM
    kv_width = N_KV_HEADS * HEAD_DIM
    G = a_w_s.shape[1]
    W = a_ln_g.shape[1]
    device = _index(_position())

    dev = device.reshape(1).astype(jnp.int32)

    def layer_keys(i):
        mixer = ("a_w_in", "a_w_out") if i % 2 == 0 else ("b_w_qkv", "b_w_o")
        return [(k, i // 2) for k in mixer] + [("ffn_w_gu", i), ("ffn_w_down", i)]

    as_view = ("b_w_qkv", "ffn_w_gu")
    in_kernel = ("a_w_in",)
    view = lambda k, t: jnp.swapaxes(t, 1, 2) if k in as_view else t
    w_view = {k: view(k, weights[k]) for k in big}
    m_view = {k: view(k, mom_m[k]) for k in big}
    v_view = {k: view(k, mom_v[k]) for k in big}

    groups_keys = [[key] for key in layer_keys(0)] + [layer_keys(i) for i in range(1, depth)]
    group_of = {}
    for g, keys in enumerate(groups_keys):
        layer = 0 if g < len(layer_keys(0)) else g - len(layer_keys(0)) + 1
        for pos, (k, _) in enumerate(keys):
            group_of[(layer, k)] = (g, pos)
    bias_land = lax.dynamic_update_slice(
        jnp.zeros((N_DEV,) + b_b_qkv.shape, F32), b_b_qkv[None], (device, 0, 0))
    h, hn = x[0], first_norm(x[0], pre_mix_g[0][None], "norm_first")
    first = len(layer_keys(0))
    in_gather, token = [], ()
    for name, lo, hi in (("gather_start_first", 0, first), ("gather_start_rest", first, len(groups_keys))):
        groups = [list(prep_weights([(w_view[k], l, k in in_kernel) for k, l in keys], dev, f"prep_{name}_{g}",
                                    after=token)) for g, keys in enumerate(groups_keys[lo:hi])]
        if lo:
            groups[0].append(bias_land)
        gather_sems, flat_lands, token = relay_start(groups, name)
        token = (token, hn)
        for grp, sems in zip(groups, gather_sems):
            in_gather.append(([flat_lands.pop(0) for _ in grp], sems))
    all_started = token[0]
    passed, gathered = {}, {}

    def pass_on(g, after):
        if g not in passed and g < len(in_gather):
            lands, sems = in_gather[g]
            passed[g], lands = relay_pass(lands, sems, after, f"gather_pass_{g}")
            in_gather[g] = (lands, sems)

    pass_on(0, (all_started,))

    def weight(i, k, *after):
        g, pos = group_of[(i, k)]
        if g not in gathered:
            pass_on(g + 1, after)
            lands, sems = in_gather[g]
            gathered[g] = relay_wait(lands, sems, passed[g], after, f"gather_wait_{g}")
        w = gathered[g][pos]
        return w.reshape(N_DEV * w.shape[1], w.shape[2])

    causal = jnp.tril(jnp.ones((CHUNK, CHUNK), dtype=bool))
    w_mask = jnp.where(causal[None, None], a_w_s, 0.0).astype(BF16)
    w_mask_t = jnp.swapaxes(w_mask, 2, 3)
    b_s_t = jnp.swapaxes(a_b_s, 1, 2)
    group_onehot = (jnp.arange(W)[:, None] // (W // G) == jnp.arange(LANES)[None, :]).astype(BF16)
    lane = jnp.arange(LANES) % HEAD_DIM
    inv_freq = ROPE_THETA ** (-jnp.arange(0, ROPE_DIM, 2, dtype=F32) / ROPE_DIM)
    inv_freq_row = jnp.where(lane < ROPE_DIM, inv_freq[lane % (ROPE_DIM // 2)], 0.0)[None, :].astype(F32)
    tabs = rope_tables(positions.reshape(T, 1), inv_freq_row, "rope_tables")

    saved = []
    for i in range(depth):
        j = i // 2
        s = dict(h=h, hn=hn)
        post_pre = (post_mix_g[i][None], pre_ffn_g[i][None])
        if i % 2 == 0:
            s["a"] = rowmm(hn, weight(i, "a_w_in", hn, all_started), True, a_b_in[j][None], BF16, "gmlp_in")
            s["gated"], s["mix"], s["h2"], s["fn"] = gmlp_gate_out(
                s["a"], a_ln_g[j][None], a_ln_b[j][None], w_mask[j], b_s_t[j], weight(i, "a_w_out", s["a"]),
                h, *post_pre, "gmlp_gate_out")
        else:
            w_qkv = weight(i, "b_w_qkv", hn)
            b_qkv_full = jnp.swapaxes(gathered[group_of[(1, "b_w_qkv")][0]][-1], 0, 1).reshape(b_b_qkv.shape[0], 1, -1)
            s["qkv"] = rowmm(hn, w_qkv, True, b_qkv_full[j], BF16, "attn_qkv")
            s["qk"] = rope_fwd(s["qkv"], tabs, q_width, kv_width, "attn_rope")
            s["o"] = attn_fwd(s["qk"], s["qkv"], b_sinks[j], q_width, kv_width, "attn_core")
            s["mix"], s["h2"], s["fn"] = mm_add_norm(s["o"], weight(i, "b_w_o", s["o"]), h, *post_pre, "attn_out")
        s["act"], s["act_partials"] = ffn_up(s["fn"], weight(i, "ffn_w_gu", s["fn"]), "ffn_up")
        w_down = weight(i, "ffn_w_down", s["act"])
        saved.append(s)
        if i + 1 < depth:
            s["f"], h, hn = mm_add_norm(s["act"], w_down, s["h2"], post_ffn_g[i][None], pre_mix_g[i + 1][None], "ffn_down")
    small = {k: [None] * weights[k].shape[0] for k in names if k not in big}
    dh, df, loss_sum, small["post_ffn_g"][depth - 1] = mm_add_norm_loss(
        saved[-1]["act"], w_down, saved[-1]["h2"], post_ffn_g[depth - 1][None], loss_target[0], "ffn_down_loss")
    loss = lax.psum(loss_sum[0, 0] * (0.5 / D), ("x", "y", "c"))

    grads = {k: [None] * weights[k].shape[0] for k in big}
    in_flight = []

    def send_grads(keys, tag):
        sems, bufs, token = exchange_start([_blocked(grads[k][l]) for k, l in keys], "exchange_start_" + tag)
        in_flight.append((keys, sems, bufs, tag))
        return token

    sent = ()
    for i in reversed(range(depth)):
        j = i // 2
        s = saved[i]
        dgu = ffn_dact(df, weight(i, "ffn_w_down"), s["act_partials"], "ffn_dact", after=sent)
        grads["ffn_w_down"][i] = grad_mm(s["act"], df, "ffn_down_grad")
        grads["ffn_w_gu"][i] = grad_mm(dgu, s["fn"], "ffn_up_grad")
        sent = send_grads(layer_keys(i)[2:], f"ffn_{i}")
        r = mm_norm_bwd(dgu, weight(i, "ffn_w_gu"), dh, s["h2"], pre_ffn_g[i][None],
                        (s["mix"], post_mix_g[i][None]), "ffn_dx", sent)
        dh, dmix = r["dh"], r["dy"]
        small["pre_ffn_g"][i], small["post_mix_g"][i] = r["dg_pre"], r["dg_post"]
        post = (saved[i - 1]["f"], post_ffn_g[i - 1][None]) if i > 0 else None
        if i % 2 == 0:
            grads["a_w_out"][j] = grad_mm(s["gated"], dmix, "gmlp_out_grad")
            r = gmlp_gate_bwd_dx(
                s["a"], dmix, weight(i, "a_w_out"), a_ln_g[j][None], a_ln_b[j][None], w_mask[j], w_mask_t[j], b_s_t[j],
                group_onehot, weight(i, "a_w_in"), dh, s["h"], pre_mix_g[i][None], post, "gmlp_gate_bwd_dx", sent)
            small["a_w_s"][j], small["a_b_s"][j] = r["dw_s"], r["db_s_t"][:, :G].T
            small["a_ln_g"][j], small["a_ln_b"][j], small["a_b_in"][j] = r["dln_g"][0], r["dln_b"][0], r["db_in"][0]
            grads["a_w_in"][j] = grad_mm(r["da"], s["hn"], "gmlp_in_grad")
            sent = (send_grads(layer_keys(i)[:2], f"mixer_{i}"),)
        else:
            grads["b_w_o"][j] = grad_mm(s["o"], dmix, "attn_out_grad")
            do = rowmm(dmix, weight(i, "b_w_o"), True, None, BF16, "attn_do")
            dq, dk, dv, dsink = attn_bwd(s["qk"], s["qkv"], do, b_sinks[j], q_width, kv_width, "attn_core_bwd")
            dqkv, dbias = rope_bwd(dq, dk, dv, tabs, "attn_rope_bwd")
            small["b_sinks"][j], small["b_b_qkv"][j] = dsink[0, :n_heads], dbias[0]
            grads["b_w_qkv"][j] = grad_mm(dqkv, s["hn"], "attn_qkv_grad")
            sent = send_grads(layer_keys(i)[:2], f"mixer_{i}")
            r = mm_norm_bwd(dqkv, weight(i, "b_w_qkv"), dh, s["h"], pre_mix_g[i][None], post, "attn_dx", sent)
            sent = ()
        dh, small["pre_mix_g"][i] = r["dh"], r["dg_pre"]
        if i > 0:
            df, small["post_ffn_g"][i - 1] = r["dy"], r["dg_post"]
    grad_x = dh[None]

    replicated = [k for k in names if k not in big and k != "b_b_qkv"]
    tile = LANES * LANES
    main = sum(weights[k].size for k in replicated)
    main_rows = -(-main // tile) * LANES
    bias_size = b_b_qkv.shape[0] * N_DEV * b_b_qkv.shape[1]
    bias_rows = -(-bias_size // (8 * LANES)) * 8
    flat = lambda vals: [v.reshape(-1) for v in vals]
    packed = jnp.concatenate(
        flat(p for k in replicated for p in small[k]) + [jnp.zeros((main_rows * LANES - main,), F32)]
        + flat(small["b_b_qkv"]) + [jnp.zeros((bias_rows * LANES - bias_size,), F32)]
    ).reshape(1, main_rows + bias_rows, LANES)
    small_sems, small_lands, small_sent = gather_start(
        [prep_weights([(packed, 0, False)], dev, "place_small_grads", F32, after=sent)], "gather_start_small")
    pack_main = lambda src: jnp.concatenate(
        flat(src[k] for k in replicated) + [jnp.ones((main_rows * LANES - main,), F32)]).reshape(main_rows, LANES)
    packed_w, packed_m, packed_v = pack_main(weights), pack_main(mom_m), pack_main(mom_v)

    stacked = {k: [lax.empty(w_view[k].shape, F32) for _ in range(4)] for k in big}
    after = (small_sent, dh)
    for keys, sems, bufs, tag in in_flight:
        partials, landed = exchange_wait(bufs, sems, after, "exchange_wait_" + tag)
        for (k, l), part, land in zip(keys, partials, landed):
            stacked[k] = sum_adamw(part, land, w_view[k], m_view[k], v_view[k], stacked[k], l,
                                   k in in_kernel, dev, "adamw_" + k)
        after = tuple(stacked[k][0] for k, _ in keys)
    out_g, out_d, out_m, out_v = {}, {}, {}, {}
    for k in big:
        out_g[k], out_d[k], out_m[k], out_v[k] = [view(k, t) for t in stacked[k]]

    gathered_small = gather_wait(small_lands, small_sems[0], after + (packed_w, packed_m, packed_v),
                                 "gather_wait_small")[0]
    results = sum_adamw_small(gathered_small, packed_w, packed_m, packed_v, "adamw_small")
    off = 0
    for k in replicated:
        size, shape = weights[k].size, weights[k].shape
        out_g[k], out_d[k], out_m[k], out_v[k] = [t.reshape(-1)[off:off + size].reshape(shape) for t in results]
        off += size
    n_local = b_b_qkv.shape[1]
    bias_sum = sum_parts(gathered_small[:, main_rows:], "sum_bias_grads").reshape(-1)[:bias_size]
    out_g["b_b_qkv"] = lax.dynamic_slice_in_dim(
        bias_sum.reshape(b_b_qkv.shape[0], -1), device * n_local, n_local, axis=1)
    pad = lambda t: jnp.concatenate([t.reshape(-1), jnp.ones((8 * LANES - t.size,), F32)]).reshape(8, LANES)
    bias_update = adamw(pad(b_b_qkv), pad(out_g["b_b_qkv"]), pad(m_b_b_qkv), pad(v_b_b_qkv), "adamw_bias")
    out_d["b_b_qkv"], out_m["b_b_qkv"], out_v["b_b_qkv"] = [
        t.reshape(-1)[:b_b_qkv.size].reshape(b_b_qkv.shape) for t in bias_update]

    return (loss, grad_x, *[out_g[k] for k in names], *[out_d[k] for k in names],
            *[out_m[k] for k in names], *[out_v[k] for k in names])
```

```python
import math

import jax
import jax.numpy as jnp
from jax import lax
from jax.experimental import pallas as pl
from jax.experimental.pallas import tpu as pltpu

F32, BF16 = jnp.float32, jnp.bfloat16
MESH = pl.DeviceIdType.MESH
N_DEV = 8

CHUNK = 128
HEAD_DIM = 64
N_KV_HEADS = 4
ROPE_DIM = HEAD_DIM // 4
ROPE_THETA = 500000.0
RMS_EPS = 1e-6
LN_EPS = 1e-5
NEG_INF = -1e30

ADAM_LR = 0.001
ADAM_B1 = 0.9
ADAM_B2 = 0.999
ADAM_EPS = 1e-08
ADAM_WD = 0.01
ADAM_STEP = 10

V7X_VMEM_BYTES = 64 * 2 ** 20
VMEM_LIMIT = V7X_VMEM_BYTES - 8 * 2 ** 20
LANES = 128


def _params(*sem):
    return pltpu.CompilerParams(dimension_semantics=sem or None, vmem_limit_bytes=VMEM_LIMIT)


def _tile(n, pref):
    if n <= pref:
        return n
    t = pref - pref % LANES
    while t >= LANES:
        if n % t == 0:
            return t
        t -= LANES
    return n


def _full(shape):
    return pl.BlockSpec(shape, lambda *_: (0,) * len(shape))


def _rows(tm, width):
    return pl.BlockSpec((tm, width), lambda i: (i, 0))


def _rms(x, g):
    r = lax.rsqrt(jnp.mean(x * x, axis=-1, keepdims=True) + RMS_EPS)
    return x * r * g


def _rms_bwd(x, g, dy):
    r = lax.rsqrt(jnp.mean(x * x, axis=-1, keepdims=True) + RMS_EPS)
    xhat = x * r
    dg = jnp.sum(dy * xhat, axis=0, keepdims=True)
    dxhat = dy * g
    dx = r * (dxhat - xhat * jnp.mean(dxhat * xhat, axis=-1, keepdims=True))
    return dx, dg


_INV_SQRT2 = 1.0 / math.sqrt(2.0)
_INV_SQRT2PI = 1.0 / math.sqrt(2.0 * math.pi)


def _gelu(x):
    return 0.5 * x * (1.0 + lax.erf(x * _INV_SQRT2))


def _gelu_and_grad(x):
    cdf = 0.5 * (1.0 + lax.erf(x * _INV_SQRT2))
    return x * cdf, cdf + x * jnp.exp(-0.5 * x * x) * _INV_SQRT2PI


def _sigmoid(x):
    return 0.5 * jnp.tanh(0.5 * x) + 0.5


def _position():
    return lax.axis_index("x"), lax.axis_index("y"), lax.axis_index("c")


def _index(p):
    return 4 * p[0] + 2 * p[1] + p[2]


_HBM = pl.BlockSpec(memory_space=pltpu.HBM)
_SEM = pl.BlockSpec(memory_space=pltpu.SEMAPHORE)
_ORDERED_BY_DATA = pltpu.SideEffectType.DATAFLOW_SIDE_EFFECTING


def _in_hbm(v):
    return pltpu.with_memory_space_constraint(v, pltpu.HBM)


def _peer(k):
    x, y, c = _position()
    return (x ^ ((k >> 2) & 1), y ^ ((k >> 1) & 1), c ^ (k & 1))


def _split_copies(srcs, dsts, send, recv):
    return [pltpu.make_async_remote_copy(
        src_ref=src(k), dst_ref=dst(k), send_sem=send.at[7 * a + k - 1], recv_sem=recv.at[7 * a + k - 1],
        device_id=_peer(k), device_id_type=MESH)
        for a, (src, dst) in enumerate(zip(srcs, dsts)) for k in range(1, N_DEV)]


def _start_call(groups, sent, copies_of, name, per_array=N_DEV - 1):
    flat = [v for grp in groups for v in grp]
    n, ng = len(flat), len(groups)

    def body(*refs):
        bufs, sems = refs[:n], refs[n:n + 2 * ng]
        q = 0
        for g, grp in enumerate(groups):
            for cp in copies_of(bufs[q:q + len(grp)], sems[2 * g], sems[2 * g + 1]):
                cp.start()
            q += len(grp)
        refs[-1][...] = jnp.zeros_like(refs[-1])

    sem_shapes = []
    for count in sent:
        sem_shapes += [pltpu.SemaphoreType.DMA((per_array * count,)) for _ in range(2)]
    res = pl.pallas_call(
        body, name=name,
        out_shape=sem_shapes + [pltpu.HBM(v.shape, v.dtype) for v in flat] + [jax.ShapeDtypeStruct((8, LANES), F32)],
        in_specs=[_HBM] * n,
        out_specs=[_SEM] * (2 * ng) + [_HBM] * n + [pl.BlockSpec(memory_space=pltpu.VMEM)],
        input_output_aliases={i: 2 * ng + i for i in range(n)},
        compiler_params=pltpu.CompilerParams(has_side_effects=_ORDERED_BY_DATA),
    )(*[_in_hbm(v) for v in flat])
    sems = [(res[2 * g], res[2 * g + 1]) for g in range(ng)]
    return sems, list(res[2 * ng:-1]), res[-1]


def _wait_call(bufs, sems, copies_of, after, name):
    n = len(bufs)

    def body(*refs):
        for cp in copies_of(refs[:n], refs[n], refs[n + 1]):
            cp.wait_send()
            cp.wait_recv()

    return list(pl.pallas_call(
        body, name=name,
        out_shape=[pltpu.HBM(v.shape, v.dtype) for v in bufs],
        in_specs=[_HBM] * n + [_SEM, _SEM] + [pl.BlockSpec(memory_space=pl.ANY)] * len(after),
        out_specs=[_HBM] * n,
        input_output_aliases={i: i for i in range(n)},
        compiler_params=pltpu.CompilerParams(has_side_effects=_ORDERED_BY_DATA),
    )(*bufs, sems[0], sems[1], *after))


def _gather_copies(lands, send, recv):
    me = _index(_position())
    mine = [lambda k, ref=ref: ref.at[pl.ds(me, 1)] for ref in lands]
    return _split_copies(mine, mine, send, recv)


def _gather_arrivals(lands, send, recv):
    me = _index(_position())
    mine = [lambda k, ref=ref: ref.at[pl.ds(me, 1)] for ref in lands]
    theirs = [lambda k, ref=ref: ref.at[pl.ds(_index(_peer(k)), 1)] for ref in lands]
    return _split_copies(mine, theirs, send, recv)


def _exchange_copies(bufs, send, recv):
    half = len(bufs) // 2
    srcs = [lambda k, ref=ref: ref.at[pl.ds(_index(_peer(k)), 1)] for ref in bufs[:half]]
    dsts = [lambda k, ref=ref: ref.at[pl.ds(k - 1, 1)] for ref in bufs[half:]]
    return _split_copies(srcs, dsts, send, recv)


def gather_start(groups, name):
    return _start_call(groups, [len(grp) for grp in groups], _gather_copies, name)


def gather_wait(lands, sems, after, name):
    return _wait_call(lands, sems, _gather_arrivals, after, name)


_DIRECT = (1, 2, 4, 6)
_PASSED = (3, 5, 7)


def _remote(src, dst, send, recv, q, k):
    return pltpu.make_async_remote_copy(src_ref=src, dst_ref=dst, send_sem=send.at[q], recv_sem=recv.at[q],
                                        device_id=_peer(k), device_id_type=MESH)


def _slot_of(land, k):
    return land.at[pl.ds(_index(_peer(k)), 1)]


def relay_start(groups, name):
    def copies(lands, send, recv):
        return [_remote(_slot_of(land, 0), _slot_of(land, 0), send, recv, len(_DIRECT) * a + j, k)
                for a, land in enumerate(lands) for j, k in enumerate(_DIRECT)]
    return _start_call(groups, [len(grp) for grp in groups], copies, name, per_array=len(_DIRECT))


def relay_pass(lands, sems, after, name):
    n = len(lands)

    def body(*refs):
        bufs, recv_first = refs[:n], refs[n]
        send, recv = refs[n + 1 + len(after):n + 3 + len(after)]
        for a, land in enumerate(bufs):
            for j, k in enumerate(_PASSED):
                came = _slot_of(land, k ^ 1)
                _remote(came, came, recv_first, recv_first, len(_DIRECT) * a + _DIRECT.index(k ^ 1), k ^ 1).wait_recv()
                _remote(came, came, send, recv, len(_PASSED) * a + j, 1).start()

    res = pl.pallas_call(
        body, name=name,
        out_shape=[pltpu.SemaphoreType.DMA((len(_PASSED) * n,))] * 2 + [pltpu.HBM(v.shape, v.dtype) for v in lands],
        in_specs=[_HBM] * n + [_SEM] + [pl.BlockSpec(memory_space=pl.ANY)] * len(after),
        out_specs=[_SEM, _SEM] + [_HBM] * n,
        input_output_aliases={i: 2 + i for i in range(n)},
        compiler_params=pltpu.CompilerParams(has_side_effects=_ORDERED_BY_DATA),
    )(*lands, sems[1], *after)
    return (res[0], res[1]), list(res[2:])


def relay_wait(lands, first, second, after, name):
    n = len(lands)

    def body(*refs):
        bufs = refs[:n]
        send_first, recv_first, send, recv = refs[n:n + 4]
        for a, land in enumerate(bufs):
            mine = _slot_of(land, 0)
            _remote(mine, _slot_of(land, 1), send_first, recv_first, len(_DIRECT) * a, 1).wait_recv()
            for j, k in enumerate(_DIRECT):
                _remote(mine, mine, send_first, recv_first, len(_DIRECT) * a + j, k).wait_send()
            for j, k in enumerate(_PASSED):
                cp = _remote(_slot_of(land, k ^ 1), _slot_of(land, k), send, recv, len(_PASSED) * a + j, 1)
                cp.wait_send()
                cp.wait_recv()

    return list(pl.pallas_call(
        body, name=name,
        out_shape=[pltpu.HBM(v.shape, v.dtype) for v in lands],
        in_specs=[_HBM] * n + [_SEM] * 4 + [pl.BlockSpec(memory_space=pl.ANY)] * len(after),
        out_specs=[_HBM] * n,
        input_output_aliases={i: i for i in range(n)},
        compiler_params=pltpu.CompilerParams(has_side_effects=_ORDERED_BY_DATA),
    )(*lands, first[0], first[1], second[0], second[1], *after))


def exchange_start(partials, name):
    lands = [lax.empty((N_DEV - 1,) + p.shape[1:], p.dtype) for p in partials]
    sems, bufs, token = _start_call([list(partials) + lands], [len(partials)], _exchange_copies, name)
    return sems[0], bufs, token


def exchange_wait(bufs, sems, after, name):
    bufs = _wait_call(bufs, sems, _exchange_copies, after, name)
    return bufs[:len(bufs) // 2], bufs[len(bufs) // 2:]


def prep_weights(items, dev, name, dtype=BF16, after=()):
    n = len(items)
    blocks = [(1, w.shape[2], w.shape[1]) if t else (1,) + w.shape[1:] for w, _, t in items]

    def body(d_ref, *refs):
        for (_, _, transpose), w_ref, o_ref in zip(items, refs[:n], refs[-n:]):
            v = w_ref[0]
            o_ref[0] = (v.T if transpose else v).astype(dtype)

    return pl.pallas_call(
        body, name=name,
        grid_spec=pltpu.PrefetchScalarGridSpec(
            num_scalar_prefetch=1, grid=(1,),
            in_specs=[pl.BlockSpec((1,) + w.shape[1:], lambda i, d, layer=layer: (layer, 0, 0)) for w, layer, _ in items]
            + [pl.BlockSpec(memory_space=pl.ANY)] * len(after),
            out_specs=[pl.BlockSpec(blk, lambda i, d: (d[0], 0, 0)) for blk in blocks]),
        out_shape=[jax.ShapeDtypeStruct((N_DEV,) + blk[1:], dtype) for blk in blocks],
        compiler_params=_params("arbitrary"),
    )(dev, *[w for w, _, _ in items], *after)


def _adamw_math(w, g, m, v):
    mn = ADAM_B1 * m + (1.0 - ADAM_B1) * g
    vn = ADAM_B2 * v + (1.0 - ADAM_B2) * (g * g)
    m_hat = mn * (1.0 / (1.0 - ADAM_B1 ** ADAM_STEP))
    v_hat = vn * (1.0 / (1.0 - ADAM_B2 ** ADAM_STEP))
    return -ADAM_LR * (m_hat / (jnp.sqrt(v_hat) + ADAM_EPS) + ADAM_WD * w), mn, vn


def sum_adamw(partial, landed, w, m, v, prev, layer, transpose, dev, name):
    _, r, c = partial.shape
    if transpose:
        tc = _tile(c, 256)
        grid = (c // tc,)
        part_spec = pl.BlockSpec((1, r, tc), lambda i, d: (d[0], 0, i))
        land_spec = pl.BlockSpec((N_DEV - 1, r, tc), lambda i, d: (0, 0, i))
        w_spec = pl.BlockSpec((1, tc, r), lambda i, d: (layer, i, 0))
    else:
        tr = r // 2 if r % 32 == 0 else r
        grid = (r // tr,)
        part_spec = pl.BlockSpec((1, tr, c), lambda i, d: (d[0], i, 0))
        land_spec = pl.BlockSpec((N_DEV - 1, tr, c), lambda i, d: (0, i, 0))
        w_spec = pl.BlockSpec((1, tr, c), lambda i, d: (layer, i, 0))

    def body(d_ref, p_ref, l_ref, w_ref, m_ref, v_ref, *rest):
        g_ref, dl_ref, mo_ref, vo_ref = rest[-4:]
        acc = p_ref[0].astype(F32)
        for k in range(N_DEV - 1):
            acc = acc + l_ref[k].astype(F32)
        g = acc.T if transpose else acc
        delta, mn, vn = _adamw_math(w_ref[0], g, m_ref[0], v_ref[0])
        g_ref[0], dl_ref[0], mo_ref[0], vo_ref[0] = g, delta, mn, vn

    return pl.pallas_call(
        body, name=name,
        grid_spec=pltpu.PrefetchScalarGridSpec(
            num_scalar_prefetch=1, grid=grid,
            in_specs=[part_spec, land_spec, w_spec, w_spec, w_spec] + [pl.BlockSpec(memory_space=pl.ANY)] * 4,
            out_specs=[w_spec] * 4),
        out_shape=[jax.ShapeDtypeStruct(w.shape, F32)] * 4,
        input_output_aliases={6 + q: q for q in range(4)},
        compiler_params=_params("parallel"),
    )(dev, partial, landed, w, m, v, *prev)


def sum_parts(parts, name):
    P, r, c = parts.shape

    def body(p_ref, o_ref):
        acc = p_ref[0].astype(F32)
        for s in range(1, P):
            acc = acc + p_ref[s].astype(F32)
        o_ref[...] = acc

    return pl.pallas_call(
        body, name=name, in_specs=[_full((P, r, c))], out_specs=_full((r, c)), grid=(1,),
        out_shape=jax.ShapeDtypeStruct((r, c), F32), compiler_params=_params("arbitrary"))(parts)


def adamw(w, g, m, v, name):
    R, C = w.shape
    tr = _tile(R, 512)

    def body(w_ref, g_ref, m_ref, v_ref, d_ref, mo_ref, vo_ref):
        d_ref[...], mo_ref[...], vo_ref[...] = _adamw_math(w_ref[...], g_ref[...], m_ref[...], v_ref[...])

    spec = pl.BlockSpec((tr, C), lambda i: (i, 0))
    return pl.pallas_call(
        body, name=name, grid=(R // tr,),
        in_specs=[spec] * 4, out_specs=[spec] * 3,
        out_shape=[jax.ShapeDtypeStruct((R, C), F32)] * 3,
        compiler_params=_params("parallel"),
    )(w, g, m, v)


def sum_adamw_small(gathered, w, m, v, name):
    rows = w.shape[0]
    tr = _tile(rows, 512)

    def body(p_ref, w_ref, m_ref, v_ref, g_ref, d_ref, mo_ref, vo_ref):
        g = p_ref[0]
        for s in range(1, N_DEV):
            g = g + p_ref[s]
        g_ref[...] = g
        d_ref[...], mo_ref[...], vo_ref[...] = _adamw_math(w_ref[...], g, m_ref[...], v_ref[...])

    spec = pl.BlockSpec((tr, LANES), lambda i: (i, 0))
    return pl.pallas_call(
        body, name=name, grid=(rows // tr,),
        in_specs=[pl.BlockSpec((N_DEV, tr, LANES), lambda i: (0, i, 0)), spec, spec, spec], out_specs=[spec] * 4,
        out_shape=[jax.ShapeDtypeStruct((rows, LANES), F32)] * 4, compiler_params=_params("parallel"))(gathered, w, m, v)


def first_norm(h, g, name):
    T, D = h.shape
    tm = _tile(T, 512)

    def body(h_ref, g_ref, hn_ref):
        hn_ref[...] = _rms(h_ref[...], g_ref[...]).astype(BF16)

    row = _rows(tm, D)
    return pl.pallas_call(
        body, name=name, grid=(T // tm,), in_specs=[row, _full((1, D))], out_specs=row,
        out_shape=jax.ShapeDtypeStruct((T, D), BF16), compiler_params=_params("parallel"))(h, g)


_NT = (((1,), (1,)), ((), ()))
_TN = (((0,), (0,)), ((), ()))


def rowmm(a, w, transposed_w, bias, out_dtype, name, after=None):
    M, K = a.shape
    N = w.shape[0] if transposed_w else w.shape[1]
    tm = _tile(M, 256)
    tn = _tile(N, 512)

    def body(*refs):
        a_ref, w_ref = refs[:2]
        o_ref = refs[-1]
        av = a_ref[...]
        for j in range(N // tn):
            cols = pl.ds(j * tn, tn)
            if transposed_w:
                acc = lax.dot_general(av, w_ref[cols, :], _NT, preferred_element_type=F32)
            else:
                acc = jnp.dot(av, w_ref[:, cols], preferred_element_type=F32)
            if bias is not None:
                acc = acc + refs[2][:, cols]
            o_ref[:, cols] = acc.astype(out_dtype)

    ins, in_specs = [a, w], [_rows(tm, K), _full(w.shape)]
    if bias is not None:
        ins.append(bias)
        in_specs.append(_full((1, N)))
    if after is not None:
        ins.append(after)
        in_specs.append(pl.BlockSpec(memory_space=pl.ANY))
    return pl.pallas_call(
        body, name=name, grid=(M // tm,), in_specs=in_specs, out_specs=_rows(tm, N),
        out_shape=jax.ShapeDtypeStruct((M, N), out_dtype), compiler_params=_params("parallel"))(*ins)


def _rows_times_weight(a_ref, w_ref, acc_ref):
    N = w_ref.shape[1]
    tn = _tile(N, 512)
    av = a_ref[...]
    for j in range(N // tn):
        cols = pl.ds(j * tn, tn)
        acc_ref[:, cols] = jnp.dot(av, w_ref[:, cols], preferred_element_type=F32)


def mm_add_norm(a, w, h, g_post, g_pre, name):
    T, K = a.shape
    D = w.shape[1]
    tm = _tile(T, 256)

    def body(a_ref, w_ref, h_ref, gp_ref, g_ref, y_ref, ho_ref, hn_ref, acc_ref):
        _rows_times_weight(a_ref, w_ref, acc_ref)
        y = acc_ref[...]
        y_ref[...] = y.astype(BF16)
        hv = h_ref[...] + _rms(y, gp_ref[...])
        ho_ref[...] = hv
        hn_ref[...] = _rms(hv, g_ref[...]).astype(BF16)

    row, vec = _rows(tm, D), _full((1, D))
    return pl.pallas_call(
        body, name=name, grid=(T // tm,), in_specs=[_rows(tm, K), _full(w.shape), row, vec, vec],
        out_specs=[row, row, row],
        out_shape=[jax.ShapeDtypeStruct((T, D), BF16), jax.ShapeDtypeStruct((T, D), F32),
                   jax.ShapeDtypeStruct((T, D), BF16)],
        scratch_shapes=[pltpu.VMEM((tm, D), F32)], compiler_params=_params("parallel"))(a, w, h, g_post, g_pre)


def mm_add_norm_loss(a, w, h, g_post, target, name):
    T, K = a.shape
    D = w.shape[1]
    tm = _tile(T, 256)

    def body(a_ref, w_ref, h_ref, gp_ref, t_ref, dh_ref, dy_ref, loss_ref, dgp_ref, acc_ref):
        @pl.when(pl.program_id(0) == 0)
        def _():
            loss_ref[...] = jnp.zeros_like(loss_ref)
            dgp_ref[...] = jnp.zeros_like(dgp_ref)
        _rows_times_weight(a_ref, w_ref, acc_ref)
        y = acc_ref[...]
        err = h_ref[...] + _rms(y, gp_ref[...]) - t_ref[...]
        dh = err * (1.0 / D)
        dh_ref[...] = dh
        loss_ref[...] += jnp.sum(err * err)
        dy, dgp = _rms_bwd(y, gp_ref[...], dh)
        dy_ref[...] = dy.astype(BF16)
        dgp_ref[...] += dgp

    row, vec = _rows(tm, D), _full((1, D))
    return pl.pallas_call(
        body, name=name, grid=(T // tm,), in_specs=[_rows(tm, K), _full(w.shape), row, vec, row],
        out_specs=[row, row, _full((8, LANES)), vec],
        out_shape=[jax.ShapeDtypeStruct((T, D), F32), jax.ShapeDtypeStruct((T, D), BF16),
                   jax.ShapeDtypeStruct((8, LANES), F32), jax.ShapeDtypeStruct((1, D), F32)],
        scratch_shapes=[pltpu.VMEM((tm, D), F32)], compiler_params=_params("arbitrary"))(a, w, h, g_post, target)


def mm_norm_bwd(a, w, dh_in, x, g_pre, post, name, after):
    T, K = a.shape
    D = w.shape[1]
    tm = _tile(T, 256)
    row, vec = _rows(tm, D), _full((1, D))
    ins, in_specs = [a, w, dh_in, x, g_pre], [_rows(tm, K), _full(w.shape), row, row, vec]
    outs = [jax.ShapeDtypeStruct((T, D), F32), jax.ShapeDtypeStruct((1, D), F32)]
    out_specs, names = [row, vec], ["dh", "dg_pre"]
    if post is not None:
        ins += list(post)
        in_specs += [row, vec]
        outs += [jax.ShapeDtypeStruct((T, D), BF16), jax.ShapeDtypeStruct((1, D), F32)]
        out_specs += [row, vec]
        names += ["dy", "dg_post"]
    n_in = len(ins) + 1

    def body(*refs):
        a_ref, w_ref, dh_ref_in, x_ref, g_ref = refs[:5]
        out_refs, acc_ref = refs[n_in:-1], refs[-1]
        first = pl.program_id(0) == 0
        _rows_times_weight(a_ref, w_ref, acc_ref)
        dx, dg = _rms_bwd(x_ref[...], g_ref[...], acc_ref[...])
        dh = dh_ref_in[...] + dx
        out_refs[0][...] = dh

        @pl.when(first)
        def _():
            for r in out_refs[1::2]:
                r[...] = jnp.zeros_like(r)
        out_refs[1][...] += dg
        if post is not None:
            y_ref, gp_ref = refs[5:7]
            dy, dgp = _rms_bwd(y_ref[...].astype(F32), gp_ref[...], dh)
            out_refs[2][...] = dy.astype(BF16)
            out_refs[3][...] += dgp

    res = pl.pallas_call(
        body, name=name, grid=(T // tm,), in_specs=in_specs + [pl.BlockSpec(memory_space=pl.ANY)],
        out_specs=out_specs, out_shape=outs, scratch_shapes=[pltpu.VMEM((tm, D), F32)],
        compiler_params=_params("arbitrary"))(*ins, after)
    return dict(zip(names, res))


def grad_mm(a, b, name, after=()):
    T, N = a.shape
    K = b.shape[1]
    tn = _tile(N, 1408)
    tt = _tile(T, 1024)

    def body(a_ref, b_ref, *rest):
        o_ref, acc_ref = rest[-2:]
        t = pl.program_id(1)

        @pl.when(t == 0)
        def _():
            acc_ref[...] = jnp.zeros_like(acc_ref)
        acc_ref[...] += lax.dot_general(a_ref[...], b_ref[...], _TN, preferred_element_type=F32)

        @pl.when(t == pl.num_programs(1) - 1)
        def _():
            o_ref[...] = acc_ref[...].astype(BF16)

    return pl.pallas_call(
        body, name=name, grid=(N // tn, T // tt),
        in_specs=[pl.BlockSpec((tt, tn), lambda j, t: (t, j)), pl.BlockSpec((tt, K), lambda j, t: (t, 0))]
        + [pl.BlockSpec(memory_space=pl.ANY)] * len(after),
        out_specs=pl.BlockSpec((tn, K), lambda j, t: (j, 0)),
        out_shape=jax.ShapeDtypeStruct((N, K), BF16),
        scratch_shapes=[pltpu.VMEM((tn, K), F32)],
        compiler_params=_params("parallel", "arbitrary"))(a, b, *after)


def ffn_up(fn, w_gu_t, name):
    T, D = fn.shape
    F = w_gu_t.shape[0] // 2
    tm = _tile(T, 256)
    tn = _tile(F, 512)

    def body(a_ref, w_ref, act_ref, part_ref):
        av = a_ref[...]
        for j in range(F // tn):
            g = lax.dot_general(av, w_ref[pl.ds(j * tn, tn), :], _NT, preferred_element_type=F32)
            up = lax.dot_general(av, w_ref[pl.ds(F + j * tn, tn), :], _NT, preferred_element_type=F32)
            sg = _sigmoid(g)
            silu = g * sg
            act_ref[:, pl.ds(j * tn, tn)] = (silu * up).astype(BF16)
            part_ref[:, pl.ds(j * tn, tn)] = (up * (sg + silu * (1.0 - sg))).astype(BF16)
            part_ref[:, pl.ds(F + j * tn, tn)] = silu.astype(BF16)

    return pl.pallas_call(
        body, name=name, grid=(T // tm,), in_specs=[_rows(tm, D), _full(w_gu_t.shape)],
        out_specs=[_rows(tm, F), _rows(tm, 2 * F)],
        out_shape=[jax.ShapeDtypeStruct((T, F), BF16), jax.ShapeDtypeStruct((T, 2 * F), BF16)],
        compiler_params=_params("parallel"))(fn, w_gu_t)


def ffn_dact(df, w_down, partials, name, after=()):
    T, D = df.shape
    F = w_down.shape[0]
    tm = _tile(T, 256)
    tn = _tile(F, 512)

    def body(d_ref, w_ref, part_ref, *rest):
        o_ref = rest[-1]
        dv = d_ref[...]
        for j in range(F // tn):
            dact = lax.dot_general(dv, w_ref[pl.ds(j * tn, tn), :], _NT, preferred_element_type=F32)
            for cols in (pl.ds(j * tn, tn), pl.ds(F + j * tn, tn)):
                o_ref[:, cols] = (dact * part_ref[:, cols].astype(F32)).astype(BF16)

    return pl.pallas_call(
        body, name=name, grid=(T // tm,),
        in_specs=[_rows(tm, D), _full(w_down.shape), _rows(tm, 2 * F)] + [pl.BlockSpec(memory_space=pl.ANY)] * len(after),
        out_specs=_rows(tm, 2 * F), out_shape=jax.ShapeDtypeStruct((T, 2 * F), BF16),
        compiler_params=_params("parallel"))(df, w_down, partials, *after)


def _layernorm_stats(v):
    mu = jnp.mean(v, axis=-1, keepdims=True)
    cen = v - mu
    rstd = lax.rsqrt(jnp.mean(cen * cen, axis=-1, keepdims=True) + LN_EPS)
    return cen * rstd, rstd


GMLP_ROWS = 2 * CHUNK


def gmlp_gate_out(a, ln_g, ln_b, w_mask, b_s_t, w_out, h, g_post, g_pre, name):
    T, W2 = a.shape
    W = W2 // 2
    G = w_mask.shape[0]
    C = W // G
    D = w_out.shape[1]
    tm = min(T, GMLP_ROWS)

    def body(a_ref, lg_ref, lb_ref, w_ref, bs_ref, wo_ref, h_ref, gp_ref, g_ref,
             gated_ref, y_ref, ho_ref, hn_ref, acc_ref):
        for c in range(tm // CHUNK):
            rows = pl.ds(c * CHUNK, CHUNK)
            xhat, _ = _layernorm_stats(_gelu(a_ref[rows, W:].astype(F32)))
            vln = (xhat * lg_ref[...] + lb_ref[...]).astype(BF16)
            for g in range(G):
                cols = pl.ds(g * C, C)
                sv = jnp.dot(w_ref[g], vln[:, g * C:(g + 1) * C], preferred_element_type=F32) + bs_ref[:, g:g + 1]
                gated_ref[rows, cols] = (_gelu(a_ref[rows, cols].astype(F32)) * sv).astype(BF16)
        _rows_times_weight(gated_ref, wo_ref, acc_ref)
        y = acc_ref[...]
        y_ref[...] = y.astype(BF16)
        hv = h_ref[...] + _rms(y, gp_ref[...])
        ho_ref[...] = hv
        hn_ref[...] = _rms(hv, g_ref[...]).astype(BF16)

    row, vec = _rows(tm, D), _full((1, D))
    return pl.pallas_call(
        body, name=name, grid=(T // tm,),
        in_specs=[_rows(tm, W2), _full((1, W)), _full((1, W)), _full(w_mask.shape), _full(b_s_t.shape),
                  _full(w_out.shape), row, vec, vec],
        out_specs=[_rows(tm, W), row, row, row],
        out_shape=[jax.ShapeDtypeStruct((T, W), BF16), jax.ShapeDtypeStruct((T, D), BF16),
                   jax.ShapeDtypeStruct((T, D), F32), jax.ShapeDtypeStruct((T, D), BF16)],
        scratch_shapes=[pltpu.VMEM((tm, D), F32)],
        compiler_params=_params("parallel"))(a, ln_g, ln_b, w_mask, b_s_t, w_out, h, g_post, g_pre)


def gmlp_gate_bwd_dx(a, dmix, w_out, ln_g, ln_b, w_mask, w_mask_t, b_s_t, group_onehot, w_in_t,
                     dh_in, x, g_pre, post, name, after):
    T, W2 = a.shape
    W = W2 // 2
    G = w_mask.shape[0]
    C = W // G
    D = w_out.shape[1]
    tm = min(T, GMLP_ROWS)
    tn = _tile(W, 512)
    row, vec = _rows(tm, D), _full((1, D))
    ins = [a, dmix, w_out, ln_g, ln_b, w_mask, w_mask_t, b_s_t, group_onehot, w_in_t, dh_in, x, g_pre]
    in_specs = [_rows(tm, W2), row, _full(w_out.shape), _full((1, W)), _full((1, W)), _full(w_mask.shape),
                _full(w_mask_t.shape), _full(b_s_t.shape), _full(group_onehot.shape), _full(w_in_t.shape), row, row, vec]
    names = ["da", "dw_s", "db_s_t", "dln_g", "dln_b", "db_in", "dh", "dg_pre"]
    outs = [jax.ShapeDtypeStruct((T, W2), BF16), jax.ShapeDtypeStruct((G, CHUNK, CHUNK), F32),
            jax.ShapeDtypeStruct((CHUNK, LANES), F32), jax.ShapeDtypeStruct((1, W), F32),
            jax.ShapeDtypeStruct((1, W), F32), jax.ShapeDtypeStruct((1, W2), F32),
            jax.ShapeDtypeStruct((T, D), F32), jax.ShapeDtypeStruct((1, D), F32)]
    out_specs = [_rows(tm, W2), _full((G, CHUNK, CHUNK)), _full((CHUNK, LANES)), _full((1, W)), _full((1, W)),
                 _full((1, W2)), row, vec]
    if post is not None:
        ins += list(post)
        in_specs += [row, vec]
        names += ["dy", "dg_post"]
        outs += [jax.ShapeDtypeStruct((T, D), BF16), jax.ShapeDtypeStruct((1, D), F32)]
        out_specs += [row, vec]
    n_in = len(ins) + 1

    def body(*refs):
        (a_ref, dm_ref, wo_ref, lg_ref, lb_ref, w_ref, wt_ref, bs_ref, e_ref, wi_ref, dh_in_ref, x_ref, gpre_ref) = refs[:13]
        out_refs = refs[n_in:-3]
        da_ref, dws_ref, dbs_ref, dlg_ref, dlb_ref, dbin_ref, dh_ref, dgpre_ref = out_refs[:8]
        dgated_ref, dvln_ref, acc_ref = refs[-3:]

        @pl.when(pl.program_id(0) == 0)
        def _():
            for r in out_refs[1:6] + out_refs[7::2]:
                r[...] = jnp.zeros_like(r)
        dm = dm_ref[...]
        for j in range(W // tn):
            cols = pl.ds(j * tn, tn)
            dgated_ref[:, cols] = lax.dot_general(dm, wo_ref[cols, :], _NT, preferred_element_type=F32)
        causal = (lax.broadcasted_iota(jnp.int32, (CHUNK, CHUNK), 1)
                  <= lax.broadcasted_iota(jnp.int32, (CHUNK, CHUNK), 0))
        for c in range(tm // CHUNK):
            rows = pl.ds(c * CHUNK, CHUNK)
            gelu_v, gelu_grad_v = _gelu_and_grad(a_ref[rows, W:].astype(F32))
            xhat, rstd = _layernorm_stats(gelu_v)
            vln = (xhat * lg_ref[...] + lb_ref[...]).astype(BF16)
            for g in range(G):
                cols = pl.ds(g * C, C)
                vg = vln[:, g * C:(g + 1) * C]
                gelu_u, gelu_grad_u = _gelu_and_grad(a_ref[rows, cols].astype(F32))
                dgated = dgated_ref[rows, cols]
                dsv = (dgated * gelu_u).astype(BF16)
                dbs_ref[...] += jnp.dot(dsv, e_ref[cols, :], preferred_element_type=F32)
                sv = jnp.dot(w_ref[g], vg, preferred_element_type=F32) + bs_ref[:, g:g + 1]
                dau = dgated * sv * gelu_grad_u
                da_ref[rows, cols] = dau.astype(BF16)
                dbin_ref[:, cols] += jnp.sum(dau, axis=0, keepdims=True)
                dws_ref[g] += jnp.where(causal, lax.dot_general(dsv, vg, _NT, preferred_element_type=F32), 0.0)
                dvln_ref[:, cols] = jnp.dot(wt_ref[g], dsv, preferred_element_type=F32)
            dvln = dvln_ref[...]
            dlg_ref[...] += jnp.sum(dvln * xhat, axis=0, keepdims=True)
            dlb_ref[...] += jnp.sum(dvln, axis=0, keepdims=True)
            dxhat = dvln * lg_ref[...]
            dv = rstd * (dxhat - jnp.mean(dxhat, axis=-1, keepdims=True)
                         - xhat * jnp.mean(dxhat * xhat, axis=-1, keepdims=True))
            dav = dv * gelu_grad_v
            da_ref[rows, W:] = dav.astype(BF16)
            dbin_ref[:, W:] += jnp.sum(dav, axis=0, keepdims=True)
        _rows_times_weight(da_ref, wi_ref, acc_ref)
        dx, dg = _rms_bwd(x_ref[...], gpre_ref[...], acc_ref[...])
        dh = dh_in_ref[...] + dx
        dh_ref[...] = dh
        dgpre_ref[...] += dg
        if post is not None:
            y_ref, gp_ref = refs[13:15]
            dy, dgp = _rms_bwd(y_ref[...].astype(F32), gp_ref[...], dh)
            out_refs[8][...] = dy.astype(BF16)
            out_refs[9][...] += dgp

    res = pl.pallas_call(
        body, name=name, grid=(T // tm,), in_specs=in_specs + [pl.BlockSpec(memory_space=pl.ANY)],
        out_specs=out_specs, out_shape=outs,
        scratch_shapes=[pltpu.VMEM((tm, W), F32), pltpu.VMEM((CHUNK, W), F32), pltpu.VMEM((tm, D), F32)],
        compiler_params=_params("arbitrary"))(*ins, after)
    return dict(zip(names, res))


def rope_tables(pos, inv_freq_row, name):
    T = pos.shape[0]
    tm = _tile(T, 512)

    def body(p_ref, f_ref, c_ref, s1_ref, s2_ref):
        ang = p_ref[...].astype(F32) * f_ref[...]
        lane = lax.broadcasted_iota(jnp.int32, (tm, LANES), 1) % HEAD_DIM
        sin = jnp.sin(ang)
        c_ref[...] = jnp.cos(ang)
        s1_ref[...] = jnp.where(lane < _HALF, -sin, 0.0)
        s2_ref[...] = jnp.where((lane >= _HALF) & (lane < ROPE_DIM), sin, 0.0)

    tab = _rows(tm, LANES)
    return pl.pallas_call(
        body, name=name, grid=(T // tm,), in_specs=[_rows(tm, 1), _full((1, LANES))], out_specs=[tab] * 3,
        out_shape=[jax.ShapeDtypeStruct((T, LANES), F32)] * 3, compiler_params=_params("parallel"))(pos, inv_freq_row)


_HALF = ROPE_DIM // 2


def _slabs(x):
    return [x[:, b * LANES:(b + 1) * LANES] for b in range(x.shape[1] // LANES)]


def _rotate(x, c, s1, s2):
    return [xs * c + pltpu.roll(xs, LANES - _HALF, 1) * s1 + pltpu.roll(xs, _HALF, 1) * s2 for xs in _slabs(x)]


def _rotate_transposed(dy, c, s1, s2):
    return [ds * c + pltpu.roll(ds * s1, _HALF, 1) + pltpu.roll(ds * s2, LANES - _HALF, 1) for ds in _slabs(dy)]


def rope_fwd(qkv, tabs, q_width, kv_width, name):
    T, QKV = qkv.shape
    tm = _tile(T, 512)
    QK = q_width + kv_width
    scale = HEAD_DIM ** -0.5

    def body(x_ref, c_ref, s1_ref, s2_ref, o_ref):
        x = x_ref[:, :QK].astype(F32)
        slabs = _rotate(x, c_ref[...], s1_ref[...], s2_ref[...])
        for b, y in enumerate(slabs):
            if b * LANES < q_width:
                y = y * scale
            o_ref[:, b * LANES:(b + 1) * LANES] = y.astype(BF16)

    tab = _rows(tm, LANES)
    return pl.pallas_call(
        body, name=name, grid=(T // tm,), in_specs=[_rows(tm, QKV), tab, tab, tab], out_specs=_rows(tm, QK),
        out_shape=jax.ShapeDtypeStruct((T, QK), BF16), compiler_params=_params("parallel"))(qkv, *tabs)


def rope_bwd(dq, dk, dv, tabs, name):
    T, Q = dq.shape
    KV = dk.shape[1]
    tm = _tile(T, 512)
    scale = HEAD_DIM ** -0.5

    def body(dq_ref, dk_ref, dv_ref, c_ref, s1_ref, s2_ref, o_ref, b_ref):
        @pl.when(pl.program_id(0) == 0)
        def _():
            b_ref[...] = jnp.zeros_like(b_ref)
        tabs_v = (c_ref[...], s1_ref[...], s2_ref[...])
        pieces = [s * scale for s in _rotate_transposed(dq_ref[...], *tabs_v)]
        pieces += _rotate_transposed(dk_ref[...], *tabs_v)
        pieces += _slabs(dv_ref[...])
        for b, y in enumerate(pieces):
            cols = pl.ds(b * LANES, LANES)
            o_ref[:, cols] = y.astype(BF16)
            b_ref[:, cols] += jnp.sum(y, axis=0, keepdims=True)

    tab = _rows(tm, LANES)
    return pl.pallas_call(
        body, name=name, grid=(T // tm,), in_specs=[_rows(tm, Q), _rows(tm, KV), _rows(tm, KV), tab, tab, tab],
        out_specs=[_rows(tm, Q + 2 * KV), _full((1, Q + 2 * KV))],
        out_shape=[jax.ShapeDtypeStruct((T, Q + 2 * KV), BF16), jax.ShapeDtypeStruct((1, Q + 2 * KV), F32)],
        compiler_params=_params("arbitrary"))(dq, dk, dv, *tabs)


def _band_mask(n, heads=1):
    qi = lax.broadcasted_iota(jnp.int32, (heads * CHUNK, 2 * CHUNK), 0) % CHUNK
    sj = lax.broadcasted_iota(jnp.int32, (heads * CHUNK, 2 * CHUNK), 1)
    return (sj > qi) & (sj <= qi + CHUNK) & ((n > 0) | (sj >= CHUNK))


def _kv_head(kc_ref, kp_ref, vc_ref, vp_ref, kh):
    lanes = slice(kh * HEAD_DIM, (kh + 1) * HEAD_DIM)
    return (jnp.concatenate([kp_ref[:, lanes], kc_ref[:, lanes]], axis=0),
            jnp.concatenate([vp_ref[:, lanes], vc_ref[:, lanes]], axis=0))


def _head_probs(q, kk, valid, sink):
    s = jnp.where(valid, lax.dot_general(q, kk, _NT, preferred_element_type=F32), NEG_INF)
    m = jnp.maximum(jnp.max(s, axis=-1, keepdims=True), sink)
    p = jnp.exp(s - m)
    e_sink = jnp.exp(sink - m)
    return p, 1.0 / (jnp.sum(p, axis=-1, keepdims=True) + e_sink), e_sink


def _attn_specs(q_width, kv_width, order):
    qb = q_width // kv_width
    prev = lambda i: jnp.maximum(order(i) - 1, 0)
    return [pl.BlockSpec((CHUNK, q_width), lambda i: (order(i), 0)),
            pl.BlockSpec((CHUNK, kv_width), lambda i: (order(i), qb)),
            pl.BlockSpec((CHUNK, kv_width), lambda i: (prev(i), qb)),
            pl.BlockSpec((CHUNK, kv_width), lambda i: (order(i), qb + 1)),
            pl.BlockSpec((CHUNK, kv_width), lambda i: (prev(i), qb + 1))]


def attn_fwd(qk, qkv, sinks, q_width, kv_width, name):
    T = qk.shape[0]
    group = q_width // kv_width

    def body(q_ref, kc_ref, kp_ref, vc_ref, vp_ref, sink_ref, o_ref):
        valid = _band_mask(pl.program_id(0))
        for kh in range(kv_width // HEAD_DIM):
            kk, vv = _kv_head(kc_ref, kp_ref, vc_ref, vp_ref, kh)
            for g in range(group):
                h = kh * group + g
                lanes = slice(h * HEAD_DIM, (h + 1) * HEAD_DIM)
                p, inv, _ = _head_probs(q_ref[:, lanes], kk, valid, sink_ref[h])
                o_ref[:, lanes] = (jnp.dot(p.astype(BF16), vv, preferred_element_type=F32) * inv).astype(BF16)

    specs = _attn_specs(q_width, kv_width, lambda i: i)
    return pl.pallas_call(
        body, name=name, grid=(T // CHUNK,),
        in_specs=specs + [pl.BlockSpec(memory_space=pltpu.SMEM)],
        out_specs=_rows(CHUNK, q_width), out_shape=jax.ShapeDtypeStruct((T, q_width), BF16),
        compiler_params=_params("parallel"))(qk, qk, qk, qkv, qkv, sinks)


def attn_bwd(qk, qkv, do, sinks, q_width, kv_width, name):
    T = qk.shape[0]
    NB = T // CHUNK
    group = q_width // kv_width

    def body(q_ref, kc_ref, kp_ref, vc_ref, vp_ref, do_ref, sink_ref, dq_ref, dk_ref, dv_ref, ds_ref, ck_ref, cv_ref):
        i = pl.program_id(0)
        n = NB - 1 - i

        @pl.when(i == 0)
        def _():
            ck_ref[...] = jnp.zeros_like(ck_ref)
            cv_ref[...] = jnp.zeros_like(cv_ref)
            ds_ref[...] = jnp.zeros_like(ds_ref)
        lane = lax.broadcasted_iota(jnp.int32, (1, LANES), 1)
        dsink_row = jnp.zeros((1, LANES), F32)
        valid = _band_mask(n, group)
        head = lax.broadcasted_iota(jnp.int32, (group * CHUNK, 1), 0) // CHUNK
        for kh in range(kv_width // HEAD_DIM):
            kk, vv = _kv_head(kc_ref, kp_ref, vc_ref, vp_ref, kh)
            heads = [slice((kh * group + g) * HEAD_DIM, (kh * group + g + 1) * HEAD_DIM) for g in range(group)]
            q = jnp.concatenate([q_ref[:, hs] for hs in heads], axis=0)
            do = jnp.concatenate([do_ref[:, hs] for hs in heads], axis=0)
            sink = jnp.zeros((group * CHUNK, 1), F32)
            for g in range(group):
                sink = jnp.where(head == g, sink_ref[kh * group + g], sink)
            p, inv, e_sink = _head_probs(q, kk, valid, sink)
            p = p * inv
            dp = lax.dot_general(do, vv, _NT, preferred_element_type=F32)
            delta = jnp.sum(p * dp, axis=-1, keepdims=True)
            ds = (p * (dp - delta)).astype(BF16)
            dsink = -e_sink * inv * delta
            dq = jnp.dot(ds, kk, preferred_element_type=F32)
            for g, hs in enumerate(heads):
                rows = slice(g * CHUNK, (g + 1) * CHUNK)
                dsink_row = dsink_row + jnp.where(lane == kh * group + g, jnp.sum(dsink[rows]), 0.0)
                dq_ref[:, hs] = dq[rows]
            dkk = lax.dot_general(ds, q, _TN, preferred_element_type=F32)
            dvv = lax.dot_general(p.astype(BF16), do, _TN, preferred_element_type=F32)
            lanes = slice(kh * HEAD_DIM, (kh + 1) * HEAD_DIM)
            dk_ref[:, lanes] = dkk[CHUNK:] + ck_ref[:, lanes]
            dv_ref[:, lanes] = dvv[CHUNK:] + cv_ref[:, lanes]
            ck_ref[:, lanes] = dkk[:CHUNK]
            cv_ref[:, lanes] = dvv[:CHUNK]
        ds_ref[0:1, :] += dsink_row

    order = lambda i: NB - 1 - i
    specs = _attn_specs(q_width, kv_width, order)
    kv_out = pl.BlockSpec((CHUNK, kv_width), lambda i: (order(i), 0))
    q_rows = pl.BlockSpec((CHUNK, q_width), lambda i: (order(i), 0))
    return pl.pallas_call(
        body, name=name, grid=(NB,),
        in_specs=specs + [q_rows, pl.BlockSpec(memory_space=pltpu.SMEM)],
        out_specs=[q_rows, kv_out, kv_out, _full((8, LANES))],
        out_shape=[jax.ShapeDtypeStruct((T, q_width), F32), jax.ShapeDtypeStruct((T, kv_width), F32),
                   jax.ShapeDtypeStruct((T, kv_width), F32), jax.ShapeDtypeStruct((8, LANES), F32)],
        scratch_shapes=[pltpu.VMEM((CHUNK, kv_width), F32), pltpu.VMEM((CHUNK, kv_width), F32)],
        compiler_params=_params("arbitrary"))(qk, qk, qk, qkv, qkv, do, sinks)


def _blocked(w):
    return w.reshape(N_DEV, w.shape[0] // N_DEV, w.shape[1])


def kernel(x, positions, pre_mix_g, post_mix_g, pre_ffn_g, post_ffn_g, a_w_in, a_b_in, a_ln_g, a_ln_b, a_w_s, a_b_s, a_w_out, b_w_qkv, b_b_qkv, b_sinks, b_w_o, ffn_w_gu, ffn_w_down, loss_target, m_pre_mix_g, m_post_mix_g, m_pre_ffn_g, m_post_ffn_g, m_a_w_in, m_a_b_in, m_a_ln_g, m_a_ln_b, m_a_w_s, m_a_b_s, m_a_w_out, m_b_w_qkv, m_b_b_qkv, m_b_sinks, m_b_w_o, m_ffn_w_gu, m_ffn_w_down, v_pre_mix_g, v_post_mix_g, v_pre_ffn_g, v_post_ffn_g, v_a_w_in, v_a_b_in, v_a_ln_g, v_a_ln_b, v_a_w_s, v_a_b_s, v_a_w_out, v_b_w_qkv, v_b_b_qkv, v_b_sinks, v_b_w_o, v_ffn_w_gu, v_ffn_w_down):
    weights = dict(pre_mix_g=pre_mix_g, post_mix_g=post_mix_g, pre_ffn_g=pre_ffn_g, post_ffn_g=post_ffn_g,
                   a_w_in=a_w_in, a_b_in=a_b_in, a_ln_g=a_ln_g, a_ln_b=a_ln_b, a_w_s=a_w_s, a_b_s=a_b_s,
                   a_w_out=a_w_out, b_w_qkv=b_w_qkv, b_b_qkv=b_b_qkv, b_sinks=b_sinks, b_w_o=b_w_o,
                   ffn_w_gu=ffn_w_gu, ffn_w_down=ffn_w_down)
    mom_m = dict(pre_mix_g=m_pre_mix_g, post_mix_g=m_post_mix_g, pre_ffn_g=m_pre_ffn_g, post_ffn_g=m_post_ffn_g,
                 a_w_in=m_a_w_in, a_b_in=m_a_b_in, a_ln_g=m_a_ln_g, a_ln_b=m_a_ln_b, a_w_s=m_a_w_s, a_b_s=m_a_b_s,
                 a_w_out=m_a_w_out, b_w_qkv=m_b_w_qkv, b_b_qkv=m_b_b_qkv, b_sinks=m_b_sinks, b_w_o=m_b_w_o,
                 ffn_w_gu=m_ffn_w_gu, ffn_w_down=m_ffn_w_down)
    mom_v = dict(pre_mix_g=v_pre_mix_g, post_mix_g=v_post_mix_g, pre_ffn_g=v_pre_ffn_g, post_ffn_g=v_post_ffn_g,
                 a_w_in=v_a_w_in, a_b_in=v_a_b_in, a_ln_g=v_a_ln_g, a_ln_b=v_a_ln_b, a_w_s=v_a_w_s, a_b_s=v_a_b_s,
                 a_w_out=v_a_w_out, b_w_qkv=v_b_w_qkv, b_b_qkv=v_b_b_qkv, b_sinks=v_b_sinks, b_w_o=v_b_w_o,
                 ffn_w_gu=v_ffn_w_gu, ffn_w_down=v_ffn_w_down)
    names = list(weights)
    big = ("a_w_in", "b_w_qkv", "ffn_w_gu", "a_w_out", "b_w_o", "ffn_w_down")

    T, D = x.shape[1], x.shape[2]
    depth = pre_mix_g.shape[0]
    n_heads = b_sinks.shape[1]
    q_width = n_heads * HEAD_DIM
    kv_width = N_KV_HEADS * HEAD_DIM
    G = a_w_s.shape[1]
    W = a_ln_g.shape[1]
    device = _index(_position())

    dev = device.reshape(1).astype(jnp.int32)

    def layer_keys(i):
        mixer = ("a_w_in", "a_w_out") if i % 2 == 0 else ("b_w_qkv", "b_w_o")
        return [(k, i // 2) for k in mixer] + [("ffn_w_gu", i), ("ffn_w_down", i)]

    as_view = ("b_w_qkv", "ffn_w_gu")
    in_kernel = ("a_w_in",)
    view = lambda k, t: jnp.swapaxes(t, 1, 2) if k in as_view else t
    w_view = {k: view(k, weights[k]) for k in big}
    m_view = {k: view(k, mom_m[k]) for k in big}
    v_view = {k: view(k, mom_v[k]) for k in big}

    groups_keys = [[key] for key in layer_keys(0)] + [layer_keys(i) for i in range(1, depth)]
    group_of = {}
    for g, keys in enumerate(groups_keys):
        layer = 0 if g < len(layer_keys(0)) else g - len(layer_keys(0)) + 1
        for pos, (k, _) in enumerate(keys):
            group_of[(layer, k)] = (g, pos)
    bias_land = lax.dynamic_update_slice(
        jnp.zeros((N_DEV,) + b_b_qkv.shape, F32), b_b_qkv[None], (device, 0, 0))
    h, hn = x[0], first_norm(x[0], pre_mix_g[0][None], "norm_first")
    first = len(layer_keys(0))
    in_gather, token = [], ()
    for name, lo, hi in (("gather_start_first", 0, first), ("gather_start_rest", first, len(groups_keys))):
        groups = [list(prep_weights([(w_view[k], l, k in in_kernel) for k, l in keys], dev, f"prep_{name}_{g}",
                                    after=token)) for g, keys in enumerate(groups_keys[lo:hi])]
        if lo:
            groups[0].append(bias_land)
        gather_sems, flat_lands, token = relay_start(groups, name)
        token = (token, hn)
        for grp, sems in zip(groups, gather_sems):
            in_gather.append(([flat_lands.pop(0) for _ in grp], sems))
    all_started = token[0]
    passed, gathered = {}, {}

    def pass_on(g, after):
        if g not in passed and g < len(in_gather):
            lands, sems = in_gather[g]
            passed[g], lands = relay_pass(lands, sems, after, f"gather_pass_{g}")
            in_gather[g] = (lands, sems)

    pass_on(0, (all_started,))

    def weight(i, k, *after):
        g, pos = group_of[(i, k)]
        if g not in gathered:
            pass_on(g + 1, after)
            lands, sems = in_gather[g]
            gathered[g] = relay_wait(lands, sems, passed[g], after, f"gather_wait_{g}")
        w = gathered[g][pos]
        return w.reshape(N_DEV * w.shape[1], w.shape[2])

    causal = jnp.tril(jnp.ones((CHUNK, CHUNK), dtype=bool))
    w_mask = jnp.where(causal[None, None], a_w_s, 0.0).astype(BF16)
    w_mask_t = jnp.swapaxes(w_mask, 2, 3)
    b_s_t = jnp.swapaxes(a_b_s, 1, 2)
    group_onehot = (jnp.arange(W)[:, None] // (W // G) == jnp.arange(LANES)[None, :]).astype(BF16)
    lane = jnp.arange(LANES) % HEAD_DIM
    inv_freq = ROPE_THETA ** (-jnp.arange(0, ROPE_DIM, 2, dtype=F32) / ROPE_DIM)
    inv_freq_row = jnp.where(lane < ROPE_DIM, inv_freq[lane % (ROPE_DIM // 2)], 0.0)[None, :].astype(F32)
    tabs = rope_tables(positions.reshape(T, 1), inv_freq_row, "rope_tables")

    saved = []
    for i in range(depth):
        j = i // 2
        s = dict(h=h, hn=hn)
        post_pre = (post_mix_g[i][None], pre_ffn_g[i][None])
        if i % 2 == 0:
            s["a"] = rowmm(hn, weight(i, "a_w_in", hn, all_started), True, a_b_in[j][None], BF16, "gmlp_in")
            s["gated"], s["mix"], s["h2"], s["fn"] = gmlp_gate_out(
                s["a"], a_ln_g[j][None], a_ln_b[j][None], w_mask[j], b_s_t[j], weight(i, "a_w_out", s["a"]),
                h, *post_pre, "gmlp_gate_out")
        else:
            w_qkv = weight(i, "b_w_qkv", hn)
            b_qkv_full = jnp.swapaxes(gathered[group_of[(1, "b_w_qkv")][0]][-1], 0, 1).reshape(b_b_qkv.shape[0], 1, -1)
            s["qkv"] = rowmm(hn, w_qkv, True, b_qkv_full[j], BF16, "attn_qkv")
            s["qk"] = rope_fwd(s["qkv"], tabs, q_width, kv_width, "attn_rope")
            s["o"] = attn_fwd(s["qk"], s["qkv"], b_sinks[j], q_width, kv_width, "attn_core")
            s["mix"], s["h2"], s["fn"] = mm_add_norm(s["o"], weight(i, "b_w_o", s["o"]), h, *post_pre, "attn_out")
        s["act"], s["act_partials"] = ffn_up(s["fn"], weight(i, "ffn_w_gu", s["fn"]), "ffn_up")
        w_down = weight(i, "ffn_w_down", s["act"])
        saved.append(s)
        if i + 1 < depth:
            s["f"], h, hn = mm_add_norm(s["act"], w_down, s["h2"], post_ffn_g[i][None], pre_mix_g[i + 1][None], "ffn_down")
    small = {k: [None] * weights[k].shape[0] for k in names if k not in big}
    dh, df, loss_sum, small["post_ffn_g"][depth - 1] = mm_add_norm_loss(
        saved[-1]["act"], w_down, saved[-1]["h2"], post_ffn_g[depth - 1][None], loss_target[0], "ffn_down_loss")
    loss = lax.psum(loss_sum[0, 0] * (0.5 / D), ("x", "y", "c"))

    grads = {k: [None] * weights[k].shape[0] for k in big}
    in_flight = []

    def send_grads(keys, tag):
        sems, bufs, token = exchange_start([_blocked(grads[k][l]) for k, l in keys], "exchange_start_" + tag)
        in_flight.append((keys, sems, bufs, tag))
        return token

    replicated = [k for k in names if k not in big and k not in ("b_b_qkv", "a_w_s")]
    main = sum(weights[k].size for k in replicated)
    main_rows = -(-main // (LANES * LANES)) * LANES
    bias_size = b_b_qkv.shape[0] * N_DEV * b_b_qkv.shape[1]
    bias_rows = -(-bias_size // (8 * LANES)) * 8
    flat = lambda vals: [v.reshape(-1) for v in vals]

    def start_small_gather():
        packed = jnp.concatenate(
            flat(p for k in replicated for p in small[k]) + [jnp.zeros((main_rows * LANES - main,), F32)]
            + flat(small["b_b_qkv"]) + [jnp.zeros((bias_rows * LANES - bias_size,), F32)]
        ).reshape(1, main_rows + bias_rows, LANES)
        spatial = jnp.stack(small["a_w_s"]).reshape(1, -1, LANES)
        return gather_start([prep_weights([(packed, 0, False), (spatial, 0, False)], dev, "place_small_grads", F32)],
                            "gather_start_small")

    sent = ()
    for i in reversed(range(depth)):
        j = i // 2
        s = saved[i]
        dgu = ffn_dact(df, weight(i, "ffn_w_down"), s["act_partials"], "ffn_dact", after=sent)
        grads["ffn_w_down"][i] = grad_mm(s["act"], df, "ffn_down_grad")
        grads["ffn_w_gu"][i] = grad_mm(dgu, s["fn"], "ffn_up_grad")
        sent = send_grads(layer_keys(i)[2:], f"ffn_{i}")
        r = mm_norm_bwd(dgu, weight(i, "ffn_w_gu"), dh, s["h2"], pre_ffn_g[i][None],
                        (s["mix"], post_mix_g[i][None]), "ffn_dx", sent)
        dh, dmix = r["dh"], r["dy"]
        small["pre_ffn_g"][i], small["post_mix_g"][i] = r["dg_pre"], r["dg_post"]
        post = (saved[i - 1]["f"], post_ffn_g[i - 1][None]) if i > 0 else None
        if i % 2 == 0:
            grads["a_w_out"][j] = grad_mm(s["gated"], dmix, "gmlp_out_grad")
            r = gmlp_gate_bwd_dx(
                s["a"], dmix, weight(i, "a_w_out"), a_ln_g[j][None], a_ln_b[j][None], w_mask[j], w_mask_t[j], b_s_t[j],
                group_onehot, weight(i, "a_w_in"), dh, s["h"], pre_mix_g[i][None], post, "gmlp_gate_bwd_dx", sent)
            small["a_w_s"][j], small["a_b_s"][j] = r["dw_s"], r["db_s_t"][:, :G].T
            small["a_ln_g"][j], small["a_ln_b"][j], small["a_b_in"][j] = r["dln_g"][0], r["dln_b"][0], r["db_in"][0]
            small["pre_mix_g"][i] = r["dg_pre"]
            first_layer = ()
            if i == 0:
                small_gather = start_small_gather()
                first_layer = (small_gather[2],)
            grads["a_w_in"][j] = grad_mm(r["da"], s["hn"], "gmlp_in_grad", after=first_layer)
            sent = (send_grads(layer_keys(i)[:2], f"mixer_{i}"),)
        else:
            grads["b_w_o"][j] = grad_mm(s["o"], dmix, "attn_out_grad")
            do = rowmm(dmix, weight(i, "b_w_o"), True, None, BF16, "attn_do")
            dq, dk, dv, dsink = attn_bwd(s["qk"], s["qkv"], do, b_sinks[j], q_width, kv_width, "attn_core_bwd")
            dqkv, dbias = rope_bwd(dq, dk, dv, tabs, "attn_rope_bwd")
            small["b_sinks"][j], small["b_b_qkv"][j] = dsink[0, :n_heads], dbias[0]
            grads["b_w_qkv"][j] = grad_mm(dqkv, s["hn"], "attn_qkv_grad")
            sent = send_grads(layer_keys(i)[:2], f"mixer_{i}")
            r = mm_norm_bwd(dqkv, weight(i, "b_w_qkv"), dh, s["h"], pre_mix_g[i][None], post, "attn_dx", sent)
            sent = ()
        dh, small["pre_mix_g"][i] = r["dh"], r["dg_pre"]
        if i > 0:
            df, small["post_ffn_g"][i - 1] = r["dy"], r["dg_post"]
    grad_x = dh[None]

    small_sems, small_lands, small_sent = small_gather
    pack_main = lambda src: jnp.concatenate(
        flat(src[k] for k in replicated) + [jnp.ones((main_rows * LANES - main,), F32)]).reshape(main_rows, LANES)
    packed_w, packed_m, packed_v = pack_main(weights), pack_main(mom_m), pack_main(mom_v)

    stacked = {k: [lax.empty(w_view[k].shape, F32) for _ in range(4)] for k in big}
    after = (small_sent, dh) + sent
    for keys, sems, bufs, tag in in_flight:
        partials, landed = exchange_wait(bufs, sems, after, "exchange_wait_" + tag)
        for (k, l), part, land in zip(keys, partials, landed):
            stacked[k] = sum_adamw(part, land, w_view[k], m_view[k], v_view[k], stacked[k], l,
                                   k in in_kernel, dev, "adamw_" + k)
        after = tuple(stacked[k][0] for k, _ in keys)
    out_g, out_d, out_m, out_v = {}, {}, {}, {}
    for k in big:
        out_g[k], out_d[k], out_m[k], out_v[k] = [view(k, t) for t in stacked[k]]

    gathered_small, gathered_spatial = gather_wait(small_lands, small_sems[0], after + (packed_w, packed_m, packed_v),
                                                   "gather_wait_small")
    rows_of = lambda t: t.reshape(-1, LANES)
    spatial = sum_adamw_small(gathered_spatial, rows_of(a_w_s), rows_of(m_a_w_s), rows_of(v_a_w_s), "adamw_a_w_s")
    out_g["a_w_s"], out_d["a_w_s"], out_m["a_w_s"], out_v["a_w_s"] = [t.reshape(a_w_s.shape) for t in spatial]
    results = sum_adamw_small(gathered_small, packed_w, packed_m, packed_v, "adamw_small")
    off = 0
    for k in replicated:
        size, shape = weights[k].size, weights[k].shape
        out_g[k], out_d[k], out_m[k], out_v[k] = [t.reshape(-1)[off:off + size].reshape(shape) for t in results]
        off += size
    n_local = b_b_qkv.shape[1]
    bias_sum = sum_parts(gathered_small[:, main_rows:], "sum_bias_grads").reshape(-1)[:bias_size]
    out_g["b_b_qkv"] = lax.dynamic_slice_in_dim(
        bias_sum.reshape(b_b_qkv.shape[0], -1), device * n_local, n_local, axis=1)
    pad = lambda t: jnp.concatenate([t.reshape(-1), jnp.ones((8 * LANES - t.size,), F32)]).reshape(8, LANES)
    bias_update = adamw(pad(b_b_qkv), pad(out_g["b_b_qkv"]), pad(m_b_b_qkv), pad(v_b_b_qkv), "adamw_bias")
    out_d["b_b_qkv"], out_m["b_b_qkv"], out_v["b_b_qkv"] = [
        t.reshape(-1)[:b_b_qkv.size].reshape(b_b_qkv.shape) for t in bias_update]

    return (loss, grad_x, *[out_g[k] for k in names], *[out_d[k] for k in names],
            *[out_m[k] for k in names], *[out_v[k] for k in names])
```

```python
import math

import jax
import jax.numpy as jnp
from jax import lax
from jax.experimental import pallas as pl
from jax.experimental.pallas import tpu as pltpu

F32, BF16 = jnp.float32, jnp.bfloat16
MESH = pl.DeviceIdType.MESH
N_DEV = 8

CHUNK = 128
HEAD_DIM = 64
N_KV_HEADS = 4
ROPE_DIM = HEAD_DIM // 4
ROPE_THETA = 500000.0
RMS_EPS = 1e-6
LN_EPS = 1e-5
NEG_INF = -1e30

ADAM_LR = 0.001
ADAM_B1 = 0.9
ADAM_B2 = 0.999
ADAM_EPS = 1e-08
ADAM_WD = 0.01
ADAM_STEP = 10

V7X_VMEM_BYTES = 64 * 2 ** 20
VMEM_LIMIT = V7X_VMEM_BYTES - 8 * 2 ** 20
LANES = 128


def _params(*sem):
    return pltpu.CompilerParams(dimension_semantics=sem or None, vmem_limit_bytes=VMEM_LIMIT)


def _tile(n, pref):
    if n <= pref:
        return n
    t = pref - pref % LANES
    while t >= LANES:
        if n % t == 0:
            return t
        t -= LANES
    return n


def _full(shape):
    return pl.BlockSpec(shape, lambda *_: (0,) * len(shape))


def _rows(tm, width):
    return pl.BlockSpec((tm, width), lambda i: (i, 0))


def _rms(x, g):
    r = lax.rsqrt(jnp.mean(x * x, axis=-1, keepdims=True) + RMS_EPS)
    return x * r * g


def _rms_bwd(x, g, dy):
    r = lax.rsqrt(jnp.mean(x * x, axis=-1, keepdims=True) + RMS_EPS)
    xhat = x * r
    dg = jnp.sum(dy * xhat, axis=0, keepdims=True)
    dxhat = dy * g
    dx = r * (dxhat - xhat * jnp.mean(dxhat * xhat, axis=-1, keepdims=True))
    return dx, dg


_INV_SQRT2 = 1.0 / math.sqrt(2.0)
_INV_SQRT2PI = 1.0 / math.sqrt(2.0 * math.pi)


def _gelu(x):
    return 0.5 * x * (1.0 + lax.erf(x * _INV_SQRT2))


def _gelu_and_grad(x):
    cdf = 0.5 * (1.0 + lax.erf(x * _INV_SQRT2))
    return x * cdf, cdf + x * jnp.exp(-0.5 * x * x) * _INV_SQRT2PI


def _sigmoid(x):
    return 0.5 * jnp.tanh(0.5 * x) + 0.5


def _position():
    return lax.axis_index("x"), lax.axis_index("y"), lax.axis_index("c")


def _index(p):
    return 4 * p[0] + 2 * p[1] + p[2]


_HBM = pl.BlockSpec(memory_space=pltpu.HBM)
_SEM = pl.BlockSpec(memory_space=pltpu.SEMAPHORE)
_ORDERED_BY_DATA = pltpu.SideEffectType.DATAFLOW_SIDE_EFFECTING


def _in_hbm(v):
    return pltpu.with_memory_space_constraint(v, pltpu.HBM)


def _peer(k):
    x, y, c = _position()
    return (x ^ ((k >> 2) & 1), y ^ ((k >> 1) & 1), c ^ (k & 1))


def _split_copies(srcs, dsts, send, recv):
    return [pltpu.make_async_remote_copy(
        src_ref=src(k), dst_ref=dst(k), send_sem=send.at[7 * a + k - 1], recv_sem=recv.at[7 * a + k - 1],
        device_id=_peer(k), device_id_type=MESH)
        for a, (src, dst) in enumerate(zip(srcs, dsts)) for k in range(1, N_DEV)]


def _start_call(groups, sent, copies_of, name, per_array=N_DEV - 1):
    flat = [v for grp in groups for v in grp]
    n, ng = len(flat), len(groups)

    def body(*refs):
        bufs, sems = refs[:n], refs[n:n + 2 * ng]
        q = 0
        for g, grp in enumerate(groups):
            for cp in copies_of(bufs[q:q + len(grp)], sems[2 * g], sems[2 * g + 1]):
                cp.start()
            q += len(grp)
        refs[-1][...] = jnp.zeros_like(refs[-1])

    sem_shapes = []
    for count in sent:
        sem_shapes += [pltpu.SemaphoreType.DMA((per_array * count,)) for _ in range(2)]
    res = pl.pallas_call(
        body, name=name,
        out_shape=sem_shapes + [pltpu.HBM(v.shape, v.dtype) for v in flat] + [jax.ShapeDtypeStruct((8, LANES), F32)],
        in_specs=[_HBM] * n,
        out_specs=[_SEM] * (2 * ng) + [_HBM] * n + [pl.BlockSpec(memory_space=pltpu.VMEM)],
        input_output_aliases={i: 2 * ng + i for i in range(n)},
        compiler_params=pltpu.CompilerParams(has_side_effects=_ORDERED_BY_DATA),
    )(*[_in_hbm(v) for v in flat])
    sems = [(res[2 * g], res[2 * g + 1]) for g in range(ng)]
    return sems, list(res[2 * ng:-1]), res[-1]


def _wait_call(bufs, sems, copies_of, after, name):
    n = len(bufs)

    def body(*refs):
        for cp in copies_of(refs[:n], refs[n], refs[n + 1]):
            cp.wait_send()
            cp.wait_recv()

    return list(pl.pallas_call(
        body, name=name,
        out_shape=[pltpu.HBM(v.shape, v.dtype) for v in bufs],
        in_specs=[_HBM] * n + [_SEM, _SEM] + [pl.BlockSpec(memory_space=pl.ANY)] * len(after),
        out_specs=[_HBM] * n,
        input_output_aliases={i: i for i in range(n)},
        compiler_params=pltpu.CompilerParams(has_side_effects=_ORDERED_BY_DATA),
    )(*bufs, sems[0], sems[1], *after))


def _gather_copies(lands, send, recv):
    me = _index(_position())
    mine = [lambda k, ref=ref: ref.at[pl.ds(me, 1)] for ref in lands]
    return _split_copies(mine, mine, send, recv)


def _gather_arrivals(lands, send, recv):
    me = _index(_position())
    mine = [lambda k, ref=ref: ref.at[pl.ds(me, 1)] for ref in lands]
    theirs = [lambda k, ref=ref: ref.at[pl.ds(_index(_peer(k)), 1)] for ref in lands]
    return _split_copies(mine, theirs, send, recv)


def _exchange_copies(bufs, send, recv):
    half = len(bufs) // 2
    srcs = [lambda k, ref=ref: ref.at[pl.ds(_index(_peer(k)), 1)] for ref in bufs[:half]]
    dsts = [lambda k, ref=ref: ref.at[pl.ds(k - 1, 1)] for ref in bufs[half:]]
    return _split_copies(srcs, dsts, send, recv)


def gather_start(groups, name):
    return _start_call(groups, [len(grp) for grp in groups], _gather_copies, name)


def gather_wait(lands, sems, after, name):
    return _wait_call(lands, sems, _gather_arrivals, after, name)


_DIRECT = (1, 2, 4, 6)
_PASSED = (3, 5, 7)


def _remote(src, dst, send, recv, q, k):
    return pltpu.make_async_remote_copy(src_ref=src, dst_ref=dst, send_sem=send.at[q], recv_sem=recv.at[q],
                                        device_id=_peer(k), device_id_type=MESH)


def _slot_of(land, k):
    return land.at[pl.ds(_index(_peer(k)), 1)]


def relay_start(groups, name):
    def copies(lands, send, recv):
        return [_remote(_slot_of(land, 0), _slot_of(land, 0), send, recv, len(_DIRECT) * a + j, k)
                for a, land in enumerate(lands) for j, k in enumerate(_DIRECT)]
    return _start_call(groups, [len(grp) for grp in groups], copies, name, per_array=len(_DIRECT))


def relay_pass(lands, sems, after, name):
    n = len(lands)

    def body(*refs):
        bufs, recv_first = refs[:n], refs[n]
        send, recv = refs[n + 1 + len(after):n + 3 + len(after)]
        for a, land in enumerate(bufs):
            for j, k in enumerate(_PASSED):
                came = _slot_of(land, k ^ 1)
                _remote(came, came, recv_first, recv_first, len(_DIRECT) * a + _DIRECT.index(k ^ 1), k ^ 1).wait_recv()
                _remote(came, came, send, recv, len(_PASSED) * a + j, 1).start()

    res = pl.pallas_call(
        body, name=name,
        out_shape=[pltpu.SemaphoreType.DMA((len(_PASSED) * n,))] * 2 + [pltpu.HBM(v.shape, v.dtype) for v in lands],
        in_specs=[_HBM] * n + [_SEM] + [pl.BlockSpec(memory_space=pl.ANY)] * len(after),
        out_specs=[_SEM, _SEM] + [_HBM] * n,
        input_output_aliases={i: 2 + i for i in range(n)},
        compiler_params=pltpu.CompilerParams(has_side_effects=_ORDERED_BY_DATA),
    )(*lands, sems[1], *after)
    return (res[0], res[1]), list(res[2:])


def relay_wait(lands, first, second, after, name):
    n = len(lands)

    def body(*refs):
        bufs = refs[:n]
        send_first, recv_first, send, recv = refs[n:n + 4]
        for a, land in enumerate(bufs):
            mine = _slot_of(land, 0)
            _remote(mine, _slot_of(land, 1), send_first, recv_first, len(_DIRECT) * a, 1).wait_recv()
            for j, k in enumerate(_DIRECT):
                _remote(mine, mine, send_first, recv_first, len(_DIRECT) * a + j, k).wait_send()
            for j, k in enumerate(_PASSED):
                cp = _remote(_slot_of(land, k ^ 1), _slot_of(land, k), send, recv, len(_PASSED) * a + j, 1)
                cp.wait_send()
                cp.wait_recv()

    return list(pl.pallas_call(
        body, name=name,
        out_shape=[pltpu.HBM(v.shape, v.dtype) for v in lands],
        in_specs=[_HBM] * n + [_SEM] * 4 + [pl.BlockSpec(memory_space=pl.ANY)] * len(after),
        out_specs=[_HBM] * n,
        input_output_aliases={i: i for i in range(n)},
        compiler_params=pltpu.CompilerParams(has_side_effects=_ORDERED_BY_DATA),
    )(*lands, first[0], first[1], second[0], second[1], *after))


def exchange_start(partials, name):
    lands = [lax.empty((N_DEV - 1,) + p.shape[1:], p.dtype) for p in partials]
    sems, bufs, token = _start_call([list(partials) + lands], [len(partials)], _exchange_copies, name)
    return sems[0], bufs, token


def exchange_wait(bufs, sems, after, name):
    bufs = _wait_call(bufs, sems, _exchange_copies, after, name)
    return bufs[:len(bufs) // 2], bufs[len(bufs) // 2:]


def prep_weights(items, dev, name, dtype=BF16, after=()):
    n = len(items)
    blocks = [(1, w.shape[2], w.shape[1]) if t else (1,) + w.shape[1:] for w, _, t in items]

    def body(d_ref, *refs):
        for (_, _, transpose), w_ref, o_ref in zip(items, refs[:n], refs[-n:]):
            v = w_ref[0]
            o_ref[0] = (v.T if transpose else v).astype(dtype)

    return pl.pallas_call(
        body, name=name,
        grid_spec=pltpu.PrefetchScalarGridSpec(
            num_scalar_prefetch=1, grid=(1,),
            in_specs=[pl.BlockSpec((1,) + w.shape[1:], lambda i, d, layer=layer: (layer, 0, 0)) for w, layer, _ in items]
            + [pl.BlockSpec(memory_space=pl.ANY)] * len(after),
            out_specs=[pl.BlockSpec(blk, lambda i, d: (d[0], 0, 0)) for blk in blocks]),
        out_shape=[jax.ShapeDtypeStruct((N_DEV,) + blk[1:], dtype) for blk in blocks],
        compiler_params=_params("arbitrary"),
    )(dev, *[w for w, _, _ in items], *after)


def _adamw_math(w, g, m, v):
    mn = ADAM_B1 * m + (1.0 - ADAM_B1) * g
    vn = ADAM_B2 * v + (1.0 - ADAM_B2) * (g * g)
    m_hat = mn * (1.0 / (1.0 - ADAM_B1 ** ADAM_STEP))
    v_hat = vn * (1.0 / (1.0 - ADAM_B2 ** ADAM_STEP))
    return -ADAM_LR * (m_hat / (jnp.sqrt(v_hat) + ADAM_EPS) + ADAM_WD * w), mn, vn


def sum_adamw(partial, landed, w, m, v, prev, layer, transpose, dev, name):
    _, r, c = partial.shape
    if transpose:
        tc = _tile(c, 256)
        grid = (c // tc,)
        part_spec = pl.BlockSpec((1, r, tc), lambda i, d: (d[0], 0, i))
        land_spec = pl.BlockSpec((N_DEV - 1, r, tc), lambda i, d: (0, 0, i))
        w_spec = pl.BlockSpec((1, tc, r), lambda i, d: (layer, i, 0))
    else:
        tr = r // 2 if r % 32 == 0 else r
        grid = (r // tr,)
        part_spec = pl.BlockSpec((1, tr, c), lambda i, d: (d[0], i, 0))
        land_spec = pl.BlockSpec((N_DEV - 1, tr, c), lambda i, d: (0, i, 0))
        w_spec = pl.BlockSpec((1, tr, c), lambda i, d: (layer, i, 0))

    def body(d_ref, p_ref, l_ref, w_ref, m_ref, v_ref, *rest):
        g_ref, dl_ref, mo_ref, vo_ref = rest[-4:]
        acc = p_ref[0].astype(F32)
        for k in range(N_DEV - 1):
            acc = acc + l_ref[k].astype(F32)
        g = acc.T if transpose else acc
        delta, mn, vn = _adamw_math(w_ref[0], g, m_ref[0], v_ref[0])
        g_ref[0], dl_ref[0], mo_ref[0], vo_ref[0] = g, delta, mn, vn

    return pl.pallas_call(
        body, name=name,
        grid_spec=pltpu.PrefetchScalarGridSpec(
            num_scalar_prefetch=1, grid=grid,
            in_specs=[part_spec, land_spec, w_spec, w_spec, w_spec] + [pl.BlockSpec(memory_space=pl.ANY)] * 4,
            out_specs=[w_spec] * 4),
        out_shape=[jax.ShapeDtypeStruct(w.shape, F32)] * 4,
        input_output_aliases={6 + q: q for q in range(4)},
        compiler_params=_params("parallel"),
    )(dev, partial, landed, w, m, v, *prev)


def sum_parts(parts, name):
    P, r, c = parts.shape

    def body(p_ref, o_ref):
        acc = p_ref[0].astype(F32)
        for s in range(1, P):
            acc = acc + p_ref[s].astype(F32)
        o_ref[...] = acc

    return pl.pallas_call(
        body, name=name, in_specs=[_full((P, r, c))], out_specs=_full((r, c)), grid=(1,),
        out_shape=jax.ShapeDtypeStruct((r, c), F32), compiler_params=_params("arbitrary"))(parts)


def adamw(w, g, m, v, name):
    R, C = w.shape
    tr = _tile(R, 512)

    def body(w_ref, g_ref, m_ref, v_ref, d_ref, mo_ref, vo_ref):
        d_ref[...], mo_ref[...], vo_ref[...] = _adamw_math(w_ref[...], g_ref[...], m_ref[...], v_ref[...])

    spec = pl.BlockSpec((tr, C), lambda i: (i, 0))
    return pl.pallas_call(
        body, name=name, grid=(R // tr,),
        in_specs=[spec] * 4, out_specs=[spec] * 3,
        out_shape=[jax.ShapeDtypeStruct((R, C), F32)] * 3,
        compiler_params=_params("parallel"),
    )(w, g, m, v)


def sum_adamw_small(gathered, w, m, v, name):
    rows = w.shape[0]
    tr = _tile(rows, 512)

    def body(p_ref, w_ref, m_ref, v_ref, g_ref, d_ref, mo_ref, vo_ref):
        g = p_ref[0]
        for s in range(1, N_DEV):
            g = g + p_ref[s]
        g_ref[...] = g
        d_ref[...], mo_ref[...], vo_ref[...] = _adamw_math(w_ref[...], g, m_ref[...], v_ref[...])

    spec = pl.BlockSpec((tr, LANES), lambda i: (i, 0))
    return pl.pallas_call(
        body, name=name, grid=(rows // tr,),
        in_specs=[pl.BlockSpec((N_DEV, tr, LANES), lambda i: (0, i, 0)), spec, spec, spec], out_specs=[spec] * 4,
        out_shape=[jax.ShapeDtypeStruct((rows, LANES), F32)] * 4, compiler_params=_params("parallel"))(gathered, w, m, v)


def first_norm(h, g, name):
    T, D = h.shape
    tm = _tile(T, 512)

    def body(h_ref, g_ref, hn_ref):
        hn_ref[...] = _rms(h_ref[...], g_ref[...]).astype(BF16)

    row = _rows(tm, D)
    return pl.pallas_call(
        body, name=name, grid=(T // tm,), in_specs=[row, _full((1, D))], out_specs=row,
        out_shape=jax.ShapeDtypeStruct((T, D), BF16), compiler_params=_params("parallel"))(h, g)


_NT = (((1,), (1,)), ((), ()))
_TN = (((0,), (0,)), ((), ()))


def rowmm(a, w, transposed_w, bias, out_dtype, name, after=None):
    M, K = a.shape
    N = w.shape[0] if transposed_w else w.shape[1]
    tm = _tile(M, 256)
    tn = _tile(N, 512)

    def body(*refs):
        a_ref, w_ref = refs[:2]
        o_ref = refs[-1]
        av = a_ref[...]
        for j in range(N // tn):
            cols = pl.ds(j * tn, tn)
            if transposed_w:
                acc = lax.dot_general(av, w_ref[cols, :], _NT, preferred_element_type=F32)
            else:
                acc = jnp.dot(av, w_ref[:, cols], preferred_element_type=F32)
            if bias is not None:
                acc = acc + refs[2][:, cols]
            o_ref[:, cols] = acc.astype(out_dtype)

    ins, in_specs = [a, w], [_rows(tm, K), _full(w.shape)]
    if bias is not None:
        ins.append(bias)
        in_specs.append(_full((1, N)))
    if after is not None:
        ins.append(after)
        in_specs.append(pl.BlockSpec(memory_space=pl.ANY))
    return pl.pallas_call(
        body, name=name, grid=(M // tm,), in_specs=in_specs, out_specs=_rows(tm, N),
        out_shape=jax.ShapeDtypeStruct((M, N), out_dtype), compiler_params=_params("parallel"))(*ins)


def _rows_times_weight(a_ref, w_ref, acc_ref):
    N = w_ref.shape[1]
    tn = _tile(N, 512)
    av = a_ref[...]
    for j in range(N // tn):
        cols = pl.ds(j * tn, tn)
        acc_ref[:, cols] = jnp.dot(av, w_ref[:, cols], preferred_element_type=F32)


def mm_add_norm(a, w, h, g_post, g_pre, name):
    T, K = a.shape
    D = w.shape[1]
    tm = _tile(T, 256)

    def body(a_ref, w_ref, h_ref, gp_ref, g_ref, y_ref, ho_ref, hn_ref, acc_ref):
        _rows_times_weight(a_ref, w_ref, acc_ref)
        y = acc_ref[...]
        y_ref[...] = y.astype(BF16)
        hv = h_ref[...] + _rms(y, gp_ref[...])
        ho_ref[...] = hv
        hn_ref[...] = _rms(hv, g_ref[...]).astype(BF16)

    row, vec = _rows(tm, D), _full((1, D))
    return pl.pallas_call(
        body, name=name, grid=(T // tm,), in_specs=[_rows(tm, K), _full(w.shape), row, vec, vec],
        out_specs=[row, row, row],
        out_shape=[jax.ShapeDtypeStruct((T, D), BF16), jax.ShapeDtypeStruct((T, D), F32),
                   jax.ShapeDtypeStruct((T, D), BF16)],
        scratch_shapes=[pltpu.VMEM((tm, D), F32)], compiler_params=_params("parallel"))(a, w, h, g_post, g_pre)


def mm_add_norm_loss(a, w, h, g_post, target, name):
    T, K = a.shape
    D = w.shape[1]
    tm = _tile(T, 256)

    def body(a_ref, w_ref, h_ref, gp_ref, t_ref, dh_ref, dy_ref, loss_ref, dgp_ref, acc_ref):
        @pl.when(pl.program_id(0) == 0)
        def _():
            loss_ref[...] = jnp.zeros_like(loss_ref)
            dgp_ref[...] = jnp.zeros_like(dgp_ref)
        _rows_times_weight(a_ref, w_ref, acc_ref)
        y = acc_ref[...]
        err = h_ref[...] + _rms(y, gp_ref[...]) - t_ref[...]
        dh = err * (1.0 / D)
        dh_ref[...] = dh
        loss_ref[...] += jnp.sum(err * err)
        dy, dgp = _rms_bwd(y, gp_ref[...], dh)
        dy_ref[...] = dy.astype(BF16)
        dgp_ref[...] += dgp

    row, vec = _rows(tm, D), _full((1, D))
    return pl.pallas_call(
        body, name=name, grid=(T // tm,), in_specs=[_rows(tm, K), _full(w.shape), row, vec, row],
        out_specs=[row, row, _full((8, LANES)), vec],
        out_shape=[jax.ShapeDtypeStruct((T, D), F32), jax.ShapeDtypeStruct((T, D), BF16),
                   jax.ShapeDtypeStruct((8, LANES), F32), jax.ShapeDtypeStruct((1, D), F32)],
        scratch_shapes=[pltpu.VMEM((tm, D), F32)], compiler_params=_params("arbitrary"))(a, w, h, g_post, target)


def mm_norm_bwd(a, w, dh_in, x, g_pre, post, name, after):
    T, K = a.shape
    D = w.shape[1]
    tm = _tile(T, 256)
    row, vec = _rows(tm, D), _full((1, D))
    ins, in_specs = [a, w, dh_in, x, g_pre], [_rows(tm, K), _full(w.shape), row, row, vec]
    outs = [jax.ShapeDtypeStruct((T, D), F32), jax.ShapeDtypeStruct((1, D), F32)]
    out_specs, names = [row, vec], ["dh", "dg_pre"]
    if post is not None:
        ins += list(post)
        in_specs += [row, vec]
        outs += [jax.ShapeDtypeStruct((T, D), BF16), jax.ShapeDtypeStruct((1, D), F32)]
        out_specs += [row, vec]
        names += ["dy", "dg_post"]
    n_in = len(ins) + 1

    def body(*refs):
        a_ref, w_ref, dh_ref_in, x_ref, g_ref = refs[:5]
        out_refs, acc_ref = refs[n_in:-1], refs[-1]
        first = pl.program_id(0) == 0
        _rows_times_weight(a_ref, w_ref, acc_ref)
        dx, dg = _rms_bwd(x_ref[...], g_ref[...], acc_ref[...])
        dh = dh_ref_in[...] + dx
        out_refs[0][...] = dh

        @pl.when(first)
        def _():
            for r in out_refs[1::2]:
                r[...] = jnp.zeros_like(r)
        out_refs[1][...] += dg
        if post is not None:
            y_ref, gp_ref = refs[5:7]
            dy, dgp = _rms_bwd(y_ref[...].astype(F32), gp_ref[...], dh)
            out_refs[2][...] = dy.astype(BF16)
            out_refs[3][...] += dgp

    res = pl.pallas_call(
        body, name=name, grid=(T // tm,), in_specs=in_specs + [pl.BlockSpec(memory_space=pl.ANY)],
        out_specs=out_specs, out_shape=outs, scratch_shapes=[pltpu.VMEM((tm, D), F32)],
        compiler_params=_params("arbitrary"))(*ins, after)
    return dict(zip(names, res))


def grad_mm(a, b, name, after=()):
    T, N = a.shape
    K = b.shape[1]
    tn = _tile(N, 1408)
    tt = _tile(T, 1024)

    def body(a_ref, b_ref, *rest):
        o_ref, acc_ref = rest[-2:]
        t = pl.program_id(1)

        @pl.when(t == 0)
        def _():
            acc_ref[...] = jnp.zeros_like(acc_ref)
        acc_ref[...] += lax.dot_general(a_ref[...], b_ref[...], _TN, preferred_element_type=F32)

        @pl.when(t == pl.num_programs(1) - 1)
        def _():
            o_ref[...] = acc_ref[...].astype(BF16)

    return pl.pallas_call(
        body, name=name, grid=(N // tn, T // tt),
        in_specs=[pl.BlockSpec((tt, tn), lambda j, t: (t, j)), pl.BlockSpec((tt, K), lambda j, t: (t, 0))]
        + [pl.BlockSpec(memory_space=pl.ANY)] * len(after),
        out_specs=pl.BlockSpec((tn, K), lambda j, t: (j, 0)),
        out_shape=jax.ShapeDtypeStruct((N, K), BF16),
        scratch_shapes=[pltpu.VMEM((tn, K), F32)],
        compiler_params=_params("parallel", "arbitrary"))(a, b, *after)


def ffn_up(fn, w_gu_t, name):
    T, D = fn.shape
    F = w_gu_t.shape[0] // 2
    tm = _tile(T, 256)
    tn = _tile(F, 512)

    def body(a_ref, w_ref, act_ref, part_ref):
        av = a_ref[...]
        for j in range(F // tn):
            g = lax.dot_general(av, w_ref[pl.ds(j * tn, tn), :], _NT, preferred_element_type=F32)
            up = lax.dot_general(av, w_ref[pl.ds(F + j * tn, tn), :], _NT, preferred_element_type=F32)
            sg = _sigmoid(g)
            silu = g * sg
            act_ref[:, pl.ds(j * tn, tn)] = (silu * up).astype(BF16)
            part_ref[:, pl.ds(j * tn, tn)] = (up * (sg + silu * (1.0 - sg))).astype(BF16)
            part_ref[:, pl.ds(F + j * tn, tn)] = silu.astype(BF16)

    return pl.pallas_call(
        body, name=name, grid=(T // tm,), in_specs=[_rows(tm, D), _full(w_gu_t.shape)],
        out_specs=[_rows(tm, F), _rows(tm, 2 * F)],
        out_shape=[jax.ShapeDtypeStruct((T, F), BF16), jax.ShapeDtypeStruct((T, 2 * F), BF16)],
        compiler_params=_params("parallel"))(fn, w_gu_t)


def ffn_dact(df, w_down, partials, name, after=()):
    T, D = df.shape
    F = w_down.shape[0]
    tm = _tile(T, 256)
    tn = _tile(F, 512)

    def body(d_ref, w_ref, part_ref, *rest):
        o_ref = rest[-1]
        dv = d_ref[...]
        for j in range(F // tn):
            dact = lax.dot_general(dv, w_ref[pl.ds(j * tn, tn), :], _NT, preferred_element_type=F32)
            for cols in (pl.ds(j * tn, tn), pl.ds(F + j * tn, tn)):
                o_ref[:, cols] = (dact * part_ref[:, cols].astype(F32)).astype(BF16)

    return pl.pallas_call(
        body, name=name, grid=(T // tm,),
        in_specs=[_rows(tm, D), _full(w_down.shape), _rows(tm, 2 * F)] + [pl.BlockSpec(memory_space=pl.ANY)] * len(after),
        out_specs=_rows(tm, 2 * F), out_shape=jax.ShapeDtypeStruct((T, 2 * F), BF16),
        compiler_params=_params("parallel"))(df, w_down, partials, *after)


def _layernorm_stats(v):
    mu = jnp.mean(v, axis=-1, keepdims=True)
    cen = v - mu
    rstd = lax.rsqrt(jnp.mean(cen * cen, axis=-1, keepdims=True) + LN_EPS)
    return cen * rstd, rstd


GMLP_ROWS = 2 * CHUNK


def gmlp_gate_out(a, ln_g, ln_b, w_mask, b_s_t, w_out, h, g_post, g_pre, name):
    T, W2 = a.shape
    W = W2 // 2
    G = w_mask.shape[0]
    C = W // G
    D = w_out.shape[1]
    tm = min(T, GMLP_ROWS)

    def body(a_ref, lg_ref, lb_ref, w_ref, bs_ref, wo_ref, h_ref, gp_ref, g_ref,
             gated_ref, y_ref, ho_ref, hn_ref, acc_ref):
        for c in range(tm // CHUNK):
            rows = pl.ds(c * CHUNK, CHUNK)
            xhat, _ = _layernorm_stats(_gelu(a_ref[rows, W:].astype(F32)))
            vln = (xhat * lg_ref[...] + lb_ref[...]).astype(BF16)
            for g in range(G):
                cols = pl.ds(g * C, C)
                sv = jnp.dot(w_ref[g], vln[:, g * C:(g + 1) * C], preferred_element_type=F32) + bs_ref[:, g:g + 1]
                gated_ref[rows, cols] = (_gelu(a_ref[rows, cols].astype(F32)) * sv).astype(BF16)
        _rows_times_weight(gated_ref, wo_ref, acc_ref)
        y = acc_ref[...]
        y_ref[...] = y.astype(BF16)
        hv = h_ref[...] + _rms(y, gp_ref[...])
        ho_ref[...] = hv
        hn_ref[...] = _rms(hv, g_ref[...]).astype(BF16)

    row, vec = _rows(tm, D), _full((1, D))
    return pl.pallas_call(
        body, name=name, grid=(T // tm,),
        in_specs=[_rows(tm, W2), _full((1, W)), _full((1, W)), _full(w_mask.shape), _full(b_s_t.shape),
                  _full(w_out.shape), row, vec, vec],
        out_specs=[_rows(tm, W), row, row, row],
        out_shape=[jax.ShapeDtypeStruct((T, W), BF16), jax.ShapeDtypeStruct((T, D), BF16),
                   jax.ShapeDtypeStruct((T, D), F32), jax.ShapeDtypeStruct((T, D), BF16)],
        scratch_shapes=[pltpu.VMEM((tm, D), F32)],
        compiler_params=_params("parallel"))(a, ln_g, ln_b, w_mask, b_s_t, w_out, h, g_post, g_pre)


def gmlp_gate_bwd_dx(a, dmix, w_out, ln_g, ln_b, w_mask, w_mask_t, b_s_t, group_onehot, w_in_t,
                     dh_in, x, g_pre, post, name, after):
    T, W2 = a.shape
    W = W2 // 2
    G = w_mask.shape[0]
    C = W // G
    D = w_out.shape[1]
    tm = min(T, GMLP_ROWS)
    tn = _tile(W, 512)
    row, vec = _rows(tm, D), _full((1, D))
    ins = [a, dmix, w_out, ln_g, ln_b, w_mask, w_mask_t, b_s_t, group_onehot, w_in_t, dh_in, x, g_pre]
    in_specs = [_rows(tm, W2), row, _full(w_out.shape), _full((1, W)), _full((1, W)), _full(w_mask.shape),
                _full(w_mask_t.shape), _full(b_s_t.shape), _full(group_onehot.shape), _full(w_in_t.shape), row, row, vec]
    names = ["da", "dw_s", "db_s_t", "dln_g", "dln_b", "db_in", "dh", "dg_pre"]
    outs = [jax.ShapeDtypeStruct((T, W2), BF16), jax.ShapeDtypeStruct((G, CHUNK, CHUNK), F32),
            jax.ShapeDtypeStruct((CHUNK, LANES), F32), jax.ShapeDtypeStruct((1, W), F32),
            jax.ShapeDtypeStruct((1, W), F32), jax.ShapeDtypeStruct((1, W2), F32),
            jax.ShapeDtypeStruct((T, D), F32), jax.ShapeDtypeStruct((1, D), F32)]
    out_specs = [_rows(tm, W2), _full((G, CHUNK, CHUNK)), _full((CHUNK, LANES)), _full((1, W)), _full((1, W)),
                 _full((1, W2)), row, vec]
    if post is not None:
        ins += list(post)
        in_specs += [row, vec]
        names += ["dy", "dg_post"]
        outs += [jax.ShapeDtypeStruct((T, D), BF16), jax.ShapeDtypeStruct((1, D), F32)]
        out_specs += [row, vec]
    n_in = len(ins) + 1

    def body(*refs):
        (a_ref, dm_ref, wo_ref, lg_ref, lb_ref, w_ref, wt_ref, bs_ref, e_ref, wi_ref, dh_in_ref, x_ref, gpre_ref) = refs[:13]
        out_refs = refs[n_in:-3]
        da_ref, dws_ref, dbs_ref, dlg_ref, dlb_ref, dbin_ref, dh_ref, dgpre_ref = out_refs[:8]
        dgated_ref, dvln_ref, acc_ref = refs[-3:]

        @pl.when(pl.program_id(0) == 0)
        def _():
            for r in out_refs[1:6] + out_refs[7::2]:
                r[...] = jnp.zeros_like(r)
        dm = dm_ref[...]
        for j in range(W // tn):
            cols = pl.ds(j * tn, tn)
            dgated_ref[:, cols] = lax.dot_general(dm, wo_ref[cols, :], _NT, preferred_element_type=F32)
        causal = (lax.broadcasted_iota(jnp.int32, (CHUNK, CHUNK), 1)
                  <= lax.broadcasted_iota(jnp.int32, (CHUNK, CHUNK), 0))
        for c in range(tm // CHUNK):
            rows = pl.ds(c * CHUNK, CHUNK)
            gelu_v, gelu_grad_v = _gelu_and_grad(a_ref[rows, W:].astype(F32))
            xhat, rstd = _layernorm_stats(gelu_v)
            vln = (xhat * lg_ref[...] + lb_ref[...]).astype(BF16)
            for g in range(G):
                cols = pl.ds(g * C, C)
                vg = vln[:, g * C:(g + 1) * C]
                gelu_u, gelu_grad_u = _gelu_and_grad(a_ref[rows, cols].astype(F32))
                dgated = dgated_ref[rows, cols]
                dsv = (dgated * gelu_u).astype(BF16)
                dbs_ref[...] += jnp.dot(dsv, e_ref[cols, :], preferred_element_type=F32)
                sv = jnp.dot(w_ref[g], vg, preferred_element_type=F32) + bs_ref[:, g:g + 1]
                dau = dgated * sv * gelu_grad_u
                da_ref[rows, cols] = dau.astype(BF16)
                dbin_ref[:, cols] += jnp.sum(dau, axis=0, keepdims=True)
                dws_ref[g] += jnp.where(causal, lax.dot_general(dsv, vg, _NT, preferred_element_type=F32), 0.0)
                dvln_ref[:, cols] = jnp.dot(wt_ref[g], dsv, preferred_element_type=F32)
            dvln = dvln_ref[...]
            dlg_ref[...] += jnp.sum(dvln * xhat, axis=0, keepdims=True)
            dlb_ref[...] += jnp.sum(dvln, axis=0, keepdims=True)
            dxhat = dvln * lg_ref[...]
            dv = rstd * (dxhat - jnp.mean(dxhat, axis=-1, keepdims=True)
                         - xhat * jnp.mean(dxhat * xhat, axis=-1, keepdims=True))
            dav = dv * gelu_grad_v
            da_ref[rows, W:] = dav.astype(BF16)
            dbin_ref[:, W:] += jnp.sum(dav, axis=0, keepdims=True)
        _rows_times_weight(da_ref, wi_ref, acc_ref)
        dx, dg = _rms_bwd(x_ref[...], gpre_ref[...], acc_ref[...])
        dh = dh_in_ref[...] + dx
        dh_ref[...] = dh
        dgpre_ref[...] += dg
        if post is not None:
            y_ref, gp_ref = refs[13:15]
            dy, dgp = _rms_bwd(y_ref[...].astype(F32), gp_ref[...], dh)
            out_refs[8][...] = dy.astype(BF16)
            out_refs[9][...] += dgp

    res = pl.pallas_call(
        body, name=name, grid=(T // tm,), in_specs=in_specs + [pl.BlockSpec(memory_space=pl.ANY)],
        out_specs=out_specs, out_shape=outs,
        scratch_shapes=[pltpu.VMEM((tm, W), F32), pltpu.VMEM((CHUNK, W), F32), pltpu.VMEM((tm, D), F32)],
        compiler_params=_params("arbitrary"))(*ins, after)
    return dict(zip(names, res))


def rope_tables(pos, inv_freq_row, name):
    T = pos.shape[0]
    tm = _tile(T, 512)

    def body(p_ref, f_ref, c_ref, s1_ref, s2_ref):
        ang = p_ref[...].astype(F32) * f_ref[...]
        lane = lax.broadcasted_iota(jnp.int32, (tm, LANES), 1) % HEAD_DIM
        sin = jnp.sin(ang)
        c_ref[...] = jnp.cos(ang)
        s1_ref[...] = jnp.where(lane < _HALF, -sin, 0.0)
        s2_ref[...] = jnp.where((lane >= _HALF) & (lane < ROPE_DIM), sin, 0.0)

    tab = _rows(tm, LANES)
    return pl.pallas_call(
        body, name=name, grid=(T // tm,), in_specs=[_rows(tm, 1), _full((1, LANES))], out_specs=[tab] * 3,
        out_shape=[jax.ShapeDtypeStruct((T, LANES), F32)] * 3, compiler_params=_params("parallel"))(pos, inv_freq_row)


_HALF = ROPE_DIM // 2


def _slabs(x):
    return [x[:, b * LANES:(b + 1) * LANES] for b in range(x.shape[1] // LANES)]


def _rotate(x, c, s1, s2):
    return [xs * c + pltpu.roll(xs, LANES - _HALF, 1) * s1 + pltpu.roll(xs, _HALF, 1) * s2 for xs in _slabs(x)]


def _rotate_transposed(dy, c, s1, s2):
    return [ds * c + pltpu.roll(ds * s1, _HALF, 1) + pltpu.roll(ds * s2, LANES - _HALF, 1) for ds in _slabs(dy)]


def rope_fwd(qkv, tabs, q_width, kv_width, name):
    T, QKV = qkv.shape
    tm = _tile(T, 512)
    QK = q_width + kv_width
    scale = HEAD_DIM ** -0.5

    def body(x_ref, c_ref, s1_ref, s2_ref, o_ref):
        x = x_ref[:, :QK].astype(F32)
        slabs = _rotate(x, c_ref[...], s1_ref[...], s2_ref[...])
        for b, y in enumerate(slabs):
            if b * LANES < q_width:
                y = y * scale
            o_ref[:, b * LANES:(b + 1) * LANES] = y.astype(BF16)

    tab = _rows(tm, LANES)
    return pl.pallas_call(
        body, name=name, grid=(T // tm,), in_specs=[_rows(tm, QKV), tab, tab, tab], out_specs=_rows(tm, QK),
        out_shape=jax.ShapeDtypeStruct((T, QK), BF16), compiler_params=_params("parallel"))(qkv, *tabs)


def rope_bwd(dq, dk, dv, tabs, name):
    T, Q = dq.shape
    KV = dk.shape[1]
    tm = _tile(T, 512)
    scale = HEAD_DIM ** -0.5

    def body(dq_ref, dk_ref, dv_ref, c_ref, s1_ref, s2_ref, o_ref, b_ref):
        @pl.when(pl.program_id(0) == 0)
        def _():
            b_ref[...] = jnp.zeros_like(b_ref)
        tabs_v = (c_ref[...], s1_ref[...], s2_ref[...])
        pieces = [s * scale for s in _rotate_transposed(dq_ref[...], *tabs_v)]
        pieces += _rotate_transposed(dk_ref[...], *tabs_v)
        pieces += _slabs(dv_ref[...])
        for b, y in enumerate(pieces):
            cols = pl.ds(b * LANES, LANES)
            o_ref[:, cols] = y.astype(BF16)
            b_ref[:, cols] += jnp.sum(y, axis=0, keepdims=True)

    tab = _rows(tm, LANES)
    return pl.pallas_call(
        body, name=name, grid=(T // tm,), in_specs=[_rows(tm, Q), _rows(tm, KV), _rows(tm, KV), tab, tab, tab],
        out_specs=[_rows(tm, Q + 2 * KV), _full((1, Q + 2 * KV))],
        out_shape=[jax.ShapeDtypeStruct((T, Q + 2 * KV), BF16), jax.ShapeDtypeStruct((1, Q + 2 * KV), F32)],
        compiler_params=_params("arbitrary"))(dq, dk, dv, *tabs)


def _band_mask(n, heads=1):
    qi = lax.broadcasted_iota(jnp.int32, (heads * CHUNK, 2 * CHUNK), 0) % CHUNK
    sj = lax.broadcasted_iota(jnp.int32, (heads * CHUNK, 2 * CHUNK), 1)
    return (sj > qi) & (sj <= qi + CHUNK) & ((n > 0) | (sj >= CHUNK))


def _kv_head(kc_ref, kp_ref, vc_ref, vp_ref, kh):
    lanes = slice(kh * HEAD_DIM, (kh + 1) * HEAD_DIM)
    return (jnp.concatenate([kp_ref[:, lanes], kc_ref[:, lanes]], axis=0),
            jnp.concatenate([vp_ref[:, lanes], vc_ref[:, lanes]], axis=0))


def _head_probs(q, kk, valid, sink):
    s = jnp.where(valid, lax.dot_general(q, kk, _NT, preferred_element_type=F32), NEG_INF)
    m = jnp.maximum(jnp.max(s, axis=-1, keepdims=True), sink)
    p = jnp.exp(s - m)
    e_sink = jnp.exp(sink - m)
    return p, 1.0 / (jnp.sum(p, axis=-1, keepdims=True) + e_sink), e_sink


def _attn_specs(q_width, kv_width, order):
    qb = q_width // kv_width
    prev = lambda i: jnp.maximum(order(i) - 1, 0)
    return [pl.BlockSpec((CHUNK, q_width), lambda i: (order(i), 0)),
            pl.BlockSpec((CHUNK, kv_width), lambda i: (order(i), qb)),
            pl.BlockSpec((CHUNK, kv_width), lambda i: (prev(i), qb)),
            pl.BlockSpec((CHUNK, kv_width), lambda i: (order(i), qb + 1)),
            pl.BlockSpec((CHUNK, kv_width), lambda i: (prev(i), qb + 1))]


def attn_fwd(qk, qkv, sinks, q_width, kv_width, name):
    T = qk.shape[0]
    group = q_width // kv_width

    def body(q_ref, kc_ref, kp_ref, vc_ref, vp_ref, sink_ref, o_ref):
        valid = _band_mask(pl.program_id(0))
        for kh in range(kv_width // HEAD_DIM):
            kk, vv = _kv_head(kc_ref, kp_ref, vc_ref, vp_ref, kh)
            for g in range(group):
                h = kh * group + g
                lanes = slice(h * HEAD_DIM, (h + 1) * HEAD_DIM)
                p, inv, _ = _head_probs(q_ref[:, lanes], kk, valid, sink_ref[h])
                o_ref[:, lanes] = (jnp.dot(p.astype(BF16), vv, preferred_element_type=F32) * inv).astype(BF16)

    specs = _attn_specs(q_width, kv_width, lambda i: i)
    return pl.pallas_call(
        body, name=name, grid=(T // CHUNK,),
        in_specs=specs + [pl.BlockSpec(memory_space=pltpu.SMEM)],
        out_specs=_rows(CHUNK, q_width), out_shape=jax.ShapeDtypeStruct((T, q_width), BF16),
        compiler_params=_params("parallel"))(qk, qk, qk, qkv, qkv, sinks)


def attn_bwd(qk, qkv, do, sinks, q_width, kv_width, name):
    T = qk.shape[0]
    NB = T // CHUNK
    group = q_width // kv_width

    def body(q_ref, kc_ref, kp_ref, vc_ref, vp_ref, do_ref, sink_ref, dq_ref, dk_ref, dv_ref, ds_ref, ck_ref, cv_ref):
        i = pl.program_id(0)
        n = NB - 1 - i

        @pl.when(i == 0)
        def _():
            ck_ref[...] = jnp.zeros_like(ck_ref)
            cv_ref[...] = jnp.zeros_like(cv_ref)
            ds_ref[...] = jnp.zeros_like(ds_ref)
        lane = lax.broadcasted_iota(jnp.int32, (1, LANES), 1)
        dsink_row = jnp.zeros((1, LANES), F32)
        valid = _band_mask(n, group)
        head = lax.broadcasted_iota(jnp.int32, (group * CHUNK, 1), 0) // CHUNK
        for kh in range(kv_width // HEAD_DIM):
            kk, vv = _kv_head(kc_ref, kp_ref, vc_ref, vp_ref, kh)
            heads = [slice((kh * group + g) * HEAD_DIM, (kh * group + g + 1) * HEAD_DIM) for g in range(group)]
            q = jnp.concatenate([q_ref[:, hs] for hs in heads], axis=0)
            do = jnp.concatenate([do_ref[:, hs] for hs in heads], axis=0)
            sink = jnp.zeros((group * CHUNK, 1), F32)
            for g in range(group):
                sink = jnp.where(head == g, sink_ref[kh * group + g], sink)
            p, inv, e_sink = _head_probs(q, kk, valid, sink)
            p = p * inv
            dp = lax.dot_general(do, vv, _NT, preferred_element_type=F32)
            delta = jnp.sum(p * dp, axis=-1, keepdims=True)
            ds = (p * (dp - delta)).astype(BF16)
            dsink = -e_sink * inv * delta
            dq = jnp.dot(ds, kk, preferred_element_type=F32)
            for g, hs in enumerate(heads):
                rows = slice(g * CHUNK, (g + 1) * CHUNK)
                dsink_row = dsink_row + jnp.where(lane == kh * group + g, jnp.sum(dsink[rows]), 0.0)
                dq_ref[:, hs] = dq[rows]
            dkk = lax.dot_general(ds, q, _TN, preferred_element_type=F32)
            dvv = lax.dot_general(p.astype(BF16), do, _TN, preferred_element_type=F32)
            lanes = slice(kh * HEAD_DIM, (kh + 1) * HEAD_DIM)
            dk_ref[:, lanes] = dkk[CHUNK:] + ck_ref[:, lanes]
            dv_ref[:, lanes] = dvv[CHUNK:] + cv_ref[:, lanes]
            ck_ref[:, lanes] = dkk[:CHUNK]
            cv_ref[:, lanes] = dvv[:CHUNK]
        ds_ref[0:1, :] += dsink_row

    order = lambda i: NB - 1 - i
    specs = _attn_specs(q_width, kv_width, order)
    kv_out = pl.BlockSpec((CHUNK, kv_width), lambda i: (order(i), 0))
    q_rows = pl.BlockSpec((CHUNK, q_width), lambda i: (order(i), 0))
    return pl.pallas_call(
        body, name=name, grid=(NB,),
        in_specs=specs + [q_rows, pl.BlockSpec(memory_space=pltpu.SMEM)],
        out_specs=[q_rows, kv_out, kv_out, _full((8, LANES))],
        out_shape=[jax.ShapeDtypeStruct((T, q_width), F32), jax.ShapeDtypeStruct((T, kv_width), F32),
                   jax.ShapeDtypeStruct((T, kv_width), F32), jax.ShapeDtypeStruct((8, LANES), F32)],
        scratch_shapes=[pltpu.VMEM((CHUNK, kv_width), F32), pltpu.VMEM((CHUNK, kv_width), F32)],
        compiler_params=_params("arbitrary"))(qk, qk, qk, qkv, qkv, do, sinks)


def _blocked(w):
    return w.reshape(N_DEV, w.shape[0] // N_DEV, w.shape[1])


def kernel(x, positions, pre_mix_g, post_mix_g, pre_ffn_g, post_ffn_g, a_w_in, a_b_in, a_ln_g, a_ln_b, a_w_s, a_b_s, a_w_out, b_w_qkv, b_b_qkv, b_sinks, b_w_o, ffn_w_gu, ffn_w_down, loss_target, m_pre_mix_g, m_post_mix_g, m_pre_ffn_g, m_post_ffn_g, m_a_w_in, m_a_b_in, m_a_ln_g, m_a_ln_b, m_a_w_s, m_a_b_s, m_a_w_out, m_b_w_qkv, m_b_b_qkv, m_b_sinks, m_b_w_o, m_ffn_w_gu, m_ffn_w_down, v_pre_mix_g, v_post_mix_g, v_pre_ffn_g, v_post_ffn_g, v_a_w_in, v_a_b_in, v_a_ln_g, v_a_ln_b, v_a_w_s, v_a_b_s, v_a_w_out, v_b_w_qkv, v_b_b_qkv, v_b_sinks, v_b_w_o, v_ffn_w_gu, v_ffn_w_down):
    weights = dict(pre_mix_g=pre_mix_g, post_mix_g=post_mix_g, pre_ffn_g=pre_ffn_g, post_ffn_g=post_ffn_g,
                   a_w_in=a_w_in, a_b_in=a_b_in, a_ln_g=a_ln_g, a_ln_b=a_ln_b, a_w_s=a_w_s, a_b_s=a_b_s,
                   a_w_out=a_w_out, b_w_qkv=b_w_qkv, b_b_qkv=b_b_qkv, b_sinks=b_sinks, b_w_o=b_w_o,
                   ffn_w_gu=ffn_w_gu, ffn_w_down=ffn_w_down)
    mom_m = dict(pre_mix_g=m_pre_mix_g, post_mix_g=m_post_mix_g, pre_ffn_g=m_pre_ffn_g, post_ffn_g=m_post_ffn_g,
                 a_w_in=m_a_w_in, a_b_in=m_a_b_in, a_ln_g=m_a_ln_g, a_ln_b=m_a_ln_b, a_w_s=m_a_w_s, a_b_s=m_a_b_s,
                 a_w_out=m_a_w_out, b_w_qkv=m_b_w_qkv, b_b_qkv=m_b_b_qkv, b_sinks=m_b_sinks, b_w_o=m_b_w_o,
                 ffn_w_gu=m_ffn_w_gu, ffn_w_down=m_ffn_w_down)
    mom_v = dict(pre_mix_g=v_pre_mix_g, post_mix_g=v_post_mix_g, pre_ffn_g=v_pre_ffn_g, post_ffn_g=v_post_ffn_g,
                 a_w_in=v_a_w_in, a_b_in=v_a_b_in, a_ln_g=v_a_ln_g, a_ln_b=v_a_ln_b, a_w_s=v_a_w_s, a_b_s=v_a_b_s,
                 a_w_out=v_a_w_out, b_w_qkv=v_b_w_qkv, b_b_qkv=v_b_b_qkv, b_sinks=v_b_sinks, b_w_o=v_b_w_o,
                 ffn_w_gu=v_ffn_w_gu, ffn_w_down=v_ffn_w_down)
    names = list(weights)
    big = ("a_w_in", "b_w_qkv", "ffn_w_gu", "a_w_out", "b_w_o", "ffn_w_down")

    T, D = x.shape[1], x.shape[2]
    depth = pre_mix_g.shape[0]
    n_heads = b_sinks.shape[1]
    q_width = n_heads * HEAD_DIM
    kv_width = N_KV_HEADS * HEAD_DIM
    G = a_w_s.shape[1]
    W = a_ln_g.shape[1]
    device = _index(_position())

    dev = device.reshape(1).astype(jnp.int32)

    def layer_keys(i):
        mixer = ("a_w_in", "a_w_out") if i % 2 == 0 else ("b_w_qkv", "b_w_o")
        return [(k, i // 2) for k in mixer] + [("ffn_w_gu", i), ("ffn_w_down", i)]

    as_view = ("b_w_qkv", "ffn_w_gu")
    in_kernel = ("a_w_in",)
    view = lambda k, t: jnp.swapaxes(t, 1, 2) if k in as_view else t
    w_view = {k: view(k, weights[k]) for k in big}
    m_view = {k: view(k, mom_m[k]) for k in big}
    v_view = {k: view(k, mom_v[k]) for k in big}

    groups_keys = [[key] for key in layer_keys(0)] + [layer_keys(i) for i in range(1, depth)]
    group_of = {}
    for g, keys in enumerate(groups_keys):
        layer = 0 if g < len(layer_keys(0)) else g - len(layer_keys(0)) + 1
        for pos, (k, _) in enumerate(keys):
            group_of[(layer, k)] = (g, pos)
    bias_land = lax.dynamic_update_slice(
        jnp.zeros((N_DEV,) + b_b_qkv.shape, F32), b_b_qkv[None], (device, 0, 0))
    h, hn = x[0], first_norm(x[0], pre_mix_g[0][None], "norm_first")
    first = len(layer_keys(0))
    in_gather, token = [], ()
    for name, lo, hi in (("gather_start_first", 0, first), ("gather_start_rest", first, len(groups_keys))):
        groups = [list(prep_weights([(w_view[k], l, k in in_kernel) for k, l in keys], dev, f"prep_{name}_{g}",
                                    after=token)) for g, keys in enumerate(groups_keys[lo:hi])]
        if lo:
            groups[0].append(bias_land)
        gather_sems, flat_lands, token = relay_start(groups, name)
        token = (token, hn)
        for grp, sems in zip(groups, gather_sems):
            in_gather.append(([flat_lands.pop(0) for _ in grp], sems))
    all_started = token[0]
    passed, gathered = {}, {}

    def pass_on(g, after):
        if g not in passed and g < len(in_gather):
            lands, sems = in_gather[g]
            passed[g], lands = relay_pass(lands, sems, after, f"gather_pass_{g}")
            in_gather[g] = (lands, sems)

    pass_on(0, (all_started,))

    def weight(i, k, *after):
        g, pos = group_of[(i, k)]
        if g not in gathered:
            pass_on(g + 1, after)
            lands, sems = in_gather[g]
            gathered[g] = relay_wait(lands, sems, passed[g], after, f"gather_wait_{g}")
        w = gathered[g][pos]
        return w.reshape(N_DEV * w.shape[1], w.shape[2])

    causal = jnp.tril(jnp.ones((CHUNK, CHUNK), dtype=bool))
    w_mask = jnp.where(causal[None, None], a_w_s, 0.0).astype(BF16)
    w_mask_t = jnp.swapaxes(w_mask, 2, 3)
    b_s_t = jnp.swapaxes(a_b_s, 1, 2)
    group_onehot = (jnp.arange(W)[:, None] // (W // G) == jnp.arange(LANES)[None, :]).astype(BF16)
    lane = jnp.arange(LANES) % HEAD_DIM
    inv_freq = ROPE_THETA ** (-jnp.arange(0, ROPE_DIM, 2, dtype=F32) / ROPE_DIM)
    inv_freq_row = jnp.where(lane < ROPE_DIM, inv_freq[lane % (ROPE_DIM // 2)], 0.0)[None, :].astype(F32)
    tabs = rope_tables(positions.reshape(T, 1), inv_freq_row, "rope_tables")

    saved = []
    for i in range(depth):
        j = i // 2
        s = dict(h=h, hn=hn)
        post_pre = (post_mix_g[i][None], pre_ffn_g[i][None])
        if i % 2 == 0:
            s["a"] = rowmm(hn, weight(i, "a_w_in", hn, all_started), True, a_b_in[j][None], BF16, "gmlp_in")
            s["gated"], s["mix"], s["h2"], s["fn"] = gmlp_gate_out(
                s["a"], a_ln_g[j][None], a_ln_b[j][None], w_mask[j], b_s_t[j], weight(i, "a_w_out", s["a"]),
                h, *post_pre, "gmlp_gate_out")
        else:
            w_qkv = weight(i, "b_w_qkv", hn)
            b_qkv_full = jnp.swapaxes(gathered[group_of[(1, "b_w_qkv")][0]][-1], 0, 1).reshape(b_b_qkv.shape[0], 1, -1)
            s["qkv"] = rowmm(hn, w_qkv, True, b_qkv_full[j], BF16, "attn_qkv")
            s["qk"] = rope_fwd(s["qkv"], tabs, q_width, kv_width, "attn_rope")
            s["o"] = attn_fwd(s["qk"], s["qkv"], b_sinks[j], q_width, kv_width, "attn_core")
            s["mix"], s["h2"], s["fn"] = mm_add_norm(s["o"], weight(i, "b_w_o", s["o"]), h, *post_pre, "attn_out")
        s["act"], s["act_partials"] = ffn_up(s["fn"], weight(i, "ffn_w_gu", s["fn"]), "ffn_up")
        w_down = weight(i, "ffn_w_down", s["act"])
        saved.append(s)
        if i + 1 < depth:
            s["f"], h, hn = mm_add_norm(s["act"], w_down, s["h2"], post_ffn_g[i][None], pre_mix_g[i + 1][None], "ffn_down")
    small = {k: [None] * weights[k].shape[0] for k in names if k not in big}
    dh, df, loss_sum, small["post_ffn_g"][depth - 1] = mm_add_norm_loss(
        saved[-1]["act"], w_down, saved[-1]["h2"], post_ffn_g[depth - 1][None], loss_target[0], "ffn_down_loss")
    loss = lax.psum(loss_sum[0, 0] * (0.5 / D), ("x", "y", "c"))

    grads = {k: [None] * weights[k].shape[0] for k in big}
    in_flight = []

    def send_grads(keys, tag):
        sems, bufs, token = exchange_start([_blocked(grads[k][l]) for k, l in keys], "exchange_start_" + tag)
        in_flight.append((keys, sems, bufs, tag))
        return token

    replicated = [k for k in names if k not in big and k not in ("b_b_qkv", "a_w_s")]
    main = sum(weights[k].size for k in replicated)
    main_rows = -(-main // (LANES * LANES)) * LANES
    bias_size = b_b_qkv.shape[0] * N_DEV * b_b_qkv.shape[1]
    bias_rows = -(-bias_size // (8 * LANES)) * 8
    flat = lambda vals: [v.reshape(-1) for v in vals]

    def start_small_gather():
        packed = jnp.concatenate(
            flat(p for k in replicated for p in small[k]) + [jnp.zeros((main_rows * LANES - main,), F32)]
            + flat(small["b_b_qkv"]) + [jnp.zeros((bias_rows * LANES - bias_size,), F32)]
        ).reshape(1, main_rows + bias_rows, LANES)
        spatial = jnp.stack(small["a_w_s"]).reshape(1, -1, LANES)
        return gather_start([prep_weights([(packed, 0, False), (spatial, 0, False)], dev, "place_small_grads", F32)],
                            "gather_start_small")

    sent = ()
    for i in reversed(range(depth)):
        j = i // 2
        s = saved[i]
        dgu = ffn_dact(df, weight(i, "ffn_w_down"), s["act_partials"], "ffn_dact", after=sent)
        alone = i == 0
        grads["ffn_w_down"][i] = grad_mm(s["act"], df, "ffn_down_grad")
        early = (send_grads(layer_keys(i)[3:], f"ffn_down_{i}"),) if alone else ()
        grads["ffn_w_gu"][i] = grad_mm(dgu, s["fn"], "ffn_up_grad", after=early)
        sent = send_grads(layer_keys(i)[2:3] if alone else layer_keys(i)[2:], f"ffn_{i}")
        r = mm_norm_bwd(dgu, weight(i, "ffn_w_gu"), dh, s["h2"], pre_ffn_g[i][None],
                        (s["mix"], post_mix_g[i][None]), "ffn_dx", sent)
        dh, dmix = r["dh"], r["dy"]
        small["pre_ffn_g"][i], small["post_mix_g"][i] = r["dg_pre"], r["dg_post"]
        post = (saved[i - 1]["f"], post_ffn_g[i - 1][None]) if i > 0 else None
        if i % 2 == 0:
            grads["a_w_out"][j] = grad_mm(s["gated"], dmix, "gmlp_out_grad")
            if alone:
                sent = send_grads(layer_keys(i)[1:2], f"mixer_out_{i}")
            r = gmlp_gate_bwd_dx(
                s["a"], dmix, weight(i, "a_w_out"), a_ln_g[j][None], a_ln_b[j][None], w_mask[j], w_mask_t[j], b_s_t[j],
                group_onehot, weight(i, "a_w_in"), dh, s["h"], pre_mix_g[i][None], post, "gmlp_gate_bwd_dx", sent)
            small["a_w_s"][j], small["a_b_s"][j] = r["dw_s"], r["db_s_t"][:, :G].T
            small["a_ln_g"][j], small["a_ln_b"][j], small["a_b_in"][j] = r["dln_g"][0], r["dln_b"][0], r["db_in"][0]
            small["pre_mix_g"][i] = r["dg_pre"]
            first_layer = ()
            if i == 0:
                small_gather = start_small_gather()
                first_layer = (small_gather[2],)
            grads["a_w_in"][j] = grad_mm(r["da"], s["hn"], "gmlp_in_grad", after=first_layer)
            sent = (send_grads(layer_keys(i)[:1] if alone else layer_keys(i)[:2], f"mixer_{i}"),)
        else:
            grads["b_w_o"][j] = grad_mm(s["o"], dmix, "attn_out_grad")
            do = rowmm(dmix, weight(i, "b_w_o"), True, None, BF16, "attn_do")
            dq, dk, dv, dsink = attn_bwd(s["qk"], s["qkv"], do, b_sinks[j], q_width, kv_width, "attn_core_bwd")
            dqkv, dbias = rope_bwd(dq, dk, dv, tabs, "attn_rope_bwd")
            small["b_sinks"][j], small["b_b_qkv"][j] = dsink[0, :n_heads], dbias[0]
            grads["b_w_qkv"][j] = grad_mm(dqkv, s["hn"], "attn_qkv_grad")
            sent = send_grads(layer_keys(i)[:2], f"mixer_{i}")
            r = mm_norm_bwd(dqkv, weight(i, "b_w_qkv"), dh, s["h"], pre_mix_g[i][None], post, "attn_dx", sent)
            sent = ()
        dh, small["pre_mix_g"][i] = r["dh"], r["dg_pre"]
        if i > 0:
            df, small["post_ffn_g"][i - 1] = r["dy"], r["dg_post"]
    grad_x = dh[None]

    small_sems, small_lands, small_sent = small_gather
    pack_main = lambda src: jnp.concatenate(
        flat(src[k] for k in replicated) + [jnp.ones((main_rows * LANES - main,), F32)]).reshape(main_rows, LANES)
    packed_w, packed_m, packed_v = pack_main(weights), pack_main(mom_m), pack_main(mom_v)

    stacked = {k: [lax.empty(w_view[k].shape, F32) for _ in range(4)] for k in big}
    after = (small_sent, dh) + sent
    for keys, sems, bufs, tag in in_flight:
        partials, landed = exchange_wait(bufs, sems, after, "exchange_wait_" + tag)
        for (k, l), part, land in zip(keys, partials, landed):
            stacked[k] = sum_adamw(part, land, w_view[k], m_view[k], v_view[k], stacked[k], l,
                                   k in in_kernel, dev, "adamw_" + k)
        after = tuple(stacked[k][0] for k, _ in keys)
    out_g, out_d, out_m, out_v = {}, {}, {}, {}
    for k in big:
        out_g[k], out_d[k], out_m[k], out_v[k] = [view(k, t) for t in stacked[k]]

    gathered_small, gathered_spatial = gather_wait(small_lands, small_sems[0], after + (packed_w, packed_m, packed_v),
                                                   "gather_wait_small")
    rows_of = lambda t: t.reshape(-1, LANES)
    spatial = sum_adamw_small(gathered_spatial, rows_of(a_w_s), rows_of(m_a_w_s), rows_of(v_a_w_s), "adamw_a_w_s")
    out_g["a_w_s"], out_d["a_w_s"], out_m["a_w_s"], out_v["a_w_s"] = [t.reshape(a_w_s.shape) for t in spatial]
    results = sum_adamw_small(gathered_small, packed_w, packed_m, packed_v, "adamw_small")
    off = 0
    for k in replicated:
        size, shape = weights[k].size, weights[k].shape
        out_g[k], out_d[k], out_m[k], out_v[k] = [t.reshape(-1)[off:off + size].reshape(shape) for t in results]
        off += size
    n_local = b_b_qkv.shape[1]
    bias_sum = sum_parts(gathered_small[:, main_rows:], "sum_bias_grads").reshape(-1)[:bias_size]
    out_g["b_b_qkv"] = lax.dynamic_slice_in_dim(
        bias_sum.reshape(b_b_qkv.shape[0], -1), device * n_local, n_local, axis=1)
    pad = lambda t: jnp.concatenate([t.reshape(-1), jnp.ones((8 * LANES - t.size,), F32)]).reshape(8, LANES)
    bias_update = adamw(pad(b_b_qkv), pad(out_g["b_b_qkv"]), pad(m_b_b_qkv), pad(v_b_b_qkv), "adamw_bias")
    out_d["b_b_qkv"], out_m["b_b_qkv"], out_v["b_b_qkv"] = [
        t.reshape(-1)[:b_b_qkv.size].reshape(b_b_qkv.shape) for t in bias_update]

    return (loss, grad_x, *[out_g[k] for k in names], *[out_d[k] for k in names],
            *[out_m[k] for k in names], *[out_v[k] for k in names])
```

```python
import math

import jax
import jax.numpy as jnp
from jax import lax
from jax.experimental import pallas as pl
from jax.experimental.pallas import tpu as pltpu

F32, BF16 = jnp.float32, jnp.bfloat16
MESH = pl.DeviceIdType.MESH
N_DEV = 8

CHUNK = 128
HEAD_DIM = 64
N_KV_HEADS = 4
ROPE_DIM = HEAD_DIM // 4
ROPE_THETA = 500000.0
RMS_EPS = 1e-6
LN_EPS = 1e-5
NEG_INF = -1e30

ADAM_LR = 0.001
ADAM_B1 = 0.9
ADAM_B2 = 0.999
ADAM_EPS = 1e-08
ADAM_WD = 0.01
ADAM_STEP = 10

V7X_VMEM_BYTES = 64 * 2 ** 20
VMEM_LIMIT = V7X_VMEM_BYTES - 8 * 2 ** 20
LANES = 128


def _params(*sem):
    return pltpu.CompilerParams(dimension_semantics=sem or None, vmem_limit_bytes=VMEM_LIMIT)


def _tile(n, pref):
    if n <= pref:
        return n
    t = pref - pref % LANES
    while t >= LANES:
        if n % t == 0:
            return t
        t -= LANES
    return n


def _full(shape):
    return pl.BlockSpec(shape, lambda *_: (0,) * len(shape))


def _rows(tm, width):
    return pl.BlockSpec((tm, width), lambda i: (i, 0))


def _rms(x, g):
    r = lax.rsqrt(jnp.mean(x * x, axis=-1, keepdims=True) + RMS_EPS)
    return x * r * g


def _rms_bwd(x, g, dy):
    r = lax.rsqrt(jnp.mean(x * x, axis=-1, keepdims=True) + RMS_EPS)
    xhat = x * r
    dg = jnp.sum(dy * xhat, axis=0, keepdims=True)
    dxhat = dy * g
    dx = r * (dxhat - xhat * jnp.mean(dxhat * xhat, axis=-1, keepdims=True))
    return dx, dg


_INV_SQRT2 = 1.0 / math.sqrt(2.0)
_INV_SQRT2PI = 1.0 / math.sqrt(2.0 * math.pi)


def _gelu(x):
    return 0.5 * x * (1.0 + lax.erf(x * _INV_SQRT2))


def _gelu_and_grad(x):
    cdf = 0.5 * (1.0 + lax.erf(x * _INV_SQRT2))
    return x * cdf, cdf + x * jnp.exp(-0.5 * x * x) * _INV_SQRT2PI


def _sigmoid(x):
    return 0.5 * jnp.tanh(0.5 * x) + 0.5


def _position():
    return lax.axis_index("x"), lax.axis_index("y"), lax.axis_index("c")


def _index(p):
    return 4 * p[0] + 2 * p[1] + p[2]


_HBM = pl.BlockSpec(memory_space=pltpu.HBM)
_SEM = pl.BlockSpec(memory_space=pltpu.SEMAPHORE)
_ORDERED_BY_DATA = pltpu.SideEffectType.DATAFLOW_SIDE_EFFECTING


def _in_hbm(v):
    return pltpu.with_memory_space_constraint(v, pltpu.HBM)


def _peer(k):
    x, y, c = _position()
    return (x ^ ((k >> 2) & 1), y ^ ((k >> 1) & 1), c ^ (k & 1))


def _split_copies(srcs, dsts, send, recv):
    return [pltpu.make_async_remote_copy(
        src_ref=src(k), dst_ref=dst(k), send_sem=send.at[7 * a + k - 1], recv_sem=recv.at[7 * a + k - 1],
        device_id=_peer(k), device_id_type=MESH)
        for a, (src, dst) in enumerate(zip(srcs, dsts)) for k in range(1, N_DEV)]


def _start_call(groups, sent, copies_of, name, per_array=N_DEV - 1):
    flat = [v for grp in groups for v in grp]
    n, ng = len(flat), len(groups)

    def body(*refs):
        bufs, sems = refs[:n], refs[n:n + 2 * ng]
        q = 0
        for g, grp in enumerate(groups):
            for cp in copies_of(bufs[q:q + len(grp)], sems[2 * g], sems[2 * g + 1]):
                cp.start()
            q += len(grp)
        refs[-1][...] = jnp.zeros_like(refs[-1])

    sem_shapes = []
    for count in sent:
        sem_shapes += [pltpu.SemaphoreType.DMA((per_array * count,)) for _ in range(2)]
    res = pl.pallas_call(
        body, name=name,
        out_shape=sem_shapes + [pltpu.HBM(v.shape, v.dtype) for v in flat] + [jax.ShapeDtypeStruct((8, LANES), F32)],
        in_specs=[_HBM] * n,
        out_specs=[_SEM] * (2 * ng) + [_HBM] * n + [pl.BlockSpec(memory_space=pltpu.VMEM)],
        input_output_aliases={i: 2 * ng + i for i in range(n)},
        compiler_params=pltpu.CompilerParams(has_side_effects=_ORDERED_BY_DATA),
    )(*[_in_hbm(v) for v in flat])
    sems = [(res[2 * g], res[2 * g + 1]) for g in range(ng)]
    return sems, list(res[2 * ng:-1]), res[-1]


def _wait_call(bufs, sems, copies_of, after, name):
    n = len(bufs)

    def body(*refs):
        for cp in copies_of(refs[:n], refs[n], refs[n + 1]):
            cp.wait_send()
            cp.wait_recv()

    return list(pl.pallas_call(
        body, name=name,
        out_shape=[pltpu.HBM(v.shape, v.dtype) for v in bufs],
        in_specs=[_HBM] * n + [_SEM, _SEM] + [pl.BlockSpec(memory_space=pl.ANY)] * len(after),
        out_specs=[_HBM] * n,
        input_output_aliases={i: i for i in range(n)},
        compiler_params=pltpu.CompilerParams(has_side_effects=_ORDERED_BY_DATA),
    )(*bufs, sems[0], sems[1], *after))


def _gather_copies(lands, send, recv):
    me = _index(_position())
    mine = [lambda k, ref=ref: ref.at[pl.ds(me, 1)] for ref in lands]
    return _split_copies(mine, mine, send, recv)


def _gather_arrivals(lands, send, recv):
    me = _index(_position())
    mine = [lambda k, ref=ref: ref.at[pl.ds(me, 1)] for ref in lands]
    theirs = [lambda k, ref=ref: ref.at[pl.ds(_index(_peer(k)), 1)] for ref in lands]
    return _split_copies(mine, theirs, send, recv)


def _exchange_copies(bufs, send, recv):
    half = len(bufs) // 2
    srcs = [lambda k, ref=ref: ref.at[pl.ds(_index(_peer(k)), 1)] for ref in bufs[:half]]
    dsts = [lambda k, ref=ref: ref.at[pl.ds(k - 1, 1)] for ref in bufs[half:]]
    return _split_copies(srcs, dsts, send, recv)


def gather_start(groups, name):
    return _start_call(groups, [len(grp) for grp in groups], _gather_copies, name)


def gather_wait(lands, sems, after, name):
    return _wait_call(lands, sems, _gather_arrivals, after, name)


_DIRECT = (1, 2, 4, 6)
_PASSED = (3, 5, 7)


def _remote(src, dst, send, recv, q, k):
    return pltpu.make_async_remote_copy(src_ref=src, dst_ref=dst, send_sem=send.at[q], recv_sem=recv.at[q],
                                        device_id=_peer(k), device_id_type=MESH)


def _slot_of(land, k):
    return land.at[pl.ds(_index(_peer(k)), 1)]


def relay_start(groups, name):
    def copies(lands, send, recv):
        return [_remote(_slot_of(land, 0), _slot_of(land, 0), send, recv, len(_DIRECT) * a + j, k)
                for a, land in enumerate(lands) for j, k in enumerate(_DIRECT)]
    return _start_call(groups, [len(grp) for grp in groups], copies, name, per_array=len(_DIRECT))


def relay_pass(lands, sems, after, name):
    n = len(lands)

    def body(*refs):
        bufs, recv_first = refs[:n], refs[n]
        send, recv = refs[n + 1 + len(after):n + 3 + len(after)]
        for a, land in enumerate(bufs):
            for j, k in enumerate(_PASSED):
                came = _slot_of(land, k ^ 1)
                _remote(came, came, recv_first, recv_first, len(_DIRECT) * a + _DIRECT.index(k ^ 1), k ^ 1).wait_recv()
                _remote(came, came, send, recv, len(_PASSED) * a + j, 1).start()

    res = pl.pallas_call(
        body, name=name,
        out_shape=[pltpu.SemaphoreType.DMA((len(_PASSED) * n,))] * 2 + [pltpu.HBM(v.shape, v.dtype) for v in lands],
        in_specs=[_HBM] * n + [_SEM] + [pl.BlockSpec(memory_space=pl.ANY)] * len(after),
        out_specs=[_SEM, _SEM] + [_HBM] * n,
        input_output_aliases={i: 2 + i for i in range(n)},
        compiler_params=pltpu.CompilerParams(has_side_effects=_ORDERED_BY_DATA),
    )(*lands, sems[1], *after)
    return (res[0], res[1]), list(res[2:])


def relay_wait(lands, first, second, after, name):
    n = len(lands)

    def body(*refs):
        bufs = refs[:n]
        send_first, recv_first, send, recv = refs[n:n + 4]
        for a, land in enumerate(bufs):
            mine = _slot_of(land, 0)
            _remote(mine, _slot_of(land, 1), send_first, recv_first, len(_DIRECT) * a, 1).wait_recv()
            for j, k in enumerate(_DIRECT):
                _remote(mine, mine, send_first, recv_first, len(_DIRECT) * a + j, k).wait_send()
            for j, k in enumerate(_PASSED):
                cp = _remote(_slot_of(land, k ^ 1), _slot_of(land, k), send, recv, len(_PASSED) * a + j, 1)
                cp.wait_send()
                cp.wait_recv()

    return list(pl.pallas_call(
        body, name=name,
        out_shape=[pltpu.HBM(v.shape, v.dtype) for v in lands],
        in_specs=[_HBM] * n + [_SEM] * 4 + [pl.BlockSpec(memory_space=pl.ANY)] * len(after),
        out_specs=[_HBM] * n,
        input_output_aliases={i: i for i in range(n)},
        compiler_params=pltpu.CompilerParams(has_side_effects=_ORDERED_BY_DATA),
    )(*lands, first[0], first[1], second[0], second[1], *after))


def exchange_start(partials, name):
    lands = [lax.empty((N_DEV - 1,) + p.shape[1:], p.dtype) for p in partials]
    sems, bufs, token = _start_call([list(partials) + lands], [len(partials)], _exchange_copies, name)
    return sems[0], bufs, token


def exchange_wait(bufs, sems, after, name):
    bufs = _wait_call(bufs, sems, _exchange_copies, after, name)
    return bufs[:len(bufs) // 2], bufs[len(bufs) // 2:]


def prep_weights(items, dev, name, dtype=BF16, after=()):
    n = len(items)
    blocks = [(1, w.shape[2], w.shape[1]) if t else (1,) + w.shape[1:] for w, _, t in items]

    def body(d_ref, *refs):
        for (_, _, transpose), w_ref, o_ref in zip(items, refs[:n], refs[-n:]):
            v = w_ref[0]
            o_ref[0] = (v.T if transpose else v).astype(dtype)

    return pl.pallas_call(
        body, name=name,
        grid_spec=pltpu.PrefetchScalarGridSpec(
            num_scalar_prefetch=1, grid=(1,),
            in_specs=[pl.BlockSpec((1,) + w.shape[1:], lambda i, d, layer=layer: (layer, 0, 0)) for w, layer, _ in items]
            + [pl.BlockSpec(memory_space=pl.ANY)] * len(after),
            out_specs=[pl.BlockSpec(blk, lambda i, d: (d[0], 0, 0)) for blk in blocks]),
        out_shape=[jax.ShapeDtypeStruct((N_DEV,) + blk[1:], dtype) for blk in blocks],
        compiler_params=_params("arbitrary"),
    )(dev, *[w for w, _, _ in items], *after)


def _adamw_math(w, g, m, v):
    mn = ADAM_B1 * m + (1.0 - ADAM_B1) * g
    vn = ADAM_B2 * v + (1.0 - ADAM_B2) * (g * g)
    m_hat = mn * (1.0 / (1.0 - ADAM_B1 ** ADAM_STEP))
    v_hat = vn * (1.0 / (1.0 - ADAM_B2 ** ADAM_STEP))
    return -ADAM_LR * (m_hat / (jnp.sqrt(v_hat) + ADAM_EPS) + ADAM_WD * w), mn, vn


def sum_adamw(partial, landed, w, m, v, prev, layer, transpose, dev, name):
    _, r, c = partial.shape
    if transpose:
        tc = _tile(c, 256)
        grid = (c // tc,)
        part_spec = pl.BlockSpec((1, r, tc), lambda i, d: (d[0], 0, i))
        land_spec = pl.BlockSpec((N_DEV - 1, r, tc), lambda i, d: (0, 0, i))
        w_spec = pl.BlockSpec((1, tc, r), lambda i, d: (layer, i, 0))
    else:
        tr = r // 2 if r % 32 == 0 else r
        grid = (r // tr,)
        part_spec = pl.BlockSpec((1, tr, c), lambda i, d: (d[0], i, 0))
        land_spec = pl.BlockSpec((N_DEV - 1, tr, c), lambda i, d: (0, i, 0))
        w_spec = pl.BlockSpec((1, tr, c), lambda i, d: (layer, i, 0))

    def body(d_ref, p_ref, l_ref, w_ref, m_ref, v_ref, *rest):
        g_ref, dl_ref, mo_ref, vo_ref = rest[-4:]
        acc = p_ref[0].astype(F32)
        for k in range(N_DEV - 1):
            acc = acc + l_ref[k].astype(F32)
        g = acc.T if transpose else acc
        delta, mn, vn = _adamw_math(w_ref[0], g, m_ref[0], v_ref[0])
        g_ref[0], dl_ref[0], mo_ref[0], vo_ref[0] = g, delta, mn, vn

    return pl.pallas_call(
        body, name=name,
        grid_spec=pltpu.PrefetchScalarGridSpec(
            num_scalar_prefetch=1, grid=grid,
            in_specs=[part_spec, land_spec, w_spec, w_spec, w_spec] + [pl.BlockSpec(memory_space=pl.ANY)] * 4,
            out_specs=[w_spec] * 4),
        out_shape=[jax.ShapeDtypeStruct(w.shape, F32)] * 4,
        input_output_aliases={6 + q: q for q in range(4)},
        compiler_params=_params("parallel"),
    )(dev, partial, landed, w, m, v, *prev)


def sum_parts(parts, name):
    P, r, c = parts.shape

    def body(p_ref, o_ref):
        acc = p_ref[0].astype(F32)
        for s in range(1, P):
            acc = acc + p_ref[s].astype(F32)
        o_ref[...] = acc

    return pl.pallas_call(
        body, name=name, in_specs=[_full((P, r, c))], out_specs=_full((r, c)), grid=(1,),
        out_shape=jax.ShapeDtypeStruct((r, c), F32), compiler_params=_params("arbitrary"))(parts)


def adamw(w, g, m, v, name):
    R, C = w.shape
    tr = _tile(R, 512)

    def body(w_ref, g_ref, m_ref, v_ref, d_ref, mo_ref, vo_ref):
        d_ref[...], mo_ref[...], vo_ref[...] = _adamw_math(w_ref[...], g_ref[...], m_ref[...], v_ref[...])

    spec = pl.BlockSpec((tr, C), lambda i: (i, 0))
    return pl.pallas_call(
        body, name=name, grid=(R // tr,),
        in_specs=[spec] * 4, out_specs=[spec] * 3,
        out_shape=[jax.ShapeDtypeStruct((R, C), F32)] * 3,
        compiler_params=_params("parallel"),
    )(w, g, m, v)


def sum_adamw_small(gathered, w, m, v, name):
    rows = w.shape[0]
    tr = _tile(rows, 512)

    def body(p_ref, w_ref, m_ref, v_ref, g_ref, d_ref, mo_ref, vo_ref):
        g = p_ref[0]
        for s in range(1, N_DEV):
            g = g + p_ref[s]
        g_ref[...] = g
        d_ref[...], mo_ref[...], vo_ref[...] = _adamw_math(w_ref[...], g, m_ref[...], v_ref[...])

    spec = pl.BlockSpec((tr, LANES), lambda i: (i, 0))
    return pl.pallas_call(
        body, name=name, grid=(rows // tr,),
        in_specs=[pl.BlockSpec((N_DEV, tr, LANES), lambda i: (0, i, 0)), spec, spec, spec], out_specs=[spec] * 4,
        out_shape=[jax.ShapeDtypeStruct((rows, LANES), F32)] * 4, compiler_params=_params("parallel"))(gathered, w, m, v)


def first_norm(h, g, name):
    T, D = h.shape
    tm = _tile(T, 512)

    def body(h_ref, g_ref, hn_ref):
        hn_ref[...] = _rms(h_ref[...], g_ref[...]).astype(BF16)

    row = _rows(tm, D)
    return pl.pallas_call(
        body, name=name, grid=(T // tm,), in_specs=[row, _full((1, D))], out_specs=row,
        out_shape=jax.ShapeDtypeStruct((T, D), BF16), compiler_params=_params("parallel"))(h, g)


_NT = (((1,), (1,)), ((), ()))
_TN = (((0,), (0,)), ((), ()))


def rowmm(a, w, transposed_w, bias, out_dtype, name, after=None):
    M, K = a.shape
    N = w.shape[0] if transposed_w else w.shape[1]
    tm = _tile(M, 256)
    tn = _tile(N, 512)

    def body(*refs):
        a_ref, w_ref = refs[:2]
        o_ref = refs[-1]
        av = a_ref[...]
        for j in range(N // tn):
            cols = pl.ds(j * tn, tn)
            if transposed_w:
                acc = lax.dot_general(av, w_ref[cols, :], _NT, preferred_element_type=F32)
            else:
                acc = jnp.dot(av, w_ref[:, cols], preferred_element_type=F32)
            if bias is not None:
                acc = acc + refs[2][:, cols]
            o_ref[:, cols] = acc.astype(out_dtype)

    ins, in_specs = [a, w], [_rows(tm, K), _full(w.shape)]
    if bias is not None:
        ins.append(bias)
        in_specs.append(_full((1, N)))
    if after is not None:
        ins.append(after)
        in_specs.append(pl.BlockSpec(memory_space=pl.ANY))
    return pl.pallas_call(
        body, name=name, grid=(M // tm,), in_specs=in_specs, out_specs=_rows(tm, N),
        out_shape=jax.ShapeDtypeStruct((M, N), out_dtype), compiler_params=_params("parallel"))(*ins)


def _rows_times_weight(a_ref, w_ref, acc_ref):
    N = w_ref.shape[1]
    tn = _tile(N, 512)
    av = a_ref[...]
    for j in range(N // tn):
        cols = pl.ds(j * tn, tn)
        acc_ref[:, cols] = jnp.dot(av, w_ref[:, cols], preferred_element_type=F32)


def mm_add_norm(a, w, h, g_post, g_pre, name):
    T, K = a.shape
    D = w.shape[1]
    tm = _tile(T, 256)

    def body(a_ref, w_ref, h_ref, gp_ref, g_ref, y_ref, ho_ref, hn_ref, acc_ref):
        _rows_times_weight(a_ref, w_ref, acc_ref)
        y = acc_ref[...]
        y_ref[...] = y.astype(BF16)
        hv = h_ref[...] + _rms(y, gp_ref[...])
        ho_ref[...] = hv
        hn_ref[...] = _rms(hv, g_ref[...]).astype(BF16)

    row, vec = _rows(tm, D), _full((1, D))
    return pl.pallas_call(
        body, name=name, grid=(T // tm,), in_specs=[_rows(tm, K), _full(w.shape), row, vec, vec],
        out_specs=[row, row, row],
        out_shape=[jax.ShapeDtypeStruct((T, D), BF16), jax.ShapeDtypeStruct((T, D), F32),
                   jax.ShapeDtypeStruct((T, D), BF16)],
        scratch_shapes=[pltpu.VMEM((tm, D), F32)], compiler_params=_params("parallel"))(a, w, h, g_post, g_pre)


def mm_add_norm_loss(a, w, h, g_post, target, name):
    T, K = a.shape
    D = w.shape[1]
    tm = _tile(T, 256)

    def body(a_ref, w_ref, h_ref, gp_ref, t_ref, dh_ref, dy_ref, loss_ref, dgp_ref, acc_ref):
        @pl.when(pl.program_id(0) == 0)
        def _():
            loss_ref[...] = jnp.zeros_like(loss_ref)
            dgp_ref[...] = jnp.zeros_like(dgp_ref)
        _rows_times_weight(a_ref, w_ref, acc_ref)
        y = acc_ref[...]
        err = h_ref[...] + _rms(y, gp_ref[...]) - t_ref[...]
        dh = err * (1.0 / D)
        dh_ref[...] = dh
        loss_ref[...] += jnp.sum(err * err)
        dy, dgp = _rms_bwd(y, gp_ref[...], dh)
        dy_ref[...] = dy.astype(BF16)
        dgp_ref[...] += dgp

    row, vec = _rows(tm, D), _full((1, D))
    return pl.pallas_call(
        body, name=name, grid=(T // tm,), in_specs=[_rows(tm, K), _full(w.shape), row, vec, row],
        out_specs=[row, row, _full((8, LANES)), vec],
        out_shape=[jax.ShapeDtypeStruct((T, D), F32), jax.ShapeDtypeStruct((T, D), BF16),
                   jax.ShapeDtypeStruct((8, LANES), F32), jax.ShapeDtypeStruct((1, D), F32)],
        scratch_shapes=[pltpu.VMEM((tm, D), F32)], compiler_params=_params("arbitrary"))(a, w, h, g_post, target)


def mm_norm_bwd(a, w, dh_in, x, g_pre, post, name, after):
    T, K = a.shape
    D = w.shape[1]
    tm = _tile(T, 256)
    row, vec = _rows(tm, D), _full((1, D))
    ins, in_specs = [a, w, dh_in, x, g_pre], [_rows(tm, K), _full(w.shape), row, row, vec]
    outs = [jax.ShapeDtypeStruct((T, D), F32), jax.ShapeDtypeStruct((1, D), F32)]
    out_specs, names = [row, vec], ["dh", "dg_pre"]
    if post is not None:
        ins += list(post)
        in_specs += [row, vec]
        outs += [jax.ShapeDtypeStruct((T, D), BF16), jax.ShapeDtypeStruct((1, D), F32)]
        out_specs += [row, vec]
        names += ["dy", "dg_post"]
    n_in = len(ins) + 1

    def body(*refs):
        a_ref, w_ref, dh_ref_in, x_ref, g_ref = refs[:5]
        out_refs, acc_ref = refs[n_in:-1], refs[-1]
        first = pl.program_id(0) == 0
        _rows_times_weight(a_ref, w_ref, acc_ref)
        dx, dg = _rms_bwd(x_ref[...], g_ref[...], acc_ref[...])
        dh = dh_ref_in[...] + dx
        out_refs[0][...] = dh

        @pl.when(first)
        def _():
            for r in out_refs[1::2]:
                r[...] = jnp.zeros_like(r)
        out_refs[1][...] += dg
        if post is not None:
            y_ref, gp_ref = refs[5:7]
            dy, dgp = _rms_bwd(y_ref[...].astype(F32), gp_ref[...], dh)
            out_refs[2][...] = dy.astype(BF16)
            out_refs[3][...] += dgp

    res = pl.pallas_call(
        body, name=name, grid=(T // tm,), in_specs=in_specs + [pl.BlockSpec(memory_space=pl.ANY)],
        out_specs=out_specs, out_shape=outs, scratch_shapes=[pltpu.VMEM((tm, D), F32)],
        compiler_params=_params("arbitrary"))(*ins, after)
    return dict(zip(names, res))


def grad_mm(a, b, name, after=()):
    T, N = a.shape
    K = b.shape[1]
    tn = _tile(N, 1408)
    tt = _tile(T, 1024)

    def body(a_ref, b_ref, *rest):
        o_ref, acc_ref = rest[-2:]
        t = pl.program_id(1)

        @pl.when(t == 0)
        def _():
            acc_ref[...] = jnp.zeros_like(acc_ref)
        acc_ref[...] += lax.dot_general(a_ref[...], b_ref[...], _TN, preferred_element_type=F32)

        @pl.when(t == pl.num_programs(1) - 1)
        def _():
            o_ref[...] = acc_ref[...].astype(BF16)

    return pl.pallas_call(
        body, name=name, grid=(N // tn, T // tt),
        in_specs=[pl.BlockSpec((tt, tn), lambda j, t: (t, j)), pl.BlockSpec((tt, K), lambda j, t: (t, 0))]
        + [pl.BlockSpec(memory_space=pl.ANY)] * len(after),
        out_specs=pl.BlockSpec((tn, K), lambda j, t: (j, 0)),
        out_shape=jax.ShapeDtypeStruct((N, K), BF16),
        scratch_shapes=[pltpu.VMEM((tn, K), F32)],
        compiler_params=_params("parallel", "arbitrary"))(a, b, *after)


def ffn_up(fn, w_gu_t, name):
    T, D = fn.shape
    F = w_gu_t.shape[0] // 2
    tm = _tile(T, 512)
    tn = _tile(F, 512)

    def body(a_ref, w_ref, act_ref, part_ref):
        av = a_ref[...]
        for j in range(F // tn):
            g = lax.dot_general(av, w_ref[pl.ds(j * tn, tn), :], _NT, preferred_element_type=F32)
            up = lax.dot_general(av, w_ref[pl.ds(F + j * tn, tn), :], _NT, preferred_element_type=F32)
            sg = _sigmoid(g)
            silu = g * sg
            act_ref[:, pl.ds(j * tn, tn)] = (silu * up).astype(BF16)
            part_ref[:, pl.ds(j * tn, tn)] = (up * (sg + silu * (1.0 - sg))).astype(BF16)
            part_ref[:, pl.ds(F + j * tn, tn)] = silu.astype(BF16)

    return pl.pallas_call(
        body, name=name, grid=(T // tm,), in_specs=[_rows(tm, D), _full(w_gu_t.shape)],
        out_specs=[_rows(tm, F), _rows(tm, 2 * F)],
        out_shape=[jax.ShapeDtypeStruct((T, F), BF16), jax.ShapeDtypeStruct((T, 2 * F), BF16)],
        compiler_params=_params("parallel"))(fn, w_gu_t)


def ffn_dact(df, w_down, partials, name, after=()):
    T, D = df.shape
    F = w_down.shape[0]
    tm = _tile(T, 512)
    tn = _tile(F, 512)

    def body(d_ref, w_ref, part_ref, *rest):
        o_ref = rest[-1]
        dv = d_ref[...]
        for j in range(F // tn):
            dact = lax.dot_general(dv, w_ref[pl.ds(j * tn, tn), :], _NT, preferred_element_type=F32)
            for cols in (pl.ds(j * tn, tn), pl.ds(F + j * tn, tn)):
                o_ref[:, cols] = (dact * part_ref[:, cols].astype(F32)).astype(BF16)

    return pl.pallas_call(
        body, name=name, grid=(T // tm,),
        in_specs=[_rows(tm, D), _full(w_down.shape), _rows(tm, 2 * F)] + [pl.BlockSpec(memory_space=pl.ANY)] * len(after),
        out_specs=_rows(tm, 2 * F), out_shape=jax.ShapeDtypeStruct((T, 2 * F), BF16),
        compiler_params=_params("parallel"))(df, w_down, partials, *after)


def _layernorm_stats(v):
    mu = jnp.mean(v, axis=-1, keepdims=True)
    cen = v - mu
    rstd = lax.rsqrt(jnp.mean(cen * cen, axis=-1, keepdims=True) + LN_EPS)
    return cen * rstd, rstd


GMLP_ROWS = 2 * CHUNK


def gmlp_gate_out(a, ln_g, ln_b, w_mask, b_s_t, w_out, h, g_post, g_pre, name):
    T, W2 = a.shape
    W = W2 // 2
    G = w_mask.shape[0]
    C = W // G
    D = w_out.shape[1]
    tm = min(T, GMLP_ROWS)

    def body(a_ref, lg_ref, lb_ref, w_ref, bs_ref, wo_ref, h_ref, gp_ref, g_ref,
             gated_ref, y_ref, ho_ref, hn_ref, acc_ref):
        for c in range(tm // CHUNK):
            rows = pl.ds(c * CHUNK, CHUNK)
            xhat, _ = _layernorm_stats(_gelu(a_ref[rows, W:].astype(F32)))
            vln = (xhat * lg_ref[...] + lb_ref[...]).astype(BF16)
            for g in range(G):
                cols = pl.ds(g * C, C)
                sv = jnp.dot(w_ref[g], vln[:, g * C:(g + 1) * C], preferred_element_type=F32) + bs_ref[:, g:g + 1]
                gated_ref[rows, cols] = (_gelu(a_ref[rows, cols].astype(F32)) * sv).astype(BF16)
        _rows_times_weight(gated_ref, wo_ref, acc_ref)
        y = acc_ref[...]
        y_ref[...] = y.astype(BF16)
        hv = h_ref[...] + _rms(y, gp_ref[...])
        ho_ref[...] = hv
        hn_ref[...] = _rms(hv, g_ref[...]).astype(BF16)

    row, vec = _rows(tm, D), _full((1, D))
    return pl.pallas_call(
        body, name=name, grid=(T // tm,),
        in_specs=[_rows(tm, W2), _full((1, W)), _full((1, W)), _full(w_mask.shape), _full(b_s_t.shape),
                  _full(w_out.shape), row, vec, vec],
        out_specs=[_rows(tm, W), row, row, row],
        out_shape=[jax.ShapeDtypeStruct((T, W), BF16), jax.ShapeDtypeStruct((T, D), BF16),
                   jax.ShapeDtypeStruct((T, D), F32), jax.ShapeDtypeStruct((T, D), BF16)],
        scratch_shapes=[pltpu.VMEM((tm, D), F32)],
        compiler_params=_params("parallel"))(a, ln_g, ln_b, w_mask, b_s_t, w_out, h, g_post, g_pre)


def gmlp_gate_bwd_dx(a, dmix, w_out, ln_g, ln_b, w_mask, w_mask_t, b_s_t, group_onehot, w_in_t,
                     dh_in, x, g_pre, post, name, after):
    T, W2 = a.shape
    W = W2 // 2
    G = w_mask.shape[0]
    C = W // G
    D = w_out.shape[1]
    tm = min(T, GMLP_ROWS)
    tn = _tile(W, 512)
    row, vec = _rows(tm, D), _full((1, D))
    ins = [a, dmix, w_out, ln_g, ln_b, w_mask, w_mask_t, b_s_t, group_onehot, w_in_t, dh_in, x, g_pre]
    in_specs = [_rows(tm, W2), row, _full(w_out.shape), _full((1, W)), _full((1, W)), _full(w_mask.shape),
                _full(w_mask_t.shape), _full(b_s_t.shape), _full(group_onehot.shape), _full(w_in_t.shape), row, row, vec]
    names = ["da", "dw_s", "db_s_t", "dln_g", "dln_b", "db_in", "dh", "dg_pre"]
    outs = [jax.ShapeDtypeStruct((T, W2), BF16), jax.ShapeDtypeStruct((G, CHUNK, CHUNK), F32),
            jax.ShapeDtypeStruct((CHUNK, LANES), F32), jax.ShapeDtypeStruct((1, W), F32),
            jax.ShapeDtypeStruct((1, W), F32), jax.ShapeDtypeStruct((1, W2), F32),
            jax.ShapeDtypeStruct((T, D), F32), jax.ShapeDtypeStruct((1, D), F32)]
    out_specs = [_rows(tm, W2), _full((G, CHUNK, CHUNK)), _full((CHUNK, LANES)), _full((1, W)), _full((1, W)),
                 _full((1, W2)), row, vec]
    if post is not None:
        ins += list(post)
        in_specs += [row, vec]
        names += ["dy", "dg_post"]
        outs += [jax.ShapeDtypeStruct((T, D), BF16), jax.ShapeDtypeStruct((1, D), F32)]
        out_specs += [row, vec]
    n_in = len(ins) + 1

    def body(*refs):
        (a_ref, dm_ref, wo_ref, lg_ref, lb_ref, w_ref, wt_ref, bs_ref, e_ref, wi_ref, dh_in_ref, x_ref, gpre_ref) = refs[:13]
        out_refs = refs[n_in:-3]
        da_ref, dws_ref, dbs_ref, dlg_ref, dlb_ref, dbin_ref, dh_ref, dgpre_ref = out_refs[:8]
        dgated_ref, dvln_ref, acc_ref = refs[-3:]

        @pl.when(pl.program_id(0) == 0)
        def _():
            for r in out_refs[1:6] + out_refs[7::2]:
                r[...] = jnp.zeros_like(r)
        dm = dm_ref[...]
        for j in range(W // tn):
            cols = pl.ds(j * tn, tn)
            dgated_ref[:, cols] = lax.dot_general(dm, wo_ref[cols, :], _NT, preferred_element_type=F32)
        causal = (lax.broadcasted_iota(jnp.int32, (CHUNK, CHUNK), 1)
                  <= lax.broadcasted_iota(jnp.int32, (CHUNK, CHUNK), 0))
        for c in range(tm // CHUNK):
            rows = pl.ds(c * CHUNK, CHUNK)
            gelu_v, gelu_grad_v = _gelu_and_grad(a_ref[rows, W:].astype(F32))
            xhat, rstd = _layernorm_stats(gelu_v)
            vln = (xhat * lg_ref[...] + lb_ref[...]).astype(BF16)
            for g in range(G):
                cols = pl.ds(g * C, C)
                vg = vln[:, g * C:(g + 1) * C]
                gelu_u, gelu_grad_u = _gelu_and_grad(a_ref[rows, cols].astype(F32))
                dgated = dgated_ref[rows, cols]
                dsv = (dgated * gelu_u).astype(BF16)
                dbs_ref[...] += jnp.dot(dsv, e_ref[cols, :], preferred_element_type=F32)
                sv = jnp.dot(w_ref[g], vg, preferred_element_type=F32) + bs_ref[:, g:g + 1]
                dau = dgated * sv * gelu_grad_u
                da_ref[rows, cols] = dau.astype(BF16)
                dbin_ref[:, cols] += jnp.sum(dau, axis=0, keepdims=True)
                dws_ref[g] += jnp.where(causal, lax.dot_general(dsv, vg, _NT, preferred_element_type=F32), 0.0)
                dvln_ref[:, cols] = jnp.dot(wt_ref[g], dsv, preferred_element_type=F32)
            dvln = dvln_ref[...]
            dlg_ref[...] += jnp.sum(dvln * xhat, axis=0, keepdims=True)
            dlb_ref[...] += jnp.sum(dvln, axis=0, keepdims=True)
            dxhat = dvln * lg_ref[...]
            dv = rstd * (dxhat - jnp.mean(dxhat, axis=-1, keepdims=True)
                         - xhat * jnp.mean(dxhat * xhat, axis=-1, keepdims=True))
            dav = dv * gelu_grad_v
            da_ref[rows, W:] = dav.astype(BF16)
            dbin_ref[:, W:] += jnp.sum(dav, axis=0, keepdims=True)
        _rows_times_weight(da_ref, wi_ref, acc_ref)
        dx, dg = _rms_bwd(x_ref[...], gpre_ref[...], acc_ref[...])
        dh = dh_in_ref[...] + dx
        dh_ref[...] = dh
        dgpre_ref[...] += dg
        if post is not None:
            y_ref, gp_ref = refs[13:15]
            dy, dgp = _rms_bwd(y_ref[...].astype(F32), gp_ref[...], dh)
            out_refs[8][...] = dy.astype(BF16)
            out_refs[9][...] += dgp

    res = pl.pallas_call(
        body, name=name, grid=(T // tm,), in_specs=in_specs + [pl.BlockSpec(memory_space=pl.ANY)],
        out_specs=out_specs, out_shape=outs,
        scratch_shapes=[pltpu.VMEM((tm, W), F32), pltpu.VMEM((CHUNK, W), F32), pltpu.VMEM((tm, D), F32)],
        compiler_params=_params("arbitrary"))(*ins, after)
    return dict(zip(names, res))


def rope_tables(pos, inv_freq_row, name):
    T = pos.shape[0]
    tm = _tile(T, 512)

    def body(p_ref, f_ref, c_ref, s1_ref, s2_ref):
        ang = p_ref[...].astype(F32) * f_ref[...]
        lane = lax.broadcasted_iota(jnp.int32, (tm, LANES), 1) % HEAD_DIM
        sin = jnp.sin(ang)
        c_ref[...] = jnp.cos(ang)
        s1_ref[...] = jnp.where(lane < _HALF, -sin, 0.0)
        s2_ref[...] = jnp.where((lane >= _HALF) & (lane < ROPE_DIM), sin, 0.0)

    tab = _rows(tm, LANES)
    return pl.pallas_call(
        body, name=name, grid=(T // tm,), in_specs=[_rows(tm, 1), _full((1, LANES))], out_specs=[tab] * 3,
        out_shape=[jax.ShapeDtypeStruct((T, LANES), F32)] * 3, compiler_params=_params("parallel"))(pos, inv_freq_row)


_HALF = ROPE_DIM // 2


def _slabs(x):
    return [x[:, b * LANES:(b + 1) * LANES] for b in range(x.shape[1] // LANES)]


def _rotate(x, c, s1, s2):
    return [xs * c + pltpu.roll(xs, LANES - _HALF, 1) * s1 + pltpu.roll(xs, _HALF, 1) * s2 for xs in _slabs(x)]


def _rotate_transposed(dy, c, s1, s2):
    return [ds * c + pltpu.roll(ds * s1, _HALF, 1) + pltpu.roll(ds * s2, LANES - _HALF, 1) for ds in _slabs(dy)]


def rope_fwd(qkv, tabs, q_width, kv_width, name):
    T, QKV = qkv.shape
    tm = _tile(T, 512)
    QK = q_width + kv_width
    scale = HEAD_DIM ** -0.5

    def body(x_ref, c_ref, s1_ref, s2_ref, o_ref):
        x = x_ref[:, :QK].astype(F32)
        slabs = _rotate(x, c_ref[...], s1_ref[...], s2_ref[...])
        for b, y in enumerate(slabs):
            if b * LANES < q_width:
                y = y * scale
            o_ref[:, b * LANES:(b + 1) * LANES] = y.astype(BF16)

    tab = _rows(tm, LANES)
    return pl.pallas_call(
        body, name=name, grid=(T // tm,), in_specs=[_rows(tm, QKV), tab, tab, tab], out_specs=_rows(tm, QK),
        out_shape=jax.ShapeDtypeStruct((T, QK), BF16), compiler_params=_params("parallel"))(qkv, *tabs)


def rope_bwd(dq, dk, dv, tabs, name):
    T, Q = dq.shape
    KV = dk.shape[1]
    tm = _tile(T, 512)
    scale = HEAD_DIM ** -0.5

    def body(dq_ref, dk_ref, dv_ref, c_ref, s1_ref, s2_ref, o_ref, b_ref):
        @pl.when(pl.program_id(0) == 0)
        def _():
            b_ref[...] = jnp.zeros_like(b_ref)
        tabs_v = (c_ref[...], s1_ref[...], s2_ref[...])
        pieces = [s * scale for s in _rotate_transposed(dq_ref[...], *tabs_v)]
        pieces += _rotate_transposed(dk_ref[...], *tabs_v)
        pieces += _slabs(dv_ref[...])
        for b, y in enumerate(pieces):
            cols = pl.ds(b * LANES, LANES)
            o_ref[:, cols] = y.astype(BF16)
            b_ref[:, cols] += jnp.sum(y, axis=0, keepdims=True)

    tab = _rows(tm, LANES)
    return pl.pallas_call(
        body, name=name, grid=(T // tm,), in_specs=[_rows(tm, Q), _rows(tm, KV), _rows(tm, KV), tab, tab, tab],
        out_specs=[_rows(tm, Q + 2 * KV), _full((1, Q + 2 * KV))],
        out_shape=[jax.ShapeDtypeStruct((T, Q + 2 * KV), BF16), jax.ShapeDtypeStruct((1, Q + 2 * KV), F32)],
        compiler_params=_params("arbitrary"))(dq, dk, dv, *tabs)


def _band_mask(n, heads=1):
    qi = lax.broadcasted_iota(jnp.int32, (heads * CHUNK, 2 * CHUNK), 0) % CHUNK
    sj = lax.broadcasted_iota(jnp.int32, (heads * CHUNK, 2 * CHUNK), 1)
    return (sj > qi) & (sj <= qi + CHUNK) & ((n > 0) | (sj >= CHUNK))


def _kv_head(kc_ref, kp_ref, vc_ref, vp_ref, kh):
    lanes = slice(kh * HEAD_DIM, (kh + 1) * HEAD_DIM)
    return (jnp.concatenate([kp_ref[:, lanes], kc_ref[:, lanes]], axis=0),
            jnp.concatenate([vp_ref[:, lanes], vc_ref[:, lanes]], axis=0))


def _head_probs(q, kk, valid, sink):
    s = jnp.where(valid, lax.dot_general(q, kk, _NT, preferred_element_type=F32), NEG_INF)
    m = jnp.maximum(jnp.max(s, axis=-1, keepdims=True), sink)
    p = jnp.exp(s - m)
    e_sink = jnp.exp(sink - m)
    return p, 1.0 / (jnp.sum(p, axis=-1, keepdims=True) + e_sink), e_sink


def _attn_specs(q_width, kv_width, order):
    qb = q_width // kv_width
    prev = lambda i: jnp.maximum(order(i) - 1, 0)
    return [pl.BlockSpec((CHUNK, q_width), lambda i: (order(i), 0)),
            pl.BlockSpec((CHUNK, kv_width), lambda i: (order(i), qb)),
            pl.BlockSpec((CHUNK, kv_width), lambda i: (prev(i), qb)),
            pl.BlockSpec((CHUNK, kv_width), lambda i: (order(i), qb + 1)),
            pl.BlockSpec((CHUNK, kv_width), lambda i: (prev(i), qb + 1))]


def attn_fwd(qk, qkv, sinks, q_width, kv_width, name):
    T = qk.shape[0]
    group = q_width // kv_width

    def body(q_ref, kc_ref, kp_ref, vc_ref, vp_ref, sink_ref, o_ref):
        valid = _band_mask(pl.program_id(0))
        for kh in range(kv_width // HEAD_DIM):
            kk, vv = _kv_head(kc_ref, kp_ref, vc_ref, vp_ref, kh)
            for g in range(group):
                h = kh * group + g
                lanes = slice(h * HEAD_DIM, (h + 1) * HEAD_DIM)
                p, inv, _ = _head_probs(q_ref[:, lanes], kk, valid, sink_ref[h])
                o_ref[:, lanes] = (jnp.dot(p.astype(BF16), vv, preferred_element_type=F32) * inv).astype(BF16)

    specs = _attn_specs(q_width, kv_width, lambda i: i)
    return pl.pallas_call(
        body, name=name, grid=(T // CHUNK,),
        in_specs=specs + [pl.BlockSpec(memory_space=pltpu.SMEM)],
        out_specs=_rows(CHUNK, q_width), out_shape=jax.ShapeDtypeStruct((T, q_width), BF16),
        compiler_params=_params("parallel"))(qk, qk, qk, qkv, qkv, sinks)


def attn_bwd(qk, qkv, do, sinks, q_width, kv_width, name):
    T = qk.shape[0]
    NB = T // CHUNK
    group = q_width // kv_width

    def body(q_ref, kc_ref, kp_ref, vc_ref, vp_ref, do_ref, sink_ref, dq_ref, dk_ref, dv_ref, ds_ref, ck_ref, cv_ref):
        i = pl.program_id(0)
        n = NB - 1 - i

        @pl.when(i == 0)
        def _():
            ck_ref[...] = jnp.zeros_like(ck_ref)
            cv_ref[...] = jnp.zeros_like(cv_ref)
            ds_ref[...] = jnp.zeros_like(ds_ref)
        lane = lax.broadcasted_iota(jnp.int32, (1, LANES), 1)
        dsink_row = jnp.zeros((1, LANES), F32)
        valid = _band_mask(n, group)
        head = lax.broadcasted_iota(jnp.int32, (group * CHUNK, 1), 0) // CHUNK
        for kh in range(kv_width // HEAD_DIM):
            kk, vv = _kv_head(kc_ref, kp_ref, vc_ref, vp_ref, kh)
            heads = [slice((kh * group + g) * HEAD_DIM, (kh * group + g + 1) * HEAD_DIM) for g in range(group)]
            q = jnp.concatenate([q_ref[:, hs] for hs in heads], axis=0)
            do = jnp.concatenate([do_ref[:, hs] for hs in heads], axis=0)
            sink = jnp.zeros((group * CHUNK, 1), F32)
            for g in range(group):
                sink = jnp.where(head == g, sink_ref[kh * group + g], sink)
            p, inv, e_sink = _head_probs(q, kk, valid, sink)
            p = p * inv
            dp = lax.dot_general(do, vv, _NT, preferred_element_type=F32)
            delta = jnp.sum(p * dp, axis=-1, keepdims=True)
            ds = (p * (dp - delta)).astype(BF16)
            dsink = -e_sink * inv * delta
            dq = jnp.dot(ds, kk, preferred_element_type=F32)
            for g, hs in enumerate(heads):
                rows = slice(g * CHUNK, (g + 1) * CHUNK)
                dsink_row = dsink_row + jnp.where(lane == kh * group + g, jnp.sum(dsink[rows]), 0.0)
                dq_ref[:, hs] = dq[rows]
            dkk = lax.dot_general(ds, q, _TN, preferred_element_type=F32)
            dvv = lax.dot_general(p.astype(BF16), do, _TN, preferred_element_type=F32)
            lanes = slice(kh * HEAD_DIM, (kh + 1) * HEAD_DIM)
            dk_ref[:, lanes] = dkk[CHUNK:] + ck_ref[:, lanes]
            dv_ref[:, lanes] = dvv[CHUNK:] + cv_ref[:, lanes]
            ck_ref[:, lanes] = dkk[:CHUNK]
            cv_ref[:, lanes] = dvv[:CHUNK]
        ds_ref[0:1, :] += dsink_row

    order = lambda i: NB - 1 - i
    specs = _attn_specs(q_width, kv_width, order)
    kv_out = pl.BlockSpec((CHUNK, kv_width), lambda i: (order(i), 0))
    q_rows = pl.BlockSpec((CHUNK, q_width), lambda i: (order(i), 0))
    return pl.pallas_call(
        body, name=name, grid=(NB,),
        in_specs=specs + [q_rows, pl.BlockSpec(memory_space=pltpu.SMEM)],
        out_specs=[q_rows, kv_out, kv_out, _full((8, LANES))],
        out_shape=[jax.ShapeDtypeStruct((T, q_width), F32), jax.ShapeDtypeStruct((T, kv_width), F32),
                   jax.ShapeDtypeStruct((T, kv_width), F32), jax.ShapeDtypeStruct((8, LANES), F32)],
        scratch_shapes=[pltpu.VMEM((CHUNK, kv_width), F32), pltpu.VMEM((CHUNK, kv_width), F32)],
        compiler_params=_params("arbitrary"))(qk, qk, qk, qkv, qkv, do, sinks)


def _blocked(w):
    return w.reshape(N_DEV, w.shape[0] // N_DEV, w.shape[1])


def kernel(x, positions, pre_mix_g, post_mix_g, pre_ffn_g, post_ffn_g, a_w_in, a_b_in, a_ln_g, a_ln_b, a_w_s, a_b_s, a_w_out, b_w_qkv, b_b_qkv, b_sinks, b_w_o, ffn_w_gu, ffn_w_down, loss_target, m_pre_mix_g, m_post_mix_g, m_pre_ffn_g, m_post_ffn_g, m_a_w_in, m_a_b_in, m_a_ln_g, m_a_ln_b, m_a_w_s, m_a_b_s, m_a_w_out, m_b_w_qkv, m_b_b_qkv, m_b_sinks, m_b_w_o, m_ffn_w_gu, m_ffn_w_down, v_pre_mix_g, v_post_mix_g, v_pre_ffn_g, v_post_ffn_g, v_a_w_in, v_a_b_in, v_a_ln_g, v_a_ln_b, v_a_w_s, v_a_b_s, v_a_w_out, v_b_w_qkv, v_b_b_qkv, v_b_sinks, v_b_w_o, v_ffn_w_gu, v_ffn_w_down):
    weights = dict(pre_mix_g=pre_mix_g, post_mix_g=post_mix_g, pre_ffn_g=pre_ffn_g, post_ffn_g=post_ffn_g,
                   a_w_in=a_w_in, a_b_in=a_b_in, a_ln_g=a_ln_g, a_ln_b=a_ln_b, a_w_s=a_w_s, a_b_s=a_b_s,
                   a_w_out=a_w_out, b_w_qkv=b_w_qkv, b_b_qkv=b_b_qkv, b_sinks=b_sinks, b_w_o=b_w_o,
                   ffn_w_gu=ffn_w_gu, ffn_w_down=ffn_w_down)
    mom_m = dict(pre_mix_g=m_pre_mix_g, post_mix_g=m_post_mix_g, pre_ffn_g=m_pre_ffn_g, post_ffn_g=m_post_ffn_g,
                 a_w_in=m_a_w_in, a_b_in=m_a_b_in, a_ln_g=m_a_ln_g, a_ln_b=m_a_ln_b, a_w_s=m_a_w_s, a_b_s=m_a_b_s,
                 a_w_out=m_a_w_out, b_w_qkv=m_b_w_qkv, b_b_qkv=m_b_b_qkv, b_sinks=m_b_sinks, b_w_o=m_b_w_o,
                 ffn_w_gu=m_ffn_w_gu, ffn_w_down=m_ffn_w_down)
    mom_v = dict(pre_mix_g=v_pre_mix_g, post_mix_g=v_post_mix_g, pre_ffn_g=v_pre_ffn_g, post_ffn_g=v_post_ffn_g,
                 a_w_in=v_a_w_in, a_b_in=v_a_b_in, a_ln_g=v_a_ln_g, a_ln_b=v_a_ln_b, a_w_s=v_a_w_s, a_b_s=v_a_b_s,
                 a_w_out=v_a_w_out, b_w_qkv=v_b_w_qkv, b_b_qkv=v_b_b_qkv, b_sinks=v_b_sinks, b_w_o=v_b_w_o,
                 ffn_w_gu=v_ffn_w_gu, ffn_w_down=v_ffn_w_down)
    names = list(weights)
    big = ("a_w_in", "b_w_qkv", "ffn_w_gu", "a_w_out", "b_w_o", "ffn_w_down")

    T, D = x.shape[1], x.shape[2]
    depth = pre_mix_g.shape[0]
    n_heads = b_sinks.shape[1]
    q_width = n_heads * HEAD_DIM
    kv_width = N_KV_HEADS * HEAD_DIM
    G = a_w_s.shape[1]
    W = a_ln_g.shape[1]
    device = _index(_position())

    dev = device.reshape(1).astype(jnp.int32)

    def layer_keys(i):
        mixer = ("a_w_in", "a_w_out") if i % 2 == 0 else ("b_w_qkv", "b_w_o")
        return [(k, i // 2) for k in mixer] + [("ffn_w_gu", i), ("ffn_w_down", i)]

    as_view = ("b_w_qkv", "ffn_w_gu")
    in_kernel = ("a_w_in",)
    view = lambda k, t: jnp.swapaxes(t, 1, 2) if k in as_view else t
    w_view = {k: view(k, weights[k]) for k in big}
    m_view = {k: view(k, mom_m[k]) for k in big}
    v_view = {k: view(k, mom_v[k]) for k in big}

    groups_keys = [[key] for key in layer_keys(0)] + [layer_keys(i) for i in range(1, depth)]
    group_of = {}
    for g, keys in enumerate(groups_keys):
        layer = 0 if g < len(layer_keys(0)) else g - len(layer_keys(0)) + 1
        for pos, (k, _) in enumerate(keys):
            group_of[(layer, k)] = (g, pos)
    bias_land = lax.dynamic_update_slice(
        jnp.zeros((N_DEV,) + b_b_qkv.shape, F32), b_b_qkv[None], (device, 0, 0))
    h, hn = x[0], first_norm(x[0], pre_mix_g[0][None], "norm_first")
    first = len(layer_keys(0))
    in_gather, token = [], ()
    for name, lo, hi in (("gather_start_first", 0, first), ("gather_start_rest", first, len(groups_keys))):
        groups = [list(prep_weights([(w_view[k], l, k in in_kernel) for k, l in keys], dev, f"prep_{name}_{g}",
                                    after=token)) for g, keys in enumerate(groups_keys[lo:hi])]
        if lo:
            groups[0].append(bias_land)
        gather_sems, flat_lands, token = relay_start(groups, name)
        token = (token, hn)
        for grp, sems in zip(groups, gather_sems):
            in_gather.append(([flat_lands.pop(0) for _ in grp], sems))
    all_started = token[0]
    passed, gathered = {}, {}

    def pass_on(g, after):
        if g not in passed and g < len(in_gather):
            lands, sems = in_gather[g]
            passed[g], lands = relay_pass(lands, sems, after, f"gather_pass_{g}")
            in_gather[g] = (lands, sems)

    pass_on(0, (all_started,))

    def weight(i, k, *after):
        g, pos = group_of[(i, k)]
        if g not in gathered:
            pass_on(g + 1, after)
            lands, sems = in_gather[g]
            gathered[g] = relay_wait(lands, sems, passed[g], after, f"gather_wait_{g}")
        w = gathered[g][pos]
        return w.reshape(N_DEV * w.shape[1], w.shape[2])

    causal = jnp.tril(jnp.ones((CHUNK, CHUNK), dtype=bool))
    w_mask = jnp.where(causal[None, None], a_w_s, 0.0).astype(BF16)
    w_mask_t = jnp.swapaxes(w_mask, 2, 3)
    b_s_t = jnp.swapaxes(a_b_s, 1, 2)
    group_onehot = (jnp.arange(W)[:, None] // (W // G) == jnp.arange(LANES)[None, :]).astype(BF16)
    lane = jnp.arange(LANES) % HEAD_DIM
    inv_freq = ROPE_THETA ** (-jnp.arange(0, ROPE_DIM, 2, dtype=F32) / ROPE_DIM)
    inv_freq_row = jnp.where(lane < ROPE_DIM, inv_freq[lane % (ROPE_DIM // 2)], 0.0)[None, :].astype(F32)
    tabs = rope_tables(positions.reshape(T, 1), inv_freq_row, "rope_tables")

    saved = []
    for i in range(depth):
        j = i // 2
        s = dict(h=h, hn=hn)
        post_pre = (post_mix_g[i][None], pre_ffn_g[i][None])
        if i % 2 == 0:
            s["a"] = rowmm(hn, weight(i, "a_w_in", hn, all_started), True, a_b_in[j][None], BF16, "gmlp_in")
            s["gated"], s["mix"], s["h2"], s["fn"] = gmlp_gate_out(
                s["a"], a_ln_g[j][None], a_ln_b[j][None], w_mask[j], b_s_t[j], weight(i, "a_w_out", s["a"]),
                h, *post_pre, "gmlp_gate_out")
        else:
            w_qkv = weight(i, "b_w_qkv", hn)
            b_qkv_full = jnp.swapaxes(gathered[group_of[(1, "b_w_qkv")][0]][-1], 0, 1).reshape(b_b_qkv.shape[0], 1, -1)
            s["qkv"] = rowmm(hn, w_qkv, True, b_qkv_full[j], BF16, "attn_qkv")
            s["qk"] = rope_fwd(s["qkv"], tabs, q_width, kv_width, "attn_rope")
            s["o"] = attn_fwd(s["qk"], s["qkv"], b_sinks[j], q_width, kv_width, "attn_core")
            s["mix"], s["h2"], s["fn"] = mm_add_norm(s["o"], weight(i, "b_w_o", s["o"]), h, *post_pre, "attn_out")
        s["act"], s["act_partials"] = ffn_up(s["fn"], weight(i, "ffn_w_gu", s["fn"]), "ffn_up")
        w_down = weight(i, "ffn_w_down", s["act"])
        saved.append(s)
        if i + 1 < depth:
            s["f"], h, hn = mm_add_norm(s["act"], w_down, s["h2"], post_ffn_g[i][None], pre_mix_g[i + 1][None], "ffn_down")
    small = {k: [None] * weights[k].shape[0] for k in names if k not in big}
    dh, df, loss_sum, small["post_ffn_g"][depth - 1] = mm_add_norm_loss(
        saved[-1]["act"], w_down, saved[-1]["h2"], post_ffn_g[depth - 1][None], loss_target[0], "ffn_down_loss")

    grads = {k: [None] * weights[k].shape[0] for k in big}
    in_flight = []

    def send_grads(keys, tag):
        sems, bufs, token = exchange_start([_blocked(grads[k][l]) for k, l in keys], "exchange_start_" + tag)
        in_flight.append((keys, sems, bufs, tag))
        return token

    replicated = [k for k in names if k not in big and k not in ("b_b_qkv", "a_w_s")]
    main = sum(weights[k].size for k in replicated)
    main_rows = -(-main // (LANES * LANES)) * LANES
    bias_size = b_b_qkv.shape[0] * N_DEV * b_b_qkv.shape[1]
    bias_rows = -(-bias_size // (8 * LANES)) * 8
    flat = lambda vals: [v.reshape(-1) for v in vals]

    def start_small_gather():
        packed = jnp.concatenate(
            flat(p for k in replicated for p in small[k])
            + [loss_sum[0, :1], jnp.zeros((main_rows * LANES - main - 1,), F32)] + flat(small["b_b_qkv"]) + [jnp.zeros((bias_rows * LANES - bias_size,), F32)]
        ).reshape(1, main_rows + bias_rows, LANES)
        spatial = jnp.stack(small["a_w_s"]).reshape(1, -1, LANES)
        return gather_start([prep_weights([(packed, 0, False), (spatial, 0, False)], dev, "place_small_grads", F32)],
                            "gather_start_small")

    sent = ()
    for i in reversed(range(depth)):
        j = i // 2
        s = saved[i]
        dgu = ffn_dact(df, weight(i, "ffn_w_down"), s["act_partials"], "ffn_dact", after=sent)
        alone = i == 0
        grads["ffn_w_down"][i] = grad_mm(s["act"], df, "ffn_down_grad")
        early = (send_grads(layer_keys(i)[3:], f"ffn_down_{i}"),) if alone else ()
        grads["ffn_w_gu"][i] = grad_mm(dgu, s["fn"], "ffn_up_grad", after=early)
        sent = send_grads(layer_keys(i)[2:3] if alone else layer_keys(i)[2:], f"ffn_{i}")
        r = mm_norm_bwd(dgu, weight(i, "ffn_w_gu"), dh, s["h2"], pre_ffn_g[i][None],
                        (s["mix"], post_mix_g[i][None]), "ffn_dx", sent)
        dh, dmix = r["dh"], r["dy"]
        small["pre_ffn_g"][i], small["post_mix_g"][i] = r["dg_pre"], r["dg_post"]
        post = (saved[i - 1]["f"], post_ffn_g[i - 1][None]) if i > 0 else None
        if i % 2 == 0:
            grads["a_w_out"][j] = grad_mm(s["gated"], dmix, "gmlp_out_grad")
            if alone:
                sent = send_grads(layer_keys(i)[1:2], f"mixer_out_{i}")
            r = gmlp_gate_bwd_dx(
                s["a"], dmix, weight(i, "a_w_out"), a_ln_g[j][None], a_ln_b[j][None], w_mask[j], w_mask_t[j], b_s_t[j],
                group_onehot, weight(i, "a_w_in"), dh, s["h"], pre_mix_g[i][None], post, "gmlp_gate_bwd_dx", sent)
            small["a_w_s"][j], small["a_b_s"][j] = r["dw_s"], r["db_s_t"][:, :G].T
            small["a_ln_g"][j], small["a_ln_b"][j], small["a_b_in"][j] = r["dln_g"][0], r["dln_b"][0], r["db_in"][0]
            small["pre_mix_g"][i] = r["dg_pre"]
            first_layer = ()
            if i == 0:
                small_gather = start_small_gather()
                first_layer = (small_gather[2],)
            grads["a_w_in"][j] = grad_mm(r["da"], s["hn"], "gmlp_in_grad", after=first_layer)
            sent = (send_grads(layer_keys(i)[:1] if alone else layer_keys(i)[:2], f"mixer_{i}"),)
        else:
            grads["b_w_o"][j] = grad_mm(s["o"], dmix, "attn_out_grad")
            do = rowmm(dmix, weight(i, "b_w_o"), True, None, BF16, "attn_do")
            dq, dk, dv, dsink = attn_bwd(s["qk"], s["qkv"], do, b_sinks[j], q_width, kv_width, "attn_core_bwd")
            dqkv, dbias = rope_bwd(dq, dk, dv, tabs, "attn_rope_bwd")
            small["b_sinks"][j], small["b_b_qkv"][j] = dsink[0, :n_heads], dbias[0]
            grads["b_w_qkv"][j] = grad_mm(dqkv, s["hn"], "attn_qkv_grad")
            sent = send_grads(layer_keys(i)[:2], f"mixer_{i}")
            r = mm_norm_bwd(dqkv, weight(i, "b_w_qkv"), dh, s["h"], pre_mix_g[i][None], post, "attn_dx", sent)
            sent = ()
        dh, small["pre_mix_g"][i] = r["dh"], r["dg_pre"]
        if i > 0:
            df, small["post_ffn_g"][i - 1] = r["dy"], r["dg_post"]
    grad_x = dh[None]

    small_sems, small_lands, small_sent = small_gather
    pack_main = lambda src: jnp.concatenate(
        flat(src[k] for k in replicated) + [jnp.ones((main_rows * LANES - main,), F32)]).reshape(main_rows, LANES)
    packed_w, packed_m, packed_v = pack_main(weights), pack_main(mom_m), pack_main(mom_v)

    stacked = {k: [lax.empty(w_view[k].shape, F32) for _ in range(4)] for k in big}
    after = (small_sent, dh) + sent
    for keys, sems, bufs, tag in in_flight:
        partials, landed = exchange_wait(bufs, sems, after, "exchange_wait_" + tag)
        for (k, l), part, land in zip(keys, partials, landed):
            stacked[k] = sum_adamw(part, land, w_view[k], m_view[k], v_view[k], stacked[k], l,
                                   k in in_kernel, dev, "adamw_" + k)
        after = tuple(stacked[k][0] for k, _ in keys)
    out_g, out_d, out_m, out_v = {}, {}, {}, {}
    for k in big:
        out_g[k], out_d[k], out_m[k], out_v[k] = [view(k, t) for t in stacked[k]]

    gathered_small, gathered_spatial = gather_wait(small_lands, small_sems[0], after + (packed_w, packed_m, packed_v),
                                                   "gather_wait_small")
    rows_of = lambda t: t.reshape(-1, LANES)
    spatial = sum_adamw_small(gathered_spatial, rows_of(a_w_s), rows_of(m_a_w_s), rows_of(v_a_w_s), "adamw_a_w_s")
    out_g["a_w_s"], out_d["a_w_s"], out_m["a_w_s"], out_v["a_w_s"] = [t.reshape(a_w_s.shape) for t in spatial]
    results = sum_adamw_small(gathered_small, packed_w, packed_m, packed_v, "adamw_small")
    loss = results[0].reshape(-1)[main] * (0.5 / D)
    off = 0
    for k in replicated:
        size, shape = weights[k].size, weights[k].shape
        out_g[k], out_d[k], out_m[k], out_v[k] = [t.reshape(-1)[off:off + size].reshape(shape) for t in results]
        off += size
    n_local = b_b_qkv.shape[1]
    bias_sum = sum_parts(gathered_small[:, main_rows:], "sum_bias_grads").reshape(-1)[:bias_size]
    out_g["b_b_qkv"] = lax.dynamic_slice_in_dim(
        bias_sum.reshape(b_b_qkv.shape[0], -1), device * n_local, n_local, axis=1)
    pad = lambda t: jnp.concatenate([t.reshape(-1), jnp.ones((8 * LANES - t.size,), F32)]).reshape(8, LANES)
    bias_update = adamw(pad(b_b_qkv), pad(out_g["b_b_qkv"]), pad(m_b_b_qkv), pad(v_b_b_qkv), "adamw_bias")
    out_d["b_b_qkv"], out_m["b_b_qkv"], out_v["b_b_qkv"] = [
        t.reshape(-1)[:b_b_qkv.size].reshape(b_b_qkv.shape) for t in bias_update]

    return (loss, grad_x, *[out_g[k] for k in names], *[out_d[k] for k in names],
            *[out_m[k] for k in names], *[out_v[k] for k in names])
```

```python
import math

import jax
import jax.numpy as jnp
from jax import lax
from jax.experimental import pallas as pl
from jax.experimental.pallas import tpu as pltpu

F32, BF16 = jnp.float32, jnp.bfloat16
MESH = pl.DeviceIdType.MESH
N_DEV = 8

CHUNK = 128
HEAD_DIM = 64
N_KV_HEADS = 4
ROPE_DIM = HEAD_DIM // 4
ROPE_THETA = 500000.0
RMS_EPS = 1e-6
LN_EPS = 1e-5
NEG_INF = -1e30

ADAM_LR = 0.001
ADAM_B1 = 0.9
ADAM_B2 = 0.999
ADAM_EPS = 1e-08
ADAM_WD = 0.01
ADAM_STEP = 10

V7X_VMEM_BYTES = 64 * 2 ** 20
VMEM_LIMIT = V7X_VMEM_BYTES - 8 * 2 ** 20
LANES = 128


def _params(*sem):
    return pltpu.CompilerParams(dimension_semantics=sem or None, vmem_limit_bytes=VMEM_LIMIT)


def _tile(n, pref):
    if n <= pref:
        return n
    t = pref - pref % LANES
    while t >= LANES:
        if n % t == 0:
            return t
        t -= LANES
    return n


def _full(shape):
    return pl.BlockSpec(shape, lambda *_: (0,) * len(shape))


def _resident(shape):
    return pl.BlockSpec(shape, lambda *_: (0,) * len(shape), pipeline_mode=pl.Buffered(1))


def _rows(tm, width):
    return pl.BlockSpec((tm, width), lambda i: (i, 0))


def _rms(x, g):
    r = lax.rsqrt(jnp.mean(x * x, axis=-1, keepdims=True) + RMS_EPS)
    return x * r * g


def _rms_bwd(x, g, dy):
    r = lax.rsqrt(jnp.mean(x * x, axis=-1, keepdims=True) + RMS_EPS)
    xhat = x * r
    dg = jnp.sum(dy * xhat, axis=0, keepdims=True)
    dxhat = dy * g
    dx = r * (dxhat - xhat * jnp.mean(dxhat * xhat, axis=-1, keepdims=True))
    return dx, dg


_INV_SQRT2 = 1.0 / math.sqrt(2.0)
_INV_SQRT2PI = 1.0 / math.sqrt(2.0 * math.pi)


def _gelu(x):
    return 0.5 * x * (1.0 + lax.erf(x * _INV_SQRT2))


def _gelu_and_grad(x):
    cdf = 0.5 * (1.0 + lax.erf(x * _INV_SQRT2))
    return x * cdf, cdf + x * jnp.exp(-0.5 * x * x) * _INV_SQRT2PI


def _sigmoid(x):
    return 0.5 * jnp.tanh(0.5 * x) + 0.5


def _position():
    return lax.axis_index("x"), lax.axis_index("y"), lax.axis_index("c")


def _index(p):
    return 4 * p[0] + 2 * p[1] + p[2]


_HBM = pl.BlockSpec(memory_space=pltpu.HBM)
_SEM = pl.BlockSpec(memory_space=pltpu.SEMAPHORE)
_ORDERED_BY_DATA = pltpu.SideEffectType.DATAFLOW_SIDE_EFFECTING


def _in_hbm(v):
    return pltpu.with_memory_space_constraint(v, pltpu.HBM)


def _peer(k):
    x, y, c = _position()
    return (x ^ ((k >> 2) & 1), y ^ ((k >> 1) & 1), c ^ (k & 1))


def _split_copies(srcs, dsts, send, recv):
    return [pltpu.make_async_remote_copy(
        src_ref=src(k), dst_ref=dst(k), send_sem=send.at[7 * a + k - 1], recv_sem=recv.at[7 * a + k - 1],
        device_id=_peer(k), device_id_type=MESH)
        for a, (src, dst) in enumerate(zip(srcs, dsts)) for k in range(1, N_DEV)]


def _start_call(groups, sent, copies_of, name, per_array=N_DEV - 1):
    flat = [v for grp in groups for v in grp]
    n, ng = len(flat), len(groups)

    def body(*refs):
        bufs, sems = refs[:n], refs[n:n + 2 * ng]
        q = 0
        for g, grp in enumerate(groups):
            for cp in copies_of(bufs[q:q + len(grp)], sems[2 * g], sems[2 * g + 1]):
                cp.start()
            q += len(grp)
        refs[-1][...] = jnp.zeros_like(refs[-1])

    sem_shapes = []
    for count in sent:
        sem_shapes += [pltpu.SemaphoreType.DMA((per_array * count,)) for _ in range(2)]
    res = pl.pallas_call(
        body, name=name,
        out_shape=sem_shapes + [pltpu.HBM(v.shape, v.dtype) for v in flat] + [jax.ShapeDtypeStruct((8, LANES), F32)],
        in_specs=[_HBM] * n,
        out_specs=[_SEM] * (2 * ng) + [_HBM] * n + [pl.BlockSpec(memory_space=pltpu.VMEM)],
        input_output_aliases={i: 2 * ng + i for i in range(n)},
        compiler_params=pltpu.CompilerParams(has_side_effects=_ORDERED_BY_DATA),
    )(*[_in_hbm(v) for v in flat])
    sems = [(res[2 * g], res[2 * g + 1]) for g in range(ng)]
    return sems, list(res[2 * ng:-1]), res[-1]


def _wait_call(bufs, sems, copies_of, after, name):
    n = len(bufs)

    def body(*refs):
        for cp in copies_of(refs[:n], refs[n], refs[n + 1]):
            cp.wait_send()
            cp.wait_recv()

    return list(pl.pallas_call(
        body, name=name,
        out_shape=[pltpu.HBM(v.shape, v.dtype) for v in bufs],
        in_specs=[_HBM] * n + [_SEM, _SEM] + [pl.BlockSpec(memory_space=pl.ANY)] * len(after),
        out_specs=[_HBM] * n,
        input_output_aliases={i: i for i in range(n)},
        compiler_params=pltpu.CompilerParams(has_side_effects=_ORDERED_BY_DATA),
    )(*bufs, sems[0], sems[1], *after))


def _gather_copies(lands, send, recv):
    me = _index(_position())
    mine = [lambda k, ref=ref: ref.at[pl.ds(me, 1)] for ref in lands]
    return _split_copies(mine, mine, send, recv)


def _gather_arrivals(lands, send, recv):
    me = _index(_position())
    mine = [lambda k, ref=ref: ref.at[pl.ds(me, 1)] for ref in lands]
    theirs = [lambda k, ref=ref: ref.at[pl.ds(_index(_peer(k)), 1)] for ref in lands]
    return _split_copies(mine, theirs, send, recv)


def _exchange_copies(bufs, send, recv):
    half = len(bufs) // 2
    srcs = [lambda k, ref=ref: ref.at[pl.ds(_index(_peer(k)), 1)] for ref in bufs[:half]]
    dsts = [lambda k, ref=ref: ref.at[pl.ds(k - 1, 1)] for ref in bufs[half:]]
    return _split_copies(srcs, dsts, send, recv)


def gather_start(groups, name):
    return _start_call(groups, [len(grp) for grp in groups], _gather_copies, name)


def gather_wait(lands, sems, after, name):
    return _wait_call(lands, sems, _gather_arrivals, after, name)


_DIRECT = (1, 2, 4, 6)
_PASSED = (3, 5, 7)


def _remote(src, dst, send, recv, q, k):
    return pltpu.make_async_remote_copy(src_ref=src, dst_ref=dst, send_sem=send.at[q], recv_sem=recv.at[q],
                                        device_id=_peer(k), device_id_type=MESH)


def _slot_of(land, k):
    return land.at[pl.ds(_index(_peer(k)), 1)]


def relay_start(groups, name):
    def copies(lands, send, recv):
        return [_remote(_slot_of(land, 0), _slot_of(land, 0), send, recv, len(_DIRECT) * a + j, k)
                for a, land in enumerate(lands) for j, k in enumerate(_DIRECT)]
    return _start_call(groups, [len(grp) for grp in groups], copies, name, per_array=len(_DIRECT))


def relay_pass(lands, sems, after, name):
    n = len(lands)

    def body(*refs):
        bufs, recv_first = refs[:n], refs[n]
        send, recv = refs[n + 1 + len(after):n + 3 + len(after)]
        for a, land in enumerate(bufs):
            for j, k in enumerate(_PASSED):
                came = _slot_of(land, k ^ 1)
                _remote(came, came, recv_first, recv_first, len(_DIRECT) * a + _DIRECT.index(k ^ 1), k ^ 1).wait_recv()
                _remote(came, came, send, recv, len(_PASSED) * a + j, 1).start()

    res = pl.pallas_call(
        body, name=name,
        out_shape=[pltpu.SemaphoreType.DMA((len(_PASSED) * n,))] * 2 + [pltpu.HBM(v.shape, v.dtype) for v in lands],
        in_specs=[_HBM] * n + [_SEM] + [pl.BlockSpec(memory_space=pl.ANY)] * len(after),
        out_specs=[_SEM, _SEM] + [_HBM] * n,
        input_output_aliases={i: 2 + i for i in range(n)},
        compiler_params=pltpu.CompilerParams(has_side_effects=_ORDERED_BY_DATA),
    )(*lands, sems[1], *after)
    return (res[0], res[1]), list(res[2:])


def relay_wait(lands, first, second, after, name):
    n = len(lands)

    def body(*refs):
        bufs = refs[:n]
        send_first, recv_first, send, recv = refs[n:n + 4]
        for a, land in enumerate(bufs):
            mine = _slot_of(land, 0)
            _remote(mine, _slot_of(land, 1), send_first, recv_first, len(_DIRECT) * a, 1).wait_recv()
            for j, k in enumerate(_DIRECT):
                _remote(mine, mine, send_first, recv_first, len(_DIRECT) * a + j, k).wait_send()
            for j, k in enumerate(_PASSED):
                cp = _remote(_slot_of(land, k ^ 1), _slot_of(land, k), send, recv, len(_PASSED) * a + j, 1)
                cp.wait_send()
                cp.wait_recv()

    return list(pl.pallas_call(
        body, name=name,
        out_shape=[pltpu.HBM(v.shape, v.dtype) for v in lands],
        in_specs=[_HBM] * n + [_SEM] * 4 + [pl.BlockSpec(memory_space=pl.ANY)] * len(after),
        out_specs=[_HBM] * n,
        input_output_aliases={i: i for i in range(n)},
        compiler_params=pltpu.CompilerParams(has_side_effects=_ORDERED_BY_DATA),
    )(*lands, first[0], first[1], second[0], second[1], *after))


def exchange_start(partials, name):
    lands = [lax.empty((N_DEV - 1,) + p.shape[1:], p.dtype) for p in partials]
    sems, bufs, token = _start_call([list(partials) + lands], [len(partials)], _exchange_copies, name)
    return sems[0], bufs, token


def exchange_wait(bufs, sems, after, name):
    bufs = _wait_call(bufs, sems, _exchange_copies, after, name)
    return bufs[:len(bufs) // 2], bufs[len(bufs) // 2:]


def prep_weights(items, dev, name, dtype=BF16, after=()):
    n = len(items)
    blocks = [(1, w.shape[2], w.shape[1]) if t else (1,) + w.shape[1:] for w, _, t in items]

    def body(d_ref, *refs):
        for (_, _, transpose), w_ref, o_ref in zip(items, refs[:n], refs[-n:]):
            v = w_ref[0]
            o_ref[0] = (v.T if transpose else v).astype(dtype)

    return pl.pallas_call(
        body, name=name,
        grid_spec=pltpu.PrefetchScalarGridSpec(
            num_scalar_prefetch=1, grid=(1,),
            in_specs=[pl.BlockSpec((1,) + w.shape[1:], lambda i, d, layer=layer: (layer, 0, 0)) for w, layer, _ in items]
            + [pl.BlockSpec(memory_space=pl.ANY)] * len(after),
            out_specs=[pl.BlockSpec(blk, lambda i, d: (d[0], 0, 0)) for blk in blocks]),
        out_shape=[jax.ShapeDtypeStruct((N_DEV,) + blk[1:], dtype) for blk in blocks],
        compiler_params=_params("arbitrary"),
    )(dev, *[w for w, _, _ in items], *after)


def _adamw_math(w, g, m, v):
    mn = ADAM_B1 * m + (1.0 - ADAM_B1) * g
    vn = ADAM_B2 * v + (1.0 - ADAM_B2) * (g * g)
    m_hat = mn * (1.0 / (1.0 - ADAM_B1 ** ADAM_STEP))
    v_hat = vn * (1.0 / (1.0 - ADAM_B2 ** ADAM_STEP))
    return -ADAM_LR * (m_hat / (jnp.sqrt(v_hat) + ADAM_EPS) + ADAM_WD * w), mn, vn


def sum_adamw(partial, landed, w, m, v, prev, layer, transpose, dev, name):
    _, r, c = partial.shape
    if transpose:
        tc = _tile(c, 256)
        grid = (c // tc,)
        part_spec = pl.BlockSpec((1, r, tc), lambda i, d: (d[0], 0, i))
        land_spec = pl.BlockSpec((N_DEV - 1, r, tc), lambda i, d: (0, 0, i))
        w_spec = pl.BlockSpec((1, tc, r), lambda i, d: (layer, i, 0))
    else:
        tr = r // 2 if r % 32 == 0 else r
        grid = (r // tr,)
        part_spec = pl.BlockSpec((1, tr, c), lambda i, d: (d[0], i, 0))
        land_spec = pl.BlockSpec((N_DEV - 1, tr, c), lambda i, d: (0, i, 0))
        w_spec = pl.BlockSpec((1, tr, c), lambda i, d: (layer, i, 0))

    def body(d_ref, p_ref, l_ref, w_ref, m_ref, v_ref, *rest):
        g_ref, dl_ref, mo_ref, vo_ref = rest[-4:]
        acc = p_ref[0].astype(F32)
        for k in range(N_DEV - 1):
            acc = acc + l_ref[k].astype(F32)
        g = acc.T if transpose else acc
        delta, mn, vn = _adamw_math(w_ref[0], g, m_ref[0], v_ref[0])
        g_ref[0], dl_ref[0], mo_ref[0], vo_ref[0] = g, delta, mn, vn

    return pl.pallas_call(
        body, name=name,
        grid_spec=pltpu.PrefetchScalarGridSpec(
            num_scalar_prefetch=1, grid=grid,
            in_specs=[part_spec, land_spec, w_spec, w_spec, w_spec] + [pl.BlockSpec(memory_space=pl.ANY)] * 4,
            out_specs=[w_spec] * 4),
        out_shape=[jax.ShapeDtypeStruct(w.shape, F32)] * 4,
        input_output_aliases={6 + q: q for q in range(4)},
        compiler_params=_params("parallel"),
    )(dev, partial, landed, w, m, v, *prev)


def sum_parts(parts, name):
    P, r, c = parts.shape

    def body(p_ref, o_ref):
        acc = p_ref[0].astype(F32)
        for s in range(1, P):
            acc = acc + p_ref[s].astype(F32)
        o_ref[...] = acc

    return pl.pallas_call(
        body, name=name, in_specs=[_full((P, r, c))], out_specs=_full((r, c)), grid=(1,),
        out_shape=jax.ShapeDtypeStruct((r, c), F32), compiler_params=_params("arbitrary"))(parts)


def adamw(w, g, m, v, name):
    R, C = w.shape
    tr = _tile(R, 512)

    def body(w_ref, g_ref, m_ref, v_ref, d_ref, mo_ref, vo_ref):
        d_ref[...], mo_ref[...], vo_ref[...] = _adamw_math(w_ref[...], g_ref[...], m_ref[...], v_ref[...])

    spec = pl.BlockSpec((tr, C), lambda i: (i, 0))
    return pl.pallas_call(
        body, name=name, grid=(R // tr,),
        in_specs=[spec] * 4, out_specs=[spec] * 3,
        out_shape=[jax.ShapeDtypeStruct((R, C), F32)] * 3,
        compiler_params=_params("parallel"),
    )(w, g, m, v)


def sum_adamw_small(gathered, w, m, v, name):
    rows = w.shape[0]
    tr = _tile(rows, 512)

    def body(p_ref, w_ref, m_ref, v_ref, g_ref, d_ref, mo_ref, vo_ref):
        g = p_ref[0]
        for s in range(1, N_DEV):
            g = g + p_ref[s]
        g_ref[...] = g
        d_ref[...], mo_ref[...], vo_ref[...] = _adamw_math(w_ref[...], g, m_ref[...], v_ref[...])

    spec = pl.BlockSpec((tr, LANES), lambda i: (i, 0))
    return pl.pallas_call(
        body, name=name, grid=(rows // tr,),
        in_specs=[pl.BlockSpec((N_DEV, tr, LANES), lambda i: (0, i, 0)), spec, spec, spec], out_specs=[spec] * 4,
        out_shape=[jax.ShapeDtypeStruct((rows, LANES), F32)] * 4, compiler_params=_params("parallel"))(gathered, w, m, v)


def first_norm(h, g, name):
    T, D = h.shape
    tm = _tile(T, 512)

    def body(h_ref, g_ref, hn_ref):
        hn_ref[...] = _rms(h_ref[...], g_ref[...]).astype(BF16)

    row = _rows(tm, D)
    return pl.pallas_call(
        body, name=name, grid=(T // tm,), in_specs=[row, _full((1, D))], out_specs=row,
        out_shape=jax.ShapeDtypeStruct((T, D), BF16), compiler_params=_params("parallel"))(h, g)


_NT = (((1,), (1,)), ((), ()))
_TN = (((0,), (0,)), ((), ()))


def rowmm(a, w, transposed_w, bias, out_dtype, name, after=None):
    M, K = a.shape
    N = w.shape[0] if transposed_w else w.shape[1]
    tm = _tile(M, 512)
    tn = _tile(N, 512)

    def body(*refs):
        a_ref, w_ref = refs[:2]
        o_ref = refs[-1]
        av = a_ref[...]
        for j in range(N // tn):
            cols = pl.ds(j * tn, tn)
            if transposed_w:
                acc = lax.dot_general(av, w_ref[cols, :], _NT, preferred_element_type=F32)
            else:
                acc = jnp.dot(av, w_ref[:, cols], preferred_element_type=F32)
            if bias is not None:
                acc = acc + refs[2][:, cols]
            o_ref[:, cols] = acc.astype(out_dtype)

    ins, in_specs = [a, w], [_rows(tm, K), _resident(w.shape)]
    if bias is not None:
        ins.append(bias)
        in_specs.append(_full((1, N)))
    if after is not None:
        ins.append(after)
        in_specs.append(pl.BlockSpec(memory_space=pl.ANY))
    return pl.pallas_call(
        body, name=name, grid=(M // tm,), in_specs=in_specs, out_specs=_rows(tm, N),
        out_shape=jax.ShapeDtypeStruct((M, N), out_dtype), compiler_params=_params("parallel"))(*ins)


def _rows_times_weight(a_ref, w_ref, acc_ref):
    N = w_ref.shape[1]
    tn = _tile(N, 512)
    av = a_ref[...]
    for j in range(N // tn):
        cols = pl.ds(j * tn, tn)
        acc_ref[:, cols] = jnp.dot(av, w_ref[:, cols], preferred_element_type=F32)


def mm_add_norm(a, w, h, g_post, g_pre, name):
    T, K = a.shape
    D = w.shape[1]
    tm = _tile(T, 512)

    def body(a_ref, w_ref, h_ref, gp_ref, g_ref, y_ref, ho_ref, hn_ref, acc_ref):
        _rows_times_weight(a_ref, w_ref, acc_ref)
        y = acc_ref[...]
        y_ref[...] = y.astype(BF16)
        hv = h_ref[...] + _rms(y, gp_ref[...])
        ho_ref[...] = hv
        hn_ref[...] = _rms(hv, g_ref[...]).astype(BF16)

    row, vec = _rows(tm, D), _full((1, D))
    return pl.pallas_call(
        body, name=name, grid=(T // tm,), in_specs=[_rows(tm, K), _resident(w.shape), row, vec, vec],
        out_specs=[row, row, row],
        out_shape=[jax.ShapeDtypeStruct((T, D), BF16), jax.ShapeDtypeStruct((T, D), F32),
                   jax.ShapeDtypeStruct((T, D), BF16)],
        scratch_shapes=[pltpu.VMEM((tm, D), F32)], compiler_params=_params("parallel"))(a, w, h, g_post, g_pre)


def mm_add_norm_loss(a, w, h, g_post, target, name):
    T, K = a.shape
    D = w.shape[1]
    tm = _tile(T, 512)

    def body(a_ref, w_ref, h_ref, gp_ref, t_ref, dh_ref, dy_ref, loss_ref, dgp_ref, acc_ref):
        @pl.when(pl.program_id(0) == 0)
        def _():
            loss_ref[...] = jnp.zeros_like(loss_ref)
            dgp_ref[...] = jnp.zeros_like(dgp_ref)
        _rows_times_weight(a_ref, w_ref, acc_ref)
        y = acc_ref[...]
        err = h_ref[...] + _rms(y, gp_ref[...]) - t_ref[...]
        dh = err * (1.0 / D)
        dh_ref[...] = dh
        loss_ref[...] += jnp.sum(err * err)
        dy, dgp = _rms_bwd(y, gp_ref[...], dh)
        dy_ref[...] = dy.astype(BF16)
        dgp_ref[...] += dgp

    row, vec = _rows(tm, D), _full((1, D))
    return pl.pallas_call(
        body, name=name, grid=(T // tm,), in_specs=[_rows(tm, K), _resident(w.shape), row, vec, row],
        out_specs=[row, row, _full((8, LANES)), vec],
        out_shape=[jax.ShapeDtypeStruct((T, D), F32), jax.ShapeDtypeStruct((T, D), BF16),
                   jax.ShapeDtypeStruct((8, LANES), F32), jax.ShapeDtypeStruct((1, D), F32)],
        scratch_shapes=[pltpu.VMEM((tm, D), F32)], compiler_params=_params("arbitrary"))(a, w, h, g_post, target)


def mm_norm_bwd(a, w, dh_in, x, g_pre, post, name, after):
    T, K = a.shape
    D = w.shape[1]
    tm = _tile(T, 512)
    row, vec = _rows(tm, D), _full((1, D))
    ins, in_specs = [a, w, dh_in, x, g_pre], [_rows(tm, K), _resident(w.shape), row, row, vec]
    outs = [jax.ShapeDtypeStruct((T, D), F32), jax.ShapeDtypeStruct((1, D), F32)]
    out_specs, names = [row, vec], ["dh", "dg_pre"]
    if post is not None:
        ins += list(post)
        in_specs += [row, vec]
        outs += [jax.ShapeDtypeStruct((T, D), BF16), jax.ShapeDtypeStruct((1, D), F32)]
        out_specs += [row, vec]
        names += ["dy", "dg_post"]
    n_in = len(ins) + 1

    def body(*refs):
        a_ref, w_ref, dh_ref_in, x_ref, g_ref = refs[:5]
        out_refs, acc_ref = refs[n_in:-1], refs[-1]
        first = pl.program_id(0) == 0
        _rows_times_weight(a_ref, w_ref, acc_ref)
        dx, dg = _rms_bwd(x_ref[...], g_ref[...], acc_ref[...])
        dh = dh_ref_in[...] + dx
        out_refs[0][...] = dh

        @pl.when(first)
        def _():
            for r in out_refs[1::2]:
                r[...] = jnp.zeros_like(r)
        out_refs[1][...] += dg
        if post is not None:
            y_ref, gp_ref = refs[5:7]
            dy, dgp = _rms_bwd(y_ref[...].astype(F32), gp_ref[...], dh)
            out_refs[2][...] = dy.astype(BF16)
            out_refs[3][...] += dgp

    res = pl.pallas_call(
        body, name=name, grid=(T // tm,), in_specs=in_specs + [pl.BlockSpec(memory_space=pl.ANY)],
        out_specs=out_specs, out_shape=outs, scratch_shapes=[pltpu.VMEM((tm, D), F32)],
        compiler_params=_params("arbitrary"))(*ins, after)
    return dict(zip(names, res))


def grad_mm(a, b, name, after=()):
    T, N = a.shape
    K = b.shape[1]
    tn = _tile(N, 1408)
    tt = _tile(T, 1024)

    def body(a_ref, b_ref, *rest):
        o_ref, acc_ref = rest[-2:]
        t = pl.program_id(1)

        @pl.when(t == 0)
        def _():
            acc_ref[...] = jnp.zeros_like(acc_ref)
        acc_ref[...] += lax.dot_general(a_ref[...], b_ref[...], _TN, preferred_element_type=F32)

        @pl.when(t == pl.num_programs(1) - 1)
        def _():
            o_ref[...] = acc_ref[...].astype(BF16)

    return pl.pallas_call(
        body, name=name, grid=(N // tn, T // tt),
        in_specs=[pl.BlockSpec((tt, tn), lambda j, t: (t, j)), pl.BlockSpec((tt, K), lambda j, t: (t, 0))]
        + [pl.BlockSpec(memory_space=pl.ANY)] * len(after),
        out_specs=pl.BlockSpec((tn, K), lambda j, t: (j, 0)),
        out_shape=jax.ShapeDtypeStruct((N, K), BF16),
        scratch_shapes=[pltpu.VMEM((tn, K), F32)],
        compiler_params=_params("parallel", "arbitrary"))(a, b, *after)


def ffn_up(fn, w_gu_t, name):
    T, D = fn.shape
    F = w_gu_t.shape[0] // 2
    tm = _tile(T, 512)
    tn = _tile(F, 512)

    def body(a_ref, w_ref, act_ref, part_ref):
        av = a_ref[...]
        for j in range(F // tn):
            g = lax.dot_general(av, w_ref[pl.ds(j * tn, tn), :], _NT, preferred_element_type=F32)
            up = lax.dot_general(av, w_ref[pl.ds(F + j * tn, tn), :], _NT, preferred_element_type=F32)
            sg = _sigmoid(g)
            silu = g * sg
            act_ref[:, pl.ds(j * tn, tn)] = (silu * up).astype(BF16)
            part_ref[:, pl.ds(j * tn, tn)] = (up * (sg + silu * (1.0 - sg))).astype(BF16)
            part_ref[:, pl.ds(F + j * tn, tn)] = silu.astype(BF16)

    return pl.pallas_call(
        body, name=name, grid=(T // tm,), in_specs=[_rows(tm, D), _resident(w_gu_t.shape)],
        out_specs=[_rows(tm, F), _rows(tm, 2 * F)],
        out_shape=[jax.ShapeDtypeStruct((T, F), BF16), jax.ShapeDtypeStruct((T, 2 * F), BF16)],
        compiler_params=_params("parallel"))(fn, w_gu_t)


def ffn_dact(df, w_down, partials, name, after=()):
    T, D = df.shape
    F = w_down.shape[0]
    tm = _tile(T, 512)
    tn = _tile(F, 512)

    def body(d_ref, w_ref, part_ref, *rest):
        o_ref = rest[-1]
        dv = d_ref[...]
        for j in range(F // tn):
            dact = lax.dot_general(dv, w_ref[pl.ds(j * tn, tn), :], _NT, preferred_element_type=F32)
            for cols in (pl.ds(j * tn, tn), pl.ds(F + j * tn, tn)):
                o_ref[:, cols] = (dact * part_ref[:, cols].astype(F32)).astype(BF16)

    return pl.pallas_call(
        body, name=name, grid=(T // tm,),
        in_specs=[_rows(tm, D), _resident(w_down.shape), _rows(tm, 2 * F)] + [pl.BlockSpec(memory_space=pl.ANY)] * len(after),
        out_specs=_rows(tm, 2 * F), out_shape=jax.ShapeDtypeStruct((T, 2 * F), BF16),
        compiler_params=_params("parallel"))(df, w_down, partials, *after)


def _layernorm_stats(v):
    mu = jnp.mean(v, axis=-1, keepdims=True)
    cen = v - mu
    rstd = lax.rsqrt(jnp.mean(cen * cen, axis=-1, keepdims=True) + LN_EPS)
    return cen * rstd, rstd


GMLP_ROWS = 2 * CHUNK


def gmlp_gate_out(a, ln_g, ln_b, w_mask, b_s_t, w_out, h, g_post, g_pre, name):
    T, W2 = a.shape
    W = W2 // 2
    G = w_mask.shape[0]
    C = W // G
    D = w_out.shape[1]
    tm = min(T, GMLP_ROWS)

    def body(a_ref, lg_ref, lb_ref, w_ref, bs_ref, wo_ref, h_ref, gp_ref, g_ref,
             gated_ref, y_ref, ho_ref, hn_ref, acc_ref):
        for c in range(tm // CHUNK):
            rows = pl.ds(c * CHUNK, CHUNK)
            xhat, _ = _layernorm_stats(_gelu(a_ref[rows, W:].astype(F32)))
            vln = (xhat * lg_ref[...] + lb_ref[...]).astype(BF16)
            for g in range(G):
                cols = pl.ds(g * C, C)
                sv = jnp.dot(w_ref[g], vln[:, g * C:(g + 1) * C], preferred_element_type=F32) + bs_ref[:, g:g + 1]
                gated_ref[rows, cols] = (_gelu(a_ref[rows, cols].astype(F32)) * sv).astype(BF16)
        _rows_times_weight(gated_ref, wo_ref, acc_ref)
        y = acc_ref[...]
        y_ref[...] = y.astype(BF16)
        hv = h_ref[...] + _rms(y, gp_ref[...])
        ho_ref[...] = hv
        hn_ref[...] = _rms(hv, g_ref[...]).astype(BF16)

    row, vec = _rows(tm, D), _full((1, D))
    return pl.pallas_call(
        body, name=name, grid=(T // tm,),
        in_specs=[_rows(tm, W2), _full((1, W)), _full((1, W)), _full(w_mask.shape), _full(b_s_t.shape),
                  _resident(w_out.shape), row, vec, vec],
        out_specs=[_rows(tm, W), row, row, row],
        out_shape=[jax.ShapeDtypeStruct((T, W), BF16), jax.ShapeDtypeStruct((T, D), BF16),
                   jax.ShapeDtypeStruct((T, D), F32), jax.ShapeDtypeStruct((T, D), BF16)],
        scratch_shapes=[pltpu.VMEM((tm, D), F32)],
        compiler_params=_params("parallel"))(a, ln_g, ln_b, w_mask, b_s_t, w_out, h, g_post, g_pre)


def gmlp_gate_bwd_dx(a, dmix, w_out, ln_g, ln_b, w_mask, w_mask_t, b_s_t, group_onehot, w_in_t,
                     dh_in, x, g_pre, post, name, after):
    T, W2 = a.shape
    W = W2 // 2
    G = w_mask.shape[0]
    C = W // G
    D = w_out.shape[1]
    tm = min(T, GMLP_ROWS)
    tn = _tile(W, 512)
    row, vec = _rows(tm, D), _full((1, D))
    ins = [a, dmix, w_out, ln_g, ln_b, w_mask, w_mask_t, b_s_t, group_onehot, w_in_t, dh_in, x, g_pre]
    in_specs = [_rows(tm, W2), row, _resident(w_out.shape), _full((1, W)), _full((1, W)), _full(w_mask.shape),
                _full(w_mask_t.shape), _full(b_s_t.shape), _full(group_onehot.shape), _resident(w_in_t.shape), row, row, vec]
    names = ["da", "dw_s", "db_s_t", "dln_g", "dln_b", "db_in", "dh", "dg_pre"]
    outs = [jax.ShapeDtypeStruct((T, W2), BF16), jax.ShapeDtypeStruct((G, CHUNK, CHUNK), F32),
            jax.ShapeDtypeStruct((CHUNK, LANES), F32), jax.ShapeDtypeStruct((1, W), F32),
            jax.ShapeDtypeStruct((1, W), F32), jax.ShapeDtypeStruct((1, W2), F32),
            jax.ShapeDtypeStruct((T, D), F32), jax.ShapeDtypeStruct((1, D), F32)]
    out_specs = [_rows(tm, W2), _full((G, CHUNK, CHUNK)), _full((CHUNK, LANES)), _full((1, W)), _full((1, W)),
                 _full((1, W2)), row, vec]
    if post is not None:
        ins += list(post)
        in_specs += [row, vec]
        names += ["dy", "dg_post"]
        outs += [jax.ShapeDtypeStruct((T, D), BF16), jax.ShapeDtypeStruct((1, D), F32)]
        out_specs += [row, vec]
    n_in = len(ins) + 1

    def body(*refs):
        (a_ref, dm_ref, wo_ref, lg_ref, lb_ref, w_ref, wt_ref, bs_ref, e_ref, wi_ref, dh_in_ref, x_ref, gpre_ref) = refs[:13]
        out_refs = refs[n_in:-3]
        da_ref, dws_ref, dbs_ref, dlg_ref, dlb_ref, dbin_ref, dh_ref, dgpre_ref = out_refs[:8]
        dgated_ref, dvln_ref, acc_ref = refs[-3:]

        @pl.when(pl.program_id(0) == 0)
        def _():
            for r in out_refs[1:6] + out_refs[7::2]:
                r[...] = jnp.zeros_like(r)
        dm = dm_ref[...]
        for j in range(W // tn):
            cols = pl.ds(j * tn, tn)
            dgated_ref[:, cols] = lax.dot_general(dm, wo_ref[cols, :], _NT, preferred_element_type=F32)
        causal = (lax.broadcasted_iota(jnp.int32, (CHUNK, CHUNK), 1)
                  <= lax.broadcasted_iota(jnp.int32, (CHUNK, CHUNK), 0))
        for c in range(tm // CHUNK):
            rows = pl.ds(c * CHUNK, CHUNK)
            gelu_v, gelu_grad_v = _gelu_and_grad(a_ref[rows, W:].astype(F32))
            xhat, rstd = _layernorm_stats(gelu_v)
            vln = (xhat * lg_ref[...] + lb_ref[...]).astype(BF16)
            for g in range(G):
                cols = pl.ds(g * C, C)
                vg = vln[:, g * C:(g + 1) * C]
                gelu_u, gelu_grad_u = _gelu_and_grad(a_ref[rows, cols].astype(F32))
                dgated = dgated_ref[rows, cols]
                dsv = (dgated * gelu_u).astype(BF16)
                dbs_ref[...] += jnp.dot(dsv, e_ref[cols, :], preferred_element_type=F32)
                sv = jnp.dot(w_ref[g], vg, preferred_element_type=F32) + bs_ref[:, g:g + 1]
                dau = dgated * sv * gelu_grad_u
                da_ref[rows, cols] = dau.astype(BF16)
                dbin_ref[:, cols] += jnp.sum(dau, axis=0, keepdims=True)
                dws_ref[g] += jnp.where(causal, lax.dot_general(dsv, vg, _NT, preferred_element_type=F32), 0.0)
                dvln_ref[:, cols] = jnp.dot(wt_ref[g], dsv, preferred_element_type=F32)
            dvln = dvln_ref[...]
            dlg_ref[...] += jnp.sum(dvln * xhat, axis=0, keepdims=True)
            dlb_ref[...] += jnp.sum(dvln, axis=0, keepdims=True)
            dxhat = dvln * lg_ref[...]
            dv = rstd * (dxhat - jnp.mean(dxhat, axis=-1, keepdims=True)
                         - xhat * jnp.mean(dxhat * xhat, axis=-1, keepdims=True))
            dav = dv * gelu_grad_v
            da_ref[rows, W:] = dav.astype(BF16)
            dbin_ref[:, W:] += jnp.sum(dav, axis=0, keepdims=True)
        _rows_times_weight(da_ref, wi_ref, acc_ref)
        dx, dg = _rms_bwd(x_ref[...], gpre_ref[...], acc_ref[...])
        dh = dh_in_ref[...] + dx
        dh_ref[...] = dh
        dgpre_ref[...] += dg
        if post is not None:
            y_ref, gp_ref = refs[13:15]
            dy, dgp = _rms_bwd(y_ref[...].astype(F32), gp_ref[...], dh)
            out_refs[8][...] = dy.astype(BF16)
            out_refs[9][...] += dgp

    res = pl.pallas_call(
        body, name=name, grid=(T // tm,), in_specs=in_specs + [pl.BlockSpec(memory_space=pl.ANY)],
        out_specs=out_specs, out_shape=outs,
        scratch_shapes=[pltpu.VMEM((tm, W), F32), pltpu.VMEM((CHUNK, W), F32), pltpu.VMEM((tm, D), F32)],
        compiler_params=_params("arbitrary"))(*ins, after)
    return dict(zip(names, res))


def rope_tables(pos, inv_freq_row, name):
    T = pos.shape[0]
    tm = _tile(T, 512)

    def body(p_ref, f_ref, c_ref, s1_ref, s2_ref):
        ang = p_ref[...].astype(F32) * f_ref[...]
        lane = lax.broadcasted_iota(jnp.int32, (tm, LANES), 1) % HEAD_DIM
        sin = jnp.sin(ang)
        c_ref[...] = jnp.cos(ang)
        s1_ref[...] = jnp.where(lane < _HALF, -sin, 0.0)
        s2_ref[...] = jnp.where((lane >= _HALF) & (lane < ROPE_DIM), sin, 0.0)

    tab = _rows(tm, LANES)
    return pl.pallas_call(
        body, name=name, grid=(T // tm,), in_specs=[_rows(tm, 1), _full((1, LANES))], out_specs=[tab] * 3,
        out_shape=[jax.ShapeDtypeStruct((T, LANES), F32)] * 3, compiler_params=_params("parallel"))(pos, inv_freq_row)


_HALF = ROPE_DIM // 2


def _slabs(x):
    return [x[:, b * LANES:(b + 1) * LANES] for b in range(x.shape[1] // LANES)]


def _rotate(x, c, s1, s2):
    return [xs * c + pltpu.roll(xs, LANES - _HALF, 1) * s1 + pltpu.roll(xs, _HALF, 1) * s2 for xs in _slabs(x)]


def _rotate_transposed(dy, c, s1, s2):
    return [ds * c + pltpu.roll(ds * s1, _HALF, 1) + pltpu.roll(ds * s2, LANES - _HALF, 1) for ds in _slabs(dy)]


def rope_fwd(qkv, tabs, q_width, kv_width, name):
    T, QKV = qkv.shape
    tm = _tile(T, 512)
    QK = q_width + kv_width
    scale = HEAD_DIM ** -0.5

    def body(x_ref, c_ref, s1_ref, s2_ref, o_ref):
        x = x_ref[:, :QK].astype(F32)
        slabs = _rotate(x, c_ref[...], s1_ref[...], s2_ref[...])
        for b, y in enumerate(slabs):
            if b * LANES < q_width:
                y = y * scale
            o_ref[:, b * LANES:(b + 1) * LANES] = y.astype(BF16)

    tab = _rows(tm, LANES)
    return pl.pallas_call(
        body, name=name, grid=(T // tm,), in_specs=[_rows(tm, QKV), tab, tab, tab], out_specs=_rows(tm, QK),
        out_shape=jax.ShapeDtypeStruct((T, QK), BF16), compiler_params=_params("parallel"))(qkv, *tabs)


def rope_bwd(dq, dk, dv, tabs, name):
    T, Q = dq.shape
    KV = dk.shape[1]
    tm = _tile(T, 512)
    scale = HEAD_DIM ** -0.5

    def body(dq_ref, dk_ref, dv_ref, c_ref, s1_ref, s2_ref, o_ref, b_ref):
        @pl.when(pl.program_id(0) == 0)
        def _():
            b_ref[...] = jnp.zeros_like(b_ref)
        tabs_v = (c_ref[...], s1_ref[...], s2_ref[...])
        pieces = [s * scale for s in _rotate_transposed(dq_ref[...], *tabs_v)]
        pieces += _rotate_transposed(dk_ref[...], *tabs_v)
        pieces += _slabs(dv_ref[...])
        for b, y in enumerate(pieces):
            cols = pl.ds(b * LANES, LANES)
            o_ref[:, cols] = y.astype(BF16)
            b_ref[:, cols] += jnp.sum(y, axis=0, keepdims=True)

    tab = _rows(tm, LANES)
    return pl.pallas_call(
        body, name=name, grid=(T // tm,), in_specs=[_rows(tm, Q), _rows(tm, KV), _rows(tm, KV), tab, tab, tab],
        out_specs=[_rows(tm, Q + 2 * KV), _full((1, Q + 2 * KV))],
        out_shape=[jax.ShapeDtypeStruct((T, Q + 2 * KV), BF16), jax.ShapeDtypeStruct((1, Q + 2 * KV), F32)],
        compiler_params=_params("arbitrary"))(dq, dk, dv, *tabs)


def _band_mask(n, heads=1):
    qi = lax.broadcasted_iota(jnp.int32, (heads * CHUNK, 2 * CHUNK), 0) % CHUNK
    sj = lax.broadcasted_iota(jnp.int32, (heads * CHUNK, 2 * CHUNK), 1)
    return (sj > qi) & (sj <= qi + CHUNK) & ((n > 0) | (sj >= CHUNK))


def _kv_head(kc_ref, kp_ref, vc_ref, vp_ref, kh):
    lanes = slice(kh * HEAD_DIM, (kh + 1) * HEAD_DIM)
    return (jnp.concatenate([kp_ref[:, lanes], kc_ref[:, lanes]], axis=0),
            jnp.concatenate([vp_ref[:, lanes], vc_ref[:, lanes]], axis=0))


def _head_probs(q, kk, valid, sink):
    s = jnp.where(valid, lax.dot_general(q, kk, _NT, preferred_element_type=F32), NEG_INF)
    m = jnp.maximum(jnp.max(s, axis=-1, keepdims=True), sink)
    p = jnp.exp(s - m)
    e_sink = jnp.exp(sink - m)
    return p, 1.0 / (jnp.sum(p, axis=-1, keepdims=True) + e_sink), e_sink


def _attn_specs(q_width, kv_width, order):
    qb = q_width // kv_width
    prev = lambda i: jnp.maximum(order(i) - 1, 0)
    return [pl.BlockSpec((CHUNK, q_width), lambda i: (order(i), 0)),
            pl.BlockSpec((CHUNK, kv_width), lambda i: (order(i), qb)),
            pl.BlockSpec((CHUNK, kv_width), lambda i: (prev(i), qb)),
            pl.BlockSpec((CHUNK, kv_width), lambda i: (order(i), qb + 1)),
            pl.BlockSpec((CHUNK, kv_width), lambda i: (prev(i), qb + 1))]


def attn_fwd(qk, qkv, sinks, q_width, kv_width, name):
    T = qk.shape[0]
    group = q_width // kv_width

    def body(q_ref, kc_ref, kp_ref, vc_ref, vp_ref, sink_ref, o_ref):
        valid = _band_mask(pl.program_id(0))
        for kh in range(kv_width // HEAD_DIM):
            kk, vv = _kv_head(kc_ref, kp_ref, vc_ref, vp_ref, kh)
            for g in range(group):
                h = kh * group + g
                lanes = slice(h * HEAD_DIM, (h + 1) * HEAD_DIM)
                p, inv, _ = _head_probs(q_ref[:, lanes], kk, valid, sink_ref[h])
                o_ref[:, lanes] = (jnp.dot(p.astype(BF16), vv, preferred_element_type=F32) * inv).astype(BF16)

    specs = _attn_specs(q_width, kv_width, lambda i: i)
    return pl.pallas_call(
        body, name=name, grid=(T // CHUNK,),
        in_specs=specs + [pl.BlockSpec(memory_space=pltpu.SMEM)],
        out_specs=_rows(CHUNK, q_width), out_shape=jax.ShapeDtypeStruct((T, q_width), BF16),
        compiler_params=_params("parallel"))(qk, qk, qk, qkv, qkv, sinks)


def attn_bwd(qk, qkv, do, sinks, q_width, kv_width, name):
    T = qk.shape[0]
    NB = T // CHUNK
    group = q_width // kv_width

    def body(q_ref, kc_ref, kp_ref, vc_ref, vp_ref, do_ref, sink_ref, dq_ref, dk_ref, dv_ref, ds_ref, ck_ref, cv_ref):
        i = pl.program_id(0)
        n = NB - 1 - i

        @pl.when(i == 0)
        def _():
            ck_ref[...] = jnp.zeros_like(ck_ref)
            cv_ref[...] = jnp.zeros_like(cv_ref)
            ds_ref[...] = jnp.zeros_like(ds_ref)
        lane = lax.broadcasted_iota(jnp.int32, (1, LANES), 1)
        dsink_row = jnp.zeros((1, LANES), F32)
        valid = _band_mask(n, group)
        head = lax.broadcasted_iota(jnp.int32, (group * CHUNK, 1), 0) // CHUNK
        for kh in range(kv_width // HEAD_DIM):
            kk, vv = _kv_head(kc_ref, kp_ref, vc_ref, vp_ref, kh)
            heads = [slice((kh * group + g) * HEAD_DIM, (kh * group + g + 1) * HEAD_DIM) for g in range(group)]
            q = jnp.concatenate([q_ref[:, hs] for hs in heads], axis=0)
            do = jnp.concatenate([do_ref[:, hs] for hs in heads], axis=0)
            sink = jnp.zeros((group * CHUNK, 1), F32)
            for g in range(group):
                sink = jnp.where(head == g, sink_ref[kh * group + g], sink)
            p, inv, e_sink = _head_probs(q, kk, valid, sink)
            p = p * inv
            dp = lax.dot_general(do, vv, _NT, preferred_element_type=F32)
            delta = jnp.sum(p * dp, axis=-1, keepdims=True)
            ds = (p * (dp - delta)).astype(BF16)
            dsink = -e_sink * inv * delta
            dq = jnp.dot(ds, kk, preferred_element_type=F32)
            for g, hs in enumerate(heads):
                rows = slice(g * CHUNK, (g + 1) * CHUNK)
                dsink_row = dsink_row + jnp.where(lane == kh * group + g, jnp.sum(dsink[rows]), 0.0)
                dq_ref[:, hs] = dq[rows]
            dkk = lax.dot_general(ds, q, _TN, preferred_element_type=F32)
            dvv = lax.dot_general(p.astype(BF16), do, _TN, preferred_element_type=F32)
            lanes = slice(kh * HEAD_DIM, (kh + 1) * HEAD_DIM)
            dk_ref[:, lanes] = dkk[CHUNK:] + ck_ref[:, lanes]
            dv_ref[:, lanes] = dvv[CHUNK:] + cv_ref[:, lanes]
            ck_ref[:, lanes] = dkk[:CHUNK]
            cv_ref[:, lanes] = dvv[:CHUNK]
        ds_ref[0:1, :] += dsink_row

    order = lambda i: NB - 1 - i
    specs = _attn_specs(q_width, kv_width, order)
    kv_out = pl.BlockSpec((CHUNK, kv_width), lambda i: (order(i), 0))
    q_rows = pl.BlockSpec((CHUNK, q_width), lambda i: (order(i), 0))
    return pl.pallas_call(
        body, name=name, grid=(NB,),
        in_specs=specs + [q_rows, pl.BlockSpec(memory_space=pltpu.SMEM)],
        out_specs=[q_rows, kv_out, kv_out, _full((8, LANES))],
        out_shape=[jax.ShapeDtypeStruct((T, q_width), F32), jax.ShapeDtypeStruct((T, kv_width), F32),
                   jax.ShapeDtypeStruct((T, kv_width), F32), jax.ShapeDtypeStruct((8, LANES), F32)],
        scratch_shapes=[pltpu.VMEM((CHUNK, kv_width), F32), pltpu.VMEM((CHUNK, kv_width), F32)],
        compiler_params=_params("arbitrary"))(qk, qk, qk, qkv, qkv, do, sinks)


def _blocked(w):
    return w.reshape(N_DEV, w.shape[0] // N_DEV, w.shape[1])


def kernel(x, positions, pre_mix_g, post_mix_g, pre_ffn_g, post_ffn_g, a_w_in, a_b_in, a_ln_g, a_ln_b, a_w_s, a_b_s, a_w_out, b_w_qkv, b_b_qkv, b_sinks, b_w_o, ffn_w_gu, ffn_w_down, loss_target, m_pre_mix_g, m_post_mix_g, m_pre_ffn_g, m_post_ffn_g, m_a_w_in, m_a_b_in, m_a_ln_g, m_a_ln_b, m_a_w_s, m_a_b_s, m_a_w_out, m_b_w_qkv, m_b_b_qkv, m_b_sinks, m_b_w_o, m_ffn_w_gu, m_ffn_w_down, v_pre_mix_g, v_post_mix_g, v_pre_ffn_g, v_post_ffn_g, v_a_w_in, v_a_b_in, v_a_ln_g, v_a_ln_b, v_a_w_s, v_a_b_s, v_a_w_out, v_b_w_qkv, v_b_b_qkv, v_b_sinks, v_b_w_o, v_ffn_w_gu, v_ffn_w_down):
    weights = dict(pre_mix_g=pre_mix_g, post_mix_g=post_mix_g, pre_ffn_g=pre_ffn_g, post_ffn_g=post_ffn_g,
                   a_w_in=a_w_in, a_b_in=a_b_in, a_ln_g=a_ln_g, a_ln_b=a_ln_b, a_w_s=a_w_s, a_b_s=a_b_s,
                   a_w_out=a_w_out, b_w_qkv=b_w_qkv, b_b_qkv=b_b_qkv, b_sinks=b_sinks, b_w_o=b_w_o,
                   ffn_w_gu=ffn_w_gu, ffn_w_down=ffn_w_down)
    mom_m = dict(pre_mix_g=m_pre_mix_g, post_mix_g=m_post_mix_g, pre_ffn_g=m_pre_ffn_g, post_ffn_g=m_post_ffn_g,
                 a_w_in=m_a_w_in, a_b_in=m_a_b_in, a_ln_g=m_a_ln_g, a_ln_b=m_a_ln_b, a_w_s=m_a_w_s, a_b_s=m_a_b_s,
                 a_w_out=m_a_w_out, b_w_qkv=m_b_w_qkv, b_b_qkv=m_b_b_qkv, b_sinks=m_b_sinks, b_w_o=m_b_w_o,
                 ffn_w_gu=m_ffn_w_gu, ffn_w_down=m_ffn_w_down)
    mom_v = dict(pre_mix_g=v_pre_mix_g, post_mix_g=v_post_mix_g, pre_ffn_g=v_pre_ffn_g, post_ffn_g=v_post_ffn_g,
                 a_w_in=v_a_w_in, a_b_in=v_a_b_in, a_ln_g=v_a_ln_g, a_ln_b=v_a_ln_b, a_w_s=v_a_w_s, a_b_s=v_a_b_s,
                 a_w_out=v_a_w_out, b_w_qkv=v_b_w_qkv, b_b_qkv=v_b_b_qkv, b_sinks=v_b_sinks, b_w_o=v_b_w_o,
                 ffn_w_gu=v_ffn_w_gu, ffn_w_down=v_ffn_w_down)
    names = list(weights)
    big = ("a_w_in", "b_w_qkv", "ffn_w_gu", "a_w_out", "b_w_o", "ffn_w_down")

    T, D = x.shape[1], x.shape[2]
    depth = pre_mix_g.shape[0]
    n_heads = b_sinks.shape[1]
    q_width = n_heads * HEAD_DIM
    kv_width = N_KV_HEADS * HEAD_DIM
    G = a_w_s.shape[1]
    W = a_ln_g.shape[1]
    device = _index(_position())

    dev = device.reshape(1).astype(jnp.int32)

    def layer_keys(i):
        mixer = ("a_w_in", "a_w_out") if i % 2 == 0 else ("b_w_qkv", "b_w_o")
        return [(k, i // 2) for k in mixer] + [("ffn_w_gu", i), ("ffn_w_down", i)]

    as_view = ("b_w_qkv", "ffn_w_gu")
    in_kernel = ("a_w_in",)
    view = lambda k, t: jnp.swapaxes(t, 1, 2) if k in as_view else t
    w_view = {k: view(k, weights[k]) for k in big}
    m_view = {k: view(k, mom_m[k]) for k in big}
    v_view = {k: view(k, mom_v[k]) for k in big}

    groups_keys = [[key] for key in layer_keys(0)] + [layer_keys(i) for i in range(1, depth)]
    group_of = {}
    for g, keys in enumerate(groups_keys):
        layer = 0 if g < len(layer_keys(0)) else g - len(layer_keys(0)) + 1
        for pos, (k, _) in enumerate(keys):
            group_of[(layer, k)] = (g, pos)
    bias_land = lax.dynamic_update_slice(
        jnp.zeros((N_DEV,) + b_b_qkv.shape, F32), b_b_qkv[None], (device, 0, 0))
    h, hn = x[0], first_norm(x[0], pre_mix_g[0][None], "norm_first")
    first = len(layer_keys(0))
    in_gather, token = [], ()
    for name, lo, hi in (("gather_start_first", 0, first), ("gather_start_rest", first, len(groups_keys))):
        groups = [list(prep_weights([(w_view[k], l, k in in_kernel) for k, l in keys], dev, f"prep_{name}_{g}",
                                    after=token)) for g, keys in enumerate(groups_keys[lo:hi])]
        if lo:
            groups[0].append(bias_land)
        gather_sems, flat_lands, token = relay_start(groups, name)
        token = (token, hn)
        for grp, sems in zip(groups, gather_sems):
            in_gather.append(([flat_lands.pop(0) for _ in grp], sems))
    all_started = token[0]
    passed, gathered = {}, {}

    def pass_on(g, after):
        if g not in passed and g < len(in_gather):
            lands, sems = in_gather[g]
            passed[g], lands = relay_pass(lands, sems, after, f"gather_pass_{g}")
            in_gather[g] = (lands, sems)

    pass_on(0, (all_started,))

    def weight(i, k, *after):
        g, pos = group_of[(i, k)]
        if g not in gathered:
            pass_on(g + 1, after)
            lands, sems = in_gather[g]
            gathered[g] = relay_wait(lands, sems, passed[g], after, f"gather_wait_{g}")
        w = gathered[g][pos]
        return w.reshape(N_DEV * w.shape[1], w.shape[2])

    causal = jnp.tril(jnp.ones((CHUNK, CHUNK), dtype=bool))
    w_mask = jnp.where(causal[None, None], a_w_s, 0.0).astype(BF16)
    w_mask_t = jnp.swapaxes(w_mask, 2, 3)
    b_s_t = jnp.swapaxes(a_b_s, 1, 2)
    group_onehot = (jnp.arange(W)[:, None] // (W // G) == jnp.arange(LANES)[None, :]).astype(BF16)
    lane = jnp.arange(LANES) % HEAD_DIM
    inv_freq = ROPE_THETA ** (-jnp.arange(0, ROPE_DIM, 2, dtype=F32) / ROPE_DIM)
    inv_freq_row = jnp.where(lane < ROPE_DIM, inv_freq[lane % (ROPE_DIM // 2)], 0.0)[None, :].astype(F32)
    tabs = rope_tables(positions.reshape(T, 1), inv_freq_row, "rope_tables")

    saved = []
    for i in range(depth):
        j = i // 2
        s = dict(h=h, hn=hn)
        post_pre = (post_mix_g[i][None], pre_ffn_g[i][None])
        if i % 2 == 0:
            s["a"] = rowmm(hn, weight(i, "a_w_in", hn, all_started), True, a_b_in[j][None], BF16, "gmlp_in")
            s["gated"], s["mix"], s["h2"], s["fn"] = gmlp_gate_out(
                s["a"], a_ln_g[j][None], a_ln_b[j][None], w_mask[j], b_s_t[j], weight(i, "a_w_out", s["a"]),
                h, *post_pre, "gmlp_gate_out")
        else:
            w_qkv = weight(i, "b_w_qkv", hn)
            b_qkv_full = jnp.swapaxes(gathered[group_of[(1, "b_w_qkv")][0]][-1], 0, 1).reshape(b_b_qkv.shape[0], 1, -1)
            s["qkv"] = rowmm(hn, w_qkv, True, b_qkv_full[j], BF16, "attn_qkv")
            s["qk"] = rope_fwd(s["qkv"], tabs, q_width, kv_width, "attn_rope")
            s["o"] = attn_fwd(s["qk"], s["qkv"], b_sinks[j], q_width, kv_width, "attn_core")
            s["mix"], s["h2"], s["fn"] = mm_add_norm(s["o"], weight(i, "b_w_o", s["o"]), h, *post_pre, "attn_out")
        s["act"], s["act_partials"] = ffn_up(s["fn"], weight(i, "ffn_w_gu", s["fn"]), "ffn_up")
        w_down = weight(i, "ffn_w_down", s["act"])
        saved.append(s)
        if i + 1 < depth:
            s["f"], h, hn = mm_add_norm(s["act"], w_down, s["h2"], post_ffn_g[i][None], pre_mix_g[i + 1][None], "ffn_down")
    small = {k: [None] * weights[k].shape[0] for k in names if k not in big}
    dh, df, loss_sum, small["post_ffn_g"][depth - 1] = mm_add_norm_loss(
        saved[-1]["act"], w_down, saved[-1]["h2"], post_ffn_g[depth - 1][None], loss_target[0], "ffn_down_loss")

    grads = {k: [None] * weights[k].shape[0] for k in big}
    in_flight = []

    def send_grads(keys, tag):
        sems, bufs, token = exchange_start([_blocked(grads[k][l]) for k, l in keys], "exchange_start_" + tag)
        in_flight.append((keys, sems, bufs, tag))
        return token

    replicated = [k for k in names if k not in big and k not in ("b_b_qkv", "a_w_s")]
    main = sum(weights[k].size for k in replicated)
    main_rows = -(-main // (LANES * LANES)) * LANES
    bias_size = b_b_qkv.shape[0] * N_DEV * b_b_qkv.shape[1]
    bias_rows = -(-bias_size // (8 * LANES)) * 8
    flat = lambda vals: [v.reshape(-1) for v in vals]

    def start_small_gather():
        packed = jnp.concatenate(
            flat(p for k in replicated for p in small[k])
            + [loss_sum[0, :1], jnp.zeros((main_rows * LANES - main - 1,), F32)] + flat(small["b_b_qkv"]) + [jnp.zeros((bias_rows * LANES - bias_size,), F32)]
        ).reshape(1, main_rows + bias_rows, LANES)
        spatial = jnp.stack(small["a_w_s"]).reshape(1, -1, LANES)
        return gather_start([prep_weights([(packed, 0, False), (spatial, 0, False)], dev, "place_small_grads", F32)],
                            "gather_start_small")

    sent = ()
    for i in reversed(range(depth)):
        j = i // 2
        s = saved[i]
        dgu = ffn_dact(df, weight(i, "ffn_w_down"), s["act_partials"], "ffn_dact", after=sent)
        alone = i == 0
        grads["ffn_w_down"][i] = grad_mm(s["act"], df, "ffn_down_grad")
        early = (send_grads(layer_keys(i)[3:], f"ffn_down_{i}"),) if alone else ()
        grads["ffn_w_gu"][i] = grad_mm(dgu, s["fn"], "ffn_up_grad", after=early)
        sent = send_grads(layer_keys(i)[2:3] if alone else layer_keys(i)[2:], f"ffn_{i}")
        r = mm_norm_bwd(dgu, weight(i, "ffn_w_gu"), dh, s["h2"], pre_ffn_g[i][None],
                        (s["mix"], post_mix_g[i][None]), "ffn_dx", sent)
        dh, dmix = r["dh"], r["dy"]
        small["pre_ffn_g"][i], small["post_mix_g"][i] = r["dg_pre"], r["dg_post"]
        post = (saved[i - 1]["f"], post_ffn_g[i - 1][None]) if i > 0 else None
        if i % 2 == 0:
            grads["a_w_out"][j] = grad_mm(s["gated"], dmix, "gmlp_out_grad")
            if alone:
                sent = send_grads(layer_keys(i)[1:2], f"mixer_out_{i}")
            r = gmlp_gate_bwd_dx(
                s["a"], dmix, weight(i, "a_w_out"), a_ln_g[j][None], a_ln_b[j][None], w_mask[j], w_mask_t[j], b_s_t[j],
                group_onehot, weight(i, "a_w_in"), dh, s["h"], pre_mix_g[i][None], post, "gmlp_gate_bwd_dx", sent)
            small["a_w_s"][j], small["a_b_s"][j] = r["dw_s"], r["db_s_t"][:, :G].T
            small["a_ln_g"][j], small["a_ln_b"][j], small["a_b_in"][j] = r["dln_g"][0], r["dln_b"][0], r["db_in"][0]
            small["pre_mix_g"][i] = r["dg_pre"]
            first_layer = ()
            if i == 0:
                small_gather = start_small_gather()
                first_layer = (small_gather[2],)
            grads["a_w_in"][j] = grad_mm(r["da"], s["hn"], "gmlp_in_grad", after=first_layer)
            sent = (send_grads(layer_keys(i)[:1] if alone else layer_keys(i)[:2], f"mixer_{i}"),)
        else:
            grads["b_w_o"][j] = grad_mm(s["o"], dmix, "attn_out_grad")
            do = rowmm(dmix, weight(i, "b_w_o"), True, None, BF16, "attn_do")
            dq, dk, dv, dsink = attn_bwd(s["qk"], s["qkv"], do, b_sinks[j], q_width, kv_width, "attn_core_bwd")
            dqkv, dbias = rope_bwd(dq, dk, dv, tabs, "attn_rope_bwd")
            small["b_sinks"][j], small["b_b_qkv"][j] = dsink[0, :n_heads], dbias[0]
            grads["b_w_qkv"][j] = grad_mm(dqkv, s["hn"], "attn_qkv_grad")
            sent = send_grads(layer_keys(i)[:2], f"mixer_{i}")
            r = mm_norm_bwd(dqkv, weight(i, "b_w_qkv"), dh, s["h"], pre_mix_g[i][None], post, "attn_dx", sent)
            sent = ()
        dh, small["pre_mix_g"][i] = r["dh"], r["dg_pre"]
        if i > 0:
            df, small["post_ffn_g"][i - 1] = r["dy"], r["dg_post"]
    grad_x = dh[None]

    small_sems, small_lands, small_sent = small_gather
    pack_main = lambda src: jnp.concatenate(
        flat(src[k] for k in replicated) + [jnp.ones((main_rows * LANES - main,), F32)]).reshape(main_rows, LANES)
    packed_w, packed_m, packed_v = pack_main(weights), pack_main(mom_m), pack_main(mom_v)

    stacked = {k: [lax.empty(w_view[k].shape, F32) for _ in range(4)] for k in big}
    after = (small_sent, dh) + sent
    for keys, sems, bufs, tag in in_flight:
        partials, landed = exchange_wait(bufs, sems, after, "exchange_wait_" + tag)
        for (k, l), part, land in zip(keys, partials, landed):
            stacked[k] = sum_adamw(part, land, w_view[k], m_view[k], v_view[k], stacked[k], l,
                                   k in in_kernel, dev, "adamw_" + k)
        after = tuple(stacked[k][0] for k, _ in keys)
    out_g, out_d, out_m, out_v = {}, {}, {}, {}
    for k in big:
        out_g[k], out_d[k], out_m[k], out_v[k] = [view(k, t) for t in stacked[k]]

    gathered_small, gathered_spatial = gather_wait(small_lands, small_sems[0], after + (packed_w, packed_m, packed_v),
                                                   "gather_wait_small")
    rows_of = lambda t: t.reshape(-1, LANES)
    spatial = sum_adamw_small(gathered_spatial, rows_of(a_w_s), rows_of(m_a_w_s), rows_of(v_a_w_s), "adamw_a_w_s")
    out_g["a_w_s"], out_d["a_w_s"], out_m["a_w_s"], out_v["a_w_s"] = [t.reshape(a_w_s.shape) for t in spatial]
    results = sum_adamw_small(gathered_small, packed_w, packed_m, packed_v, "adamw_small")
    loss = results[0].reshape(-1)[main] * (0.5 / D)
    off = 0
    for k in replicated:
        size, shape = weights[k].size, weights[k].shape
        out_g[k], out_d[k], out_m[k], out_v[k] = [t.reshape(-1)[off:off + size].reshape(shape) for t in results]
        off += size
    n_local = b_b_qkv.shape[1]
    bias_sum = sum_parts(gathered_small[:, main_rows:], "sum_bias_grads").reshape(-1)[:bias_size]
    out_g["b_b_qkv"] = lax.dynamic_slice_in_dim(
        bias_sum.reshape(b_b_qkv.shape[0], -1), device * n_local, n_local, axis=1)
    pad = lambda t: jnp.concatenate([t.reshape(-1), jnp.ones((8 * LANES - t.size,), F32)]).reshape(8, LANES)
    bias_update = adamw(pad(b_b_qkv), pad(out_g["b_b_qkv"]), pad(m_b_b_qkv), pad(v_b_b_qkv), "adamw_bias")
    out_d["b_b_qkv"], out_m["b_b_qkv"], out_v["b_b_qkv"] = [
        t.reshape(-1)[:b_b_qkv.size].reshape(b_b_qkv.shape) for t in bias_update]

    return (loss, grad_x, *[out_g[k] for k in names], *[out_d[k] for k in names],
            *[out_m[k] for k in names], *[out_v[k] for k in names])
```

```python
import math

import jax
import jax.numpy as jnp
from jax import lax
from jax.experimental import pallas as pl
from jax.experimental.pallas import tpu as pltpu

F32, BF16 = jnp.float32, jnp.bfloat16
MESH = pl.DeviceIdType.MESH
N_DEV = 8

CHUNK = 128
HEAD_DIM = 64
N_KV_HEADS = 4
ROPE_DIM = HEAD_DIM // 4
ROPE_THETA = 500000.0
RMS_EPS = 1e-6
LN_EPS = 1e-5
NEG_INF = -1e30

ADAM_LR = 0.001
ADAM_B1 = 0.9
ADAM_B2 = 0.999
ADAM_EPS = 1e-08
ADAM_WD = 0.01
ADAM_STEP = 10

V7X_VMEM_BYTES = 64 * 2 ** 20
VMEM_LIMIT = V7X_VMEM_BYTES - 8 * 2 ** 20
LANES = 128


def _params(*sem):
    return pltpu.CompilerParams(dimension_semantics=sem or None, vmem_limit_bytes=VMEM_LIMIT)


def _tile(n, pref):
    if n <= pref:
        return n
    t = pref - pref % LANES
    while t >= LANES:
        if n % t == 0:
            return t
        t -= LANES
    return n


def _full(shape):
    return pl.BlockSpec(shape, lambda *_: (0,) * len(shape))


def _resident(shape):
    return pl.BlockSpec(shape, lambda *_: (0,) * len(shape), pipeline_mode=pl.Buffered(1))


def _rows(tm, width):
    return pl.BlockSpec((tm, width), lambda i: (i, 0))


def _rms(x, g):
    r = lax.rsqrt(jnp.mean(x * x, axis=-1, keepdims=True) + RMS_EPS)
    return x * r * g


def _rms_bwd(x, g, dy):
    r = lax.rsqrt(jnp.mean(x * x, axis=-1, keepdims=True) + RMS_EPS)
    xhat = x * r
    dg = jnp.sum(dy * xhat, axis=0, keepdims=True)
    dxhat = dy * g
    dx = r * (dxhat - xhat * jnp.mean(dxhat * xhat, axis=-1, keepdims=True))
    return dx, dg


_INV_SQRT2 = 1.0 / math.sqrt(2.0)
_INV_SQRT2PI = 1.0 / math.sqrt(2.0 * math.pi)


def _gelu(x):
    return 0.5 * x * (1.0 + lax.erf(x * _INV_SQRT2))


def _gelu_and_grad(x):
    cdf = 0.5 * (1.0 + lax.erf(x * _INV_SQRT2))
    return x * cdf, cdf + x * jnp.exp(-0.5 * x * x) * _INV_SQRT2PI


def _sigmoid(x):
    return 0.5 * jnp.tanh(0.5 * x) + 0.5


def _position():
    return lax.axis_index("x"), lax.axis_index("y"), lax.axis_index("c")


def _index(p):
    return 4 * p[0] + 2 * p[1] + p[2]


_HBM = pl.BlockSpec(memory_space=pltpu.HBM)
_SEM = pl.BlockSpec(memory_space=pltpu.SEMAPHORE)
_ORDERED_BY_DATA = pltpu.SideEffectType.DATAFLOW_SIDE_EFFECTING


def _in_hbm(v):
    return pltpu.with_memory_space_constraint(v, pltpu.HBM)


def _peer(k):
    x, y, c = _position()
    return (x ^ ((k >> 2) & 1), y ^ ((k >> 1) & 1), c ^ (k & 1))


def _split_copies(srcs, dsts, send, recv):
    return [pltpu.make_async_remote_copy(
        src_ref=src(k), dst_ref=dst(k), send_sem=send.at[7 * a + k - 1], recv_sem=recv.at[7 * a + k - 1],
        device_id=_peer(k), device_id_type=MESH)
        for a, (src, dst) in enumerate(zip(srcs, dsts)) for k in range(1, N_DEV)]


def _start_call(groups, sent, copies_of, name, per_array=N_DEV - 1):
    flat = [v for grp in groups for v in grp]
    n, ng = len(flat), len(groups)

    def body(*refs):
        bufs, sems = refs[:n], refs[n:n + 2 * ng]
        q = 0
        for g, grp in enumerate(groups):
            for cp in copies_of(bufs[q:q + len(grp)], sems[2 * g], sems[2 * g + 1]):
                cp.start()
            q += len(grp)
        refs[-1][...] = jnp.zeros_like(refs[-1])

    sem_shapes = []
    for count in sent:
        sem_shapes += [pltpu.SemaphoreType.DMA((per_array * count,)) for _ in range(2)]
    res = pl.pallas_call(
        body, name=name,
        out_shape=sem_shapes + [pltpu.HBM(v.shape, v.dtype) for v in flat] + [jax.ShapeDtypeStruct((8, LANES), F32)],
        in_specs=[_HBM] * n,
        out_specs=[_SEM] * (2 * ng) + [_HBM] * n + [pl.BlockSpec(memory_space=pltpu.VMEM)],
        input_output_aliases={i: 2 * ng + i for i in range(n)},
        compiler_params=pltpu.CompilerParams(has_side_effects=_ORDERED_BY_DATA),
    )(*[_in_hbm(v) for v in flat])
    sems = [(res[2 * g], res[2 * g + 1]) for g in range(ng)]
    return sems, list(res[2 * ng:-1]), res[-1]


def _wait_call(bufs, sems, copies_of, after, name):
    n = len(bufs)

    def body(*refs):
        for cp in copies_of(refs[:n], refs[n], refs[n + 1]):
            cp.wait_send()
            cp.wait_recv()

    return list(pl.pallas_call(
        body, name=name,
        out_shape=[pltpu.HBM(v.shape, v.dtype) for v in bufs],
        in_specs=[_HBM] * n + [_SEM, _SEM] + [pl.BlockSpec(memory_space=pl.ANY)] * len(after),
        out_specs=[_HBM] * n,
        input_output_aliases={i: i for i in range(n)},
        compiler_params=pltpu.CompilerParams(has_side_effects=_ORDERED_BY_DATA),
    )(*bufs, sems[0], sems[1], *after))


def _gather_copies(lands, send, recv):
    me = _index(_position())
    mine = [lambda k, ref=ref: ref.at[pl.ds(me, 1)] for ref in lands]
    return _split_copies(mine, mine, send, recv)


def _gather_arrivals(lands, send, recv):
    me = _index(_position())
    mine = [lambda k, ref=ref: ref.at[pl.ds(me, 1)] for ref in lands]
    theirs = [lambda k, ref=ref: ref.at[pl.ds(_index(_peer(k)), 1)] for ref in lands]
    return _split_copies(mine, theirs, send, recv)


def _exchange_copies(bufs, send, recv):
    half = len(bufs) // 2
    srcs = [lambda k, ref=ref: ref.at[pl.ds(_index(_peer(k)), 1)] for ref in bufs[:half]]
    dsts = [lambda k, ref=ref: ref.at[pl.ds(k - 1, 1)] for ref in bufs[half:]]
    return _split_copies(srcs, dsts, send, recv)


def gather_start(groups, name):
    return _start_call(groups, [len(grp) for grp in groups], _gather_copies, name)


def gather_wait(lands, sems, after, name):
    return _wait_call(lands, sems, _gather_arrivals, after, name)


_DIRECT = (1, 2, 4, 6)
_PASSED = (3, 5, 7)


def _remote(src, dst, send, recv, q, k):
    return pltpu.make_async_remote_copy(src_ref=src, dst_ref=dst, send_sem=send.at[q], recv_sem=recv.at[q],
                                        device_id=_peer(k), device_id_type=MESH)


def _slot_of(land, k):
    return land.at[pl.ds(_index(_peer(k)), 1)]


def relay_start(groups, name):
    def copies(lands, send, recv):
        return [_remote(_slot_of(land, 0), _slot_of(land, 0), send, recv, len(_DIRECT) * a + j, k)
                for a, land in enumerate(lands) for j, k in enumerate(_DIRECT)]
    return _start_call(groups, [len(grp) for grp in groups], copies, name, per_array=len(_DIRECT))


def relay_pass(lands, sems, after, name):
    n = len(lands)

    def body(*refs):
        bufs, recv_first = refs[:n], refs[n]
        send, recv = refs[n + 1 + len(after):n + 3 + len(after)]
        for a, land in enumerate(bufs):
            for j, k in enumerate(_PASSED):
                came = _slot_of(land, k ^ 1)
                _remote(came, came, recv_first, recv_first, len(_DIRECT) * a + _DIRECT.index(k ^ 1), k ^ 1).wait_recv()
                _remote(came, came, send, recv, len(_PASSED) * a + j, 1).start()

    res = pl.pallas_call(
        body, name=name,
        out_shape=[pltpu.SemaphoreType.DMA((len(_PASSED) * n,))] * 2 + [pltpu.HBM(v.shape, v.dtype) for v in lands],
        in_specs=[_HBM] * n + [_SEM] + [pl.BlockSpec(memory_space=pl.ANY)] * len(after),
        out_specs=[_SEM, _SEM] + [_HBM] * n,
        input_output_aliases={i: 2 + i for i in range(n)},
        compiler_params=pltpu.CompilerParams(has_side_effects=_ORDERED_BY_DATA),
    )(*lands, sems[1], *after)
    return (res[0], res[1]), list(res[2:])


def relay_wait(lands, first, second, after, name):
    n = len(lands)

    def body(*refs):
        bufs = refs[:n]
        send_first, recv_first, send, recv = refs[n:n + 4]
        for a, land in enumerate(bufs):
            mine = _slot_of(land, 0)
            _remote(mine, _slot_of(land, 1), send_first, recv_first, len(_DIRECT) * a, 1).wait_recv()
            for j, k in enumerate(_DIRECT):
                _remote(mine, mine, send_first, recv_first, len(_DIRECT) * a + j, k).wait_send()
            for j, k in enumerate(_PASSED):
                cp = _remote(_slot_of(land, k ^ 1), _slot_of(land, k), send, recv, len(_PASSED) * a + j, 1)
                cp.wait_send()
                cp.wait_recv()

    return list(pl.pallas_call(
        body, name=name,
        out_shape=[pltpu.HBM(v.shape, v.dtype) for v in lands],
        in_specs=[_HBM] * n + [_SEM] * 4 + [pl.BlockSpec(memory_space=pl.ANY)] * len(after),
        out_specs=[_HBM] * n,
        input_output_aliases={i: i for i in range(n)},
        compiler_params=pltpu.CompilerParams(has_side_effects=_ORDERED_BY_DATA),
    )(*lands, first[0], first[1], second[0], second[1], *after))


def exchange_start(partials, name):
    lands = [lax.empty((N_DEV - 1,) + p.shape[1:], p.dtype) for p in partials]
    sems, bufs, token = _start_call([list(partials) + lands], [len(partials)], _exchange_copies, name)
    return sems[0], bufs, token


def exchange_wait(bufs, sems, after, name):
    bufs = _wait_call(bufs, sems, _exchange_copies, after, name)
    return bufs[:len(bufs) // 2], bufs[len(bufs) // 2:]


def prep_weights(items, dev, name, dtype=BF16, after=()):
    n = len(items)
    blocks = [(1, w.shape[2], w.shape[1]) if t else (1,) + w.shape[1:] for w, _, t in items]

    def body(d_ref, *refs):
        for (_, _, transpose), w_ref, o_ref in zip(items, refs[:n], refs[-n:]):
            v = w_ref[0]
            o_ref[0] = (v.T if transpose else v).astype(dtype)

    return pl.pallas_call(
        body, name=name,
        grid_spec=pltpu.PrefetchScalarGridSpec(
            num_scalar_prefetch=1, grid=(1,),
            in_specs=[pl.BlockSpec((1,) + w.shape[1:], lambda i, d, layer=layer: (layer, 0, 0)) for w, layer, _ in items]
            + [pl.BlockSpec(memory_space=pl.ANY)] * len(after),
            out_specs=[pl.BlockSpec(blk, lambda i, d: (d[0], 0, 0)) for blk in blocks]),
        out_shape=[jax.ShapeDtypeStruct((N_DEV,) + blk[1:], dtype) for blk in blocks],
        compiler_params=_params("arbitrary"),
    )(dev, *[w for w, _, _ in items], *after)


def _adamw_math(w, g, m, v):
    mn = ADAM_B1 * m + (1.0 - ADAM_B1) * g
    vn = ADAM_B2 * v + (1.0 - ADAM_B2) * (g * g)
    m_hat = mn * (1.0 / (1.0 - ADAM_B1 ** ADAM_STEP))
    v_hat = vn * (1.0 / (1.0 - ADAM_B2 ** ADAM_STEP))
    return -ADAM_LR * (m_hat / (jnp.sqrt(v_hat) + ADAM_EPS) + ADAM_WD * w), mn, vn


def sum_adamw(partial, landed, w, m, v, prev, layer, transpose, dev, name):
    _, r, c = partial.shape
    if transpose:
        tc = _tile(c, 256)
        grid = (c // tc,)
        part_spec = pl.BlockSpec((1, r, tc), lambda i, d: (d[0], 0, i))
        land_spec = pl.BlockSpec((N_DEV - 1, r, tc), lambda i, d: (0, 0, i))
        w_spec = pl.BlockSpec((1, tc, r), lambda i, d: (layer, i, 0))
    else:
        tr = r // 2 if r % 32 == 0 else r
        grid = (r // tr,)
        part_spec = pl.BlockSpec((1, tr, c), lambda i, d: (d[0], i, 0))
        land_spec = pl.BlockSpec((N_DEV - 1, tr, c), lambda i, d: (0, i, 0))
        w_spec = pl.BlockSpec((1, tr, c), lambda i, d: (layer, i, 0))

    def body(d_ref, p_ref, l_ref, w_ref, m_ref, v_ref, *rest):
        g_ref, dl_ref, mo_ref, vo_ref = rest[-4:]
        acc = p_ref[0].astype(F32)
        for k in range(N_DEV - 1):
            acc = acc + l_ref[k].astype(F32)
        g = acc.T if transpose else acc
        delta, mn, vn = _adamw_math(w_ref[0], g, m_ref[0], v_ref[0])
        g_ref[0], dl_ref[0], mo_ref[0], vo_ref[0] = g, delta, mn, vn

    return pl.pallas_call(
        body, name=name,
        grid_spec=pltpu.PrefetchScalarGridSpec(
            num_scalar_prefetch=1, grid=grid,
            in_specs=[part_spec, land_spec, w_spec, w_spec, w_spec] + [pl.BlockSpec(memory_space=pl.ANY)] * 4,
            out_specs=[w_spec] * 4),
        out_shape=[jax.ShapeDtypeStruct(w.shape, F32)] * 4,
        input_output_aliases={6 + q: q for q in range(4)},
        compiler_params=_params("parallel"),
    )(dev, partial, landed, w, m, v, *prev)


def sum_parts(parts, name):
    P, r, c = parts.shape

    def body(p_ref, o_ref):
        acc = p_ref[0].astype(F32)
        for s in range(1, P):
            acc = acc + p_ref[s].astype(F32)
        o_ref[...] = acc

    return pl.pallas_call(
        body, name=name, in_specs=[_full((P, r, c))], out_specs=_full((r, c)), grid=(1,),
        out_shape=jax.ShapeDtypeStruct((r, c), F32), compiler_params=_params("arbitrary"))(parts)


def adamw(w, g, m, v, name):
    R, C = w.shape
    tr = _tile(R, 512)

    def body(w_ref, g_ref, m_ref, v_ref, d_ref, mo_ref, vo_ref):
        d_ref[...], mo_ref[...], vo_ref[...] = _adamw_math(w_ref[...], g_ref[...], m_ref[...], v_ref[...])

    spec = pl.BlockSpec((tr, C), lambda i: (i, 0))
    return pl.pallas_call(
        body, name=name, grid=(R // tr,),
        in_specs=[spec] * 4, out_specs=[spec] * 3,
        out_shape=[jax.ShapeDtypeStruct((R, C), F32)] * 3,
        compiler_params=_params("parallel"),
    )(w, g, m, v)


def sum_adamw_small(gathered, w, m, v, name):
    rows = w.shape[0]
    tr = _tile(rows, 512)

    def body(p_ref, w_ref, m_ref, v_ref, g_ref, d_ref, mo_ref, vo_ref):
        g = p_ref[0]
        for s in range(1, N_DEV):
            g = g + p_ref[s]
        g_ref[...] = g
        d_ref[...], mo_ref[...], vo_ref[...] = _adamw_math(w_ref[...], g, m_ref[...], v_ref[...])

    spec = pl.BlockSpec((tr, LANES), lambda i: (i, 0))
    return pl.pallas_call(
        body, name=name, grid=(rows // tr,),
        in_specs=[pl.BlockSpec((N_DEV, tr, LANES), lambda i: (0, i, 0)), spec, spec, spec], out_specs=[spec] * 4,
        out_shape=[jax.ShapeDtypeStruct((rows, LANES), F32)] * 4, compiler_params=_params("parallel"))(gathered, w, m, v)


def first_norm(h, g, name):
    T, D = h.shape
    tm = _tile(T, 512)

    def body(h_ref, g_ref, hn_ref):
        hn_ref[...] = _rms(h_ref[...], g_ref[...]).astype(BF16)

    row = _rows(tm, D)
    return pl.pallas_call(
        body, name=name, grid=(T // tm,), in_specs=[row, _full((1, D))], out_specs=row,
        out_shape=jax.ShapeDtypeStruct((T, D), BF16), compiler_params=_params("parallel"))(h, g)


_NT = (((1,), (1,)), ((), ()))
_TN = (((0,), (0,)), ((), ()))


def rowmm(a, w, transposed_w, bias, out_dtype, name, after=None):
    M, K = a.shape
    N = w.shape[0] if transposed_w else w.shape[1]
    tm = _tile(M, 512)
    tn = _tile(N, 512)

    def body(*refs):
        a_ref, w_ref = refs[:2]
        o_ref = refs[-1]
        av = a_ref[...]
        for j in range(N // tn):
            cols = pl.ds(j * tn, tn)
            if transposed_w:
                acc = lax.dot_general(av, w_ref[cols, :], _NT, preferred_element_type=F32)
            else:
                acc = jnp.dot(av, w_ref[:, cols], preferred_element_type=F32)
            if bias is not None:
                acc = acc + refs[2][:, cols]
            o_ref[:, cols] = acc.astype(out_dtype)

    ins, in_specs = [a, w], [_rows(tm, K), _resident(w.shape)]
    if bias is not None:
        ins.append(bias)
        in_specs.append(_full((1, N)))
    if after is not None:
        ins.append(after)
        in_specs.append(pl.BlockSpec(memory_space=pl.ANY))
    return pl.pallas_call(
        body, name=name, grid=(M // tm,), in_specs=in_specs, out_specs=_rows(tm, N),
        out_shape=jax.ShapeDtypeStruct((M, N), out_dtype), compiler_params=_params("parallel"))(*ins)


def _rows_times_weight(a_ref, w_ref, acc_ref):
    N = w_ref.shape[1]
    tn = _tile(N, 512)
    av = a_ref[...]
    for j in range(N // tn):
        cols = pl.ds(j * tn, tn)
        acc_ref[:, cols] = jnp.dot(av, w_ref[:, cols], preferred_element_type=F32)


def mm_add_norm(a, w, h, g_post, g_pre, name):
    T, K = a.shape
    D = w.shape[1]
    tm = _tile(T, 512)

    def body(a_ref, w_ref, h_ref, gp_ref, g_ref, y_ref, ho_ref, hn_ref, acc_ref):
        _rows_times_weight(a_ref, w_ref, acc_ref)
        y = acc_ref[...]
        y_ref[...] = y.astype(BF16)
        hv = h_ref[...] + _rms(y, gp_ref[...])
        ho_ref[...] = hv
        hn_ref[...] = _rms(hv, g_ref[...]).astype(BF16)

    row, vec = _rows(tm, D), _full((1, D))
    return pl.pallas_call(
        body, name=name, grid=(T // tm,), in_specs=[_rows(tm, K), _resident(w.shape), row, vec, vec],
        out_specs=[row, row, row],
        out_shape=[jax.ShapeDtypeStruct((T, D), BF16), jax.ShapeDtypeStruct((T, D), F32),
                   jax.ShapeDtypeStruct((T, D), BF16)],
        scratch_shapes=[pltpu.VMEM((tm, D), F32)], compiler_params=_params("parallel"))(a, w, h, g_post, g_pre)


def mm_add_norm_loss(a, w, h, g_post, target, name):
    T, K = a.shape
    D = w.shape[1]
    tm = _tile(T, 512)

    def body(a_ref, w_ref, h_ref, gp_ref, t_ref, dh_ref, dy_ref, loss_ref, dgp_ref, acc_ref):
        @pl.when(pl.program_id(0) == 0)
        def _():
            loss_ref[...] = jnp.zeros_like(loss_ref)
            dgp_ref[...] = jnp.zeros_like(dgp_ref)
        _rows_times_weight(a_ref, w_ref, acc_ref)
        y = acc_ref[...]
        err = h_ref[...] + _rms(y, gp_ref[...]) - t_ref[...]
        dh = err * (1.0 / D)
        dh_ref[...] = dh
        loss_ref[...] += jnp.sum(err * err)
        dy, dgp = _rms_bwd(y, gp_ref[...], dh)
        dy_ref[...] = dy.astype(BF16)
        dgp_ref[...] += dgp

    row, vec = _rows(tm, D), _full((1, D))
    return pl.pallas_call(
        body, name=name, grid=(T // tm,), in_specs=[_rows(tm, K), _resident(w.shape), row, vec, row],
        out_specs=[row, row, _full((8, LANES)), vec],
        out_shape=[jax.ShapeDtypeStruct((T, D), F32), jax.ShapeDtypeStruct((T, D), BF16),
                   jax.ShapeDtypeStruct((8, LANES), F32), jax.ShapeDtypeStruct((1, D), F32)],
        scratch_shapes=[pltpu.VMEM((tm, D), F32)], compiler_params=_params("arbitrary"))(a, w, h, g_post, target)


def mm_norm_bwd(a, w, dh_in, x, g_pre, post, name, after):
    T, K = a.shape
    D = w.shape[1]
    tm = _tile(T, 512)
    row, vec = _rows(tm, D), _full((1, D))
    ins, in_specs = [a, w, dh_in, x, g_pre], [_rows(tm, K), _resident(w.shape), row, row, vec]
    outs = [jax.ShapeDtypeStruct((T, D), F32), jax.ShapeDtypeStruct((1, D), F32)]
    out_specs, names = [row, vec], ["dh", "dg_pre"]
    if post is not None:
        ins += list(post)
        in_specs += [row, vec]
        outs += [jax.ShapeDtypeStruct((T, D), BF16), jax.ShapeDtypeStruct((1, D), F32)]
        out_specs += [row, vec]
        names += ["dy", "dg_post"]
    n_in = len(ins) + 1

    def body(*refs):
        a_ref, w_ref, dh_ref_in, x_ref, g_ref = refs[:5]
        out_refs, acc_ref = refs[n_in:-1], refs[-1]
        first = pl.program_id(0) == 0
        _rows_times_weight(a_ref, w_ref, acc_ref)
        dx, dg = _rms_bwd(x_ref[...], g_ref[...], acc_ref[...])
        dh = dh_ref_in[...] + dx
        out_refs[0][...] = dh

        @pl.when(first)
        def _():
            for r in out_refs[1::2]:
                r[...] = jnp.zeros_like(r)
        out_refs[1][...] += dg
        if post is not None:
            y_ref, gp_ref = refs[5:7]
            dy, dgp = _rms_bwd(y_ref[...].astype(F32), gp_ref[...], dh)
            out_refs[2][...] = dy.astype(BF16)
            out_refs[3][...] += dgp

    res = pl.pallas_call(
        body, name=name, grid=(T // tm,), in_specs=in_specs + [pl.BlockSpec(memory_space=pl.ANY)],
        out_specs=out_specs, out_shape=outs, scratch_shapes=[pltpu.VMEM((tm, D), F32)],
        compiler_params=_params("arbitrary"))(*ins, after)
    return dict(zip(names, res))


def grad_mm(a, b, name, after=()):
    T, N = a.shape
    K = b.shape[1]
    tn = _tile(N, 1408)
    tt = _tile(T, 1024)

    def body(a_ref, b_ref, *rest):
        o_ref, acc_ref = rest[-2:]
        t = pl.program_id(1)

        @pl.when(t == 0)
        def _():
            acc_ref[...] = jnp.zeros_like(acc_ref)
        acc_ref[...] += lax.dot_general(a_ref[...], b_ref[...], _TN, preferred_element_type=F32)

        @pl.when(t == pl.num_programs(1) - 1)
        def _():
            o_ref[...] = acc_ref[...].astype(BF16)

    return pl.pallas_call(
        body, name=name, grid=(N // tn, T // tt),
        in_specs=[pl.BlockSpec((tt, tn), lambda j, t: (t, j)), pl.BlockSpec((tt, K), lambda j, t: (t, 0))]
        + [pl.BlockSpec(memory_space=pl.ANY)] * len(after),
        out_specs=pl.BlockSpec((tn, K), lambda j, t: (j, 0)),
        out_shape=jax.ShapeDtypeStruct((N, K), BF16),
        scratch_shapes=[pltpu.VMEM((tn, K), F32)],
        compiler_params=_params("parallel", "arbitrary"))(a, b, *after)


def ffn_up(fn, w_gu_t, name):
    T, D = fn.shape
    F = w_gu_t.shape[0] // 2
    tm = _tile(T, 512)
    tn = _tile(F, 512)

    def body(a_ref, w_ref, act_ref, part_ref):
        av = a_ref[...]
        for j in range(F // tn):
            g = lax.dot_general(av, w_ref[pl.ds(j * tn, tn), :], _NT, preferred_element_type=F32)
            up = lax.dot_general(av, w_ref[pl.ds(F + j * tn, tn), :], _NT, preferred_element_type=F32)
            sg = _sigmoid(g)
            silu = g * sg
            act_ref[:, pl.ds(j * tn, tn)] = (silu * up).astype(BF16)
            part_ref[:, pl.ds(j * tn, tn)] = (up * (sg + silu * (1.0 - sg))).astype(BF16)
            part_ref[:, pl.ds(F + j * tn, tn)] = silu.astype(BF16)

    return pl.pallas_call(
        body, name=name, grid=(T // tm,), in_specs=[_rows(tm, D), _resident(w_gu_t.shape)],
        out_specs=[_rows(tm, F), _rows(tm, 2 * F)],
        out_shape=[jax.ShapeDtypeStruct((T, F), BF16), jax.ShapeDtypeStruct((T, 2 * F), BF16)],
        compiler_params=_params("parallel"))(fn, w_gu_t)


def ffn_dact(df, w_down, partials, name, after=()):
    T, D = df.shape
    F = w_down.shape[0]
    tm = _tile(T, 512)
    tn = _tile(F, 512)

    def body(d_ref, w_ref, part_ref, *rest):
        o_ref = rest[-1]
        dv = d_ref[...]
        for j in range(F // tn):
            dact = lax.dot_general(dv, w_ref[pl.ds(j * tn, tn), :], _NT, preferred_element_type=F32)
            for cols in (pl.ds(j * tn, tn), pl.ds(F + j * tn, tn)):
                o_ref[:, cols] = (dact * part_ref[:, cols].astype(F32)).astype(BF16)

    return pl.pallas_call(
        body, name=name, grid=(T // tm,),
        in_specs=[_rows(tm, D), _resident(w_down.shape), _rows(tm, 2 * F)] + [pl.BlockSpec(memory_space=pl.ANY)] * len(after),
        out_specs=_rows(tm, 2 * F), out_shape=jax.ShapeDtypeStruct((T, 2 * F), BF16),
        compiler_params=_params("parallel"))(df, w_down, partials, *after)


def _layernorm_stats(v):
    mu = jnp.mean(v, axis=-1, keepdims=True)
    cen = v - mu
    rstd = lax.rsqrt(jnp.mean(cen * cen, axis=-1, keepdims=True) + LN_EPS)
    return cen * rstd, rstd


GMLP_ROWS = 2 * CHUNK


def gmlp_gate_out(a, ln_g, ln_b, w_mask, b_s_t, w_out, h, g_post, g_pre, name):
    T, W2 = a.shape
    W = W2 // 2
    G = w_mask.shape[0]
    C = W // G
    D = w_out.shape[1]
    tm = min(T, GMLP_ROWS)

    def body(a_ref, lg_ref, lb_ref, w_ref, bs_ref, wo_ref, h_ref, gp_ref, g_ref,
             gated_ref, y_ref, ho_ref, hn_ref, acc_ref):
        for c in range(tm // CHUNK):
            rows = pl.ds(c * CHUNK, CHUNK)
            xhat, _ = _layernorm_stats(_gelu(a_ref[rows, W:].astype(F32)))
            vln = (xhat * lg_ref[...] + lb_ref[...]).astype(BF16)
            for g in range(G):
                cols = pl.ds(g * C, C)
                sv = jnp.dot(w_ref[g], vln[:, g * C:(g + 1) * C], preferred_element_type=F32) + bs_ref[:, g:g + 1]
                gated_ref[rows, cols] = (_gelu(a_ref[rows, cols].astype(F32)) * sv).astype(BF16)
        _rows_times_weight(gated_ref, wo_ref, acc_ref)
        y = acc_ref[...]
        y_ref[...] = y.astype(BF16)
        hv = h_ref[...] + _rms(y, gp_ref[...])
        ho_ref[...] = hv
        hn_ref[...] = _rms(hv, g_ref[...]).astype(BF16)

    row, vec = _rows(tm, D), _full((1, D))
    return pl.pallas_call(
        body, name=name, grid=(T // tm,),
        in_specs=[_rows(tm, W2), _full((1, W)), _full((1, W)), _full(w_mask.shape), _full(b_s_t.shape),
                  _resident(w_out.shape), row, vec, vec],
        out_specs=[_rows(tm, W), row, row, row],
        out_shape=[jax.ShapeDtypeStruct((T, W), BF16), jax.ShapeDtypeStruct((T, D), BF16),
                   jax.ShapeDtypeStruct((T, D), F32), jax.ShapeDtypeStruct((T, D), BF16)],
        scratch_shapes=[pltpu.VMEM((tm, D), F32)],
        compiler_params=_params("parallel"))(a, ln_g, ln_b, w_mask, b_s_t, w_out, h, g_post, g_pre)


def gmlp_gate_bwd_dx(a, dmix, w_out, ln_g, ln_b, w_mask, w_mask_t, b_s_t, group_onehot, w_in_t,
                     dh_in, x, g_pre, post, name, after):
    T, W2 = a.shape
    W = W2 // 2
    G = w_mask.shape[0]
    C = W // G
    D = w_out.shape[1]
    tm = min(T, GMLP_ROWS)
    tn = _tile(W, 512)
    row, vec = _rows(tm, D), _full((1, D))
    ins = [a, dmix, w_out, ln_g, ln_b, w_mask, w_mask_t, b_s_t, group_onehot, w_in_t, dh_in, x, g_pre]
    in_specs = [_rows(tm, W2), row, _resident(w_out.shape), _full((1, W)), _full((1, W)), _full(w_mask.shape),
                _full(w_mask_t.shape), _full(b_s_t.shape), _full(group_onehot.shape), _resident(w_in_t.shape), row, row, vec]
    names = ["da", "dw_s", "db_s_t", "dln_g", "dln_b", "db_in", "dh", "dg_pre"]
    outs = [jax.ShapeDtypeStruct((T, W2), BF16), jax.ShapeDtypeStruct((G, CHUNK, CHUNK), F32),
            jax.ShapeDtypeStruct((CHUNK, LANES), F32), jax.ShapeDtypeStruct((1, W), F32),
            jax.ShapeDtypeStruct((1, W), F32), jax.ShapeDtypeStruct((1, W2), F32),
            jax.ShapeDtypeStruct((T, D), F32), jax.ShapeDtypeStruct((1, D), F32)]
    out_specs = [_rows(tm, W2), _full((G, CHUNK, CHUNK)), _full((CHUNK, LANES)), _full((1, W)), _full((1, W)),
                 _full((1, W2)), row, vec]
    if post is not None:
        ins += list(post)
        in_specs += [row, vec]
        names += ["dy", "dg_post"]
        outs += [jax.ShapeDtypeStruct((T, D), BF16), jax.ShapeDtypeStruct((1, D), F32)]
        out_specs += [row, vec]
    n_in = len(ins) + 1

    def body(*refs):
        (a_ref, dm_ref, wo_ref, lg_ref, lb_ref, w_ref, wt_ref, bs_ref, e_ref, wi_ref, dh_in_ref, x_ref, gpre_ref) = refs[:13]
        out_refs = refs[n_in:-3]
        da_ref, dws_ref, dbs_ref, dlg_ref, dlb_ref, dbin_ref, dh_ref, dgpre_ref = out_refs[:8]
        dgated_ref, dvln_ref, acc_ref = refs[-3:]

        @pl.when(pl.program_id(0) == 0)
        def _():
            for r in out_refs[1:6] + out_refs[7::2]:
                r[...] = jnp.zeros_like(r)
        dm = dm_ref[...]
        for j in range(W // tn):
            cols = pl.ds(j * tn, tn)
            dgated_ref[:, cols] = lax.dot_general(dm, wo_ref[cols, :], _NT, preferred_element_type=F32)
        causal = (lax.broadcasted_iota(jnp.int32, (CHUNK, CHUNK), 1)
                  <= lax.broadcasted_iota(jnp.int32, (CHUNK, CHUNK), 0))
        for c in range(tm // CHUNK):
            rows = pl.ds(c * CHUNK, CHUNK)
            gelu_v, gelu_grad_v = _gelu_and_grad(a_ref[rows, W:].astype(F32))
            xhat, rstd = _layernorm_stats(gelu_v)
            vln = (xhat * lg_ref[...] + lb_ref[...]).astype(BF16)
            for g in range(G):
                cols = pl.ds(g * C, C)
                vg = vln[:, g * C:(g + 1) * C]
                gelu_u, gelu_grad_u = _gelu_and_grad(a_ref[rows, cols].astype(F32))
                dgated = dgated_ref[rows, cols]
                dsv = (dgated * gelu_u).astype(BF16)
                dbs_ref[...] += jnp.dot(dsv, e_ref[cols, :], preferred_element_type=F32)
                sv = jnp.dot(w_ref[g], vg, preferred_element_type=F32) + bs_ref[:, g:g + 1]
                dau = dgated * sv * gelu_grad_u
                da_ref[rows, cols] = dau.astype(BF16)
                dbin_ref[:, cols] += jnp.sum(dau, axis=0, keepdims=True)
                dws_ref[g] += jnp.where(causal, lax.dot_general(dsv, vg, _NT, preferred_element_type=F32), 0.0)
                dvln_ref[:, cols] = jnp.dot(wt_ref[g], dsv, preferred_element_type=F32)
            dvln = dvln_ref[...]
            dlg_ref[...] += jnp.sum(dvln * xhat, axis=0, keepdims=True)
            dlb_ref[...] += jnp.sum(dvln, axis=0, keepdims=True)
            dxhat = dvln * lg_ref[...]
            dv = rstd * (dxhat - jnp.mean(dxhat, axis=-1, keepdims=True)
                         - xhat * jnp.mean(dxhat * xhat, axis=-1, keepdims=True))
            dav = dv * gelu_grad_v
            da_ref[rows, W:] = dav.astype(BF16)
            dbin_ref[:, W:] += jnp.sum(dav, axis=0, keepdims=True)
        _rows_times_weight(da_ref, wi_ref, acc_ref)
        dx, dg = _rms_bwd(x_ref[...], gpre_ref[...], acc_ref[...])
        dh = dh_in_ref[...] + dx
        dh_ref[...] = dh
        dgpre_ref[...] += dg
        if post is not None:
            y_ref, gp_ref = refs[13:15]
            dy, dgp = _rms_bwd(y_ref[...].astype(F32), gp_ref[...], dh)
            out_refs[8][...] = dy.astype(BF16)
            out_refs[9][...] += dgp

    res = pl.pallas_call(
        body, name=name, grid=(T // tm,), in_specs=in_specs + [pl.BlockSpec(memory_space=pl.ANY)],
        out_specs=out_specs, out_shape=outs,
        scratch_shapes=[pltpu.VMEM((tm, W), F32), pltpu.VMEM((CHUNK, W), F32), pltpu.VMEM((tm, D), F32)],
        compiler_params=_params("arbitrary"))(*ins, after)
    return dict(zip(names, res))


def rope_tables(pos, inv_freq_row, name):
    T = pos.shape[0]
    tm = _tile(T, 512)

    def body(p_ref, f_ref, c_ref, s1_ref, s2_ref):
        ang = p_ref[...].astype(F32) * f_ref[...]
        lane = lax.broadcasted_iota(jnp.int32, (tm, LANES), 1) % HEAD_DIM
        sin = jnp.sin(ang)
        c_ref[...] = jnp.cos(ang)
        s1_ref[...] = jnp.where(lane < _HALF, -sin, 0.0)
        s2_ref[...] = jnp.where((lane >= _HALF) & (lane < ROPE_DIM), sin, 0.0)

    tab = _rows(tm, LANES)
    return pl.pallas_call(
        body, name=name, grid=(T // tm,), in_specs=[_rows(tm, 1), _full((1, LANES))], out_specs=[tab] * 3,
        out_shape=[jax.ShapeDtypeStruct((T, LANES), F32)] * 3, compiler_params=_params("parallel"))(pos, inv_freq_row)


_HALF = ROPE_DIM // 2


def _slabs(x):
    return [x[:, b * LANES:(b + 1) * LANES] for b in range(x.shape[1] // LANES)]


def _rotate(x, c, s1, s2):
    return [xs * c + pltpu.roll(xs, LANES - _HALF, 1) * s1 + pltpu.roll(xs, _HALF, 1) * s2 for xs in _slabs(x)]


def _rotate_transposed(dy, c, s1, s2):
    return [ds * c + pltpu.roll(ds * s1, _HALF, 1) + pltpu.roll(ds * s2, LANES - _HALF, 1) for ds in _slabs(dy)]


def rope_fwd(qkv, tabs, q_width, kv_width, name):
    T, QKV = qkv.shape
    tm = _tile(T, 512)
    QK = q_width + kv_width
    scale = HEAD_DIM ** -0.5

    def body(x_ref, c_ref, s1_ref, s2_ref, o_ref):
        x = x_ref[:, :QK].astype(F32)
        slabs = _rotate(x, c_ref[...], s1_ref[...], s2_ref[...])
        for b, y in enumerate(slabs):
            if b * LANES < q_width:
                y = y * scale
            o_ref[:, b * LANES:(b + 1) * LANES] = y.astype(BF16)

    tab = _rows(tm, LANES)
    return pl.pallas_call(
        body, name=name, grid=(T // tm,), in_specs=[_rows(tm, QKV), tab, tab, tab], out_specs=_rows(tm, QK),
        out_shape=jax.ShapeDtypeStruct((T, QK), BF16), compiler_params=_params("parallel"))(qkv, *tabs)


def rope_bwd(dq, dk, dv, tabs, name):
    T, Q = dq.shape
    KV = dk.shape[1]
    tm = _tile(T, 512)
    scale = HEAD_DIM ** -0.5

    def body(dq_ref, dk_ref, dv_ref, c_ref, s1_ref, s2_ref, o_ref, b_ref):
        @pl.when(pl.program_id(0) == 0)
        def _():
            b_ref[...] = jnp.zeros_like(b_ref)
        tabs_v = (c_ref[...], s1_ref[...], s2_ref[...])
        pieces = [s * scale for s in _rotate_transposed(dq_ref[...], *tabs_v)]
        pieces += _rotate_transposed(dk_ref[...], *tabs_v)
        pieces += _slabs(dv_ref[...])
        for b, y in enumerate(pieces):
            cols = pl.ds(b * LANES, LANES)
            o_ref[:, cols] = y.astype(BF16)
            b_ref[:, cols] += jnp.sum(y, axis=0, keepdims=True)

    tab = _rows(tm, LANES)
    return pl.pallas_call(
        body, name=name, grid=(T // tm,), in_specs=[_rows(tm, Q), _rows(tm, KV), _rows(tm, KV), tab, tab, tab],
        out_specs=[_rows(tm, Q + 2 * KV), _full((1, Q + 2 * KV))],
        out_shape=[jax.ShapeDtypeStruct((T, Q + 2 * KV), BF16), jax.ShapeDtypeStruct((1, Q + 2 * KV), F32)],
        compiler_params=_params("arbitrary"))(dq, dk, dv, *tabs)


def _band_mask(n, heads=1):
    qi = lax.broadcasted_iota(jnp.int32, (heads * CHUNK, 2 * CHUNK), 0) % CHUNK
    sj = lax.broadcasted_iota(jnp.int32, (heads * CHUNK, 2 * CHUNK), 1)
    return (sj > qi) & (sj <= qi + CHUNK) & ((n > 0) | (sj >= CHUNK))


def _kv_head(kc_ref, kp_ref, vc_ref, vp_ref, kh):
    lanes = slice(kh * HEAD_DIM, (kh + 1) * HEAD_DIM)
    return (jnp.concatenate([kp_ref[:, lanes], kc_ref[:, lanes]], axis=0),
            jnp.concatenate([vp_ref[:, lanes], vc_ref[:, lanes]], axis=0))


def _row_sums(x):
    return jnp.dot(x, jnp.ones((x.shape[1], LANES), BF16), preferred_element_type=F32)


def _lanes(v, n):
    return v[:, :n] if n <= LANES else jnp.concatenate([v] * (n // LANES), axis=-1)


def _head_probs(q, kk, valid, sink, sums_on_mxu):
    s = jnp.where(valid, lax.dot_general(q, kk, _NT, preferred_element_type=F32), NEG_INF)
    m = jnp.maximum(jnp.max(s, axis=-1, keepdims=True), sink)
    p = jnp.exp(s - m)
    e_sink = jnp.exp(sink - m)
    total = _row_sums(p.astype(BF16)) if sums_on_mxu else jnp.sum(p, axis=-1, keepdims=True)
    return p.astype(BF16), 1.0 / (total + e_sink), e_sink


def _attn_specs(q_width, kv_width, order):
    qb = q_width // kv_width
    prev = lambda i: jnp.maximum(order(i) - 1, 0)
    return [pl.BlockSpec((CHUNK, q_width), lambda i: (order(i), 0)),
            pl.BlockSpec((CHUNK, kv_width), lambda i: (order(i), qb)),
            pl.BlockSpec((CHUNK, kv_width), lambda i: (prev(i), qb)),
            pl.BlockSpec((CHUNK, kv_width), lambda i: (order(i), qb + 1)),
            pl.BlockSpec((CHUNK, kv_width), lambda i: (prev(i), qb + 1))]


def attn_fwd(qk, qkv, sinks, q_width, kv_width, name):
    T = qk.shape[0]
    group = q_width // kv_width

    def body(q_ref, kc_ref, kp_ref, vc_ref, vp_ref, sink_ref, o_ref):
        valid = _band_mask(pl.program_id(0))
        for kh in range(kv_width // HEAD_DIM):
            kk, vv = _kv_head(kc_ref, kp_ref, vc_ref, vp_ref, kh)
            for g in range(group):
                h = kh * group + g
                lanes = slice(h * HEAD_DIM, (h + 1) * HEAD_DIM)
                p, inv, _ = _head_probs(q_ref[:, lanes], kk, valid, sink_ref[h], False)
                o_ref[:, lanes] = (jnp.dot(p, vv, preferred_element_type=F32) * inv).astype(BF16)

    specs = _attn_specs(q_width, kv_width, lambda i: i)
    return pl.pallas_call(
        body, name=name, grid=(T // CHUNK,),
        in_specs=specs + [pl.BlockSpec(memory_space=pltpu.SMEM)],
        out_specs=_rows(CHUNK, q_width), out_shape=jax.ShapeDtypeStruct((T, q_width), BF16),
        compiler_params=_params("parallel"))(qk, qk, qk, qkv, qkv, sinks)


def attn_bwd(qk, qkv, do, sinks, q_width, kv_width, name):
    T = qk.shape[0]
    NB = T // CHUNK
    group = q_width // kv_width

    def body(q_ref, kc_ref, kp_ref, vc_ref, vp_ref, do_ref, sink_ref, dq_ref, dk_ref, dv_ref, ds_ref, ck_ref, cv_ref):
        i = pl.program_id(0)
        n = NB - 1 - i

        @pl.when(i == 0)
        def _():
            ck_ref[...] = jnp.zeros_like(ck_ref)
            cv_ref[...] = jnp.zeros_like(cv_ref)
            ds_ref[...] = jnp.zeros_like(ds_ref)
        lane = lax.broadcasted_iota(jnp.int32, (1, LANES), 1)
        dsink_row = jnp.zeros((1, LANES), F32)
        valid = _band_mask(n, group)
        head = lax.broadcasted_iota(jnp.int32, (group * CHUNK, 1), 0) // CHUNK
        for kh in range(kv_width // HEAD_DIM):
            kk, vv = _kv_head(kc_ref, kp_ref, vc_ref, vp_ref, kh)
            heads = [slice((kh * group + g) * HEAD_DIM, (kh * group + g + 1) * HEAD_DIM) for g in range(group)]
            q = jnp.concatenate([q_ref[:, hs] for hs in heads], axis=0)
            do = jnp.concatenate([do_ref[:, hs] for hs in heads], axis=0)
            sink = jnp.zeros((group * CHUNK, 1), F32)
            for g in range(group):
                sink = jnp.where(head == g, sink_ref[kh * group + g], sink)
            p, inv, e_sink = _head_probs(q, kk, valid, sink, True)
            p = p.astype(F32) * _lanes(inv, 2 * CHUNK)
            dp = lax.dot_general(do, vv, _NT, preferred_element_type=F32)
            delta = _row_sums((p * dp).astype(BF16))
            ds = (p * (dp - _lanes(delta, 2 * CHUNK))).astype(BF16)
            dsink = -e_sink * inv[:, :1] * delta[:, :1]
            dq = jnp.dot(ds, kk, preferred_element_type=F32)
            for g, hs in enumerate(heads):
                rows = slice(g * CHUNK, (g + 1) * CHUNK)
                dsink_row = dsink_row + jnp.where(lane == kh * group + g, jnp.sum(dsink[rows]), 0.0)
                dq_ref[:, hs] = dq[rows]
            dkk = lax.dot_general(ds, q, _TN, preferred_element_type=F32)
            dvv = lax.dot_general(p.astype(BF16), do, _TN, preferred_element_type=F32)
            lanes = slice(kh * HEAD_DIM, (kh + 1) * HEAD_DIM)
            dk_ref[:, lanes] = dkk[CHUNK:] + ck_ref[:, lanes]
            dv_ref[:, lanes] = dvv[CHUNK:] + cv_ref[:, lanes]
            ck_ref[:, lanes] = dkk[:CHUNK]
            cv_ref[:, lanes] = dvv[:CHUNK]
        ds_ref[0:1, :] += dsink_row

    order = lambda i: NB - 1 - i
    specs = _attn_specs(q_width, kv_width, order)
    kv_out = pl.BlockSpec((CHUNK, kv_width), lambda i: (order(i), 0))
    q_rows = pl.BlockSpec((CHUNK, q_width), lambda i: (order(i), 0))
    return pl.pallas_call(
        body, name=name, grid=(NB,),
        in_specs=specs + [q_rows, pl.BlockSpec(memory_space=pltpu.SMEM)],
        out_specs=[q_rows, kv_out, kv_out, _full((8, LANES))],
        out_shape=[jax.ShapeDtypeStruct((T, q_width), F32), jax.ShapeDtypeStruct((T, kv_width), F32),
                   jax.ShapeDtypeStruct((T, kv_width), F32), jax.ShapeDtypeStruct((8, LANES), F32)],
        scratch_shapes=[pltpu.VMEM((CHUNK, kv_width), F32), pltpu.VMEM((CHUNK, kv_width), F32)],
        compiler_params=_params("arbitrary"))(qk, qk, qk, qkv, qkv, do, sinks)


def _blocked(w):
    return w.reshape(N_DEV, w.shape[0] // N_DEV, w.shape[1])


def kernel(x, positions, pre_mix_g, post_mix_g, pre_ffn_g, post_ffn_g, a_w_in, a_b_in, a_ln_g, a_ln_b, a_w_s, a_b_s, a_w_out, b_w_qkv, b_b_qkv, b_sinks, b_w_o, ffn_w_gu, ffn_w_down, loss_target, m_pre_mix_g, m_post_mix_g, m_pre_ffn_g, m_post_ffn_g, m_a_w_in, m_a_b_in, m_a_ln_g, m_a_ln_b, m_a_w_s, m_a_b_s, m_a_w_out, m_b_w_qkv, m_b_b_qkv, m_b_sinks, m_b_w_o, m_ffn_w_gu, m_ffn_w_down, v_pre_mix_g, v_post_mix_g, v_pre_ffn_g, v_post_ffn_g, v_a_w_in, v_a_b_in, v_a_ln_g, v_a_ln_b, v_a_w_s, v_a_b_s, v_a_w_out, v_b_w_qkv, v_b_b_qkv, v_b_sinks, v_b_w_o, v_ffn_w_gu, v_ffn_w_down):
    weights = dict(pre_mix_g=pre_mix_g, post_mix_g=post_mix_g, pre_ffn_g=pre_ffn_g, post_ffn_g=post_ffn_g,
                   a_w_in=a_w_in, a_b_in=a_b_in, a_ln_g=a_ln_g, a_ln_b=a_ln_b, a_w_s=a_w_s, a_b_s=a_b_s,
                   a_w_out=a_w_out, b_w_qkv=b_w_qkv, b_b_qkv=b_b_qkv, b_sinks=b_sinks, b_w_o=b_w_o,
                   ffn_w_gu=ffn_w_gu, ffn_w_down=ffn_w_down)
    mom_m = dict(pre_mix_g=m_pre_mix_g, post_mix_g=m_post_mix_g, pre_ffn_g=m_pre_ffn_g, post_ffn_g=m_post_ffn_g,
                 a_w_in=m_a_w_in, a_b_in=m_a_b_in, a_ln_g=m_a_ln_g, a_ln_b=m_a_ln_b, a_w_s=m_a_w_s, a_b_s=m_a_b_s,
                 a_w_out=m_a_w_out, b_w_qkv=m_b_w_qkv, b_b_qkv=m_b_b_qkv, b_sinks=m_b_sinks, b_w_o=m_b_w_o,
                 ffn_w_gu=m_ffn_w_gu, ffn_w_down=m_ffn_w_down)
    mom_v = dict(pre_mix_g=v_pre_mix_g, post_mix_g=v_post_mix_g, pre_ffn_g=v_pre_ffn_g, post_ffn_g=v_post_ffn_g,
                 a_w_in=v_a_w_in, a_b_in=v_a_b_in, a_ln_g=v_a_ln_g, a_ln_b=v_a_ln_b, a_w_s=v_a_w_s, a_b_s=v_a_b_s,
                 a_w_out=v_a_w_out, b_w_qkv=v_b_w_qkv, b_b_qkv=v_b_b_qkv, b_sinks=v_b_sinks, b_w_o=v_b_w_o,
                 ffn_w_gu=v_ffn_w_gu, ffn_w_down=v_ffn_w_down)
    names = list(weights)
    big = ("a_w_in", "b_w_qkv", "ffn_w_gu", "a_w_out", "b_w_o", "ffn_w_down")

    T, D = x.shape[1], x.shape[2]
    depth = pre_mix_g.shape[0]
    n_heads = b_sinks.shape[1]
    q_width = n_heads * HEAD_DIM
    kv_width = N_KV_HEADS * HEAD_DIM
    G = a_w_s.shape[1]
    W = a_ln_g.shape[1]
    device = _index(_position())

    dev = device.reshape(1).astype(jnp.int32)

    def layer_keys(i):
        mixer = ("a_w_in", "a_w_out") if i % 2 == 0 else ("b_w_qkv", "b_w_o")
        return [(k, i // 2) for k in mixer] + [("ffn_w_gu", i), ("ffn_w_down", i)]

    as_view = ("b_w_qkv", "ffn_w_gu")
    in_kernel = ("a_w_in",)
    view = lambda k, t: jnp.swapaxes(t, 1, 2) if k in as_view else t
    w_view = {k: view(k, weights[k]) for k in big}
    m_view = {k: view(k, mom_m[k]) for k in big}
    v_view = {k: view(k, mom_v[k]) for k in big}

    groups_keys = [[key] for key in layer_keys(0)] + [layer_keys(i) for i in range(1, depth)]
    group_of = {}
    for g, keys in enumerate(groups_keys):
        layer = 0 if g < len(layer_keys(0)) else g - len(layer_keys(0)) + 1
        for pos, (k, _) in enumerate(keys):
            group_of[(layer, k)] = (g, pos)
    bias_land = lax.dynamic_update_slice(
        jnp.zeros((N_DEV,) + b_b_qkv.shape, F32), b_b_qkv[None], (device, 0, 0))
    h, hn = x[0], first_norm(x[0], pre_mix_g[0][None], "norm_first")
    first = len(layer_keys(0))
    in_gather, token = [], ()
    for name, lo, hi in (("gather_start_first", 0, first), ("gather_start_rest", first, len(groups_keys))):
        groups = [list(prep_weights([(w_view[k], l, k in in_kernel) for k, l in keys], dev, f"prep_{name}_{g}",
                                    after=token)) for g, keys in enumerate(groups_keys[lo:hi])]
        if lo:
            groups[0].append(bias_land)
        gather_sems, flat_lands, token = relay_start(groups, name)
        token = (token, hn)
        for grp, sems in zip(groups, gather_sems):
            in_gather.append(([flat_lands.pop(0) for _ in grp], sems))
    all_started = token[0]
    passed, gathered = {}, {}

    def pass_on(g, after):
        if g not in passed and g < len(in_gather):
            lands, sems = in_gather[g]
            passed[g], lands = relay_pass(lands, sems, after, f"gather_pass_{g}")
            in_gather[g] = (lands, sems)

    pass_on(0, (all_started,))

    def weight(i, k, *after):
        g, pos = group_of[(i, k)]
        if g not in gathered:
            pass_on(g + 1, after)
            lands, sems = in_gather[g]
            gathered[g] = relay_wait(lands, sems, passed[g], after, f"gather_wait_{g}")
        w = gathered[g][pos]
        return w.reshape(N_DEV * w.shape[1], w.shape[2])

    causal = jnp.tril(jnp.ones((CHUNK, CHUNK), dtype=bool))
    w_mask = jnp.where(causal[None, None], a_w_s, 0.0).astype(BF16)
    w_mask_t = jnp.swapaxes(w_mask, 2, 3)
    b_s_t = jnp.swapaxes(a_b_s, 1, 2)
    group_onehot = (jnp.arange(W)[:, None] // (W // G) == jnp.arange(LANES)[None, :]).astype(BF16)
    lane = jnp.arange(LANES) % HEAD_DIM
    inv_freq = ROPE_THETA ** (-jnp.arange(0, ROPE_DIM, 2, dtype=F32) / ROPE_DIM)
    inv_freq_row = jnp.where(lane < ROPE_DIM, inv_freq[lane % (ROPE_DIM // 2)], 0.0)[None, :].astype(F32)
    tabs = rope_tables(positions.reshape(T, 1), inv_freq_row, "rope_tables")

    saved = []
    for i in range(depth):
        j = i // 2
        s = dict(h=h, hn=hn)
        post_pre = (post_mix_g[i][None], pre_ffn_g[i][None])
        if i % 2 == 0:
            s["a"] = rowmm(hn, weight(i, "a_w_in", hn, all_started), True, a_b_in[j][None], BF16, "gmlp_in")
            s["gated"], s["mix"], s["h2"], s["fn"] = gmlp_gate_out(
                s["a"], a_ln_g[j][None], a_ln_b[j][None], w_mask[j], b_s_t[j], weight(i, "a_w_out", s["a"]),
                h, *post_pre, "gmlp_gate_out")
        else:
            w_qkv = weight(i, "b_w_qkv", hn)
            b_qkv_full = jnp.swapaxes(gathered[group_of[(1, "b_w_qkv")][0]][-1], 0, 1).reshape(b_b_qkv.shape[0], 1, -1)
            s["qkv"] = rowmm(hn, w_qkv, True, b_qkv_full[j], BF16, "attn_qkv")
            s["qk"] = rope_fwd(s["qkv"], tabs, q_width, kv_width, "attn_rope")
            s["o"] = attn_fwd(s["qk"], s["qkv"], b_sinks[j], q_width, kv_width, "attn_core")
            s["mix"], s["h2"], s["fn"] = mm_add_norm(s["o"], weight(i, "b_w_o", s["o"]), h, *post_pre, "attn_out")
        s["act"], s["act_partials"] = ffn_up(s["fn"], weight(i, "ffn_w_gu", s["fn"]), "ffn_up")
        w_down = weight(i, "ffn_w_down", s["act"])
        saved.append(s)
        if i + 1 < depth:
            s["f"], h, hn = mm_add_norm(s["act"], w_down, s["h2"], post_ffn_g[i][None], pre_mix_g[i + 1][None], "ffn_down")
    small = {k: [None] * weights[k].shape[0] for k in names if k not in big}
    dh, df, loss_sum, small["post_ffn_g"][depth - 1] = mm_add_norm_loss(
        saved[-1]["act"], w_down, saved[-1]["h2"], post_ffn_g[depth - 1][None], loss_target[0], "ffn_down_loss")

    grads = {k: [None] * weights[k].shape[0] for k in big}
    in_flight = []

    def send_grads(keys, tag):
        sems, bufs, token = exchange_start([_blocked(grads[k][l]) for k, l in keys], "exchange_start_" + tag)
        in_flight.append((keys, sems, bufs, tag))
        return token

    replicated = [k for k in names if k not in big and k not in ("b_b_qkv", "a_w_s")]
    main = sum(weights[k].size for k in replicated)
    main_rows = -(-main // (LANES * LANES)) * LANES
    bias_size = b_b_qkv.shape[0] * N_DEV * b_b_qkv.shape[1]
    bias_rows = -(-bias_size // (8 * LANES)) * 8
    flat = lambda vals: [v.reshape(-1) for v in vals]

    def start_small_gather():
        packed = jnp.concatenate(
            flat(p for k in replicated for p in small[k])
            + [loss_sum[0, :1], jnp.zeros((main_rows * LANES - main - 1,), F32)] + flat(small["b_b_qkv"]) + [jnp.zeros((bias_rows * LANES - bias_size,), F32)]
        ).reshape(1, main_rows + bias_rows, LANES)
        spatial = jnp.stack(small["a_w_s"]).reshape(1, -1, LANES)
        return gather_start([prep_weights([(packed, 0, False), (spatial, 0, False)], dev, "place_small_grads", F32)],
                            "gather_start_small")

    sent = ()
    for i in reversed(range(depth)):
        j = i // 2
        s = saved[i]
        dgu = ffn_dact(df, weight(i, "ffn_w_down"), s["act_partials"], "ffn_dact", after=sent)
        alone = i == 0
        grads["ffn_w_down"][i] = grad_mm(s["act"], df, "ffn_down_grad")
        early = (send_grads(layer_keys(i)[3:], f"ffn_down_{i}"),) if alone else ()
        grads["ffn_w_gu"][i] = grad_mm(dgu, s["fn"], "ffn_up_grad", after=early)
        sent = send_grads(layer_keys(i)[2:3] if alone else layer_keys(i)[2:], f"ffn_{i}")
        r = mm_norm_bwd(dgu, weight(i, "ffn_w_gu"), dh, s["h2"], pre_ffn_g[i][None],
                        (s["mix"], post_mix_g[i][None]), "ffn_dx", sent)
        dh, dmix = r["dh"], r["dy"]
        small["pre_ffn_g"][i], small["post_mix_g"][i] = r["dg_pre"], r["dg_post"]
        post = (saved[i - 1]["f"], post_ffn_g[i - 1][None]) if i > 0 else None
        if i % 2 == 0:
            grads["a_w_out"][j] = grad_mm(s["gated"], dmix, "gmlp_out_grad")
            if alone:
                sent = send_grads(layer_keys(i)[1:2], f"mixer_out_{i}")
            r = gmlp_gate_bwd_dx(
                s["a"], dmix, weight(i, "a_w_out"), a_ln_g[j][None], a_ln_b[j][None], w_mask[j], w_mask_t[j], b_s_t[j],
                group_onehot, weight(i, "a_w_in"), dh, s["h"], pre_mix_g[i][None], post, "gmlp_gate_bwd_dx", sent)
            small["a_w_s"][j], small["a_b_s"][j] = r["dw_s"], r["db_s_t"][:, :G].T
            small["a_ln_g"][j], small["a_ln_b"][j], small["a_b_in"][j] = r["dln_g"][0], r["dln_b"][0], r["db_in"][0]
            small["pre_mix_g"][i] = r["dg_pre"]
            first_layer = ()
            if i == 0:
                small_gather = start_small_gather()
                first_layer = (small_gather[2],)
            grads["a_w_in"][j] = grad_mm(r["da"], s["hn"], "gmlp_in_grad", after=first_layer)
            sent = (send_grads(layer_keys(i)[:1] if alone else layer_keys(i)[:2], f"mixer_{i}"),)
        else:
            grads["b_w_o"][j] = grad_mm(s["o"], dmix, "attn_out_grad")
            do = rowmm(dmix, weight(i, "b_w_o"), True, None, BF16, "attn_do")
            dq, dk, dv, dsink = attn_bwd(s["qk"], s["qkv"], do, b_sinks[j], q_width, kv_width, "attn_core_bwd")
            dqkv, dbias = rope_bwd(dq, dk, dv, tabs, "attn_rope_bwd")
            small["b_sinks"][j], small["b_b_qkv"][j] = dsink[0, :n_heads], dbias[0]
            grads["b_w_qkv"][j] = grad_mm(dqkv, s["hn"], "attn_qkv_grad")
            sent = send_grads(layer_keys(i)[:2], f"mixer_{i}")
            r = mm_norm_bwd(dqkv, weight(i, "b_w_qkv"), dh, s["h"], pre_mix_g[i][None], post, "attn_dx", sent)
            sent = ()
        dh, small["pre_mix_g"][i] = r["dh"], r["dg_pre"]
        if i > 0:
            df, small["post_ffn_g"][i - 1] = r["dy"], r["dg_post"]
    grad_x = dh[None]

    small_sems, small_lands, small_sent = small_gather
    pack_main = lambda src: jnp.concatenate(
        flat(src[k] for k in replicated) + [jnp.ones((main_rows * LANES - main,), F32)]).reshape(main_rows, LANES)
    packed_w, packed_m, packed_v = pack_main(weights), pack_main(mom_m), pack_main(mom_v)

    stacked = {k: [lax.empty(w_view[k].shape, F32) for _ in range(4)] for k in big}
    after = (small_sent, dh) + sent
    for keys, sems, bufs, tag in in_flight:
        partials, landed = exchange_wait(bufs, sems, after, "exchange_wait_" + tag)
        for (k, l), part, land in zip(keys, partials, landed):
            stacked[k] = sum_adamw(part, land, w_view[k], m_view[k], v_view[k], stacked[k], l,
                                   k in in_kernel, dev, "adamw_" + k)
        after = tuple(stacked[k][0] for k, _ in keys)
    out_g, out_d, out_m, out_v = {}, {}, {}, {}
    for k in big:
        out_g[k], out_d[k], out_m[k], out_v[k] = [view(k, t) for t in stacked[k]]

    gathered_small, gathered_spatial = gather_wait(small_lands, small_sems[0], after + (packed_w, packed_m, packed_v),
                                                   "gather_wait_small")
    rows_of = lambda t: t.reshape(-1, LANES)
    spatial = sum_adamw_small(gathered_spatial, rows_of(a_w_s), rows_of(m_a_w_s), rows_of(v_a_w_s), "adamw_a_w_s")
    out_g["a_w_s"], out_d["a_w_s"], out_m["a_w_s"], out_v["a_w_s"] = [t.reshape(a_w_s.shape) for t in spatial]
    results = sum_adamw_small(gathered_small, packed_w, packed_m, packed_v, "adamw_small")
    loss = results[0].reshape(-1)[main] * (0.5 / D)
    off = 0
    for k in replicated:
        size, shape = weights[k].size, weights[k].shape
        out_g[k], out_d[k], out_m[k], out_v[k] = [t.reshape(-1)[off:off + size].reshape(shape) for t in results]
        off += size
    n_local = b_b_qkv.shape[1]
    bias_sum = sum_parts(gathered_small[:, main_rows:], "sum_bias_grads").reshape(-1)[:bias_size]
    out_g["b_b_qkv"] = lax.dynamic_slice_in_dim(
        bias_sum.reshape(b_b_qkv.shape[0], -1), device * n_local, n_local, axis=1)
    pad = lambda t: jnp.concatenate([t.reshape(-1), jnp.ones((8 * LANES - t.size,), F32)]).reshape(8, LANES)
    bias_update = adamw(pad(b_b_qkv), pad(out_g["b_b_qkv"]), pad(m_b_b_qkv), pad(v_b_b_qkv), "adamw_bias")
    out_d["b_b_qkv"], out_m["b_b_qkv"], out_v["b_b_qkv"] = [
        t.reshape(-1)[:b_b_qkv.size].reshape(b_b_qkv.shape) for t in bias_update]

    return (loss, grad_x, *[out_g[k] for k in names], *[out_d[k] for k in names],
            *[out_m[k] for k in names], *[out_v[k] for k in names])
```

```python
import math

import jax
import jax.numpy as jnp
from jax import lax
from jax.experimental import pallas as pl
from jax.experimental.pallas import tpu as pltpu

F32, BF16 = jnp.float32, jnp.bfloat16
MESH = pl.DeviceIdType.MESH
N_DEV = 8

CHUNK = 128
HEAD_DIM = 64
N_KV_HEADS = 4
ROPE_DIM = HEAD_DIM // 4
ROPE_THETA = 500000.0
RMS_EPS = 1e-6
LN_EPS = 1e-5
NEG_INF = -1e30

ADAM_LR = 0.001
ADAM_B1 = 0.9
ADAM_B2 = 0.999
ADAM_EPS = 1e-08
ADAM_WD = 0.01
ADAM_STEP = 10

V7X_VMEM_BYTES = 64 * 2 ** 20
VMEM_LIMIT = V7X_VMEM_BYTES - 8 * 2 ** 20
LANES = 128


def _params(*sem):
    return pltpu.CompilerParams(dimension_semantics=sem or None, vmem_limit_bytes=VMEM_LIMIT)


def _tile(n, pref):
    if n <= pref:
        return n
    t = pref - pref % LANES
    while t >= LANES:
        if n % t == 0:
            return t
        t -= LANES
    return n


def _full(shape):
    return pl.BlockSpec(shape, lambda *_: (0,) * len(shape))


def _resident(shape):
    return pl.BlockSpec(shape, lambda *_: (0,) * len(shape), pipeline_mode=pl.Buffered(1))


def _rows(tm, width):
    return pl.BlockSpec((tm, width), lambda i: (i, 0))


def _rms(x, g):
    r = lax.rsqrt(jnp.mean(x * x, axis=-1, keepdims=True) + RMS_EPS)
    return x * r * g


def _rms_bwd(x, g, dy):
    r = lax.rsqrt(jnp.mean(x * x, axis=-1, keepdims=True) + RMS_EPS)
    xhat = x * r
    dg = jnp.sum(dy * xhat, axis=0, keepdims=True)
    dxhat = dy * g
    dx = r * (dxhat - xhat * jnp.mean(dxhat * xhat, axis=-1, keepdims=True))
    return dx, dg


_INV_SQRT2 = 1.0 / math.sqrt(2.0)
_INV_SQRT2PI = 1.0 / math.sqrt(2.0 * math.pi)


def _gelu(x):
    return 0.5 * x * (1.0 + lax.erf(x * _INV_SQRT2))


def _gelu_and_grad(x):
    cdf = 0.5 * (1.0 + lax.erf(x * _INV_SQRT2))
    return x * cdf, cdf + x * jnp.exp(-0.5 * x * x) * _INV_SQRT2PI


def _sigmoid(x):
    return 0.5 * jnp.tanh(0.5 * x) + 0.5


def _position():
    return lax.axis_index("x"), lax.axis_index("y"), lax.axis_index("c")


def _index(p):
    return 4 * p[0] + 2 * p[1] + p[2]


_HBM = pl.BlockSpec(memory_space=pltpu.HBM)
_SEM = pl.BlockSpec(memory_space=pltpu.SEMAPHORE)
_ORDERED_BY_DATA = pltpu.SideEffectType.DATAFLOW_SIDE_EFFECTING


def _in_hbm(v):
    return pltpu.with_memory_space_constraint(v, pltpu.HBM)


def _peer(k):
    x, y, c = _position()
    return (x ^ ((k >> 2) & 1), y ^ ((k >> 1) & 1), c ^ (k & 1))


def _split_copies(srcs, dsts, send, recv):
    return [pltpu.make_async_remote_copy(
        src_ref=src(k), dst_ref=dst(k), send_sem=send.at[7 * a + k - 1], recv_sem=recv.at[7 * a + k - 1],
        device_id=_peer(k), device_id_type=MESH)
        for a, (src, dst) in enumerate(zip(srcs, dsts)) for k in range(1, N_DEV)]


def _start_call(groups, sent, copies_of, name, per_array=N_DEV - 1):
    flat = [v for grp in groups for v in grp]
    n, ng = len(flat), len(groups)

    def body(*refs):
        bufs, sems = refs[:n], refs[n:n + 2 * ng]
        q = 0
        for g, grp in enumerate(groups):
            for cp in copies_of(bufs[q:q + len(grp)], sems[2 * g], sems[2 * g + 1]):
                cp.start()
            q += len(grp)
        refs[-1][...] = jnp.zeros_like(refs[-1])

    sem_shapes = []
    for count in sent:
        sem_shapes += [pltpu.SemaphoreType.DMA((per_array * count,)) for _ in range(2)]
    res = pl.pallas_call(
        body, name=name,
        out_shape=sem_shapes + [pltpu.HBM(v.shape, v.dtype) for v in flat] + [jax.ShapeDtypeStruct((8, LANES), F32)],
        in_specs=[_HBM] * n,
        out_specs=[_SEM] * (2 * ng) + [_HBM] * n + [pl.BlockSpec(memory_space=pltpu.VMEM)],
        input_output_aliases={i: 2 * ng + i for i in range(n)},
        compiler_params=pltpu.CompilerParams(has_side_effects=_ORDERED_BY_DATA),
    )(*[_in_hbm(v) for v in flat])
    sems = [(res[2 * g], res[2 * g + 1]) for g in range(ng)]
    return sems, list(res[2 * ng:-1]), res[-1]


def _wait_call(bufs, sems, copies_of, after, name):
    n = len(bufs)

    def body(*refs):
        for cp in copies_of(refs[:n], refs[n], refs[n + 1]):
            cp.wait_send()
            cp.wait_recv()

    return list(pl.pallas_call(
        body, name=name,
        out_shape=[pltpu.HBM(v.shape, v.dtype) for v in bufs],
        in_specs=[_HBM] * n + [_SEM, _SEM] + [pl.BlockSpec(memory_space=pl.ANY)] * len(after),
        out_specs=[_HBM] * n,
        input_output_aliases={i: i for i in range(n)},
        compiler_params=pltpu.CompilerParams(has_side_effects=_ORDERED_BY_DATA),
    )(*bufs, sems[0], sems[1], *after))


def _gather_copies(lands, send, recv):
    me = _index(_position())
    mine = [lambda k, ref=ref: ref.at[pl.ds(me, 1)] for ref in lands]
    return _split_copies(mine, mine, send, recv)


def _gather_arrivals(lands, send, recv):
    me = _index(_position())
    mine = [lambda k, ref=ref: ref.at[pl.ds(me, 1)] for ref in lands]
    theirs = [lambda k, ref=ref: ref.at[pl.ds(_index(_peer(k)), 1)] for ref in lands]
    return _split_copies(mine, theirs, send, recv)


def _exchange_copies(bufs, send, recv):
    half = len(bufs) // 2
    srcs = [lambda k, ref=ref: ref.at[pl.ds(_index(_peer(k)), 1)] for ref in bufs[:half]]
    dsts = [lambda k, ref=ref: ref.at[pl.ds(k - 1, 1)] for ref in bufs[half:]]
    return _split_copies(srcs, dsts, send, recv)


def gather_start(groups, name):
    return _start_call(groups, [len(grp) for grp in groups], _gather_copies, name)


def gather_wait(lands, sems, after, name):
    return _wait_call(lands, sems, _gather_arrivals, after, name)


_DIRECT = (1, 2, 4, 6)
_PASSED = (3, 5, 7)


def _remote(src, dst, send, recv, q, k):
    return pltpu.make_async_remote_copy(src_ref=src, dst_ref=dst, send_sem=send.at[q], recv_sem=recv.at[q],
                                        device_id=_peer(k), device_id_type=MESH)


def _slot_of(land, k):
    return land.at[pl.ds(_index(_peer(k)), 1)]


def relay_start(groups, name):
    def copies(lands, send, recv):
        return [_remote(_slot_of(land, 0), _slot_of(land, 0), send, recv, len(_DIRECT) * a + j, k)
                for a, land in enumerate(lands) for j, k in enumerate(_DIRECT)]
    return _start_call(groups, [len(grp) for grp in groups], copies, name, per_array=len(_DIRECT))


def relay_pass(lands, sems, after, name):
    n = len(lands)

    def body(*refs):
        bufs, recv_first = refs[:n], refs[n]
        send, recv = refs[n + 1 + len(after):n + 3 + len(after)]
        for a, land in enumerate(bufs):
            for j, k in enumerate(_PASSED):
                came = _slot_of(land, k ^ 1)
                _remote(came, came, recv_first, recv_first, len(_DIRECT) * a + _DIRECT.index(k ^ 1), k ^ 1).wait_recv()
                _remote(came, came, send, recv, len(_PASSED) * a + j, 1).start()

    res = pl.pallas_call(
        body, name=name,
        out_shape=[pltpu.SemaphoreType.DMA((len(_PASSED) * n,))] * 2 + [pltpu.HBM(v.shape, v.dtype) for v in lands],
        in_specs=[_HBM] * n + [_SEM] + [pl.BlockSpec(memory_space=pl.ANY)] * len(after),
        out_specs=[_SEM, _SEM] + [_HBM] * n,
        input_output_aliases={i: 2 + i for i in range(n)},
        compiler_params=pltpu.CompilerParams(has_side_effects=_ORDERED_BY_DATA),
    )(*lands, sems[1], *after)
    return (res[0], res[1]), list(res[2:])


def relay_wait(lands, first, second, after, name):
    n = len(lands)

    def body(*refs):
        bufs = refs[:n]
        send_first, recv_first, send, recv = refs[n:n + 4]
        for a, land in enumerate(bufs):
            mine = _slot_of(land, 0)
            _remote(mine, _slot_of(land, 1), send_first, recv_first, len(_DIRECT) * a, 1).wait_recv()
            for j, k in enumerate(_DIRECT):
                _remote(mine, mine, send_first, recv_first, len(_DIRECT) * a + j, k).wait_send()
            for j, k in enumerate(_PASSED):
                cp = _remote(_slot_of(land, k ^ 1), _slot_of(land, k), send, recv, len(_PASSED) * a + j, 1)
                cp.wait_send()
                cp.wait_recv()

    return list(pl.pallas_call(
        body, name=name,
        out_shape=[pltpu.HBM(v.shape, v.dtype) for v in lands],
        in_specs=[_HBM] * n + [_SEM] * 4 + [pl.BlockSpec(memory_space=pl.ANY)] * len(after),
        out_specs=[_HBM] * n,
        input_output_aliases={i: i for i in range(n)},
        compiler_params=pltpu.CompilerParams(has_side_effects=_ORDERED_BY_DATA),
    )(*lands, first[0], first[1], second[0], second[1], *after))


def exchange_start(partials, name):
    lands = [lax.empty((N_DEV - 1,) + p.shape[1:], p.dtype) for p in partials]
    sems, bufs, token = _start_call([list(partials) + lands], [len(partials)], _exchange_copies, name)
    return sems[0], bufs, token


def exchange_wait(bufs, sems, after, name):
    bufs = _wait_call(bufs, sems, _exchange_copies, after, name)
    return bufs[:len(bufs) // 2], bufs[len(bufs) // 2:]


def prep_weights(items, dev, name, dtype=BF16, after=()):
    n = len(items)
    blocks = [(1, w.shape[2], w.shape[1]) if t else (1,) + w.shape[1:] for w, _, t in items]

    def body(d_ref, *refs):
        for (_, _, transpose), w_ref, o_ref in zip(items, refs[:n], refs[-n:]):
            v = w_ref[0]
            o_ref[0] = (v.T if transpose else v).astype(dtype)

    return pl.pallas_call(
        body, name=name,
        grid_spec=pltpu.PrefetchScalarGridSpec(
            num_scalar_prefetch=1, grid=(1,),
            in_specs=[pl.BlockSpec((1,) + w.shape[1:], lambda i, d, layer=layer: (layer, 0, 0)) for w, layer, _ in items]
            + [pl.BlockSpec(memory_space=pl.ANY)] * len(after),
            out_specs=[pl.BlockSpec(blk, lambda i, d: (d[0], 0, 0)) for blk in blocks]),
        out_shape=[jax.ShapeDtypeStruct((N_DEV,) + blk[1:], dtype) for blk in blocks],
        compiler_params=_params("arbitrary"),
    )(dev, *[w for w, _, _ in items], *after)


def _adamw_math(w, g, m, v):
    mn = ADAM_B1 * m + (1.0 - ADAM_B1) * g
    vn = ADAM_B2 * v + (1.0 - ADAM_B2) * (g * g)
    m_hat = mn * (1.0 / (1.0 - ADAM_B1 ** ADAM_STEP))
    v_hat = vn * (1.0 / (1.0 - ADAM_B2 ** ADAM_STEP))
    return -ADAM_LR * (m_hat / (jnp.sqrt(v_hat) + ADAM_EPS) + ADAM_WD * w), mn, vn


def sum_adamw(partial, landed, w, m, v, prev, layer, transpose, dev, name):
    _, r, c = partial.shape
    if transpose:
        tc = _tile(c, 256)
        grid = (c // tc,)
        part_spec = pl.BlockSpec((1, r, tc), lambda i, d: (d[0], 0, i))
        land_spec = pl.BlockSpec((N_DEV - 1, r, tc), lambda i, d: (0, 0, i))
        w_spec = pl.BlockSpec((1, tc, r), lambda i, d: (layer, i, 0))
    else:
        tr = r // 2 if r % 32 == 0 else r
        grid = (r // tr,)
        part_spec = pl.BlockSpec((1, tr, c), lambda i, d: (d[0], i, 0))
        land_spec = pl.BlockSpec((N_DEV - 1, tr, c), lambda i, d: (0, i, 0))
        w_spec = pl.BlockSpec((1, tr, c), lambda i, d: (layer, i, 0))

    def body(d_ref, p_ref, l_ref, w_ref, m_ref, v_ref, *rest):
        g_ref, dl_ref, mo_ref, vo_ref = rest[-4:]
        acc = p_ref[0].astype(F32)
        for k in range(N_DEV - 1):
            acc = acc + l_ref[k].astype(F32)
        g = acc.T if transpose else acc
        delta, mn, vn = _adamw_math(w_ref[0], g, m_ref[0], v_ref[0])
        g_ref[0], dl_ref[0], mo_ref[0], vo_ref[0] = g, delta, mn, vn

    return pl.pallas_call(
        body, name=name,
        grid_spec=pltpu.PrefetchScalarGridSpec(
            num_scalar_prefetch=1, grid=grid,
            in_specs=[part_spec, land_spec, w_spec, w_spec, w_spec] + [pl.BlockSpec(memory_space=pl.ANY)] * 4,
            out_specs=[w_spec] * 4),
        out_shape=[jax.ShapeDtypeStruct(w.shape, F32)] * 4,
        input_output_aliases={6 + q: q for q in range(4)},
        compiler_params=_params("parallel"),
    )(dev, partial, landed, w, m, v, *prev)


def sum_parts(parts, name):
    P, r, c = parts.shape

    def body(p_ref, o_ref):
        acc = p_ref[0].astype(F32)
        for s in range(1, P):
            acc = acc + p_ref[s].astype(F32)
        o_ref[...] = acc

    return pl.pallas_call(
        body, name=name, in_specs=[_full((P, r, c))], out_specs=_full((r, c)), grid=(1,),
        out_shape=jax.ShapeDtypeStruct((r, c), F32), compiler_params=_params("arbitrary"))(parts)


def adamw(w, g, m, v, name):
    R, C = w.shape
    tr = _tile(R, 512)

    def body(w_ref, g_ref, m_ref, v_ref, d_ref, mo_ref, vo_ref):
        d_ref[...], mo_ref[...], vo_ref[...] = _adamw_math(w_ref[...], g_ref[...], m_ref[...], v_ref[...])

    spec = pl.BlockSpec((tr, C), lambda i: (i, 0))
    return pl.pallas_call(
        body, name=name, grid=(R // tr,),
        in_specs=[spec] * 4, out_specs=[spec] * 3,
        out_shape=[jax.ShapeDtypeStruct((R, C), F32)] * 3,
        compiler_params=_params("parallel"),
    )(w, g, m, v)


def sum_adamw_small(gathered, w, m, v, name):
    rows = w.shape[0]
    tr = _tile(rows, 512)

    def body(p_ref, w_ref, m_ref, v_ref, g_ref, d_ref, mo_ref, vo_ref):
        g = p_ref[0]
        for s in range(1, N_DEV):
            g = g + p_ref[s]
        g_ref[...] = g
        d_ref[...], mo_ref[...], vo_ref[...] = _adamw_math(w_ref[...], g, m_ref[...], v_ref[...])

    spec = pl.BlockSpec((tr, LANES), lambda i: (i, 0))
    return pl.pallas_call(
        body, name=name, grid=(rows // tr,),
        in_specs=[pl.BlockSpec((N_DEV, tr, LANES), lambda i: (0, i, 0)), spec, spec, spec], out_specs=[spec] * 4,
        out_shape=[jax.ShapeDtypeStruct((rows, LANES), F32)] * 4, compiler_params=_params("parallel"))(gathered, w, m, v)


def first_norm(h, g, name):
    T, D = h.shape
    tm = _tile(T, 512)

    def body(h_ref, g_ref, hn_ref):
        hn_ref[...] = _rms(h_ref[...], g_ref[...]).astype(BF16)

    row = _rows(tm, D)
    return pl.pallas_call(
        body, name=name, grid=(T // tm,), in_specs=[row, _full((1, D))], out_specs=row,
        out_shape=jax.ShapeDtypeStruct((T, D), BF16), compiler_params=_params("parallel"))(h, g)


_NT = (((1,), (1,)), ((), ()))
_TN = (((0,), (0,)), ((), ()))


def rowmm(a, w, transposed_w, bias, out_dtype, name, after=None):
    M, K = a.shape
    N = w.shape[0] if transposed_w else w.shape[1]
    tm = _tile(M, 512)
    tn = _tile(N, 512)

    def body(*refs):
        a_ref, w_ref = refs[:2]
        o_ref = refs[-1]
        av = a_ref[...]
        for j in range(N // tn):
            cols = pl.ds(j * tn, tn)
            if transposed_w:
                acc = lax.dot_general(av, w_ref[cols, :], _NT, preferred_element_type=F32)
            else:
                acc = jnp.dot(av, w_ref[:, cols], preferred_element_type=F32)
            if bias is not None:
                acc = acc + refs[2][:, cols]
            o_ref[:, cols] = acc.astype(out_dtype)

    ins, in_specs = [a, w], [_rows(tm, K), _resident(w.shape)]
    if bias is not None:
        ins.append(bias)
        in_specs.append(_full((1, N)))
    if after is not None:
        ins.append(after)
        in_specs.append(pl.BlockSpec(memory_space=pl.ANY))
    return pl.pallas_call(
        body, name=name, grid=(M // tm,), in_specs=in_specs, out_specs=_rows(tm, N),
        out_shape=jax.ShapeDtypeStruct((M, N), out_dtype), compiler_params=_params("parallel"))(*ins)


def _rows_times_weight(a_ref, w_ref, acc_ref):
    N = w_ref.shape[1]
    tn = _tile(N, 512)
    av = a_ref[...]
    for j in range(N // tn):
        cols = pl.ds(j * tn, tn)
        acc_ref[:, cols] = jnp.dot(av, w_ref[:, cols], preferred_element_type=F32)


def mm_add_norm(a, w, h, g_post, g_pre, name):
    T, K = a.shape
    D = w.shape[1]
    tm = _tile(T, 512)

    def body(a_ref, w_ref, h_ref, gp_ref, g_ref, y_ref, ho_ref, hn_ref, acc_ref):
        _rows_times_weight(a_ref, w_ref, acc_ref)
        y = acc_ref[...]
        y_ref[...] = y.astype(BF16)
        hv = h_ref[...] + _rms(y, gp_ref[...])
        ho_ref[...] = hv
        hn_ref[...] = _rms(hv, g_ref[...]).astype(BF16)

    row, vec = _rows(tm, D), _full((1, D))
    return pl.pallas_call(
        body, name=name, grid=(T // tm,), in_specs=[_rows(tm, K), _resident(w.shape), row, vec, vec],
        out_specs=[row, row, row],
        out_shape=[jax.ShapeDtypeStruct((T, D), BF16), jax.ShapeDtypeStruct((T, D), F32),
                   jax.ShapeDtypeStruct((T, D), BF16)],
        scratch_shapes=[pltpu.VMEM((tm, D), F32)], compiler_params=_params("parallel"))(a, w, h, g_post, g_pre)


def mm_add_norm_loss(a, w, h, g_post, target, name):
    T, K = a.shape
    D = w.shape[1]
    tm = _tile(T, 512)

    def body(a_ref, w_ref, h_ref, gp_ref, t_ref, dh_ref, dy_ref, loss_ref, dgp_ref, acc_ref):
        @pl.when(pl.program_id(0) == 0)
        def _():
            loss_ref[...] = jnp.zeros_like(loss_ref)
            dgp_ref[...] = jnp.zeros_like(dgp_ref)
        _rows_times_weight(a_ref, w_ref, acc_ref)
        y = acc_ref[...]
        err = h_ref[...] + _rms(y, gp_ref[...]) - t_ref[...]
        dh = err * (1.0 / D)
        dh_ref[...] = dh
        loss_ref[...] += jnp.sum(err * err)
        dy, dgp = _rms_bwd(y, gp_ref[...], dh)
        dy_ref[...] = dy.astype(BF16)
        dgp_ref[...] += dgp

    row, vec = _rows(tm, D), _full((1, D))
    return pl.pallas_call(
        body, name=name, grid=(T // tm,), in_specs=[_rows(tm, K), _resident(w.shape), row, vec, row],
        out_specs=[row, row, _full((8, LANES)), vec],
        out_shape=[jax.ShapeDtypeStruct((T, D), F32), jax.ShapeDtypeStruct((T, D), BF16),
                   jax.ShapeDtypeStruct((8, LANES), F32), jax.ShapeDtypeStruct((1, D), F32)],
        scratch_shapes=[pltpu.VMEM((tm, D), F32)], compiler_params=_params("arbitrary"))(a, w, h, g_post, target)


def mm_norm_bwd(a, w, dh_in, x, g_pre, post, name, after):
    T, K = a.shape
    D = w.shape[1]
    tm = _tile(T, 512)
    row, vec = _rows(tm, D), _full((1, D))
    ins, in_specs = [a, w, dh_in, x, g_pre], [_rows(tm, K), _resident(w.shape), row, row, vec]
    outs = [jax.ShapeDtypeStruct((T, D), F32), jax.ShapeDtypeStruct((1, D), F32)]
    out_specs, names = [row, vec], ["dh", "dg_pre"]
    if post is not None:
        ins += list(post)
        in_specs += [row, vec]
        outs += [jax.ShapeDtypeStruct((T, D), BF16), jax.ShapeDtypeStruct((1, D), F32)]
        out_specs += [row, vec]
        names += ["dy", "dg_post"]
    n_in = len(ins) + 1

    def body(*refs):
        a_ref, w_ref, dh_ref_in, x_ref, g_ref = refs[:5]
        out_refs, acc_ref = refs[n_in:-1], refs[-1]
        first = pl.program_id(0) == 0
        _rows_times_weight(a_ref, w_ref, acc_ref)
        dx, dg = _rms_bwd(x_ref[...], g_ref[...], acc_ref[...])
        dh = dh_ref_in[...] + dx
        out_refs[0][...] = dh

        @pl.when(first)
        def _():
            for r in out_refs[1::2]:
                r[...] = jnp.zeros_like(r)
        out_refs[1][...] += dg
        if post is not None:
            y_ref, gp_ref = refs[5:7]
            dy, dgp = _rms_bwd(y_ref[...].astype(F32), gp_ref[...], dh)
            out_refs[2][...] = dy.astype(BF16)
            out_refs[3][...] += dgp

    res = pl.pallas_call(
        body, name=name, grid=(T // tm,), in_specs=in_specs + [pl.BlockSpec(memory_space=pl.ANY)],
        out_specs=out_specs, out_shape=outs, scratch_shapes=[pltpu.VMEM((tm, D), F32)],
        compiler_params=_params("arbitrary"))(*ins, after)
    return dict(zip(names, res))


def grad_mm(a, b, name, after=()):
    T, N = a.shape
    K = b.shape[1]
    tn = _tile(N, 1408)
    tt = _tile(T, 1024)

    def body(a_ref, b_ref, *rest):
        o_ref, acc_ref = rest[-2:]
        t = pl.program_id(1)

        @pl.when(t == 0)
        def _():
            acc_ref[...] = jnp.zeros_like(acc_ref)
        acc_ref[...] += lax.dot_general(a_ref[...], b_ref[...], _TN, preferred_element_type=F32)

        @pl.when(t == pl.num_programs(1) - 1)
        def _():
            o_ref[...] = acc_ref[...].astype(BF16)

    return pl.pallas_call(
        body, name=name, grid=(N // tn, T // tt),
        in_specs=[pl.BlockSpec((tt, tn), lambda j, t: (t, j)), pl.BlockSpec((tt, K), lambda j, t: (t, 0))]
        + [pl.BlockSpec(memory_space=pl.ANY)] * len(after),
        out_specs=pl.BlockSpec((tn, K), lambda j, t: (j, 0)),
        out_shape=jax.ShapeDtypeStruct((N, K), BF16),
        scratch_shapes=[pltpu.VMEM((tn, K), F32)],
        compiler_params=_params("parallel", "arbitrary"))(a, b, *after)


def ffn_up(fn, w_gu_t, name):
    T, D = fn.shape
    F = w_gu_t.shape[0] // 2
    tm = _tile(T, 512)
    tn = _tile(F, 512)

    def body(a_ref, w_ref, act_ref, part_ref):
        av = a_ref[...]
        for j in range(F // tn):
            g = lax.dot_general(av, w_ref[pl.ds(j * tn, tn), :], _NT, preferred_element_type=F32)
            up = lax.dot_general(av, w_ref[pl.ds(F + j * tn, tn), :], _NT, preferred_element_type=F32)
            sg = _sigmoid(g)
            silu = g * sg
            act_ref[:, pl.ds(j * tn, tn)] = (silu * up).astype(BF16)
            part_ref[:, pl.ds(j * tn, tn)] = (up * (sg + silu * (1.0 - sg))).astype(BF16)
            part_ref[:, pl.ds(F + j * tn, tn)] = silu.astype(BF16)

    return pl.pallas_call(
        body, name=name, grid=(T // tm,), in_specs=[_rows(tm, D), _resident(w_gu_t.shape)],
        out_specs=[_rows(tm, F), _rows(tm, 2 * F)],
        out_shape=[jax.ShapeDtypeStruct((T, F), BF16), jax.ShapeDtypeStruct((T, 2 * F), BF16)],
        compiler_params=_params("parallel"))(fn, w_gu_t)


def ffn_dact(df, w_down, partials, name, after=()):
    T, D = df.shape
    F = w_down.shape[0]
    tm = _tile(T, 512)
    tn = _tile(F, 512)

    def body(d_ref, w_ref, part_ref, *rest):
        o_ref = rest[-1]
        dv = d_ref[...]
        for j in range(F // tn):
            dact = lax.dot_general(dv, w_ref[pl.ds(j * tn, tn), :], _NT, preferred_element_type=F32)
            for cols in (pl.ds(j * tn, tn), pl.ds(F + j * tn, tn)):
                o_ref[:, cols] = (dact * part_ref[:, cols].astype(F32)).astype(BF16)

    return pl.pallas_call(
        body, name=name, grid=(T // tm,),
        in_specs=[_rows(tm, D), _resident(w_down.shape), _rows(tm, 2 * F)] + [pl.BlockSpec(memory_space=pl.ANY)] * len(after),
        out_specs=_rows(tm, 2 * F), out_shape=jax.ShapeDtypeStruct((T, 2 * F), BF16),
        compiler_params=_params("parallel"))(df, w_down, partials, *after)


def _layernorm_stats(v):
    mu = jnp.mean(v, axis=-1, keepdims=True)
    cen = v - mu
    rstd = lax.rsqrt(jnp.mean(cen * cen, axis=-1, keepdims=True) + LN_EPS)
    return cen * rstd, rstd


GMLP_ROWS = 2 * CHUNK


def gmlp_gate_out(a, ln_g, ln_b, w_mask, b_s_t, w_out, h, g_post, g_pre, name):
    T, W2 = a.shape
    W = W2 // 2
    G = w_mask.shape[0]
    C = W // G
    D = w_out.shape[1]
    tm = min(T, GMLP_ROWS)

    def body(a_ref, lg_ref, lb_ref, w_ref, bs_ref, wo_ref, h_ref, gp_ref, g_ref,
             gated_ref, y_ref, ho_ref, hn_ref, acc_ref):
        for c in range(tm // CHUNK):
            rows = pl.ds(c * CHUNK, CHUNK)
            xhat, _ = _layernorm_stats(_gelu(a_ref[rows, W:].astype(F32)))
            vln = (xhat * lg_ref[...] + lb_ref[...]).astype(BF16)
            for g in range(G):
                cols = pl.ds(g * C, C)
                sv = jnp.dot(w_ref[g], vln[:, g * C:(g + 1) * C], preferred_element_type=F32) + bs_ref[:, g:g + 1]
                gated_ref[rows, cols] = (_gelu(a_ref[rows, cols].astype(F32)) * sv).astype(BF16)
        _rows_times_weight(gated_ref, wo_ref, acc_ref)
        y = acc_ref[...]
        y_ref[...] = y.astype(BF16)
        hv = h_ref[...] + _rms(y, gp_ref[...])
        ho_ref[...] = hv
        hn_ref[...] = _rms(hv, g_ref[...]).astype(BF16)

    row, vec = _rows(tm, D), _full((1, D))
    return pl.pallas_call(
        body, name=name, grid=(T // tm,),
        in_specs=[_rows(tm, W2), _full((1, W)), _full((1, W)), _full(w_mask.shape), _full(b_s_t.shape),
                  _resident(w_out.shape), row, vec, vec],
        out_specs=[_rows(tm, W), row, row, row],
        out_shape=[jax.ShapeDtypeStruct((T, W), BF16), jax.ShapeDtypeStruct((T, D), BF16),
                   jax.ShapeDtypeStruct((T, D), F32), jax.ShapeDtypeStruct((T, D), BF16)],
        scratch_shapes=[pltpu.VMEM((tm, D), F32)],
        compiler_params=_params("parallel"))(a, ln_g, ln_b, w_mask, b_s_t, w_out, h, g_post, g_pre)


def gmlp_gate_bwd_dx(a, dmix, w_out, ln_g, ln_b, w_mask, w_mask_t, b_s_t, group_onehot, w_in_t,
                     dh_in, x, g_pre, post, name, after):
    T, W2 = a.shape
    W = W2 // 2
    G = w_mask.shape[0]
    C = W // G
    D = w_out.shape[1]
    tm = min(T, GMLP_ROWS)
    tn = _tile(W, 512)
    row, vec = _rows(tm, D), _full((1, D))
    ins = [a, dmix, w_out, ln_g, ln_b, w_mask, w_mask_t, b_s_t, group_onehot, w_in_t, dh_in, x, g_pre]
    in_specs = [_rows(tm, W2), row, _resident(w_out.shape), _full((1, W)), _full((1, W)), _full(w_mask.shape),
                _full(w_mask_t.shape), _full(b_s_t.shape), _full(group_onehot.shape), _resident(w_in_t.shape), row, row, vec]
    names = ["da", "dw_s", "db_s_t", "dln_g", "dln_b", "db_in", "dh", "dg_pre"]
    outs = [jax.ShapeDtypeStruct((T, W2), BF16), jax.ShapeDtypeStruct((G, CHUNK, CHUNK), F32),
            jax.ShapeDtypeStruct((CHUNK, LANES), F32), jax.ShapeDtypeStruct((1, W), F32),
            jax.ShapeDtypeStruct((1, W), F32), jax.ShapeDtypeStruct((1, W2), F32),
            jax.ShapeDtypeStruct((T, D), F32), jax.ShapeDtypeStruct((1, D), F32)]
    out_specs = [_rows(tm, W2), _full((G, CHUNK, CHUNK)), _full((CHUNK, LANES)), _full((1, W)), _full((1, W)),
                 _full((1, W2)), row, vec]
    if post is not None:
        ins += list(post)
        in_specs += [row, vec]
        names += ["dy", "dg_post"]
        outs += [jax.ShapeDtypeStruct((T, D), BF16), jax.ShapeDtypeStruct((1, D), F32)]
        out_specs += [row, vec]
    n_in = len(ins) + 1

    def body(*refs):
        (a_ref, dm_ref, wo_ref, lg_ref, lb_ref, w_ref, wt_ref, bs_ref, e_ref, wi_ref, dh_in_ref, x_ref, gpre_ref) = refs[:13]
        out_refs = refs[n_in:-3]
        da_ref, dws_ref, dbs_ref, dlg_ref, dlb_ref, dbin_ref, dh_ref, dgpre_ref = out_refs[:8]
        dgated_ref, dvln_ref, acc_ref = refs[-3:]

        @pl.when(pl.program_id(0) == 0)
        def _():
            for r in out_refs[1:6] + out_refs[7::2]:
                r[...] = jnp.zeros_like(r)
        dm = dm_ref[...]
        for j in range(W // tn):
            cols = pl.ds(j * tn, tn)
            dgated_ref[:, cols] = lax.dot_general(dm, wo_ref[cols, :], _NT, preferred_element_type=F32)
        causal = (lax.broadcasted_iota(jnp.int32, (CHUNK, CHUNK), 1)
                  <= lax.broadcasted_iota(jnp.int32, (CHUNK, CHUNK), 0))
        for c in range(tm // CHUNK):
            rows = pl.ds(c * CHUNK, CHUNK)
            gelu_v, gelu_grad_v = _gelu_and_grad(a_ref[rows, W:].astype(F32))
            xhat, rstd = _layernorm_stats(gelu_v)
            vln = (xhat * lg_ref[...] + lb_ref[...]).astype(BF16)
            for g in range(G):
                cols = pl.ds(g * C, C)
                vg = vln[:, g * C:(g + 1) * C]
                gelu_u, gelu_grad_u = _gelu_and_grad(a_ref[rows, cols].astype(F32))
                dgated = dgated_ref[rows, cols]
                dsv = (dgated * gelu_u).astype(BF16)
                dbs_ref[...] += jnp.dot(dsv, e_ref[cols, :], preferred_element_type=F32)
                sv = jnp.dot(w_ref[g], vg, preferred_element_type=F32) + bs_ref[:, g:g + 1]
                dau = dgated * sv * gelu_grad_u
                da_ref[rows, cols] = dau.astype(BF16)
                dbin_ref[:, cols] += jnp.sum(dau, axis=0, keepdims=True)
                dws_ref[g] += jnp.where(causal, lax.dot_general(dsv, vg, _NT, preferred_element_type=F32), 0.0)
                dvln_ref[:, cols] = jnp.dot(wt_ref[g], dsv, preferred_element_type=F32)
            dvln = dvln_ref[...]
            dlg_ref[...] += jnp.sum(dvln * xhat, axis=0, keepdims=True)
            dlb_ref[...] += jnp.sum(dvln, axis=0, keepdims=True)
            dxhat = dvln * lg_ref[...]
            dv = rstd * (dxhat - jnp.mean(dxhat, axis=-1, keepdims=True)
                         - xhat * jnp.mean(dxhat * xhat, axis=-1, keepdims=True))
            dav = dv * gelu_grad_v
            da_ref[rows, W:] = dav.astype(BF16)
            dbin_ref[:, W:] += jnp.sum(dav, axis=0, keepdims=True)
        _rows_times_weight(da_ref, wi_ref, acc_ref)
        dx, dg = _rms_bwd(x_ref[...], gpre_ref[...], acc_ref[...])
        dh = dh_in_ref[...] + dx
        dh_ref[...] = dh
        dgpre_ref[...] += dg
        if post is not None:
            y_ref, gp_ref = refs[13:15]
            dy, dgp = _rms_bwd(y_ref[...].astype(F32), gp_ref[...], dh)
            out_refs[8][...] = dy.astype(BF16)
            out_refs[9][...] += dgp

    res = pl.pallas_call(
        body, name=name, grid=(T // tm,), in_specs=in_specs + [pl.BlockSpec(memory_space=pl.ANY)],
        out_specs=out_specs, out_shape=outs,
        scratch_shapes=[pltpu.VMEM((tm, W), F32), pltpu.VMEM((CHUNK, W), F32), pltpu.VMEM((tm, D), F32)],
        compiler_params=_params("arbitrary"))(*ins, after)
    return dict(zip(names, res))


def rope_tables(pos, inv_freq_row, name):
    T = pos.shape[0]
    tm = _tile(T, 512)

    def body(p_ref, f_ref, c_ref, s1_ref, s2_ref):
        ang = p_ref[...].astype(F32) * f_ref[...]
        lane = lax.broadcasted_iota(jnp.int32, (tm, LANES), 1) % HEAD_DIM
        sin = jnp.sin(ang)
        c_ref[...] = jnp.cos(ang)
        s1_ref[...] = jnp.where(lane < _HALF, -sin, 0.0)
        s2_ref[...] = jnp.where((lane >= _HALF) & (lane < ROPE_DIM), sin, 0.0)

    tab = _rows(tm, LANES)
    return pl.pallas_call(
        body, name=name, grid=(T // tm,), in_specs=[_rows(tm, 1), _full((1, LANES))], out_specs=[tab] * 3,
        out_shape=[jax.ShapeDtypeStruct((T, LANES), F32)] * 3, compiler_params=_params("parallel"))(pos, inv_freq_row)


_HALF = ROPE_DIM // 2


def _slabs(x):
    return [x[:, b * LANES:(b + 1) * LANES] for b in range(x.shape[1] // LANES)]


def _rotate(x, c, s1, s2):
    return [xs * c + pltpu.roll(xs, LANES - _HALF, 1) * s1 + pltpu.roll(xs, _HALF, 1) * s2 for xs in _slabs(x)]


def _rotate_transposed(dy, c, s1, s2):
    return [ds * c + pltpu.roll(ds * s1, _HALF, 1) + pltpu.roll(ds * s2, LANES - _HALF, 1) for ds in _slabs(dy)]


def rope_fwd(qkv, tabs, q_width, kv_width, name):
    T, QKV = qkv.shape
    tm = _tile(T, 512)
    QK = q_width + kv_width
    scale = HEAD_DIM ** -0.5

    def body(x_ref, c_ref, s1_ref, s2_ref, o_ref):
        x = x_ref[:, :QK].astype(F32)
        slabs = _rotate(x, c_ref[...], s1_ref[...], s2_ref[...])
        for b, y in enumerate(slabs):
            if b * LANES < q_width:
                y = y * scale
            o_ref[:, b * LANES:(b + 1) * LANES] = y.astype(BF16)

    tab = _rows(tm, LANES)
    return pl.pallas_call(
        body, name=name, grid=(T // tm,), in_specs=[_rows(tm, QKV), tab, tab, tab], out_specs=_rows(tm, QK),
        out_shape=jax.ShapeDtypeStruct((T, QK), BF16), compiler_params=_params("parallel"))(qkv, *tabs)


def rope_bwd(dq, dk, dv, tabs, name):
    T, Q = dq.shape
    KV = dk.shape[1]
    tm = _tile(T, 512)
    scale = HEAD_DIM ** -0.5

    def body(dq_ref, dk_ref, dv_ref, c_ref, s1_ref, s2_ref, o_ref, b_ref):
        @pl.when(pl.program_id(0) == 0)
        def _():
            b_ref[...] = jnp.zeros_like(b_ref)
        tabs_v = (c_ref[...], s1_ref[...], s2_ref[...])
        pieces = [s * scale for s in _rotate_transposed(dq_ref[...], *tabs_v)]
        pieces += _rotate_transposed(dk_ref[...], *tabs_v)
        pieces += _slabs(dv_ref[...])
        for b, y in enumerate(pieces):
            cols = pl.ds(b * LANES, LANES)
            o_ref[:, cols] = y.astype(BF16)
            b_ref[:, cols] += jnp.sum(y, axis=0, keepdims=True)

    tab = _rows(tm, LANES)
    return pl.pallas_call(
        body, name=name, grid=(T // tm,), in_specs=[_rows(tm, Q), _rows(tm, KV), _rows(tm, KV), tab, tab, tab],
        out_specs=[_rows(tm, Q + 2 * KV), _full((1, Q + 2 * KV))],
        out_shape=[jax.ShapeDtypeStruct((T, Q + 2 * KV), BF16), jax.ShapeDtypeStruct((1, Q + 2 * KV), F32)],
        compiler_params=_params("arbitrary"))(dq, dk, dv, *tabs)


def _band_mask(n, heads=1):
    qi = lax.broadcasted_iota(jnp.int32, (heads * CHUNK, 2 * CHUNK), 0) % CHUNK
    sj = lax.broadcasted_iota(jnp.int32, (heads * CHUNK, 2 * CHUNK), 1)
    return (sj > qi) & (sj <= qi + CHUNK) & ((n > 0) | (sj >= CHUNK))


def _kv_head(kc_ref, kp_ref, vc_ref, vp_ref, kh):
    lanes = slice(kh * HEAD_DIM, (kh + 1) * HEAD_DIM)
    return (jnp.concatenate([kp_ref[:, lanes], kc_ref[:, lanes]], axis=0),
            jnp.concatenate([vp_ref[:, lanes], vc_ref[:, lanes]], axis=0))


def _row_sums(x):
    return jnp.dot(x, jnp.ones((x.shape[1], LANES), BF16), preferred_element_type=F32)


def _lanes(v, n):
    return v[:, :n] if n <= LANES else jnp.concatenate([v] * (n // LANES), axis=-1)


def _head_probs(q, kk, valid, sink, sums_on_mxu):
    s = jnp.where(valid, lax.dot_general(q, kk, _NT, preferred_element_type=F32), NEG_INF)
    m = jnp.maximum(jnp.max(s, axis=-1, keepdims=True), sink)
    p = jnp.exp(s - m)
    e_sink = jnp.exp(sink - m)
    total = _row_sums(p.astype(BF16)) if sums_on_mxu else jnp.sum(p, axis=-1, keepdims=True)
    return p.astype(BF16), 1.0 / (total + e_sink), e_sink


def _attn_specs(q_width, kv_width, order):
    qb = q_width // kv_width
    prev = lambda i: jnp.maximum(order(i) - 1, 0)
    return [pl.BlockSpec((CHUNK, q_width), lambda i: (order(i), 0)),
            pl.BlockSpec((CHUNK, kv_width), lambda i: (order(i), qb)),
            pl.BlockSpec((CHUNK, kv_width), lambda i: (prev(i), qb)),
            pl.BlockSpec((CHUNK, kv_width), lambda i: (order(i), qb + 1)),
            pl.BlockSpec((CHUNK, kv_width), lambda i: (prev(i), qb + 1))]


def attn_fwd(qk, qkv, sinks, q_width, kv_width, name):
    T = qk.shape[0]
    group = q_width // kv_width

    def body(q_ref, kc_ref, kp_ref, vc_ref, vp_ref, sink_ref, o_ref):
        valid = _band_mask(pl.program_id(0))
        for kh in range(kv_width // HEAD_DIM):
            kk, vv = _kv_head(kc_ref, kp_ref, vc_ref, vp_ref, kh)
            for g in range(group):
                h = kh * group + g
                lanes = slice(h * HEAD_DIM, (h + 1) * HEAD_DIM)
                p, inv, _ = _head_probs(q_ref[:, lanes], kk, valid, sink_ref[h], False)
                o_ref[:, lanes] = (jnp.dot(p, vv, preferred_element_type=F32) * inv).astype(BF16)

    specs = _attn_specs(q_width, kv_width, lambda i: i)
    return pl.pallas_call(
        body, name=name, grid=(T // CHUNK,),
        in_specs=specs + [pl.BlockSpec(memory_space=pltpu.SMEM)],
        out_specs=_rows(CHUNK, q_width), out_shape=jax.ShapeDtypeStruct((T, q_width), BF16),
        compiler_params=_params("parallel"))(qk, qk, qk, qkv, qkv, sinks)


def attn_bwd(qk, qkv, do, sinks, q_width, kv_width, name):
    T = qk.shape[0]
    NB = T // CHUNK
    group = q_width // kv_width

    def body(q_ref, kc_ref, kp_ref, vc_ref, vp_ref, do_ref, sink_ref, dq_ref, dk_ref, dv_ref, ds_ref, ck_ref, cv_ref):
        i = pl.program_id(0)
        n = NB - 1 - i

        @pl.when(i == 0)
        def _():
            ck_ref[...] = jnp.zeros_like(ck_ref)
            cv_ref[...] = jnp.zeros_like(cv_ref)
            ds_ref[...] = jnp.zeros_like(ds_ref)
        lane = lax.broadcasted_iota(jnp.int32, (1, LANES), 1)
        dsink_row = jnp.zeros((1, LANES), F32)
        valid = _band_mask(n, group)
        head = lax.broadcasted_iota(jnp.int32, (group * CHUNK, 1), 0) // CHUNK
        for kh in range(kv_width // HEAD_DIM):
            kk, vv = _kv_head(kc_ref, kp_ref, vc_ref, vp_ref, kh)
            heads = [slice((kh * group + g) * HEAD_DIM, (kh * group + g + 1) * HEAD_DIM) for g in range(group)]
            q = jnp.concatenate([q_ref[:, hs] for hs in heads], axis=0)
            do = jnp.concatenate([do_ref[:, hs] for hs in heads], axis=0)
            sink = jnp.zeros((group * CHUNK, 1), F32)
            for g in range(group):
                sink = jnp.where(head == g, sink_ref[kh * group + g], sink)
            p, inv, e_sink = _head_probs(q, kk, valid, sink, True)
            p = p.astype(F32) * _lanes(inv, 2 * CHUNK)
            dp = lax.dot_general(do, vv, _NT, preferred_element_type=F32)
            delta = _row_sums((p * dp).astype(BF16))
            ds = (p * (dp - _lanes(delta, 2 * CHUNK))).astype(BF16)
            dsink = -e_sink * inv[:, :1] * delta[:, :1]
            dq = jnp.dot(ds, kk, preferred_element_type=F32)
            for g, hs in enumerate(heads):
                rows = slice(g * CHUNK, (g + 1) * CHUNK)
                dsink_row = dsink_row + jnp.where(lane == kh * group + g, jnp.sum(dsink[rows]), 0.0)
                dq_ref[:, hs] = dq[rows]
            dkk = lax.dot_general(ds, q, _TN, preferred_element_type=F32)
            dvv = lax.dot_general(p.astype(BF16), do, _TN, preferred_element_type=F32)
            lanes = slice(kh * HEAD_DIM, (kh + 1) * HEAD_DIM)
            dk_ref[:, lanes] = dkk[CHUNK:] + ck_ref[:, lanes]
            dv_ref[:, lanes] = dvv[CHUNK:] + cv_ref[:, lanes]
            ck_ref[:, lanes] = dkk[:CHUNK]
            cv_ref[:, lanes] = dvv[:CHUNK]
        ds_ref[0:1, :] += dsink_row

    order = lambda i: NB - 1 - i
    specs = _attn_specs(q_width, kv_width, order)
    kv_out = pl.BlockSpec((CHUNK, kv_width), lambda i: (order(i), 0))
    q_rows = pl.BlockSpec((CHUNK, q_width), lambda i: (order(i), 0))
    return pl.pallas_call(
        body, name=name, grid=(NB,),
        in_specs=specs + [q_rows, pl.BlockSpec(memory_space=pltpu.SMEM)],
        out_specs=[q_rows, kv_out, kv_out, _full((8, LANES))],
        out_shape=[jax.ShapeDtypeStruct((T, q_width), F32), jax.ShapeDtypeStruct((T, kv_width), F32),
                   jax.ShapeDtypeStruct((T, kv_width), F32), jax.ShapeDtypeStruct((8, LANES), F32)],
        scratch_shapes=[pltpu.VMEM((CHUNK, kv_width), F32), pltpu.VMEM((CHUNK, kv_width), F32)],
        compiler_params=_params("arbitrary"))(qk, qk, qk, qkv, qkv, do, sinks)


def _blocked(w):
    return w.reshape(N_DEV, w.shape[0] // N_DEV, w.shape[1])


def kernel(x, positions, pre_mix_g, post_mix_g, pre_ffn_g, post_ffn_g, a_w_in, a_b_in, a_ln_g, a_ln_b, a_w_s, a_b_s, a_w_out, b_w_qkv, b_b_qkv, b_sinks, b_w_o, ffn_w_gu, ffn_w_down, loss_target, m_pre_mix_g, m_post_mix_g, m_pre_ffn_g, m_post_ffn_g, m_a_w_in, m_a_b_in, m_a_ln_g, m_a_ln_b, m_a_w_s, m_a_b_s, m_a_w_out, m_b_w_qkv, m_b_b_qkv, m_b_sinks, m_b_w_o, m_ffn_w_gu, m_ffn_w_down, v_pre_mix_g, v_post_mix_g, v_pre_ffn_g, v_post_ffn_g, v_a_w_in, v_a_b_in, v_a_ln_g, v_a_ln_b, v_a_w_s, v_a_b_s, v_a_w_out, v_b_w_qkv, v_b_b_qkv, v_b_sinks, v_b_w_o, v_ffn_w_gu, v_ffn_w_down):
    weights = dict(pre_mix_g=pre_mix_g, post_mix_g=post_mix_g, pre_ffn_g=pre_ffn_g, post_ffn_g=post_ffn_g,
                   a_w_in=a_w_in, a_b_in=a_b_in, a_ln_g=a_ln_g, a_ln_b=a_ln_b, a_w_s=a_w_s, a_b_s=a_b_s,
                   a_w_out=a_w_out, b_w_qkv=b_w_qkv, b_b_qkv=b_b_qkv, b_sinks=b_sinks, b_w_o=b_w_o,
                   ffn_w_gu=ffn_w_gu, ffn_w_down=ffn_w_down)
    mom_m = dict(pre_mix_g=m_pre_mix_g, post_mix_g=m_post_mix_g, pre_ffn_g=m_pre_ffn_g, post_ffn_g=m_post_ffn_g,
                 a_w_in=m_a_w_in, a_b_in=m_a_b_in, a_ln_g=m_a_ln_g, a_ln_b=m_a_ln_b, a_w_s=m_a_w_s, a_b_s=m_a_b_s,
                 a_w_out=m_a_w_out, b_w_qkv=m_b_w_qkv, b_b_qkv=m_b_b_qkv, b_sinks=m_b_sinks, b_w_o=m_b_w_o,
                 ffn_w_gu=m_ffn_w_gu, ffn_w_down=m_ffn_w_down)
    mom_v = dict(pre_mix_g=v_pre_mix_g, post_mix_g=v_post_mix_g, pre_ffn_g=v_pre_ffn_g, post_ffn_g=v_post_ffn_g,
                 a_w_in=v_a_w_in, a_b_in=v_a_b_in, a_ln_g=v_a_ln_g, a_ln_b=v_a_ln_b, a_w_s=v_a_w_s, a_b_s=v_a_b_s,
                 a_w_out=v_a_w_out, b_w_qkv=v_b_w_qkv, b_b_qkv=v_b_b_qkv, b_sinks=v_b_sinks, b_w_o=v_b_w_o,
                 ffn_w_gu=v_ffn_w_gu, ffn_w_down=v_ffn_w_down)
    names = list(weights)
    big = ("a_w_in", "b_w_qkv", "ffn_w_gu", "a_w_out", "b_w_o", "ffn_w_down")

    T, D = x.shape[1], x.shape[2]
    depth = pre_mix_g.shape[0]
    n_heads = b_sinks.shape[1]
    q_width = n_heads * HEAD_DIM
    kv_width = N_KV_HEADS * HEAD_DIM
    G = a_w_s.shape[1]
    W = a_ln_g.shape[1]
    device = _index(_position())

    dev = device.reshape(1).astype(jnp.int32)

    def layer_keys(i):
        mixer = ("a_w_in", "a_w_out") if i % 2 == 0 else ("b_w_qkv", "b_w_o")
        return [(k, i // 2) for k in mixer] + [("ffn_w_gu", i), ("ffn_w_down", i)]

    as_view = ("b_w_qkv", "ffn_w_gu")
    in_kernel = ("a_w_in",)
    view = lambda k, t: jnp.swapaxes(t, 1, 2) if k in as_view else t
    w_view = {k: view(k, weights[k]) for k in big}
    m_view = {k: view(k, mom_m[k]) for k in big}
    v_view = {k: view(k, mom_v[k]) for k in big}

    groups_keys = [[key] for key in layer_keys(0)] + [layer_keys(i) for i in range(1, depth)]
    group_of = {}
    for g, keys in enumerate(groups_keys):
        layer = 0 if g < len(layer_keys(0)) else g - len(layer_keys(0)) + 1
        for pos, (k, _) in enumerate(keys):
            group_of[(layer, k)] = (g, pos)
    bias_land = lax.dynamic_update_slice(
        jnp.zeros((N_DEV,) + b_b_qkv.shape, F32), b_b_qkv[None], (device, 0, 0))
    h, hn = x[0], first_norm(x[0], pre_mix_g[0][None], "norm_first")
    first = len(layer_keys(0))
    in_gather, token = [], ()
    for name, lo, hi in (("gather_start_first", 0, first), ("gather_start_rest", first, len(groups_keys))):
        groups = [list(prep_weights([(w_view[k], l, k in in_kernel) for k, l in keys], dev, f"prep_{name}_{g}",
                                    after=token)) for g, keys in enumerate(groups_keys[lo:hi])]
        if lo:
            groups[0].append(bias_land)
        gather_sems, flat_lands, token = relay_start(groups, name)
        token = (token, hn)
        for grp, sems in zip(groups, gather_sems):
            in_gather.append(([flat_lands.pop(0) for _ in grp], sems))
    all_started = token[0]
    passed, gathered = {}, {}

    def pass_on(g, after):
        if g not in passed and g < len(in_gather):
            lands, sems = in_gather[g]
            passed[g], lands = relay_pass(lands, sems, after, f"gather_pass_{g}")
            in_gather[g] = (lands, sems)

    pass_on(0, (all_started,))

    def weight(i, k, *after, ahead=True):
        g, pos = group_of[(i, k)]
        if g not in gathered:
            if ahead:
                pass_on(g + 1, after)
            lands, sems = in_gather[g]
            gathered[g] = relay_wait(lands, sems, passed[g], after, f"gather_wait_{g}")
        w = gathered[g][pos]
        return w.reshape(N_DEV * w.shape[1], w.shape[2])

    causal = jnp.tril(jnp.ones((CHUNK, CHUNK), dtype=bool))
    w_mask = jnp.where(causal[None, None], a_w_s, 0.0).astype(BF16)
    w_mask_t = jnp.swapaxes(w_mask, 2, 3)
    b_s_t = jnp.swapaxes(a_b_s, 1, 2)
    group_onehot = (jnp.arange(W)[:, None] // (W // G) == jnp.arange(LANES)[None, :]).astype(BF16)
    lane = jnp.arange(LANES) % HEAD_DIM
    inv_freq = ROPE_THETA ** (-jnp.arange(0, ROPE_DIM, 2, dtype=F32) / ROPE_DIM)
    inv_freq_row = jnp.where(lane < ROPE_DIM, inv_freq[lane % (ROPE_DIM // 2)], 0.0)[None, :].astype(F32)
    tabs = rope_tables(positions.reshape(T, 1), inv_freq_row, "rope_tables")

    saved = []
    for i in range(depth):
        j = i // 2
        s = dict(h=h, hn=hn)
        post_pre = (post_mix_g[i][None], pre_ffn_g[i][None])
        if i % 2 == 0:
            s["a"] = rowmm(hn, weight(i, "a_w_in", hn, all_started), True, a_b_in[j][None], BF16, "gmlp_in")
            s["gated"], s["mix"], s["h2"], s["fn"] = gmlp_gate_out(
                s["a"], a_ln_g[j][None], a_ln_b[j][None], w_mask[j], b_s_t[j], weight(i, "a_w_out", s["a"], ahead=i > 0),
                h, *post_pre, "gmlp_gate_out")
            if i == 0:
                pass_on(group_of[(0, "ffn_w_gu")][0], (s["mix"],))
        else:
            w_qkv = weight(i, "b_w_qkv", hn)
            b_qkv_full = jnp.swapaxes(gathered[group_of[(1, "b_w_qkv")][0]][-1], 0, 1).reshape(b_b_qkv.shape[0], 1, -1)
            s["qkv"] = rowmm(hn, w_qkv, True, b_qkv_full[j], BF16, "attn_qkv")
            s["qk"] = rope_fwd(s["qkv"], tabs, q_width, kv_width, "attn_rope")
            s["o"] = attn_fwd(s["qk"], s["qkv"], b_sinks[j], q_width, kv_width, "attn_core")
            s["mix"], s["h2"], s["fn"] = mm_add_norm(s["o"], weight(i, "b_w_o", s["o"]), h, *post_pre, "attn_out")
        s["act"], s["act_partials"] = ffn_up(s["fn"], weight(i, "ffn_w_gu", s["fn"]), "ffn_up")
        w_down = weight(i, "ffn_w_down", s["act"])
        saved.append(s)
        if i + 1 < depth:
            s["f"], h, hn = mm_add_norm(s["act"], w_down, s["h2"], post_ffn_g[i][None], pre_mix_g[i + 1][None], "ffn_down")
    small = {k: [None] * weights[k].shape[0] for k in names if k not in big}
    dh, df, loss_sum, small["post_ffn_g"][depth - 1] = mm_add_norm_loss(
        saved[-1]["act"], w_down, saved[-1]["h2"], post_ffn_g[depth - 1][None], loss_target[0], "ffn_down_loss")

    grads = {k: [None] * weights[k].shape[0] for k in big}
    in_flight = []

    def send_grads(keys, tag):
        sems, bufs, token = exchange_start([_blocked(grads[k][l]) for k, l in keys], "exchange_start_" + tag)
        in_flight.append((keys, sems, bufs, tag))
        return token

    replicated = [k for k in names if k not in big and k not in ("b_b_qkv", "a_w_s")]
    main = sum(weights[k].size for k in replicated)
    main_rows = -(-main // (LANES * LANES)) * LANES
    bias_size = b_b_qkv.shape[0] * N_DEV * b_b_qkv.shape[1]
    bias_rows = -(-bias_size // (8 * LANES)) * 8
    flat = lambda vals: [v.reshape(-1) for v in vals]

    def start_small_gather():
        packed = jnp.concatenate(
            flat(p for k in replicated for p in small[k])
            + [loss_sum[0, :1], jnp.zeros((main_rows * LANES - main - 1,), F32)] + flat(small["b_b_qkv"]) + [jnp.zeros((bias_rows * LANES - bias_size,), F32)]
        ).reshape(1, main_rows + bias_rows, LANES)
        spatial = jnp.stack(small["a_w_s"]).reshape(1, -1, LANES)
        return gather_start([prep_weights([(packed, 0, False), (spatial, 0, False)], dev, "place_small_grads", F32)],
                            "gather_start_small")

    sent = ()
    for i in reversed(range(depth)):
        j = i // 2
        s = saved[i]
        dgu = ffn_dact(df, weight(i, "ffn_w_down"), s["act_partials"], "ffn_dact", after=sent)
        alone = i == 0
        grads["ffn_w_down"][i] = grad_mm(s["act"], df, "ffn_down_grad")
        early = (send_grads(layer_keys(i)[3:], f"ffn_down_{i}"),) if alone else ()
        grads["ffn_w_gu"][i] = grad_mm(dgu, s["fn"], "ffn_up_grad", after=early)
        sent = send_grads(layer_keys(i)[2:3] if alone else layer_keys(i)[2:], f"ffn_{i}")
        r = mm_norm_bwd(dgu, weight(i, "ffn_w_gu"), dh, s["h2"], pre_ffn_g[i][None],
                        (s["mix"], post_mix_g[i][None]), "ffn_dx", sent)
        dh, dmix = r["dh"], r["dy"]
        small["pre_ffn_g"][i], small["post_mix_g"][i] = r["dg_pre"], r["dg_post"]
        post = (saved[i - 1]["f"], post_ffn_g[i - 1][None]) if i > 0 else None
        if i % 2 == 0:
            grads["a_w_out"][j] = grad_mm(s["gated"], dmix, "gmlp_out_grad")
            if alone:
                sent = send_grads(layer_keys(i)[1:2], f"mixer_out_{i}")
            r = gmlp_gate_bwd_dx(
                s["a"], dmix, weight(i, "a_w_out"), a_ln_g[j][None], a_ln_b[j][None], w_mask[j], w_mask_t[j], b_s_t[j],
                group_onehot, weight(i, "a_w_in"), dh, s["h"], pre_mix_g[i][None], post, "gmlp_gate_bwd_dx", sent)
            small["a_w_s"][j], small["a_b_s"][j] = r["dw_s"], r["db_s_t"][:, :G].T
            small["a_ln_g"][j], small["a_ln_b"][j], small["a_b_in"][j] = r["dln_g"][0], r["dln_b"][0], r["db_in"][0]
            small["pre_mix_g"][i] = r["dg_pre"]
            first_layer = ()
            if i == 0:
                small_gather = start_small_gather()
                first_layer = (small_gather[2],)
            grads["a_w_in"][j] = grad_mm(r["da"], s["hn"], "gmlp_in_grad", after=first_layer)
            sent = (send_grads(layer_keys(i)[:1] if alone else layer_keys(i)[:2], f"mixer_{i}"),)
        else:
            grads["b_w_o"][j] = grad_mm(s["o"], dmix, "attn_out_grad")
            do = rowmm(dmix, weight(i, "b_w_o"), True, None, BF16, "attn_do")
            dq, dk, dv, dsink = attn_bwd(s["qk"], s["qkv"], do, b_sinks[j], q_width, kv_width, "attn_core_bwd")
            dqkv, dbias = rope_bwd(dq, dk, dv, tabs, "attn_rope_bwd")
            small["b_sinks"][j], small["b_b_qkv"][j] = dsink[0, :n_heads], dbias[0]
            grads["b_w_qkv"][j] = grad_mm(dqkv, s["hn"], "attn_qkv_grad")
            sent = send_grads(layer_keys(i)[:2], f"mixer_{i}")
            r = mm_norm_bwd(dqkv, weight(i, "b_w_qkv"), dh, s["h"], pre_mix_g[i][None], post, "attn_dx", sent)
            sent = ()
        dh, small["pre_mix_g"][i] = r["dh"], r["dg_pre"]
        if i > 0:
            df, small["post_ffn_g"][i - 1] = r["dy"], r["dg_post"]
    grad_x = dh[None]

    small_sems, small_lands, small_sent = small_gather
    pack_main = lambda src: jnp.concatenate(
        flat(src[k] for k in replicated) + [jnp.ones((main_rows * LANES - main,), F32)]).reshape(main_rows, LANES)
    packed_w, packed_m, packed_v = pack_main(weights), pack_main(mom_m), pack_main(mom_v)

    stacked = {k: [lax.empty(w_view[k].shape, F32) for _ in range(4)] for k in big}

    def update_group(entry, after):
        keys, sems, bufs, tag = entry
        partials, landed = exchange_wait(bufs, sems, after, "exchange_wait_" + tag)
        for (k, l), part, land in zip(keys, partials, landed):
            stacked[k] = sum_adamw(part, land, w_view[k], m_view[k], v_view[k], stacked[k], l,
                                   k in in_kernel, dev, "adamw_" + k)
        return tuple(stacked[k][0] for k, _ in keys)

    after = (small_sent, dh) + sent
    for entry in in_flight[:-1]:
        after = update_group(entry, after)
    out_g, out_d, out_m, out_v = {}, {}, {}, {}

    gathered_small, gathered_spatial = gather_wait(small_lands, small_sems[0], after + (packed_w, packed_m, packed_v),
                                                   "gather_wait_small")
    rows_of = lambda t: t.reshape(-1, LANES)
    spatial = sum_adamw_small(gathered_spatial, rows_of(a_w_s), rows_of(m_a_w_s), rows_of(v_a_w_s), "adamw_a_w_s")
    out_g["a_w_s"], out_d["a_w_s"], out_m["a_w_s"], out_v["a_w_s"] = [t.reshape(a_w_s.shape) for t in spatial]
    results = sum_adamw_small(gathered_small, packed_w, packed_m, packed_v, "adamw_small")
    loss = results[0].reshape(-1)[main] * (0.5 / D)
    off = 0
    for k in replicated:
        size, shape = weights[k].size, weights[k].shape
        out_g[k], out_d[k], out_m[k], out_v[k] = [t.reshape(-1)[off:off + size].reshape(shape) for t in results]
        off += size
    n_local = b_b_qkv.shape[1]
    bias_sum = sum_parts(gathered_small[:, main_rows:], "sum_bias_grads").reshape(-1)[:bias_size]
    out_g["b_b_qkv"] = lax.dynamic_slice_in_dim(
        bias_sum.reshape(b_b_qkv.shape[0], -1), device * n_local, n_local, axis=1)
    pad = lambda t: jnp.concatenate([t.reshape(-1), jnp.ones((8 * LANES - t.size,), F32)]).reshape(8, LANES)
    bias_update = adamw(pad(b_b_qkv), pad(out_g["b_b_qkv"]), pad(m_b_b_qkv), pad(v_b_b_qkv), "adamw_bias")
    out_d["b_b_qkv"], out_m["b_b_qkv"], out_v["b_b_qkv"] = [
        t.reshape(-1)[:b_b_qkv.size].reshape(b_b_qkv.shape) for t in bias_update]

    update_group(in_flight[-1], (spatial[0], results[0], bias_update[0]))
    for k in big:
        out_g[k], out_d[k], out_m[k], out_v[k] = [view(k, t) for t in stacked[k]]

    return (loss, grad_x, *[out_g[k] for k in names], *[out_d[k] for k in names],
            *[out_m[k] for k in names], *[out_v[k] for k in names])
```

```python
import math

import jax
import jax.numpy as jnp
from jax import lax
from jax.experimental import pallas as pl
from jax.experimental.pallas import tpu as pltpu

F32, BF16 = jnp.float32, jnp.bfloat16
MESH = pl.DeviceIdType.MESH
N_DEV = 8

CHUNK = 128
HEAD_DIM = 64
N_KV_HEADS = 4
ROPE_DIM = HEAD_DIM // 4
ROPE_THETA = 500000.0
RMS_EPS = 1e-6
LN_EPS = 1e-5
NEG_INF = -1e30

ADAM_LR = 0.001
ADAM_B1 = 0.9
ADAM_B2 = 0.999
ADAM_EPS = 1e-08
ADAM_WD = 0.01
ADAM_STEP = 10

V7X_VMEM_BYTES = 64 * 2 ** 20
VMEM_LIMIT = V7X_VMEM_BYTES - 8 * 2 ** 20
LANES = 128

ROW_TILE = 512
COL_TILE = 512
GRAD_TILE_ROWS = 1408
GRAD_TILE_DEPTH = 1024
ADAM_TILE = 256


def _params(*sem):
    return pltpu.CompilerParams(dimension_semantics=sem or None, vmem_limit_bytes=VMEM_LIMIT)


def _tile(n, pref):
    if n <= pref:
        return n
    t = pref - pref % LANES
    while t >= LANES:
        if n % t == 0:
            return t
        t -= LANES
    return n


def _full(shape):
    return pl.BlockSpec(shape, lambda *_: (0,) * len(shape))


def _resident(shape):
    return pl.BlockSpec(shape, lambda *_: (0,) * len(shape), pipeline_mode=pl.Buffered(1))


def _rows(tm, width):
    return pl.BlockSpec((tm, width), lambda i: (i, 0))


def _rms(x, g):
    r = lax.rsqrt(jnp.mean(x * x, axis=-1, keepdims=True) + RMS_EPS)
    return x * r * g


def _rms_bwd(x, g, dy):
    r = lax.rsqrt(jnp.mean(x * x, axis=-1, keepdims=True) + RMS_EPS)
    xhat = x * r
    dg = jnp.sum(dy * xhat, axis=0, keepdims=True)
    dxhat = dy * g
    dx = r * (dxhat - xhat * jnp.mean(dxhat * xhat, axis=-1, keepdims=True))
    return dx, dg


_INV_SQRT2 = 1.0 / math.sqrt(2.0)
_INV_SQRT2PI = 1.0 / math.sqrt(2.0 * math.pi)


def _gelu(x):
    return 0.5 * x * (1.0 + lax.erf(x * _INV_SQRT2))


def _gelu_and_grad(x):
    cdf = 0.5 * (1.0 + lax.erf(x * _INV_SQRT2))
    return x * cdf, cdf + x * jnp.exp(-0.5 * x * x) * _INV_SQRT2PI


def _sigmoid(x):
    return 0.5 * jnp.tanh(0.5 * x) + 0.5


def _position():
    return lax.axis_index("x"), lax.axis_index("y"), lax.axis_index("c")


def _index(p):
    return 4 * p[0] + 2 * p[1] + p[2]


_HBM = pl.BlockSpec(memory_space=pltpu.HBM)
_SEM = pl.BlockSpec(memory_space=pltpu.SEMAPHORE)
_ORDERED_BY_DATA = pltpu.SideEffectType.DATAFLOW_SIDE_EFFECTING


def _in_hbm(v):
    return pltpu.with_memory_space_constraint(v, pltpu.HBM)


def _peer(k):
    x, y, c = _position()
    return (x ^ ((k >> 2) & 1), y ^ ((k >> 1) & 1), c ^ (k & 1))


def _split_copies(srcs, dsts, send, recv):
    return [pltpu.make_async_remote_copy(
        src_ref=src(k), dst_ref=dst(k), send_sem=send.at[7 * a + k - 1], recv_sem=recv.at[7 * a + k - 1],
        device_id=_peer(k), device_id_type=MESH)
        for a, (src, dst) in enumerate(zip(srcs, dsts)) for k in range(1, N_DEV)]


def _start_call(groups, sent, copies_of, name, per_array=N_DEV - 1):
    flat = [v for grp in groups for v in grp]
    n, ng = len(flat), len(groups)

    def body(*refs):
        bufs, sems = refs[:n], refs[n:n + 2 * ng]
        q = 0
        for g, grp in enumerate(groups):
            for cp in copies_of(bufs[q:q + len(grp)], sems[2 * g], sems[2 * g + 1]):
                cp.start()
            q += len(grp)
        refs[-1][...] = jnp.zeros_like(refs[-1])

    sem_shapes = []
    for count in sent:
        sem_shapes += [pltpu.SemaphoreType.DMA((per_array * count,)) for _ in range(2)]
    res = pl.pallas_call(
        body, name=name,
        out_shape=sem_shapes + [pltpu.HBM(v.shape, v.dtype) for v in flat] + [jax.ShapeDtypeStruct((8, LANES), F32)],
        in_specs=[_HBM] * n,
        out_specs=[_SEM] * (2 * ng) + [_HBM] * n + [pl.BlockSpec(memory_space=pltpu.VMEM)],
        input_output_aliases={i: 2 * ng + i for i in range(n)},
        compiler_params=pltpu.CompilerParams(has_side_effects=_ORDERED_BY_DATA),
    )(*[_in_hbm(v) for v in flat])
    sems = [(res[2 * g], res[2 * g + 1]) for g in range(ng)]
    return sems, list(res[2 * ng:-1]), res[-1]


def _wait_call(bufs, sems, copies_of, after, name):
    n = len(bufs)

    def body(*refs):
        for cp in copies_of(refs[:n], refs[n], refs[n + 1]):
            cp.wait_send()
            cp.wait_recv()

    return list(pl.pallas_call(
        body, name=name,
        out_shape=[pltpu.HBM(v.shape, v.dtype) for v in bufs],
        in_specs=[_HBM] * n + [_SEM, _SEM] + [pl.BlockSpec(memory_space=pl.ANY)] * len(after),
        out_specs=[_HBM] * n,
        input_output_aliases={i: i for i in range(n)},
        compiler_params=pltpu.CompilerParams(has_side_effects=_ORDERED_BY_DATA),
    )(*bufs, sems[0], sems[1], *after))


def _gather_copies(lands, send, recv):
    me = _index(_position())
    mine = [lambda k, ref=ref: ref.at[pl.ds(me, 1)] for ref in lands]
    return _split_copies(mine, mine, send, recv)


def _gather_arrivals(lands, send, recv):
    me = _index(_position())
    mine = [lambda k, ref=ref: ref.at[pl.ds(me, 1)] for ref in lands]
    theirs = [lambda k, ref=ref: ref.at[pl.ds(_index(_peer(k)), 1)] for ref in lands]
    return _split_copies(mine, theirs, send, recv)


def _exchange_copies(bufs, send, recv):
    half = len(bufs) // 2
    srcs = [lambda k, ref=ref: ref.at[pl.ds(_index(_peer(k)), 1)] for ref in bufs[:half]]
    dsts = [lambda k, ref=ref: ref.at[pl.ds(k - 1, 1)] for ref in bufs[half:]]
    return _split_copies(srcs, dsts, send, recv)


def gather_start(groups, name):
    return _start_call(groups, [len(grp) for grp in groups], _gather_copies, name)


def gather_wait(lands, sems, after, name):
    return _wait_call(lands, sems, _gather_arrivals, after, name)


_DIRECT = (1, 2, 4, 6)
_PASSED = (3, 5, 7)


def _remote(src, dst, send, recv, q, k):
    return pltpu.make_async_remote_copy(src_ref=src, dst_ref=dst, send_sem=send.at[q], recv_sem=recv.at[q],
                                        device_id=_peer(k), device_id_type=MESH)


def _slot_of(land, k):
    return land.at[pl.ds(_index(_peer(k)), 1)]


def relay_start(groups, name):
    def copies(lands, send, recv):
        return [_remote(_slot_of(land, 0), _slot_of(land, 0), send, recv, len(_DIRECT) * a + j, k)
                for a, land in enumerate(lands) for j, k in enumerate(_DIRECT)]
    return _start_call(groups, [len(grp) for grp in groups], copies, name, per_array=len(_DIRECT))


def relay_pass(lands, sems, after, name):
    n = len(lands)

    def body(*refs):
        bufs, recv_first = refs[:n], refs[n]
        send, recv = refs[n + 1 + len(after):n + 3 + len(after)]
        for a, land in enumerate(bufs):
            for j, k in enumerate(_PASSED):
                came = _slot_of(land, k ^ 1)
                _remote(came, came, recv_first, recv_first, len(_DIRECT) * a + _DIRECT.index(k ^ 1), k ^ 1).wait_recv()
                _remote(came, came, send, recv, len(_PASSED) * a + j, 1).start()

    res = pl.pallas_call(
        body, name=name,
        out_shape=[pltpu.SemaphoreType.DMA((len(_PASSED) * n,))] * 2 + [pltpu.HBM(v.shape, v.dtype) for v in lands],
        in_specs=[_HBM] * n + [_SEM] + [pl.BlockSpec(memory_space=pl.ANY)] * len(after),
        out_specs=[_SEM, _SEM] + [_HBM] * n,
        input_output_aliases={i: 2 + i for i in range(n)},
        compiler_params=pltpu.CompilerParams(has_side_effects=_ORDERED_BY_DATA),
    )(*lands, sems[1], *after)
    return (res[0], res[1]), list(res[2:])


def relay_wait(lands, first, second, after, name):
    n = len(lands)

    def body(*refs):
        bufs = refs[:n]
        send_first, recv_first, send, recv = refs[n:n + 4]
        for a, land in enumerate(bufs):
            mine = _slot_of(land, 0)
            _remote(mine, _slot_of(land, 1), send_first, recv_first, len(_DIRECT) * a, 1).wait_recv()
            for j, k in enumerate(_DIRECT):
                _remote(mine, mine, send_first, recv_first, len(_DIRECT) * a + j, k).wait_send()
            for j, k in enumerate(_PASSED):
                cp = _remote(_slot_of(land, k ^ 1), _slot_of(land, k), send, recv, len(_PASSED) * a + j, 1)
                cp.wait_send()
                cp.wait_recv()

    return list(pl.pallas_call(
        body, name=name,
        out_shape=[pltpu.HBM(v.shape, v.dtype) for v in lands],
        in_specs=[_HBM] * n + [_SEM] * 4 + [pl.BlockSpec(memory_space=pl.ANY)] * len(after),
        out_specs=[_HBM] * n,
        input_output_aliases={i: i for i in range(n)},
        compiler_params=pltpu.CompilerParams(has_side_effects=_ORDERED_BY_DATA),
    )(*lands, first[0], first[1], second[0], second[1], *after))


def exchange_start(partials, name):
    lands = [lax.empty((N_DEV - 1,) + p.shape[1:], p.dtype) for p in partials]
    sems, bufs, token = _start_call([list(partials) + lands], [len(partials)], _exchange_copies, name)
    return sems[0], bufs, token


def exchange_wait(bufs, sems, after, name):
    bufs = _wait_call(bufs, sems, _exchange_copies, after, name)
    return bufs[:len(bufs) // 2], bufs[len(bufs) // 2:]


def prep_weights(items, dev, name, dtype=BF16, after=()):
    n = len(items)
    blocks = [(1, w.shape[2], w.shape[1]) if t else (1,) + w.shape[1:] for w, _, t in items]

    def body(d_ref, *refs):
        for (_, _, transpose), w_ref, o_ref in zip(items, refs[:n], refs[-n:]):
            v = w_ref[0]
            o_ref[0] = (v.T if transpose else v).astype(dtype)

    return pl.pallas_call(
        body, name=name,
        grid_spec=pltpu.PrefetchScalarGridSpec(
            num_scalar_prefetch=1, grid=(1,),
            in_specs=[pl.BlockSpec((1,) + w.shape[1:], lambda i, d, layer=layer: (layer, 0, 0)) for w, layer, _ in items]
            + [pl.BlockSpec(memory_space=pl.ANY)] * len(after),
            out_specs=[pl.BlockSpec(blk, lambda i, d: (d[0], 0, 0)) for blk in blocks]),
        out_shape=[jax.ShapeDtypeStruct((N_DEV,) + blk[1:], dtype) for blk in blocks],
        compiler_params=_params("arbitrary"),
    )(dev, *[w for w, _, _ in items], *after)


def _adamw_math(w, g, m, v):
    mn = ADAM_B1 * m + (1.0 - ADAM_B1) * g
    vn = ADAM_B2 * v + (1.0 - ADAM_B2) * (g * g)
    m_hat = mn * (1.0 / (1.0 - ADAM_B1 ** ADAM_STEP))
    v_hat = vn * (1.0 / (1.0 - ADAM_B2 ** ADAM_STEP))
    return -ADAM_LR * (m_hat / (jnp.sqrt(v_hat) + ADAM_EPS) + ADAM_WD * w), mn, vn


def sum_adamw(partial, landed, w, m, v, prev, layer, transpose, dev, name):
    _, r, c = partial.shape
    if transpose:
        tc = _tile(c, ADAM_TILE)
        grid = (c // tc,)
        part_spec = pl.BlockSpec((1, r, tc), lambda i, d: (d[0], 0, i))
        land_spec = pl.BlockSpec((N_DEV - 1, r, tc), lambda i, d: (0, 0, i))
        w_spec = pl.BlockSpec((1, tc, r), lambda i, d: (layer, i, 0))
    else:
        tr = r // 2 if r % 32 == 0 else r
        grid = (r // tr,)
        part_spec = pl.BlockSpec((1, tr, c), lambda i, d: (d[0], i, 0))
        land_spec = pl.BlockSpec((N_DEV - 1, tr, c), lambda i, d: (0, i, 0))
        w_spec = pl.BlockSpec((1, tr, c), lambda i, d: (layer, i, 0))

    def body(d_ref, p_ref, l_ref, w_ref, m_ref, v_ref, *rest):
        g_ref, dl_ref, mo_ref, vo_ref = rest[-4:]
        acc = p_ref[0].astype(F32)
        for k in range(N_DEV - 1):
            acc = acc + l_ref[k].astype(F32)
        g = acc.T if transpose else acc
        delta, mn, vn = _adamw_math(w_ref[0], g, m_ref[0], v_ref[0])
        g_ref[0], dl_ref[0], mo_ref[0], vo_ref[0] = g, delta, mn, vn

    return pl.pallas_call(
        body, name=name,
        grid_spec=pltpu.PrefetchScalarGridSpec(
            num_scalar_prefetch=1, grid=grid,
            in_specs=[part_spec, land_spec, w_spec, w_spec, w_spec] + [pl.BlockSpec(memory_space=pl.ANY)] * 4,
            out_specs=[w_spec] * 4),
        out_shape=[jax.ShapeDtypeStruct(w.shape, F32)] * 4,
        input_output_aliases={6 + q: q for q in range(4)},
        compiler_params=_params("parallel"),
    )(dev, partial, landed, w, m, v, *prev)


def sum_parts(parts, name):
    P, r, c = parts.shape

    def body(p_ref, o_ref):
        acc = p_ref[0].astype(F32)
        for s in range(1, P):
            acc = acc + p_ref[s].astype(F32)
        o_ref[...] = acc

    return pl.pallas_call(
        body, name=name, in_specs=[_full((P, r, c))], out_specs=_full((r, c)), grid=(1,),
        out_shape=jax.ShapeDtypeStruct((r, c), F32), compiler_params=_params("arbitrary"))(parts)


def adamw(w, g, m, v, name):
    R, C = w.shape
    tr = _tile(R, ROW_TILE)

    def body(w_ref, g_ref, m_ref, v_ref, d_ref, mo_ref, vo_ref):
        d_ref[...], mo_ref[...], vo_ref[...] = _adamw_math(w_ref[...], g_ref[...], m_ref[...], v_ref[...])

    spec = pl.BlockSpec((tr, C), lambda i: (i, 0))
    return pl.pallas_call(
        body, name=name, grid=(R // tr,),
        in_specs=[spec] * 4, out_specs=[spec] * 3,
        out_shape=[jax.ShapeDtypeStruct((R, C), F32)] * 3,
        compiler_params=_params("parallel"),
    )(w, g, m, v)


def sum_adamw_small(gathered, w, m, v, name):
    rows = w.shape[0]
    tr = _tile(rows, ROW_TILE)

    def body(p_ref, w_ref, m_ref, v_ref, g_ref, d_ref, mo_ref, vo_ref):
        g = p_ref[0]
        for s in range(1, N_DEV):
            g = g + p_ref[s]
        g_ref[...] = g
        d_ref[...], mo_ref[...], vo_ref[...] = _adamw_math(w_ref[...], g, m_ref[...], v_ref[...])

    spec = pl.BlockSpec((tr, LANES), lambda i: (i, 0))
    return pl.pallas_call(
        body, name=name, grid=(rows // tr,),
        in_specs=[pl.BlockSpec((N_DEV, tr, LANES), lambda i: (0, i, 0)), spec, spec, spec], out_specs=[spec] * 4,
        out_shape=[jax.ShapeDtypeStruct((rows, LANES), F32)] * 4, compiler_params=_params("parallel"))(gathered, w, m, v)


def first_norm(h, g, name):
    T, D = h.shape
    tm = _tile(T, ROW_TILE)

    def body(h_ref, g_ref, hn_ref):
        hn_ref[...] = _rms(h_ref[...], g_ref[...]).astype(BF16)

    row = _rows(tm, D)
    return pl.pallas_call(
        body, name=name, grid=(T // tm,), in_specs=[row, _full((1, D))], out_specs=row,
        out_shape=jax.ShapeDtypeStruct((T, D), BF16), compiler_params=_params("parallel"))(h, g)


_NT = (((1,), (1,)), ((), ()))
_TN = (((0,), (0,)), ((), ()))


def rowmm(a, w, transposed_w, bias, out_dtype, name, after=None):
    M, K = a.shape
    N = w.shape[0] if transposed_w else w.shape[1]
    tm = _tile(M, ROW_TILE)
    tn = _tile(N, COL_TILE)

    def body(*refs):
        a_ref, w_ref = refs[:2]
        o_ref = refs[-1]
        av = a_ref[...]
        for j in range(N // tn):
            cols = pl.ds(j * tn, tn)
            if transposed_w:
                acc = lax.dot_general(av, w_ref[cols, :], _NT, preferred_element_type=F32)
            else:
                acc = jnp.dot(av, w_ref[:, cols], preferred_element_type=F32)
            if bias is not None:
                acc = acc + refs[2][:, cols]
            o_ref[:, cols] = acc.astype(out_dtype)

    ins, in_specs = [a, w], [_rows(tm, K), _resident(w.shape)]
    if bias is not None:
        ins.append(bias)
        in_specs.append(_full((1, N)))
    if after is not None:
        ins.append(after)
        in_specs.append(pl.BlockSpec(memory_space=pl.ANY))
    return pl.pallas_call(
        body, name=name, grid=(M // tm,), in_specs=in_specs, out_specs=_rows(tm, N),
        out_shape=jax.ShapeDtypeStruct((M, N), out_dtype), compiler_params=_params("parallel"))(*ins)


def _rows_times_weight(a_ref, w_ref, acc_ref):
    N = w_ref.shape[1]
    tn = _tile(N, COL_TILE)
    av = a_ref[...]
    for j in range(N // tn):
        cols = pl.ds(j * tn, tn)
        acc_ref[:, cols] = jnp.dot(av, w_ref[:, cols], preferred_element_type=F32)


def mm_add_norm(a, w, h, g_post, g_pre, name):
    T, K = a.shape
    D = w.shape[1]
    tm = _tile(T, ROW_TILE)

    def body(a_ref, w_ref, h_ref, gp_ref, g_ref, y_ref, ho_ref, hn_ref, acc_ref):
        _rows_times_weight(a_ref, w_ref, acc_ref)
        y = acc_ref[...]
        y_ref[...] = y.astype(BF16)
        hv = h_ref[...] + _rms(y, gp_ref[...])
        ho_ref[...] = hv
        hn_ref[...] = _rms(hv, g_ref[...]).astype(BF16)

    row, vec = _rows(tm, D), _full((1, D))
    return pl.pallas_call(
        body, name=name, grid=(T // tm,), in_specs=[_rows(tm, K), _resident(w.shape), row, vec, vec],
        out_specs=[row, row, row],
        out_shape=[jax.ShapeDtypeStruct((T, D), BF16), jax.ShapeDtypeStruct((T, D), F32),
                   jax.ShapeDtypeStruct((T, D), BF16)],
        scratch_shapes=[pltpu.VMEM((tm, D), F32)], compiler_params=_params("parallel"))(a, w, h, g_post, g_pre)


def mm_add_norm_loss(a, w, h, g_post, target, name):
    T, K = a.shape
    D = w.shape[1]
    tm = _tile(T, ROW_TILE)

    def body(a_ref, w_ref, h_ref, gp_ref, t_ref, dh_ref, dy_ref, loss_ref, dgp_ref, acc_ref):
        @pl.when(pl.program_id(0) == 0)
        def _():
            loss_ref[...] = jnp.zeros_like(loss_ref)
            dgp_ref[...] = jnp.zeros_like(dgp_ref)
        _rows_times_weight(a_ref, w_ref, acc_ref)
        y = acc_ref[...]
        err = h_ref[...] + _rms(y, gp_ref[...]) - t_ref[...]
        dh = err * (1.0 / D)
        dh_ref[...] = dh
        loss_ref[...] += jnp.sum(err * err)
        dy, dgp = _rms_bwd(y, gp_ref[...], dh)
        dy_ref[...] = dy.astype(BF16)
        dgp_ref[...] += dgp

    row, vec = _rows(tm, D), _full((1, D))
    return pl.pallas_call(
        body, name=name, grid=(T // tm,), in_specs=[_rows(tm, K), _resident(w.shape), row, vec, row],
        out_specs=[row, row, _full((8, LANES)), vec],
        out_shape=[jax.ShapeDtypeStruct((T, D), F32), jax.ShapeDtypeStruct((T, D), BF16),
                   jax.ShapeDtypeStruct((8, LANES), F32), jax.ShapeDtypeStruct((1, D), F32)],
        scratch_shapes=[pltpu.VMEM((tm, D), F32)], compiler_params=_params("arbitrary"))(a, w, h, g_post, target)


def mm_norm_bwd(a, w, dh_in, x, g_pre, post, name, after):
    T, K = a.shape
    D = w.shape[1]
    tm = _tile(T, ROW_TILE)
    row, vec = _rows(tm, D), _full((1, D))
    ins, in_specs = [a, w, dh_in, x, g_pre], [_rows(tm, K), _resident(w.shape), row, row, vec]
    outs = [jax.ShapeDtypeStruct((T, D), F32), jax.ShapeDtypeStruct((1, D), F32)]
    out_specs, names = [row, vec], ["dh", "dg_pre"]
    if post is not None:
        ins += list(post)
        in_specs += [row, vec]
        outs += [jax.ShapeDtypeStruct((T, D), BF16), jax.ShapeDtypeStruct((1, D), F32)]
        out_specs += [row, vec]
        names += ["dy", "dg_post"]
    n_in = len(ins) + 1

    def body(*refs):
        a_ref, w_ref, dh_ref_in, x_ref, g_ref = refs[:5]
        out_refs, acc_ref = refs[n_in:-1], refs[-1]
        first = pl.program_id(0) == 0
        _rows_times_weight(a_ref, w_ref, acc_ref)
        dx, dg = _rms_bwd(x_ref[...], g_ref[...], acc_ref[...])
        dh = dh_ref_in[...] + dx
        out_refs[0][...] = dh

        @pl.when(first)
        def _():
            for r in out_refs[1::2]:
                r[...] = jnp.zeros_like(r)
        out_refs[1][...] += dg
        if post is not None:
            y_ref, gp_ref = refs[5:7]
            dy, dgp = _rms_bwd(y_ref[...].astype(F32), gp_ref[...], dh)
            out_refs[2][...] = dy.astype(BF16)
            out_refs[3][...] += dgp

    res = pl.pallas_call(
        body, name=name, grid=(T // tm,), in_specs=in_specs + [pl.BlockSpec(memory_space=pl.ANY)],
        out_specs=out_specs, out_shape=outs, scratch_shapes=[pltpu.VMEM((tm, D), F32)],
        compiler_params=_params("arbitrary"))(*ins, after)
    return dict(zip(names, res))


def grad_mm(a, b, name, after=()):
    T, N = a.shape
    K = b.shape[1]
    tn = _tile(N, GRAD_TILE_ROWS)
    tt = _tile(T, GRAD_TILE_DEPTH)

    def body(a_ref, b_ref, *rest):
        o_ref, acc_ref = rest[-2:]
        t = pl.program_id(1)

        @pl.when(t == 0)
        def _():
            acc_ref[...] = jnp.zeros_like(acc_ref)
        acc_ref[...] += lax.dot_general(a_ref[...], b_ref[...], _TN, preferred_element_type=F32)

        @pl.when(t == pl.num_programs(1) - 1)
        def _():
            o_ref[...] = acc_ref[...].astype(BF16)

    return pl.pallas_call(
        body, name=name, grid=(N // tn, T // tt),
        in_specs=[pl.BlockSpec((tt, tn), lambda j, t: (t, j)), pl.BlockSpec((tt, K), lambda j, t: (t, 0))]
        + [pl.BlockSpec(memory_space=pl.ANY)] * len(after),
        out_specs=pl.BlockSpec((tn, K), lambda j, t: (j, 0)),
        out_shape=jax.ShapeDtypeStruct((N, K), BF16),
        scratch_shapes=[pltpu.VMEM((tn, K), F32)],
        compiler_params=_params("parallel", "arbitrary"))(a, b, *after)


def ffn_up(fn, w_gu_t, name):
    T, D = fn.shape
    F = w_gu_t.shape[0] // 2
    tm = _tile(T, ROW_TILE)
    tn = _tile(F, COL_TILE)

    def body(a_ref, w_ref, act_ref, part_ref):
        av = a_ref[...]
        for j in range(F // tn):
            g = lax.dot_general(av, w_ref[pl.ds(j * tn, tn), :], _NT, preferred_element_type=F32)
            up = lax.dot_general(av, w_ref[pl.ds(F + j * tn, tn), :], _NT, preferred_element_type=F32)
            sg = _sigmoid(g)
            silu = g * sg
            act_ref[:, pl.ds(j * tn, tn)] = (silu * up).astype(BF16)
            part_ref[:, pl.ds(j * tn, tn)] = (up * (sg + silu * (1.0 - sg))).astype(BF16)
            part_ref[:, pl.ds(F + j * tn, tn)] = silu.astype(BF16)

    return pl.pallas_call(
        body, name=name, grid=(T // tm,), in_specs=[_rows(tm, D), _resident(w_gu_t.shape)],
        out_specs=[_rows(tm, F), _rows(tm, 2 * F)],
        out_shape=[jax.ShapeDtypeStruct((T, F), BF16), jax.ShapeDtypeStruct((T, 2 * F), BF16)],
        compiler_params=_params("parallel"))(fn, w_gu_t)


def ffn_dact(df, w_down, partials, name, after=()):
    T, D = df.shape
    F = w_down.shape[0]
    tm = _tile(T, ROW_TILE)
    tn = _tile(F, COL_TILE)

    def body(d_ref, w_ref, part_ref, *rest):
        o_ref = rest[-1]
        dv = d_ref[...]
        for j in range(F // tn):
            dact = lax.dot_general(dv, w_ref[pl.ds(j * tn, tn), :], _NT, preferred_element_type=F32)
            for cols in (pl.ds(j * tn, tn), pl.ds(F + j * tn, tn)):
                o_ref[:, cols] = (dact * part_ref[:, cols].astype(F32)).astype(BF16)

    return pl.pallas_call(
        body, name=name, grid=(T // tm,),
        in_specs=[_rows(tm, D), _resident(w_down.shape), _rows(tm, 2 * F)] + [pl.BlockSpec(memory_space=pl.ANY)] * len(after),
        out_specs=_rows(tm, 2 * F), out_shape=jax.ShapeDtypeStruct((T, 2 * F), BF16),
        compiler_params=_params("parallel"))(df, w_down, partials, *after)


def _layernorm_stats(v):
    mu = jnp.mean(v, axis=-1, keepdims=True)
    cen = v - mu
    rstd = lax.rsqrt(jnp.mean(cen * cen, axis=-1, keepdims=True) + LN_EPS)
    return cen * rstd, rstd


GMLP_ROWS = 2 * CHUNK


def gmlp_gate_out(a, ln_g, ln_b, w_mask, b_s_t, w_out, h, g_post, g_pre, name):
    T, W2 = a.shape
    W = W2 // 2
    G = w_mask.shape[0]
    C = W // G
    D = w_out.shape[1]
    tm = min(T, GMLP_ROWS)

    def body(a_ref, lg_ref, lb_ref, w_ref, bs_ref, wo_ref, h_ref, gp_ref, g_ref,
             gated_ref, y_ref, ho_ref, hn_ref, acc_ref):
        for c in range(tm // CHUNK):
            rows = pl.ds(c * CHUNK, CHUNK)
            xhat, _ = _layernorm_stats(_gelu(a_ref[rows, W:].astype(F32)))
            vln = (xhat * lg_ref[...] + lb_ref[...]).astype(BF16)
            for g in range(G):
                cols = pl.ds(g * C, C)
                sv = jnp.dot(w_ref[g], vln[:, g * C:(g + 1) * C], preferred_element_type=F32) + bs_ref[:, g:g + 1]
                gated_ref[rows, cols] = (_gelu(a_ref[rows, cols].astype(F32)) * sv).astype(BF16)
        _rows_times_weight(gated_ref, wo_ref, acc_ref)
        y = acc_ref[...]
        y_ref[...] = y.astype(BF16)
        hv = h_ref[...] + _rms(y, gp_ref[...])
        ho_ref[...] = hv
        hn_ref[...] = _rms(hv, g_ref[...]).astype(BF16)

    row, vec = _rows(tm, D), _full((1, D))
    return pl.pallas_call(
        body, name=name, grid=(T // tm,),
        in_specs=[_rows(tm, W2), _full((1, W)), _full((1, W)), _full(w_mask.shape), _full(b_s_t.shape),
                  _resident(w_out.shape), row, vec, vec],
        out_specs=[_rows(tm, W), row, row, row],
        out_shape=[jax.ShapeDtypeStruct((T, W), BF16), jax.ShapeDtypeStruct((T, D), BF16),
                   jax.ShapeDtypeStruct((T, D), F32), jax.ShapeDtypeStruct((T, D), BF16)],
        scratch_shapes=[pltpu.VMEM((tm, D), F32)],
        compiler_params=_params("parallel"))(a, ln_g, ln_b, w_mask, b_s_t, w_out, h, g_post, g_pre)


def gmlp_gate_bwd_dx(a, dmix, w_out, ln_g, ln_b, w_mask, w_mask_t, b_s_t, group_onehot, w_in_t,
                     dh_in, x, g_pre, post, name, after):
    T, W2 = a.shape
    W = W2 // 2
    G = w_mask.shape[0]
    C = W // G
    D = w_out.shape[1]
    tm = min(T, GMLP_ROWS)
    tn = _tile(W, COL_TILE)
    row, vec = _rows(tm, D), _full((1, D))
    ins = [a, dmix, w_out, ln_g, ln_b, w_mask, w_mask_t, b_s_t, group_onehot, w_in_t, dh_in, x, g_pre]
    in_specs = [_rows(tm, W2), row, _resident(w_out.shape), _full((1, W)), _full((1, W)), _full(w_mask.shape),
                _full(w_mask_t.shape), _full(b_s_t.shape), _full(group_onehot.shape), _resident(w_in_t.shape), row, row, vec]
    names = ["da", "dw_s", "db_s_t", "dln_g", "dln_b", "db_in", "dh", "dg_pre"]
    outs = [jax.ShapeDtypeStruct((T, W2), BF16), jax.ShapeDtypeStruct((G, CHUNK, CHUNK), F32),
            jax.ShapeDtypeStruct((CHUNK, LANES), F32), jax.ShapeDtypeStruct((1, W), F32),
            jax.ShapeDtypeStruct((1, W), F32), jax.ShapeDtypeStruct((1, W2), F32),
            jax.ShapeDtypeStruct((T, D), F32), jax.ShapeDtypeStruct((1, D), F32)]
    out_specs = [_rows(tm, W2), _full((G, CHUNK, CHUNK)), _full((CHUNK, LANES)), _full((1, W)), _full((1, W)),
                 _full((1, W2)), row, vec]
    if post is not None:
        ins += list(post)
        in_specs += [row, vec]
        names += ["dy", "dg_post"]
        outs += [jax.ShapeDtypeStruct((T, D), BF16), jax.ShapeDtypeStruct((1, D), F32)]
        out_specs += [row, vec]
    n_in = len(ins) + 1

    def body(*refs):
        (a_ref, dm_ref, wo_ref, lg_ref, lb_ref, w_ref, wt_ref, bs_ref, e_ref, wi_ref, dh_in_ref, x_ref, gpre_ref) = refs[:13]
        out_refs = refs[n_in:-3]
        da_ref, dws_ref, dbs_ref, dlg_ref, dlb_ref, dbin_ref, dh_ref, dgpre_ref = out_refs[:8]
        dgated_ref, dvln_ref, acc_ref = refs[-3:]

        @pl.when(pl.program_id(0) == 0)
        def _():
            for r in out_refs[1:6] + out_refs[7::2]:
                r[...] = jnp.zeros_like(r)
        dm = dm_ref[...]
        for j in range(W // tn):
            cols = pl.ds(j * tn, tn)
            dgated_ref[:, cols] = lax.dot_general(dm, wo_ref[cols, :], _NT, preferred_element_type=F32)
        causal = (lax.broadcasted_iota(jnp.int32, (CHUNK, CHUNK), 1)
                  <= lax.broadcasted_iota(jnp.int32, (CHUNK, CHUNK), 0))
        for c in range(tm // CHUNK):
            rows = pl.ds(c * CHUNK, CHUNK)
            gelu_v, gelu_grad_v = _gelu_and_grad(a_ref[rows, W:].astype(F32))
            xhat, rstd = _layernorm_stats(gelu_v)
            vln = (xhat * lg_ref[...] + lb_ref[...]).astype(BF16)
            for g in range(G):
                cols = pl.ds(g * C, C)
                vg = vln[:, g * C:(g + 1) * C]
                gelu_u, gelu_grad_u = _gelu_and_grad(a_ref[rows, cols].astype(F32))
                dgated = dgated_ref[rows, cols]
                dsv = (dgated * gelu_u).astype(BF16)
                dbs_ref[...] += jnp.dot(dsv, e_ref[cols, :], preferred_element_type=F32)
                sv = jnp.dot(w_ref[g], vg, preferred_element_type=F32) + bs_ref[:, g:g + 1]
                dau = dgated * sv * gelu_grad_u
                da_ref[rows, cols] = dau.astype(BF16)
                dbin_ref[:, cols] += jnp.sum(dau, axis=0, keepdims=True)
                dws_ref[g] += jnp.where(causal, lax.dot_general(dsv, vg, _NT, preferred_element_type=F32), 0.0)
                dvln_ref[:, cols] = jnp.dot(wt_ref[g], dsv, preferred_element_type=F32)
            dvln = dvln_ref[...]
            dlg_ref[...] += jnp.sum(dvln * xhat, axis=0, keepdims=True)
            dlb_ref[...] += jnp.sum(dvln, axis=0, keepdims=True)
            dxhat = dvln * lg_ref[...]
            dv = rstd * (dxhat - jnp.mean(dxhat, axis=-1, keepdims=True)
                         - xhat * jnp.mean(dxhat * xhat, axis=-1, keepdims=True))
            dav = dv * gelu_grad_v
            da_ref[rows, W:] = dav.astype(BF16)
            dbin_ref[:, W:] += jnp.sum(dav, axis=0, keepdims=True)
        _rows_times_weight(da_ref, wi_ref, acc_ref)
        dx, dg = _rms_bwd(x_ref[...], gpre_ref[...], acc_ref[...])
        dh = dh_in_ref[...] + dx
        dh_ref[...] = dh
        dgpre_ref[...] += dg
        if post is not None:
            y_ref, gp_ref = refs[13:15]
            dy, dgp = _rms_bwd(y_ref[...].astype(F32), gp_ref[...], dh)
            out_refs[8][...] = dy.astype(BF16)
            out_refs[9][...] += dgp

    res = pl.pallas_call(
        body, name=name, grid=(T // tm,), in_specs=in_specs + [pl.BlockSpec(memory_space=pl.ANY)],
        out_specs=out_specs, out_shape=outs,
        scratch_shapes=[pltpu.VMEM((tm, W), F32), pltpu.VMEM((CHUNK, W), F32), pltpu.VMEM((tm, D), F32)],
        compiler_params=_params("arbitrary"))(*ins, after)
    return dict(zip(names, res))


def rope_tables(pos, inv_freq_row, name):
    T = pos.shape[0]
    tm = _tile(T, ROW_TILE)

    def body(p_ref, f_ref, c_ref, s1_ref, s2_ref):
        ang = p_ref[...].astype(F32) * f_ref[...]
        lane = lax.broadcasted_iota(jnp.int32, (tm, LANES), 1) % HEAD_DIM
        sin = jnp.sin(ang)
        c_ref[...] = jnp.cos(ang)
        s1_ref[...] = jnp.where(lane < _HALF, -sin, 0.0)
        s2_ref[...] = jnp.where((lane >= _HALF) & (lane < ROPE_DIM), sin, 0.0)

    tab = _rows(tm, LANES)
    return pl.pallas_call(
        body, name=name, grid=(T // tm,), in_specs=[_rows(tm, 1), _full((1, LANES))], out_specs=[tab] * 3,
        out_shape=[jax.ShapeDtypeStruct((T, LANES), F32)] * 3, compiler_params=_params("parallel"))(pos, inv_freq_row)


_HALF = ROPE_DIM // 2


def _slabs(x):
    return [x[:, b * LANES:(b + 1) * LANES] for b in range(x.shape[1] // LANES)]


def _rotate(x, c, s1, s2):
    return [xs * c + pltpu.roll(xs, LANES - _HALF, 1) * s1 + pltpu.roll(xs, _HALF, 1) * s2 for xs in _slabs(x)]


def _rotate_transposed(dy, c, s1, s2):
    return [ds * c + pltpu.roll(ds * s1, _HALF, 1) + pltpu.roll(ds * s2, LANES - _HALF, 1) for ds in _slabs(dy)]


def qkv_rope(hn, w_t, bias, tabs, q_width, kv_width, name):
    T, D = hn.shape
    N = w_t.shape[0]
    QK = q_width + kv_width
    tm = _tile(T, ROW_TILE)
    tn = 2 * LANES
    scale = HEAD_DIM ** -0.5

    def body(a_ref, w_ref, b_ref, c_ref, s1_ref, s2_ref, qkv_ref, qk_ref):
        av = a_ref[...]
        tabs_v = (c_ref[...], s1_ref[...], s2_ref[...])
        for j in range(N // tn):
            cols = pl.ds(j * tn, tn)
            acc = lax.dot_general(av, w_ref[cols, :], _NT, preferred_element_type=F32) + b_ref[:, cols]
            qkv_ref[:, cols] = acc.astype(BF16)
            if j * tn < QK:
                for b, y in enumerate(_rotate(acc, *tabs_v)):
                    if j * tn < q_width:
                        y = y * scale
                    qk_ref[:, pl.ds(j * tn + b * LANES, LANES)] = y.astype(BF16)

    tab = _rows(tm, LANES)
    return pl.pallas_call(
        body, name=name, grid=(T // tm,), in_specs=[_rows(tm, D), _resident(w_t.shape), _full((1, N)), tab, tab, tab],
        out_specs=[_rows(tm, N), _rows(tm, QK)],
        out_shape=[jax.ShapeDtypeStruct((T, N), BF16), jax.ShapeDtypeStruct((T, QK), BF16)],
        compiler_params=_params("parallel"))(hn, w_t, bias, *tabs)


def rope_bwd(dq, dk, dv, tabs, name):
    T, Q = dq.shape
    KV = dk.shape[1]
    tm = _tile(T, ROW_TILE)
    scale = HEAD_DIM ** -0.5

    def body(dq_ref, dk_ref, dv_ref, c_ref, s1_ref, s2_ref, o_ref, b_ref):
        @pl.when(pl.program_id(0) == 0)
        def _():
            b_ref[...] = jnp.zeros_like(b_ref)
        tabs_v = (c_ref[...], s1_ref[...], s2_ref[...])
        pieces = [s * scale for s in _rotate_transposed(dq_ref[...], *tabs_v)]
        pieces += _rotate_transposed(dk_ref[...], *tabs_v)
        pieces += _slabs(dv_ref[...])
        for b, y in enumerate(pieces):
            cols = pl.ds(b * LANES, LANES)
            o_ref[:, cols] = y.astype(BF16)
            b_ref[:, cols] += jnp.sum(y, axis=0, keepdims=True)

    tab = _rows(tm, LANES)
    return pl.pallas_call(
        body, name=name, grid=(T // tm,), in_specs=[_rows(tm, Q), _rows(tm, KV), _rows(tm, KV), tab, tab, tab],
        out_specs=[_rows(tm, Q + 2 * KV), _full((1, Q + 2 * KV))],
        out_shape=[jax.ShapeDtypeStruct((T, Q + 2 * KV), BF16), jax.ShapeDtypeStruct((1, Q + 2 * KV), F32)],
        compiler_params=_params("arbitrary"))(dq, dk, dv, *tabs)


def _band_mask(n, heads=1):
    qi = lax.broadcasted_iota(jnp.int32, (heads * CHUNK, 2 * CHUNK), 0) % CHUNK
    sj = lax.broadcasted_iota(jnp.int32, (heads * CHUNK, 2 * CHUNK), 1)
    return (sj > qi) & (sj <= qi + CHUNK) & ((n > 0) | (sj >= CHUNK))


def _kv_head(kc_ref, kp_ref, vc_ref, vp_ref, kh):
    lanes = slice(kh * HEAD_DIM, (kh + 1) * HEAD_DIM)
    return (jnp.concatenate([kp_ref[:, lanes], kc_ref[:, lanes]], axis=0),
            jnp.concatenate([vp_ref[:, lanes], vc_ref[:, lanes]], axis=0))


def _row_sums(x):
    return jnp.dot(x, jnp.ones((x.shape[1], LANES), BF16), preferred_element_type=F32)


def _lanes(v, n):
    return v[:, :n] if n <= LANES else jnp.concatenate([v] * (n // LANES), axis=-1)


def _head_probs(q, kk, valid, sink, sums_on_mxu):
    s = jnp.where(valid, lax.dot_general(q, kk, _NT, preferred_element_type=F32), NEG_INF)
    m = jnp.maximum(jnp.max(s, axis=-1, keepdims=True), sink)
    p = jnp.exp(s - m)
    e_sink = jnp.exp(sink - m)
    total = _row_sums(p.astype(BF16)) if sums_on_mxu else jnp.sum(p, axis=-1, keepdims=True)
    return p.astype(BF16), 1.0 / (total + e_sink), e_sink


def _attn_specs(q_width, kv_width, order):
    qb = q_width // kv_width
    prev = lambda i: jnp.maximum(order(i) - 1, 0)
    return [pl.BlockSpec((CHUNK, q_width), lambda i: (order(i), 0)),
            pl.BlockSpec((CHUNK, kv_width), lambda i: (order(i), qb)),
            pl.BlockSpec((CHUNK, kv_width), lambda i: (prev(i), qb)),
            pl.BlockSpec((CHUNK, kv_width), lambda i: (order(i), qb + 1)),
            pl.BlockSpec((CHUNK, kv_width), lambda i: (prev(i), qb + 1))]


def attn_fwd(qk, qkv, sinks, q_width, kv_width, name):
    T = qk.shape[0]
    group = q_width // kv_width

    def body(q_ref, kc_ref, kp_ref, vc_ref, vp_ref, sink_ref, o_ref):
        valid = _band_mask(pl.program_id(0))
        for kh in range(kv_width // HEAD_DIM):
            kk, vv = _kv_head(kc_ref, kp_ref, vc_ref, vp_ref, kh)
            for g in range(group):
                h = kh * group + g
                lanes = slice(h * HEAD_DIM, (h + 1) * HEAD_DIM)
                p, inv, _ = _head_probs(q_ref[:, lanes], kk, valid, sink_ref[h], False)
                o_ref[:, lanes] = (jnp.dot(p, vv, preferred_element_type=F32) * inv).astype(BF16)

    specs = _attn_specs(q_width, kv_width, lambda i: i)
    return pl.pallas_call(
        body, name=name, grid=(T // CHUNK,),
        in_specs=specs + [pl.BlockSpec(memory_space=pltpu.SMEM)],
        out_specs=_rows(CHUNK, q_width), out_shape=jax.ShapeDtypeStruct((T, q_width), BF16),
        compiler_params=_params("parallel"))(qk, qk, qk, qkv, qkv, sinks)


def attn_bwd(qk, qkv, do, sinks, q_width, kv_width, name):
    T = qk.shape[0]
    NB = T // CHUNK
    group = q_width // kv_width

    def body(q_ref, kc_ref, kp_ref, vc_ref, vp_ref, do_ref, sink_ref, dq_ref, dk_ref, dv_ref, ds_ref, ck_ref, cv_ref):
        i = pl.program_id(0)
        n = NB - 1 - i

        @pl.when(i == 0)
        def _():
            ck_ref[...] = jnp.zeros_like(ck_ref)
            cv_ref[...] = jnp.zeros_like(cv_ref)
            ds_ref[...] = jnp.zeros_like(ds_ref)
        lane = lax.broadcasted_iota(jnp.int32, (1, LANES), 1)
        dsink_row = jnp.zeros((1, LANES), F32)
        valid = _band_mask(n, group)
        head = lax.broadcasted_iota(jnp.int32, (group * CHUNK, 1), 0) // CHUNK
        for kh in range(kv_width // HEAD_DIM):
            kk, vv = _kv_head(kc_ref, kp_ref, vc_ref, vp_ref, kh)
            heads = [slice((kh * group + g) * HEAD_DIM, (kh * group + g + 1) * HEAD_DIM) for g in range(group)]
            q = jnp.concatenate([q_ref[:, hs] for hs in heads], axis=0)
            do = jnp.concatenate([do_ref[:, hs] for hs in heads], axis=0)
            sink = jnp.zeros((group * CHUNK, 1), F32)
            for g in range(group):
                sink = jnp.where(head == g, sink_ref[kh * group + g], sink)
            p, inv, e_sink = _head_probs(q, kk, valid, sink, True)
            p = p.astype(F32) * _lanes(inv, 2 * CHUNK)
            dp = lax.dot_general(do, vv, _NT, preferred_element_type=F32)
            delta = _row_sums((p * dp).astype(BF16))
            ds = (p * (dp - _lanes(delta, 2 * CHUNK))).astype(BF16)
            dsink = -e_sink * inv[:, :1] * delta[:, :1]
            dq = jnp.dot(ds, kk, preferred_element_type=F32)
            for g, hs in enumerate(heads):
                rows = slice(g * CHUNK, (g + 1) * CHUNK)
                dsink_row = dsink_row + jnp.where(lane == kh * group + g, jnp.sum(dsink[rows]), 0.0)
                dq_ref[:, hs] = dq[rows]
            dkk = lax.dot_general(ds, q, _TN, preferred_element_type=F32)
            dvv = lax.dot_general(p.astype(BF16), do, _TN, preferred_element_type=F32)
            lanes = slice(kh * HEAD_DIM, (kh + 1) * HEAD_DIM)
            dk_ref[:, lanes] = dkk[CHUNK:] + ck_ref[:, lanes]
            dv_ref[:, lanes] = dvv[CHUNK:] + cv_ref[:, lanes]
            ck_ref[:, lanes] = dkk[:CHUNK]
            cv_ref[:, lanes] = dvv[:CHUNK]
        ds_ref[0:1, :] += dsink_row

    order = lambda i: NB - 1 - i
    specs = _attn_specs(q_width, kv_width, order)
    kv_out = pl.BlockSpec((CHUNK, kv_width), lambda i: (order(i), 0))
    q_rows = pl.BlockSpec((CHUNK, q_width), lambda i: (order(i), 0))
    return pl.pallas_call(
        body, name=name, grid=(NB,),
        in_specs=specs + [q_rows, pl.BlockSpec(memory_space=pltpu.SMEM)],
        out_specs=[q_rows, kv_out, kv_out, _full((8, LANES))],
        out_shape=[jax.ShapeDtypeStruct((T, q_width), F32), jax.ShapeDtypeStruct((T, kv_width), F32),
                   jax.ShapeDtypeStruct((T, kv_width), F32), jax.ShapeDtypeStruct((8, LANES), F32)],
        scratch_shapes=[pltpu.VMEM((CHUNK, kv_width), F32), pltpu.VMEM((CHUNK, kv_width), F32)],
        compiler_params=_params("arbitrary"))(qk, qk, qk, qkv, qkv, do, sinks)


def _blocked(w):
    return w.reshape(N_DEV, w.shape[0] // N_DEV, w.shape[1])


def kernel(x, positions, pre_mix_g, post_mix_g, pre_ffn_g, post_ffn_g, a_w_in, a_b_in, a_ln_g, a_ln_b, a_w_s, a_b_s, a_w_out, b_w_qkv, b_b_qkv, b_sinks, b_w_o, ffn_w_gu, ffn_w_down, loss_target, m_pre_mix_g, m_post_mix_g, m_pre_ffn_g, m_post_ffn_g, m_a_w_in, m_a_b_in, m_a_ln_g, m_a_ln_b, m_a_w_s, m_a_b_s, m_a_w_out, m_b_w_qkv, m_b_b_qkv, m_b_sinks, m_b_w_o, m_ffn_w_gu, m_ffn_w_down, v_pre_mix_g, v_post_mix_g, v_pre_ffn_g, v_post_ffn_g, v_a_w_in, v_a_b_in, v_a_ln_g, v_a_ln_b, v_a_w_s, v_a_b_s, v_a_w_out, v_b_w_qkv, v_b_b_qkv, v_b_sinks, v_b_w_o, v_ffn_w_gu, v_ffn_w_down):
    weights = dict(pre_mix_g=pre_mix_g, post_mix_g=post_mix_g, pre_ffn_g=pre_ffn_g, post_ffn_g=post_ffn_g,
                   a_w_in=a_w_in, a_b_in=a_b_in, a_ln_g=a_ln_g, a_ln_b=a_ln_b, a_w_s=a_w_s, a_b_s=a_b_s,
                   a_w_out=a_w_out, b_w_qkv=b_w_qkv, b_b_qkv=b_b_qkv, b_sinks=b_sinks, b_w_o=b_w_o,
                   ffn_w_gu=ffn_w_gu, ffn_w_down=ffn_w_down)
    mom_m = dict(pre_mix_g=m_pre_mix_g, post_mix_g=m_post_mix_g, pre_ffn_g=m_pre_ffn_g, post_ffn_g=m_post_ffn_g,
                 a_w_in=m_a_w_in, a_b_in=m_a_b_in, a_ln_g=m_a_ln_g, a_ln_b=m_a_ln_b, a_w_s=m_a_w_s, a_b_s=m_a_b_s,
                 a_w_out=m_a_w_out, b_w_qkv=m_b_w_qkv, b_b_qkv=m_b_b_qkv, b_sinks=m_b_sinks, b_w_o=m_b_w_o,
                 ffn_w_gu=m_ffn_w_gu, ffn_w_down=m_ffn_w_down)
    mom_v = dict(pre_mix_g=v_pre_mix_g, post_mix_g=v_post_mix_g, pre_ffn_g=v_pre_ffn_g, post_ffn_g=v_post_ffn_g,
                 a_w_in=v_a_w_in, a_b_in=v_a_b_in, a_ln_g=v_a_ln_g, a_ln_b=v_a_ln_b, a_w_s=v_a_w_s, a_b_s=v_a_b_s,
                 a_w_out=v_a_w_out, b_w_qkv=v_b_w_qkv, b_b_qkv=v_b_b_qkv, b_sinks=v_b_sinks, b_w_o=v_b_w_o,
                 ffn_w_gu=v_ffn_w_gu, ffn_w_down=v_ffn_w_down)
    names = list(weights)
    big = ("a_w_in", "b_w_qkv", "ffn_w_gu", "a_w_out", "b_w_o", "ffn_w_down")

    T, D = x.shape[1], x.shape[2]
    depth = pre_mix_g.shape[0]
    n_heads = b_sinks.shape[1]
    q_width = n_heads * HEAD_DIM
    kv_width = N_KV_HEADS * HEAD_DIM
    G = a_w_s.shape[1]
    W = a_ln_g.shape[1]
    device = _index(_position())

    dev = device.reshape(1).astype(jnp.int32)

    def layer_keys(i):
        mixer = ("a_w_in", "a_w_out") if i % 2 == 0 else ("b_w_qkv", "b_w_o")
        return [(k, i // 2) for k in mixer] + [("ffn_w_gu", i), ("ffn_w_down", i)]

    as_view = ("b_w_qkv", "ffn_w_gu")
    in_kernel = ("a_w_in",)
    view = lambda k, t: jnp.swapaxes(t, 1, 2) if k in as_view else t
    w_view = {k: view(k, weights[k]) for k in big}
    m_view = {k: view(k, mom_m[k]) for k in big}
    v_view = {k: view(k, mom_v[k]) for k in big}

    groups_keys = [[key] for key in layer_keys(0)] + [layer_keys(i) for i in range(1, depth)]
    group_of = {}
    for g, keys in enumerate(groups_keys):
        layer = 0 if g < len(layer_keys(0)) else g - len(layer_keys(0)) + 1
        for pos, (k, _) in enumerate(keys):
            group_of[(layer, k)] = (g, pos)
    bias_land = lax.dynamic_update_slice(
        jnp.zeros((N_DEV,) + b_b_qkv.shape, F32), b_b_qkv[None], (device, 0, 0))
    h, hn = x[0], first_norm(x[0], pre_mix_g[0][None], "norm_first")
    first = len(layer_keys(0))
    in_gather, token = [], ()
    for name, lo, hi in (("gather_start_first", 0, first), ("gather_start_rest", first, len(groups_keys))):
        groups = [list(prep_weights([(w_view[k], l, k in in_kernel) for k, l in keys], dev, f"prep_{name}_{g}",
                                    after=token)) for g, keys in enumerate(groups_keys[lo:hi])]
        if lo:
            groups[0].append(bias_land)
        gather_sems, flat_lands, token = relay_start(groups, name)
        token = (token, hn)
        for grp, sems in zip(groups, gather_sems):
            in_gather.append(([flat_lands.pop(0) for _ in grp], sems))
    all_started = token[0]
    passed, gathered = {}, {}

    def pass_on(g, after):
        if g not in passed and g < len(in_gather):
            lands, sems = in_gather[g]
            passed[g], lands = relay_pass(lands, sems, after, f"gather_pass_{g}")
            in_gather[g] = (lands, sems)

    pass_on(0, (all_started,))

    def weight(i, k, *after, ahead=True):
        g, pos = group_of[(i, k)]
        if g not in gathered:
            if ahead:
                pass_on(g + 1, after)
            lands, sems = in_gather[g]
            gathered[g] = relay_wait(lands, sems, passed[g], after, f"gather_wait_{g}")
        w = gathered[g][pos]
        return w.reshape(N_DEV * w.shape[1], w.shape[2])

    causal = jnp.tril(jnp.ones((CHUNK, CHUNK), dtype=bool))
    w_mask = jnp.where(causal[None, None], a_w_s, 0.0).astype(BF16)
    w_mask_t = jnp.swapaxes(w_mask, 2, 3)
    b_s_t = jnp.swapaxes(a_b_s, 1, 2)
    group_onehot = (jnp.arange(W)[:, None] // (W // G) == jnp.arange(LANES)[None, :]).astype(BF16)
    lane = jnp.arange(LANES) % HEAD_DIM
    inv_freq = ROPE_THETA ** (-jnp.arange(0, ROPE_DIM, 2, dtype=F32) / ROPE_DIM)
    inv_freq_row = jnp.where(lane < ROPE_DIM, inv_freq[lane % (ROPE_DIM // 2)], 0.0)[None, :].astype(F32)
    tabs = rope_tables(positions.reshape(T, 1), inv_freq_row, "rope_tables")

    saved = []
    for i in range(depth):
        j = i // 2
        s = dict(h=h, hn=hn)
        post_pre = (post_mix_g[i][None], pre_ffn_g[i][None])
        if i % 2 == 0:
            s["a"] = rowmm(hn, weight(i, "a_w_in", hn, all_started), True, a_b_in[j][None], BF16, "gmlp_in")
            s["gated"], s["mix"], s["h2"], s["fn"] = gmlp_gate_out(
                s["a"], a_ln_g[j][None], a_ln_b[j][None], w_mask[j], b_s_t[j], weight(i, "a_w_out", s["a"], ahead=i > 0),
                h, *post_pre, "gmlp_gate_out")
            if i == 0:
                pass_on(group_of[(0, "ffn_w_gu")][0], (s["mix"],))
        else:
            w_qkv = weight(i, "b_w_qkv", hn)
            b_qkv_full = jnp.swapaxes(gathered[group_of[(1, "b_w_qkv")][0]][-1], 0, 1).reshape(b_b_qkv.shape[0], 1, -1)
            s["qkv"], s["qk"] = qkv_rope(hn, w_qkv, b_qkv_full[j], tabs, q_width, kv_width, "attn_qkv_rope")
            s["o"] = attn_fwd(s["qk"], s["qkv"], b_sinks[j], q_width, kv_width, "attn_core")
            s["mix"], s["h2"], s["fn"] = mm_add_norm(s["o"], weight(i, "b_w_o", s["o"]), h, *post_pre, "attn_out")
        s["act"], s["act_partials"] = ffn_up(s["fn"], weight(i, "ffn_w_gu", s["fn"]), "ffn_up")
        w_down = weight(i, "ffn_w_down", s["act"])
        saved.append(s)
        if i + 1 < depth:
            s["f"], h, hn = mm_add_norm(s["act"], w_down, s["h2"], post_ffn_g[i][None], pre_mix_g[i + 1][None], "ffn_down")
    small = {k: [None] * weights[k].shape[0] for k in names if k not in big}
    dh, df, loss_sum, small["post_ffn_g"][depth - 1] = mm_add_norm_loss(
        saved[-1]["act"], w_down, saved[-1]["h2"], post_ffn_g[depth - 1][None], loss_target[0], "ffn_down_loss")

    grads = {k: [None] * weights[k].shape[0] for k in big}
    in_flight = []

    def send_grads(keys, tag):
        sems, bufs, token = exchange_start([_blocked(grads[k][l]) for k, l in keys], "exchange_start_" + tag)
        in_flight.append((keys, sems, bufs, tag))
        return token

    replicated = [k for k in names if k not in big and k not in ("b_b_qkv", "a_w_s")]
    main = sum(weights[k].size for k in replicated)
    main_rows = -(-main // (LANES * LANES)) * LANES
    bias_size = b_b_qkv.shape[0] * N_DEV * b_b_qkv.shape[1]
    bias_rows = -(-bias_size // (8 * LANES)) * 8
    flat = lambda vals: [v.reshape(-1) for v in vals]

    def start_small_gather():
        packed = jnp.concatenate(
            flat(p for k in replicated for p in small[k])
            + [loss_sum[0, :1], jnp.zeros((main_rows * LANES - main - 1,), F32)] + flat(small["b_b_qkv"]) + [jnp.zeros((bias_rows * LANES - bias_size,), F32)]
        ).reshape(1, main_rows + bias_rows, LANES)
        spatial = jnp.stack(small["a_w_s"]).reshape(1, -1, LANES)
        return gather_start([prep_weights([(packed, 0, False), (spatial, 0, False)], dev, "place_small_grads", F32)],
                            "gather_start_small")

    sent = ()
    for i in reversed(range(depth)):
        j = i // 2
        s = saved[i]
        dgu = ffn_dact(df, weight(i, "ffn_w_down"), s["act_partials"], "ffn_dact", after=sent)
        alone = i == 0
        grads["ffn_w_down"][i] = grad_mm(s["act"], df, "ffn_down_grad")
        early = (send_grads(layer_keys(i)[3:], f"ffn_down_{i}"),) if alone else ()
        grads["ffn_w_gu"][i] = grad_mm(dgu, s["fn"], "ffn_up_grad", after=early)
        sent = send_grads(layer_keys(i)[2:3] if alone else layer_keys(i)[2:], f"ffn_{i}")
        r = mm_norm_bwd(dgu, weight(i, "ffn_w_gu"), dh, s["h2"], pre_ffn_g[i][None],
                        (s["mix"], post_mix_g[i][None]), "ffn_dx", sent)
        dh, dmix = r["dh"], r["dy"]
        small["pre_ffn_g"][i], small["post_mix_g"][i] = r["dg_pre"], r["dg_post"]
        post = (saved[i - 1]["f"], post_ffn_g[i - 1][None]) if i > 0 else None
        if i % 2 == 0:
            grads["a_w_out"][j] = grad_mm(s["gated"], dmix, "gmlp_out_grad")
            if alone:
                sent = send_grads(layer_keys(i)[1:2], f"mixer_out_{i}")
            r = gmlp_gate_bwd_dx(
                s["a"], dmix, weight(i, "a_w_out"), a_ln_g[j][None], a_ln_b[j][None], w_mask[j], w_mask_t[j], b_s_t[j],
                group_onehot, weight(i, "a_w_in"), dh, s["h"], pre_mix_g[i][None], post, "gmlp_gate_bwd_dx", sent)
            small["a_w_s"][j], small["a_b_s"][j] = r["dw_s"], r["db_s_t"][:, :G].T
            small["a_ln_g"][j], small["a_ln_b"][j], small["a_b_in"][j] = r["dln_g"][0], r["dln_b"][0], r["db_in"][0]
            small["pre_mix_g"][i] = r["dg_pre"]
            first_layer = ()
            if i == 0:
                small_gather = start_small_gather()
                first_layer = (small_gather[2],)
            grads["a_w_in"][j] = grad_mm(r["da"], s["hn"], "gmlp_in_grad", after=first_layer)
            sent = (send_grads(layer_keys(i)[:1] if alone else layer_keys(i)[:2], f"mixer_{i}"),)
        else:
            grads["b_w_o"][j] = grad_mm(s["o"], dmix, "attn_out_grad")
            do = rowmm(dmix, weight(i, "b_w_o"), True, None, BF16, "attn_do")
            dq, dk, dv, dsink = attn_bwd(s["qk"], s["qkv"], do, b_sinks[j], q_width, kv_width, "attn_core_bwd")
            dqkv, dbias = rope_bwd(dq, dk, dv, tabs, "attn_rope_bwd")
            small["b_sinks"][j], small["b_b_qkv"][j] = dsink[0, :n_heads], dbias[0]
            grads["b_w_qkv"][j] = grad_mm(dqkv, s["hn"], "attn_qkv_grad")
            sent = send_grads(layer_keys(i)[:2], f"mixer_{i}")
            r = mm_norm_bwd(dqkv, weight(i, "b_w_qkv"), dh, s["h"], pre_mix_g[i][None], post, "attn_dx", sent)
            sent = ()
        dh, small["pre_mix_g"][i] = r["dh"], r["dg_pre"]
        if i > 0:
            df, small["post_ffn_g"][i - 1] = r["dy"], r["dg_post"]
    grad_x = dh[None]

    small_sems, small_lands, small_sent = small_gather
    pack_main = lambda src: jnp.concatenate(
        flat(src[k] for k in replicated) + [jnp.ones((main_rows * LANES - main,), F32)]).reshape(main_rows, LANES)
    packed_w, packed_m, packed_v = pack_main(weights), pack_main(mom_m), pack_main(mom_v)

    stacked = {k: [lax.empty(w_view[k].shape, F32) for _ in range(4)] for k in big}

    def update_group(entry, after):
        keys, sems, bufs, tag = entry
        partials, landed = exchange_wait(bufs, sems, after, "exchange_wait_" + tag)
        for (k, l), part, land in zip(keys, partials, landed):
            stacked[k] = sum_adamw(part, land, w_view[k], m_view[k], v_view[k], stacked[k], l,
                                   k in in_kernel, dev, "adamw_" + k)
        return tuple(stacked[k][0] for k, _ in keys)

    after = (small_sent, dh) + sent
    for entry in in_flight[:-1]:
        after = update_group(entry, after)
    out_g, out_d, out_m, out_v = {}, {}, {}, {}

    gathered_small, gathered_spatial = gather_wait(small_lands, small_sems[0], after + (packed_w, packed_m, packed_v),
                                                   "gather_wait_small")
    rows_of = lambda t: t.reshape(-1, LANES)
    spatial = sum_adamw_small(gathered_spatial, rows_of(a_w_s), rows_of(m_a_w_s), rows_of(v_a_w_s), "adamw_a_w_s")
    out_g["a_w_s"], out_d["a_w_s"], out_m["a_w_s"], out_v["a_w_s"] = [t.reshape(a_w_s.shape) for t in spatial]
    results = sum_adamw_small(gathered_small, packed_w, packed_m, packed_v, "adamw_small")
    loss = results[0].reshape(-1)[main] * (0.5 / D)
    off = 0
    for k in replicated:
        size, shape = weights[k].size, weights[k].shape
        out_g[k], out_d[k], out_m[k], out_v[k] = [t.reshape(-1)[off:off + size].reshape(shape) for t in results]
        off += size
    n_local = b_b_qkv.shape[1]
    bias_sum = sum_parts(gathered_small[:, main_rows:], "sum_bias_grads").reshape(-1)[:bias_size]
    out_g["b_b_qkv"] = lax.dynamic_slice_in_dim(
        bias_sum.reshape(b_b_qkv.shape[0], -1), device * n_local, n_local, axis=1)
    pad = lambda t: jnp.concatenate([t.reshape(-1), jnp.ones((8 * LANES - t.size,), F32)]).reshape(8, LANES)
    bias_update = adamw(pad(b_b_qkv), pad(out_g["b_b_qkv"]), pad(m_b_b_qkv), pad(v_b_b_qkv), "adamw_bias")
    out_d["b_b_qkv"], out_m["b_b_qkv"], out_v["b_b_qkv"] = [
        t.reshape(-1)[:b_b_qkv.size].reshape(b_b_qkv.shape) for t in bias_update]

    update_group(in_flight[-1], (spatial[0], results[0], bias_update[0]))
    for k in big:
        out_g[k], out_d[k], out_m[k], out_v[k] = [view(k, t) for t in stacked[k]]

    return (loss, grad_x, *[out_g[k] for k in names], *[out_d[k] for k in names],
            *[out_m[k] for k in names], *[out_v[k] for k in names])
```

```python
import math

import jax
import jax.numpy as jnp
from jax import lax
from jax.experimental import pallas as pl
from jax.experimental.pallas import tpu as pltpu

F32, BF16 = jnp.float32, jnp.bfloat16
MESH = pl.DeviceIdType.MESH
N_DEV = 8

CHUNK = 128
HEAD_DIM = 64
N_KV_HEADS = 4
ROPE_DIM = HEAD_DIM // 4
ROPE_THETA = 500000.0
RMS_EPS = 1e-6
LN_EPS = 1e-5
NEG_INF = -1e30

ADAM_LR = 0.001
ADAM_B1 = 0.9
ADAM_B2 = 0.999
ADAM_EPS = 1e-08
ADAM_WD = 0.01
ADAM_STEP = 10

V7X_VMEM_BYTES = 64 * 2 ** 20
VMEM_LIMIT = V7X_VMEM_BYTES - 8 * 2 ** 20
LANES = 128

ROW_TILE = 512
COL_TILE = 512
GRAD_TILE_ROWS = 1408
GRAD_TILE_DEPTH = 2048
ADAM_TILE = 256


def _params(*sem):
    return pltpu.CompilerParams(dimension_semantics=sem or None, vmem_limit_bytes=VMEM_LIMIT)


def _tile(n, pref):
    if n <= pref:
        return n
    t = pref - pref % LANES
    while t >= LANES:
        if n % t == 0:
            return t
        t -= LANES
    return n


def _full(shape):
    return pl.BlockSpec(shape, lambda *_: (0,) * len(shape))


def _resident(shape):
    return pl.BlockSpec(shape, lambda *_: (0,) * len(shape), pipeline_mode=pl.Buffered(1))


def _rows(tm, width):
    return pl.BlockSpec((tm, width), lambda i: (i, 0))


def _rms(x, g):
    r = lax.rsqrt(jnp.mean(x * x, axis=-1, keepdims=True) + RMS_EPS)
    return x * r * g


def _rms_bwd(x, g, dy):
    r = lax.rsqrt(jnp.mean(x * x, axis=-1, keepdims=True) + RMS_EPS)
    xhat = x * r
    dg = jnp.sum(dy * xhat, axis=0, keepdims=True)
    dxhat = dy * g
    dx = r * (dxhat - xhat * jnp.mean(dxhat * xhat, axis=-1, keepdims=True))
    return dx, dg


_INV_SQRT2 = 1.0 / math.sqrt(2.0)
_INV_SQRT2PI = 1.0 / math.sqrt(2.0 * math.pi)


def _gelu(x):
    return 0.5 * x * (1.0 + lax.erf(x * _INV_SQRT2))


def _gelu_and_grad(x):
    cdf = 0.5 * (1.0 + lax.erf(x * _INV_SQRT2))
    return x * cdf, cdf + x * jnp.exp(-0.5 * x * x) * _INV_SQRT2PI


def _sigmoid(x):
    return 0.5 * jnp.tanh(0.5 * x) + 0.5


def _position():
    return lax.axis_index("x"), lax.axis_index("y"), lax.axis_index("c")


def _index(p):
    return 4 * p[0] + 2 * p[1] + p[2]


_HBM = pl.BlockSpec(memory_space=pltpu.HBM)
_SEM = pl.BlockSpec(memory_space=pltpu.SEMAPHORE)
_ORDERED_BY_DATA = pltpu.SideEffectType.DATAFLOW_SIDE_EFFECTING


def _in_hbm(v):
    return pltpu.with_memory_space_constraint(v, pltpu.HBM)


def _peer(k):
    x, y, c = _position()
    return (x ^ ((k >> 2) & 1), y ^ ((k >> 1) & 1), c ^ (k & 1))


def _split_copies(srcs, dsts, send, recv):
    return [pltpu.make_async_remote_copy(
        src_ref=src(k), dst_ref=dst(k), send_sem=send.at[7 * a + k - 1], recv_sem=recv.at[7 * a + k - 1],
        device_id=_peer(k), device_id_type=MESH)
        for a, (src, dst) in enumerate(zip(srcs, dsts)) for k in range(1, N_DEV)]


def _start_call(groups, sent, copies_of, name, per_array=N_DEV - 1):
    flat = [v for grp in groups for v in grp]
    n, ng = len(flat), len(groups)

    def body(*refs):
        bufs, sems = refs[:n], refs[n:n + 2 * ng]
        q = 0
        for g, grp in enumerate(groups):
            for cp in copies_of(bufs[q:q + len(grp)], sems[2 * g], sems[2 * g + 1]):
                cp.start()
            q += len(grp)
        refs[-1][...] = jnp.zeros_like(refs[-1])

    sem_shapes = []
    for count in sent:
        sem_shapes += [pltpu.SemaphoreType.DMA((per_array * count,)) for _ in range(2)]
    res = pl.pallas_call(
        body, name=name,
        out_shape=sem_shapes + [pltpu.HBM(v.shape, v.dtype) for v in flat] + [jax.ShapeDtypeStruct((8, LANES), F32)],
        in_specs=[_HBM] * n,
        out_specs=[_SEM] * (2 * ng) + [_HBM] * n + [pl.BlockSpec(memory_space=pltpu.VMEM)],
        input_output_aliases={i: 2 * ng + i for i in range(n)},
        compiler_params=pltpu.CompilerParams(has_side_effects=_ORDERED_BY_DATA),
    )(*[_in_hbm(v) for v in flat])
    sems = [(res[2 * g], res[2 * g + 1]) for g in range(ng)]
    return sems, list(res[2 * ng:-1]), res[-1]


def _wait_call(bufs, sems, copies_of, after, name):
    n = len(bufs)

    def body(*refs):
        for cp in copies_of(refs[:n], refs[n], refs[n + 1]):
            cp.wait_send()
            cp.wait_recv()

    return list(pl.pallas_call(
        body, name=name,
        out_shape=[pltpu.HBM(v.shape, v.dtype) for v in bufs],
        in_specs=[_HBM] * n + [_SEM, _SEM] + [pl.BlockSpec(memory_space=pl.ANY)] * len(after),
        out_specs=[_HBM] * n,
        input_output_aliases={i: i for i in range(n)},
        compiler_params=pltpu.CompilerParams(has_side_effects=_ORDERED_BY_DATA),
    )(*bufs, sems[0], sems[1], *after))


def _gather_copies(lands, send, recv):
    me = _index(_position())
    mine = [lambda k, ref=ref: ref.at[pl.ds(me, 1)] for ref in lands]
    return _split_copies(mine, mine, send, recv)


def _gather_arrivals(lands, send, recv):
    me = _index(_position())
    mine = [lambda k, ref=ref: ref.at[pl.ds(me, 1)] for ref in lands]
    theirs = [lambda k, ref=ref: ref.at[pl.ds(_index(_peer(k)), 1)] for ref in lands]
    return _split_copies(mine, theirs, send, recv)


def _exchange_copies(bufs, send, recv):
    half = len(bufs) // 2
    srcs = [lambda k, ref=ref: ref.at[pl.ds(_index(_peer(k)), 1)] for ref in bufs[:half]]
    dsts = [lambda k, ref=ref: ref.at[pl.ds(k - 1, 1)] for ref in bufs[half:]]
    return _split_copies(srcs, dsts, send, recv)


def gather_start(groups, name):
    return _start_call(groups, [len(grp) for grp in groups], _gather_copies, name)


def gather_wait(lands, sems, after, name):
    return _wait_call(lands, sems, _gather_arrivals, after, name)


_DIRECT = (1, 2, 4, 6)
_PASSED = (3, 5, 7)


def _remote(src, dst, send, recv, q, k):
    return pltpu.make_async_remote_copy(src_ref=src, dst_ref=dst, send_sem=send.at[q], recv_sem=recv.at[q],
                                        device_id=_peer(k), device_id_type=MESH)


def _slot_of(land, k):
    return land.at[pl.ds(_index(_peer(k)), 1)]


def relay_start(groups, name):
    def copies(lands, send, recv):
        return [_remote(_slot_of(land, 0), _slot_of(land, 0), send, recv, len(_DIRECT) * a + j, k)
                for a, land in enumerate(lands) for j, k in enumerate(_DIRECT)]
    return _start_call(groups, [len(grp) for grp in groups], copies, name, per_array=len(_DIRECT))


def relay_pass(lands, sems, after, name):
    n = len(lands)

    def body(*refs):
        bufs, recv_first = refs[:n], refs[n]
        send, recv = refs[n + 1 + len(after):n + 3 + len(after)]
        for a, land in enumerate(bufs):
            for j, k in enumerate(_PASSED):
                came = _slot_of(land, k ^ 1)
                _remote(came, came, recv_first, recv_first, len(_DIRECT) * a + _DIRECT.index(k ^ 1), k ^ 1).wait_recv()
                _remote(came, came, send, recv, len(_PASSED) * a + j, 1).start()

    res = pl.pallas_call(
        body, name=name,
        out_shape=[pltpu.SemaphoreType.DMA((len(_PASSED) * n,))] * 2 + [pltpu.HBM(v.shape, v.dtype) for v in lands],
        in_specs=[_HBM] * n + [_SEM] + [pl.BlockSpec(memory_space=pl.ANY)] * len(after),
        out_specs=[_SEM, _SEM] + [_HBM] * n,
        input_output_aliases={i: 2 + i for i in range(n)},
        compiler_params=pltpu.CompilerParams(has_side_effects=_ORDERED_BY_DATA),
    )(*lands, sems[1], *after)
    return (res[0], res[1]), list(res[2:])


def relay_wait(lands, first, second, after, name):
    n = len(lands)

    def body(*refs):
        bufs = refs[:n]
        send_first, recv_first, send, recv = refs[n:n + 4]
        for a, land in enumerate(bufs):
            mine = _slot_of(land, 0)
            _remote(mine, _slot_of(land, 1), send_first, recv_first, len(_DIRECT) * a, 1).wait_recv()
            for j, k in enumerate(_DIRECT):
                _remote(mine, mine, send_first, recv_first, len(_DIRECT) * a + j, k).wait_send()
            for j, k in enumerate(_PASSED):
                cp = _remote(_slot_of(land, k ^ 1), _slot_of(land, k), send, recv, len(_PASSED) * a + j, 1)
                cp.wait_send()
                cp.wait_recv()

    return list(pl.pallas_call(
        body, name=name,
        out_shape=[pltpu.HBM(v.shape, v.dtype) for v in lands],
        in_specs=[_HBM] * n + [_SEM] * 4 + [pl.BlockSpec(memory_space=pl.ANY)] * len(after),
        out_specs=[_HBM] * n,
        input_output_aliases={i: i for i in range(n)},
        compiler_params=pltpu.CompilerParams(has_side_effects=_ORDERED_BY_DATA),
    )(*lands, first[0], first[1], second[0], second[1], *after))


def exchange_start(partials, name):
    lands = [lax.empty((N_DEV - 1,) + p.shape[1:], p.dtype) for p in partials]
    sems, bufs, token = _start_call([list(partials) + lands], [len(partials)], _exchange_copies, name)
    return sems[0], bufs, token


def exchange_wait(bufs, sems, after, name):
    bufs = _wait_call(bufs, sems, _exchange_copies, after, name)
    return bufs[:len(bufs) // 2], bufs[len(bufs) // 2:]


def prep_weights(items, dev, name, dtype=BF16, after=()):
    n = len(items)
    blocks = [(1, w.shape[2], w.shape[1]) if t else (1,) + w.shape[1:] for w, _, t in items]

    def body(d_ref, *refs):
        for (_, _, transpose), w_ref, o_ref in zip(items, refs[:n], refs[-n:]):
            v = w_ref[0]
            o_ref[0] = (v.T if transpose else v).astype(dtype)

    return pl.pallas_call(
        body, name=name,
        grid_spec=pltpu.PrefetchScalarGridSpec(
            num_scalar_prefetch=1, grid=(1,),
            in_specs=[pl.BlockSpec((1,) + w.shape[1:], lambda i, d, layer=layer: (layer, 0, 0)) for w, layer, _ in items]
            + [pl.BlockSpec(memory_space=pl.ANY)] * len(after),
            out_specs=[pl.BlockSpec(blk, lambda i, d: (d[0], 0, 0)) for blk in blocks]),
        out_shape=[jax.ShapeDtypeStruct((N_DEV,) + blk[1:], dtype) for blk in blocks],
        compiler_params=_params("arbitrary"),
    )(dev, *[w for w, _, _ in items], *after)


def _adamw_math(w, g, m, v):
    mn = ADAM_B1 * m + (1.0 - ADAM_B1) * g
    vn = ADAM_B2 * v + (1.0 - ADAM_B2) * (g * g)
    m_hat = mn * (1.0 / (1.0 - ADAM_B1 ** ADAM_STEP))
    v_hat = vn * (1.0 / (1.0 - ADAM_B2 ** ADAM_STEP))
    return -ADAM_LR * (m_hat / (jnp.sqrt(v_hat) + ADAM_EPS) + ADAM_WD * w), mn, vn


def sum_adamw(partial, landed, w, m, v, prev, layer, transpose, dev, name):
    _, r, c = partial.shape
    if transpose:
        tc = _tile(c, ADAM_TILE)
        grid = (c // tc,)
        part_spec = pl.BlockSpec((1, r, tc), lambda i, d: (d[0], 0, i))
        land_spec = pl.BlockSpec((N_DEV - 1, r, tc), lambda i, d: (0, 0, i))
        w_spec = pl.BlockSpec((1, tc, r), lambda i, d: (layer, i, 0))
    else:
        tr = r // 2 if r % 32 == 0 else r
        grid = (r // tr,)
        part_spec = pl.BlockSpec((1, tr, c), lambda i, d: (d[0], i, 0))
        land_spec = pl.BlockSpec((N_DEV - 1, tr, c), lambda i, d: (0, i, 0))
        w_spec = pl.BlockSpec((1, tr, c), lambda i, d: (layer, i, 0))

    def body(d_ref, p_ref, l_ref, w_ref, m_ref, v_ref, *rest):
        g_ref, dl_ref, mo_ref, vo_ref = rest[-4:]
        acc = p_ref[0].astype(F32)
        for k in range(N_DEV - 1):
            acc = acc + l_ref[k].astype(F32)
        g = acc.T if transpose else acc
        delta, mn, vn = _adamw_math(w_ref[0], g, m_ref[0], v_ref[0])
        g_ref[0], dl_ref[0], mo_ref[0], vo_ref[0] = g, delta, mn, vn

    return pl.pallas_call(
        body, name=name,
        grid_spec=pltpu.PrefetchScalarGridSpec(
            num_scalar_prefetch=1, grid=grid,
            in_specs=[part_spec, land_spec, w_spec, w_spec, w_spec] + [pl.BlockSpec(memory_space=pl.ANY)] * 4,
            out_specs=[w_spec] * 4),
        out_shape=[jax.ShapeDtypeStruct(w.shape, F32)] * 4,
        input_output_aliases={6 + q: q for q in range(4)},
        compiler_params=_params("parallel"),
    )(dev, partial, landed, w, m, v, *prev)


def sum_parts(parts, name):
    P, r, c = parts.shape

    def body(p_ref, o_ref):
        acc = p_ref[0].astype(F32)
        for s in range(1, P):
            acc = acc + p_ref[s].astype(F32)
        o_ref[...] = acc

    return pl.pallas_call(
        body, name=name, in_specs=[_full((P, r, c))], out_specs=_full((r, c)), grid=(1,),
        out_shape=jax.ShapeDtypeStruct((r, c), F32), compiler_params=_params("arbitrary"))(parts)


def adamw(w, g, m, v, name):
    R, C = w.shape
    tr = _tile(R, ROW_TILE)

    def body(w_ref, g_ref, m_ref, v_ref, d_ref, mo_ref, vo_ref):
        d_ref[...], mo_ref[...], vo_ref[...] = _adamw_math(w_ref[...], g_ref[...], m_ref[...], v_ref[...])

    spec = pl.BlockSpec((tr, C), lambda i: (i, 0))
    return pl.pallas_call(
        body, name=name, grid=(R // tr,),
        in_specs=[spec] * 4, out_specs=[spec] * 3,
        out_shape=[jax.ShapeDtypeStruct((R, C), F32)] * 3,
        compiler_params=_params("parallel"),
    )(w, g, m, v)


def sum_adamw_small(gathered, w, m, v, name):
    rows = w.shape[0]
    tr = _tile(rows, ROW_TILE)

    def body(p_ref, w_ref, m_ref, v_ref, g_ref, d_ref, mo_ref, vo_ref):
        g = p_ref[0]
        for s in range(1, N_DEV):
            g = g + p_ref[s]
        g_ref[...] = g
        d_ref[...], mo_ref[...], vo_ref[...] = _adamw_math(w_ref[...], g, m_ref[...], v_ref[...])

    spec = pl.BlockSpec((tr, LANES), lambda i: (i, 0))
    return pl.pallas_call(
        body, name=name, grid=(rows // tr,),
        in_specs=[pl.BlockSpec((N_DEV, tr, LANES), lambda i: (0, i, 0)), spec, spec, spec], out_specs=[spec] * 4,
        out_shape=[jax.ShapeDtypeStruct((rows, LANES), F32)] * 4, compiler_params=_params("parallel"))(gathered, w, m, v)


def first_norm(h, g, name):
    T, D = h.shape
    tm = _tile(T, ROW_TILE)

    def body(h_ref, g_ref, hn_ref):
        hn_ref[...] = _rms(h_ref[...], g_ref[...]).astype(BF16)

    row = _rows(tm, D)
    return pl.pallas_call(
        body, name=name, grid=(T // tm,), in_specs=[row, _full((1, D))], out_specs=row,
        out_shape=jax.ShapeDtypeStruct((T, D), BF16), compiler_params=_params("parallel"))(h, g)


_NT = (((1,), (1,)), ((), ()))
_TN = (((0,), (0,)), ((), ()))


def rowmm(a, w, transposed_w, bias, out_dtype, name, after=None):
    M, K = a.shape
    N = w.shape[0] if transposed_w else w.shape[1]
    tm = _tile(M, ROW_TILE)
    tn = _tile(N, COL_TILE)

    def body(*refs):
        a_ref, w_ref = refs[:2]
        o_ref = refs[-1]
        av = a_ref[...]
        for j in range(N // tn):
            cols = pl.ds(j * tn, tn)
            if transposed_w:
                acc = lax.dot_general(av, w_ref[cols, :], _NT, preferred_element_type=F32)
            else:
                acc = jnp.dot(av, w_ref[:, cols], preferred_element_type=F32)
            if bias is not None:
                acc = acc + refs[2][:, cols]
            o_ref[:, cols] = acc.astype(out_dtype)

    ins, in_specs = [a, w], [_rows(tm, K), _resident(w.shape)]
    if bias is not None:
        ins.append(bias)
        in_specs.append(_full((1, N)))
    if after is not None:
        ins.append(after)
        in_specs.append(pl.BlockSpec(memory_space=pl.ANY))
    return pl.pallas_call(
        body, name=name, grid=(M // tm,), in_specs=in_specs, out_specs=_rows(tm, N),
        out_shape=jax.ShapeDtypeStruct((M, N), out_dtype), compiler_params=_params("parallel"))(*ins)


def _rows_times_weight(a_ref, w_ref, acc_ref):
    N = w_ref.shape[1]
    tn = _tile(N, COL_TILE)
    av = a_ref[...]
    for j in range(N // tn):
        cols = pl.ds(j * tn, tn)
        acc_ref[:, cols] = jnp.dot(av, w_ref[:, cols], preferred_element_type=F32)


def mm_add_norm(a, w, h, g_post, g_pre, name):
    T, K = a.shape
    D = w.shape[1]
    tm = _tile(T, ROW_TILE)

    def body(a_ref, w_ref, h_ref, gp_ref, g_ref, y_ref, ho_ref, hn_ref, acc_ref):
        _rows_times_weight(a_ref, w_ref, acc_ref)
        y = acc_ref[...]
        y_ref[...] = y.astype(BF16)
        hv = h_ref[...] + _rms(y, gp_ref[...])
        ho_ref[...] = hv
        hn_ref[...] = _rms(hv, g_ref[...]).astype(BF16)

    row, vec = _rows(tm, D), _full((1, D))
    return pl.pallas_call(
        body, name=name, grid=(T // tm,), in_specs=[_rows(tm, K), _resident(w.shape), row, vec, vec],
        out_specs=[row, row, row],
        out_shape=[jax.ShapeDtypeStruct((T, D), BF16), jax.ShapeDtypeStruct((T, D), F32),
                   jax.ShapeDtypeStruct((T, D), BF16)],
        scratch_shapes=[pltpu.VMEM((tm, D), F32)], compiler_params=_params("parallel"))(a, w, h, g_post, g_pre)


def mm_add_norm_loss(a, w, h, g_post, target, name):
    T, K = a.shape
    D = w.shape[1]
    tm = _tile(T, ROW_TILE)

    def body(a_ref, w_ref, h_ref, gp_ref, t_ref, dh_ref, dy_ref, loss_ref, dgp_ref, acc_ref):
        @pl.when(pl.program_id(0) == 0)
        def _():
            loss_ref[...] = jnp.zeros_like(loss_ref)
            dgp_ref[...] = jnp.zeros_like(dgp_ref)
        _rows_times_weight(a_ref, w_ref, acc_ref)
        y = acc_ref[...]
        err = h_ref[...] + _rms(y, gp_ref[...]) - t_ref[...]
        dh = err * (1.0 / D)
        dh_ref[...] = dh
        loss_ref[...] += jnp.sum(err * err)
        dy, dgp = _rms_bwd(y, gp_ref[...], dh)
        dy_ref[...] = dy.astype(BF16)
        dgp_ref[...] += dgp

    row, vec = _rows(tm, D), _full((1, D))
    return pl.pallas_call(
        body, name=name, grid=(T // tm,), in_specs=[_rows(tm, K), _resident(w.shape), row, vec, row],
        out_specs=[row, row, _full((8, LANES)), vec],
        out_shape=[jax.ShapeDtypeStruct((T, D), F32), jax.ShapeDtypeStruct((T, D), BF16),
                   jax.ShapeDtypeStruct((8, LANES), F32), jax.ShapeDtypeStruct((1, D), F32)],
        scratch_shapes=[pltpu.VMEM((tm, D), F32)], compiler_params=_params("arbitrary"))(a, w, h, g_post, target)


def mm_norm_bwd(a, w, dh_in, x, g_pre, post, name, after):
    T, K = a.shape
    D = w.shape[1]
    tm = _tile(T, ROW_TILE)
    row, vec = _rows(tm, D), _full((1, D))
    ins, in_specs = [a, w, dh_in, x, g_pre], [_rows(tm, K), _resident(w.shape), row, row, vec]
    outs = [jax.ShapeDtypeStruct((T, D), F32), jax.ShapeDtypeStruct((1, D), F32)]
    out_specs, names = [row, vec], ["dh", "dg_pre"]
    if post is not None:
        ins += list(post)
        in_specs += [row, vec]
        outs += [jax.ShapeDtypeStruct((T, D), BF16), jax.ShapeDtypeStruct((1, D), F32)]
        out_specs += [row, vec]
        names += ["dy", "dg_post"]
    n_in = len(ins) + 1

    def body(*refs):
        a_ref, w_ref, dh_ref_in, x_ref, g_ref = refs[:5]
        out_refs, acc_ref = refs[n_in:-1], refs[-1]
        first = pl.program_id(0) == 0
        _rows_times_weight(a_ref, w_ref, acc_ref)
        dx, dg = _rms_bwd(x_ref[...], g_ref[...], acc_ref[...])
        dh = dh_ref_in[...] + dx
        out_refs[0][...] = dh

        @pl.when(first)
        def _():
            for r in out_refs[1::2]:
                r[...] = jnp.zeros_like(r)
        out_refs[1][...] += dg
        if post is not None:
            y_ref, gp_ref = refs[5:7]
            dy, dgp = _rms_bwd(y_ref[...].astype(F32), gp_ref[...], dh)
            out_refs[2][...] = dy.astype(BF16)
            out_refs[3][...] += dgp

    res = pl.pallas_call(
        body, name=name, grid=(T // tm,), in_specs=in_specs + [pl.BlockSpec(memory_space=pl.ANY)],
        out_specs=out_specs, out_shape=outs, scratch_shapes=[pltpu.VMEM((tm, D), F32)],
        compiler_params=_params("arbitrary"))(*ins, after)
    return dict(zip(names, res))


def grad_mm(a, b, name, after=()):
    T, N = a.shape
    K = b.shape[1]
    tn = _tile(N, GRAD_TILE_ROWS)
    tt = _tile(T, GRAD_TILE_DEPTH)

    def body(a_ref, b_ref, *rest):
        o_ref, acc_ref = rest[-2:]
        t = pl.program_id(1)

        @pl.when(t == 0)
        def _():
            acc_ref[...] = jnp.zeros_like(acc_ref)
        acc_ref[...] += lax.dot_general(a_ref[...], b_ref[...], _TN, preferred_element_type=F32)

        @pl.when(t == pl.num_programs(1) - 1)
        def _():
            o_ref[...] = acc_ref[...].astype(BF16)

    return pl.pallas_call(
        body, name=name, grid=(N // tn, T // tt),
        in_specs=[pl.BlockSpec((tt, tn), lambda j, t: (t, j)), pl.BlockSpec((tt, K), lambda j, t: (t, 0))]
        + [pl.BlockSpec(memory_space=pl.ANY)] * len(after),
        out_specs=pl.BlockSpec((tn, K), lambda j, t: (j, 0)),
        out_shape=jax.ShapeDtypeStruct((N, K), BF16),
        scratch_shapes=[pltpu.VMEM((tn, K), F32)],
        compiler_params=_params("parallel", "arbitrary"))(a, b, *after)


def ffn_up(fn, w_gu_t, name):
    T, D = fn.shape
    F = w_gu_t.shape[0] // 2
    tm = _tile(T, ROW_TILE)
    tn = _tile(F, COL_TILE)

    def body(a_ref, w_ref, act_ref, part_ref):
        av = a_ref[...]
        for j in range(F // tn):
            g = lax.dot_general(av, w_ref[pl.ds(j * tn, tn), :], _NT, preferred_element_type=F32)
            up = lax.dot_general(av, w_ref[pl.ds(F + j * tn, tn), :], _NT, preferred_element_type=F32)
            sg = _sigmoid(g)
            silu = g * sg
            act_ref[:, pl.ds(j * tn, tn)] = (silu * up).astype(BF16)
            part_ref[:, pl.ds(j * tn, tn)] = (up * (sg + silu * (1.0 - sg))).astype(BF16)
            part_ref[:, pl.ds(F + j * tn, tn)] = silu.astype(BF16)

    return pl.pallas_call(
        body, name=name, grid=(T // tm,), in_specs=[_rows(tm, D), _resident(w_gu_t.shape)],
        out_specs=[_rows(tm, F), _rows(tm, 2 * F)],
        out_shape=[jax.ShapeDtypeStruct((T, F), BF16), jax.ShapeDtypeStruct((T, 2 * F), BF16)],
        compiler_params=_params("parallel"))(fn, w_gu_t)


def ffn_dact(df, w_down, partials, name, after=()):
    T, D = df.shape
    F = w_down.shape[0]
    tm = _tile(T, ROW_TILE)
    tn = _tile(F, COL_TILE)

    def body(d_ref, w_ref, part_ref, *rest):
        o_ref = rest[-1]
        dv = d_ref[...]
        for j in range(F // tn):
            dact = lax.dot_general(dv, w_ref[pl.ds(j * tn, tn), :], _NT, preferred_element_type=F32)
            for cols in (pl.ds(j * tn, tn), pl.ds(F + j * tn, tn)):
                o_ref[:, cols] = (dact * part_ref[:, cols].astype(F32)).astype(BF16)

    return pl.pallas_call(
        body, name=name, grid=(T // tm,),
        in_specs=[_rows(tm, D), _resident(w_down.shape), _rows(tm, 2 * F)] + [pl.BlockSpec(memory_space=pl.ANY)] * len(after),
        out_specs=_rows(tm, 2 * F), out_shape=jax.ShapeDtypeStruct((T, 2 * F), BF16),
        compiler_params=_params("parallel"))(df, w_down, partials, *after)


def _layernorm_stats(v):
    mu = jnp.mean(v, axis=-1, keepdims=True)
    cen = v - mu
    rstd = lax.rsqrt(jnp.mean(cen * cen, axis=-1, keepdims=True) + LN_EPS)
    return cen * rstd, rstd


GMLP_ROWS = 2 * CHUNK


def gmlp_gate_out(a, ln_g, ln_b, w_mask, b_s_t, w_out, h, g_post, g_pre, name):
    T, W2 = a.shape
    W = W2 // 2
    G = w_mask.shape[0]
    C = W // G
    D = w_out.shape[1]
    tm = min(T, GMLP_ROWS)

    def body(a_ref, lg_ref, lb_ref, w_ref, bs_ref, wo_ref, h_ref, gp_ref, g_ref,
             gated_ref, y_ref, ho_ref, hn_ref, acc_ref):
        for c in range(tm // CHUNK):
            rows = pl.ds(c * CHUNK, CHUNK)
            xhat, _ = _layernorm_stats(_gelu(a_ref[rows, W:].astype(F32)))
            vln = (xhat * lg_ref[...] + lb_ref[...]).astype(BF16)
            for g in range(G):
                cols = pl.ds(g * C, C)
                sv = jnp.dot(w_ref[g], vln[:, g * C:(g + 1) * C], preferred_element_type=F32) + bs_ref[:, g:g + 1]
                gated_ref[rows, cols] = (_gelu(a_ref[rows, cols].astype(F32)) * sv).astype(BF16)
        _rows_times_weight(gated_ref, wo_ref, acc_ref)
        y = acc_ref[...]
        y_ref[...] = y.astype(BF16)
        hv = h_ref[...] + _rms(y, gp_ref[...])
        ho_ref[...] = hv
        hn_ref[...] = _rms(hv, g_ref[...]).astype(BF16)

    row, vec = _rows(tm, D), _full((1, D))
    return pl.pallas_call(
        body, name=name, grid=(T // tm,),
        in_specs=[_rows(tm, W2), _full((1, W)), _full((1, W)), _full(w_mask.shape), _full(b_s_t.shape),
                  _resident(w_out.shape), row, vec, vec],
        out_specs=[_rows(tm, W), row, row, row],
        out_shape=[jax.ShapeDtypeStruct((T, W), BF16), jax.ShapeDtypeStruct((T, D), BF16),
                   jax.ShapeDtypeStruct((T, D), F32), jax.ShapeDtypeStruct((T, D), BF16)],
        scratch_shapes=[pltpu.VMEM((tm, D), F32)],
        compiler_params=_params("parallel"))(a, ln_g, ln_b, w_mask, b_s_t, w_out, h, g_post, g_pre)


def gmlp_gate_bwd_dx(a, dmix, w_out, ln_g, ln_b, w_mask, w_mask_t, b_s_t, group_onehot, w_in_t,
                     dh_in, x, g_pre, post, name, after):
    T, W2 = a.shape
    W = W2 // 2
    G = w_mask.shape[0]
    C = W // G
    D = w_out.shape[1]
    tm = min(T, GMLP_ROWS)
    tn = _tile(W, COL_TILE)
    row, vec = _rows(tm, D), _full((1, D))
    ins = [a, dmix, w_out, ln_g, ln_b, w_mask, w_mask_t, b_s_t, group_onehot, w_in_t, dh_in, x, g_pre]
    in_specs = [_rows(tm, W2), row, _resident(w_out.shape), _full((1, W)), _full((1, W)), _full(w_mask.shape),
                _full(w_mask_t.shape), _full(b_s_t.shape), _full(group_onehot.shape), _resident(w_in_t.shape), row, row, vec]
    names = ["da", "dw_s", "db_s_t", "dln_g", "dln_b", "db_in", "dh", "dg_pre"]
    outs = [jax.ShapeDtypeStruct((T, W2), BF16), jax.ShapeDtypeStruct((G, CHUNK, CHUNK), F32),
            jax.ShapeDtypeStruct((CHUNK, LANES), F32), jax.ShapeDtypeStruct((1, W), F32),
            jax.ShapeDtypeStruct((1, W), F32), jax.ShapeDtypeStruct((1, W2), F32),
            jax.ShapeDtypeStruct((T, D), F32), jax.ShapeDtypeStruct((1, D), F32)]
    out_specs = [_rows(tm, W2), _full((G, CHUNK, CHUNK)), _full((CHUNK, LANES)), _full((1, W)), _full((1, W)),
                 _full((1, W2)), row, vec]
    if post is not None:
        ins += list(post)
        in_specs += [row, vec]
        names += ["dy", "dg_post"]
        outs += [jax.ShapeDtypeStruct((T, D), BF16), jax.ShapeDtypeStruct((1, D), F32)]
        out_specs += [row, vec]
    n_in = len(ins) + 1

    def body(*refs):
        (a_ref, dm_ref, wo_ref, lg_ref, lb_ref, w_ref, wt_ref, bs_ref, e_ref, wi_ref, dh_in_ref, x_ref, gpre_ref) = refs[:13]
        out_refs = refs[n_in:-3]
        da_ref, dws_ref, dbs_ref, dlg_ref, dlb_ref, dbin_ref, dh_ref, dgpre_ref = out_refs[:8]
        dgated_ref, dvln_ref, acc_ref = refs[-3:]

        @pl.when(pl.program_id(0) == 0)
        def _():
            for r in out_refs[1:6] + out_refs[7::2]:
                r[...] = jnp.zeros_like(r)
        dm = dm_ref[...]
        for j in range(W // tn):
            cols = pl.ds(j * tn, tn)
            dgated_ref[:, cols] = lax.dot_general(dm, wo_ref[cols, :], _NT, preferred_element_type=F32)
        causal = (lax.broadcasted_iota(jnp.int32, (CHUNK, CHUNK), 1)
                  <= lax.broadcasted_iota(jnp.int32, (CHUNK, CHUNK), 0))
        for c in range(tm // CHUNK):
            rows = pl.ds(c * CHUNK, CHUNK)
            gelu_v, gelu_grad_v = _gelu_and_grad(a_ref[rows, W:].astype(F32))
            xhat, rstd = _layernorm_stats(gelu_v)
            vln = (xhat * lg_ref[...] + lb_ref[...]).astype(BF16)
            for g in range(G):
                cols = pl.ds(g * C, C)
                vg = vln[:, g * C:(g + 1) * C]
                gelu_u, gelu_grad_u = _gelu_and_grad(a_ref[rows, cols].astype(F32))
                dgated = dgated_ref[rows, cols]
                dsv = (dgated * gelu_u).astype(BF16)
                dbs_ref[...] += jnp.dot(dsv, e_ref[cols, :], preferred_element_type=F32)
                sv = jnp.dot(w_ref[g], vg, preferred_element_type=F32) + bs_ref[:, g:g + 1]
                dau = dgated * sv * gelu_grad_u
                da_ref[rows, cols] = dau.astype(BF16)
                dbin_ref[:, cols] += jnp.sum(dau, axis=0, keepdims=True)
                dws_ref[g] += jnp.where(causal, lax.dot_general(dsv, vg, _NT, preferred_element_type=F32), 0.0)
                dvln_ref[:, cols] = jnp.dot(wt_ref[g], dsv, preferred_element_type=F32)
            dvln = dvln_ref[...]
            dlg_ref[...] += jnp.sum(dvln * xhat, axis=0, keepdims=True)
            dlb_ref[...] += jnp.sum(dvln, axis=0, keepdims=True)
            dxhat = dvln * lg_ref[...]
            dv = rstd * (dxhat - jnp.mean(dxhat, axis=-1, keepdims=True)
                         - xhat * jnp.mean(dxhat * xhat, axis=-1, keepdims=True))
            dav = dv * gelu_grad_v
            da_ref[rows, W:] = dav.astype(BF16)
            dbin_ref[:, W:] += jnp.sum(dav, axis=0, keepdims=True)
        _rows_times_weight(da_ref, wi_ref, acc_ref)
        dx, dg = _rms_bwd(x_ref[...], gpre_ref[...], acc_ref[...])
        dh = dh_in_ref[...] + dx
        dh_ref[...] = dh
        dgpre_ref[...] += dg
        if post is not None:
            y_ref, gp_ref = refs[13:15]
            dy, dgp = _rms_bwd(y_ref[...].astype(F32), gp_ref[...], dh)
            out_refs[8][...] = dy.astype(BF16)
            out_refs[9][...] += dgp

    res = pl.pallas_call(
        body, name=name, grid=(T // tm,), in_specs=in_specs + [pl.BlockSpec(memory_space=pl.ANY)],
        out_specs=out_specs, out_shape=outs,
        scratch_shapes=[pltpu.VMEM((tm, W), F32), pltpu.VMEM((CHUNK, W), F32), pltpu.VMEM((tm, D), F32)],
        compiler_params=_params("arbitrary"))(*ins, after)
    return dict(zip(names, res))


def rope_tables(pos, inv_freq_row, name):
    T = pos.shape[0]
    tm = _tile(T, ROW_TILE)

    def body(p_ref, f_ref, c_ref, s1_ref, s2_ref):
        ang = p_ref[...].astype(F32) * f_ref[...]
        lane = lax.broadcasted_iota(jnp.int32, (tm, LANES), 1) % HEAD_DIM
        sin = jnp.sin(ang)
        c_ref[...] = jnp.cos(ang)
        s1_ref[...] = jnp.where(lane < _HALF, -sin, 0.0)
        s2_ref[...] = jnp.where((lane >= _HALF) & (lane < ROPE_DIM), sin, 0.0)

    tab = _rows(tm, LANES)
    return pl.pallas_call(
        body, name=name, grid=(T // tm,), in_specs=[_rows(tm, 1), _full((1, LANES))], out_specs=[tab] * 3,
        out_shape=[jax.ShapeDtypeStruct((T, LANES), F32)] * 3, compiler_params=_params("parallel"))(pos, inv_freq_row)


_HALF = ROPE_DIM // 2


def _slabs(x):
    return [x[:, b * LANES:(b + 1) * LANES] for b in range(x.shape[1] // LANES)]


def _rotate(x, c, s1, s2):
    return [xs * c + pltpu.roll(xs, LANES - _HALF, 1) * s1 + pltpu.roll(xs, _HALF, 1) * s2 for xs in _slabs(x)]


def _rotate_transposed(dy, c, s1, s2):
    return [ds * c + pltpu.roll(ds * s1, _HALF, 1) + pltpu.roll(ds * s2, LANES - _HALF, 1) for ds in _slabs(dy)]


def qkv_rope(hn, w_t, bias, tabs, q_width, kv_width, name):
    T, D = hn.shape
    N = w_t.shape[0]
    QK = q_width + kv_width
    tm = _tile(T, ROW_TILE)
    tn = 2 * LANES
    scale = HEAD_DIM ** -0.5

    def body(a_ref, w_ref, b_ref, c_ref, s1_ref, s2_ref, qkv_ref, qk_ref):
        av = a_ref[...]
        tabs_v = (c_ref[...], s1_ref[...], s2_ref[...])
        for j in range(N // tn):
            cols = pl.ds(j * tn, tn)
            acc = lax.dot_general(av, w_ref[cols, :], _NT, preferred_element_type=F32) + b_ref[:, cols]
            qkv_ref[:, cols] = acc.astype(BF16)
            if j * tn < QK:
                for b, y in enumerate(_rotate(acc, *tabs_v)):
                    if j * tn < q_width:
                        y = y * scale
                    qk_ref[:, pl.ds(j * tn + b * LANES, LANES)] = y.astype(BF16)

    tab = _rows(tm, LANES)
    return pl.pallas_call(
        body, name=name, grid=(T // tm,), in_specs=[_rows(tm, D), _resident(w_t.shape), _full((1, N)), tab, tab, tab],
        out_specs=[_rows(tm, N), _rows(tm, QK)],
        out_shape=[jax.ShapeDtypeStruct((T, N), BF16), jax.ShapeDtypeStruct((T, QK), BF16)],
        compiler_params=_params("parallel"))(hn, w_t, bias, *tabs)


def rope_bwd(dq, dk, dv, tabs, name):
    T, Q = dq.shape
    KV = dk.shape[1]
    tm = _tile(T, ROW_TILE)
    scale = HEAD_DIM ** -0.5

    def body(dq_ref, dk_ref, dv_ref, c_ref, s1_ref, s2_ref, o_ref, b_ref):
        @pl.when(pl.program_id(0) == 0)
        def _():
            b_ref[...] = jnp.zeros_like(b_ref)
        tabs_v = (c_ref[...], s1_ref[...], s2_ref[...])
        pieces = [s * scale for s in _rotate_transposed(dq_ref[...], *tabs_v)]
        pieces += _rotate_transposed(dk_ref[...], *tabs_v)
        pieces += _slabs(dv_ref[...])
        for b, y in enumerate(pieces):
            cols = pl.ds(b * LANES, LANES)
            o_ref[:, cols] = y.astype(BF16)
            b_ref[:, cols] += jnp.sum(y, axis=0, keepdims=True)

    tab = _rows(tm, LANES)
    return pl.pallas_call(
        body, name=name, grid=(T // tm,), in_specs=[_rows(tm, Q), _rows(tm, KV), _rows(tm, KV), tab, tab, tab],
        out_specs=[_rows(tm, Q + 2 * KV), _full((1, Q + 2 * KV))],
        out_shape=[jax.ShapeDtypeStruct((T, Q + 2 * KV), BF16), jax.ShapeDtypeStruct((1, Q + 2 * KV), F32)],
        compiler_params=_params("arbitrary"))(dq, dk, dv, *tabs)


def _band_mask(n, heads=1):
    qi = lax.broadcasted_iota(jnp.int32, (heads * CHUNK, 2 * CHUNK), 0) % CHUNK
    sj = lax.broadcasted_iota(jnp.int32, (heads * CHUNK, 2 * CHUNK), 1)
    return (sj > qi) & (sj <= qi + CHUNK) & ((n > 0) | (sj >= CHUNK))


def _kv_head(kc_ref, kp_ref, vc_ref, vp_ref, kh):
    lanes = slice(kh * HEAD_DIM, (kh + 1) * HEAD_DIM)
    return (jnp.concatenate([kp_ref[:, lanes], kc_ref[:, lanes]], axis=0),
            jnp.concatenate([vp_ref[:, lanes], vc_ref[:, lanes]], axis=0))


def _row_sums(x):
    return jnp.dot(x, jnp.ones((x.shape[1], LANES), BF16), preferred_element_type=F32)


def _lanes(v, n):
    return v[:, :n] if n <= LANES else jnp.concatenate([v] * (n // LANES), axis=-1)


def _head_probs(q, kk, valid, sink, sums_on_mxu):
    s = jnp.where(valid, lax.dot_general(q, kk, _NT, preferred_element_type=F32), NEG_INF)
    m = jnp.maximum(jnp.max(s, axis=-1, keepdims=True), sink)
    p = jnp.exp(s - m)
    e_sink = jnp.exp(sink - m)
    total = _row_sums(p.astype(BF16)) if sums_on_mxu else jnp.sum(p, axis=-1, keepdims=True)
    return p.astype(BF16), 1.0 / (total + e_sink), e_sink


def _attn_specs(q_width, kv_width, order):
    qb = q_width // kv_width
    prev = lambda i: jnp.maximum(order(i) - 1, 0)
    return [pl.BlockSpec((CHUNK, q_width), lambda i: (order(i), 0)),
            pl.BlockSpec((CHUNK, kv_width), lambda i: (order(i), qb)),
            pl.BlockSpec((CHUNK, kv_width), lambda i: (prev(i), qb)),
            pl.BlockSpec((CHUNK, kv_width), lambda i: (order(i), qb + 1)),
            pl.BlockSpec((CHUNK, kv_width), lambda i: (prev(i), qb + 1))]


def attn_fwd(qk, qkv, sinks, q_width, kv_width, name):
    T = qk.shape[0]
    group = q_width // kv_width

    def body(q_ref, kc_ref, kp_ref, vc_ref, vp_ref, sink_ref, o_ref):
        valid = _band_mask(pl.program_id(0))
        for kh in range(kv_width // HEAD_DIM):
            kk, vv = _kv_head(kc_ref, kp_ref, vc_ref, vp_ref, kh)
            for g in range(group):
                h = kh * group + g
                lanes = slice(h * HEAD_DIM, (h + 1) * HEAD_DIM)
                p, inv, _ = _head_probs(q_ref[:, lanes], kk, valid, sink_ref[h], False)
                o_ref[:, lanes] = (jnp.dot(p, vv, preferred_element_type=F32) * inv).astype(BF16)

    specs = _attn_specs(q_width, kv_width, lambda i: i)
    return pl.pallas_call(
        body, name=name, grid=(T // CHUNK,),
        in_specs=specs + [pl.BlockSpec(memory_space=pltpu.SMEM)],
        out_specs=_rows(CHUNK, q_width), out_shape=jax.ShapeDtypeStruct((T, q_width), BF16),
        compiler_params=_params("parallel"))(qk, qk, qk, qkv, qkv, sinks)


def attn_bwd(qk, qkv, do, sinks, q_width, kv_width, name):
    T = qk.shape[0]
    NB = T // CHUNK
    group = q_width // kv_width

    def body(q_ref, kc_ref, kp_ref, vc_ref, vp_ref, do_ref, sink_ref, dq_ref, dk_ref, dv_ref, ds_ref, ck_ref, cv_ref):
        i = pl.program_id(0)
        n = NB - 1 - i

        @pl.when(i == 0)
        def _():
            ck_ref[...] = jnp.zeros_like(ck_ref)
            cv_ref[...] = jnp.zeros_like(cv_ref)
            ds_ref[...] = jnp.zeros_like(ds_ref)
        lane = lax.broadcasted_iota(jnp.int32, (1, LANES), 1)
        dsink_row = jnp.zeros((1, LANES), F32)
        valid = _band_mask(n, group)
        head = lax.broadcasted_iota(jnp.int32, (group * CHUNK, 1), 0) // CHUNK
        for kh in range(kv_width // HEAD_DIM):
            kk, vv = _kv_head(kc_ref, kp_ref, vc_ref, vp_ref, kh)
            heads = [slice((kh * group + g) * HEAD_DIM, (kh * group + g + 1) * HEAD_DIM) for g in range(group)]
            q = jnp.concatenate([q_ref[:, hs] for hs in heads], axis=0)
            do = jnp.concatenate([do_ref[:, hs] for hs in heads], axis=0)
            sink = jnp.zeros((group * CHUNK, 1), F32)
            for g in range(group):
                sink = jnp.where(head == g, sink_ref[kh * group + g], sink)
            p, inv, e_sink = _head_probs(q, kk, valid, sink, True)
            p = p.astype(F32) * _lanes(inv, 2 * CHUNK)
            dp = lax.dot_general(do, vv, _NT, preferred_element_type=F32)
            delta = _row_sums((p * dp).astype(BF16))
            ds = (p * (dp - _lanes(delta, 2 * CHUNK))).astype(BF16)
            dsink = -e_sink * inv[:, :1] * delta[:, :1]
            dq = jnp.dot(ds, kk, preferred_element_type=F32)
            for g, hs in enumerate(heads):
                rows = slice(g * CHUNK, (g + 1) * CHUNK)
                dsink_row = dsink_row + jnp.where(lane == kh * group + g, jnp.sum(dsink[rows]), 0.0)
                dq_ref[:, hs] = dq[rows]
            dkk = lax.dot_general(ds, q, _TN, preferred_element_type=F32)
            dvv = lax.dot_general(p.astype(BF16), do, _TN, preferred_element_type=F32)
            lanes = slice(kh * HEAD_DIM, (kh + 1) * HEAD_DIM)
            dk_ref[:, lanes] = dkk[CHUNK:] + ck_ref[:, lanes]
            dv_ref[:, lanes] = dvv[CHUNK:] + cv_ref[:, lanes]
            ck_ref[:, lanes] = dkk[:CHUNK]
            cv_ref[:, lanes] = dvv[:CHUNK]
        ds_ref[0:1, :] += dsink_row

    order = lambda i: NB - 1 - i
    specs = _attn_specs(q_width, kv_width, order)
    kv_out = pl.BlockSpec((CHUNK, kv_width), lambda i: (order(i), 0))
    q_rows = pl.BlockSpec((CHUNK, q_width), lambda i: (order(i), 0))
    return pl.pallas_call(
        body, name=name, grid=(NB,),
        in_specs=specs + [q_rows, pl.BlockSpec(memory_space=pltpu.SMEM)],
        out_specs=[q_rows, kv_out, kv_out, _full((8, LANES))],
        out_shape=[jax.ShapeDtypeStruct((T, q_width), F32), jax.ShapeDtypeStruct((T, kv_width), F32),
                   jax.ShapeDtypeStruct((T, kv_width), F32), jax.ShapeDtypeStruct((8, LANES), F32)],
        scratch_shapes=[pltpu.VMEM((CHUNK, kv_width), F32), pltpu.VMEM((CHUNK, kv_width), F32)],
        compiler_params=_params("arbitrary"))(qk, qk, qk, qkv, qkv, do, sinks)


def _blocked(w):
    return w.reshape(N_DEV, w.shape[0] // N_DEV, w.shape[1])


def kernel(x, positions, pre_mix_g, post_mix_g, pre_ffn_g, post_ffn_g, a_w_in, a_b_in, a_ln_g, a_ln_b, a_w_s, a_b_s, a_w_out, b_w_qkv, b_b_qkv, b_sinks, b_w_o, ffn_w_gu, ffn_w_down, loss_target, m_pre_mix_g, m_post_mix_g, m_pre_ffn_g, m_post_ffn_g, m_a_w_in, m_a_b_in, m_a_ln_g, m_a_ln_b, m_a_w_s, m_a_b_s, m_a_w_out, m_b_w_qkv, m_b_b_qkv, m_b_sinks, m_b_w_o, m_ffn_w_gu, m_ffn_w_down, v_pre_mix_g, v_post_mix_g, v_pre_ffn_g, v_post_ffn_g, v_a_w_in, v_a_b_in, v_a_ln_g, v_a_ln_b, v_a_w_s, v_a_b_s, v_a_w_out, v_b_w_qkv, v_b_b_qkv, v_b_sinks, v_b_w_o, v_ffn_w_gu, v_ffn_w_down):
    weights = dict(pre_mix_g=pre_mix_g, post_mix_g=post_mix_g, pre_ffn_g=pre_ffn_g, post_ffn_g=post_ffn_g,
                   a_w_in=a_w_in, a_b_in=a_b_in, a_ln_g=a_ln_g, a_ln_b=a_ln_b, a_w_s=a_w_s, a_b_s=a_b_s,
                   a_w_out=a_w_out, b_w_qkv=b_w_qkv, b_b_qkv=b_b_qkv, b_sinks=b_sinks, b_w_o=b_w_o,
                   ffn_w_gu=ffn_w_gu, ffn_w_down=ffn_w_down)
    mom_m = dict(pre_mix_g=m_pre_mix_g, post_mix_g=m_post_mix_g, pre_ffn_g=m_pre_ffn_g, post_ffn_g=m_post_ffn_g,
                 a_w_in=m_a_w_in, a_b_in=m_a_b_in, a_ln_g=m_a_ln_g, a_ln_b=m_a_ln_b, a_w_s=m_a_w_s, a_b_s=m_a_b_s,
                 a_w_out=m_a_w_out, b_w_qkv=m_b_w_qkv, b_b_qkv=m_b_b_qkv, b_sinks=m_b_sinks, b_w_o=m_b_w_o,
                 ffn_w_gu=m_ffn_w_gu, ffn_w_down=m_ffn_w_down)
    mom_v = dict(pre_mix_g=v_pre_mix_g, post_mix_g=v_post_mix_g, pre_ffn_g=v_pre_ffn_g, post_ffn_g=v_post_ffn_g,
                 a_w_in=v_a_w_in, a_b_in=v_a_b_in, a_ln_g=v_a_ln_g, a_ln_b=v_a_ln_b, a_w_s=v_a_w_s, a_b_s=v_a_b_s,
                 a_w_out=v_a_w_out, b_w_qkv=v_b_w_qkv, b_b_qkv=v_b_b_qkv, b_sinks=v_b_sinks, b_w_o=v_b_w_o,
                 ffn_w_gu=v_ffn_w_gu, ffn_w_down=v_ffn_w_down)
    names = list(weights)
    big = ("a_w_in", "b_w_qkv", "ffn_w_gu", "a_w_out", "b_w_o", "ffn_w_down")

    T, D = x.shape[1], x.shape[2]
    depth = pre_mix_g.shape[0]
    n_heads = b_sinks.shape[1]
    q_width = n_heads * HEAD_DIM
    kv_width = N_KV_HEADS * HEAD_DIM
    G = a_w_s.shape[1]
    W = a_ln_g.shape[1]
    device = _index(_position())

    dev = device.reshape(1).astype(jnp.int32)

    def layer_keys(i):
        mixer = ("a_w_in", "a_w_out") if i % 2 == 0 else ("b_w_qkv", "b_w_o")
        return [(k, i // 2) for k in mixer] + [("ffn_w_gu", i), ("ffn_w_down", i)]

    as_view = ("b_w_qkv", "ffn_w_gu")
    in_kernel = ("a_w_in",)
    view = lambda k, t: jnp.swapaxes(t, 1, 2) if k in as_view else t
    w_view = {k: view(k, weights[k]) for k in big}
    m_view = {k: view(k, mom_m[k]) for k in big}
    v_view = {k: view(k, mom_v[k]) for k in big}

    groups_keys = [[key] for key in layer_keys(0)] + [layer_keys(i) for i in range(1, depth)]
    group_of = {}
    for g, keys in enumerate(groups_keys):
        layer = 0 if g < len(layer_keys(0)) else g - len(layer_keys(0)) + 1
        for pos, (k, _) in enumerate(keys):
            group_of[(layer, k)] = (g, pos)
    bias_land = lax.dynamic_update_slice(
        jnp.zeros((N_DEV,) + b_b_qkv.shape, F32), b_b_qkv[None], (device, 0, 0))
    h, hn = x[0], first_norm(x[0], pre_mix_g[0][None], "norm_first")
    first = len(layer_keys(0))
    in_gather, token = [], ()
    for name, lo, hi in (("gather_start_first", 0, first), ("gather_start_rest", first, len(groups_keys))):
        groups = [list(prep_weights([(w_view[k], l, k in in_kernel) for k, l in keys], dev, f"prep_{name}_{g}",
                                    after=token)) for g, keys in enumerate(groups_keys[lo:hi])]
        if lo:
            groups[0].append(bias_land)
        gather_sems, flat_lands, token = relay_start(groups, name)
        token = (token, hn)
        for grp, sems in zip(groups, gather_sems):
            in_gather.append(([flat_lands.pop(0) for _ in grp], sems))
    all_started = token[0]
    passed, gathered = {}, {}

    def pass_on(g, after):
        if g not in passed and g < len(in_gather):
            lands, sems = in_gather[g]
            passed[g], lands = relay_pass(lands, sems, after, f"gather_pass_{g}")
            in_gather[g] = (lands, sems)

    pass_on(0, (all_started,))

    def weight(i, k, *after, ahead=True):
        g, pos = group_of[(i, k)]
        if g not in gathered:
            if ahead:
                pass_on(g + 1, after)
            lands, sems = in_gather[g]
            gathered[g] = relay_wait(lands, sems, passed[g], after, f"gather_wait_{g}")
        w = gathered[g][pos]
        return w.reshape(N_DEV * w.shape[1], w.shape[2])

    causal = jnp.tril(jnp.ones((CHUNK, CHUNK), dtype=bool))
    w_mask = jnp.where(causal[None, None], a_w_s, 0.0).astype(BF16)
    w_mask_t = jnp.swapaxes(w_mask, 2, 3)
    b_s_t = jnp.swapaxes(a_b_s, 1, 2)
    group_onehot = (jnp.arange(W)[:, None] // (W // G) == jnp.arange(LANES)[None, :]).astype(BF16)
    lane = jnp.arange(LANES) % HEAD_DIM
    inv_freq = ROPE_THETA ** (-jnp.arange(0, ROPE_DIM, 2, dtype=F32) / ROPE_DIM)
    inv_freq_row = jnp.where(lane < ROPE_DIM, inv_freq[lane % (ROPE_DIM // 2)], 0.0)[None, :].astype(F32)
    tabs = rope_tables(positions.reshape(T, 1), inv_freq_row, "rope_tables")

    saved = []
    for i in range(depth):
        j = i // 2
        s = dict(h=h, hn=hn)
        post_pre = (post_mix_g[i][None], pre_ffn_g[i][None])
        if i % 2 == 0:
            s["a"] = rowmm(hn, weight(i, "a_w_in", hn, all_started), True, a_b_in[j][None], BF16, "gmlp_in")
            s["gated"], s["mix"], s["h2"], s["fn"] = gmlp_gate_out(
                s["a"], a_ln_g[j][None], a_ln_b[j][None], w_mask[j], b_s_t[j], weight(i, "a_w_out", s["a"], ahead=i > 0),
                h, *post_pre, "gmlp_gate_out")
            if i == 0:
                pass_on(group_of[(0, "ffn_w_gu")][0], (s["mix"],))
        else:
            w_qkv = weight(i, "b_w_qkv", hn)
            b_qkv_full = jnp.swapaxes(gathered[group_of[(1, "b_w_qkv")][0]][-1], 0, 1).reshape(b_b_qkv.shape[0], 1, -1)
            s["qkv"], s["qk"] = qkv_rope(hn, w_qkv, b_qkv_full[j], tabs, q_width, kv_width, "attn_qkv_rope")
            s["o"] = attn_fwd(s["qk"], s["qkv"], b_sinks[j], q_width, kv_width, "attn_core")
            s["mix"], s["h2"], s["fn"] = mm_add_norm(s["o"], weight(i, "b_w_o", s["o"]), h, *post_pre, "attn_out")
        s["act"], s["act_partials"] = ffn_up(s["fn"], weight(i, "ffn_w_gu", s["fn"]), "ffn_up")
        w_down = weight(i, "ffn_w_down", s["act"])
        saved.append(s)
        if i + 1 < depth:
            s["f"], h, hn = mm_add_norm(s["act"], w_down, s["h2"], post_ffn_g[i][None], pre_mix_g[i + 1][None], "ffn_down")
    small = {k: [None] * weights[k].shape[0] for k in names if k not in big}
    dh, df, loss_sum, small["post_ffn_g"][depth - 1] = mm_add_norm_loss(
        saved[-1]["act"], w_down, saved[-1]["h2"], post_ffn_g[depth - 1][None], loss_target[0], "ffn_down_loss")

    grads = {k: [None] * weights[k].shape[0] for k in big}
    in_flight = []

    def send_grads(keys, tag):
        sems, bufs, token = exchange_start([_blocked(grads[k][l]) for k, l in keys], "exchange_start_" + tag)
        in_flight.append((keys, sems, bufs, tag))
        return token

    replicated = [k for k in names if k not in big and k not in ("b_b_qkv", "a_w_s")]
    main = sum(weights[k].size for k in replicated)
    main_rows = -(-main // (LANES * LANES)) * LANES
    bias_size = b_b_qkv.shape[0] * N_DEV * b_b_qkv.shape[1]
    bias_rows = -(-bias_size // (8 * LANES)) * 8
    flat = lambda vals: [v.reshape(-1) for v in vals]

    def start_small_gather():
        packed = jnp.concatenate(
            flat(p for k in replicated for p in small[k])
            + [loss_sum[0, :1], jnp.zeros((main_rows * LANES - main - 1,), F32)] + flat(small["b_b_qkv"]) + [jnp.zeros((bias_rows * LANES - bias_size,), F32)]
        ).reshape(1, main_rows + bias_rows, LANES)
        spatial = jnp.stack(small["a_w_s"]).reshape(1, -1, LANES)
        return gather_start([prep_weights([(packed, 0, False), (spatial, 0, False)], dev, "place_small_grads", F32)],
                            "gather_start_small")

    sent = ()
    for i in reversed(range(depth)):
        j = i // 2
        s = saved[i]
        dgu = ffn_dact(df, weight(i, "ffn_w_down"), s["act_partials"], "ffn_dact", after=sent)
        alone = i == 0
        grads["ffn_w_down"][i] = grad_mm(s["act"], df, "ffn_down_grad")
        early = (send_grads(layer_keys(i)[3:], f"ffn_down_{i}"),) if alone else ()
        grads["ffn_w_gu"][i] = grad_mm(dgu, s["fn"], "ffn_up_grad", after=early)
        sent = send_grads(layer_keys(i)[2:3] if alone else layer_keys(i)[2:], f"ffn_{i}")
        r = mm_norm_bwd(dgu, weight(i, "ffn_w_gu"), dh, s["h2"], pre_ffn_g[i][None],
                        (s["mix"], post_mix_g[i][None]), "ffn_dx", sent)
        dh, dmix = r["dh"], r["dy"]
        small["pre_ffn_g"][i], small["post_mix_g"][i] = r["dg_pre"], r["dg_post"]
        post = (saved[i - 1]["f"], post_ffn_g[i - 1][None]) if i > 0 else None
        if i % 2 == 0:
            grads["a_w_out"][j] = grad_mm(s["gated"], dmix, "gmlp_out_grad")
            if alone:
                sent = send_grads(layer_keys(i)[1:2], f"mixer_out_{i}")
            r = gmlp_gate_bwd_dx(
                s["a"], dmix, weight(i, "a_w_out"), a_ln_g[j][None], a_ln_b[j][None], w_mask[j], w_mask_t[j], b_s_t[j],
                group_onehot, weight(i, "a_w_in"), dh, s["h"], pre_mix_g[i][None], post, "gmlp_gate_bwd_dx", sent)
            small["a_w_s"][j], small["a_b_s"][j] = r["dw_s"], r["db_s_t"][:, :G].T
            small["a_ln_g"][j], small["a_ln_b"][j], small["a_b_in"][j] = r["dln_g"][0], r["dln_b"][0], r["db_in"][0]
            small["pre_mix_g"][i] = r["dg_pre"]
            first_layer = ()
            if i == 0:
                small_gather = start_small_gather()
                first_layer = (small_gather[2],)
            grads["a_w_in"][j] = grad_mm(r["da"], s["hn"], "gmlp_in_grad", after=first_layer)
            sent = (send_grads(layer_keys(i)[:1] if alone else layer_keys(i)[:2], f"mixer_{i}"),)
        else:
            grads["b_w_o"][j] = grad_mm(s["o"], dmix, "attn_out_grad")
            do = rowmm(dmix, weight(i, "b_w_o"), True, None, BF16, "attn_do")
            dq, dk, dv, dsink = attn_bwd(s["qk"], s["qkv"], do, b_sinks[j], q_width, kv_width, "attn_core_bwd")
            dqkv, dbias = rope_bwd(dq, dk, dv, tabs, "attn_rope_bwd")
            small["b_sinks"][j], small["b_b_qkv"][j] = dsink[0, :n_heads], dbias[0]
            grads["b_w_qkv"][j] = grad_mm(dqkv, s["hn"], "attn_qkv_grad")
            sent = send_grads(layer_keys(i)[:2], f"mixer_{i}")
            r = mm_norm_bwd(dqkv, weight(i, "b_w_qkv"), dh, s["h"], pre_mix_g[i][None], post, "attn_dx", sent)
            sent = ()
        dh, small["pre_mix_g"][i] = r["dh"], r["dg_pre"]
        if i > 0:
            df, small["post_ffn_g"][i - 1] = r["dy"], r["dg_post"]
    grad_x = dh[None]

    small_sems, small_lands, small_sent = small_gather
    pack_main = lambda src: jnp.concatenate(
        flat(src[k] for k in replicated) + [jnp.ones((main_rows * LANES - main,), F32)]).reshape(main_rows, LANES)
    packed_w, packed_m, packed_v = pack_main(weights), pack_main(mom_m), pack_main(mom_v)

    stacked = {k: [lax.empty(w_view[k].shape, F32) for _ in range(4)] for k in big}

    def update_group(entry, after):
        keys, sems, bufs, tag = entry
        partials, landed = exchange_wait(bufs, sems, after, "exchange_wait_" + tag)
        for (k, l), part, land in zip(keys, partials, landed):
            stacked[k] = sum_adamw(part, land, w_view[k], m_view[k], v_view[k], stacked[k], l,
                                   k in in_kernel, dev, "adamw_" + k)
        return tuple(stacked[k][0] for k, _ in keys)

    after = (small_sent, dh) + sent
    for entry in in_flight[:-1]:
        after = update_group(entry, after)
    out_g, out_d, out_m, out_v = {}, {}, {}, {}

    gathered_small, gathered_spatial = gather_wait(small_lands, small_sems[0], after + (packed_w, packed_m, packed_v),
                                                   "gather_wait_small")
    rows_of = lambda t: t.reshape(-1, LANES)
    spatial = sum_adamw_small(gathered_spatial, rows_of(a_w_s), rows_of(m_a_w_s), rows_of(v_a_w_s), "adamw_a_w_s")
    out_g["a_w_s"], out_d["a_w_s"], out_m["a_w_s"], out_v["a_w_s"] = [t.reshape(a_w_s.shape) for t in spatial]
    results = sum_adamw_small(gathered_small, packed_w, packed_m, packed_v, "adamw_small")
    loss = results[0].reshape(-1)[main] * (0.5 / D)
    off = 0
    for k in replicated:
        size, shape = weights[k].size, weights[k].shape
        out_g[k], out_d[k], out_m[k], out_v[k] = [t.reshape(-1)[off:off + size].reshape(shape) for t in results]
        off += size
    n_local = b_b_qkv.shape[1]
    bias_sum = sum_parts(gathered_small[:, main_rows:], "sum_bias_grads").reshape(-1)[:bias_size]
    out_g["b_b_qkv"] = lax.dynamic_slice_in_dim(
        bias_sum.reshape(b_b_qkv.shape[0], -1), device * n_local, n_local, axis=1)
    pad = lambda t: jnp.concatenate([t.reshape(-1), jnp.ones((8 * LANES - t.size,), F32)]).reshape(8, LANES)
    bias_update = adamw(pad(b_b_qkv), pad(out_g["b_b_qkv"]), pad(m_b_b_qkv), pad(v_b_b_qkv), "adamw_bias")
    out_d["b_b_qkv"], out_m["b_b_qkv"], out_v["b_b_qkv"] = [
        t.reshape(-1)[:b_b_qkv.size].reshape(b_b_qkv.shape) for t in bias_update]

    update_group(in_flight[-1], (spatial[0], results[0], bias_update[0]))
    for k in big:
        out_g[k], out_d[k], out_m[k], out_v[k] = [view(k, t) for t in stacked[k]]

    return (loss, grad_x, *[out_g[k] for k in names], *[out_d[k] for k in names],
            *[out_m[k] for k in names], *[out_v[k] for k in names])
```

```python
import math

import jax
import jax.numpy as jnp
from jax import lax
from jax.experimental import pallas as pl
from jax.experimental.pallas import tpu as pltpu

F32, BF16 = jnp.float32, jnp.bfloat16
MESH = pl.DeviceIdType.MESH
N_DEV = 8

CHUNK = 128
HEAD_DIM = 64
N_KV_HEADS = 4
ROPE_DIM = HEAD_DIM // 4
ROPE_THETA = 500000.0
RMS_EPS = 1e-6
LN_EPS = 1e-5
NEG_INF = -1e30

ADAM_LR = 0.001
ADAM_B1 = 0.9
ADAM_B2 = 0.999
ADAM_EPS = 1e-08
ADAM_WD = 0.01
ADAM_STEP = 10

V7X_VMEM_BYTES = 64 * 2 ** 20
VMEM_LIMIT = V7X_VMEM_BYTES - 8 * 2 ** 20
LANES = 128

ROW_TILE = 512
COL_TILE = 512
GRAD_TILE_ROWS = 1408
GRAD_TILE_DEPTH = 2048
ADAM_TILE = 256


def _params(*sem):
    return pltpu.CompilerParams(dimension_semantics=sem or None, vmem_limit_bytes=VMEM_LIMIT)


def _tile(n, pref):
    if n <= pref:
        return n
    t = pref - pref % LANES
    while t >= LANES:
        if n % t == 0:
            return t
        t -= LANES
    return n


def _full(shape):
    return pl.BlockSpec(shape, lambda *_: (0,) * len(shape))


def _resident(shape):
    return pl.BlockSpec(shape, lambda *_: (0,) * len(shape), pipeline_mode=pl.Buffered(1))


def _rows(tm, width):
    return pl.BlockSpec((tm, width), lambda i: (i, 0))


def _rms(x, g):
    r = lax.rsqrt(jnp.mean(x * x, axis=-1, keepdims=True) + RMS_EPS)
    return x * r * g


def _rms_bwd(x, g, dy):
    r = lax.rsqrt(jnp.mean(x * x, axis=-1, keepdims=True) + RMS_EPS)
    xhat = x * r
    dg = jnp.sum(dy * xhat, axis=0, keepdims=True)
    dxhat = dy * g
    dx = r * (dxhat - xhat * jnp.mean(dxhat * xhat, axis=-1, keepdims=True))
    return dx, dg


_INV_SQRT2 = 1.0 / math.sqrt(2.0)
_INV_SQRT2PI = 1.0 / math.sqrt(2.0 * math.pi)


def _gelu(x):
    return 0.5 * x * (1.0 + lax.erf(x * _INV_SQRT2))


def _gelu_and_grad(x):
    cdf = 0.5 * (1.0 + lax.erf(x * _INV_SQRT2))
    return x * cdf, cdf + x * jnp.exp(-0.5 * x * x) * _INV_SQRT2PI


def _sigmoid(x):
    return 0.5 * jnp.tanh(0.5 * x) + 0.5


def _position():
    return lax.axis_index("x"), lax.axis_index("y"), lax.axis_index("c")


def _index(p):
    return 4 * p[0] + 2 * p[1] + p[2]


_HBM = pl.BlockSpec(memory_space=pltpu.HBM)
_SEM = pl.BlockSpec(memory_space=pltpu.SEMAPHORE)
_ORDERED_BY_DATA = pltpu.SideEffectType.DATAFLOW_SIDE_EFFECTING


def _in_hbm(v):
    return pltpu.with_memory_space_constraint(v, pltpu.HBM)


def _peer(k):
    x, y, c = _position()
    return (x ^ ((k >> 2) & 1), y ^ ((k >> 1) & 1), c ^ (k & 1))


def _split_copies(srcs, dsts, send, recv):
    return [pltpu.make_async_remote_copy(
        src_ref=src(k), dst_ref=dst(k), send_sem=send.at[7 * a + k - 1], recv_sem=recv.at[7 * a + k - 1],
        device_id=_peer(k), device_id_type=MESH)
        for a, (src, dst) in enumerate(zip(srcs, dsts)) for k in range(1, N_DEV)]


def _start_call(groups, sent, copies_of, name, per_array=N_DEV - 1):
    flat = [v for grp in groups for v in grp]
    n, ng = len(flat), len(groups)

    def body(*refs):
        bufs, sems = refs[:n], refs[n:n + 2 * ng]
        q = 0
        for g, grp in enumerate(groups):
            for cp in copies_of(bufs[q:q + len(grp)], sems[2 * g], sems[2 * g + 1]):
                cp.start()
            q += len(grp)
        refs[-1][...] = jnp.zeros_like(refs[-1])

    sem_shapes = []
    for count in sent:
        sem_shapes += [pltpu.SemaphoreType.DMA((per_array * count,)) for _ in range(2)]
    res = pl.pallas_call(
        body, name=name,
        out_shape=sem_shapes + [pltpu.HBM(v.shape, v.dtype) for v in flat] + [jax.ShapeDtypeStruct((8, LANES), F32)],
        in_specs=[_HBM] * n,
        out_specs=[_SEM] * (2 * ng) + [_HBM] * n + [pl.BlockSpec(memory_space=pltpu.VMEM)],
        input_output_aliases={i: 2 * ng + i for i in range(n)},
        compiler_params=pltpu.CompilerParams(has_side_effects=_ORDERED_BY_DATA),
    )(*[_in_hbm(v) for v in flat])
    sems = [(res[2 * g], res[2 * g + 1]) for g in range(ng)]
    return sems, list(res[2 * ng:-1]), res[-1]


def _wait_call(bufs, sems, copies_of, after, name):
    n = len(bufs)

    def body(*refs):
        for cp in copies_of(refs[:n], refs[n], refs[n + 1]):
            cp.wait_send()
            cp.wait_recv()

    return list(pl.pallas_call(
        body, name=name,
        out_shape=[pltpu.HBM(v.shape, v.dtype) for v in bufs],
        in_specs=[_HBM] * n + [_SEM, _SEM] + [pl.BlockSpec(memory_space=pl.ANY)] * len(after),
        out_specs=[_HBM] * n,
        input_output_aliases={i: i for i in range(n)},
        compiler_params=pltpu.CompilerParams(has_side_effects=_ORDERED_BY_DATA),
    )(*bufs, sems[0], sems[1], *after))


def _gather_copies(lands, send, recv):
    me = _index(_position())
    mine = [lambda k, ref=ref: ref.at[pl.ds(me, 1)] for ref in lands]
    return _split_copies(mine, mine, send, recv)


def _gather_arrivals(lands, send, recv):
    me = _index(_position())
    mine = [lambda k, ref=ref: ref.at[pl.ds(me, 1)] for ref in lands]
    theirs = [lambda k, ref=ref: ref.at[pl.ds(_index(_peer(k)), 1)] for ref in lands]
    return _split_copies(mine, theirs, send, recv)


def _exchange_copies(bufs, send, recv):
    half = len(bufs) // 2
    srcs = [lambda k, ref=ref: ref.at[pl.ds(_index(_peer(k)), 1)] for ref in bufs[:half]]
    dsts = [lambda k, ref=ref: ref.at[pl.ds(k - 1, 1)] for ref in bufs[half:]]
    return _split_copies(srcs, dsts, send, recv)


def gather_start(groups, name):
    return _start_call(groups, [len(grp) for grp in groups], _gather_copies, name)


def gather_wait(lands, sems, after, name):
    return _wait_call(lands, sems, _gather_arrivals, after, name)


_DIRECT = (1, 2, 4, 6)
_PASSED = (3, 5, 7)


def _remote(src, dst, send, recv, q, k):
    return pltpu.make_async_remote_copy(src_ref=src, dst_ref=dst, send_sem=send.at[q], recv_sem=recv.at[q],
                                        device_id=_peer(k), device_id_type=MESH)


def _slot_of(land, k):
    return land.at[pl.ds(_index(_peer(k)), 1)]


def relay_start(groups, name):
    def copies(lands, send, recv):
        return [_remote(_slot_of(land, 0), _slot_of(land, 0), send, recv, len(_DIRECT) * a + j, k)
                for a, land in enumerate(lands) for j, k in enumerate(_DIRECT)]
    return _start_call(groups, [len(grp) for grp in groups], copies, name, per_array=len(_DIRECT))


def relay_pass(lands, sems, after, name):
    n = len(lands)

    def body(*refs):
        bufs, recv_first = refs[:n], refs[n]
        send, recv = refs[n + 1 + len(after):n + 3 + len(after)]
        for a, land in enumerate(bufs):
            for j, k in enumerate(_PASSED):
                came = _slot_of(land, k ^ 1)
                _remote(came, came, recv_first, recv_first, len(_DIRECT) * a + _DIRECT.index(k ^ 1), k ^ 1).wait_recv()
                _remote(came, came, send, recv, len(_PASSED) * a + j, 1).start()

    res = pl.pallas_call(
        body, name=name,
        out_shape=[pltpu.SemaphoreType.DMA((len(_PASSED) * n,))] * 2 + [pltpu.HBM(v.shape, v.dtype) for v in lands],
        in_specs=[_HBM] * n + [_SEM] + [pl.BlockSpec(memory_space=pl.ANY)] * len(after),
        out_specs=[_SEM, _SEM] + [_HBM] * n,
        input_output_aliases={i: 2 + i for i in range(n)},
        compiler_params=pltpu.CompilerParams(has_side_effects=_ORDERED_BY_DATA),
    )(*lands, sems[1], *after)
    return (res[0], res[1]), list(res[2:])


def relay_wait(lands, first, second, after, name):
    n = len(lands)

    def body(*refs):
        bufs = refs[:n]
        send_first, recv_first, send, recv = refs[n:n + 4]
        for a, land in enumerate(bufs):
            mine = _slot_of(land, 0)
            _remote(mine, _slot_of(land, 1), send_first, recv_first, len(_DIRECT) * a, 1).wait_recv()
            for j, k in enumerate(_DIRECT):
                _remote(mine, mine, send_first, recv_first, len(_DIRECT) * a + j, k).wait_send()
            for j, k in enumerate(_PASSED):
                cp = _remote(_slot_of(land, k ^ 1), _slot_of(land, k), send, recv, len(_PASSED) * a + j, 1)
                cp.wait_send()
                cp.wait_recv()

    return list(pl.pallas_call(
        body, name=name,
        out_shape=[pltpu.HBM(v.shape, v.dtype) for v in lands],
        in_specs=[_HBM] * n + [_SEM] * 4 + [pl.BlockSpec(memory_space=pl.ANY)] * len(after),
        out_specs=[_HBM] * n,
        input_output_aliases={i: i for i in range(n)},
        compiler_params=pltpu.CompilerParams(has_side_effects=_ORDERED_BY_DATA),
    )(*lands, first[0], first[1], second[0], second[1], *after))


def exchange_start(partials, name):
    lands = [lax.empty((N_DEV - 1,) + p.shape[1:], p.dtype) for p in partials]
    sems, bufs, token = _start_call([list(partials) + lands], [len(partials)], _exchange_copies, name)
    return sems[0], bufs, token


def exchange_wait(bufs, sems, after, name):
    bufs = _wait_call(bufs, sems, _exchange_copies, after, name)
    return bufs[:len(bufs) // 2], bufs[len(bufs) // 2:]


def prep_weights(items, dev, name, dtype=BF16, after=()):
    n = len(items)
    blocks = [(1, w.shape[2], w.shape[1]) if t else (1,) + w.shape[1:] for w, _, t in items]

    def body(d_ref, *refs):
        for (_, _, transpose), w_ref, o_ref in zip(items, refs[:n], refs[-n:]):
            v = w_ref[0]
            o_ref[0] = (v.T if transpose else v).astype(dtype)

    return pl.pallas_call(
        body, name=name,
        grid_spec=pltpu.PrefetchScalarGridSpec(
            num_scalar_prefetch=1, grid=(1,),
            in_specs=[pl.BlockSpec((1,) + w.shape[1:], lambda i, d, layer=layer: (layer, 0, 0)) for w, layer, _ in items]
            + [pl.BlockSpec(memory_space=pl.ANY)] * len(after),
            out_specs=[pl.BlockSpec(blk, lambda i, d: (d[0], 0, 0)) for blk in blocks]),
        out_shape=[jax.ShapeDtypeStruct((N_DEV,) + blk[1:], dtype) for blk in blocks],
        compiler_params=_params("arbitrary"),
    )(dev, *[w for w, _, _ in items], *after)


def _adamw_math(w, g, m, v):
    mn = ADAM_B1 * m + (1.0 - ADAM_B1) * g
    vn = ADAM_B2 * v + (1.0 - ADAM_B2) * (g * g)
    m_hat = mn * (1.0 / (1.0 - ADAM_B1 ** ADAM_STEP))
    v_hat = vn * (1.0 / (1.0 - ADAM_B2 ** ADAM_STEP))
    return -ADAM_LR * (m_hat / (jnp.sqrt(v_hat) + ADAM_EPS) + ADAM_WD * w), mn, vn


def sum_adamw(partial, landed, w, m, v, prev, layer, transpose, dev, name):
    _, r, c = partial.shape
    if transpose:
        tc = _tile(c, ADAM_TILE)
        grid = (c // tc,)
        part_spec = pl.BlockSpec((1, r, tc), lambda i, d: (d[0], 0, i))
        land_spec = pl.BlockSpec((N_DEV - 1, r, tc), lambda i, d: (0, 0, i))
        w_spec = pl.BlockSpec((1, tc, r), lambda i, d: (layer, i, 0))
    else:
        tr = r // 2 if r % 32 == 0 else r
        grid = (r // tr,)
        part_spec = pl.BlockSpec((1, tr, c), lambda i, d: (d[0], i, 0))
        land_spec = pl.BlockSpec((N_DEV - 1, tr, c), lambda i, d: (0, i, 0))
        w_spec = pl.BlockSpec((1, tr, c), lambda i, d: (layer, i, 0))

    def body(d_ref, p_ref, l_ref, w_ref, m_ref, v_ref, *rest):
        g_ref, dl_ref, mo_ref, vo_ref = rest[-4:]
        acc = p_ref[0].astype(F32)
        for k in range(N_DEV - 1):
            acc = acc + l_ref[k].astype(F32)
        g = acc.T if transpose else acc
        delta, mn, vn = _adamw_math(w_ref[0], g, m_ref[0], v_ref[0])
        g_ref[0], dl_ref[0], mo_ref[0], vo_ref[0] = g, delta, mn, vn

    return pl.pallas_call(
        body, name=name,
        grid_spec=pltpu.PrefetchScalarGridSpec(
            num_scalar_prefetch=1, grid=grid,
            in_specs=[part_spec, land_spec, w_spec, w_spec, w_spec] + [pl.BlockSpec(memory_space=pl.ANY)] * 4,
            out_specs=[w_spec] * 4),
        out_shape=[jax.ShapeDtypeStruct(w.shape, F32)] * 4,
        input_output_aliases={6 + q: q for q in range(4)},
        compiler_params=_params("parallel"),
    )(dev, partial, landed, w, m, v, *prev)


def sum_parts(parts, name):
    P, r, c = parts.shape

    def body(p_ref, o_ref):
        acc = p_ref[0].astype(F32)
        for s in range(1, P):
            acc = acc + p_ref[s].astype(F32)
        o_ref[...] = acc

    return pl.pallas_call(
        body, name=name, in_specs=[_full((P, r, c))], out_specs=_full((r, c)), grid=(1,),
        out_shape=jax.ShapeDtypeStruct((r, c), F32), compiler_params=_params("arbitrary"))(parts)


def adamw(w, g, m, v, name):
    R, C = w.shape
    tr = _tile(R, ROW_TILE)

    def body(w_ref, g_ref, m_ref, v_ref, d_ref, mo_ref, vo_ref):
        d_ref[...], mo_ref[...], vo_ref[...] = _adamw_math(w_ref[...], g_ref[...], m_ref[...], v_ref[...])

    spec = pl.BlockSpec((tr, C), lambda i: (i, 0))
    return pl.pallas_call(
        body, name=name, grid=(R // tr,),
        in_specs=[spec] * 4, out_specs=[spec] * 3,
        out_shape=[jax.ShapeDtypeStruct((R, C), F32)] * 3,
        compiler_params=_params("parallel"),
    )(w, g, m, v)


def sum_adamw_small(gathered, w, m, v, name):
    rows = w.shape[0]
    tr = _tile(rows, ROW_TILE)

    def body(p_ref, w_ref, m_ref, v_ref, g_ref, d_ref, mo_ref, vo_ref):
        g = p_ref[0]
        for s in range(1, N_DEV):
            g = g + p_ref[s]
        g_ref[...] = g
        d_ref[...], mo_ref[...], vo_ref[...] = _adamw_math(w_ref[...], g, m_ref[...], v_ref[...])

    spec = pl.BlockSpec((tr, LANES), lambda i: (i, 0))
    return pl.pallas_call(
        body, name=name, grid=(rows // tr,),
        in_specs=[pl.BlockSpec((N_DEV, tr, LANES), lambda i: (0, i, 0)), spec, spec, spec], out_specs=[spec] * 4,
        out_shape=[jax.ShapeDtypeStruct((rows, LANES), F32)] * 4, compiler_params=_params("parallel"))(gathered, w, m, v)


def first_norm(h, g, name):
    T, D = h.shape
    tm = _tile(T, ROW_TILE)

    def body(h_ref, g_ref, hn_ref):
        hn_ref[...] = _rms(h_ref[...], g_ref[...]).astype(BF16)

    row = _rows(tm, D)
    return pl.pallas_call(
        body, name=name, grid=(T // tm,), in_specs=[row, _full((1, D))], out_specs=row,
        out_shape=jax.ShapeDtypeStruct((T, D), BF16), compiler_params=_params("parallel"))(h, g)


_NT = (((1,), (1,)), ((), ()))
_TN = (((0,), (0,)), ((), ()))


def rowmm(a, w, transposed_w, bias, out_dtype, name, after=None):
    M, K = a.shape
    N = w.shape[0] if transposed_w else w.shape[1]
    tm = _tile(M, ROW_TILE)
    tn = _tile(N, COL_TILE)

    def body(*refs):
        a_ref, w_ref = refs[:2]
        o_ref = refs[-1]
        av = a_ref[...]
        for j in range(N // tn):
            cols = pl.ds(j * tn, tn)
            if transposed_w:
                acc = lax.dot_general(av, w_ref[cols, :], _NT, preferred_element_type=F32)
            else:
                acc = jnp.dot(av, w_ref[:, cols], preferred_element_type=F32)
            if bias is not None:
                acc = acc + refs[2][:, cols]
            o_ref[:, cols] = acc.astype(out_dtype)

    ins, in_specs = [a, w], [_rows(tm, K), _resident(w.shape)]
    if bias is not None:
        ins.append(bias)
        in_specs.append(_full((1, N)))
    if after is not None:
        ins.append(after)
        in_specs.append(pl.BlockSpec(memory_space=pl.ANY))
    return pl.pallas_call(
        body, name=name, grid=(M // tm,), in_specs=in_specs, out_specs=_rows(tm, N),
        out_shape=jax.ShapeDtypeStruct((M, N), out_dtype), compiler_params=_params("parallel"))(*ins)


def _rows_times_weight(a_ref, w_ref, acc_ref):
    N = w_ref.shape[1]
    tn = _tile(N, COL_TILE)
    av = a_ref[...]
    for j in range(N // tn):
        cols = pl.ds(j * tn, tn)
        acc_ref[:, cols] = jnp.dot(av, w_ref[:, cols], preferred_element_type=F32)


def mm_add_norm(a, w, h, g_post, g_pre, name):
    T, K = a.shape
    D = w.shape[1]
    tm = _tile(T, ROW_TILE)

    def body(a_ref, w_ref, h_ref, gp_ref, g_ref, y_ref, ho_ref, hn_ref, acc_ref):
        _rows_times_weight(a_ref, w_ref, acc_ref)
        y = acc_ref[...]
        y_ref[...] = y.astype(BF16)
        hv = h_ref[...] + _rms(y, gp_ref[...])
        ho_ref[...] = hv
        hn_ref[...] = _rms(hv, g_ref[...]).astype(BF16)

    row, vec = _rows(tm, D), _full((1, D))
    return pl.pallas_call(
        body, name=name, grid=(T // tm,), in_specs=[_rows(tm, K), _resident(w.shape), row, vec, vec],
        out_specs=[row, row, row],
        out_shape=[jax.ShapeDtypeStruct((T, D), BF16), jax.ShapeDtypeStruct((T, D), F32),
                   jax.ShapeDtypeStruct((T, D), BF16)],
        scratch_shapes=[pltpu.VMEM((tm, D), F32)], compiler_params=_params("parallel"))(a, w, h, g_post, g_pre)


def mm_add_norm_loss(a, w, h, g_post, target, name):
    T, K = a.shape
    D = w.shape[1]
    tm = _tile(T, ROW_TILE)

    def body(a_ref, w_ref, h_ref, gp_ref, t_ref, dh_ref, dy_ref, loss_ref, dgp_ref, acc_ref):
        @pl.when(pl.program_id(0) == 0)
        def _():
            loss_ref[...] = jnp.zeros_like(loss_ref)
            dgp_ref[...] = jnp.zeros_like(dgp_ref)
        _rows_times_weight(a_ref, w_ref, acc_ref)
        y = acc_ref[...]
        err = h_ref[...] + _rms(y, gp_ref[...]) - t_ref[...]
        dh = err * (1.0 / D)
        dh_ref[...] = dh
        loss_ref[...] += jnp.sum(err * err)
        dy, dgp = _rms_bwd(y, gp_ref[...], dh)
        dy_ref[...] = dy.astype(BF16)
        dgp_ref[...] += dgp

    row, vec = _rows(tm, D), _full((1, D))
    return pl.pallas_call(
        body, name=name, grid=(T // tm,), in_specs=[_rows(tm, K), _resident(w.shape), row, vec, row],
        out_specs=[row, row, _full((8, LANES)), vec],
        out_shape=[jax.ShapeDtypeStruct((T, D), F32), jax.ShapeDtypeStruct((T, D), BF16),
                   jax.ShapeDtypeStruct((8, LANES), F32), jax.ShapeDtypeStruct((1, D), F32)],
        scratch_shapes=[pltpu.VMEM((tm, D), F32)], compiler_params=_params("arbitrary"))(a, w, h, g_post, target)


def mm_norm_bwd(a, w, dh_in, x, g_pre, post, name, after):
    T, K = a.shape
    D = w.shape[1]
    tm = _tile(T, ROW_TILE)
    row, vec = _rows(tm, D), _full((1, D))
    ins, in_specs = [a, w, dh_in, x, g_pre], [_rows(tm, K), _resident(w.shape), row, row, vec]
    outs = [jax.ShapeDtypeStruct((T, D), F32), jax.ShapeDtypeStruct((1, D), F32)]
    out_specs, names = [row, vec], ["dh", "dg_pre"]
    if post is not None:
        ins += list(post)
        in_specs += [row, vec]
        outs += [jax.ShapeDtypeStruct((T, D), BF16), jax.ShapeDtypeStruct((1, D), F32)]
        out_specs += [row, vec]
        names += ["dy", "dg_post"]
    n_in = len(ins) + 1

    def body(*refs):
        a_ref, w_ref, dh_ref_in, x_ref, g_ref = refs[:5]
        out_refs, acc_ref = refs[n_in:-1], refs[-1]
        first = pl.program_id(0) == 0
        _rows_times_weight(a_ref, w_ref, acc_ref)
        dx, dg = _rms_bwd(x_ref[...], g_ref[...], acc_ref[...])
        dh = dh_ref_in[...] + dx
        out_refs[0][...] = dh

        @pl.when(first)
        def _():
            for r in out_refs[1::2]:
                r[...] = jnp.zeros_like(r)
        out_refs[1][...] += dg
        if post is not None:
            y_ref, gp_ref = refs[5:7]
            dy, dgp = _rms_bwd(y_ref[...].astype(F32), gp_ref[...], dh)
            out_refs[2][...] = dy.astype(BF16)
            out_refs[3][...] += dgp

    res = pl.pallas_call(
        body, name=name, grid=(T // tm,), in_specs=in_specs + [pl.BlockSpec(memory_space=pl.ANY)],
        out_specs=out_specs, out_shape=outs, scratch_shapes=[pltpu.VMEM((tm, D), F32)],
        compiler_params=_params("arbitrary"))(*ins, after)
    return dict(zip(names, res))


def grad_mm(a, b, name, after=()):
    T, N = a.shape
    K = b.shape[1]
    tn = _tile(N, GRAD_TILE_ROWS)
    tt = _tile(T, GRAD_TILE_DEPTH)

    def body(a_ref, b_ref, *rest):
        o_ref, acc_ref = rest[-2:]
        t = pl.program_id(1)

        @pl.when(t == 0)
        def _():
            acc_ref[...] = jnp.zeros_like(acc_ref)
        acc_ref[...] += lax.dot_general(a_ref[...], b_ref[...], _TN, preferred_element_type=F32)

        @pl.when(t == pl.num_programs(1) - 1)
        def _():
            o_ref[...] = acc_ref[...].astype(BF16)

    return pl.pallas_call(
        body, name=name, grid=(N // tn, T // tt),
        in_specs=[pl.BlockSpec((tt, tn), lambda j, t: (t, j)), pl.BlockSpec((tt, K), lambda j, t: (t, 0))]
        + [pl.BlockSpec(memory_space=pl.ANY)] * len(after),
        out_specs=pl.BlockSpec((tn, K), lambda j, t: (j, 0)),
        out_shape=jax.ShapeDtypeStruct((N, K), BF16),
        scratch_shapes=[pltpu.VMEM((tn, K), F32)],
        compiler_params=_params("parallel", "arbitrary"))(a, b, *after)


def ffn_up(fn, w_gu_t, name):
    T, D = fn.shape
    F = w_gu_t.shape[0] // 2
    tm = _tile(T, ROW_TILE)
    tn = _tile(F, COL_TILE)

    def body(a_ref, w_ref, act_ref, part_ref):
        av = a_ref[...]
        for j in range(F // tn):
            g = lax.dot_general(av, w_ref[pl.ds(j * tn, tn), :], _NT, preferred_element_type=F32)
            up = lax.dot_general(av, w_ref[pl.ds(F + j * tn, tn), :], _NT, preferred_element_type=F32)
            sg = _sigmoid(g)
            silu = g * sg
            act_ref[:, pl.ds(j * tn, tn)] = (silu * up).astype(BF16)
            part_ref[:, pl.ds(j * tn, tn)] = (up * (sg + silu * (1.0 - sg))).astype(BF16)
            part_ref[:, pl.ds(F + j * tn, tn)] = silu.astype(BF16)

    return pl.pallas_call(
        body, name=name, grid=(T // tm,), in_specs=[_rows(tm, D), _resident(w_gu_t.shape)],
        out_specs=[_rows(tm, F), _rows(tm, 2 * F)],
        out_shape=[jax.ShapeDtypeStruct((T, F), BF16), jax.ShapeDtypeStruct((T, 2 * F), BF16)],
        compiler_params=_params("parallel"))(fn, w_gu_t)


def ffn_dact(df, w_down, partials, name, after=()):
    T, D = df.shape
    F = w_down.shape[0]
    tm = _tile(T, ROW_TILE)
    tn = _tile(F, COL_TILE)

    def body(d_ref, w_ref, part_ref, *rest):
        o_ref = rest[-1]
        dv = d_ref[...]
        for j in range(F // tn):
            dact = lax.dot_general(dv, w_ref[pl.ds(j * tn, tn), :], _NT, preferred_element_type=F32)
            for cols in (pl.ds(j * tn, tn), pl.ds(F + j * tn, tn)):
                o_ref[:, cols] = (dact * part_ref[:, cols].astype(F32)).astype(BF16)

    return pl.pallas_call(
        body, name=name, grid=(T // tm,),
        in_specs=[_rows(tm, D), _resident(w_down.shape), _rows(tm, 2 * F)] + [pl.BlockSpec(memory_space=pl.ANY)] * len(after),
        out_specs=_rows(tm, 2 * F), out_shape=jax.ShapeDtypeStruct((T, 2 * F), BF16),
        compiler_params=_params("parallel"))(df, w_down, partials, *after)


def _layernorm_stats(v):
    mu = jnp.mean(v, axis=-1, keepdims=True)
    cen = v - mu
    rstd = lax.rsqrt(jnp.mean(cen * cen, axis=-1, keepdims=True) + LN_EPS)
    return cen * rstd, rstd


GMLP_ROWS = 2 * CHUNK


def gmlp_gate_out(a, ln_g, ln_b, w_mask, b_s_t, w_out, h, g_post, g_pre, name):
    T, W2 = a.shape
    W = W2 // 2
    G = w_mask.shape[0]
    C = W // G
    D = w_out.shape[1]
    tm = min(T, GMLP_ROWS)

    def body(a_ref, lg_ref, lb_ref, w_ref, bs_ref, wo_ref, h_ref, gp_ref, g_ref,
             gated_ref, y_ref, ho_ref, hn_ref, acc_ref):
        for c in range(tm // CHUNK):
            rows = pl.ds(c * CHUNK, CHUNK)
            xhat, _ = _layernorm_stats(_gelu(a_ref[rows, W:].astype(F32)))
            vln = (xhat * lg_ref[...] + lb_ref[...]).astype(BF16)
            for g in range(G):
                cols = pl.ds(g * C, C)
                sv = jnp.dot(w_ref[g], vln[:, g * C:(g + 1) * C], preferred_element_type=F32) + bs_ref[:, g:g + 1]
                gated_ref[rows, cols] = (_gelu(a_ref[rows, cols].astype(F32)) * sv).astype(BF16)
        _rows_times_weight(gated_ref, wo_ref, acc_ref)
        y = acc_ref[...]
        y_ref[...] = y.astype(BF16)
        hv = h_ref[...] + _rms(y, gp_ref[...])
        ho_ref[...] = hv
        hn_ref[...] = _rms(hv, g_ref[...]).astype(BF16)

    row, vec = _rows(tm, D), _full((1, D))
    return pl.pallas_call(
        body, name=name, grid=(T // tm,),
        in_specs=[_rows(tm, W2), _full((1, W)), _full((1, W)), _full(w_mask.shape), _full(b_s_t.shape),
                  _resident(w_out.shape), row, vec, vec],
        out_specs=[_rows(tm, W), row, row, row],
        out_shape=[jax.ShapeDtypeStruct((T, W), BF16), jax.ShapeDtypeStruct((T, D), BF16),
                   jax.ShapeDtypeStruct((T, D), F32), jax.ShapeDtypeStruct((T, D), BF16)],
        scratch_shapes=[pltpu.VMEM((tm, D), F32)],
        compiler_params=_params("parallel"))(a, ln_g, ln_b, w_mask, b_s_t, w_out, h, g_post, g_pre)


def gmlp_gate_bwd_dx(a, dmix, w_out, ln_g, ln_b, w_mask, w_mask_t, b_s_t, group_onehot, w_in_t,
                     dh_in, x, g_pre, post, name, after):
    T, W2 = a.shape
    W = W2 // 2
    G = w_mask.shape[0]
    C = W // G
    D = w_out.shape[1]
    tm = min(T, GMLP_ROWS)
    tn = _tile(W, COL_TILE)
    row, vec = _rows(tm, D), _full((1, D))
    ins = [a, dmix, w_out, ln_g, ln_b, w_mask, w_mask_t, b_s_t, group_onehot, w_in_t, dh_in, x, g_pre]
    in_specs = [_rows(tm, W2), row, _resident(w_out.shape), _full((1, W)), _full((1, W)), _full(w_mask.shape),
                _full(w_mask_t.shape), _full(b_s_t.shape), _full(group_onehot.shape), _resident(w_in_t.shape), row, row, vec]
    names = ["da", "dw_s", "db_s_t", "dln_g", "dln_b", "db_in", "dh", "dg_pre"]
    outs = [jax.ShapeDtypeStruct((T, W2), BF16), jax.ShapeDtypeStruct((G, CHUNK, CHUNK), F32),
            jax.ShapeDtypeStruct((CHUNK, LANES), F32), jax.ShapeDtypeStruct((1, W), F32),
            jax.ShapeDtypeStruct((1, W), F32), jax.ShapeDtypeStruct((1, W2), F32),
            jax.ShapeDtypeStruct((T, D), F32), jax.ShapeDtypeStruct((1, D), F32)]
    out_specs = [_rows(tm, W2), _full((G, CHUNK, CHUNK)), _full((CHUNK, LANES)), _full((1, W)), _full((1, W)),
                 _full((1, W2)), row, vec]
    if post is not None:
        ins += list(post)
        in_specs += [row, vec]
        names += ["dy", "dg_post"]
        outs += [jax.ShapeDtypeStruct((T, D), BF16), jax.ShapeDtypeStruct((1, D), F32)]
        out_specs += [row, vec]
    n_in = len(ins) + 1

    def body(*refs):
        (a_ref, dm_ref, wo_ref, lg_ref, lb_ref, w_ref, wt_ref, bs_ref, e_ref, wi_ref, dh_in_ref, x_ref, gpre_ref) = refs[:13]
        out_refs = refs[n_in:-3]
        da_ref, dws_ref, dbs_ref, dlg_ref, dlb_ref, dbin_ref, dh_ref, dgpre_ref = out_refs[:8]
        dgated_ref, dvln_ref, acc_ref = refs[-3:]

        @pl.when(pl.program_id(0) == 0)
        def _():
            for r in out_refs[1:6] + out_refs[7::2]:
                r[...] = jnp.zeros_like(r)
        dm = dm_ref[...]
        for j in range(W // tn):
            cols = pl.ds(j * tn, tn)
            dgated_ref[:, cols] = lax.dot_general(dm, wo_ref[cols, :], _NT, preferred_element_type=F32)
        causal = (lax.broadcasted_iota(jnp.int32, (CHUNK, CHUNK), 1)
                  <= lax.broadcasted_iota(jnp.int32, (CHUNK, CHUNK), 0))
        for c in range(tm // CHUNK):
            rows = pl.ds(c * CHUNK, CHUNK)
            gelu_v, gelu_grad_v = _gelu_and_grad(a_ref[rows, W:].astype(F32))
            xhat, rstd = _layernorm_stats(gelu_v)
            vln = (xhat * lg_ref[...] + lb_ref[...]).astype(BF16)
            for g in range(G):
                cols = pl.ds(g * C, C)
                vg = vln[:, g * C:(g + 1) * C]
                gelu_u, gelu_grad_u = _gelu_and_grad(a_ref[rows, cols].astype(F32))
                dgated = dgated_ref[rows, cols]
                dsv = (dgated * gelu_u).astype(BF16)
                dbs_ref[...] += jnp.dot(dsv, e_ref[cols, :], preferred_element_type=F32)
                sv = jnp.dot(w_ref[g], vg, preferred_element_type=F32) + bs_ref[:, g:g + 1]
                dau = dgated * sv * gelu_grad_u
                da_ref[rows, cols] = dau.astype(BF16)
                dbin_ref[:, cols] += jnp.sum(dau, axis=0, keepdims=True)
                dws_ref[g] += jnp.where(causal, lax.dot_general(dsv, vg, _NT, preferred_element_type=F32), 0.0)
                dvln_ref[:, cols] = jnp.dot(wt_ref[g], dsv, preferred_element_type=F32)
            dvln = dvln_ref[...]
            dlg_ref[...] += jnp.sum(dvln * xhat, axis=0, keepdims=True)
            dlb_ref[...] += jnp.sum(dvln, axis=0, keepdims=True)
            dxhat = dvln * lg_ref[...]
            dv = rstd * (dxhat - jnp.mean(dxhat, axis=-1, keepdims=True)
                         - xhat * jnp.mean(dxhat * xhat, axis=-1, keepdims=True))
            dav = dv * gelu_grad_v
            da_ref[rows, W:] = dav.astype(BF16)
            dbin_ref[:, W:] += jnp.sum(dav, axis=0, keepdims=True)
        _rows_times_weight(da_ref, wi_ref, acc_ref)
        dx, dg = _rms_bwd(x_ref[...], gpre_ref[...], acc_ref[...])
        dh = dh_in_ref[...] + dx
        dh_ref[...] = dh
        dgpre_ref[...] += dg
        if post is not None:
            y_ref, gp_ref = refs[13:15]
            dy, dgp = _rms_bwd(y_ref[...].astype(F32), gp_ref[...], dh)
            out_refs[8][...] = dy.astype(BF16)
            out_refs[9][...] += dgp

    res = pl.pallas_call(
        body, name=name, grid=(T // tm,), in_specs=in_specs + [pl.BlockSpec(memory_space=pl.ANY)],
        out_specs=out_specs, out_shape=outs,
        scratch_shapes=[pltpu.VMEM((tm, W), F32), pltpu.VMEM((CHUNK, W), F32), pltpu.VMEM((tm, D), F32)],
        compiler_params=_params("arbitrary"))(*ins, after)
    return dict(zip(names, res))


def rope_tables(pos, inv_freq_row, name):
    T = pos.shape[0]
    tm = _tile(T, ROW_TILE)

    def body(p_ref, f_ref, c_ref, s1_ref, s2_ref):
        ang = p_ref[...].astype(F32) * f_ref[...]
        lane = lax.broadcasted_iota(jnp.int32, (tm, LANES), 1) % HEAD_DIM
        sin = jnp.sin(ang)
        c_ref[...] = jnp.cos(ang)
        s1_ref[...] = jnp.where(lane < _HALF, -sin, 0.0)
        s2_ref[...] = jnp.where((lane >= _HALF) & (lane < ROPE_DIM), sin, 0.0)

    tab = _rows(tm, LANES)
    return pl.pallas_call(
        body, name=name, grid=(T // tm,), in_specs=[_rows(tm, 1), _full((1, LANES))], out_specs=[tab] * 3,
        out_shape=[jax.ShapeDtypeStruct((T, LANES), F32)] * 3, compiler_params=_params("parallel"))(pos, inv_freq_row)


_HALF = ROPE_DIM // 2


def _slabs(x):
    return [x[:, b * LANES:(b + 1) * LANES] for b in range(x.shape[1] // LANES)]


def _rotate(x, c, s1, s2):
    return [xs * c + pltpu.roll(xs, LANES - _HALF, 1) * s1 + pltpu.roll(xs, _HALF, 1) * s2 for xs in _slabs(x)]


def _rotate_transposed(dy, c, s1, s2):
    return [ds * c + pltpu.roll(ds * s1, _HALF, 1) + pltpu.roll(ds * s2, LANES - _HALF, 1) for ds in _slabs(dy)]


def qkv_rope(hn, w_t, bias, tabs, q_width, kv_width, name):
    T, D = hn.shape
    N = w_t.shape[0]
    QK = q_width + kv_width
    tm = _tile(T, ROW_TILE)
    tn = 2 * LANES
    scale = HEAD_DIM ** -0.5

    def body(a_ref, w_ref, b_ref, c_ref, s1_ref, s2_ref, qkv_ref, qk_ref):
        av = a_ref[...]
        tabs_v = (c_ref[...], s1_ref[...], s2_ref[...])
        for j in range(N // tn):
            cols = pl.ds(j * tn, tn)
            acc = lax.dot_general(av, w_ref[cols, :], _NT, preferred_element_type=F32) + b_ref[:, cols]
            qkv_ref[:, cols] = acc.astype(BF16)
            if j * tn < QK:
                for b, y in enumerate(_rotate(acc, *tabs_v)):
                    if j * tn < q_width:
                        y = y * scale
                    qk_ref[:, pl.ds(j * tn + b * LANES, LANES)] = y.astype(BF16)

    tab = _rows(tm, LANES)
    return pl.pallas_call(
        body, name=name, grid=(T // tm,), in_specs=[_rows(tm, D), _resident(w_t.shape), _full((1, N)), tab, tab, tab],
        out_specs=[_rows(tm, N), _rows(tm, QK)],
        out_shape=[jax.ShapeDtypeStruct((T, N), BF16), jax.ShapeDtypeStruct((T, QK), BF16)],
        compiler_params=_params("parallel"))(hn, w_t, bias, *tabs)


def rope_bwd_dx(dq, dk, dv, tabs, w, dh_in, x, g_pre, post, name, after):
    T, Q = dq.shape
    KV = dk.shape[1]
    N = Q + 2 * KV
    D = w.shape[1]
    tm = _tile(T, ROW_TILE)
    scale = HEAD_DIM ** -0.5
    row, vec, tab = _rows(tm, D), _full((1, D)), _rows(tm, LANES)
    ins = [dq, dk, dv, *tabs, w, dh_in, x, g_pre]
    in_specs = [_rows(tm, Q), _rows(tm, KV), _rows(tm, KV), tab, tab, tab, _resident(w.shape), row, row, vec]
    names = ["dqkv", "dbias", "dh", "dg_pre"]
    outs = [jax.ShapeDtypeStruct((T, N), BF16), jax.ShapeDtypeStruct((1, N), F32),
            jax.ShapeDtypeStruct((T, D), F32), jax.ShapeDtypeStruct((1, D), F32)]
    out_specs = [_rows(tm, N), _full((1, N)), row, vec]
    if post is not None:
        ins += list(post)
        in_specs += [row, vec]
        names += ["dy", "dg_post"]
        outs += [jax.ShapeDtypeStruct((T, D), BF16), jax.ShapeDtypeStruct((1, D), F32)]
        out_specs += [row, vec]
    n_in = len(ins) + 1

    def body(*refs):
        dq_ref, dk_ref, dv_ref, c_ref, s1_ref, s2_ref, w_ref, dh_in_ref, x_ref, g_ref = refs[:10]
        out_refs, acc_ref = refs[n_in:-1], refs[-1]
        dqkv_ref, db_ref, dh_ref, dg_ref = out_refs[:4]

        @pl.when(pl.program_id(0) == 0)
        def _():
            for r in out_refs[1::2]:
                r[...] = jnp.zeros_like(r)
        tabs_v = (c_ref[...], s1_ref[...], s2_ref[...])
        pieces = [s * scale for s in _rotate_transposed(dq_ref[...], *tabs_v)]
        pieces += _rotate_transposed(dk_ref[...], *tabs_v)
        pieces += _slabs(dv_ref[...])
        for b, y in enumerate(pieces):
            cols = pl.ds(b * LANES, LANES)
            dqkv_ref[:, cols] = y.astype(BF16)
            db_ref[:, cols] += jnp.sum(y, axis=0, keepdims=True)
        _rows_times_weight(dqkv_ref, w_ref, acc_ref)
        dx, dg = _rms_bwd(x_ref[...], g_ref[...], acc_ref[...])
        dh = dh_in_ref[...] + dx
        dh_ref[...] = dh
        dg_ref[...] += dg
        if post is not None:
            y_ref, gp_ref = refs[10:12]
            dy, dgp = _rms_bwd(y_ref[...].astype(F32), gp_ref[...], dh)
            out_refs[4][...] = dy.astype(BF16)
            out_refs[5][...] += dgp

    res = pl.pallas_call(
        body, name=name, grid=(T // tm,), in_specs=in_specs + [pl.BlockSpec(memory_space=pl.ANY)],
        out_specs=out_specs, out_shape=outs, scratch_shapes=[pltpu.VMEM((tm, D), F32)],
        compiler_params=_params("arbitrary"))(*ins, after)
    return dict(zip(names, res))


def _band_mask(n, heads=1):
    qi = lax.broadcasted_iota(jnp.int32, (heads * CHUNK, 2 * CHUNK), 0) % CHUNK
    sj = lax.broadcasted_iota(jnp.int32, (heads * CHUNK, 2 * CHUNK), 1)
    return (sj > qi) & (sj <= qi + CHUNK) & ((n > 0) | (sj >= CHUNK))


def _kv_head(kc_ref, kp_ref, vc_ref, vp_ref, kh):
    lanes = slice(kh * HEAD_DIM, (kh + 1) * HEAD_DIM)
    return (jnp.concatenate([kp_ref[:, lanes], kc_ref[:, lanes]], axis=0),
            jnp.concatenate([vp_ref[:, lanes], vc_ref[:, lanes]], axis=0))


def _row_sums(x):
    return jnp.dot(x, jnp.ones((x.shape[1], LANES), BF16), preferred_element_type=F32)


def _lanes(v, n):
    return v[:, :n] if n <= LANES else jnp.concatenate([v] * (n // LANES), axis=-1)


def _head_probs(q, kk, valid, sink, sums_on_mxu):
    s = jnp.where(valid, lax.dot_general(q, kk, _NT, preferred_element_type=F32), NEG_INF)
    m = jnp.maximum(jnp.max(s, axis=-1, keepdims=True), sink)
    p = jnp.exp(s - m)
    e_sink = jnp.exp(sink - m)
    total = _row_sums(p.astype(BF16)) if sums_on_mxu else jnp.sum(p, axis=-1, keepdims=True)
    return p.astype(BF16), 1.0 / (total + e_sink), e_sink


def _attn_specs(q_width, kv_width, order):
    qb = q_width // kv_width
    prev = lambda i: jnp.maximum(order(i) - 1, 0)
    return [pl.BlockSpec((CHUNK, q_width), lambda i: (order(i), 0)),
            pl.BlockSpec((CHUNK, kv_width), lambda i: (order(i), qb)),
            pl.BlockSpec((CHUNK, kv_width), lambda i: (prev(i), qb)),
            pl.BlockSpec((CHUNK, kv_width), lambda i: (order(i), qb + 1)),
            pl.BlockSpec((CHUNK, kv_width), lambda i: (prev(i), qb + 1))]


def attn_fwd(qk, qkv, sinks, q_width, kv_width, name):
    T = qk.shape[0]
    group = q_width // kv_width

    def body(q_ref, kc_ref, kp_ref, vc_ref, vp_ref, sink_ref, o_ref):
        valid = _band_mask(pl.program_id(0))
        for kh in range(kv_width // HEAD_DIM):
            kk, vv = _kv_head(kc_ref, kp_ref, vc_ref, vp_ref, kh)
            for g in range(group):
                h = kh * group + g
                lanes = slice(h * HEAD_DIM, (h + 1) * HEAD_DIM)
                p, inv, _ = _head_probs(q_ref[:, lanes], kk, valid, sink_ref[h], False)
                o_ref[:, lanes] = (jnp.dot(p, vv, preferred_element_type=F32) * inv).astype(BF16)

    specs = _attn_specs(q_width, kv_width, lambda i: i)
    return pl.pallas_call(
        body, name=name, grid=(T // CHUNK,),
        in_specs=specs + [pl.BlockSpec(memory_space=pltpu.SMEM)],
        out_specs=_rows(CHUNK, q_width), out_shape=jax.ShapeDtypeStruct((T, q_width), BF16),
        compiler_params=_params("parallel"))(qk, qk, qk, qkv, qkv, sinks)


def attn_bwd(qk, qkv, do, sinks, q_width, kv_width, name):
    T = qk.shape[0]
    NB = T // CHUNK
    group = q_width // kv_width

    def body(q_ref, kc_ref, kp_ref, vc_ref, vp_ref, do_ref, sink_ref, dq_ref, dk_ref, dv_ref, ds_ref, ck_ref, cv_ref):
        i = pl.program_id(0)
        n = NB - 1 - i

        @pl.when(i == 0)
        def _():
            ck_ref[...] = jnp.zeros_like(ck_ref)
            cv_ref[...] = jnp.zeros_like(cv_ref)
            ds_ref[...] = jnp.zeros_like(ds_ref)
        lane = lax.broadcasted_iota(jnp.int32, (1, LANES), 1)
        dsink_row = jnp.zeros((1, LANES), F32)
        valid = _band_mask(n, group)
        head = lax.broadcasted_iota(jnp.int32, (group * CHUNK, 1), 0) // CHUNK
        for kh in range(kv_width // HEAD_DIM):
            kk, vv = _kv_head(kc_ref, kp_ref, vc_ref, vp_ref, kh)
            heads = [slice((kh * group + g) * HEAD_DIM, (kh * group + g + 1) * HEAD_DIM) for g in range(group)]
            q = jnp.concatenate([q_ref[:, hs] for hs in heads], axis=0)
            do = jnp.concatenate([do_ref[:, hs] for hs in heads], axis=0)
            sink = jnp.zeros((group * CHUNK, 1), F32)
            for g in range(group):
                sink = jnp.where(head == g, sink_ref[kh * group + g], sink)
            p, inv, e_sink = _head_probs(q, kk, valid, sink, True)
            p = p.astype(F32) * _lanes(inv, 2 * CHUNK)
            dp = lax.dot_general(do, vv, _NT, preferred_element_type=F32)
            delta = _row_sums((p * dp).astype(BF16))
            ds = (p * (dp - _lanes(delta, 2 * CHUNK))).astype(BF16)
            dsink = -e_sink * inv[:, :1] * delta[:, :1]
            dq = jnp.dot(ds, kk, preferred_element_type=F32)
            for g, hs in enumerate(heads):
                rows = slice(g * CHUNK, (g + 1) * CHUNK)
                dsink_row = dsink_row + jnp.where(lane == kh * group + g, jnp.sum(dsink[rows]), 0.0)
                dq_ref[:, hs] = dq[rows]
            dkk = lax.dot_general(ds, q, _TN, preferred_element_type=F32)
            dvv = lax.dot_general(p.astype(BF16), do, _TN, preferred_element_type=F32)
            lanes = slice(kh * HEAD_DIM, (kh + 1) * HEAD_DIM)
            dk_ref[:, lanes] = dkk[CHUNK:] + ck_ref[:, lanes]
            dv_ref[:, lanes] = dvv[CHUNK:] + cv_ref[:, lanes]
            ck_ref[:, lanes] = dkk[:CHUNK]
            cv_ref[:, lanes] = dvv[:CHUNK]
        ds_ref[0:1, :] += dsink_row

    order = lambda i: NB - 1 - i
    specs = _attn_specs(q_width, kv_width, order)
    kv_out = pl.BlockSpec((CHUNK, kv_width), lambda i: (order(i), 0))
    q_rows = pl.BlockSpec((CHUNK, q_width), lambda i: (order(i), 0))
    return pl.pallas_call(
        body, name=name, grid=(NB,),
        in_specs=specs + [q_rows, pl.BlockSpec(memory_space=pltpu.SMEM)],
        out_specs=[q_rows, kv_out, kv_out, _full((8, LANES))],
        out_shape=[jax.ShapeDtypeStruct((T, q_width), F32), jax.ShapeDtypeStruct((T, kv_width), F32),
                   jax.ShapeDtypeStruct((T, kv_width), F32), jax.ShapeDtypeStruct((8, LANES), F32)],
        scratch_shapes=[pltpu.VMEM((CHUNK, kv_width), F32), pltpu.VMEM((CHUNK, kv_width), F32)],
        compiler_params=_params("arbitrary"))(qk, qk, qk, qkv, qkv, do, sinks)


def _blocked(w):
    return w.reshape(N_DEV, w.shape[0] // N_DEV, w.shape[1])


def kernel(x, positions, pre_mix_g, post_mix_g, pre_ffn_g, post_ffn_g, a_w_in, a_b_in, a_ln_g, a_ln_b, a_w_s, a_b_s, a_w_out, b_w_qkv, b_b_qkv, b_sinks, b_w_o, ffn_w_gu, ffn_w_down, loss_target, m_pre_mix_g, m_post_mix_g, m_pre_ffn_g, m_post_ffn_g, m_a_w_in, m_a_b_in, m_a_ln_g, m_a_ln_b, m_a_w_s, m_a_b_s, m_a_w_out, m_b_w_qkv, m_b_b_qkv, m_b_sinks, m_b_w_o, m_ffn_w_gu, m_ffn_w_down, v_pre_mix_g, v_post_mix_g, v_pre_ffn_g, v_post_ffn_g, v_a_w_in, v_a_b_in, v_a_ln_g, v_a_ln_b, v_a_w_s, v_a_b_s, v_a_w_out, v_b_w_qkv, v_b_b_qkv, v_b_sinks, v_b_w_o, v_ffn_w_gu, v_ffn_w_down):
    weights = dict(pre_mix_g=pre_mix_g, post_mix_g=post_mix_g, pre_ffn_g=pre_ffn_g, post_ffn_g=post_ffn_g,
                   a_w_in=a_w_in, a_b_in=a_b_in, a_ln_g=a_ln_g, a_ln_b=a_ln_b, a_w_s=a_w_s, a_b_s=a_b_s,
                   a_w_out=a_w_out, b_w_qkv=b_w_qkv, b_b_qkv=b_b_qkv, b_sinks=b_sinks, b_w_o=b_w_o,
                   ffn_w_gu=ffn_w_gu, ffn_w_down=ffn_w_down)
    mom_m = dict(pre_mix_g=m_pre_mix_g, post_mix_g=m_post_mix_g, pre_ffn_g=m_pre_ffn_g, post_ffn_g=m_post_ffn_g,
                 a_w_in=m_a_w_in, a_b_in=m_a_b_in, a_ln_g=m_a_ln_g, a_ln_b=m_a_ln_b, a_w_s=m_a_w_s, a_b_s=m_a_b_s,
                 a_w_out=m_a_w_out, b_w_qkv=m_b_w_qkv, b_b_qkv=m_b_b_qkv, b_sinks=m_b_sinks, b_w_o=m_b_w_o,
                 ffn_w_gu=m_ffn_w_gu, ffn_w_down=m_ffn_w_down)
    mom_v = dict(pre_mix_g=v_pre_mix_g, post_mix_g=v_post_mix_g, pre_ffn_g=v_pre_ffn_g, post_ffn_g=v_post_ffn_g,
                 a_w_in=v_a_w_in, a_b_in=v_a_b_in, a_ln_g=v_a_ln_g, a_ln_b=v_a_ln_b, a_w_s=v_a_w_s, a_b_s=v_a_b_s,
                 a_w_out=v_a_w_out, b_w_qkv=v_b_w_qkv, b_b_qkv=v_b_b_qkv, b_sinks=v_b_sinks, b_w_o=v_b_w_o,
                 ffn_w_gu=v_ffn_w_gu, ffn_w_down=v_ffn_w_down)
    names = list(weights)
    big = ("a_w_in", "b_w_qkv", "ffn_w_gu", "a_w_out", "b_w_o", "ffn_w_down")

    T, D = x.shape[1], x.shape[2]
    depth = pre_mix_g.shape[0]
    n_heads = b_sinks.shape[1]
    q_width = n_heads * HEAD_DIM
    kv_width = N_KV_HEADS * HEAD_DIM
    G = a_w_s.shape[1]
    W = a_ln_g.shape[1]
    device = _index(_position())

    dev = device.reshape(1).astype(jnp.int32)

    def layer_keys(i):
        mixer = ("a_w_in", "a_w_out") if i % 2 == 0 else ("b_w_qkv", "b_w_o")
        return [(k, i // 2) for k in mixer] + [("ffn_w_gu", i), ("ffn_w_down", i)]

    as_view = ("b_w_qkv", "ffn_w_gu")
    in_kernel = ("a_w_in",)
    view = lambda k, t: jnp.swapaxes(t, 1, 2) if k in as_view else t
    w_view = {k: view(k, weights[k]) for k in big}
    m_view = {k: view(k, mom_m[k]) for k in big}
    v_view = {k: view(k, mom_v[k]) for k in big}

    groups_keys = [[key] for key in layer_keys(0)] + [layer_keys(i) for i in range(1, depth)]
    group_of = {}
    for g, keys in enumerate(groups_keys):
        layer = 0 if g < len(layer_keys(0)) else g - len(layer_keys(0)) + 1
        for pos, (k, _) in enumerate(keys):
            group_of[(layer, k)] = (g, pos)
    bias_land = lax.dynamic_update_slice(
        jnp.zeros((N_DEV,) + b_b_qkv.shape, F32), b_b_qkv[None], (device, 0, 0))
    h, hn = x[0], first_norm(x[0], pre_mix_g[0][None], "norm_first")
    first = len(layer_keys(0))
    in_gather, token = [], ()
    for name, lo, hi in (("gather_start_first", 0, first), ("gather_start_rest", first, len(groups_keys))):
        groups = [list(prep_weights([(w_view[k], l, k in in_kernel) for k, l in keys], dev, f"prep_{name}_{g}",
                                    after=token)) for g, keys in enumerate(groups_keys[lo:hi])]
        if lo:
            groups[0].append(bias_land)
        gather_sems, flat_lands, token = relay_start(groups, name)
        token = (token, hn)
        for grp, sems in zip(groups, gather_sems):
            in_gather.append(([flat_lands.pop(0) for _ in grp], sems))
    all_started = token[0]
    passed, gathered = {}, {}

    def pass_on(g, after):
        if g not in passed and g < len(in_gather):
            lands, sems = in_gather[g]
            passed[g], lands = relay_pass(lands, sems, after, f"gather_pass_{g}")
            in_gather[g] = (lands, sems)

    pass_on(0, (all_started,))

    def weight(i, k, *after, ahead=True):
        g, pos = group_of[(i, k)]
        if g not in gathered:
            if ahead:
                pass_on(g + 1, after)
            lands, sems = in_gather[g]
            gathered[g] = relay_wait(lands, sems, passed[g], after, f"gather_wait_{g}")
        w = gathered[g][pos]
        return w.reshape(N_DEV * w.shape[1], w.shape[2])

    causal = jnp.tril(jnp.ones((CHUNK, CHUNK), dtype=bool))
    w_mask = jnp.where(causal[None, None], a_w_s, 0.0).astype(BF16)
    w_mask_t = jnp.swapaxes(w_mask, 2, 3)
    b_s_t = jnp.swapaxes(a_b_s, 1, 2)
    group_onehot = (jnp.arange(W)[:, None] // (W // G) == jnp.arange(LANES)[None, :]).astype(BF16)
    lane = jnp.arange(LANES) % HEAD_DIM
    inv_freq = ROPE_THETA ** (-jnp.arange(0, ROPE_DIM, 2, dtype=F32) / ROPE_DIM)
    inv_freq_row = jnp.where(lane < ROPE_DIM, inv_freq[lane % (ROPE_DIM // 2)], 0.0)[None, :].astype(F32)
    tabs = rope_tables(positions.reshape(T, 1), inv_freq_row, "rope_tables")

    saved = []
    for i in range(depth):
        j = i // 2
        s = dict(h=h, hn=hn)
        post_pre = (post_mix_g[i][None], pre_ffn_g[i][None])
        if i % 2 == 0:
            s["a"] = rowmm(hn, weight(i, "a_w_in", hn, all_started), True, a_b_in[j][None], BF16, "gmlp_in")
            s["gated"], s["mix"], s["h2"], s["fn"] = gmlp_gate_out(
                s["a"], a_ln_g[j][None], a_ln_b[j][None], w_mask[j], b_s_t[j], weight(i, "a_w_out", s["a"], ahead=i > 0),
                h, *post_pre, "gmlp_gate_out")
            if i == 0:
                pass_on(group_of[(0, "ffn_w_gu")][0], (s["mix"],))
        else:
            w_qkv = weight(i, "b_w_qkv", hn)
            b_qkv_full = jnp.swapaxes(gathered[group_of[(1, "b_w_qkv")][0]][-1], 0, 1).reshape(b_b_qkv.shape[0], 1, -1)
            s["qkv"], s["qk"] = qkv_rope(hn, w_qkv, b_qkv_full[j], tabs, q_width, kv_width, "attn_qkv_rope")
            s["o"] = attn_fwd(s["qk"], s["qkv"], b_sinks[j], q_width, kv_width, "attn_core")
            s["mix"], s["h2"], s["fn"] = mm_add_norm(s["o"], weight(i, "b_w_o", s["o"]), h, *post_pre, "attn_out")
        s["act"], s["act_partials"] = ffn_up(s["fn"], weight(i, "ffn_w_gu", s["fn"]), "ffn_up")
        w_down = weight(i, "ffn_w_down", s["act"])
        saved.append(s)
        if i + 1 < depth:
            s["f"], h, hn = mm_add_norm(s["act"], w_down, s["h2"], post_ffn_g[i][None], pre_mix_g[i + 1][None], "ffn_down")
    small = {k: [None] * weights[k].shape[0] for k in names if k not in big}
    dh, df, loss_sum, small["post_ffn_g"][depth - 1] = mm_add_norm_loss(
        saved[-1]["act"], w_down, saved[-1]["h2"], post_ffn_g[depth - 1][None], loss_target[0], "ffn_down_loss")

    grads = {k: [None] * weights[k].shape[0] for k in big}
    in_flight = []

    def send_grads(keys, tag):
        sems, bufs, token = exchange_start([_blocked(grads[k][l]) for k, l in keys], "exchange_start_" + tag)
        in_flight.append((keys, sems, bufs, tag))
        return token

    replicated = [k for k in names if k not in big and k not in ("b_b_qkv", "a_w_s")]
    main = sum(weights[k].size for k in replicated)
    main_rows = -(-main // (LANES * LANES)) * LANES
    bias_size = b_b_qkv.shape[0] * N_DEV * b_b_qkv.shape[1]
    bias_rows = -(-bias_size // (8 * LANES)) * 8
    flat = lambda vals: [v.reshape(-1) for v in vals]

    def start_small_gather():
        packed = jnp.concatenate(
            flat(p for k in replicated for p in small[k])
            + [loss_sum[0, :1], jnp.zeros((main_rows * LANES - main - 1,), F32)] + flat(small["b_b_qkv"]) + [jnp.zeros((bias_rows * LANES - bias_size,), F32)]
        ).reshape(1, main_rows + bias_rows, LANES)
        spatial = jnp.stack(small["a_w_s"]).reshape(1, -1, LANES)
        return gather_start([prep_weights([(packed, 0, False), (spatial, 0, False)], dev, "place_small_grads", F32)],
                            "gather_start_small")

    sent = ()
    for i in reversed(range(depth)):
        j = i // 2
        s = saved[i]
        dgu = ffn_dact(df, weight(i, "ffn_w_down"), s["act_partials"], "ffn_dact", after=sent)
        alone = i == 0
        grads["ffn_w_down"][i] = grad_mm(s["act"], df, "ffn_down_grad")
        early = (send_grads(layer_keys(i)[3:], f"ffn_down_{i}"),) if alone else ()
        grads["ffn_w_gu"][i] = grad_mm(dgu, s["fn"], "ffn_up_grad", after=early)
        sent = send_grads(layer_keys(i)[2:3] if alone else layer_keys(i)[2:], f"ffn_{i}")
        r = mm_norm_bwd(dgu, weight(i, "ffn_w_gu"), dh, s["h2"], pre_ffn_g[i][None],
                        (s["mix"], post_mix_g[i][None]), "ffn_dx", sent)
        dh, dmix = r["dh"], r["dy"]
        small["pre_ffn_g"][i], small["post_mix_g"][i] = r["dg_pre"], r["dg_post"]
        post = (saved[i - 1]["f"], post_ffn_g[i - 1][None]) if i > 0 else None
        if i % 2 == 0:
            grads["a_w_out"][j] = grad_mm(s["gated"], dmix, "gmlp_out_grad")
            if alone:
                sent = send_grads(layer_keys(i)[1:2], f"mixer_out_{i}")
            r = gmlp_gate_bwd_dx(
                s["a"], dmix, weight(i, "a_w_out"), a_ln_g[j][None], a_ln_b[j][None], w_mask[j], w_mask_t[j], b_s_t[j],
                group_onehot, weight(i, "a_w_in"), dh, s["h"], pre_mix_g[i][None], post, "gmlp_gate_bwd_dx", sent)
            small["a_w_s"][j], small["a_b_s"][j] = r["dw_s"], r["db_s_t"][:, :G].T
            small["a_ln_g"][j], small["a_ln_b"][j], small["a_b_in"][j] = r["dln_g"][0], r["dln_b"][0], r["db_in"][0]
            small["pre_mix_g"][i] = r["dg_pre"]
            first_layer = ()
            if i == 0:
                small_gather = start_small_gather()
                first_layer = (small_gather[2],)
            grads["a_w_in"][j] = grad_mm(r["da"], s["hn"], "gmlp_in_grad", after=first_layer)
            sent = (send_grads(layer_keys(i)[:1] if alone else layer_keys(i)[:2], f"mixer_{i}"),)
        else:
            grads["b_w_o"][j] = grad_mm(s["o"], dmix, "attn_out_grad")
            do = rowmm(dmix, weight(i, "b_w_o"), True, None, BF16, "attn_do")
            dq, dk, dv, dsink = attn_bwd(s["qk"], s["qkv"], do, b_sinks[j], q_width, kv_width, "attn_core_bwd")
            r = rope_bwd_dx(dq, dk, dv, tabs, weight(i, "b_w_qkv"), dh, s["h"], pre_mix_g[i][None], post,
                            "attn_rope_bwd_dx", sent)
            small["b_sinks"][j], small["b_b_qkv"][j] = dsink[0, :n_heads], r["dbias"][0]
            grads["b_w_qkv"][j] = grad_mm(r["dqkv"], s["hn"], "attn_qkv_grad")
            sent = (send_grads(layer_keys(i)[:2], f"mixer_{i}"),)
        dh, small["pre_mix_g"][i] = r["dh"], r["dg_pre"]
        if i > 0:
            df, small["post_ffn_g"][i - 1] = r["dy"], r["dg_post"]
    grad_x = dh[None]

    small_sems, small_lands, small_sent = small_gather
    pack_main = lambda src: jnp.concatenate(
        flat(src[k] for k in replicated) + [jnp.ones((main_rows * LANES - main,), F32)]).reshape(main_rows, LANES)
    packed_w, packed_m, packed_v = pack_main(weights), pack_main(mom_m), pack_main(mom_v)

    stacked = {k: [lax.empty(w_view[k].shape, F32) for _ in range(4)] for k in big}

    def update_group(entry, after):
        keys, sems, bufs, tag = entry
        partials, landed = exchange_wait(bufs, sems, after, "exchange_wait_" + tag)
        for (k, l), part, land in zip(keys, partials, landed):
            stacked[k] = sum_adamw(part, land, w_view[k], m_view[k], v_view[k], stacked[k], l,
                                   k in in_kernel, dev, "adamw_" + k)
        return tuple(stacked[k][0] for k, _ in keys)

    after = (small_sent, dh) + sent
    for entry in in_flight[:-1]:
        after = update_group(entry, after)
    out_g, out_d, out_m, out_v = {}, {}, {}, {}

    gathered_small, gathered_spatial = gather_wait(small_lands, small_sems[0], after + (packed_w, packed_m, packed_v),
                                                   "gather_wait_small")
    rows_of = lambda t: t.reshape(-1, LANES)
    spatial = sum_adamw_small(gathered_spatial, rows_of(a_w_s), rows_of(m_a_w_s), rows_of(v_a_w_s), "adamw_a_w_s")
    out_g["a_w_s"], out_d["a_w_s"], out_m["a_w_s"], out_v["a_w_s"] = [t.reshape(a_w_s.shape) for t in spatial]
    results = sum_adamw_small(gathered_small, packed_w, packed_m, packed_v, "adamw_small")
    loss = results[0].reshape(-1)[main] * (0.5 / D)
    off = 0
    for k in replicated:
        size, shape = weights[k].size, weights[k].shape
        out_g[k], out_d[k], out_m[k], out_v[k] = [t.reshape(-1)[off:off + size].reshape(shape) for t in results]
        off += size
    n_local = b_b_qkv.shape[1]
    bias_sum = sum_parts(gathered_small[:, main_rows:], "sum_bias_grads").reshape(-1)[:bias_size]
    out_g["b_b_qkv"] = lax.dynamic_slice_in_dim(
        bias_sum.reshape(b_b_qkv.shape[0], -1), device * n_local, n_local, axis=1)
    pad = lambda t: jnp.concatenate([t.reshape(-1), jnp.ones((8 * LANES - t.size,), F32)]).reshape(8, LANES)
    bias_update = adamw(pad(b_b_qkv), pad(out_g["b_b_qkv"]), pad(m_b_b_qkv), pad(v_b_b_qkv), "adamw_bias")
    out_d["b_b_qkv"], out_m["b_b_qkv"], out_v["b_b_qkv"] = [
        t.reshape(-1)[:b_b_qkv.size].reshape(b_b_qkv.shape) for t in bias_update]

    update_group(in_flight[-1], (spatial[0], results[0], bias_update[0]))
    for k in big:
        out_g[k], out_d[k], out_m[k], out_v[k] = [view(k, t) for t in stacked[k]]

    return (loss, grad_x, *[out_g[k] for k in names], *[out_d[k] for k in names],
            *[out_m[k] for k in names], *[out_v[k] for k in names])
```

```python
import math

import jax
import jax.numpy as jnp
from jax import lax
from jax.experimental import pallas as pl
from jax.experimental.pallas import tpu as pltpu

F32, BF16 = jnp.float32, jnp.bfloat16
MESH = pl.DeviceIdType.MESH
N_DEV = 8

CHUNK = 128
HEAD_DIM = 64
N_KV_HEADS = 4
ROPE_DIM = HEAD_DIM // 4
ROPE_THETA = 500000.0
RMS_EPS = 1e-6
LN_EPS = 1e-5
NEG_INF = -1e30

ADAM_LR = 0.001
ADAM_B1 = 0.9
ADAM_B2 = 0.999
ADAM_EPS = 1e-08
ADAM_WD = 0.01
ADAM_STEP = 10

V7X_VMEM_BYTES = 64 * 2 ** 20
VMEM_LIMIT = V7X_VMEM_BYTES - 8 * 2 ** 20
LANES = 128

ROW_TILE = 512
COL_TILE = 512
GRAD_TILE_ROWS = 1408
GRAD_TILE_DEPTH = 2048
ADAM_TILE = 256


def _params(*sem):
    return pltpu.CompilerParams(dimension_semantics=sem or None, vmem_limit_bytes=VMEM_LIMIT)


def _tile(n, pref):
    if n <= pref:
        return n
    t = pref - pref % LANES
    while t >= LANES:
        if n % t == 0:
            return t
        t -= LANES
    return n


def _full(shape):
    return pl.BlockSpec(shape, lambda *_: (0,) * len(shape))


def _resident(shape):
    return pl.BlockSpec(shape, lambda *_: (0,) * len(shape), pipeline_mode=pl.Buffered(1))


def _rows(tm, width):
    return pl.BlockSpec((tm, width), lambda i: (i, 0))


def _rms(x, g):
    r = lax.rsqrt(jnp.mean(x * x, axis=-1, keepdims=True) + RMS_EPS)
    return x * r * g


def _rms_bwd(x, g, dy):
    r = lax.rsqrt(jnp.mean(x * x, axis=-1, keepdims=True) + RMS_EPS)
    xhat = x * r
    dg = jnp.sum(dy * xhat, axis=0, keepdims=True)
    dxhat = dy * g
    dx = r * (dxhat - xhat * jnp.mean(dxhat * xhat, axis=-1, keepdims=True))
    return dx, dg


_INV_SQRT2 = 1.0 / math.sqrt(2.0)
_INV_SQRT2PI = 1.0 / math.sqrt(2.0 * math.pi)


def _gelu(x):
    return 0.5 * x * (1.0 + lax.erf(x * _INV_SQRT2))


def _gelu_and_grad(x):
    cdf = 0.5 * (1.0 + lax.erf(x * _INV_SQRT2))
    return x * cdf, cdf + x * jnp.exp(-0.5 * x * x) * _INV_SQRT2PI


def _sigmoid(x):
    return 0.5 * jnp.tanh(0.5 * x) + 0.5


def _position():
    return lax.axis_index("x"), lax.axis_index("y"), lax.axis_index("c")


def _index(p):
    return 4 * p[0] + 2 * p[1] + p[2]


_HBM = pl.BlockSpec(memory_space=pltpu.HBM)
_SEM = pl.BlockSpec(memory_space=pltpu.SEMAPHORE)
_ORDERED_BY_DATA = pltpu.SideEffectType.DATAFLOW_SIDE_EFFECTING


def _in_hbm(v):
    return pltpu.with_memory_space_constraint(v, pltpu.HBM)


def _peer(k):
    x, y, c = _position()
    return (x ^ ((k >> 2) & 1), y ^ ((k >> 1) & 1), c ^ (k & 1))


def _split_copies(srcs, dsts, send, recv):
    return [pltpu.make_async_remote_copy(
        src_ref=src(k), dst_ref=dst(k), send_sem=send.at[7 * a + k - 1], recv_sem=recv.at[7 * a + k - 1],
        device_id=_peer(k), device_id_type=MESH)
        for a, (src, dst) in enumerate(zip(srcs, dsts)) for k in range(1, N_DEV)]


def _start_call(groups, sent, copies_of, name, per_array=N_DEV - 1):
    flat = [v for grp in groups for v in grp]
    n, ng = len(flat), len(groups)

    def body(*refs):
        bufs, sems = refs[:n], refs[n:n + 2 * ng]
        q = 0
        for g, grp in enumerate(groups):
            for cp in copies_of(bufs[q:q + len(grp)], sems[2 * g], sems[2 * g + 1]):
                cp.start()
            q += len(grp)
        refs[-1][...] = jnp.zeros_like(refs[-1])

    sem_shapes = []
    for count in sent:
        sem_shapes += [pltpu.SemaphoreType.DMA((per_array * count,)) for _ in range(2)]
    res = pl.pallas_call(
        body, name=name,
        out_shape=sem_shapes + [pltpu.HBM(v.shape, v.dtype) for v in flat] + [jax.ShapeDtypeStruct((8, LANES), F32)],
        in_specs=[_HBM] * n,
        out_specs=[_SEM] * (2 * ng) + [_HBM] * n + [pl.BlockSpec(memory_space=pltpu.VMEM)],
        input_output_aliases={i: 2 * ng + i for i in range(n)},
        compiler_params=pltpu.CompilerParams(has_side_effects=_ORDERED_BY_DATA),
    )(*[_in_hbm(v) for v in flat])
    sems = [(res[2 * g], res[2 * g + 1]) for g in range(ng)]
    return sems, list(res[2 * ng:-1]), res[-1]


def _wait_call(bufs, sems, copies_of, after, name):
    n = len(bufs)

    def body(*refs):
        for cp in copies_of(refs[:n], refs[n], refs[n + 1]):
            cp.wait_send()
            cp.wait_recv()

    return list(pl.pallas_call(
        body, name=name,
        out_shape=[pltpu.HBM(v.shape, v.dtype) for v in bufs],
        in_specs=[_HBM] * n + [_SEM, _SEM] + [pl.BlockSpec(memory_space=pl.ANY)] * len(after),
        out_specs=[_HBM] * n,
        input_output_aliases={i: i for i in range(n)},
        compiler_params=pltpu.CompilerParams(has_side_effects=_ORDERED_BY_DATA),
    )(*bufs, sems[0], sems[1], *after))


def _gather_copies(lands, send, recv):
    me = _index(_position())
    mine = [lambda k, ref=ref: ref.at[pl.ds(me, 1)] for ref in lands]
    return _split_copies(mine, mine, send, recv)


def _gather_arrivals(lands, send, recv):
    me = _index(_position())
    mine = [lambda k, ref=ref: ref.at[pl.ds(me, 1)] for ref in lands]
    theirs = [lambda k, ref=ref: ref.at[pl.ds(_index(_peer(k)), 1)] for ref in lands]
    return _split_copies(mine, theirs, send, recv)


def _exchange_copies(bufs, send, recv):
    half = len(bufs) // 2
    srcs = [lambda k, ref=ref: ref.at[pl.ds(_index(_peer(k)), 1)] for ref in bufs[:half]]
    dsts = [lambda k, ref=ref: ref.at[pl.ds(k - 1, 1)] for ref in bufs[half:]]
    return _split_copies(srcs, dsts, send, recv)


def gather_start(groups, name):
    return _start_call(groups, [len(grp) for grp in groups], _gather_copies, name)


def gather_wait(lands, sems, after, name):
    return _wait_call(lands, sems, _gather_arrivals, after, name)


_DIRECT = (1, 2, 4, 6)
_PASSED = (3, 5, 7)


def _remote(src, dst, send, recv, q, k):
    return pltpu.make_async_remote_copy(src_ref=src, dst_ref=dst, send_sem=send.at[q], recv_sem=recv.at[q],
                                        device_id=_peer(k), device_id_type=MESH)


def _slot_of(land, k):
    return land.at[pl.ds(_index(_peer(k)), 1)]


def relay_start(groups, name):
    def copies(lands, send, recv):
        return [_remote(_slot_of(land, 0), _slot_of(land, 0), send, recv, len(_DIRECT) * a + j, k)
                for a, land in enumerate(lands) for j, k in enumerate(_DIRECT)]
    return _start_call(groups, [len(grp) for grp in groups], copies, name, per_array=len(_DIRECT))


def relay_pass(lands, sems, after, name):
    n = len(lands)

    def body(*refs):
        bufs, recv_first = refs[:n], refs[n]
        send, recv = refs[n + 1 + len(after):n + 3 + len(after)]
        for a, land in enumerate(bufs):
            for j, k in enumerate(_PASSED):
                came = _slot_of(land, k ^ 1)
                _remote(came, came, recv_first, recv_first, len(_DIRECT) * a + _DIRECT.index(k ^ 1), k ^ 1).wait_recv()
                _remote(came, came, send, recv, len(_PASSED) * a + j, 1).start()

    res = pl.pallas_call(
        body, name=name,
        out_shape=[pltpu.SemaphoreType.DMA((len(_PASSED) * n,))] * 2 + [pltpu.HBM(v.shape, v.dtype) for v in lands],
        in_specs=[_HBM] * n + [_SEM] + [pl.BlockSpec(memory_space=pl.ANY)] * len(after),
        out_specs=[_SEM, _SEM] + [_HBM] * n,
        input_output_aliases={i: 2 + i for i in range(n)},
        compiler_params=pltpu.CompilerParams(has_side_effects=_ORDERED_BY_DATA),
    )(*lands, sems[1], *after)
    return (res[0], res[1]), list(res[2:])


def relay_wait(lands, first, second, after, name):
    n = len(lands)

    def body(*refs):
        bufs = refs[:n]
        send_first, recv_first, send, recv = refs[n:n + 4]
        for a, land in enumerate(bufs):
            mine = _slot_of(land, 0)
            _remote(mine, _slot_of(land, 1), send_first, recv_first, len(_DIRECT) * a, 1).wait_recv()
            for j, k in enumerate(_DIRECT):
                _remote(mine, mine, send_first, recv_first, len(_DIRECT) * a + j, k).wait_send()
            for j, k in enumerate(_PASSED):
                cp = _remote(_slot_of(land, k ^ 1), _slot_of(land, k), send, recv, len(_PASSED) * a + j, 1)
                cp.wait_send()
                cp.wait_recv()

    return list(pl.pallas_call(
        body, name=name,
        out_shape=[pltpu.HBM(v.shape, v.dtype) for v in lands],
        in_specs=[_HBM] * n + [_SEM] * 4 + [pl.BlockSpec(memory_space=pl.ANY)] * len(after),
        out_specs=[_HBM] * n,
        input_output_aliases={i: i for i in range(n)},
        compiler_params=pltpu.CompilerParams(has_side_effects=_ORDERED_BY_DATA),
    )(*lands, first[0], first[1], second[0], second[1], *after))


def exchange_start(partials, name):
    lands = [lax.empty((N_DEV - 1,) + p.shape[1:], p.dtype) for p in partials]
    sems, bufs, token = _start_call([list(partials) + lands], [len(partials)], _exchange_copies, name)
    return sems[0], bufs, token


def exchange_wait(bufs, sems, after, name):
    bufs = _wait_call(bufs, sems, _exchange_copies, after, name)
    return bufs[:len(bufs) // 2], bufs[len(bufs) // 2:]


def prep_weights(items, dev, name, dtype=BF16, after=()):
    n = len(items)
    blocks = [(1, w.shape[2], w.shape[1]) if t else (1,) + w.shape[1:] for w, _, t in items]

    def body(d_ref, *refs):
        for (_, _, transpose), w_ref, o_ref in zip(items, refs[:n], refs[-n:]):
            v = w_ref[0]
            o_ref[0] = (v.T if transpose else v).astype(dtype)

    return pl.pallas_call(
        body, name=name,
        grid_spec=pltpu.PrefetchScalarGridSpec(
            num_scalar_prefetch=1, grid=(1,),
            in_specs=[pl.BlockSpec((1,) + w.shape[1:], lambda i, d, layer=layer: (layer, 0, 0)) for w, layer, _ in items]
            + [pl.BlockSpec(memory_space=pl.ANY)] * len(after),
            out_specs=[pl.BlockSpec(blk, lambda i, d: (d[0], 0, 0)) for blk in blocks]),
        out_shape=[jax.ShapeDtypeStruct((N_DEV,) + blk[1:], dtype) for blk in blocks],
        compiler_params=_params("arbitrary"),
    )(dev, *[w for w, _, _ in items], *after)


def _adamw_math(w, g, m, v):
    mn = ADAM_B1 * m + (1.0 - ADAM_B1) * g
    vn = ADAM_B2 * v + (1.0 - ADAM_B2) * (g * g)
    m_hat = mn * (1.0 / (1.0 - ADAM_B1 ** ADAM_STEP))
    v_hat = vn * (1.0 / (1.0 - ADAM_B2 ** ADAM_STEP))
    return -ADAM_LR * (m_hat / (jnp.sqrt(v_hat) + ADAM_EPS) + ADAM_WD * w), mn, vn


def sum_adamw(partial, landed, w, m, v, prev, layer, transpose, dev, name):
    _, r, c = partial.shape
    if transpose:
        tc = _tile(c, ADAM_TILE)
        grid = (c // tc,)
        part_spec = pl.BlockSpec((1, r, tc), lambda i, d: (d[0], 0, i))
        land_spec = pl.BlockSpec((N_DEV - 1, r, tc), lambda i, d: (0, 0, i))
        w_spec = pl.BlockSpec((1, tc, r), lambda i, d: (layer, i, 0))
    else:
        tr = r // 2 if r % 32 == 0 else r
        grid = (r // tr,)
        part_spec = pl.BlockSpec((1, tr, c), lambda i, d: (d[0], i, 0))
        land_spec = pl.BlockSpec((N_DEV - 1, tr, c), lambda i, d: (0, i, 0))
        w_spec = pl.BlockSpec((1, tr, c), lambda i, d: (layer, i, 0))

    def body(d_ref, p_ref, l_ref, w_ref, m_ref, v_ref, *rest):
        g_ref, dl_ref, mo_ref, vo_ref = rest[-4:]
        acc = p_ref[0].astype(F32)
        for k in range(N_DEV - 1):
            acc = acc + l_ref[k].astype(F32)
        g = acc.T if transpose else acc
        delta, mn, vn = _adamw_math(w_ref[0], g, m_ref[0], v_ref[0])
        g_ref[0], dl_ref[0], mo_ref[0], vo_ref[0] = g, delta, mn, vn

    return pl.pallas_call(
        body, name=name,
        grid_spec=pltpu.PrefetchScalarGridSpec(
            num_scalar_prefetch=1, grid=grid,
            in_specs=[part_spec, land_spec, w_spec, w_spec, w_spec] + [pl.BlockSpec(memory_space=pl.ANY)] * 4,
            out_specs=[w_spec] * 4),
        out_shape=[jax.ShapeDtypeStruct(w.shape, F32)] * 4,
        input_output_aliases={6 + q: q for q in range(4)},
        compiler_params=_params("parallel"),
    )(dev, partial, landed, w, m, v, *prev)


def sum_parts(parts, name):
    P, r, c = parts.shape

    def body(p_ref, o_ref):
        acc = p_ref[0].astype(F32)
        for s in range(1, P):
            acc = acc + p_ref[s].astype(F32)
        o_ref[...] = acc

    return pl.pallas_call(
        body, name=name, in_specs=[_full((P, r, c))], out_specs=_full((r, c)), grid=(1,),
        out_shape=jax.ShapeDtypeStruct((r, c), F32), compiler_params=_params("arbitrary"))(parts)


def adamw(w, g, m, v, name):
    R, C = w.shape
    tr = _tile(R, ROW_TILE)

    def body(w_ref, g_ref, m_ref, v_ref, d_ref, mo_ref, vo_ref):
        d_ref[...], mo_ref[...], vo_ref[...] = _adamw_math(w_ref[...], g_ref[...], m_ref[...], v_ref[...])

    spec = pl.BlockSpec((tr, C), lambda i: (i, 0))
    return pl.pallas_call(
        body, name=name, grid=(R // tr,),
        in_specs=[spec] * 4, out_specs=[spec] * 3,
        out_shape=[jax.ShapeDtypeStruct((R, C), F32)] * 3,
        compiler_params=_params("parallel"),
    )(w, g, m, v)


def sum_adamw_small(gathered, w, m, v, name):
    rows = w.shape[0]
    tr = _tile(rows, ROW_TILE)

    def body(p_ref, w_ref, m_ref, v_ref, g_ref, d_ref, mo_ref, vo_ref):
        g = p_ref[0]
        for s in range(1, N_DEV):
            g = g + p_ref[s]
        g_ref[...] = g
        d_ref[...], mo_ref[...], vo_ref[...] = _adamw_math(w_ref[...], g, m_ref[...], v_ref[...])

    spec = pl.BlockSpec((tr, LANES), lambda i: (i, 0))
    return pl.pallas_call(
        body, name=name, grid=(rows // tr,),
        in_specs=[pl.BlockSpec((N_DEV, tr, LANES), lambda i: (0, i, 0)), spec, spec, spec], out_specs=[spec] * 4,
        out_shape=[jax.ShapeDtypeStruct((rows, LANES), F32)] * 4, compiler_params=_params("parallel"))(gathered, w, m, v)


def first_norm(h, g, name):
    T, D = h.shape
    tm = _tile(T, ROW_TILE)

    def body(h_ref, g_ref, hn_ref):
        hn_ref[...] = _rms(h_ref[...], g_ref[...]).astype(BF16)

    row = _rows(tm, D)
    return pl.pallas_call(
        body, name=name, grid=(T // tm,), in_specs=[row, _full((1, D))], out_specs=row,
        out_shape=jax.ShapeDtypeStruct((T, D), BF16), compiler_params=_params("parallel"))(h, g)


_NT = (((1,), (1,)), ((), ()))
_TN = (((0,), (0,)), ((), ()))


def rowmm(a, w, transposed_w, bias, out_dtype, name, after=None):
    M, K = a.shape
    N = w.shape[0] if transposed_w else w.shape[1]
    tm = _tile(M, ROW_TILE)
    tn = _tile(N, COL_TILE)

    def body(*refs):
        a_ref, w_ref = refs[:2]
        o_ref = refs[-1]
        av = a_ref[...]
        for j in range(N // tn):
            cols = pl.ds(j * tn, tn)
            if transposed_w:
                acc = lax.dot_general(av, w_ref[cols, :], _NT, preferred_element_type=F32)
            else:
                acc = jnp.dot(av, w_ref[:, cols], preferred_element_type=F32)
            if bias is not None:
                acc = acc + refs[2][:, cols]
            o_ref[:, cols] = acc.astype(out_dtype)

    ins, in_specs = [a, w], [_rows(tm, K), _resident(w.shape)]
    if bias is not None:
        ins.append(bias)
        in_specs.append(_full((1, N)))
    if after is not None:
        ins.append(after)
        in_specs.append(pl.BlockSpec(memory_space=pl.ANY))
    return pl.pallas_call(
        body, name=name, grid=(M // tm,), in_specs=in_specs, out_specs=_rows(tm, N),
        out_shape=jax.ShapeDtypeStruct((M, N), out_dtype), compiler_params=_params("parallel"))(*ins)


def _rows_times_weight(a_ref, w_ref, acc_ref):
    N = w_ref.shape[1]
    tn = _tile(N, COL_TILE)
    av = a_ref[...]
    for j in range(N // tn):
        cols = pl.ds(j * tn, tn)
        acc_ref[:, cols] = jnp.dot(av, w_ref[:, cols], preferred_element_type=F32)


def mm_add_norm(a, w, h, g_post, g_pre, name):
    T, K = a.shape
    D = w.shape[1]
    tm = _tile(T, ROW_TILE)

    def body(a_ref, w_ref, h_ref, gp_ref, g_ref, y_ref, ho_ref, hn_ref, acc_ref):
        _rows_times_weight(a_ref, w_ref, acc_ref)
        y = acc_ref[...]
        y_ref[...] = y.astype(BF16)
        hv = h_ref[...] + _rms(y, gp_ref[...])
        ho_ref[...] = hv
        hn_ref[...] = _rms(hv, g_ref[...]).astype(BF16)

    row, vec = _rows(tm, D), _full((1, D))
    return pl.pallas_call(
        body, name=name, grid=(T // tm,), in_specs=[_rows(tm, K), _resident(w.shape), row, vec, vec],
        out_specs=[row, row, row],
        out_shape=[jax.ShapeDtypeStruct((T, D), BF16), jax.ShapeDtypeStruct((T, D), F32),
                   jax.ShapeDtypeStruct((T, D), BF16)],
        scratch_shapes=[pltpu.VMEM((tm, D), F32)], compiler_params=_params("parallel"))(a, w, h, g_post, g_pre)


def mm_add_norm_loss(a, w, h, g_post, target, name):
    T, K = a.shape
    D = w.shape[1]
    tm = _tile(T, ROW_TILE)

    def body(a_ref, w_ref, h_ref, gp_ref, t_ref, dh_ref, dy_ref, loss_ref, dgp_ref, acc_ref):
        @pl.when(pl.program_id(0) == 0)
        def _():
            loss_ref[...] = jnp.zeros_like(loss_ref)
            dgp_ref[...] = jnp.zeros_like(dgp_ref)
        _rows_times_weight(a_ref, w_ref, acc_ref)
        y = acc_ref[...]
        err = h_ref[...] + _rms(y, gp_ref[...]) - t_ref[...]
        dh = err * (1.0 / D)
        dh_ref[...] = dh
        loss_ref[...] += jnp.sum(err * err)
        dy, dgp = _rms_bwd(y, gp_ref[...], dh)
        dy_ref[...] = dy.astype(BF16)
        dgp_ref[...] += dgp

    row, vec = _rows(tm, D), _full((1, D))
    return pl.pallas_call(
        body, name=name, grid=(T // tm,), in_specs=[_rows(tm, K), _resident(w.shape), row, vec, row],
        out_specs=[row, row, _full((8, LANES)), vec],
        out_shape=[jax.ShapeDtypeStruct((T, D), F32), jax.ShapeDtypeStruct((T, D), BF16),
                   jax.ShapeDtypeStruct((8, LANES), F32), jax.ShapeDtypeStruct((1, D), F32)],
        scratch_shapes=[pltpu.VMEM((tm, D), F32)], compiler_params=_params("arbitrary"))(a, w, h, g_post, target)


def mm_norm_bwd(a, w, dh_in, x, g_pre, post, name, after, project=None):
    T, K = a.shape
    D = w.shape[1]
    tm = _tile(T, ROW_TILE)
    row, vec = _rows(tm, D), _full((1, D))
    ins, in_specs = [a, w, dh_in, x, g_pre], [_rows(tm, K), _resident(w.shape), row, row, vec]
    outs = [jax.ShapeDtypeStruct((T, D), F32), jax.ShapeDtypeStruct((1, D), F32)]
    out_specs, names = [row, vec], ["dh", "dg_pre"]
    if post is not None:
        ins += list(post)
        in_specs += [row, vec]
        outs += [jax.ShapeDtypeStruct((T, D), BF16), jax.ShapeDtypeStruct((1, D), F32)]
        out_specs += [row, vec]
        names += ["dy", "dg_post"]
    if project is not None:
        ins.append(project)
        in_specs.append(_resident(project.shape))
        outs.append(jax.ShapeDtypeStruct((T, project.shape[0]), BF16))
        out_specs.append(_rows(tm, project.shape[0]))
        names.append("dy_projected")
    n_in = len(ins) + 1

    def body(*refs):
        a_ref, w_ref, dh_ref_in, x_ref, g_ref = refs[:5]
        out_refs, acc_ref = refs[n_in:-1], refs[-1]
        first = pl.program_id(0) == 0
        _rows_times_weight(a_ref, w_ref, acc_ref)
        dx, dg = _rms_bwd(x_ref[...], g_ref[...], acc_ref[...])
        dh = dh_ref_in[...] + dx
        out_refs[0][...] = dh

        @pl.when(first)
        def _():
            for r in out_refs[1::2]:
                r[...] = jnp.zeros_like(r)
        out_refs[1][...] += dg
        if post is not None:
            y_ref, gp_ref = refs[5:7]
            dy, dgp = _rms_bwd(y_ref[...].astype(F32), gp_ref[...], dh)
            dy = dy.astype(BF16)
            out_refs[2][...] = dy
            out_refs[3][...] += dgp
            if project is not None:
                p_ref, Q = refs[7], project.shape[0]
                tq = _tile(Q, COL_TILE)
                for j in range(Q // tq):
                    cols = pl.ds(j * tq, tq)
                    out_refs[4][:, cols] = lax.dot_general(
                        dy, p_ref[cols, :], _NT, preferred_element_type=F32).astype(BF16)

    res = pl.pallas_call(
        body, name=name, grid=(T // tm,), in_specs=in_specs + [pl.BlockSpec(memory_space=pl.ANY)],
        out_specs=out_specs, out_shape=outs, scratch_shapes=[pltpu.VMEM((tm, D), F32)],
        compiler_params=_params("arbitrary"))(*ins, after)
    return dict(zip(names, res))


def grad_mm(a, b, name, after=()):
    T, N = a.shape
    K = b.shape[1]
    tn = _tile(N, GRAD_TILE_ROWS)
    tt = _tile(T, GRAD_TILE_DEPTH)

    def body(a_ref, b_ref, *rest):
        o_ref, acc_ref = rest[-2:]
        t = pl.program_id(1)

        @pl.when(t == 0)
        def _():
            acc_ref[...] = jnp.zeros_like(acc_ref)
        acc_ref[...] += lax.dot_general(a_ref[...], b_ref[...], _TN, preferred_element_type=F32)

        @pl.when(t == pl.num_programs(1) - 1)
        def _():
            o_ref[...] = acc_ref[...].astype(BF16)

    return pl.pallas_call(
        body, name=name, grid=(N // tn, T // tt),
        in_specs=[pl.BlockSpec((tt, tn), lambda j, t: (t, j)), pl.BlockSpec((tt, K), lambda j, t: (t, 0))]
        + [pl.BlockSpec(memory_space=pl.ANY)] * len(after),
        out_specs=pl.BlockSpec((tn, K), lambda j, t: (j, 0)),
        out_shape=jax.ShapeDtypeStruct((N, K), BF16),
        scratch_shapes=[pltpu.VMEM((tn, K), F32)],
        compiler_params=_params("parallel", "arbitrary"))(a, b, *after)


def ffn_up(fn, w_gu_t, name):
    T, D = fn.shape
    F = w_gu_t.shape[0] // 2
    tm = _tile(T, ROW_TILE)
    tn = _tile(F, COL_TILE)

    def body(a_ref, w_ref, act_ref, part_ref):
        av = a_ref[...]
        for j in range(F // tn):
            g = lax.dot_general(av, w_ref[pl.ds(j * tn, tn), :], _NT, preferred_element_type=F32)
            up = lax.dot_general(av, w_ref[pl.ds(F + j * tn, tn), :], _NT, preferred_element_type=F32)
            sg = _sigmoid(g)
            silu = g * sg
            act_ref[:, pl.ds(j * tn, tn)] = (silu * up).astype(BF16)
            part_ref[:, pl.ds(j * tn, tn)] = (up * (sg + silu * (1.0 - sg))).astype(BF16)
            part_ref[:, pl.ds(F + j * tn, tn)] = silu.astype(BF16)

    return pl.pallas_call(
        body, name=name, grid=(T // tm,), in_specs=[_rows(tm, D), _resident(w_gu_t.shape)],
        out_specs=[_rows(tm, F), _rows(tm, 2 * F)],
        out_shape=[jax.ShapeDtypeStruct((T, F), BF16), jax.ShapeDtypeStruct((T, 2 * F), BF16)],
        compiler_params=_params("parallel"))(fn, w_gu_t)


def ffn_dact(df, w_down, partials, name, after=()):
    T, D = df.shape
    F = w_down.shape[0]
    tm = _tile(T, ROW_TILE)
    tn = _tile(F, COL_TILE)

    def body(d_ref, w_ref, part_ref, *rest):
        o_ref = rest[-1]
        dv = d_ref[...]
        for j in range(F // tn):
            dact = lax.dot_general(dv, w_ref[pl.ds(j * tn, tn), :], _NT, preferred_element_type=F32)
            for cols in (pl.ds(j * tn, tn), pl.ds(F + j * tn, tn)):
                o_ref[:, cols] = (dact * part_ref[:, cols].astype(F32)).astype(BF16)

    return pl.pallas_call(
        body, name=name, grid=(T // tm,),
        in_specs=[_rows(tm, D), _resident(w_down.shape), _rows(tm, 2 * F)] + [pl.BlockSpec(memory_space=pl.ANY)] * len(after),
        out_specs=_rows(tm, 2 * F), out_shape=jax.ShapeDtypeStruct((T, 2 * F), BF16),
        compiler_params=_params("parallel"))(df, w_down, partials, *after)


def _layernorm_stats(v):
    mu = jnp.mean(v, axis=-1, keepdims=True)
    cen = v - mu
    rstd = lax.rsqrt(jnp.mean(cen * cen, axis=-1, keepdims=True) + LN_EPS)
    return cen * rstd, rstd


GMLP_ROWS = 2 * CHUNK


def gmlp_gate_out(a, ln_g, ln_b, w_mask, b_s_t, w_out, h, g_post, g_pre, name):
    T, W2 = a.shape
    W = W2 // 2
    G = w_mask.shape[0]
    C = W // G
    D = w_out.shape[1]
    tm = min(T, GMLP_ROWS)

    def body(a_ref, lg_ref, lb_ref, w_ref, bs_ref, wo_ref, h_ref, gp_ref, g_ref,
             gated_ref, y_ref, ho_ref, hn_ref, acc_ref):
        for c in range(tm // CHUNK):
            rows = pl.ds(c * CHUNK, CHUNK)
            xhat, _ = _layernorm_stats(_gelu(a_ref[rows, W:].astype(F32)))
            vln = (xhat * lg_ref[...] + lb_ref[...]).astype(BF16)
            for g in range(G):
                cols = pl.ds(g * C, C)
                sv = jnp.dot(w_ref[g], vln[:, g * C:(g + 1) * C], preferred_element_type=F32) + bs_ref[:, g:g + 1]
                gated_ref[rows, cols] = (_gelu(a_ref[rows, cols].astype(F32)) * sv).astype(BF16)
        _rows_times_weight(gated_ref, wo_ref, acc_ref)
        y = acc_ref[...]
        y_ref[...] = y.astype(BF16)
        hv = h_ref[...] + _rms(y, gp_ref[...])
        ho_ref[...] = hv
        hn_ref[...] = _rms(hv, g_ref[...]).astype(BF16)

    row, vec = _rows(tm, D), _full((1, D))
    return pl.pallas_call(
        body, name=name, grid=(T // tm,),
        in_specs=[_rows(tm, W2), _full((1, W)), _full((1, W)), _full(w_mask.shape), _full(b_s_t.shape),
                  _resident(w_out.shape), row, vec, vec],
        out_specs=[_rows(tm, W), row, row, row],
        out_shape=[jax.ShapeDtypeStruct((T, W), BF16), jax.ShapeDtypeStruct((T, D), BF16),
                   jax.ShapeDtypeStruct((T, D), F32), jax.ShapeDtypeStruct((T, D), BF16)],
        scratch_shapes=[pltpu.VMEM((tm, D), F32)],
        compiler_params=_params("parallel"))(a, ln_g, ln_b, w_mask, b_s_t, w_out, h, g_post, g_pre)


def gmlp_gate_bwd_dx(a, dmix, w_out, ln_g, ln_b, w_mask, w_mask_t, b_s_t, group_onehot, w_in_t,
                     dh_in, x, g_pre, post, name, after):
    T, W2 = a.shape
    W = W2 // 2
    G = w_mask.shape[0]
    C = W // G
    D = w_out.shape[1]
    tm = min(T, GMLP_ROWS)
    tn = _tile(W, COL_TILE)
    row, vec = _rows(tm, D), _full((1, D))
    ins = [a, dmix, w_out, ln_g, ln_b, w_mask, w_mask_t, b_s_t, group_onehot, w_in_t, dh_in, x, g_pre]
    in_specs = [_rows(tm, W2), row, _resident(w_out.shape), _full((1, W)), _full((1, W)), _full(w_mask.shape),
                _full(w_mask_t.shape), _full(b_s_t.shape), _full(group_onehot.shape), _resident(w_in_t.shape), row, row, vec]
    names = ["da", "dw_s", "db_s_t", "dln_g", "dln_b", "db_in", "dh", "dg_pre"]
    outs = [jax.ShapeDtypeStruct((T, W2), BF16), jax.ShapeDtypeStruct((G, CHUNK, CHUNK), F32),
            jax.ShapeDtypeStruct((CHUNK, LANES), F32), jax.ShapeDtypeStruct((1, W), F32),
            jax.ShapeDtypeStruct((1, W), F32), jax.ShapeDtypeStruct((1, W2), F32),
            jax.ShapeDtypeStruct((T, D), F32), jax.ShapeDtypeStruct((1, D), F32)]
    out_specs = [_rows(tm, W2), _full((G, CHUNK, CHUNK)), _full((CHUNK, LANES)), _full((1, W)), _full((1, W)),
                 _full((1, W2)), row, vec]
    if post is not None:
        ins += list(post)
        in_specs += [row, vec]
        names += ["dy", "dg_post"]
        outs += [jax.ShapeDtypeStruct((T, D), BF16), jax.ShapeDtypeStruct((1, D), F32)]
        out_specs += [row, vec]
    n_in = len(ins) + 1

    def body(*refs):
        (a_ref, dm_ref, wo_ref, lg_ref, lb_ref, w_ref, wt_ref, bs_ref, e_ref, wi_ref, dh_in_ref, x_ref, gpre_ref) = refs[:13]
        out_refs = refs[n_in:-3]
        da_ref, dws_ref, dbs_ref, dlg_ref, dlb_ref, dbin_ref, dh_ref, dgpre_ref = out_refs[:8]
        dgated_ref, dvln_ref, acc_ref = refs[-3:]

        @pl.when(pl.program_id(0) == 0)
        def _():
            for r in out_refs[1:6] + out_refs[7::2]:
                r[...] = jnp.zeros_like(r)
        dm = dm_ref[...]
        for j in range(W // tn):
            cols = pl.ds(j * tn, tn)
            dgated_ref[:, cols] = lax.dot_general(dm, wo_ref[cols, :], _NT, preferred_element_type=F32)
        causal = (lax.broadcasted_iota(jnp.int32, (CHUNK, CHUNK), 1)
                  <= lax.broadcasted_iota(jnp.int32, (CHUNK, CHUNK), 0))
        for c in range(tm // CHUNK):
            rows = pl.ds(c * CHUNK, CHUNK)
            gelu_v, gelu_grad_v = _gelu_and_grad(a_ref[rows, W:].astype(F32))
            xhat, rstd = _layernorm_stats(gelu_v)
            vln = (xhat * lg_ref[...] + lb_ref[...]).astype(BF16)
            for g in range(G):
                cols = pl.ds(g * C, C)
                vg = vln[:, g * C:(g + 1) * C]
                gelu_u, gelu_grad_u = _gelu_and_grad(a_ref[rows, cols].astype(F32))
                dgated = dgated_ref[rows, cols]
                dsv = (dgated * gelu_u).astype(BF16)
                dbs_ref[...] += jnp.dot(dsv, e_ref[cols, :], preferred_element_type=F32)
                sv = jnp.dot(w_ref[g], vg, preferred_element_type=F32) + bs_ref[:, g:g + 1]
                dau = dgated * sv * gelu_grad_u
                da_ref[rows, cols] = dau.astype(BF16)
                dbin_ref[:, cols] += jnp.sum(dau, axis=0, keepdims=True)
                dws_ref[g] += jnp.where(causal, lax.dot_general(dsv, vg, _NT, preferred_element_type=F32), 0.0)
                dvln_ref[:, cols] = jnp.dot(wt_ref[g], dsv, preferred_element_type=F32)
            dvln = dvln_ref[...]
            dlg_ref[...] += jnp.sum(dvln * xhat, axis=0, keepdims=True)
            dlb_ref[...] += jnp.sum(dvln, axis=0, keepdims=True)
            dxhat = dvln * lg_ref[...]
            dv = rstd * (dxhat - jnp.mean(dxhat, axis=-1, keepdims=True)
                         - xhat * jnp.mean(dxhat * xhat, axis=-1, keepdims=True))
            dav = dv * gelu_grad_v
            da_ref[rows, W:] = dav.astype(BF16)
            dbin_ref[:, W:] += jnp.sum(dav, axis=0, keepdims=True)
        _rows_times_weight(da_ref, wi_ref, acc_ref)
        dx, dg = _rms_bwd(x_ref[...], gpre_ref[...], acc_ref[...])
        dh = dh_in_ref[...] + dx
        dh_ref[...] = dh
        dgpre_ref[...] += dg
        if post is not None:
            y_ref, gp_ref = refs[13:15]
            dy, dgp = _rms_bwd(y_ref[...].astype(F32), gp_ref[...], dh)
            out_refs[8][...] = dy.astype(BF16)
            out_refs[9][...] += dgp

    res = pl.pallas_call(
        body, name=name, grid=(T // tm,), in_specs=in_specs + [pl.BlockSpec(memory_space=pl.ANY)],
        out_specs=out_specs, out_shape=outs,
        scratch_shapes=[pltpu.VMEM((tm, W), F32), pltpu.VMEM((CHUNK, W), F32), pltpu.VMEM((tm, D), F32)],
        compiler_params=_params("arbitrary"))(*ins, after)
    return dict(zip(names, res))


def rope_tables(pos, inv_freq_row, name):
    T = pos.shape[0]
    tm = _tile(T, ROW_TILE)

    def body(p_ref, f_ref, c_ref, s1_ref, s2_ref):
        ang = p_ref[...].astype(F32) * f_ref[...]
        lane = lax.broadcasted_iota(jnp.int32, (tm, LANES), 1) % HEAD_DIM
        sin = jnp.sin(ang)
        c_ref[...] = jnp.cos(ang)
        s1_ref[...] = jnp.where(lane < _HALF, -sin, 0.0)
        s2_ref[...] = jnp.where((lane >= _HALF) & (lane < ROPE_DIM), sin, 0.0)

    tab = _rows(tm, LANES)
    return pl.pallas_call(
        body, name=name, grid=(T // tm,), in_specs=[_rows(tm, 1), _full((1, LANES))], out_specs=[tab] * 3,
        out_shape=[jax.ShapeDtypeStruct((T, LANES), F32)] * 3, compiler_params=_params("parallel"))(pos, inv_freq_row)


_HALF = ROPE_DIM // 2


def _slabs(x):
    return [x[:, b * LANES:(b + 1) * LANES] for b in range(x.shape[1] // LANES)]


def _rotate(x, c, s1, s2):
    return [xs * c + pltpu.roll(xs, LANES - _HALF, 1) * s1 + pltpu.roll(xs, _HALF, 1) * s2 for xs in _slabs(x)]


def _rotate_transposed(dy, c, s1, s2):
    return [ds * c + pltpu.roll(ds * s1, _HALF, 1) + pltpu.roll(ds * s2, LANES - _HALF, 1) for ds in _slabs(dy)]


def qkv_rope(hn, w_t, bias, tabs, q_width, kv_width, name):
    T, D = hn.shape
    N = w_t.shape[0]
    QK = q_width + kv_width
    tm = _tile(T, ROW_TILE)
    tn = 2 * LANES
    scale = HEAD_DIM ** -0.5

    def body(a_ref, w_ref, b_ref, c_ref, s1_ref, s2_ref, qkv_ref, qk_ref):
        av = a_ref[...]
        tabs_v = (c_ref[...], s1_ref[...], s2_ref[...])
        for j in range(N // tn):
            cols = pl.ds(j * tn, tn)
            acc = lax.dot_general(av, w_ref[cols, :], _NT, preferred_element_type=F32) + b_ref[:, cols]
            qkv_ref[:, cols] = acc.astype(BF16)
            if j * tn < QK:
                for b, y in enumerate(_rotate(acc, *tabs_v)):
                    if j * tn < q_width:
                        y = y * scale
                    qk_ref[:, pl.ds(j * tn + b * LANES, LANES)] = y.astype(BF16)

    tab = _rows(tm, LANES)
    return pl.pallas_call(
        body, name=name, grid=(T // tm,), in_specs=[_rows(tm, D), _resident(w_t.shape), _full((1, N)), tab, tab, tab],
        out_specs=[_rows(tm, N), _rows(tm, QK)],
        out_shape=[jax.ShapeDtypeStruct((T, N), BF16), jax.ShapeDtypeStruct((T, QK), BF16)],
        compiler_params=_params("parallel"))(hn, w_t, bias, *tabs)


def rope_bwd_dx(dq, dk, dv, tabs, w, dh_in, x, g_pre, post, name, after):
    T, Q = dq.shape
    KV = dk.shape[1]
    N = Q + 2 * KV
    D = w.shape[1]
    tm = _tile(T, ROW_TILE)
    scale = HEAD_DIM ** -0.5
    row, vec, tab = _rows(tm, D), _full((1, D)), _rows(tm, LANES)
    ins = [dq, dk, dv, *tabs, w, dh_in, x, g_pre]
    in_specs = [_rows(tm, Q), _rows(tm, KV), _rows(tm, KV), tab, tab, tab, _resident(w.shape), row, row, vec]
    names = ["dqkv", "dbias", "dh", "dg_pre"]
    outs = [jax.ShapeDtypeStruct((T, N), BF16), jax.ShapeDtypeStruct((1, N), F32),
            jax.ShapeDtypeStruct((T, D), F32), jax.ShapeDtypeStruct((1, D), F32)]
    out_specs = [_rows(tm, N), _full((1, N)), row, vec]
    if post is not None:
        ins += list(post)
        in_specs += [row, vec]
        names += ["dy", "dg_post"]
        outs += [jax.ShapeDtypeStruct((T, D), BF16), jax.ShapeDtypeStruct((1, D), F32)]
        out_specs += [row, vec]
    n_in = len(ins) + 1

    def body(*refs):
        dq_ref, dk_ref, dv_ref, c_ref, s1_ref, s2_ref, w_ref, dh_in_ref, x_ref, g_ref = refs[:10]
        out_refs, acc_ref = refs[n_in:-1], refs[-1]
        dqkv_ref, db_ref, dh_ref, dg_ref = out_refs[:4]

        @pl.when(pl.program_id(0) == 0)
        def _():
            for r in out_refs[1::2]:
                r[...] = jnp.zeros_like(r)
        tabs_v = (c_ref[...], s1_ref[...], s2_ref[...])
        pieces = [s * scale for s in _rotate_transposed(dq_ref[...], *tabs_v)]
        pieces += _rotate_transposed(dk_ref[...], *tabs_v)
        pieces += _slabs(dv_ref[...])
        for b, y in enumerate(pieces):
            cols = pl.ds(b * LANES, LANES)
            dqkv_ref[:, cols] = y.astype(BF16)
            db_ref[:, cols] += jnp.sum(y, axis=0, keepdims=True)
        _rows_times_weight(dqkv_ref, w_ref, acc_ref)
        dx, dg = _rms_bwd(x_ref[...], g_ref[...], acc_ref[...])
        dh = dh_in_ref[...] + dx
        dh_ref[...] = dh
        dg_ref[...] += dg
        if post is not None:
            y_ref, gp_ref = refs[10:12]
            dy, dgp = _rms_bwd(y_ref[...].astype(F32), gp_ref[...], dh)
            out_refs[4][...] = dy.astype(BF16)
            out_refs[5][...] += dgp

    res = pl.pallas_call(
        body, name=name, grid=(T // tm,), in_specs=in_specs + [pl.BlockSpec(memory_space=pl.ANY)],
        out_specs=out_specs, out_shape=outs, scratch_shapes=[pltpu.VMEM((tm, D), F32)],
        compiler_params=_params("arbitrary"))(*ins, after)
    return dict(zip(names, res))


def _band_mask(n, heads=1):
    qi = lax.broadcasted_iota(jnp.int32, (heads * CHUNK, 2 * CHUNK), 0) % CHUNK
    sj = lax.broadcasted_iota(jnp.int32, (heads * CHUNK, 2 * CHUNK), 1)
    return (sj > qi) & (sj <= qi + CHUNK) & ((n > 0) | (sj >= CHUNK))


def _kv_head(kc_ref, kp_ref, vc_ref, vp_ref, kh):
    lanes = slice(kh * HEAD_DIM, (kh + 1) * HEAD_DIM)
    return (jnp.concatenate([kp_ref[:, lanes], kc_ref[:, lanes]], axis=0),
            jnp.concatenate([vp_ref[:, lanes], vc_ref[:, lanes]], axis=0))


def _row_sums(x):
    return jnp.dot(x, jnp.ones((x.shape[1], LANES), BF16), preferred_element_type=F32)


def _lanes(v, n):
    return v[:, :n] if n <= LANES else jnp.concatenate([v] * (n // LANES), axis=-1)


def _head_probs(q, kk, valid, sink, sums_on_mxu):
    s = jnp.where(valid, lax.dot_general(q, kk, _NT, preferred_element_type=F32), NEG_INF)
    m = jnp.maximum(jnp.max(s, axis=-1, keepdims=True), sink)
    p = jnp.exp(s - m)
    e_sink = jnp.exp(sink - m)
    total = _row_sums(p.astype(BF16)) if sums_on_mxu else jnp.sum(p, axis=-1, keepdims=True)
    return p.astype(BF16), 1.0 / (total + e_sink), e_sink


def _attn_specs(q_width, kv_width, order):
    qb = q_width // kv_width
    prev = lambda i: jnp.maximum(order(i) - 1, 0)
    return [pl.BlockSpec((CHUNK, q_width), lambda i: (order(i), 0)),
            pl.BlockSpec((CHUNK, kv_width), lambda i: (order(i), qb)),
            pl.BlockSpec((CHUNK, kv_width), lambda i: (prev(i), qb)),
            pl.BlockSpec((CHUNK, kv_width), lambda i: (order(i), qb + 1)),
            pl.BlockSpec((CHUNK, kv_width), lambda i: (prev(i), qb + 1))]


def attn_fwd(qk, qkv, sinks, q_width, kv_width, name):
    T = qk.shape[0]
    group = q_width // kv_width

    def body(q_ref, kc_ref, kp_ref, vc_ref, vp_ref, sink_ref, o_ref):
        valid = _band_mask(pl.program_id(0))
        for kh in range(kv_width // HEAD_DIM):
            kk, vv = _kv_head(kc_ref, kp_ref, vc_ref, vp_ref, kh)
            for g in range(group):
                h = kh * group + g
                lanes = slice(h * HEAD_DIM, (h + 1) * HEAD_DIM)
                p, inv, _ = _head_probs(q_ref[:, lanes], kk, valid, sink_ref[h], False)
                o_ref[:, lanes] = (jnp.dot(p, vv, preferred_element_type=F32) * inv).astype(BF16)

    specs = _attn_specs(q_width, kv_width, lambda i: i)
    return pl.pallas_call(
        body, name=name, grid=(T // CHUNK,),
        in_specs=specs + [pl.BlockSpec(memory_space=pltpu.SMEM)],
        out_specs=_rows(CHUNK, q_width), out_shape=jax.ShapeDtypeStruct((T, q_width), BF16),
        compiler_params=_params("parallel"))(qk, qk, qk, qkv, qkv, sinks)


def attn_bwd(qk, qkv, do, sinks, q_width, kv_width, name):
    T = qk.shape[0]
    NB = T // CHUNK
    group = q_width // kv_width

    def body(q_ref, kc_ref, kp_ref, vc_ref, vp_ref, do_ref, sink_ref, dq_ref, dk_ref, dv_ref, ds_ref, ck_ref, cv_ref):
        i = pl.program_id(0)
        n = NB - 1 - i

        @pl.when(i == 0)
        def _():
            ck_ref[...] = jnp.zeros_like(ck_ref)
            cv_ref[...] = jnp.zeros_like(cv_ref)
            ds_ref[...] = jnp.zeros_like(ds_ref)
        lane = lax.broadcasted_iota(jnp.int32, (1, LANES), 1)
        dsink_row = jnp.zeros((1, LANES), F32)
        valid = _band_mask(n, group)
        head = lax.broadcasted_iota(jnp.int32, (group * CHUNK, 1), 0) // CHUNK
        for kh in range(kv_width // HEAD_DIM):
            kk, vv = _kv_head(kc_ref, kp_ref, vc_ref, vp_ref, kh)
            heads = [slice((kh * group + g) * HEAD_DIM, (kh * group + g + 1) * HEAD_DIM) for g in range(group)]
            q = jnp.concatenate([q_ref[:, hs] for hs in heads], axis=0)
            do = jnp.concatenate([do_ref[:, hs] for hs in heads], axis=0)
            sink = jnp.zeros((group * CHUNK, 1), F32)
            for g in range(group):
                sink = jnp.where(head == g, sink_ref[kh * group + g], sink)
            p, inv, e_sink = _head_probs(q, kk, valid, sink, True)
            p = p.astype(F32) * _lanes(inv, 2 * CHUNK)
            dp = lax.dot_general(do, vv, _NT, preferred_element_type=F32)
            delta = _row_sums((p * dp).astype(BF16))
            ds = (p * (dp - _lanes(delta, 2 * CHUNK))).astype(BF16)
            dsink = -e_sink * inv[:, :1] * delta[:, :1]
            dq = jnp.dot(ds, kk, preferred_element_type=F32)
            for g, hs in enumerate(heads):
                rows = slice(g * CHUNK, (g + 1) * CHUNK)
                dsink_row = dsink_row + jnp.where(lane == kh * group + g, jnp.sum(dsink[rows]), 0.0)
                dq_ref[:, hs] = dq[rows]
            dkk = lax.dot_general(ds, q, _TN, preferred_element_type=F32)
            dvv = lax.dot_general(p.astype(BF16), do, _TN, preferred_element_type=F32)
            lanes = slice(kh * HEAD_DIM, (kh + 1) * HEAD_DIM)
            dk_ref[:, lanes] = dkk[CHUNK:] + ck_ref[:, lanes]
            dv_ref[:, lanes] = dvv[CHUNK:] + cv_ref[:, lanes]
            ck_ref[:, lanes] = dkk[:CHUNK]
            cv_ref[:, lanes] = dvv[:CHUNK]
        ds_ref[0:1, :] += dsink_row

    order = lambda i: NB - 1 - i
    specs = _attn_specs(q_width, kv_width, order)
    kv_out = pl.BlockSpec((CHUNK, kv_width), lambda i: (order(i), 0))
    q_rows = pl.BlockSpec((CHUNK, q_width), lambda i: (order(i), 0))
    return pl.pallas_call(
        body, name=name, grid=(NB,),
        in_specs=specs + [q_rows, pl.BlockSpec(memory_space=pltpu.SMEM)],
        out_specs=[q_rows, kv_out, kv_out, _full((8, LANES))],
        out_shape=[jax.ShapeDtypeStruct((T, q_width), F32), jax.ShapeDtypeStruct((T, kv_width), F32),
                   jax.ShapeDtypeStruct((T, kv_width), F32), jax.ShapeDtypeStruct((8, LANES), F32)],
        scratch_shapes=[pltpu.VMEM((CHUNK, kv_width), F32), pltpu.VMEM((CHUNK, kv_width), F32)],
        compiler_params=_params("arbitrary"))(qk, qk, qk, qkv, qkv, do, sinks)


def _blocked(w):
    return w.reshape(N_DEV, w.shape[0] // N_DEV, w.shape[1])


def kernel(x, positions, pre_mix_g, post_mix_g, pre_ffn_g, post_ffn_g, a_w_in, a_b_in, a_ln_g, a_ln_b, a_w_s, a_b_s, a_w_out, b_w_qkv, b_b_qkv, b_sinks, b_w_o, ffn_w_gu, ffn_w_down, loss_target, m_pre_mix_g, m_post_mix_g, m_pre_ffn_g, m_post_ffn_g, m_a_w_in, m_a_b_in, m_a_ln_g, m_a_ln_b, m_a_w_s, m_a_b_s, m_a_w_out, m_b_w_qkv, m_b_b_qkv, m_b_sinks, m_b_w_o, m_ffn_w_gu, m_ffn_w_down, v_pre_mix_g, v_post_mix_g, v_pre_ffn_g, v_post_ffn_g, v_a_w_in, v_a_b_in, v_a_ln_g, v_a_ln_b, v_a_w_s, v_a_b_s, v_a_w_out, v_b_w_qkv, v_b_b_qkv, v_b_sinks, v_b_w_o, v_ffn_w_gu, v_ffn_w_down):
    weights = dict(pre_mix_g=pre_mix_g, post_mix_g=post_mix_g, pre_ffn_g=pre_ffn_g, post_ffn_g=post_ffn_g,
                   a_w_in=a_w_in, a_b_in=a_b_in, a_ln_g=a_ln_g, a_ln_b=a_ln_b, a_w_s=a_w_s, a_b_s=a_b_s,
                   a_w_out=a_w_out, b_w_qkv=b_w_qkv, b_b_qkv=b_b_qkv, b_sinks=b_sinks, b_w_o=b_w_o,
                   ffn_w_gu=ffn_w_gu, ffn_w_down=ffn_w_down)
    mom_m = dict(pre_mix_g=m_pre_mix_g, post_mix_g=m_post_mix_g, pre_ffn_g=m_pre_ffn_g, post_ffn_g=m_post_ffn_g,
                 a_w_in=m_a_w_in, a_b_in=m_a_b_in, a_ln_g=m_a_ln_g, a_ln_b=m_a_ln_b, a_w_s=m_a_w_s, a_b_s=m_a_b_s,
                 a_w_out=m_a_w_out, b_w_qkv=m_b_w_qkv, b_b_qkv=m_b_b_qkv, b_sinks=m_b_sinks, b_w_o=m_b_w_o,
                 ffn_w_gu=m_ffn_w_gu, ffn_w_down=m_ffn_w_down)
    mom_v = dict(pre_mix_g=v_pre_mix_g, post_mix_g=v_post_mix_g, pre_ffn_g=v_pre_ffn_g, post_ffn_g=v_post_ffn_g,
                 a_w_in=v_a_w_in, a_b_in=v_a_b_in, a_ln_g=v_a_ln_g, a_ln_b=v_a_ln_b, a_w_s=v_a_w_s, a_b_s=v_a_b_s,
                 a_w_out=v_a_w_out, b_w_qkv=v_b_w_qkv, b_b_qkv=v_b_b_qkv, b_sinks=v_b_sinks, b_w_o=v_b_w_o,
                 ffn_w_gu=v_ffn_w_gu, ffn_w_down=v_ffn_w_down)
    names = list(weights)
    big = ("a_w_in", "b_w_qkv", "ffn_w_gu", "a_w_out", "b_w_o", "ffn_w_down")

    T, D = x.shape[1], x.shape[2]
    depth = pre_mix_g.shape[0]
    n_heads = b_sinks.shape[1]
    q_width = n_heads * HEAD_DIM
    kv_width = N_KV_HEADS * HEAD_DIM
    G = a_w_s.shape[1]
    W = a_ln_g.shape[1]
    device = _index(_position())

    dev = device.reshape(1).astype(jnp.int32)

    def layer_keys(i):
        mixer = ("a_w_in", "a_w_out") if i % 2 == 0 else ("b_w_qkv", "b_w_o")
        return [(k, i // 2) for k in mixer] + [("ffn_w_gu", i), ("ffn_w_down", i)]

    as_view = ("b_w_qkv", "ffn_w_gu")
    in_kernel = ("a_w_in",)
    view = lambda k, t: jnp.swapaxes(t, 1, 2) if k in as_view else t
    w_view = {k: view(k, weights[k]) for k in big}
    m_view = {k: view(k, mom_m[k]) for k in big}
    v_view = {k: view(k, mom_v[k]) for k in big}

    groups_keys = [[key] for key in layer_keys(0)] + [layer_keys(i) for i in range(1, depth)]
    group_of = {}
    for g, keys in enumerate(groups_keys):
        layer = 0 if g < len(layer_keys(0)) else g - len(layer_keys(0)) + 1
        for pos, (k, _) in enumerate(keys):
            group_of[(layer, k)] = (g, pos)
    bias_land = lax.dynamic_update_slice(
        jnp.zeros((N_DEV,) + b_b_qkv.shape, F32), b_b_qkv[None], (device, 0, 0))
    h, hn = x[0], first_norm(x[0], pre_mix_g[0][None], "norm_first")
    first = len(layer_keys(0))
    in_gather, token = [], ()
    for name, lo, hi in (("gather_start_first", 0, first), ("gather_start_rest", first, len(groups_keys))):
        groups = [list(prep_weights([(w_view[k], l, k in in_kernel) for k, l in keys], dev, f"prep_{name}_{g}",
                                    after=token)) for g, keys in enumerate(groups_keys[lo:hi])]
        if lo:
            groups[0].append(bias_land)
        gather_sems, flat_lands, token = relay_start(groups, name)
        token = (token, hn)
        for grp, sems in zip(groups, gather_sems):
            in_gather.append(([flat_lands.pop(0) for _ in grp], sems))
    all_started = token[0]
    passed, gathered = {}, {}

    def pass_on(g, after):
        if g not in passed and g < len(in_gather):
            lands, sems = in_gather[g]
            passed[g], lands = relay_pass(lands, sems, after, f"gather_pass_{g}")
            in_gather[g] = (lands, sems)

    pass_on(0, (all_started,))

    def weight(i, k, *after, ahead=True):
        g, pos = group_of[(i, k)]
        if g not in gathered:
            if ahead:
                pass_on(g + 1, after)
            lands, sems = in_gather[g]
            gathered[g] = relay_wait(lands, sems, passed[g], after, f"gather_wait_{g}")
        w = gathered[g][pos]
        return w.reshape(N_DEV * w.shape[1], w.shape[2])

    causal = jnp.tril(jnp.ones((CHUNK, CHUNK), dtype=bool))
    w_mask = jnp.where(causal[None, None], a_w_s, 0.0).astype(BF16)
    w_mask_t = jnp.swapaxes(w_mask, 2, 3)
    b_s_t = jnp.swapaxes(a_b_s, 1, 2)
    group_onehot = (jnp.arange(W)[:, None] // (W // G) == jnp.arange(LANES)[None, :]).astype(BF16)
    lane = jnp.arange(LANES) % HEAD_DIM
    inv_freq = ROPE_THETA ** (-jnp.arange(0, ROPE_DIM, 2, dtype=F32) / ROPE_DIM)
    inv_freq_row = jnp.where(lane < ROPE_DIM, inv_freq[lane % (ROPE_DIM // 2)], 0.0)[None, :].astype(F32)
    tabs = rope_tables(positions.reshape(T, 1), inv_freq_row, "rope_tables")

    saved = []
    for i in range(depth):
        j = i // 2
        s = dict(h=h, hn=hn)
        post_pre = (post_mix_g[i][None], pre_ffn_g[i][None])
        if i % 2 == 0:
            s["a"] = rowmm(hn, weight(i, "a_w_in", hn, all_started), True, a_b_in[j][None], BF16, "gmlp_in")
            s["gated"], s["mix"], s["h2"], s["fn"] = gmlp_gate_out(
                s["a"], a_ln_g[j][None], a_ln_b[j][None], w_mask[j], b_s_t[j], weight(i, "a_w_out", s["a"], ahead=i > 0),
                h, *post_pre, "gmlp_gate_out")
            if i == 0:
                pass_on(group_of[(0, "ffn_w_gu")][0], (s["mix"],))
        else:
            w_qkv = weight(i, "b_w_qkv", hn)
            b_qkv_full = jnp.swapaxes(gathered[group_of[(1, "b_w_qkv")][0]][-1], 0, 1).reshape(b_b_qkv.shape[0], 1, -1)
            s["qkv"], s["qk"] = qkv_rope(hn, w_qkv, b_qkv_full[j], tabs, q_width, kv_width, "attn_qkv_rope")
            s["o"] = attn_fwd(s["qk"], s["qkv"], b_sinks[j], q_width, kv_width, "attn_core")
            s["mix"], s["h2"], s["fn"] = mm_add_norm(s["o"], weight(i, "b_w_o", s["o"]), h, *post_pre, "attn_out")
        s["act"], s["act_partials"] = ffn_up(s["fn"], weight(i, "ffn_w_gu", s["fn"]), "ffn_up")
        w_down = weight(i, "ffn_w_down", s["act"])
        saved.append(s)
        if i + 1 < depth:
            s["f"], h, hn = mm_add_norm(s["act"], w_down, s["h2"], post_ffn_g[i][None], pre_mix_g[i + 1][None], "ffn_down")
    small = {k: [None] * weights[k].shape[0] for k in names if k not in big}
    dh, df, loss_sum, small["post_ffn_g"][depth - 1] = mm_add_norm_loss(
        saved[-1]["act"], w_down, saved[-1]["h2"], post_ffn_g[depth - 1][None], loss_target[0], "ffn_down_loss")

    grads = {k: [None] * weights[k].shape[0] for k in big}
    in_flight = []

    def send_grads(keys, tag):
        sems, bufs, token = exchange_start([_blocked(grads[k][l]) for k, l in keys], "exchange_start_" + tag)
        in_flight.append((keys, sems, bufs, tag))
        return token

    replicated = [k for k in names if k not in big and k not in ("b_b_qkv", "a_w_s")]
    main = sum(weights[k].size for k in replicated)
    main_rows = -(-main // (LANES * LANES)) * LANES
    bias_size = b_b_qkv.shape[0] * N_DEV * b_b_qkv.shape[1]
    bias_rows = -(-bias_size // (8 * LANES)) * 8
    flat = lambda vals: [v.reshape(-1) for v in vals]

    def start_small_gather():
        packed = jnp.concatenate(
            flat(p for k in replicated for p in small[k])
            + [loss_sum[0, :1], jnp.zeros((main_rows * LANES - main - 1,), F32)] + flat(small["b_b_qkv"]) + [jnp.zeros((bias_rows * LANES - bias_size,), F32)]
        ).reshape(1, main_rows + bias_rows, LANES)
        spatial = jnp.stack(small["a_w_s"]).reshape(1, -1, LANES)
        return gather_start([prep_weights([(packed, 0, False), (spatial, 0, False)], dev, "place_small_grads", F32)],
                            "gather_start_small")

    sent = ()
    for i in reversed(range(depth)):
        j = i // 2
        s = saved[i]
        dgu = ffn_dact(df, weight(i, "ffn_w_down"), s["act_partials"], "ffn_dact", after=sent)
        alone = i == 0
        grads["ffn_w_down"][i] = grad_mm(s["act"], df, "ffn_down_grad")
        early = (send_grads(layer_keys(i)[3:], f"ffn_down_{i}"),) if alone else ()
        grads["ffn_w_gu"][i] = grad_mm(dgu, s["fn"], "ffn_up_grad", after=early)
        sent = send_grads(layer_keys(i)[2:3] if alone else layer_keys(i)[2:], f"ffn_{i}")
        r = mm_norm_bwd(dgu, weight(i, "ffn_w_gu"), dh, s["h2"], pre_ffn_g[i][None],
                        (s["mix"], post_mix_g[i][None]), "ffn_dx", sent,
                        project=weight(i, "b_w_o") if i % 2 else None)
        dh, dmix = r["dh"], r["dy"]
        small["pre_ffn_g"][i], small["post_mix_g"][i] = r["dg_pre"], r["dg_post"]
        post = (saved[i - 1]["f"], post_ffn_g[i - 1][None]) if i > 0 else None
        if i % 2 == 0:
            grads["a_w_out"][j] = grad_mm(s["gated"], dmix, "gmlp_out_grad")
            if alone:
                sent = send_grads(layer_keys(i)[1:2], f"mixer_out_{i}")
            r = gmlp_gate_bwd_dx(
                s["a"], dmix, weight(i, "a_w_out"), a_ln_g[j][None], a_ln_b[j][None], w_mask[j], w_mask_t[j], b_s_t[j],
                group_onehot, weight(i, "a_w_in"), dh, s["h"], pre_mix_g[i][None], post, "gmlp_gate_bwd_dx", sent)
            small["a_w_s"][j], small["a_b_s"][j] = r["dw_s"], r["db_s_t"][:, :G].T
            small["a_ln_g"][j], small["a_ln_b"][j], small["a_b_in"][j] = r["dln_g"][0], r["dln_b"][0], r["db_in"][0]
            small["pre_mix_g"][i] = r["dg_pre"]
            first_layer = ()
            if i == 0:
                small_gather = start_small_gather()
                first_layer = (small_gather[2],)
            grads["a_w_in"][j] = grad_mm(r["da"], s["hn"], "gmlp_in_grad", after=first_layer)
            sent = (send_grads(layer_keys(i)[:1] if alone else layer_keys(i)[:2], f"mixer_{i}"),)
        else:
            grads["b_w_o"][j] = grad_mm(s["o"], dmix, "attn_out_grad")
            dq, dk, dv, dsink = attn_bwd(s["qk"], s["qkv"], r["dy_projected"], b_sinks[j], q_width, kv_width, "attn_core_bwd")
            r = rope_bwd_dx(dq, dk, dv, tabs, weight(i, "b_w_qkv"), dh, s["h"], pre_mix_g[i][None], post,
                            "attn_rope_bwd_dx", sent)
            small["b_sinks"][j], small["b_b_qkv"][j] = dsink[0, :n_heads], r["dbias"][0]
            grads["b_w_qkv"][j] = grad_mm(r["dqkv"], s["hn"], "attn_qkv_grad")
            sent = (send_grads(layer_keys(i)[:2], f"mixer_{i}"),)
        dh, small["pre_mix_g"][i] = r["dh"], r["dg_pre"]
        if i > 0:
            df, small["post_ffn_g"][i - 1] = r["dy"], r["dg_post"]
    grad_x = dh[None]

    small_sems, small_lands, small_sent = small_gather
    pack_main = lambda src: jnp.concatenate(
        flat(src[k] for k in replicated) + [jnp.ones((main_rows * LANES - main,), F32)]).reshape(main_rows, LANES)
    packed_w, packed_m, packed_v = pack_main(weights), pack_main(mom_m), pack_main(mom_v)

    stacked = {k: [lax.empty(w_view[k].shape, F32) for _ in range(4)] for k in big}

    def update_group(entry, after):
        keys, sems, bufs, tag = entry
        partials, landed = exchange_wait(bufs, sems, after, "exchange_wait_" + tag)
        for (k, l), part, land in zip(keys, partials, landed):
            stacked[k] = sum_adamw(part, land, w_view[k], m_view[k], v_view[k], stacked[k], l,
                                   k in in_kernel, dev, "adamw_" + k)
        return tuple(stacked[k][0] for k, _ in keys)

    after = (small_sent, dh) + sent
    for entry in in_flight[:-1]:
        after = update_group(entry, after)
    out_g, out_d, out_m, out_v = {}, {}, {}, {}

    gathered_small, gathered_spatial = gather_wait(small_lands, small_sems[0], after + (packed_w, packed_m, packed_v),
                                                   "gather_wait_small")
    rows_of = lambda t: t.reshape(-1, LANES)
    spatial = sum_adamw_small(gathered_spatial, rows_of(a_w_s), rows_of(m_a_w_s), rows_of(v_a_w_s), "adamw_a_w_s")
    out_g["a_w_s"], out_d["a_w_s"], out_m["a_w_s"], out_v["a_w_s"] = [t.reshape(a_w_s.shape) for t in spatial]
    results = sum_adamw_small(gathered_small, packed_w, packed_m, packed_v, "adamw_small")
    loss = results[0].reshape(-1)[main] * (0.5 / D)
    off = 0
    for k in replicated:
        size, shape = weights[k].size, weights[k].shape
        out_g[k], out_d[k], out_m[k], out_v[k] = [t.reshape(-1)[off:off + size].reshape(shape) for t in results]
        off += size
    n_local = b_b_qkv.shape[1]
    bias_sum = sum_parts(gathered_small[:, main_rows:], "sum_bias_grads").reshape(-1)[:bias_size]
    out_g["b_b_qkv"] = lax.dynamic_slice_in_dim(
        bias_sum.reshape(b_b_qkv.shape[0], -1), device * n_local, n_local, axis=1)
    pad = lambda t: jnp.concatenate([t.reshape(-1), jnp.ones((8 * LANES - t.size,), F32)]).reshape(8, LANES)
    bias_update = adamw(pad(b_b_qkv), pad(out_g["b_b_qkv"]), pad(m_b_b_qkv), pad(v_b_b_qkv), "adamw_bias")
    out_d["b_b_qkv"], out_m["b_b_qkv"], out_v["b_b_qkv"] = [
        t.reshape(-1)[:b_b_qkv.size].reshape(b_b_qkv.shape) for t in bias_update]

    update_group(in_flight[-1], (spatial[0], results[0], bias_update[0]))
    for k in big:
        out_g[k], out_d[k], out_m[k], out_v[k] = [view(k, t) for t in stacked[k]]

    return (loss, grad_x, *[out_g[k] for k in names], *[out_d[k] for k in names],
            *[out_m[k] for k in names], *[out_v[k] for k in names])
```
